```python
import jax
import jax.numpy as jnp
from jax import lax
import numpy as np


D_MODEL = 1024
BATCH = 4
SEQ = 4096
DEPTH = 2

CTX_LEN = 256
GRID_W = 64
EPS = 1e-6
MASK_VALUE = -1e30
LB_FLOOR = 1e-30
N_MOD = 6

ATT_W = D_MODEL // 2
HG_W = D_MODEL // 4
CV_W = D_MODEL - ATT_W - HG_W
MIX_W = ATT_W + HG_W + CV_W

HEAD_DIM = 64
N_Q = ATT_W // HEAD_DIM
N_KV = 2
GQA_GROUP = N_Q // N_KV
KV_W = N_KV * HEAD_DIM
WINDOW = 128
ATT_BLOCK = 128
ROPE_BASE = 10000.0
ROPE_FREQS = HEAD_DIM // 4

HG_DIM = 64
HG_HEADS = HG_W // HG_DIM
HG_CHUNK = 16

CONV_WIDTH = 3

N_EXPERTS = 16
EC_CAPACITY = 2
D_EXPERT = 2048

OFF_K = 0
OFF_V = OFF_K + KV_W
OFF_I = OFF_V + KV_W
OFF_FF = OFF_I + HG_W
OFF_FB = OFF_FF + HG_W
STATE_COLS = OFF_FB + HG_W
OFF_Q = STATE_COLS
OFF_HQ = OFF_Q + ATT_W
OFF_G = OFF_HQ + HG_W
OFF_CB = OFF_G + HG_W
OFF_CC = OFF_CB + CV_W
OFF_CH = OFF_CC + CV_W
IN_COLS = OFF_CH + CV_W

kernel_name = 'hybrid_hymba_dit_block'

F32 = jnp.float32


def rmsnorm(x, g):
    x32 = x.astype(F32)
    y = x32 * lax.rsqrt(jnp.mean(x32 * x32, axis=-1, keepdims=True) + EPS)
    return (y * g.astype(F32)).astype(x.dtype)


def modulate(h, shift, scale):
    return h * (1 + scale) + shift


def axial_rope_tables(n_rows, dtype):
    row = jnp.repeat(jnp.arange(n_rows), GRID_W)
    col = jnp.tile(jnp.arange(GRID_W), n_rows)
    pos = jnp.stack([row, col], axis=-1).astype(F32)
    inv = ROPE_BASE ** (-jnp.arange(ROPE_FREQS, dtype=F32) / ROPE_FREQS)
    ang = pos[:, :, None] * inv
    return jnp.cos(ang).astype(dtype), jnp.sin(ang).astype(dtype)


def apply_rope(x, cos, sin):
    B, T, H, _ = x.shape
    xr = x.reshape(B, T, H, 2, 2, ROPE_FREQS)
    x1, x2 = xr[..., 0, :], xr[..., 1, :]
    c, s = cos[None, :, None], sin[None, :, None]
    out = jnp.stack([x1 * c - x2 * s, x2 * c + x1 * s], axis=-2)
    return out.reshape(x.shape)


def window_attention(q, k, v, kc, vc, sink):
    B, T = q.shape[:2]
    L = kc.shape[1]
    nb = T // ATT_BLOCK
    scale = HEAD_DIM ** -0.5
    qb = q.reshape(B, nb, ATT_BLOCK, N_KV, GQA_GROUP, HEAD_DIM)

    def band_blocks(a):
        ap = jnp.pad(a, ((0, 0), (ATT_BLOCK, ATT_BLOCK), (0, 0), (0, 0)))
        ap = ap.reshape(B, nb + 2, ATT_BLOCK, N_KV, HEAD_DIM)
        return jnp.concatenate([ap[:, :-2], ap[:, 1:-1], ap[:, 2:]], axis=2)

    kw, vw = band_blocks(k), band_blocks(v)
    n_lat = 3 * ATT_BLOCK
    qi = jnp.arange(ATT_BLOCK)[:, None]
    kj = jnp.arange(n_lat)[None, :] - ATT_BLOCK
    band = jnp.abs(kj - qi) <= WINDOW
    abs_pos = jnp.arange(nb)[:, None] * ATT_BLOCK + kj
    valid = (abs_pos >= 0) & (abs_pos < T)
    mask = band[None] & valid[:, None, :]

    s_lat = jnp.einsum('bnqkgd,bnskd->bnkgqs', qb, kw).astype(F32) * scale
    s_lat = jnp.where(mask[None, :, None, None], s_lat, MASK_VALUE)
    s_ctx = jnp.einsum('bnqkgd,bmkd->bnkgqm', qb, kc).astype(F32) * scale
    s_sink = jnp.broadcast_to(sink.astype(F32).reshape(N_KV, GQA_GROUP, 1, 1), s_lat.shape[:-1] + (1,))
    p = jax.nn.softmax(jnp.concatenate([s_lat, s_ctx, s_sink], axis=-1), axis=-1).astype(v.dtype)
    out = (jnp.einsum('bnkgqs,bnskd->bnqkgd', p[..., :n_lat], vw)
           + jnp.einsum('bnkgqm,bmkd->bnqkgd', p[..., n_lat:n_lat + L], vc))
    return out.reshape(B, T, N_Q * HEAD_DIM)


def context_attention(qc, kc, vc, sink):
    B, L = qc.shape[:2]
    qg = qc.reshape(B, L, N_KV, GQA_GROUP, HEAD_DIM)
    s = jnp.einsum('blkgd,bmkd->bkglm', qg, kc).astype(F32) * HEAD_DIM ** -0.5
    s_sink = jnp.broadcast_to(sink.astype(F32).reshape(N_KV, GQA_GROUP, 1, 1), s.shape[:-1] + (1,))
    p = jax.nn.softmax(jnp.concatenate([s, s_sink], axis=-1), axis=-1)[..., :L].astype(vc.dtype)
    out = jnp.einsum('bkglm,bmkd->blkgd', p, vc)
    return out.reshape(B, L, N_Q * HEAD_DIM)


def hgrn_gates(z, lb):
    z32 = z.astype(F32)
    log_lb = jnp.log(jnp.maximum(lb, LB_FLOOR))
    log_f = jnp.logaddexp(log_lb, jnp.log1p(-lb) + jax.nn.log_sigmoid(z32))
    k = (1.0 - lb) * jax.nn.sigmoid(-z32)
    return log_f, k


def _chunk(a):
    return a.reshape(a.shape[0], a.shape[1] // HG_CHUNK, HG_CHUNK, *a.shape[2:])


def _chunk_summaries(k, v, log_f):
    lam = jnp.cumsum(log_f, axis=2)
    lam_last = lam[:, :, -1]
    k_dec = k * jnp.exp(lam_last[:, :, None] - lam)
    u = jnp.einsum('bnshd,bnshv->bnhdv', k_dec, v)
    return lam, jnp.exp(lam_last), u


def _scan_states(decay, u, s0):
    def step(s, xs):
        a, uu = xs
        return a[..., None] * s + uu, s
    s_fin, s_before = lax.scan(step, s0, (jnp.moveaxis(decay, 1, 0), jnp.moveaxis(u, 1, 0)))
    return s_before, s_fin


def gla_final_state(k, v, log_f, s0):
    k, v, log_f = (_chunk(a.astype(F32)) for a in (k, v, log_f))
    _, decay, u = _chunk_summaries(k, v, log_f)
    return _scan_states(decay, u, s0)[1]


def gla_chunkwise(q, k, v, log_f, s0):
    B, T, H, dv = v.shape
    dtype = v.dtype
    q, k, v, log_f = (_chunk(a.astype(F32)) for a in (q, k, v, log_f))
    lam, decay, u = _chunk_summaries(k, v, log_f)
    s_before, s_fin = _scan_states(decay, u, s0)
    causal = jnp.tril(jnp.ones((HG_CHUNK, HG_CHUNK), dtype=bool))[:, :, None, None]
    diff = lam[:, :, :, None] - lam[:, :, None, :]
    dec = jnp.where(causal, jnp.exp(jnp.where(causal, diff, 0.0)), 0.0)
    a = jnp.einsum('bnthd,bnshd,bntshd->bnhts', q, k, dec)
    o = (jnp.einsum('bnhts,bnshv->bnthv', a, v)
         + jnp.einsum('bnthd,nbhdv->bnthv', q * jnp.exp(lam), s_before))
    return o.reshape(B, T, H, dv).astype(dtype), s_fin


def hgrn_readout(o, g, gain):
    B, T = o.shape[:2]
    o32 = o.astype(F32)
    on = o32 * lax.rsqrt(jnp.mean(o32 * o32, axis=-1, keepdims=True) + EPS)
    on = on * gain.astype(F32).reshape(HG_HEADS, HG_DIM)
    return (on.reshape(B, T, HG_W) * jax.nn.silu(g.astype(F32))).astype(g.dtype)


def dwconv3(u, w):
    up = jnp.pad(u, ((0, 0), (1, 1), (0, 0)))
    return up[:, :-2] * w[0] + up[:, 1:-1] * w[1] + up[:, 2:] * w[2]


def token_mixers(hl, hc, w_in, w_o, sink, lb_f, lb_b, hg_gain, conv_w, cos, sin, need_ctx_out):
    pl = hl @ w_in
    pc = hc @ (w_in if need_ctx_out else w_in[:, :STATE_COLS])

    def col(p, off, width):
        return p[..., off:off + width]

    def heads(a, n_heads):
        return a.reshape(a.shape[0], a.shape[1], n_heads, -1)

    def flip(a):
        return jnp.flip(a, axis=1)

    ql = apply_rope(heads(col(pl, OFF_Q, ATT_W), N_Q), cos, sin)
    kl = apply_rope(heads(col(pl, OFF_K, KV_W), N_KV), cos, sin)
    vl = heads(col(pl, OFF_V, KV_W), N_KV)
    kc = heads(col(pc, OFF_K, KV_W), N_KV)
    vc = heads(col(pc, OFF_V, KV_W), N_KV)
    att_l = window_attention(ql, kl, vl, kc, vc, sink)

    def hg_inputs(p):
        i = heads(col(p, OFF_I, HG_W), HG_HEADS)
        lf_f, k_f = hgrn_gates(heads(col(p, OFF_FF, HG_W), HG_HEADS), lb_f)
        lf_b, k_b = hgrn_gates(heads(col(p, OFF_FB, HG_W), HG_HEADS), lb_b)
        return i, lf_f, k_f, lf_b, k_b

    i_c, lfc_f, kc_f, lfc_b, kc_b = hg_inputs(pc)
    i_l, lfl_f, kl_f, lfl_b, kl_b = hg_inputs(pl)
    s0 = jnp.zeros((hl.shape[0], HG_HEADS, HG_DIM, HG_DIM), F32)
    if need_ctx_out:
        qc_h = jax.nn.silu(heads(col(pc, OFF_HQ, HG_W), HG_HEADS))
        oc_f, sc_f = gla_chunkwise(qc_h, kc_f, i_c, lfc_f, s0)
        oc_b, sc_b = gla_chunkwise(flip(qc_h), flip(kc_b), flip(i_c), flip(lfc_b), s0)
    else:
        sc_f = gla_final_state(kc_f, i_c, lfc_f, s0)
        sc_b = gla_final_state(flip(kc_b), flip(i_c), flip(lfc_b), s0)
    ql_h = jax.nn.silu(heads(col(pl, OFF_HQ, HG_W), HG_HEADS))
    ol_f, _ = gla_chunkwise(ql_h, kl_f, i_l, lfl_f, sc_f)
    ol_b, _ = gla_chunkwise(flip(ql_h), flip(kl_b), flip(i_l), flip(lfl_b), sc_b)
    hg_l = hgrn_readout(ol_f + flip(ol_b), col(pl, OFF_G, HG_W), hg_gain)

    conv_l = col(pl, OFF_CB, CV_W) * dwconv3(col(pl, OFF_CC, CV_W) * col(pl, OFF_CH, CV_W), conv_w)

    mix_l = jnp.concatenate([att_l, hg_l, conv_l], axis=-1) @ w_o
    if not need_ctx_out:
        return mix_l, None

    att_c = context_attention(heads(col(pc, OFF_Q, ATT_W), N_Q), kc, vc, sink)
    hg_c = hgrn_readout(oc_f + flip(oc_b), col(pc, OFF_G, HG_W), hg_gain)
    conv_c = col(pc, OFF_CB, CV_W) * dwconv3(col(pc, OFF_CC, CV_W) * col(pc, OFF_CH, CV_W), conv_w)
    mix_c = jnp.concatenate([att_c, hg_c, conv_c], axis=-1) @ w_o
    return mix_l, mix_c


def expert_choice_moe(h, w_router, w_gate, w_up, w_down):
    B, N, D = h.shape
    cap = EC_CAPACITY * N // N_EXPERTS
    aff = jax.nn.softmax((h @ w_router).astype(F32), axis=-1)
    g, idx = lax.top_k(jnp.swapaxes(aff, 1, 2), cap)
    xs = jax.vmap(lambda hb, ib: hb[ib])(h, idx)
    hid = jax.nn.silu(jnp.einsum('becd,edf->becf', xs, w_gate)) * jnp.einsum('becd,edf->becf', xs, w_up)
    y = jnp.einsum('becf,efd->becd', hid, w_down) * g[..., None].astype(h.dtype)
    return jax.vmap(lambda ib, yb: jnp.zeros((N, D), yb.dtype).at[ib.reshape(-1)].add(yb.reshape(-1, D)))(idx, y)


def setup_inputs(seed: int = 0) -> dict:
    key = jax.random.key(seed)
    ks = jax.random.split(key, 20)

    def nrm(k, shape, s):
        return jax.random.normal(k, shape, F32) * s

    return {
        'x': nrm(ks[0], (BATCH, SEQ, D_MODEL), 1.0),
        'c': nrm(ks[1], (BATCH, D_MODEL), 1.0),
        'ctx': nrm(ks[2], (BATCH, CTX_LEN, D_MODEL), 1.0),
        'c_ctx': nrm(ks[3], (D_MODEL,), 1.0),
        'ada_w': nrm(ks[4], (DEPTH, D_MODEL, N_MOD * D_MODEL), D_MODEL ** -0.5),
        'ada_b': nrm(ks[5], (DEPTH, N_MOD * D_MODEL), 0.02),
        'norm1_g': 1.0 + nrm(ks[6], (DEPTH, D_MODEL), 0.1),
        'norm2_g': 1.0 + nrm(ks[7], (DEPTH, D_MODEL), 0.1),
        'w_in': nrm(ks[8], (DEPTH, D_MODEL, IN_COLS), D_MODEL ** -0.5),
        'attn_sink': nrm(ks[9], (DEPTH, N_Q), 1.0),
        'hgrn_lb': nrm(ks[10], (DEPTH, 2, HG_W), 1.0),
        'hgrn_norm_g': 1.0 + nrm(ks[11], (DEPTH, HG_W), 0.1),
        'conv_w': nrm(ks[12], (DEPTH, CONV_WIDTH, CV_W), CONV_WIDTH ** -0.5),
        'w_o': nrm(ks[13], (DEPTH, MIX_W, D_MODEL), MIX_W ** -0.5),
        'router_w': nrm(ks[14], (DEPTH, D_MODEL, N_EXPERTS), D_MODEL ** -0.5),
        'exp_w_gate': nrm(ks[15], (DEPTH, N_EXPERTS, D_MODEL, D_EXPERT), D_MODEL ** -0.5),
        'exp_w_up': nrm(ks[16], (DEPTH, N_EXPERTS, D_MODEL, D_EXPERT), D_MODEL ** -0.5),
        'exp_w_down': nrm(ks[17], (DEPTH, N_EXPERTS, D_EXPERT, D_MODEL), D_EXPERT ** -0.5),
        'final_norm_g': 1.0 + nrm(ks[18], (D_MODEL,), 0.1),
    }


def reference(x, c, ctx, c_ctx, ada_w, ada_b, norm1_g, norm2_g, w_in, attn_sink, hgrn_lb,
              hgrn_norm_g, conv_w, w_o, router_w, exp_w_gate, exp_w_up, exp_w_down, final_norm_g):
    n_rows = x.shape[1] // GRID_W
    cos, sin = axial_rope_tables(n_rows, x.dtype)
    gamma = jax.nn.softmax(hgrn_lb.astype(F32), axis=0)
    lb_all = jnp.cumsum(gamma, axis=0) - gamma[0]
    sc = jax.nn.silu(c)
    scc = jax.nn.silu(c_ctx)
    xl, xc = x, ctx
    for l in range(DEPTH):
        last = l == DEPTH - 1
        n_ctx_mod = 2 if last else N_MOD
        mod_l = [m[:, None, :] for m in jnp.split(sc @ ada_w[l] + ada_b[l], N_MOD, axis=-1)]
        mod_c = jnp.split(scc @ ada_w[l][:, :n_ctx_mod * D_MODEL] + ada_b[l][:n_ctx_mod * D_MODEL], n_ctx_mod, axis=-1)
        hl = modulate(rmsnorm(xl, norm1_g[l]), mod_l[0], mod_l[1])
        hc = modulate(rmsnorm(xc, norm1_g[l]), mod_c[0], mod_c[1])
        lb_f = lb_all[l, 0].reshape(HG_HEADS, HG_DIM)
        lb_b = lb_all[l, 1].reshape(HG_HEADS, HG_DIM)
        mix_l, mix_c = token_mixers(hl, hc, w_in[l], w_o[l], attn_sink[l], lb_f, lb_b,
                                    hgrn_norm_g[l], conv_w[l], cos, sin, not last)
        xl = xl + mod_l[2] * mix_l
        hl2 = modulate(rmsnorm(xl, norm2_g[l]), mod_l[3], mod_l[4])
        xl = xl + mod_l[5] * expert_choice_moe(hl2, router_w[l], exp_w_gate[l], exp_w_up[l], exp_w_down[l])
        if not last:
            xc = xc + mod_c[2] * mix_c
            hc2 = modulate(rmsnorm(xc, norm2_g[l]), mod_c[3], mod_c[4])
            xc = xc + mod_c[5] * expert_choice_moe(hc2, router_w[l], exp_w_gate[l], exp_w_up[l], exp_w_down[l])
    return rmsnorm(xl, final_norm_g)
```

```python
import functools

import numpy as np
import jax
import jax.numpy as jnp
from jax import lax
from jax.experimental import pallas as pl
from jax.experimental.pallas import tpu as pltpu

F32 = jnp.float32
BF16 = jnp.bfloat16
I32 = jnp.int32
HIGHEST = lax.Precision.HIGHEST

D_MODEL = 1024
GRID_W = 64
EPS = 1e-6
LB_FLOOR = 1e-30
N_MOD = 6
ATT_W = 512
HG_W = 256
CV_W = 256
HEAD_DIM = 64
N_Q = 8
N_KV = 2
GQA_GROUP = 4
KV_W = 128
ROPE_BASE = 10000.0
ROPE_FREQS = 16
HG_HEADS = 4
N_EXPERTS = 16
EC_CAPACITY = 2
D_EXPERT = 2048
IN_COLS = 2816

ROW_TILE = 256
ATT_BLOCK = 128
HG_CHUNK = 128
HG_LEVELS = 7
HG_BATCH = 2
FF_TILE = 512
MOD_TILE = 1536
VMEM_LIMIT = 56 * 1024 * 1024

NEG_BIG = -1e30


def _cparams(sem):
    return pltpu.CompilerParams(dimension_semantics=sem, vmem_limit_bytes=VMEM_LIMIT)


def _dot(a, b):
    return jnp.dot(a, b, preferred_element_type=F32)


def _dot_nt(a, b):
    return lax.dot_general(a, b, (((1,), (1,)), ((), ())), preferred_element_type=F32)


def _mod_kernel(a_ref, w_ref, b_ref, o_ref):
    a = a_ref[...]
    a = a * jax.nn.sigmoid(a)
    o_ref[0] = jnp.dot(a, w_ref[0], precision=HIGHEST, preferred_element_type=F32) + b_ref[0]


def _modulation(cvec, ada_w, ada_b):
    depth = ada_w.shape[0]
    ncol = ada_w.shape[2]
    return pl.pallas_call(
        _mod_kernel,
        grid=(depth, ncol // MOD_TILE),
        in_specs=[
            pl.BlockSpec((8, D_MODEL), lambda l, j: (0, 0)),
            pl.BlockSpec((1, D_MODEL, MOD_TILE), lambda l, j: (l, 0, j)),
            pl.BlockSpec((1, 1, MOD_TILE), lambda l, j: (l, 0, j)),
        ],
        out_specs=pl.BlockSpec((1, 8, MOD_TILE), lambda l, j: (l, 0, j)),
        out_shape=jax.ShapeDtypeStruct((depth, 8, ncol), F32),
        name="modulation",
        compiler_params=_cparams(("arbitrary", "arbitrary")),
    )(cvec, ada_w, ada_b.reshape(depth, 1, ncol))


def _swap_halves(x):
    n = x.shape[-1]
    lane = lax.broadcasted_iota(I32, x.shape, x.ndim - 1)
    up = pltpu.roll(x, n - ROPE_FREQS, x.ndim - 1)
    dn = pltpu.roll(x, ROPE_FREQS, x.ndim - 1)
    return jnp.where((lane % (2 * ROPE_FREQS)) < ROPE_FREQS, up, dn)


def _rmsnorm_mod(x, g, shift, scale):
    var = jnp.mean(x * x, axis=-1, keepdims=True)
    y = x * lax.rsqrt(var + EPS) * g
    return y * (1.0 + scale) + shift


def _inproj_kernel(x_ref, mod_ref, g_ref, w_ref, cos_ref, sin_ref,
                   q_ref, kv_ref, hg_ref, hqg_ref, cv_ref):
    b = pl.program_id(0)
    i = pl.program_id(1)
    row = jnp.where(i == 0, 4, b)
    shift = mod_ref[pl.ds(row, 1), 0:D_MODEL]
    scale = mod_ref[pl.ds(row, 1), D_MODEL:2 * D_MODEL]
    h = _rmsnorm_mod(x_ref[0], g_ref[...], shift, scale)
    p = _dot(h.astype(BF16), w_ref[...])
    cos2 = cos_ref[...]
    sin2 = sin_ref[...]
    k = p[:, 0:KV_W]
    k = k * cos2 + _swap_halves(k) * sin2
    kv_ref[0, :, 0:KV_W] = k.astype(BF16)
    kv_ref[0, :, KV_W:2 * KV_W] = p[:, KV_W:2 * KV_W].astype(BF16)
    hg_ref[0] = p[:, 256:1024]
    q = p[:, 1024:1536]
    cos8 = jnp.concatenate([cos2] * 4, axis=1)
    sin8 = jnp.concatenate([sin2] * 4, axis=1)
    q = (q * cos8 + _swap_halves(q) * sin8) * (HEAD_DIM ** -0.5)
    q_ref[0] = q.astype(BF16)
    hqg_ref[0] = p[:, 1536:2048]
    cv_ref[0] = p[:, 2048:2816]


def _in_projection(xs, mod_l, g, w_bf16, cos_t, sin_t):
    B, S, _ = xs.shape
    nt = S // ROW_TILE
    row_map = lambda b, i: (b, i, 0)
    const2 = lambda b, i: (0, 0)
    return pl.pallas_call(
        _inproj_kernel,
        grid=(B, nt),
        in_specs=[
            pl.BlockSpec((1, ROW_TILE, D_MODEL), row_map),
            pl.BlockSpec((8, N_MOD * D_MODEL), const2),
            pl.BlockSpec((1, D_MODEL), const2),
            pl.BlockSpec((D_MODEL, IN_COLS), const2),
            pl.BlockSpec((ROW_TILE, 2 * HEAD_DIM), lambda b, i: (i, 0)),
            pl.BlockSpec((ROW_TILE, 2 * HEAD_DIM), lambda b, i: (i, 0)),
        ],
        out_specs=[
            pl.BlockSpec((1, ROW_TILE, ATT_W), row_map),
            pl.BlockSpec((1, ROW_TILE, 2 * KV_W), row_map),
            pl.BlockSpec((1, ROW_TILE, 3 * HG_W), row_map),
            pl.BlockSpec((1, ROW_TILE, 2 * HG_W), row_map),
            pl.BlockSpec((1, ROW_TILE, 3 * CV_W), row_map),
        ],
        out_shape=[
            jax.ShapeDtypeStruct((B, S, ATT_W), BF16),
            jax.ShapeDtypeStruct((B, S, 2 * KV_W), BF16),
            jax.ShapeDtypeStruct((B, S, 3 * HG_W), F32),
            jax.ShapeDtypeStruct((B, S, 2 * HG_W), F32),
            jax.ShapeDtypeStruct((B, S, 3 * CV_W), F32),
        ],
        name="in_projection",
        compiler_params=_cparams(("arbitrary", "arbitrary")),
    )(xs, mod_l, g, w_bf16, cos_t, sin_t)


def _attn_kernel(sink_ref, q_ref, kp_ref, kc_ref, kn_ref, kx_ref, o_ref, *, blk0, nblk, nctx):
    n = pl.program_id(1) + blk0
    is_lat = n >= nctx
    has_prev = n > nctx
    has_next = n < nblk - 1
    q = q_ref[0]
    kp = kp_ref[0]
    kc = kc_ref[0]
    kn = kn_ref[0]
    kx = kx_ref[0]
    rows = GQA_GROUP * ATT_BLOCK
    ri = lax.broadcasted_iota(I32, (rows, ATT_BLOCK), 0) % ATT_BLOCK
    cj = lax.broadcasted_iota(I32, (rows, ATT_BLOCK), 1)
    m_prev = (cj >= ri) & has_prev
    m_cur = jnp.broadcast_to(is_lat, (rows, ATT_BLOCK))
    m_next = (cj <= ri) & (has_next & is_lat)
    grp = lax.broadcasted_iota(I32, (rows, 1), 0) // ATT_BLOCK
    for h in range(N_KV):
        ks = slice(h * HEAD_DIM, (h + 1) * HEAD_DIM)
        vs = slice(KV_W + h * HEAD_DIM, KV_W + (h + 1) * HEAD_DIM)
        qg = jnp.concatenate(
            [q[:, (h * GQA_GROUP + g) * HEAD_DIM:(h * GQA_GROUP + g + 1) * HEAD_DIM]
             for g in range(GQA_GROUP)], axis=0)
        sink = jnp.zeros((rows, 1), F32)
        for g in range(GQA_GROUP):
            sink = jnp.where(grp == g, sink_ref[h * GQA_GROUP + g], sink)
        s_p = jnp.where(m_prev, _dot_nt(qg, kp[:, ks]), NEG_BIG)
        s_c = jnp.where(m_cur, _dot_nt(qg, kc[:, ks]), NEG_BIG)
        s_n = jnp.where(m_next, _dot_nt(qg, kn[:, ks]), NEG_BIG)
        s_x = _dot_nt(qg, kx[:, ks])
        m = jnp.maximum(jnp.maximum(jnp.max(s_p, axis=1, keepdims=True),
                                    jnp.max(s_c, axis=1, keepdims=True)),
                        jnp.maximum(jnp.max(s_n, axis=1, keepdims=True),
                                    jnp.max(s_x, axis=1, keepdims=True)))
        m = jnp.maximum(m, sink)
        p_p = jnp.exp(s_p - m)
        p_c = jnp.exp(s_c - m)
        p_n = jnp.exp(s_n - m)
        p_x = jnp.exp(s_x - m)
        den = (jnp.sum(p_p, axis=1, keepdims=True) + jnp.sum(p_c, axis=1, keepdims=True)
               + jnp.sum(p_n, axis=1, keepdims=True) + jnp.sum(p_x, axis=1, keepdims=True)
               + jnp.exp(sink - m))
        o = (_dot(p_p.astype(BF16), kp[:, vs]) + _dot(p_c.astype(BF16), kc[:, vs])
             + _dot(p_n.astype(BF16), kn[:, vs]) + _dot(p_x.astype(BF16), kx[:, vs]))
        o = o / den
        for g in range(GQA_GROUP):
            c0 = (h * GQA_GROUP + g) * HEAD_DIM
            o_ref[0, :, c0:c0 + HEAD_DIM] = o[g * ATT_BLOCK:(g + 1) * ATT_BLOCK].astype(BF16)


def _attention(q, kv, sink, ctx_len, skip_ctx):
    B, S, _ = q.shape
    nblk = S // ATT_BLOCK
    nctx = ctx_len // ATT_BLOCK
    blk0 = nctx if skip_ctx else 0
    blk = lambda f: (lambda b, n: (b, f(n + blk0), 0))
    kern = functools.partial(_attn_kernel, blk0=blk0, nblk=nblk, nctx=nctx)
    return pl.pallas_call(
        kern,
        grid=(B, nblk - blk0),
        in_specs=[
            pl.BlockSpec(memory_space=pltpu.SMEM),
            pl.BlockSpec((1, ATT_BLOCK, ATT_W), blk(lambda n: n)),
            pl.BlockSpec((1, ATT_BLOCK, 2 * KV_W), blk(lambda n: jnp.maximum(n - 1, 0))),
            pl.BlockSpec((1, ATT_BLOCK, 2 * KV_W), blk(lambda n: n)),
            pl.BlockSpec((1, ATT_BLOCK, 2 * KV_W), blk(lambda n: jnp.minimum(n + 1, nblk - 1))),
            pl.BlockSpec((1, ctx_len, 2 * KV_W), lambda b, n: (b, 0, 0)),
        ],
        out_specs=pl.BlockSpec((1, ATT_BLOCK, ATT_W), blk(lambda n: n)),
        out_shape=jax.ShapeDtypeStruct((B, S, ATT_W), BF16),
        name="attention",
        compiler_params=_cparams(("arbitrary", "arbitrary")),
    )(sink, q, kv, kv, kv, kv)


def _hgrn_constants():
    C = HG_CHUNK
    t = np.arange(C)[:, None]
    r = np.arange(C)[None, :]
    blocks = [r <= t, r > t]
    for l in range(HG_LEVELS):
        m = 1 << l
        p = (t // (2 * m)) * (2 * m) + m
        upper = (t % (2 * m)) >= m
        blocks.append(np.where(upper, (r >= p) & (r <= t), (r > t) & (r <= p - 1)))
    fwd = np.concatenate(blocks, axis=0).astype(np.float32)
    bwd = np.concatenate([blk[::-1, ::-1] for blk in blocks], axis=0).astype(np.float32)
    x = t ^ r
    lvl = np.where(x > 0, np.floor(np.log2(np.maximum(x, 1))).astype(np.int32), HG_LEVELS)
    lvl_f = np.where(t >= r, lvl, -1).astype(np.int32)
    lvl_b = np.where(t <= r, lvl, -1).astype(np.int32)
    return np.stack([fwd, bwd]), np.stack([lvl_f, lvl_b])


def _hgrn_chunk(v, z, qr, lb, mm, masks, st_ref, st0, backward):
    C = HG_CHUNK
    logf = jnp.log(jnp.maximum(lb, LB_FLOOR) + (1.0 - lb) * jax.nn.sigmoid(z))
    k = (1.0 - lb) * jax.nn.sigmoid(-z)
    q = qr * jax.nn.sigmoid(qr)
    hi = logf.astype(BF16)
    lo = (logf - hi.astype(F32)).astype(BF16)
    fac = jnp.exp(_dot(mm, hi) + _dot(mm, lo))
    f_in = fac[0:C]
    f_out = fac[C:2 * C]
    f_tot = f_in[0:1] if backward else f_in[C - 1:C]
    qb = q.astype(BF16)
    kb = k.astype(BF16)
    vb = v.astype(BF16)
    q_in = (q * f_in).astype(BF16)
    k_out = (k * f_out).astype(BF16)
    vt = v.T.astype(BF16)
    q_l = []
    k_l = []
    for l in range(HG_LEVELS):
        fl = fac[(2 + l) * C:(3 + l) * C]
        q_l.append((q * fl).astype(BF16))
        k_l.append((k * fl).astype(BF16))
    outs = []
    for h in range(HG_HEADS):
        sl = slice(h * HEAD_DIM, (h + 1) * HEAD_DIM)
        a = jnp.where(masks[HG_LEVELS], _dot_nt(qb[:, sl], kb[:, sl]), 0.0)
        for l in range(HG_LEVELS):
            a = jnp.where(masks[l], _dot_nt(q_l[l][:, sl], k_l[l][:, sl]), a)
        st = st_ref[st0 + h]
        outs.append(_dot(a.astype(BF16), vb[:, sl]) + _dot_nt(q_in[:, sl], st.astype(BF16)))
        st_ref[st0 + h] = st * f_tot[:, sl] + _dot(vt[sl, :], k_out[:, sl])
    return jnp.concatenate(outs, axis=1)


def _hgrn_kernel(vf_ref, zf_ref, qf_ref, vb_ref, zb_ref, qb_ref, lb_ref, m_ref, lvl_ref,
                 of_ref, ob_ref, st_ref):
    j = pl.program_id(1)

    @pl.when(j == 0)
    def _():
        st_ref[...] = jnp.zeros_like(st_ref)

    for d, (v_ref, z_ref, q_ref, o_ref) in enumerate(
            ((vf_ref, zf_ref, qf_ref, of_ref), (vb_ref, zb_ref, qb_ref, ob_ref))):
        lvl = lvl_ref[d]
        masks = [lvl == l for l in range(HG_LEVELS + 1)]
        for s in range(v_ref.shape[0]):
            st0 = (2 * s + d) * HG_HEADS
            o_ref[s] = _hgrn_chunk(v_ref[s], z_ref[s], q_ref[s], lb_ref[d], m_ref[d], masks,
                                   st_ref, st0, backward=(d == 1))


def _hgrn(hg, hqg, lb_l, ctx_len):
    B, S, _ = hg.shape
    nc = S // HG_CHUNK
    nctx = ctx_len // HG_CHUNK
    m_np, lvl_np = _hgrn_constants()
    m_all = jnp.asarray(m_np, BF16)
    lvl = jnp.asarray(lvl_np, I32)

    def back(j):
        return jnp.where(j < nctx, nctx - 1 - j, nc - 1 - (j - nctx))

    bs = HG_BATCH if B % HG_BATCH == 0 else 1
    nrow = m_np.shape[1]
    blk = (bs, HG_CHUNK, HG_W)
    fwd = lambda col: (lambda b, j: (b, j, col))
    bwd = lambda col: (lambda b, j: (b, back(j), col))
    const3 = lambda b, j: (0, 0, 0)
    return pl.pallas_call(
        _hgrn_kernel,
        grid=(B // bs, nc),
        in_specs=[
            pl.BlockSpec(blk, fwd(0)), pl.BlockSpec(blk, fwd(1)), pl.BlockSpec(blk, fwd(0)),
            pl.BlockSpec(blk, bwd(0)), pl.BlockSpec(blk, bwd(2)), pl.BlockSpec(blk, bwd(0)),
            pl.BlockSpec((2, 1, HG_W), const3),
            pl.BlockSpec((2, nrow, HG_CHUNK), const3),
            pl.BlockSpec((2, HG_CHUNK, HG_CHUNK), const3),
        ],
        out_specs=[pl.BlockSpec(blk, fwd(0)), pl.BlockSpec(blk, bwd(0))],
        out_shape=[jax.ShapeDtypeStruct((B, S, HG_W), F32)] * 2,
        scratch_shapes=[pltpu.VMEM((2 * bs * HG_HEADS, HEAD_DIM, HEAD_DIM), F32)],
        name="hgrn_scan",
        compiler_params=_cparams(("arbitrary", "arbitrary")),
    )(hg, hg, hqg, hg, hg, hqg, lb_l.reshape(2, 1, HG_W), m_all, lvl)


def _mixout_kernel(x_ref, att_ref, of_ref, ob_ref, g_ref, cv_ref, cprev_ref, cnext_ref,
                   mod_ref, gain_ref, cw_ref, wo_ref, n2_ref, wr_ref, ones_ref,
                   x1_ref, h2_ref, aff_ref, *, tile0, ntile):
    b = pl.program_id(0)
    i = pl.program_id(1) + tile0
    row = jnp.where(i == 0, 4, b)
    R = ROW_TILE

    def modv(c):
        return mod_ref[pl.ds(row, 1), c * D_MODEL:(c + 1) * D_MODEL]

    o = of_ref[0] + ob_ref[0]
    ms = jnp.dot(o * o, ones_ref[...], precision=HIGHEST, preferred_element_type=F32)
    g = g_ref[0]
    hg = o * lax.rsqrt(ms * (1.0 / HEAD_DIM) + EPS) * gain_ref[...] * (g * jax.nn.sigmoid(g))
    cv = cv_ref[0]
    u = cv[:, CV_W:2 * CV_W] * cv[:, 2 * CV_W:3 * CV_W]
    up = cprev_ref[0]
    un = cnext_ref[0]
    u_prev_row = up[7:8, CV_W:2 * CV_W] * up[7:8, 2 * CV_W:3 * CV_W]
    u_next_row = un[0:1, CV_W:2 * CV_W] * un[0:1, 2 * CV_W:3 * CV_W]
    u_prev_row = jnp.where(i <= 1, 0.0, u_prev_row)
    u_next_row = jnp.where((i == 0) | (i == ntile - 1), 0.0, u_next_row)
    ridx = lax.broadcasted_iota(I32, (R, CV_W), 0)
    u_m1 = jnp.where(ridx == 0, u_prev_row, pltpu.roll(u, 1, 0))
    u_p1 = jnp.where(ridx == R - 1, u_next_row, pltpu.roll(u, R - 1, 0))
    cw = cw_ref[...]
    conv = cv[:, 0:CV_W] * (u_m1 * cw[0:1] + u * cw[1:2] + u_p1 * cw[2:3])
    mix = (_dot(att_ref[0], wo_ref[0:ATT_W])
           + _dot(hg.astype(BF16), wo_ref[ATT_W:ATT_W + HG_W])
           + _dot(conv.astype(BF16), wo_ref[ATT_W + HG_W:D_MODEL]))
    x1 = x_ref[0] + modv(2) * mix
    x1_ref[0] = x1
    h2 = _rmsnorm_mod(x1, n2_ref[...], modv(3), modv(4))
    h2_ref[0] = h2.astype(BF16)
    logits = lax.dot_general(wr_ref[...], h2, (((1,), (1,)), ((), ())),
                             precision=HIGHEST, preferred_element_type=F32)
    e = jnp.exp(logits - jnp.max(logits, axis=0, keepdims=True))
    aff_ref[0] = e / jnp.sum(e, axis=0, keepdims=True)


def _mix_out(xs, att, o2, hqg, cv, mod_l, gain, conv_w, wo_bf16, n2g, wr_t, skip_ctx):
    B, S, _ = xs.shape
    ntile = S // ROW_TILE
    tile0 = 1 if skip_ctx else 0
    sub = ROW_TILE // 8
    nsub = S // 8
    rmap = lambda b, i: (b, i + tile0, 0)
    const2 = lambda b, i: (0, 0)
    ones = jnp.asarray(np.kron(np.eye(HG_HEADS), np.ones((HEAD_DIM, HEAD_DIM))), F32)
    kern = functools.partial(_mixout_kernel, tile0=tile0, ntile=ntile)
    return pl.pallas_call(
        kern,
        grid=(B, ntile - tile0),
        in_specs=[
            pl.BlockSpec((1, ROW_TILE, D_MODEL), rmap),
            pl.BlockSpec((1, ROW_TILE, ATT_W), rmap),
            pl.BlockSpec((1, ROW_TILE, HG_W), rmap),
            pl.BlockSpec((1, ROW_TILE, HG_W), rmap),
            pl.BlockSpec((1, ROW_TILE, HG_W), lambda b, i: (b, i + tile0, 1)),
            pl.BlockSpec((1, ROW_TILE, 3 * CV_W), rmap),
            pl.BlockSpec((1, 8, 3 * CV_W),
                         lambda b, i: (b, jnp.maximum((i + tile0) * sub - 1, 0), 0)),
            pl.BlockSpec((1, 8, 3 * CV_W),
                         lambda b, i: (b, jnp.minimum((i + tile0 + 1) * sub, nsub - 1), 0)),
            pl.BlockSpec((8, N_MOD * D_MODEL), const2),
            pl.BlockSpec((1, HG_W), const2),
            pl.BlockSpec((3, CV_W), const2),
            pl.BlockSpec((D_MODEL, D_MODEL), const2),
            pl.BlockSpec((1, D_MODEL), const2),
            pl.BlockSpec((N_EXPERTS, D_MODEL), const2),
            pl.BlockSpec((HG_W, HG_W), const2),
        ],
        out_specs=[
            pl.BlockSpec((1, ROW_TILE, D_MODEL), rmap),
            pl.BlockSpec((1, ROW_TILE, D_MODEL), rmap),
            pl.BlockSpec((1, N_EXPERTS, ROW_TILE), lambda b, i: (b, 0, i + tile0)),
        ],
        out_shape=[
            jax.ShapeDtypeStruct((B, S, D_MODEL), F32),
            jax.ShapeDtypeStruct((B, S, D_MODEL), BF16),
            jax.ShapeDtypeStruct((B, N_EXPERTS, S), F32),
        ],
        name="mix_out",
        compiler_params=_cparams(("arbitrary", "arbitrary")),
    )(xs, att, o2[0], o2[1], hqg, cv, cv, cv, mod_l, gain, conv_w, wo_bf16, n2g, wr_t, ones)


def _topk_kernel(aff_ref, tri_ref, pos_ref, *, segments):
    tri = tri_ref[...]

    def excl_cumsum(mask, n):
        carry = jnp.zeros((N_EXPERTS, 1), F32)
        parts = []
        for c in range(n // 128):
            blk = jnp.where(mask[:, c * 128:(c + 1) * 128], 1.0, 0.0)
            parts.append(_dot(blk.astype(BF16), tri) + carry)
            carry = carry + jnp.sum(blk, axis=1, keepdims=True)
        return jnp.concatenate(parts, axis=1), carry

    for (lo, n, k) in segments:
        if k == 0:
            pos_ref[0, :, lo:lo + n] = jnp.full((N_EXPERTS, n), -1, I32)
            continue
        a = aff_ref[0, :, lo:lo + n]

        def body(it, thr):
            cand = thr | jnp.left_shift(jnp.int32(1), 30 - it)
            cnt = jnp.sum(jnp.where(a >= pltpu.bitcast(cand, F32), 1.0, 0.0), axis=1, keepdims=True)
            return jnp.where(cnt >= k, cand, thr)

        thr = lax.fori_loop(0, 31, body, jnp.zeros((N_EXPERTS, 1), I32))
        above = a >= pltpu.bitcast(thr + 1, F32)
        tied = jnp.logical_and(a >= pltpu.bitcast(thr, F32), jnp.logical_not(above))
        n_above = jnp.sum(jnp.where(above, 1.0, 0.0), axis=1, keepdims=True)
        rank_tied, _ = excl_cumsum(tied, n)
        sel = above | (tied & (rank_tied < (k - n_above)))
        pos, _ = excl_cumsum(sel, n)
        pos_ref[0, :, lo:lo + n] = jnp.where(sel, pos.astype(I32), -1)


def _topk_positions(aff, segments):
    B, E, S = aff.shape
    tri = jnp.asarray(np.triu(np.ones((128, 128)), 1), BF16)
    kern = functools.partial(_topk_kernel, segments=segments)
    return pl.pallas_call(
        kern,
        grid=(B,),
        in_specs=[pl.BlockSpec((1, E, S), lambda b: (b, 0, 0)),
                  pl.BlockSpec((128, 128), lambda b: (0, 0))],
        out_specs=pl.BlockSpec((1, E, S), lambda b: (b, 0, 0)),
        out_shape=jax.ShapeDtypeStruct((B, E, S), I32),
        name="topk_positions",
        compiler_params=_cparams(("arbitrary",)),
    )(aff, tri)


def _gather_kernel(pos_ref, h_ref, xl_ref, xc_ref, acc_ref, *, ctx_len, cap_l, cap_c):
    e = pl.program_id(1)
    S = h_ref.shape[1]
    pos = pos_ref[0, pl.ds(e, 1), :]
    nchunk = (S - ctx_len) // ROW_TILE
    slot = lax.broadcasted_iota(I32, (cap_l, ROW_TILE), 0)
    acc_ref[...] = jnp.zeros_like(acc_ref)
    for c in range(nchunk):
        lo = ctx_len + c * ROW_TILE
        onehot = jnp.where(pos[:, lo:lo + ROW_TILE] == slot, 1.0, 0.0).astype(BF16)
        acc_ref[...] += _dot(onehot, h_ref[0, lo:lo + ROW_TILE, :])
    xl_ref[0, 0] = acc_ref[...].astype(BF16)
    if cap_c:
        slot_c = lax.broadcasted_iota(I32, (cap_c, ctx_len), 0)
        onehot = jnp.where(pos[:, 0:ctx_len] == slot_c, 1.0, 0.0).astype(BF16)
        xc_ref[0, 0] = _dot(onehot, h_ref[0, 0:ctx_len, :]).astype(BF16)
    else:
        xc_ref[0, 0] = jnp.zeros(xc_ref.shape[2:], BF16)


def _gather_rows(posm, h2, ctx_len, cap_l, cap_c):
    B, E, S = posm.shape
    cc = max(cap_c, 16)
    kern = functools.partial(_gather_kernel, ctx_len=ctx_len, cap_l=cap_l, cap_c=cap_c)
    return pl.pallas_call(
        kern,
        grid=(B, E),
        in_specs=[pl.BlockSpec((1, E, S), lambda b, e: (b, 0, 0)),
                  pl.BlockSpec((1, S, D_MODEL), lambda b, e: (b, 0, 0))],
        out_specs=[pl.BlockSpec((1, 1, cap_l, D_MODEL), lambda b, e: (e, b, 0, 0)),
                   pl.BlockSpec((1, 1, cc, D_MODEL), lambda b, e: (e, b, 0, 0))],
        out_shape=[jax.ShapeDtypeStruct((E, B, cap_l, D_MODEL), BF16),
                   jax.ShapeDtypeStruct((E, B, cc, D_MODEL), BF16)],
        scratch_shapes=[pltpu.VMEM((cap_l, D_MODEL), F32)],
        name="gather_rows",
        compiler_params=_cparams(("arbitrary", "arbitrary")),
    )(posm, h2)


def _expert_kernel(*refs, row_chunk, n_sets):
    x_refs = refs[:n_sets]
    wg_ref, wu_ref, wd_ref = refs[n_sets:n_sets + 3]
    y_refs = refs[n_sets + 3:2 * n_sets + 3]
    acc_refs = refs[2 * n_sets + 3:]
    f = pl.program_id(1)
    nf = pl.num_programs(1)
    wg = wg_ref[0, 0].astype(BF16)
    wu = wu_ref[0, 0].astype(BF16)
    wd = wd_ref[0, 0].astype(BF16)
    for x_ref, y_ref, acc_ref in zip(x_refs, y_refs, acc_refs):
        rows = x_ref.shape[1]
        step = min(row_chunk, rows)
        for r in range(rows // step):
            rs = slice(r * step, (r + 1) * step)
            x = x_ref[0, rs, :]
            g = _dot(x, wg)
            u = _dot(x, wu)
            hid = (g * jax.nn.sigmoid(g) * u).astype(BF16)
            y = _dot(hid, wd)

            @pl.when(f == 0)
            def _():
                acc_ref[rs, :] = y

            @pl.when(f > 0)
            def _():
                acc_ref[rs, :] += y

        @pl.when(f == nf - 1)
        def _():
            y_ref[0] = acc_ref[...].astype(BF16)


def _expert_mlp(x_sets, w_gate, w_up, w_down, layer, row_chunk):
    E = N_EXPERTS
    nf = D_EXPERT // FF_TILE
    n_sets = len(x_sets)
    kern = functools.partial(_expert_kernel, row_chunk=row_chunk, n_sets=n_sets)
    xspec = lambda r: pl.BlockSpec((1, r, D_MODEL), lambda e, f: (e, 0, 0))
    return pl.pallas_call(
        kern,
        grid=(E, nf),
        in_specs=[xspec(xs.shape[1]) for xs in x_sets] + [
            pl.BlockSpec((1, 1, D_MODEL, FF_TILE), lambda e, f: (layer, e, 0, f)),
            pl.BlockSpec((1, 1, D_MODEL, FF_TILE), lambda e, f: (layer, e, 0, f)),
            pl.BlockSpec((1, 1, FF_TILE, D_MODEL), lambda e, f: (layer, e, f, 0))],
        out_specs=[xspec(xs.shape[1]) for xs in x_sets],
        out_shape=[jax.ShapeDtypeStruct(xs.shape, BF16) for xs in x_sets],
        scratch_shapes=[pltpu.VMEM(xs.shape[1:], F32) for xs in x_sets],
        name="expert_mlp",
        compiler_params=_cparams(("arbitrary", "arbitrary")),
    )(*x_sets, w_gate, w_up, w_down)


def _combine_kernel(x_ref, pos_ref, aff_ref, yl_ref, yc_ref, mod_ref, fg_ref, o_ref, acc_ref,
                    *, tile0, cap_l, cap_c, final):
    b = pl.program_id(0)
    i = pl.program_id(1) + tile0
    row = jnp.where(i == 0, 4, b)
    gate = mod_ref[pl.ds(row, 1), 5 * D_MODEL:6 * D_MODEL]
    pos = pos_ref[0]
    aff = aff_ref[0]
    acc_ref[...] = jnp.zeros_like(acc_ref)

    def add_from(y_of, ncol, off):
        lane = lax.broadcasted_iota(I32, (ROW_TILE, ncol), 1)
        for e in range(N_EXPERTS):
            pe = pos[:, e:e + 1]
            tgt = jnp.where(pe >= 0, pe + off, -1)
            onehot = jnp.where(tgt == lane, 1.0, 0.0).astype(BF16)
            acc_ref[...] += _dot(onehot, y_of(e)) * aff[:, e:e + 1]

    if tile0 == 0 and cap_c:
        @pl.when(i == 0)
        def _():
            add_from(lambda e: yc_ref[e], yc_ref.shape[1], b * cap_c)

        @pl.when(i > 0)
        def _():
            add_from(lambda e: yl_ref[e, 0], cap_l, 0)
    else:
        add_from(lambda e: yl_ref[e, 0], cap_l, 0)
    x2 = x_ref[0] + gate * acc_ref[...]
    if final:
        var = jnp.mean(x2 * x2, axis=-1, keepdims=True)
        x2 = x2 * lax.rsqrt(var + EPS) * fg_ref[...]
    o_ref[0] = x2


def _combine(x1, pos_t, aff_t, yl, yc, mod_l, final_g, cap_l, cap_c, final):
    B, S, _ = x1.shape
    E = N_EXPERTS
    tile0 = 1 if final else 0
    ntile = S // ROW_TILE - tile0
    rmap = lambda b, i: (b, i + tile0, 0)
    const2 = lambda b, i: (0, 0)
    kern = functools.partial(_combine_kernel, tile0=tile0, cap_l=cap_l, cap_c=cap_c, final=final)
    return pl.pallas_call(
        kern,
        grid=(B, ntile),
        in_specs=[
            pl.BlockSpec((1, ROW_TILE, D_MODEL), rmap),
            pl.BlockSpec((1, ROW_TILE, E), rmap),
            pl.BlockSpec((1, ROW_TILE, E), rmap),
            pl.BlockSpec((E, 1, cap_l, D_MODEL), lambda b, i: (0, b, 0, 0),
                         pipeline_mode=pl.Buffered(1)),
            pl.BlockSpec((E, yc.shape[1], D_MODEL), lambda b, i: (0, 0, 0),
                         pipeline_mode=pl.Buffered(1)),
            pl.BlockSpec((8, N_MOD * D_MODEL), const2),
            pl.BlockSpec((1, D_MODEL), const2),
        ],
        out_specs=pl.BlockSpec((1, ROW_TILE, D_MODEL), lambda b, i: (b, i, 0)),
        out_shape=jax.ShapeDtypeStruct((B, ntile * ROW_TILE, D_MODEL), F32),
        scratch_shapes=[pltpu.VMEM((ROW_TILE, D_MODEL), F32)],
        name="combine",
        compiler_params=_cparams(("arbitrary", "arbitrary")),
    )(x1, pos_t, aff_t, yl, yc, mod_l, final_g)


def _rope_tables(n_lat, ctx_len):
    t = np.arange(n_lat)
    pos = np.stack([t // GRID_W, t % GRID_W], axis=-1).astype(np.float32)
    inv = (ROPE_BASE ** (-np.arange(ROPE_FREQS, dtype=np.float32) / ROPE_FREQS)).astype(np.float32)
    ang = pos[:, :, None] * inv
    cos = np.cos(ang).astype(np.float32)
    sin = np.sin(ang).astype(np.float32)
    cos64 = np.concatenate([cos[:, 0], cos[:, 0], cos[:, 1], cos[:, 1]], axis=-1)
    sin64 = np.concatenate([-sin[:, 0], sin[:, 0], -sin[:, 1], sin[:, 1]], axis=-1)
    cos_t = np.concatenate([np.ones((ctx_len, 64), np.float32), cos64], axis=0)
    sin_t = np.concatenate([np.zeros((ctx_len, 64), np.float32), sin64], axis=0)
    return (jnp.asarray(np.tile(cos_t, (1, 2))), jnp.asarray(np.tile(sin_t, (1, 2))))


def kernel(x, c, ctx, c_ctx, ada_w, ada_b, norm1_g, norm2_g, w_in, attn_sink, hgrn_lb,
           hgrn_norm_g, conv_w, w_o, router_w, exp_w_gate, exp_w_up, exp_w_down, final_norm_g):
    B, T, D = x.shape
    L = ctx.shape[1]
    depth = ada_w.shape[0]
    assert D == D_MODEL and L == ROW_TILE and T % ROW_TILE == 0 and B <= 4
    S = L + T
    cap_l = EC_CAPACITY * T // N_EXPERTS
    cap_c = EC_CAPACITY * L // N_EXPERTS
    assert cap_l % 16 == 0 and cap_c % 16 == 0 and (B * cap_c) % 16 == 0

    cos_t, sin_t = _rope_tables(T, L)
    gamma = jax.nn.softmax(hgrn_lb.astype(F32), axis=0)
    lb_all = jnp.cumsum(gamma, axis=0) - gamma[0]
    cvec = jnp.concatenate([c, jnp.zeros((4 - B, D), F32), c_ctx[None],
                            jnp.zeros((3, D), F32)], axis=0)
    mod = _modulation(cvec, ada_w, ada_b)
    xs = jnp.concatenate([ctx, x], axis=1)

    for l in range(depth):
        last = l == depth - 1
        q, kv, hg, hqg, cv = _in_projection(xs, mod[l], norm1_g[l][None], w_in[l].astype(BF16),
                                            cos_t, sin_t)
        att = _attention(q, kv, attn_sink[l], L, skip_ctx=last)
        o2 = _hgrn(hg, hqg, lb_all[l], L)
        x1, h2, aff = _mix_out(xs, att, o2, hqg, cv, mod[l], hgrn_norm_g[l][None], conv_w[l],
                               w_o[l].astype(BF16), norm2_g[l][None], router_w[l].T, skip_ctx=last)
        cc = 0 if last else cap_c
        posm = _topk_positions(aff, ((0, L, cc), (L, T, cap_l)))
        xl_g, xc_g = _gather_rows(posm, h2, L, cap_l, cc)
        x_sets = [xl_g.reshape(N_EXPERTS, B * cap_l, D)]
        if not last:
            x_sets.append(xc_g.reshape(N_EXPERTS, B * xc_g.shape[2], D))
        ys = _expert_mlp(x_sets, exp_w_gate, exp_w_up, exp_w_down, l, cap_l)
        yl = ys[0].reshape(N_EXPERTS, B, cap_l, D)
        yc = jnp.zeros((N_EXPERTS, 16, D), BF16) if last else ys[1]
        xs = _combine(x1, jnp.swapaxes(posm, 1, 2), jnp.swapaxes(aff, 1, 2), yl, yc, mod[l],
                      final_norm_g[None], cap_l, cc, final=last)
    return xs
```

```python
import functools

import numpy as np
import jax
import jax.numpy as jnp
from jax import lax
from jax.experimental import pallas as pl
from jax.experimental.pallas import tpu as pltpu

F32 = jnp.float32
BF16 = jnp.bfloat16
I32 = jnp.int32
HIGHEST = lax.Precision.HIGHEST

D_MODEL = 1024
GRID_W = 64
EPS = 1e-6
LB_FLOOR = 1e-30
N_MOD = 6
ATT_W = 512
HG_W = 256
CV_W = 256
HEAD_DIM = 64
N_Q = 8
N_KV = 2
GQA_GROUP = 4
KV_W = 128
ROPE_BASE = 10000.0
ROPE_FREQS = 16
HG_HEADS = 4
N_EXPERTS = 16
EC_CAPACITY = 2
D_EXPERT = 2048
IN_COLS = 2816

ROW_TILE = 256
ATT_BLOCK = 128
HG_CHUNK = 128
HG_LEVELS = 7
HG_BATCH = 2
FF_TILE = 512
MOD_TILE = 1536
VMEM_LIMIT = 56 * 1024 * 1024

NEG_BIG = -1e30


def _cparams(sem):
    return pltpu.CompilerParams(dimension_semantics=sem, vmem_limit_bytes=VMEM_LIMIT)


def _dot(a, b):
    return jnp.dot(a, b, preferred_element_type=F32)


def _dot_nt(a, b):
    return lax.dot_general(a, b, (((1,), (1,)), ((), ())), preferred_element_type=F32)


def _mod_kernel(a_ref, w_ref, b_ref, o_ref):
    a = a_ref[...]
    a = a * jax.nn.sigmoid(a)
    o_ref[0] = jnp.dot(a, w_ref[0], precision=HIGHEST, preferred_element_type=F32) + b_ref[0]


def _modulation(cvec, ada_w, ada_b):
    depth = ada_w.shape[0]
    ncol = ada_w.shape[2]
    return pl.pallas_call(
        _mod_kernel,
        grid=(depth, ncol // MOD_TILE),
        in_specs=[
            pl.BlockSpec((8, D_MODEL), lambda l, j: (0, 0)),
            pl.BlockSpec((1, D_MODEL, MOD_TILE), lambda l, j: (l, 0, j)),
            pl.BlockSpec((1, 1, MOD_TILE), lambda l, j: (l, 0, j)),
        ],
        out_specs=pl.BlockSpec((1, 8, MOD_TILE), lambda l, j: (l, 0, j)),
        out_shape=jax.ShapeDtypeStruct((depth, 8, ncol), F32),
        name="modulation",
        compiler_params=_cparams(("arbitrary", "arbitrary")),
    )(cvec, ada_w, ada_b.reshape(depth, 1, ncol))


def _swap_halves(x):
    n = x.shape[-1]
    lane = lax.broadcasted_iota(I32, x.shape, x.ndim - 1)
    up = pltpu.roll(x, n - ROPE_FREQS, x.ndim - 1)
    dn = pltpu.roll(x, ROPE_FREQS, x.ndim - 1)
    return jnp.where((lane % (2 * ROPE_FREQS)) < ROPE_FREQS, up, dn)


def _rmsnorm_mod(x, g, shift, scale):
    var = jnp.mean(x * x, axis=-1, keepdims=True)
    y = x * lax.rsqrt(var + EPS) * g
    return y * (1.0 + scale) + shift


def _inproj_kernel(x_ref, mod_ref, g_ref, w_ref, cos_ref, sin_ref,
                   q_ref, kv_ref, hg_ref, hqg_ref, cv_ref):
    b = pl.program_id(0)
    i = pl.program_id(1)
    row = jnp.where(i == 0, 4, b)
    shift = mod_ref[pl.ds(row, 1), 0:D_MODEL]
    scale = mod_ref[pl.ds(row, 1), D_MODEL:2 * D_MODEL]
    h = _rmsnorm_mod(x_ref[0], g_ref[...], shift, scale)
    p = _dot(h.astype(BF16), w_ref[...])
    cos2 = cos_ref[...]
    sin2 = sin_ref[...]
    k = p[:, 0:KV_W]
    k = k * cos2 + _swap_halves(k) * sin2
    kv_ref[0, :, 0:KV_W] = k.astype(BF16)
    kv_ref[0, :, KV_W:2 * KV_W] = p[:, KV_W:2 * KV_W].astype(BF16)
    hg_ref[0] = p[:, 256:1024]
    q = p[:, 1024:1536]
    cos8 = jnp.concatenate([cos2] * 4, axis=1)
    sin8 = jnp.concatenate([sin2] * 4, axis=1)
    q = (q * cos8 + _swap_halves(q) * sin8) * (HEAD_DIM ** -0.5)
    q_ref[0] = q.astype(BF16)
    hqg_ref[0] = p[:, 1536:2048]
    cv_ref[0] = p[:, 2048:2816]


def _in_projection(xs, mod_l, g, w_bf16, cos_t, sin_t):
    B, S, _ = xs.shape
    nt = S // ROW_TILE
    row_map = lambda b, i: (b, i, 0)
    const2 = lambda b, i: (0, 0)
    return pl.pallas_call(
        _inproj_kernel,
        grid=(B, nt),
        in_specs=[
            pl.BlockSpec((1, ROW_TILE, D_MODEL), row_map),
            pl.BlockSpec((8, N_MOD * D_MODEL), const2),
            pl.BlockSpec((1, D_MODEL), const2),
            pl.BlockSpec((D_MODEL, IN_COLS), const2),
            pl.BlockSpec((ROW_TILE, 2 * HEAD_DIM), lambda b, i: (i, 0)),
            pl.BlockSpec((ROW_TILE, 2 * HEAD_DIM), lambda b, i: (i, 0)),
        ],
        out_specs=[
            pl.BlockSpec((1, ROW_TILE, ATT_W), row_map),
            pl.BlockSpec((1, ROW_TILE, 2 * KV_W), row_map),
            pl.BlockSpec((1, ROW_TILE, 3 * HG_W), row_map),
            pl.BlockSpec((1, ROW_TILE, 2 * HG_W), row_map),
            pl.BlockSpec((1, ROW_TILE, 3 * CV_W), row_map),
        ],
        out_shape=[
            jax.ShapeDtypeStruct((B, S, ATT_W), BF16),
            jax.ShapeDtypeStruct((B, S, 2 * KV_W), BF16),
            jax.ShapeDtypeStruct((B, S, 3 * HG_W), F32),
            jax.ShapeDtypeStruct((B, S, 2 * HG_W), F32),
            jax.ShapeDtypeStruct((B, S, 3 * CV_W), F32),
        ],
        name="in_projection",
        compiler_params=_cparams(("arbitrary", "arbitrary")),
    )(xs, mod_l, g, w_bf16, cos_t, sin_t)


def _attn_kernel(sink_ref, q_ref, kp_ref, kc_ref, kn_ref, kx_ref, o_ref, *, blk0, nblk, nctx):
    n = pl.program_id(1) + blk0
    is_lat = n >= nctx
    has_prev = n > nctx
    has_next = n < nblk - 1
    W = ATT_BLOCK
    q = q_ref[0]
    qrows = jnp.concatenate([q[:, g * W:(g + 1) * W] for g in range(GQA_GROUP)], axis=0)
    kv_all = jnp.concatenate([kp_ref[0], kc_ref[0], kn_ref[0], kx_ref[0]], axis=0)
    nkeys = kv_all.shape[0]
    k_all = kv_all[:, 0:KV_W]
    v_ext = jnp.concatenate([kv_all[:, KV_W:2 * KV_W], jnp.ones((nkeys, KV_W), BF16)], axis=1)
    rows = GQA_GROUP * W
    ri = lax.broadcasted_iota(I32, (rows, W), 0) % W
    cj = lax.broadcasted_iota(I32, (rows, W), 1)
    m_prev = (cj >= ri) & has_prev
    m_cur = jnp.broadcast_to(is_lat, (rows, W))
    m_next = (cj <= ri) & (has_next & is_lat)
    grp = lax.broadcasted_iota(I32, (rows, 1), 0) // W
    low = cj < HEAD_DIM
    outs = []
    for h in range(N_KV):
        qh = jnp.where(low if h == 0 else jnp.logical_not(low), qrows, jnp.zeros_like(qrows))
        sink = jnp.zeros((rows, 1), F32)
        for g in range(GQA_GROUP):
            sink = jnp.where(grp == g, sink_ref[h * GQA_GROUP + g], sink)
        s = _dot_nt(qh, k_all)
        segs = [jnp.where(m_prev, s[:, 0:W], NEG_BIG),
                jnp.where(m_cur, s[:, W:2 * W], NEG_BIG),
                jnp.where(m_next, s[:, 2 * W:3 * W], NEG_BIG)]
        segs += [s[:, c:c + W] for c in range(3 * W, nkeys, W)]
        mx = segs[0]
        for sg in segs[1:]:
            mx = jnp.maximum(mx, sg)
        m = jnp.maximum(jnp.max(mx, axis=1, keepdims=True), sink)
        p = jnp.concatenate([jnp.exp(sg - m).astype(BF16) for sg in segs], axis=1)
        oe = _dot(p, v_ext)
        den = oe[:, KV_W:2 * KV_W] + jnp.exp(sink - m)
        outs.append(oe[:, 0:KV_W] / den)
    o = jnp.where(low, outs[0], outs[1])
    for g in range(GQA_GROUP):
        o_ref[0, :, g * W:(g + 1) * W] = o[g * W:(g + 1) * W].astype(BF16)


def _attention(q, kv, sink, ctx_len, skip_ctx):
    B, S, _ = q.shape
    nblk = S // ATT_BLOCK
    nctx = ctx_len // ATT_BLOCK
    blk0 = nctx if skip_ctx else 0
    blk = lambda f: (lambda b, n: (b, f(n + blk0), 0))
    kern = functools.partial(_attn_kernel, blk0=blk0, nblk=nblk, nctx=nctx)
    return pl.pallas_call(
        kern,
        grid=(B, nblk - blk0),
        in_specs=[
            pl.BlockSpec(memory_space=pltpu.SMEM),
            pl.BlockSpec((1, ATT_BLOCK, ATT_W), blk(lambda n: n)),
            pl.BlockSpec((1, ATT_BLOCK, 2 * KV_W), blk(lambda n: jnp.maximum(n - 1, 0))),
            pl.BlockSpec((1, ATT_BLOCK, 2 * KV_W), blk(lambda n: n)),
            pl.BlockSpec((1, ATT_BLOCK, 2 * KV_W), blk(lambda n: jnp.minimum(n + 1, nblk - 1))),
            pl.BlockSpec((1, ctx_len, 2 * KV_W), lambda b, n: (b, 0, 0)),
        ],
        out_specs=pl.BlockSpec((1, ATT_BLOCK, ATT_W), blk(lambda n: n)),
        out_shape=jax.ShapeDtypeStruct((B, S, ATT_W), BF16),
        name="attention",
        compiler_params=_cparams(("arbitrary", "arbitrary")),
    )(sink, q, kv, kv, kv, kv)


def _hgrn_constants():
    C = HG_CHUNK
    t = np.arange(C)[:, None]
    r = np.arange(C)[None, :]
    tri = np.stack([r <= t, r >= t]).astype(np.float32)
    x = t ^ r
    lvl = np.where(x > 0, np.floor(np.log2(np.maximum(x, 1))).astype(np.int32), HG_LEVELS)
    lvl_f = np.where(t >= r, lvl, -1).astype(np.int32)
    lvl_b = np.where(t <= r, lvl, -1).astype(np.int32)
    lvl2 = np.stack([np.tile(lvl_f, (1, HG_HEADS)), np.tile(lvl_b, (1, HG_HEADS))])
    return tri, lvl2


def _span_row(x, span, row):
    C = x.shape[0]
    if span >= 8:
        x3 = x.reshape(C // span, span, x.shape[1])
        return jnp.broadcast_to(x3[:, row:row + 1, :], x3.shape).reshape(x.shape)
    pos = lax.broadcasted_iota(I32, x.shape, 0) % span
    out = x
    for p in range(span):
        if p != row:
            out = jnp.where(pos == p, pltpu.roll(x, (p - row) % C, 0), out)
    return out


def _hgrn_chunk(v, z, qr, lb, tri, masks, head_of_lane, same_head, st_ref, sidx, backward):
    C = HG_CHUNK
    logf = jnp.log(jnp.maximum(lb, LB_FLOOR) + (1.0 - lb) * jax.nn.sigmoid(z))
    k = (1.0 - lb) * jax.nn.sigmoid(-z)
    q = qr * jax.nn.sigmoid(qr)
    hi = logf.astype(BF16)
    lo = (logf - hi.astype(F32)).astype(BF16)
    cs = _dot(tri, jnp.concatenate([hi, lo], axis=1))
    lam = cs[:, 0:HG_W] + cs[:, HG_W:2 * HG_W]
    tot = lam[0:1] if backward else lam[C - 1:C]
    f_tot = jnp.exp(tot)

    def per_head_rows(x):
        zero = jnp.zeros_like(x)
        return jnp.concatenate([jnp.where(head_of_lane == h, x, zero) for h in range(HG_HEADS)],
                               axis=0)

    a = jnp.where(masks[HG_LEVELS], _dot_nt(q.astype(BF16), per_head_rows(k.astype(BF16))), 0.0)
    for l in range(HG_LEVELS):
        m = 1 << l
        ref = _span_row(lam, 2 * m, m if backward else m - 1)
        fac = jnp.exp(-jnp.abs(lam - ref))
        s_l = _dot_nt((q * fac).astype(BF16), per_head_rows((k * fac).astype(BF16)))
        a = jnp.where(masks[l], s_l, a)
    st = st_ref[sidx]
    q_in = (q * jnp.exp(lam)).astype(BF16)
    o = _dot(a.astype(BF16), per_head_rows(v.astype(BF16))) + _dot_nt(q_in, st.astype(BF16))
    k_out = (k * jnp.exp(tot - lam)).astype(BF16)
    upd = _dot(v.T.astype(BF16), k_out)
    st_ref[sidx] = st * f_tot + jnp.where(same_head, upd, 0.0)
    return o


def _hgrn_kernel(vf_ref, zf_ref, qf_ref, vb_ref, zb_ref, qb_ref, lb_ref, tri_ref, lvl_ref,
                 of_ref, ob_ref, st_ref):
    j = pl.program_id(1)

    @pl.when(j == 0)
    def _():
        st_ref[...] = jnp.zeros_like(st_ref)

    head_of_lane = lax.broadcasted_iota(I32, (HG_CHUNK, HG_W), 1) // HEAD_DIM
    same_head = (lax.broadcasted_iota(I32, (HG_W, HG_W), 0) // HEAD_DIM
                 == lax.broadcasted_iota(I32, (HG_W, HG_W), 1) // HEAD_DIM)
    for d, (v_ref, z_ref, q_ref, o_ref) in enumerate(
            ((vf_ref, zf_ref, qf_ref, of_ref), (vb_ref, zb_ref, qb_ref, ob_ref))):
        lvl = lvl_ref[d]
        masks = [lvl == l for l in range(HG_LEVELS + 1)]
        for s in range(v_ref.shape[0]):
            o_ref[s] = _hgrn_chunk(v_ref[s], z_ref[s], q_ref[s], lb_ref[d], tri_ref[d], masks,
                                   head_of_lane, same_head, st_ref, 2 * s + d, backward=(d == 1))


def _hgrn(hg, hqg, lb_l, ctx_len):
    B, S, _ = hg.shape
    nc = S // HG_CHUNK
    nctx = ctx_len // HG_CHUNK
    tri_np, lvl_np = _hgrn_constants()
    tri = jnp.asarray(tri_np, BF16)
    lvl = jnp.asarray(lvl_np, I32)

    def back(j):
        return jnp.where(j < nctx, nctx - 1 - j, nc - 1 - (j - nctx))

    bs = HG_BATCH if B % HG_BATCH == 0 else 1
    blk = (bs, HG_CHUNK, HG_W)
    fwd = lambda col: (lambda b, j: (b, j, col))
    bwd = lambda col: (lambda b, j: (b, back(j), col))
    const3 = lambda b, j: (0, 0, 0)
    return pl.pallas_call(
        _hgrn_kernel,
        grid=(B // bs, nc),
        in_specs=[
            pl.BlockSpec(blk, fwd(0)), pl.BlockSpec(blk, fwd(1)), pl.BlockSpec(blk, fwd(0)),
            pl.BlockSpec(blk, bwd(0)), pl.BlockSpec(blk, bwd(2)), pl.BlockSpec(blk, bwd(0)),
            pl.BlockSpec((2, 1, HG_W), const3),
            pl.BlockSpec((2, HG_CHUNK, HG_CHUNK), const3),
            pl.BlockSpec((2, HG_CHUNK, HG_HEADS * HG_CHUNK), const3),
        ],
        out_specs=[pl.BlockSpec(blk, fwd(0)), pl.BlockSpec(blk, bwd(0))],
        out_shape=[jax.ShapeDtypeStruct((B, S, HG_W), F32)] * 2,
        scratch_shapes=[pltpu.VMEM((2 * bs, HG_W, HG_W), F32)],
        name="hgrn_scan",
        compiler_params=_cparams(("arbitrary", "arbitrary")),
    )(hg, hg, hqg, hg, hg, hqg, lb_l.reshape(2, 1, HG_W), tri, lvl)


def _mixout_kernel(x_ref, att_ref, of_ref, ob_ref, g_ref, cv_ref, cprev_ref, cnext_ref,
                   mod_ref, gain_ref, cw_ref, wo_ref, n2_ref, wr_ref, ones_ref,
                   x1_ref, h2_ref, aff_ref, *, tile0, ntile):
    b = pl.program_id(0)
    i = pl.program_id(1) + tile0
    row = jnp.where(i == 0, 4, b)
    R = ROW_TILE

    def modv(c):
        return mod_ref[pl.ds(row, 1), c * D_MODEL:(c + 1) * D_MODEL]

    o = of_ref[0] + ob_ref[0]
    sq = o * o
    sq_hi = sq.astype(BF16)
    sq_lo = (sq - sq_hi.astype(F32)).astype(BF16)
    ms = _dot(sq_hi, ones_ref[...]) + _dot(sq_lo, ones_ref[...])
    g = g_ref[0]
    hg = o * lax.rsqrt(ms * (1.0 / HEAD_DIM) + EPS) * gain_ref[...] * (g * jax.nn.sigmoid(g))
    cv = cv_ref[0]
    u = cv[:, CV_W:2 * CV_W] * cv[:, 2 * CV_W:3 * CV_W]
    up = cprev_ref[0]
    un = cnext_ref[0]
    u_prev_row = up[7:8, CV_W:2 * CV_W] * up[7:8, 2 * CV_W:3 * CV_W]
    u_next_row = un[0:1, CV_W:2 * CV_W] * un[0:1, 2 * CV_W:3 * CV_W]
    u_prev_row = jnp.where(i <= 1, 0.0, u_prev_row)
    u_next_row = jnp.where((i == 0) | (i == ntile - 1), 0.0, u_next_row)
    ridx = lax.broadcasted_iota(I32, (R, CV_W), 0)
    u_m1 = jnp.where(ridx == 0, u_prev_row, pltpu.roll(u, 1, 0))
    u_p1 = jnp.where(ridx == R - 1, u_next_row, pltpu.roll(u, R - 1, 0))
    cw = cw_ref[...]
    conv = cv[:, 0:CV_W] * (u_m1 * cw[0:1] + u * cw[1:2] + u_p1 * cw[2:3])
    mix = (_dot(att_ref[0], wo_ref[0:ATT_W])
           + _dot(hg.astype(BF16), wo_ref[ATT_W:ATT_W + HG_W])
           + _dot(conv.astype(BF16), wo_ref[ATT_W + HG_W:D_MODEL]))
    x1 = x_ref[0] + modv(2) * mix
    x1_ref[0] = x1
    h2 = _rmsnorm_mod(x1, n2_ref[...], modv(3), modv(4))
    h_hi = h2.astype(BF16)
    h2_ref[0] = h_hi
    h_lo = (h2 - h_hi.astype(F32)).astype(BF16)
    wr = wr_ref[...]
    w_hi = wr.astype(BF16)
    w_lo = (wr - w_hi.astype(F32)).astype(BF16)
    r1 = _dot_nt(jnp.concatenate([w_hi, w_lo], axis=0), h_hi)
    logits = r1[0:N_EXPERTS] + r1[N_EXPERTS:2 * N_EXPERTS] + _dot_nt(w_hi, h_lo)
    e = jnp.exp(logits - jnp.max(logits, axis=0, keepdims=True))
    aff_ref[0] = e / jnp.sum(e, axis=0, keepdims=True)


def _mix_out(xs, att, o2, hqg, cv, mod_l, gain, conv_w, wo_bf16, n2g, wr_t, skip_ctx):
    B, S, _ = xs.shape
    ntile = S // ROW_TILE
    tile0 = 1 if skip_ctx else 0
    sub = ROW_TILE // 8
    nsub = S // 8
    rmap = lambda b, i: (b, i + tile0, 0)
    const2 = lambda b, i: (0, 0)
    ones = jnp.asarray(np.kron(np.eye(HG_HEADS), np.ones((HEAD_DIM, HEAD_DIM))), BF16)
    kern = functools.partial(_mixout_kernel, tile0=tile0, ntile=ntile)
    return pl.pallas_call(
        kern,
        grid=(B, ntile - tile0),
        in_specs=[
            pl.BlockSpec((1, ROW_TILE, D_MODEL), rmap),
            pl.BlockSpec((1, ROW_TILE, ATT_W), rmap),
            pl.BlockSpec((1, ROW_TILE, HG_W), rmap),
            pl.BlockSpec((1, ROW_TILE, HG_W), rmap),
            pl.BlockSpec((1, ROW_TILE, HG_W), lambda b, i: (b, i + tile0, 1)),
            pl.BlockSpec((1, ROW_TILE, 3 * CV_W), rmap),
            pl.BlockSpec((1, 8, 3 * CV_W),
                         lambda b, i: (b, jnp.maximum((i + tile0) * sub - 1, 0), 0)),
            pl.BlockSpec((1, 8, 3 * CV_W),
                         lambda b, i: (b, jnp.minimum((i + tile0 + 1) * sub, nsub - 1), 0)),
            pl.BlockSpec((8, N_MOD * D_MODEL), const2),
            pl.BlockSpec((1, HG_W), const2),
            pl.BlockSpec((3, CV_W), const2),
            pl.BlockSpec((D_MODEL, D_MODEL), const2),
            pl.BlockSpec((1, D_MODEL), const2),
            pl.BlockSpec((N_EXPERTS, D_MODEL), const2),
            pl.BlockSpec((HG_W, HG_W), const2),
        ],
        out_specs=[
            pl.BlockSpec((1, ROW_TILE, D_MODEL), rmap),
            pl.BlockSpec((1, ROW_TILE, D_MODEL), rmap),
            pl.BlockSpec((1, N_EXPERTS, ROW_TILE), lambda b, i: (b, 0, i + tile0)),
        ],
        out_shape=[
            jax.ShapeDtypeStruct((B, S, D_MODEL), F32),
            jax.ShapeDtypeStruct((B, S, D_MODEL), BF16),
            jax.ShapeDtypeStruct((B, N_EXPERTS, S), F32),
        ],
        name="mix_out",
        compiler_params=_cparams(("arbitrary", "arbitrary")),
    )(xs, att, o2[0], o2[1], hqg, cv, cv, cv, mod_l, gain, conv_w, wo_bf16, n2g, wr_t, ones)


def _topk_kernel(aff_ref, tri_ref, pos_ref, *, segments):
    tri = tri_ref[...]

    def excl_cumsum(mask, n):
        carry = jnp.zeros((N_EXPERTS, 1), F32)
        parts = []
        for c in range(n // 128):
            blk = jnp.where(mask[:, c * 128:(c + 1) * 128], 1.0, 0.0)
            parts.append(_dot(blk.astype(BF16), tri) + carry)
            carry = carry + jnp.sum(blk, axis=1, keepdims=True)
        return jnp.concatenate(parts, axis=1), carry

    for (lo, n, k) in segments:
        if k == 0:
            pos_ref[0, :, lo:lo + n] = jnp.full((N_EXPERTS, n), -1, I32)
            continue
        a = aff_ref[0, :, lo:lo + n]

        def body(it, thr):
            cand = thr | jnp.left_shift(jnp.int32(1), 30 - it)
            cnt = jnp.sum(jnp.where(a >= pltpu.bitcast(cand, F32), 1.0, 0.0), axis=1, keepdims=True)
            return jnp.where(cnt >= k, cand, thr)

        thr = lax.fori_loop(0, 31, body, jnp.zeros((N_EXPERTS, 1), I32))
        above = a >= pltpu.bitcast(thr + 1, F32)
        tied = jnp.logical_and(a >= pltpu.bitcast(thr, F32), jnp.logical_not(above))
        n_above = jnp.sum(jnp.where(above, 1.0, 0.0), axis=1, keepdims=True)
        rank_tied, _ = excl_cumsum(tied, n)
        sel = above | (tied & (rank_tied < (k - n_above)))
        pos, _ = excl_cumsum(sel, n)
        pos_ref[0, :, lo:lo + n] = jnp.where(sel, pos.astype(I32), -1)


def _topk_positions(aff, segments):
    B, E, S = aff.shape
    tri = jnp.asarray(np.triu(np.ones((128, 128)), 1), BF16)
    kern = functools.partial(_topk_kernel, segments=segments)
    return pl.pallas_call(
        kern,
        grid=(B,),
        in_specs=[pl.BlockSpec((1, E, S), lambda b: (b, 0, 0)),
                  pl.BlockSpec((128, 128), lambda b: (0, 0))],
        out_specs=pl.BlockSpec((1, E, S), lambda b: (b, 0, 0)),
        out_shape=jax.ShapeDtypeStruct((B, E, S), I32),
        name="topk_positions",
        compiler_params=_cparams(("arbitrary",)),
    )(aff, tri)


def _gather_kernel(pos_ref, h_ref, xl_ref, xc_ref, acc_ref, *, ctx_len, cap_l, cap_c):
    e = pl.program_id(1)
    S = h_ref.shape[1]
    pos = pos_ref[0, pl.ds(e, 1), :]
    nchunk = (S - ctx_len) // ROW_TILE
    slot = lax.broadcasted_iota(I32, (cap_l, ROW_TILE), 0)
    acc_ref[...] = jnp.zeros_like(acc_ref)
    for c in range(nchunk):
        lo = ctx_len + c * ROW_TILE
        onehot = jnp.where(pos[:, lo:lo + ROW_TILE] == slot, 1.0, 0.0).astype(BF16)
        acc_ref[...] += _dot(onehot, h_ref[0, lo:lo + ROW_TILE, :])
    xl_ref[0, 0] = acc_ref[...].astype(BF16)
    if cap_c:
        slot_c = lax.broadcasted_iota(I32, (cap_c, ctx_len), 0)
        onehot = jnp.where(pos[:, 0:ctx_len] == slot_c, 1.0, 0.0).astype(BF16)
        xc_ref[0, 0] = _dot(onehot, h_ref[0, 0:ctx_len, :]).astype(BF16)
    else:
        xc_ref[0, 0] = jnp.zeros(xc_ref.shape[2:], BF16)


def _gather_rows(posm, h2, ctx_len, cap_l, cap_c):
    B, E, S = posm.shape
    cc = max(cap_c, 16)
    kern = functools.partial(_gather_kernel, ctx_len=ctx_len, cap_l=cap_l, cap_c=cap_c)
    return pl.pallas_call(
        kern,
        grid=(B, E),
        in_specs=[pl.BlockSpec((1, E, S), lambda b, e: (b, 0, 0)),
                  pl.BlockSpec((1, S, D_MODEL), lambda b, e: (b, 0, 0))],
        out_specs=[pl.BlockSpec((1, 1, cap_l, D_MODEL), lambda b, e: (e, b, 0, 0)),
                   pl.BlockSpec((1, 1, cc, D_MODEL), lambda b, e: (e, b, 0, 0))],
        out_shape=[jax.ShapeDtypeStruct((E, B, cap_l, D_MODEL), BF16),
                   jax.ShapeDtypeStruct((E, B, cc, D_MODEL), BF16)],
        scratch_shapes=[pltpu.VMEM((cap_l, D_MODEL), F32)],
        name="gather_rows",
        compiler_params=_cparams(("arbitrary", "arbitrary")),
    )(posm, h2)


def _expert_kernel(*refs, row_chunk, n_sets):
    x_refs = refs[:n_sets]
    wg_ref, wu_ref, wd_ref = refs[n_sets:n_sets + 3]
    y_refs = refs[n_sets + 3:2 * n_sets + 3]
    acc_refs = refs[2 * n_sets + 3:]
    f = pl.program_id(1)
    nf = pl.num_programs(1)
    wg = wg_ref[0, 0].astype(BF16)
    wu = wu_ref[0, 0].astype(BF16)
    wd = wd_ref[0, 0].astype(BF16)
    for x_ref, y_ref, acc_ref in zip(x_refs, y_refs, acc_refs):
        rows = x_ref.shape[1]
        step = min(row_chunk, rows)
        for r in range(rows // step):
            rs = slice(r * step, (r + 1) * step)
            x = x_ref[0, rs, :]
            g = _dot(x, wg)
            u = _dot(x, wu)
            hid = (g * jax.nn.sigmoid(g) * u).astype(BF16)
            y = _dot(hid, wd)

            @pl.when(f == 0)
            def _():
                acc_ref[rs, :] = y

            @pl.when(f > 0)
            def _():
                acc_ref[rs, :] += y

        @pl.when(f == nf - 1)
        def _():
            y_ref[0] = acc_ref[...].astype(BF16)


def _expert_mlp(x_sets, w_gate, w_up, w_down, layer, row_chunk):
    E = N_EXPERTS
    nf = D_EXPERT // FF_TILE
    n_sets = len(x_sets)
    kern = functools.partial(_expert_kernel, row_chunk=row_chunk, n_sets=n_sets)
    xspec = lambda r: pl.BlockSpec((1, r, D_MODEL), lambda e, f: (e, 0, 0))
    return pl.pallas_call(
        kern,
        grid=(E, nf),
        in_specs=[xspec(xs.shape[1]) for xs in x_sets] + [
            pl.BlockSpec((1, 1, D_MODEL, FF_TILE), lambda e, f: (layer, e, 0, f)),
            pl.BlockSpec((1, 1, D_MODEL, FF_TILE), lambda e, f: (layer, e, 0, f)),
            pl.BlockSpec((1, 1, FF_TILE, D_MODEL), lambda e, f: (layer, e, f, 0))],
        out_specs=[xspec(xs.shape[1]) for xs in x_sets],
        out_shape=[jax.ShapeDtypeStruct(xs.shape, BF16) for xs in x_sets],
        scratch_shapes=[pltpu.VMEM(xs.shape[1:], F32) for xs in x_sets],
        name="expert_mlp",
        compiler_params=_cparams(("arbitrary", "arbitrary")),
    )(*x_sets, w_gate, w_up, w_down)


def _combine_kernel(x_ref, pos_ref, aff_ref, yl_ref, yc_ref, mod_ref, fg_ref, o_ref, acc_ref,
                    *, tile0, cap_l, cap_c, final):
    b = pl.program_id(0)
    i = pl.program_id(1) + tile0
    row = jnp.where(i == 0, 4, b)
    gate = mod_ref[pl.ds(row, 1), 5 * D_MODEL:6 * D_MODEL]
    pos = pos_ref[0]
    aff = aff_ref[0]
    acc_ref[...] = jnp.zeros_like(acc_ref)

    def add_from(y_of, ncol, off):
        lane = lax.broadcasted_iota(I32, (ROW_TILE, ncol), 1)
        for e in range(N_EXPERTS):
            pe = pos[:, e:e + 1]
            tgt = jnp.where(pe >= 0, pe + off, -1)
            onehot = jnp.where(tgt == lane, 1.0, 0.0).astype(BF16)
            acc_ref[...] += _dot(onehot, y_of(e)) * aff[:, e:e + 1]

    if tile0 == 0 and cap_c:
        @pl.when(i == 0)
        def _():
            add_from(lambda e: yc_ref[e], yc_ref.shape[1], b * cap_c)

        @pl.when(i > 0)
        def _():
            add_from(lambda e: yl_ref[e, 0], cap_l, 0)
    else:
        add_from(lambda e: yl_ref[e, 0], cap_l, 0)
    x2 = x_ref[0] + gate * acc_ref[...]
    if final:
        var = jnp.mean(x2 * x2, axis=-1, keepdims=True)
        x2 = x2 * lax.rsqrt(var + EPS) * fg_ref[...]
    o_ref[0] = x2


def _combine(x1, pos_t, aff_t, yl, yc, mod_l, final_g, cap_l, cap_c, final):
    B, S, _ = x1.shape
    E = N_EXPERTS
    tile0 = 1 if final else 0
    ntile = S // ROW_TILE - tile0
    rmap = lambda b, i: (b, i + tile0, 0)
    const2 = lambda b, i: (0, 0)
    kern = functools.partial(_combine_kernel, tile0=tile0, cap_l=cap_l, cap_c=cap_c, final=final)
    return pl.pallas_call(
        kern,
        grid=(B, ntile),
        in_specs=[
            pl.BlockSpec((1, ROW_TILE, D_MODEL), rmap),
            pl.BlockSpec((1, ROW_TILE, E), rmap),
            pl.BlockSpec((1, ROW_TILE, E), rmap),
            pl.BlockSpec((E, 1, cap_l, D_MODEL), lambda b, i: (0, b, 0, 0),
                         pipeline_mode=pl.Buffered(1)),
            pl.BlockSpec((E, yc.shape[1], D_MODEL), lambda b, i: (0, 0, 0),
                         pipeline_mode=pl.Buffered(1)),
            pl.BlockSpec((8, N_MOD * D_MODEL), const2),
            pl.BlockSpec((1, D_MODEL), const2),
        ],
        out_specs=pl.BlockSpec((1, ROW_TILE, D_MODEL), lambda b, i: (b, i, 0)),
        out_shape=jax.ShapeDtypeStruct((B, ntile * ROW_TILE, D_MODEL), F32),
        scratch_shapes=[pltpu.VMEM((ROW_TILE, D_MODEL), F32)],
        name="combine",
        compiler_params=_cparams(("arbitrary", "arbitrary")),
    )(x1, pos_t, aff_t, yl, yc, mod_l, final_g)


def _rope_tables(n_lat, ctx_len):
    t = np.arange(n_lat)
    pos = np.stack([t // GRID_W, t % GRID_W], axis=-1).astype(np.float32)
    inv = (ROPE_BASE ** (-np.arange(ROPE_FREQS, dtype=np.float32) / ROPE_FREQS)).astype(np.float32)
    ang = pos[:, :, None] * inv
    cos = np.cos(ang).astype(np.float32)
    sin = np.sin(ang).astype(np.float32)
    cos64 = np.concatenate([cos[:, 0], cos[:, 0], cos[:, 1], cos[:, 1]], axis=-1)
    sin64 = np.concatenate([-sin[:, 0], sin[:, 0], -sin[:, 1], sin[:, 1]], axis=-1)
    cos_t = np.concatenate([np.ones((ctx_len, 64), np.float32), cos64], axis=0)
    sin_t = np.concatenate([np.zeros((ctx_len, 64), np.float32), sin64], axis=0)
    return (jnp.asarray(np.tile(cos_t, (1, 2))), jnp.asarray(np.tile(sin_t, (1, 2))))


def kernel(x, c, ctx, c_ctx, ada_w, ada_b, norm1_g, norm2_g, w_in, attn_sink, hgrn_lb,
           hgrn_norm_g, conv_w, w_o, router_w, exp_w_gate, exp_w_up, exp_w_down, final_norm_g):
    B, T, D = x.shape
    L = ctx.shape[1]
    depth = ada_w.shape[0]
    assert D == D_MODEL and L == ROW_TILE and T % ROW_TILE == 0 and B <= 4
    S = L + T
    cap_l = EC_CAPACITY * T // N_EXPERTS
    cap_c = EC_CAPACITY * L // N_EXPERTS
    assert cap_l % 16 == 0 and cap_c % 16 == 0 and (B * cap_c) % 16 == 0

    cos_t, sin_t = _rope_tables(T, L)
    gamma = jax.nn.softmax(hgrn_lb.astype(F32), axis=0)
    lb_all = jnp.cumsum(gamma, axis=0) - gamma[0]
    cvec = jnp.concatenate([c, jnp.zeros((4 - B, D), F32), c_ctx[None],
                            jnp.zeros((3, D), F32)], axis=0)
    mod = _modulation(cvec, ada_w, ada_b)
    xs = jnp.concatenate([ctx, x], axis=1)

    def interleave_heads(w, axis):
        shp = w.shape
        w = w.reshape(shp[:axis] + (N_KV, GQA_GROUP, HEAD_DIM) + shp[axis + 1:])
        return jnp.swapaxes(w, axis, axis + 1).reshape(shp)

    for l in range(depth):
        last = l == depth - 1
        w_in_l = w_in[l].astype(BF16)
        w_in_l = jnp.concatenate([w_in_l[:, :1024], interleave_heads(w_in_l[:, 1024:1536], 1),
                                  w_in_l[:, 1536:]], axis=1)
        w_o_l = w_o[l].astype(BF16)
        w_o_l = jnp.concatenate([interleave_heads(w_o_l[:ATT_W], 0), w_o_l[ATT_W:]], axis=0)
        q, kv, hg, hqg, cv = _in_projection(xs, mod[l], norm1_g[l][None], w_in_l, cos_t, sin_t)
        att = _attention(q, kv, attn_sink[l], L, skip_ctx=last)
        o2 = _hgrn(hg, hqg, lb_all[l], L)
        x1, h2, aff = _mix_out(xs, att, o2, hqg, cv, mod[l], hgrn_norm_g[l][None], conv_w[l],
                               w_o_l, norm2_g[l][None], router_w[l].T, skip_ctx=last)
        cc = 0 if last else cap_c
        posm = _topk_positions(aff, ((0, L, cc), (L, T, cap_l)))
        xl_g, xc_g = _gather_rows(posm, h2, L, cap_l, cc)
        x_sets = [xl_g.reshape(N_EXPERTS, B * cap_l, D)]
        if not last:
            x_sets.append(xc_g.reshape(N_EXPERTS, B * xc_g.shape[2], D))
        ys = _expert_mlp(x_sets, exp_w_gate, exp_w_up, exp_w_down, l, cap_l)
        yl = ys[0].reshape(N_EXPERTS, B, cap_l, D)
        yc = jnp.zeros((N_EXPERTS, 16, D), BF16) if last else ys[1]
        xs = _combine(x1, jnp.swapaxes(posm, 1, 2), jnp.swapaxes(aff, 1, 2), yl, yc, mod[l],
                      final_norm_g[None], cap_l, cc, final=last)
    return xs
```

```python
import functools

import numpy as np
import jax
import jax.numpy as jnp
from jax import lax
from jax.experimental import pallas as pl
from jax.experimental.pallas import tpu as pltpu

F32 = jnp.float32
BF16 = jnp.bfloat16
I32 = jnp.int32
HIGHEST = lax.Precision.HIGHEST

D_MODEL = 1024
GRID_W = 64
EPS = 1e-6
LB_FLOOR = 1e-30
N_MOD = 6
ATT_W = 512
HG_W = 256
CV_W = 256
HEAD_DIM = 64
N_Q = 8
N_KV = 2
GQA_GROUP = 4
KV_W = 128
ROPE_BASE = 10000.0
ROPE_FREQS = 16
HG_HEADS = 4
N_EXPERTS = 16
EC_CAPACITY = 2
D_EXPERT = 2048
IN_COLS = 2816

ROW_TILE = 256
ATT_BLOCK = 128
HG_CHUNK = 128
HG_LEVELS = 7
HG_BATCH = 2
SLOT_BLOCK = 256
FF_TILE = 512
MOD_TILE = 1536
VMEM_LIMIT = 56 * 1024 * 1024

NEG_BIG = -1e30


def _cparams(sem):
    return pltpu.CompilerParams(dimension_semantics=sem, vmem_limit_bytes=VMEM_LIMIT)


def _dot(a, b):
    return jnp.dot(a, b, preferred_element_type=F32)


def _dot_nt(a, b):
    return lax.dot_general(a, b, (((1,), (1,)), ((), ())), preferred_element_type=F32)


def _mod_kernel(a_ref, w_ref, b_ref, o_ref):
    a = a_ref[...]
    a = a * jax.nn.sigmoid(a)
    o_ref[0] = jnp.dot(a, w_ref[0], precision=HIGHEST, preferred_element_type=F32) + b_ref[0]


def _modulation(cvec, ada_w, ada_b):
    depth = ada_w.shape[0]
    ncol = ada_w.shape[2]
    return pl.pallas_call(
        _mod_kernel,
        grid=(depth, ncol // MOD_TILE),
        in_specs=[
            pl.BlockSpec((8, D_MODEL), lambda l, j: (0, 0)),
            pl.BlockSpec((1, D_MODEL, MOD_TILE), lambda l, j: (l, 0, j)),
            pl.BlockSpec((1, 1, MOD_TILE), lambda l, j: (l, 0, j)),
        ],
        out_specs=pl.BlockSpec((1, 8, MOD_TILE), lambda l, j: (l, 0, j)),
        out_shape=jax.ShapeDtypeStruct((depth, 8, ncol), F32),
        name="modulation",
        compiler_params=_cparams(("arbitrary", "arbitrary")),
    )(cvec, ada_w, ada_b.reshape(depth, 1, ncol))


def _swap_halves(x):
    n = x.shape[-1]
    lane = lax.broadcasted_iota(I32, x.shape, x.ndim - 1)
    up = pltpu.roll(x, n - ROPE_FREQS, x.ndim - 1)
    dn = pltpu.roll(x, ROPE_FREQS, x.ndim - 1)
    return jnp.where((lane % (2 * ROPE_FREQS)) < ROPE_FREQS, up, dn)


def _rmsnorm_mod(x, g, shift, scale):
    var = jnp.mean(x * x, axis=-1, keepdims=True)
    y = x * lax.rsqrt(var + EPS) * g
    return y * (1.0 + scale) + shift


def _inproj_kernel(x_ref, mod_ref, g_ref, w_ref, cos_ref, sin_ref,
                   q_ref, kv_ref, hg_ref, hqg_ref, cv_ref):
    b = pl.program_id(0)
    i = pl.program_id(1)
    row = jnp.where(i == 0, 4, b)
    shift = mod_ref[pl.ds(row, 1), 0:D_MODEL]
    scale = mod_ref[pl.ds(row, 1), D_MODEL:2 * D_MODEL]
    h = _rmsnorm_mod(x_ref[0], g_ref[...], shift, scale)
    p = _dot(h.astype(BF16), w_ref[...])
    cos2 = cos_ref[...]
    sin2 = sin_ref[...]
    k = p[:, 0:KV_W]
    k = k * cos2 + _swap_halves(k) * sin2
    kv_ref[0, :, 0:KV_W] = k.astype(BF16)
    kv_ref[0, :, KV_W:2 * KV_W] = p[:, KV_W:2 * KV_W].astype(BF16)
    hg_ref[0] = p[:, 256:1024]
    q = p[:, 1024:1536]
    cos8 = jnp.concatenate([cos2] * 4, axis=1)
    sin8 = jnp.concatenate([sin2] * 4, axis=1)
    q = (q * cos8 + _swap_halves(q) * sin8) * (HEAD_DIM ** -0.5)
    q_ref[0] = q.astype(BF16)
    hqg_ref[0] = p[:, 1536:2048]
    cv_ref[0] = p[:, 2048:2816]


def _in_projection(xs, mod_l, g, w_bf16, cos_t, sin_t):
    B, S, _ = xs.shape
    nt = S // ROW_TILE
    row_map = lambda b, i: (b, i, 0)
    const2 = lambda b, i: (0, 0)
    return pl.pallas_call(
        _inproj_kernel,
        grid=(B, nt),
        in_specs=[
            pl.BlockSpec((1, ROW_TILE, D_MODEL), row_map),
            pl.BlockSpec((8, N_MOD * D_MODEL), const2),
            pl.BlockSpec((1, D_MODEL), const2),
            pl.BlockSpec((D_MODEL, IN_COLS), const2),
            pl.BlockSpec((ROW_TILE, 2 * HEAD_DIM), lambda b, i: (i, 0)),
            pl.BlockSpec((ROW_TILE, 2 * HEAD_DIM), lambda b, i: (i, 0)),
        ],
        out_specs=[
            pl.BlockSpec((1, ROW_TILE, ATT_W), row_map),
            pl.BlockSpec((1, ROW_TILE, 2 * KV_W), row_map),
            pl.BlockSpec((1, ROW_TILE, 3 * HG_W), row_map),
            pl.BlockSpec((1, ROW_TILE, 2 * HG_W), row_map),
            pl.BlockSpec((1, ROW_TILE, 3 * CV_W), row_map),
        ],
        out_shape=[
            jax.ShapeDtypeStruct((B, S, ATT_W), BF16),
            jax.ShapeDtypeStruct((B, S, 2 * KV_W), BF16),
            jax.ShapeDtypeStruct((B, S, 3 * HG_W), F32),
            jax.ShapeDtypeStruct((B, S, 2 * HG_W), F32),
            jax.ShapeDtypeStruct((B, S, 3 * CV_W), F32),
        ],
        name="in_projection",
        compiler_params=_cparams(("arbitrary", "arbitrary")),
    )(xs, mod_l, g, w_bf16, cos_t, sin_t)


def _attn_kernel(sink_ref, q_ref, kp_ref, kc_ref, kn_ref, kx_ref, o_ref, *, blk0, nblk, nctx):
    n = pl.program_id(1) + blk0
    is_lat = n >= nctx
    has_prev = n > nctx
    has_next = n < nblk - 1
    W = ATT_BLOCK
    q = q_ref[0]
    qrows = jnp.concatenate([q[:, g * W:(g + 1) * W] for g in range(GQA_GROUP)], axis=0)
    kv_all = jnp.concatenate([kp_ref[0], kc_ref[0], kn_ref[0], kx_ref[0]], axis=0)
    nkeys = kv_all.shape[0]
    k_all = kv_all[:, 0:KV_W]
    v_ext = jnp.concatenate([kv_all[:, KV_W:2 * KV_W], jnp.ones((nkeys, KV_W), BF16)], axis=1)
    rows = GQA_GROUP * W
    ri = lax.broadcasted_iota(I32, (rows, W), 0) % W
    cj = lax.broadcasted_iota(I32, (rows, W), 1)
    m_prev = (cj >= ri) & has_prev
    m_cur = jnp.broadcast_to(is_lat, (rows, W))
    m_next = (cj <= ri) & (has_next & is_lat)
    grp = lax.broadcasted_iota(I32, (rows, 1), 0) // W
    low = cj < HEAD_DIM
    outs = []
    for h in range(N_KV):
        qh = jnp.where(low if h == 0 else jnp.logical_not(low), qrows, jnp.zeros_like(qrows))
        sink = jnp.zeros((rows, 1), F32)
        for g in range(GQA_GROUP):
            sink = jnp.where(grp == g, sink_ref[h * GQA_GROUP + g], sink)
        s = _dot_nt(qh, k_all)
        segs = [jnp.where(m_prev, s[:, 0:W], NEG_BIG),
                jnp.where(m_cur, s[:, W:2 * W], NEG_BIG),
                jnp.where(m_next, s[:, 2 * W:3 * W], NEG_BIG)]
        segs += [s[:, c:c + W] for c in range(3 * W, nkeys, W)]
        mx = segs[0]
        for sg in segs[1:]:
            mx = jnp.maximum(mx, sg)
        m = jnp.maximum(jnp.max(mx, axis=1, keepdims=True), sink)
        p = jnp.concatenate([jnp.exp(sg - m).astype(BF16) for sg in segs], axis=1)
        oe = _dot(p, v_ext)
        den = oe[:, KV_W:2 * KV_W] + jnp.exp(sink - m)
        outs.append(oe[:, 0:KV_W] / den)
    o = jnp.where(low, outs[0], outs[1])
    for g in range(GQA_GROUP):
        o_ref[0, :, g * W:(g + 1) * W] = o[g * W:(g + 1) * W].astype(BF16)


def _attention(q, kv, sink, ctx_len, skip_ctx):
    B, S, _ = q.shape
    nblk = S // ATT_BLOCK
    nctx = ctx_len // ATT_BLOCK
    blk0 = nctx if skip_ctx else 0
    blk = lambda f: (lambda b, n: (b, f(n + blk0), 0))
    kern = functools.partial(_attn_kernel, blk0=blk0, nblk=nblk, nctx=nctx)
    return pl.pallas_call(
        kern,
        grid=(B, nblk - blk0),
        in_specs=[
            pl.BlockSpec(memory_space=pltpu.SMEM),
            pl.BlockSpec((1, ATT_BLOCK, ATT_W), blk(lambda n: n)),
            pl.BlockSpec((1, ATT_BLOCK, 2 * KV_W), blk(lambda n: jnp.maximum(n - 1, 0))),
            pl.BlockSpec((1, ATT_BLOCK, 2 * KV_W), blk(lambda n: n)),
            pl.BlockSpec((1, ATT_BLOCK, 2 * KV_W), blk(lambda n: jnp.minimum(n + 1, nblk - 1))),
            pl.BlockSpec((1, ctx_len, 2 * KV_W), lambda b, n: (b, 0, 0)),
        ],
        out_specs=pl.BlockSpec((1, ATT_BLOCK, ATT_W), lambda b, n: (b, n, 0)),
        out_shape=jax.ShapeDtypeStruct((B, S - blk0 * ATT_BLOCK, ATT_W), BF16),
        name="attention",
        compiler_params=_cparams(("arbitrary", "arbitrary")),
    )(sink, q, kv, kv, kv, kv)


def _hgrn_constants():
    C = HG_CHUNK
    t = np.arange(C)[:, None]
    r = np.arange(C)[None, :]
    tri = np.stack([r <= t, r >= t]).astype(np.float32)
    x = t ^ r
    lvl = np.where(x > 0, np.floor(np.log2(np.maximum(x, 1))).astype(np.int32), HG_LEVELS)
    lvl_f = np.where(t >= r, lvl, -1).astype(np.int32)
    lvl_b = np.where(t <= r, lvl, -1).astype(np.int32)
    lvl2 = np.stack([np.tile(lvl_f, (1, HG_HEADS)), np.tile(lvl_b, (1, HG_HEADS))])
    return tri, lvl2


def _span_row(x, span, row):
    C = x.shape[0]
    if span >= 8:
        x3 = x.reshape(C // span, span, x.shape[1])
        return jnp.broadcast_to(x3[:, row:row + 1, :], x3.shape).reshape(x.shape)
    pos = lax.broadcasted_iota(I32, x.shape, 0) % span
    out = x
    for p in range(span):
        if p != row:
            out = jnp.where(pos == p, pltpu.roll(x, (p - row) % C, 0), out)
    return out


def _hgrn_chunk(v, z, qr, lb, tri, masks, head_of_lane, same_head, st_ref, sidx, backward):
    C = HG_CHUNK
    logf = jnp.log(jnp.maximum(lb, LB_FLOOR) + (1.0 - lb) * jax.nn.sigmoid(z))
    k = (1.0 - lb) * jax.nn.sigmoid(-z)
    q = qr * jax.nn.sigmoid(qr)
    hi = logf.astype(BF16)
    lo = (logf - hi.astype(F32)).astype(BF16)
    cs = _dot(tri, jnp.concatenate([hi, lo], axis=1))
    lam = cs[:, 0:HG_W] + cs[:, HG_W:2 * HG_W]
    tot = lam[0:1] if backward else lam[C - 1:C]
    f_tot = jnp.exp(tot)

    def per_head_rows(x):
        zero = jnp.zeros_like(x)
        return jnp.concatenate([jnp.where(head_of_lane == h, x, zero) for h in range(HG_HEADS)],
                               axis=0)

    a = jnp.where(masks[HG_LEVELS], _dot_nt(q.astype(BF16), per_head_rows(k.astype(BF16))), 0.0)
    for l in range(HG_LEVELS):
        m = 1 << l
        ref = _span_row(lam, 2 * m, m if backward else m - 1)
        fac = jnp.exp(-jnp.abs(lam - ref))
        s_l = _dot_nt((q * fac).astype(BF16), per_head_rows((k * fac).astype(BF16)))
        a = jnp.where(masks[l], s_l, a)
    st = st_ref[sidx]
    q_in = (q * jnp.exp(lam)).astype(BF16)
    o = _dot(a.astype(BF16), per_head_rows(v.astype(BF16))) + _dot_nt(q_in, st.astype(BF16))
    k_out = (k * jnp.exp(tot - lam)).astype(BF16)
    upd = _dot(v.T.astype(BF16), k_out)
    st_ref[sidx] = st * f_tot + jnp.where(same_head, upd, 0.0)
    return o


def _hgrn_kernel(vf_ref, zf_ref, qf_ref, vb_ref, zb_ref, qb_ref, lb_ref, tri_ref, lvl_ref,
                 of_ref, ob_ref, st_ref):
    j = pl.program_id(1)

    @pl.when(j == 0)
    def _():
        st_ref[...] = jnp.zeros_like(st_ref)

    head_of_lane = lax.broadcasted_iota(I32, (HG_CHUNK, HG_W), 1) // HEAD_DIM
    same_head = (lax.broadcasted_iota(I32, (HG_W, HG_W), 0) // HEAD_DIM
                 == lax.broadcasted_iota(I32, (HG_W, HG_W), 1) // HEAD_DIM)
    for d, (v_ref, z_ref, q_ref, o_ref) in enumerate(
            ((vf_ref, zf_ref, qf_ref, of_ref), (vb_ref, zb_ref, qb_ref, ob_ref))):
        lvl = lvl_ref[d]
        masks = [lvl == l for l in range(HG_LEVELS + 1)]
        for s in range(v_ref.shape[0]):
            o_ref[s] = _hgrn_chunk(v_ref[s], z_ref[s], q_ref[s], lb_ref[d], tri_ref[d], masks,
                                   head_of_lane, same_head, st_ref, 2 * s + d, backward=(d == 1))


def _hgrn(hg, hqg, lb_l, ctx_len):
    B, S, _ = hg.shape
    nc = S // HG_CHUNK
    nctx = ctx_len // HG_CHUNK
    tri_np, lvl_np = _hgrn_constants()
    tri = jnp.asarray(tri_np, BF16)
    lvl = jnp.asarray(lvl_np, I32)

    def back(j):
        return jnp.where(j < nctx, nctx - 1 - j, nc - 1 - (j - nctx))

    bs = HG_BATCH if B % HG_BATCH == 0 else 1
    blk = (bs, HG_CHUNK, HG_W)
    fwd = lambda col: (lambda b, j: (b, j, col))
    bwd = lambda col: (lambda b, j: (b, back(j), col))
    const3 = lambda b, j: (0, 0, 0)
    return pl.pallas_call(
        _hgrn_kernel,
        grid=(B // bs, nc),
        in_specs=[
            pl.BlockSpec(blk, fwd(0)), pl.BlockSpec(blk, fwd(1)), pl.BlockSpec(blk, fwd(0)),
            pl.BlockSpec(blk, bwd(0)), pl.BlockSpec(blk, bwd(2)), pl.BlockSpec(blk, bwd(0)),
            pl.BlockSpec((2, 1, HG_W), const3),
            pl.BlockSpec((2, HG_CHUNK, HG_CHUNK), const3),
            pl.BlockSpec((2, HG_CHUNK, HG_HEADS * HG_CHUNK), const3),
        ],
        out_specs=[pl.BlockSpec(blk, fwd(0)), pl.BlockSpec(blk, bwd(0))],
        out_shape=[jax.ShapeDtypeStruct((B, S, HG_W), F32)] * 2,
        scratch_shapes=[pltpu.VMEM((2 * bs, HG_W, HG_W), F32)],
        name="hgrn_scan",
        compiler_params=_cparams(("arbitrary", "arbitrary")),
    )(hg, hg, hqg, hg, hg, hqg, lb_l.reshape(2, 1, HG_W), tri, lvl)


def _mixout_kernel(x_ref, att_ref, of_ref, ob_ref, g_ref, cv_ref, cprev_ref, cnext_ref,
                   mod_ref, gain_ref, cw_ref, wo_ref, n2_ref, wr_ref, ones_ref,
                   x1_ref, h2_ref, aff_ref, *, tile0, ntile):
    b = pl.program_id(0)
    i = pl.program_id(1) + tile0
    row = jnp.where(i == 0, 4, b)
    R = ROW_TILE

    def modv(c):
        return mod_ref[pl.ds(row, 1), c * D_MODEL:(c + 1) * D_MODEL]

    o = of_ref[0] + ob_ref[0]
    sq = o * o
    sq_hi = sq.astype(BF16)
    sq_lo = (sq - sq_hi.astype(F32)).astype(BF16)
    ms = _dot(sq_hi, ones_ref[...]) + _dot(sq_lo, ones_ref[...])
    g = g_ref[0]
    hg = o * lax.rsqrt(ms * (1.0 / HEAD_DIM) + EPS) * gain_ref[...] * (g * jax.nn.sigmoid(g))
    cv = cv_ref[0]
    u = cv[:, CV_W:2 * CV_W] * cv[:, 2 * CV_W:3 * CV_W]
    up = cprev_ref[0]
    un = cnext_ref[0]
    u_prev_row = up[7:8, CV_W:2 * CV_W] * up[7:8, 2 * CV_W:3 * CV_W]
    u_next_row = un[0:1, CV_W:2 * CV_W] * un[0:1, 2 * CV_W:3 * CV_W]
    u_prev_row = jnp.where(i <= 1, 0.0, u_prev_row)
    u_next_row = jnp.where((i == 0) | (i == ntile - 1), 0.0, u_next_row)
    ridx = lax.broadcasted_iota(I32, (R, CV_W), 0)
    u_m1 = jnp.where(ridx == 0, u_prev_row, pltpu.roll(u, 1, 0))
    u_p1 = jnp.where(ridx == R - 1, u_next_row, pltpu.roll(u, R - 1, 0))
    cw = cw_ref[...]
    conv = cv[:, 0:CV_W] * (u_m1 * cw[0:1] + u * cw[1:2] + u_p1 * cw[2:3])
    mix = (_dot(att_ref[0], wo_ref[0:ATT_W])
           + _dot(hg.astype(BF16), wo_ref[ATT_W:ATT_W + HG_W])
           + _dot(conv.astype(BF16), wo_ref[ATT_W + HG_W:D_MODEL]))
    x1 = x_ref[0] + modv(2) * mix
    x1_ref[0] = x1
    h2 = _rmsnorm_mod(x1, n2_ref[...], modv(3), modv(4))
    h_hi = h2.astype(BF16)
    h2_ref[0] = h_hi
    h_lo = (h2 - h_hi.astype(F32)).astype(BF16)
    wr = wr_ref[...]
    w_hi = wr.astype(BF16)
    w_lo = (wr - w_hi.astype(F32)).astype(BF16)
    r1 = _dot_nt(jnp.concatenate([w_hi, w_lo], axis=0), h_hi)
    logits = r1[0:N_EXPERTS] + r1[N_EXPERTS:2 * N_EXPERTS] + _dot_nt(w_hi, h_lo)
    e = jnp.exp(logits - jnp.max(logits, axis=0, keepdims=True))
    aff_ref[0] = e / jnp.sum(e, axis=0, keepdims=True)


def _mix_out(xs, att, o2, hqg, cv, mod_l, gain, conv_w, wo_bf16, n2g, wr_t, skip_ctx):
    B, S, _ = xs.shape
    ntile = S // ROW_TILE
    tile0 = 1 if skip_ctx else 0
    sub = ROW_TILE // 8
    nsub = S // 8
    rmap = lambda b, i: (b, i + tile0, 0)
    omap = lambda b, i: (b, i, 0)
    s_out = S - tile0 * ROW_TILE
    const2 = lambda b, i: (0, 0)
    ones = jnp.asarray(np.kron(np.eye(HG_HEADS), np.ones((HEAD_DIM, HEAD_DIM))), BF16)
    kern = functools.partial(_mixout_kernel, tile0=tile0, ntile=ntile)
    return pl.pallas_call(
        kern,
        grid=(B, ntile - tile0),
        in_specs=[
            pl.BlockSpec((1, ROW_TILE, D_MODEL), rmap),
            pl.BlockSpec((1, ROW_TILE, ATT_W), omap),
            pl.BlockSpec((1, ROW_TILE, HG_W), rmap),
            pl.BlockSpec((1, ROW_TILE, HG_W), rmap),
            pl.BlockSpec((1, ROW_TILE, HG_W), lambda b, i: (b, i + tile0, 1)),
            pl.BlockSpec((1, ROW_TILE, 3 * CV_W), rmap),
            pl.BlockSpec((1, 8, 3 * CV_W),
                         lambda b, i: (b, jnp.maximum((i + tile0) * sub - 1, 0), 0)),
            pl.BlockSpec((1, 8, 3 * CV_W),
                         lambda b, i: (b, jnp.minimum((i + tile0 + 1) * sub, nsub - 1), 0)),
            pl.BlockSpec((8, N_MOD * D_MODEL), const2),
            pl.BlockSpec((1, HG_W), const2),
            pl.BlockSpec((3, CV_W), const2),
            pl.BlockSpec((D_MODEL, D_MODEL), const2),
            pl.BlockSpec((1, D_MODEL), const2),
            pl.BlockSpec((N_EXPERTS, D_MODEL), const2),
            pl.BlockSpec((HG_W, HG_W), const2),
        ],
        out_specs=[
            pl.BlockSpec((1, ROW_TILE, D_MODEL), omap),
            pl.BlockSpec((1, ROW_TILE, D_MODEL), omap),
            pl.BlockSpec((1, N_EXPERTS, ROW_TILE), lambda b, i: (b, 0, i)),
        ],
        out_shape=[
            jax.ShapeDtypeStruct((B, s_out, D_MODEL), F32),
            jax.ShapeDtypeStruct((B, s_out, D_MODEL), BF16),
            jax.ShapeDtypeStruct((B, N_EXPERTS, s_out), F32),
        ],
        name="mix_out",
        compiler_params=_cparams(("arbitrary", "arbitrary")),
    )(xs, att, o2[0], o2[1], hqg, cv, cv, cv, mod_l, gain, conv_w, wo_bf16, n2g, wr_t, ones)


def _topk_kernel(aff_ref, tri_ref, pos_ref, cnt_ref, *, segments):
    tri = tri_ref[...]

    def excl_cumsum(mask, n):
        carry = jnp.zeros((N_EXPERTS, 1), F32)
        parts = []
        carries = []
        for c in range(n // 128):
            carries.append(carry)
            blk = jnp.where(mask[:, c * 128:(c + 1) * 128], 1.0, 0.0)
            parts.append(_dot(blk.astype(BF16), tri) + carry)
            carry = carry + jnp.sum(blk, axis=1, keepdims=True)
        return jnp.concatenate(parts, axis=1), carries + [carry]

    for (lo, n, k) in segments:
        a = aff_ref[0, :, lo:lo + n]

        def body(it, thr):
            cand = thr | jnp.left_shift(jnp.int32(1), 30 - it)
            cnt = jnp.sum(jnp.where(a >= pltpu.bitcast(cand, F32), 1.0, 0.0), axis=1, keepdims=True)
            return jnp.where(cnt >= k, cand, thr)

        thr = lax.fori_loop(0, 31, body, jnp.zeros((N_EXPERTS, 1), I32))
        above = a >= pltpu.bitcast(thr + 1, F32)
        tied = jnp.logical_and(a >= pltpu.bitcast(thr, F32), jnp.logical_not(above))
        n_above = jnp.sum(jnp.where(above, 1.0, 0.0), axis=1, keepdims=True)
        rank_tied, _ = excl_cumsum(tied, n)
        sel = above | (tied & (rank_tied < (k - n_above)))
        pos, counts = excl_cumsum(sel, n)
        pos_ref[0, :, lo:lo + n] = jnp.where(sel, pos.astype(I32), -1)
    step = ROW_TILE // 128
    lane = lax.broadcasted_iota(I32, (N_EXPERTS, 128), 1)
    cnt = jnp.zeros((N_EXPERTS, 128), F32)
    for c, col in enumerate(counts[0::step]):
        cnt = jnp.where(lane == c, col, cnt)
    cnt_ref[0] = cnt.astype(I32)


def _topk_positions(aff, segments):
    B, E, S = aff.shape
    assert segments[-1][1] // ROW_TILE + 1 <= 128
    tri = jnp.asarray(np.triu(np.ones((128, 128)), 1), BF16)
    kern = functools.partial(_topk_kernel, segments=segments)
    return pl.pallas_call(
        kern,
        grid=(B,),
        in_specs=[pl.BlockSpec((1, E, S), lambda b: (b, 0, 0)),
                  pl.BlockSpec((128, 128), lambda b: (0, 0))],
        out_specs=[pl.BlockSpec((1, E, S), lambda b: (b, 0, 0)),
                   pl.BlockSpec((1, E, 128), lambda b: (b, 0, 0))],
        out_shape=[jax.ShapeDtypeStruct((B, E, S), I32),
                   jax.ShapeDtypeStruct((B, E, 128), I32)],
        name="topk_positions",
        compiler_params=_cparams(("arbitrary",)),
    )(aff, tri)


def _slot_block(cap):
    return min(SLOT_BLOCK, cap)


def _gather_kernel(cnt_ref, pos_ref, aff_ref, h_ref, *refs, ctx_len, cap_l, cap_c):
    if cap_c:
        xl_ref, gl_ref, xc_ref, gc_ref, acc_ref, gacc_ref = refs
    else:
        xl_ref, gl_ref, acc_ref, gacc_ref = refs
    b = pl.program_id(0)
    e = pl.program_id(1)
    S = h_ref.shape[1]
    pos = pos_ref[0, pl.ds(e, 1), :]
    aff = aff_ref[0, pl.ds(e, 1), :]
    nchunk = (S - ctx_len) // ROW_TILE
    sbs = _slot_block(cap_l)
    acc_ref[...] = jnp.zeros_like(acc_ref)
    gacc_ref[...] = jnp.zeros_like(gacc_ref)
    for c in range(nchunk):
        lo = ctx_len + c * ROW_TILE
        c0 = cnt_ref[b, e, c]
        c1 = cnt_ref[b, e, c + 1]
        for sb in range(cap_l // sbs):
            def visit(lo=lo, sb=sb):
                slot = lax.broadcasted_iota(I32, (sbs, ROW_TILE), 0) + sb * sbs
                hit = pos[:, lo:lo + ROW_TILE] == slot
                rs = slice(sb * sbs, (sb + 1) * sbs)
                acc_ref[rs, :] += _dot(jnp.where(hit, 1.0, 0.0).astype(BF16),
                                       h_ref[0, lo:lo + ROW_TILE, :])
                gacc_ref[rs, :] += jnp.sum(jnp.where(hit, aff[:, lo:lo + ROW_TILE], 0.0),
                                           axis=1, keepdims=True)

            pl.when((c1 > sb * sbs) & (c0 < (sb + 1) * sbs))(visit)
    xl_ref[0, 0] = acc_ref[...].astype(BF16)
    gl_ref[0, 0] = gacc_ref[...]
    if cap_c:
        slot_c = lax.broadcasted_iota(I32, (cap_c, ctx_len), 0)
        hit = pos[:, 0:ctx_len] == slot_c
        xc_ref[0, 0] = _dot(jnp.where(hit, 1.0, 0.0).astype(BF16),
                            h_ref[0, 0:ctx_len, :]).astype(BF16)
        gc_ref[0, 0] = jnp.sum(jnp.where(hit, aff[:, 0:ctx_len], 0.0), axis=1, keepdims=True)


def _gather_rows(cnt, posm, aff, h2, ctx_len, cap_l, cap_c):
    B, E, S = posm.shape
    kern = functools.partial(_gather_kernel, ctx_len=ctx_len, cap_l=cap_l, cap_c=cap_c)
    omap = lambda b, e, cnt_ref: (e, b, 0, 0)
    caps = [cap_l] + ([cap_c] if cap_c else [])
    out_specs = []
    out_shape = []
    for cap in caps:
        out_specs += [pl.BlockSpec((1, 1, cap, D_MODEL), omap), pl.BlockSpec((1, 1, cap, 1), omap)]
        out_shape += [jax.ShapeDtypeStruct((E, B, cap, D_MODEL), BF16),
                      jax.ShapeDtypeStruct((E, B, cap, 1), F32)]
    grid_spec = pltpu.PrefetchScalarGridSpec(
        num_scalar_prefetch=1,
        grid=(B, E),
        in_specs=[pl.BlockSpec((1, E, S), lambda b, e, cnt_ref: (b, 0, 0)),
                  pl.BlockSpec((1, E, S), lambda b, e, cnt_ref: (b, 0, 0)),
                  pl.BlockSpec((1, S, D_MODEL), lambda b, e, cnt_ref: (b, 0, 0))],
        out_specs=out_specs,
        scratch_shapes=[pltpu.VMEM((cap_l, D_MODEL), F32), pltpu.VMEM((cap_l, 1), F32)],
    )
    return pl.pallas_call(
        kern,
        grid_spec=grid_spec,
        out_shape=out_shape,
        name="gather_rows",
        compiler_params=_cparams(("arbitrary", "arbitrary")),
    )(cnt, posm, aff, h2)


def _expert_kernel(*refs, row_chunk, n_sets):
    x_refs = refs[:n_sets]
    g_refs = refs[n_sets:2 * n_sets]
    wg_ref, wu_ref, wd_ref = refs[2 * n_sets:2 * n_sets + 3]
    y_refs = refs[2 * n_sets + 3:3 * n_sets + 3]
    acc_refs = refs[3 * n_sets + 3:]
    f = pl.program_id(1)
    nf = pl.num_programs(1)
    wg = wg_ref[0, 0].astype(BF16)
    wu = wu_ref[0, 0].astype(BF16)
    wd = wd_ref[0, 0].astype(BF16)
    for x_ref, g_ref, y_ref, acc_ref in zip(x_refs, g_refs, y_refs, acc_refs):
        rows = x_ref.shape[1]
        step = min(row_chunk, rows)
        for r in range(rows // step):
            rs = slice(r * step, (r + 1) * step)
            x = x_ref[0, rs, :]
            g = _dot(x, wg)
            u = _dot(x, wu)
            hid = (g * jax.nn.sigmoid(g) * u).astype(BF16)
            y = _dot(hid, wd)

            @pl.when(f == 0)
            def _():
                acc_ref[rs, :] = y

            @pl.when(f > 0)
            def _():
                acc_ref[rs, :] += y

        @pl.when(f == nf - 1)
        def _():
            y_ref[0] = (acc_ref[...] * g_ref[0]).astype(BF16)


def _expert_mlp(x_sets, g_sets, w_gate, w_up, w_down, layer, row_chunk):
    E = N_EXPERTS
    nf = D_EXPERT // FF_TILE
    n_sets = len(x_sets)
    kern = functools.partial(_expert_kernel, row_chunk=row_chunk, n_sets=n_sets)
    xspec = lambda r: pl.BlockSpec((1, r, D_MODEL), lambda e, f: (e, 0, 0))
    gspec = lambda r: pl.BlockSpec((1, r, 1), lambda e, f: (e, 0, 0))
    return pl.pallas_call(
        kern,
        grid=(E, nf),
        in_specs=[xspec(xs.shape[1]) for xs in x_sets] + [gspec(xs.shape[1]) for xs in x_sets] + [
            pl.BlockSpec((1, 1, D_MODEL, FF_TILE), lambda e, f: (layer, e, 0, f)),
            pl.BlockSpec((1, 1, D_MODEL, FF_TILE), lambda e, f: (layer, e, 0, f)),
            pl.BlockSpec((1, 1, FF_TILE, D_MODEL), lambda e, f: (layer, e, f, 0))],
        out_specs=[xspec(xs.shape[1]) for xs in x_sets],
        out_shape=[jax.ShapeDtypeStruct(xs.shape, BF16) for xs in x_sets],
        scratch_shapes=[pltpu.VMEM(xs.shape[1:], F32) for xs in x_sets],
        name="expert_mlp",
        compiler_params=_cparams(("arbitrary", "arbitrary")),
    )(*x_sets, *g_sets, w_gate, w_up, w_down)


def _combine_kernel(cnt_ref, x_ref, pos_ref, *refs, cap_l, cap_c, final):
    if cap_c:
        yl_ref, yc_ref, mod_ref, fg_ref, o_ref, acc_ref = refs
    else:
        yl_ref, mod_ref, fg_ref, o_ref, acc_ref = refs
    b = pl.program_id(0)
    i = pl.program_id(1)
    row = jnp.where(i == 0, 4, b) if cap_c else b
    gate = mod_ref[pl.ds(row, 1), 5 * D_MODEL:6 * D_MODEL]
    pos = pos_ref[0]
    acc_ref[...] = jnp.zeros_like(acc_ref)
    sbs = _slot_block(cap_l)

    def add_latent():
        t = i - 1 if cap_c else i
        for e in range(N_EXPERTS):
            c0 = cnt_ref[b, e, t]
            c1 = cnt_ref[b, e, t + 1]
            for sb in range(cap_l // sbs):
                def visit(e=e, sb=sb):
                    lane = lax.broadcasted_iota(I32, (ROW_TILE, sbs), 1) + sb * sbs
                    onehot = jnp.where(pos[:, e:e + 1] == lane, 1.0, 0.0).astype(BF16)
                    acc_ref[...] += _dot(onehot, yl_ref[e, 0, sb * sbs:(sb + 1) * sbs, :])

                pl.when((c1 > sb * sbs) & (c0 < (sb + 1) * sbs))(visit)

    def add_context():
        ncol = yc_ref.shape[1]
        lane = lax.broadcasted_iota(I32, (ROW_TILE, ncol), 1)
        for e in range(N_EXPERTS):
            pe = pos[:, e:e + 1]
            tgt = jnp.where(pe >= 0, pe + b * cap_c, -1)
            acc_ref[...] += _dot(jnp.where(tgt == lane, 1.0, 0.0).astype(BF16), yc_ref[e])

    if cap_c:
        pl.when(i == 0)(add_context)
        pl.when(i > 0)(add_latent)
    else:
        add_latent()
    x2 = x_ref[0] + gate * acc_ref[...]
    if final:
        var = jnp.mean(x2 * x2, axis=-1, keepdims=True)
        x2 = x2 * lax.rsqrt(var + EPS) * fg_ref[...]
    o_ref[0] = x2


def _combine(cnt, x1, pos_t, yl, yc, mod_l, final_g, cap_l, cap_c, final):
    B, S, _ = x1.shape
    E = N_EXPERTS
    rmap = lambda b, i, cnt_ref: (b, i, 0)
    const2 = lambda b, i, cnt_ref: (0, 0)
    kern = functools.partial(_combine_kernel, cap_l=cap_l, cap_c=cap_c, final=final)
    y_specs = [pl.BlockSpec((E, 1, cap_l, D_MODEL), lambda b, i, cnt_ref: (0, b, 0, 0),
                            pipeline_mode=pl.Buffered(1))]
    y_args = [yl]
    if cap_c:
        y_specs.append(pl.BlockSpec(yc.shape, lambda b, i, cnt_ref: (0, 0, 0),
                                    pipeline_mode=pl.Buffered(1)))
        y_args.append(yc)
    grid_spec = pltpu.PrefetchScalarGridSpec(
        num_scalar_prefetch=1,
        grid=(B, S // ROW_TILE),
        in_specs=[pl.BlockSpec((1, ROW_TILE, D_MODEL), rmap),
                  pl.BlockSpec((1, ROW_TILE, E), rmap)] + y_specs + [
                  pl.BlockSpec((8, N_MOD * D_MODEL), const2),
                  pl.BlockSpec((1, D_MODEL), const2)],
        out_specs=pl.BlockSpec((1, ROW_TILE, D_MODEL), rmap),
        scratch_shapes=[pltpu.VMEM((ROW_TILE, D_MODEL), F32)],
    )
    return pl.pallas_call(
        kern,
        grid_spec=grid_spec,
        out_shape=jax.ShapeDtypeStruct((B, S, D_MODEL), F32),
        name="combine",
        compiler_params=_cparams(("arbitrary", "arbitrary")),
    )(cnt, x1, pos_t, *y_args, mod_l, final_g)


def _rope_tables(n_lat, ctx_len):
    t = np.arange(n_lat)
    pos = np.stack([t // GRID_W, t % GRID_W], axis=-1).astype(np.float32)
    inv = (ROPE_BASE ** (-np.arange(ROPE_FREQS, dtype=np.float32) / ROPE_FREQS)).astype(np.float32)
    ang = pos[:, :, None] * inv
    cos = np.cos(ang).astype(np.float32)
    sin = np.sin(ang).astype(np.float32)
    cos64 = np.concatenate([cos[:, 0], cos[:, 0], cos[:, 1], cos[:, 1]], axis=-1)
    sin64 = np.concatenate([-sin[:, 0], sin[:, 0], -sin[:, 1], sin[:, 1]], axis=-1)
    cos_t = np.concatenate([np.ones((ctx_len, 64), np.float32), cos64], axis=0)
    sin_t = np.concatenate([np.zeros((ctx_len, 64), np.float32), sin64], axis=0)
    return (jnp.asarray(np.tile(cos_t, (1, 2))), jnp.asarray(np.tile(sin_t, (1, 2))))


def kernel(x, c, ctx, c_ctx, ada_w, ada_b, norm1_g, norm2_g, w_in, attn_sink, hgrn_lb,
           hgrn_norm_g, conv_w, w_o, router_w, exp_w_gate, exp_w_up, exp_w_down, final_norm_g):
    B, T, D = x.shape
    L = ctx.shape[1]
    depth = ada_w.shape[0]
    assert D == D_MODEL and L == ROW_TILE and T % ROW_TILE == 0 and B <= 4
    S = L + T
    cap_l = EC_CAPACITY * T // N_EXPERTS
    cap_c = EC_CAPACITY * L // N_EXPERTS
    assert cap_l % 16 == 0 and cap_c % 16 == 0 and (B * cap_c) % 16 == 0

    cos_t, sin_t = _rope_tables(T, L)
    gamma = jax.nn.softmax(hgrn_lb.astype(F32), axis=0)
    lb_all = jnp.cumsum(gamma, axis=0) - gamma[0]
    cvec = jnp.concatenate([c, jnp.zeros((4 - B, D), F32), c_ctx[None],
                            jnp.zeros((3, D), F32)], axis=0)
    mod = _modulation(cvec, ada_w, ada_b)
    xs = jnp.concatenate([ctx, x], axis=1)

    def interleave_heads(w, axis):
        shp = w.shape
        w = w.reshape(shp[:axis] + (N_KV, GQA_GROUP, HEAD_DIM) + shp[axis + 1:])
        return jnp.swapaxes(w, axis, axis + 1).reshape(shp)

    for l in range(depth):
        last = l == depth - 1
        w_in_l = w_in[l].astype(BF16)
        w_in_l = jnp.concatenate([w_in_l[:, :1024], interleave_heads(w_in_l[:, 1024:1536], 1),
                                  w_in_l[:, 1536:]], axis=1)
        w_o_l = w_o[l].astype(BF16)
        w_o_l = jnp.concatenate([interleave_heads(w_o_l[:ATT_W], 0), w_o_l[ATT_W:]], axis=0)
        q, kv, hg, hqg, cv = _in_projection(xs, mod[l], norm1_g[l][None], w_in_l, cos_t, sin_t)
        att = _attention(q, kv, attn_sink[l], L, skip_ctx=last)
        o2 = _hgrn(hg, hqg, lb_all[l], L)
        x1, h2, aff = _mix_out(xs, att, o2, hqg, cv, mod[l], hgrn_norm_g[l][None], conv_w[l],
                               w_o_l, norm2_g[l][None], router_w[l].T, skip_ctx=last)
        lm, cc = (0, 0) if last else (L, cap_c)
        segments = ((lm, T, cap_l),) if last else ((0, L, cap_c), (L, T, cap_l))
        posm, cnt = _topk_positions(aff, segments)
        gathered = _gather_rows(cnt, posm, aff, h2, lm, cap_l, cc)
        x_sets = [a.reshape(N_EXPERTS, -1, D) for a in gathered[0::2]]
        g_sets = [a.reshape(N_EXPERTS, -1, 1) for a in gathered[1::2]]
        ys = _expert_mlp(x_sets, g_sets, exp_w_gate, exp_w_up, exp_w_down, l, cap_l)
        yl = ys[0].reshape(N_EXPERTS, B, cap_l, D)
        xs = _combine(cnt, x1, jnp.swapaxes(posm, 1, 2), yl, None if last else ys[1], mod[l],
                      final_norm_g[None], cap_l, cc, final=last)
    return xs
```

```python
import functools

import numpy as np
import jax
import jax.numpy as jnp
from jax import lax
from jax.experimental import pallas as pl
from jax.experimental.pallas import tpu as pltpu

F32 = jnp.float32
BF16 = jnp.bfloat16
I32 = jnp.int32
HIGHEST = lax.Precision.HIGHEST

D_MODEL = 1024
GRID_W = 64
EPS = 1e-6
LB_FLOOR = 1e-30
N_MOD = 6
ATT_W = 512
HG_W = 256
CV_W = 256
HEAD_DIM = 64
N_Q = 8
N_KV = 2
GQA_GROUP = 4
KV_W = 128
ROPE_BASE = 10000.0
ROPE_FREQS = 16
HG_HEADS = 4
N_EXPERTS = 16
EC_CAPACITY = 2
D_EXPERT = 2048
IN_COLS = 2816

ROW_TILE = 256
ATT_BLOCK = 128
HG_CHUNK = 128
HG_LEVELS = 7
HG_BATCH = 2
SLOT_BLOCK = 256
FF_TILE = 512
MOD_TILE = 1536
VMEM_LIMIT = 56 * 1024 * 1024

NEG_BIG = -1e30


def _cparams(sem):
    return pltpu.CompilerParams(dimension_semantics=sem, vmem_limit_bytes=VMEM_LIMIT)


def _dot(a, b):
    return jnp.dot(a, b, preferred_element_type=F32)


def _dot_nt(a, b):
    return lax.dot_general(a, b, (((1,), (1,)), ((), ())), preferred_element_type=F32)


def _mod_kernel(a_ref, w_ref, b_ref, o_ref):
    a = a_ref[...]
    a = a * jax.nn.sigmoid(a)
    o_ref[0] = jnp.dot(a, w_ref[0], precision=HIGHEST, preferred_element_type=F32) + b_ref[0]


def _modulation(cvec, ada_w, ada_b):
    depth = ada_w.shape[0]
    ncol = ada_w.shape[2]
    return pl.pallas_call(
        _mod_kernel,
        grid=(depth, ncol // MOD_TILE),
        in_specs=[
            pl.BlockSpec((8, D_MODEL), lambda l, j: (0, 0)),
            pl.BlockSpec((1, D_MODEL, MOD_TILE), lambda l, j: (l, 0, j)),
            pl.BlockSpec((1, 1, MOD_TILE), lambda l, j: (l, 0, j)),
        ],
        out_specs=pl.BlockSpec((1, 8, MOD_TILE), lambda l, j: (l, 0, j)),
        out_shape=jax.ShapeDtypeStruct((depth, 8, ncol), F32),
        name="modulation",
        compiler_params=_cparams(("arbitrary", "arbitrary")),
    )(cvec, ada_w, ada_b.reshape(depth, 1, ncol))


def _swap_halves(x):
    n = x.shape[-1]
    lane = lax.broadcasted_iota(I32, x.shape, x.ndim - 1)
    up = pltpu.roll(x, n - ROPE_FREQS, x.ndim - 1)
    dn = pltpu.roll(x, ROPE_FREQS, x.ndim - 1)
    return jnp.where((lane % (2 * ROPE_FREQS)) < ROPE_FREQS, up, dn)


def _rmsnorm_mod(x, g, shift, scale):
    var = jnp.mean(x * x, axis=-1, keepdims=True)
    y = x * lax.rsqrt(var + EPS) * g
    return y * (1.0 + scale) + shift


def _inproj_kernel(x_ref, mod_ref, g_ref, w_ref, cos_ref, sin_ref,
                   q_ref, kv_ref, hg_ref, hqg_ref, cv_ref):
    b = pl.program_id(0)
    i = pl.program_id(1)
    row = jnp.where(i == 0, 4, b)
    shift = mod_ref[pl.ds(row, 1), 0:D_MODEL]
    scale = mod_ref[pl.ds(row, 1), D_MODEL:2 * D_MODEL]
    h = _rmsnorm_mod(x_ref[0], g_ref[...], shift, scale)
    p = _dot(h.astype(BF16), w_ref[...])
    cos2 = cos_ref[...]
    sin2 = sin_ref[...]
    k = p[:, 0:KV_W]
    k = k * cos2 + _swap_halves(k) * sin2
    kv_ref[0, :, 0:KV_W] = k.astype(BF16)
    kv_ref[0, :, KV_W:2 * KV_W] = p[:, KV_W:2 * KV_W].astype(BF16)
    hg_ref[0] = p[:, 256:1024]
    q = p[:, 1024:1536]
    cos8 = jnp.concatenate([cos2] * 4, axis=1)
    sin8 = jnp.concatenate([sin2] * 4, axis=1)
    q = (q * cos8 + _swap_halves(q) * sin8) * (HEAD_DIM ** -0.5)
    q_ref[0] = q.astype(BF16)
    hqg_ref[0] = p[:, 1536:2048]
    cv_ref[0] = p[:, 2048:2816]


def _in_projection(xs, mod_l, g, w_bf16, cos_t, sin_t):
    B, S, _ = xs.shape
    nt = S // ROW_TILE
    row_map = lambda b, i: (b, i, 0)
    const2 = lambda b, i: (0, 0)
    return pl.pallas_call(
        _inproj_kernel,
        grid=(B, nt),
        in_specs=[
            pl.BlockSpec((1, ROW_TILE, D_MODEL), row_map),
            pl.BlockSpec((8, N_MOD * D_MODEL), const2),
            pl.BlockSpec((1, D_MODEL), const2),
            pl.BlockSpec((D_MODEL, IN_COLS), const2),
            pl.BlockSpec((ROW_TILE, 2 * HEAD_DIM), lambda b, i: (i, 0)),
            pl.BlockSpec((ROW_TILE, 2 * HEAD_DIM), lambda b, i: (i, 0)),
        ],
        out_specs=[
            pl.BlockSpec((1, ROW_TILE, ATT_W), row_map),
            pl.BlockSpec((1, ROW_TILE, 2 * KV_W), row_map),
            pl.BlockSpec((1, ROW_TILE, 3 * HG_W), row_map),
            pl.BlockSpec((1, ROW_TILE, 2 * HG_W), row_map),
            pl.BlockSpec((1, ROW_TILE, 3 * CV_W), row_map),
        ],
        out_shape=[
            jax.ShapeDtypeStruct((B, S, ATT_W), BF16),
            jax.ShapeDtypeStruct((B, S, 2 * KV_W), BF16),
            jax.ShapeDtypeStruct((B, S, 3 * HG_W), F32),
            jax.ShapeDtypeStruct((B, S, 2 * HG_W), F32),
            jax.ShapeDtypeStruct((B, S, 3 * CV_W), F32),
        ],
        name="in_projection",
        compiler_params=_cparams(("arbitrary", "arbitrary")),
    )(xs, mod_l, g, w_bf16, cos_t, sin_t)


def _attn_kernel(sink_ref, q_ref, kp_ref, kc_ref, kn_ref, kx_ref, o_ref, *, blk0, nblk, nctx):
    n = pl.program_id(1) + blk0
    is_lat = n >= nctx
    has_prev = n > nctx
    has_next = n < nblk - 1
    W = ATT_BLOCK
    q = q_ref[0]
    qrows = jnp.concatenate([q[:, g * W:(g + 1) * W] for g in range(GQA_GROUP)], axis=0)
    kv_all = jnp.concatenate([kp_ref[0], kc_ref[0], kn_ref[0], kx_ref[0]], axis=0)
    nkeys = kv_all.shape[0]
    k_all = kv_all[:, 0:KV_W]
    v_ext = jnp.concatenate([kv_all[:, KV_W:2 * KV_W], jnp.ones((nkeys, KV_W), BF16)], axis=1)
    rows = GQA_GROUP * W
    ri = lax.broadcasted_iota(I32, (rows, W), 0) % W
    cj = lax.broadcasted_iota(I32, (rows, W), 1)
    m_prev = (cj >= ri) & has_prev
    m_cur = jnp.broadcast_to(is_lat, (rows, W))
    m_next = (cj <= ri) & (has_next & is_lat)
    grp = lax.broadcasted_iota(I32, (rows, 1), 0) // W
    low = cj < HEAD_DIM
    outs = []
    for h in range(N_KV):
        qh = jnp.where(low if h == 0 else jnp.logical_not(low), qrows, jnp.zeros_like(qrows))
        sink = jnp.zeros((rows, 1), F32)
        for g in range(GQA_GROUP):
            sink = jnp.where(grp == g, sink_ref[h * GQA_GROUP + g], sink)
        s = _dot_nt(qh, k_all)
        segs = [jnp.where(m_prev, s[:, 0:W], NEG_BIG),
                jnp.where(m_cur, s[:, W:2 * W], NEG_BIG),
                jnp.where(m_next, s[:, 2 * W:3 * W], NEG_BIG)]
        segs += [s[:, c:c + W] for c in range(3 * W, nkeys, W)]
        mx = segs[0]
        for sg in segs[1:]:
            mx = jnp.maximum(mx, sg)
        m = jnp.maximum(jnp.max(mx, axis=1, keepdims=True), sink)
        p = jnp.concatenate([jnp.exp(sg - m).astype(BF16) for sg in segs], axis=1)
        oe = _dot(p, v_ext)
        den = oe[:, KV_W:2 * KV_W] + jnp.exp(sink - m)
        outs.append(oe[:, 0:KV_W] / den)
    o = jnp.where(low, outs[0], outs[1])
    for g in range(GQA_GROUP):
        o_ref[0, :, g * W:(g + 1) * W] = o[g * W:(g + 1) * W].astype(BF16)


def _attention(q, kv, sink, ctx_len, skip_ctx):
    B, S, _ = q.shape
    nblk = S // ATT_BLOCK
    nctx = ctx_len // ATT_BLOCK
    blk0 = nctx if skip_ctx else 0
    blk = lambda f: (lambda b, n: (b, f(n + blk0), 0))
    kern = functools.partial(_attn_kernel, blk0=blk0, nblk=nblk, nctx=nctx)
    return pl.pallas_call(
        kern,
        grid=(B, nblk - blk0),
        in_specs=[
            pl.BlockSpec(memory_space=pltpu.SMEM),
            pl.BlockSpec((1, ATT_BLOCK, ATT_W), blk(lambda n: n)),
            pl.BlockSpec((1, ATT_BLOCK, 2 * KV_W), blk(lambda n: jnp.maximum(n - 1, 0))),
            pl.BlockSpec((1, ATT_BLOCK, 2 * KV_W), blk(lambda n: n)),
            pl.BlockSpec((1, ATT_BLOCK, 2 * KV_W), blk(lambda n: jnp.minimum(n + 1, nblk - 1))),
            pl.BlockSpec((1, ctx_len, 2 * KV_W), lambda b, n: (b, 0, 0)),
        ],
        out_specs=pl.BlockSpec((1, ATT_BLOCK, ATT_W), lambda b, n: (b, n, 0)),
        out_shape=jax.ShapeDtypeStruct((B, S - blk0 * ATT_BLOCK, ATT_W), BF16),
        name="attention",
        compiler_params=_cparams(("arbitrary", "arbitrary")),
    )(sink, q, kv, kv, kv, kv)


def _hgrn_constants():
    C = HG_CHUNK
    t = np.arange(C)[:, None]
    r = np.arange(C)[None, :]
    tri = np.stack([r <= t, r >= t]).astype(np.float32)
    x = t ^ r
    lvl = np.where(x > 0, np.floor(np.log2(np.maximum(x, 1))).astype(np.int32), HG_LEVELS)
    lvl_f = np.where(t >= r, lvl, -1).astype(np.int32)
    lvl_b = np.where(t <= r, lvl, -1).astype(np.int32)
    lvl2 = np.stack([np.tile(lvl_f, (1, HG_HEADS)), np.tile(lvl_b, (1, HG_HEADS))])
    return tri, lvl2


def _span_row(x, span, row):
    C = x.shape[0]
    if span >= 8:
        x3 = x.reshape(C // span, span, x.shape[1])
        return jnp.broadcast_to(x3[:, row:row + 1, :], x3.shape).reshape(x.shape)
    pos = lax.broadcasted_iota(I32, x.shape, 0) % span
    out = x
    for p in range(span):
        if p != row:
            out = jnp.where(pos == p, pltpu.roll(x, (p - row) % C, 0), out)
    return out


def _hgrn_chunk(v, z, qr, lb, tri, masks, head_of_lane, same_head, st_ref, sidx, backward):
    C = HG_CHUNK
    logf = jnp.log(jnp.maximum(lb, LB_FLOOR) + (1.0 - lb) * jax.nn.sigmoid(z))
    k = (1.0 - lb) * jax.nn.sigmoid(-z)
    q = qr * jax.nn.sigmoid(qr)
    hi = logf.astype(BF16)
    lo = (logf - hi.astype(F32)).astype(BF16)
    cs = _dot(tri, jnp.concatenate([hi, lo], axis=1))
    lam = cs[:, 0:HG_W] + cs[:, HG_W:2 * HG_W]
    tot = lam[0:1] if backward else lam[C - 1:C]
    f_tot = jnp.exp(tot)

    def per_head_rows(x):
        zero = jnp.zeros_like(x)
        return jnp.concatenate([jnp.where(head_of_lane == h, x, zero) for h in range(HG_HEADS)],
                               axis=0)

    a = jnp.where(masks[HG_LEVELS], _dot_nt(q.astype(BF16), per_head_rows(k.astype(BF16))), 0.0)
    for l in range(HG_LEVELS):
        m = 1 << l
        ref = _span_row(lam, 2 * m, m if backward else m - 1)
        fac = jnp.exp(-jnp.abs(lam - ref))
        s_l = _dot_nt((q * fac).astype(BF16), per_head_rows((k * fac).astype(BF16)))
        a = jnp.where(masks[l], s_l, a)
    st = st_ref[sidx]
    q_in = (q * jnp.exp(lam)).astype(BF16)
    o = _dot(a.astype(BF16), per_head_rows(v.astype(BF16))) + _dot_nt(q_in, st.astype(BF16))
    k_out = (k * jnp.exp(tot - lam)).astype(BF16)
    upd = _dot(v.T.astype(BF16), k_out)
    st_ref[sidx] = st * f_tot + jnp.where(same_head, upd, 0.0)
    return o


def _hgrn_kernel(vf_ref, zf_ref, qf_ref, vb_ref, zb_ref, qb_ref, lb_ref, tri_ref, lvl_ref,
                 of_ref, ob_ref, st_ref):
    j = pl.program_id(1)

    @pl.when(j == 0)
    def _():
        st_ref[...] = jnp.zeros_like(st_ref)

    head_of_lane = lax.broadcasted_iota(I32, (HG_CHUNK, HG_W), 1) // HEAD_DIM
    same_head = (lax.broadcasted_iota(I32, (HG_W, HG_W), 0) // HEAD_DIM
                 == lax.broadcasted_iota(I32, (HG_W, HG_W), 1) // HEAD_DIM)
    for d, (v_ref, z_ref, q_ref, o_ref) in enumerate(
            ((vf_ref, zf_ref, qf_ref, of_ref), (vb_ref, zb_ref, qb_ref, ob_ref))):
        lvl = lvl_ref[d]
        masks = [lvl == l for l in range(HG_LEVELS + 1)]
        for s in range(v_ref.shape[0]):
            o_ref[s] = _hgrn_chunk(v_ref[s], z_ref[s], q_ref[s], lb_ref[d], tri_ref[d], masks,
                                   head_of_lane, same_head, st_ref, 2 * s + d, backward=(d == 1))


def _hgrn(hg, hqg, lb_l, ctx_len):
    B, S, _ = hg.shape
    nc = S // HG_CHUNK
    nctx = ctx_len // HG_CHUNK
    tri_np, lvl_np = _hgrn_constants()
    tri = jnp.asarray(tri_np, BF16)
    lvl = jnp.asarray(lvl_np, I32)

    def back(j):
        return jnp.where(j < nctx, nctx - 1 - j, nc - 1 - (j - nctx))

    bs = HG_BATCH if B % HG_BATCH == 0 else 1
    blk = (bs, HG_CHUNK, HG_W)
    fwd = lambda col: (lambda b, j: (b, j, col))
    bwd = lambda col: (lambda b, j: (b, back(j), col))
    const3 = lambda b, j: (0, 0, 0)
    return pl.pallas_call(
        _hgrn_kernel,
        grid=(B // bs, nc),
        in_specs=[
            pl.BlockSpec(blk, fwd(0)), pl.BlockSpec(blk, fwd(1)), pl.BlockSpec(blk, fwd(0)),
            pl.BlockSpec(blk, bwd(0)), pl.BlockSpec(blk, bwd(2)), pl.BlockSpec(blk, bwd(0)),
            pl.BlockSpec((2, 1, HG_W), const3),
            pl.BlockSpec((2, HG_CHUNK, HG_CHUNK), const3),
            pl.BlockSpec((2, HG_CHUNK, HG_HEADS * HG_CHUNK), const3),
        ],
        out_specs=[pl.BlockSpec(blk, fwd(0)), pl.BlockSpec(blk, bwd(0))],
        out_shape=[jax.ShapeDtypeStruct((B, S, HG_W), F32)] * 2,
        scratch_shapes=[pltpu.VMEM((2 * bs, HG_W, HG_W), F32)],
        name="hgrn_scan",
        compiler_params=_cparams(("arbitrary", "arbitrary")),
    )(hg, hg, hqg, hg, hg, hqg, lb_l.reshape(2, 1, HG_W), tri, lvl)


def _mixout_kernel(x_ref, att_ref, of_ref, ob_ref, g_ref, cv_ref, cprev_ref, cnext_ref,
                   mod_ref, gain_ref, cw_ref, wo_ref, n2_ref, wr_ref, ones_ref,
                   x1_ref, h2_ref, aff_ref, *, tile0, ntile):
    b = pl.program_id(0)
    i = pl.program_id(1) + tile0
    row = jnp.where(i == 0, 4, b)
    R = ROW_TILE

    def modv(c):
        return mod_ref[pl.ds(row, 1), c * D_MODEL:(c + 1) * D_MODEL]

    o = of_ref[0] + ob_ref[0]
    sq = o * o
    sq_hi = sq.astype(BF16)
    sq_lo = (sq - sq_hi.astype(F32)).astype(BF16)
    ms = _dot(sq_hi, ones_ref[...]) + _dot(sq_lo, ones_ref[...])
    g = g_ref[0]
    hg = o * lax.rsqrt(ms * (1.0 / HEAD_DIM) + EPS) * gain_ref[...] * (g * jax.nn.sigmoid(g))
    cv = cv_ref[0]
    u = cv[:, CV_W:2 * CV_W] * cv[:, 2 * CV_W:3 * CV_W]
    up = cprev_ref[0]
    un = cnext_ref[0]
    u_prev_row = up[7:8, CV_W:2 * CV_W] * up[7:8, 2 * CV_W:3 * CV_W]
    u_next_row = un[0:1, CV_W:2 * CV_W] * un[0:1, 2 * CV_W:3 * CV_W]
    u_prev_row = jnp.where(i <= 1, 0.0, u_prev_row)
    u_next_row = jnp.where((i == 0) | (i == ntile - 1), 0.0, u_next_row)
    ridx = lax.broadcasted_iota(I32, (R, CV_W), 0)
    u_m1 = jnp.where(ridx == 0, u_prev_row, pltpu.roll(u, 1, 0))
    u_p1 = jnp.where(ridx == R - 1, u_next_row, pltpu.roll(u, R - 1, 0))
    cw = cw_ref[...]
    conv = cv[:, 0:CV_W] * (u_m1 * cw[0:1] + u * cw[1:2] + u_p1 * cw[2:3])
    mix = (_dot(att_ref[0], wo_ref[0:ATT_W])
           + _dot(hg.astype(BF16), wo_ref[ATT_W:ATT_W + HG_W])
           + _dot(conv.astype(BF16), wo_ref[ATT_W + HG_W:D_MODEL]))
    x1 = x_ref[0] + modv(2) * mix
    x1_ref[0] = x1
    h2 = _rmsnorm_mod(x1, n2_ref[...], modv(3), modv(4))
    h_hi = h2.astype(BF16)
    h2_ref[0] = h_hi
    h_lo = (h2 - h_hi.astype(F32)).astype(BF16)
    wr = wr_ref[...]
    w_hi = wr.astype(BF16)
    w_lo = (wr - w_hi.astype(F32)).astype(BF16)
    r1 = _dot_nt(jnp.concatenate([w_hi, w_lo], axis=0), h_hi)
    logits = r1[0:N_EXPERTS] + r1[N_EXPERTS:2 * N_EXPERTS] + _dot_nt(w_hi, h_lo)
    e = jnp.exp(logits - jnp.max(logits, axis=0, keepdims=True))
    aff_ref[0] = e / jnp.sum(e, axis=0, keepdims=True)


def _mix_out(xs, att, o2, hqg, cv, mod_l, gain, conv_w, wo_bf16, n2g, wr_t, skip_ctx):
    B, S, _ = xs.shape
    ntile = S // ROW_TILE
    tile0 = 1 if skip_ctx else 0
    sub = ROW_TILE // 8
    nsub = S // 8
    rmap = lambda b, i: (b, i + tile0, 0)
    omap = lambda b, i: (b, i, 0)
    s_out = S - tile0 * ROW_TILE
    const2 = lambda b, i: (0, 0)
    ones = jnp.asarray(np.kron(np.eye(HG_HEADS), np.ones((HEAD_DIM, HEAD_DIM))), BF16)
    kern = functools.partial(_mixout_kernel, tile0=tile0, ntile=ntile)
    return pl.pallas_call(
        kern,
        grid=(B, ntile - tile0),
        in_specs=[
            pl.BlockSpec((1, ROW_TILE, D_MODEL), rmap),
            pl.BlockSpec((1, ROW_TILE, ATT_W), omap),
            pl.BlockSpec((1, ROW_TILE, HG_W), rmap),
            pl.BlockSpec((1, ROW_TILE, HG_W), rmap),
            pl.BlockSpec((1, ROW_TILE, HG_W), lambda b, i: (b, i + tile0, 1)),
            pl.BlockSpec((1, ROW_TILE, 3 * CV_W), rmap),
            pl.BlockSpec((1, 8, 3 * CV_W),
                         lambda b, i: (b, jnp.maximum((i + tile0) * sub - 1, 0), 0)),
            pl.BlockSpec((1, 8, 3 * CV_W),
                         lambda b, i: (b, jnp.minimum((i + tile0 + 1) * sub, nsub - 1), 0)),
            pl.BlockSpec((8, N_MOD * D_MODEL), const2),
            pl.BlockSpec((1, HG_W), const2),
            pl.BlockSpec((3, CV_W), const2),
            pl.BlockSpec((D_MODEL, D_MODEL), const2),
            pl.BlockSpec((1, D_MODEL), const2),
            pl.BlockSpec((N_EXPERTS, D_MODEL), const2),
            pl.BlockSpec((HG_W, HG_W), const2),
        ],
        out_specs=[
            pl.BlockSpec((1, ROW_TILE, D_MODEL), omap),
            pl.BlockSpec((1, ROW_TILE, D_MODEL), omap),
            pl.BlockSpec((1, N_EXPERTS, ROW_TILE), lambda b, i: (b, 0, i)),
        ],
        out_shape=[
            jax.ShapeDtypeStruct((B, s_out, D_MODEL), F32),
            jax.ShapeDtypeStruct((B, s_out, D_MODEL), BF16),
            jax.ShapeDtypeStruct((B, N_EXPERTS, s_out), F32),
        ],
        name="mix_out",
        compiler_params=_cparams(("arbitrary", "arbitrary")),
    )(xs, att, o2[0], o2[1], hqg, cv, cv, cv, mod_l, gain, conv_w, wo_bf16, n2g, wr_t, ones)


def _topk_kernel(aff_ref, tri_ref, pos_ref, cnt_ref, *, segments):
    tri = tri_ref[...]

    def excl_cumsum(mask, n):
        carry = jnp.zeros((N_EXPERTS, 1), F32)
        parts = []
        carries = []
        for c in range(n // 128):
            carries.append(carry)
            blk = jnp.where(mask[:, c * 128:(c + 1) * 128], 1.0, 0.0)
            parts.append(_dot(blk.astype(BF16), tri) + carry)
            carry = carry + jnp.sum(blk, axis=1, keepdims=True)
        return jnp.concatenate(parts, axis=1), carries + [carry]

    for (lo, n, k) in segments:
        a = aff_ref[0, :, lo:lo + n]

        def body(it, thr):
            cand = thr | jnp.left_shift(jnp.int32(1), 30 - it)
            cnt = jnp.sum(jnp.where(a >= pltpu.bitcast(cand, F32), 1.0, 0.0), axis=1, keepdims=True)
            return jnp.where(cnt >= k, cand, thr)

        thr = lax.fori_loop(0, 31, body, jnp.zeros((N_EXPERTS, 1), I32))
        above = a >= pltpu.bitcast(thr + 1, F32)
        tied = jnp.logical_and(a >= pltpu.bitcast(thr, F32), jnp.logical_not(above))
        n_above = jnp.sum(jnp.where(above, 1.0, 0.0), axis=1, keepdims=True)
        rank_tied, _ = excl_cumsum(tied, n)
        sel = above | (tied & (rank_tied < (k - n_above)))
        pos, counts = excl_cumsum(sel, n)
        pos_ref[0, :, lo:lo + n] = jnp.where(sel, pos.astype(I32), -1)
    step = ROW_TILE // 128
    lane = lax.broadcasted_iota(I32, (N_EXPERTS, 128), 1)
    cnt = jnp.zeros((N_EXPERTS, 128), F32)
    for c, col in enumerate(counts[0::step]):
        cnt = jnp.where(lane == c, col, cnt)
    cnt_ref[0] = cnt.astype(I32)


def _topk_positions(aff, segments):
    B, E, S = aff.shape
    assert segments[-1][1] // ROW_TILE + 1 <= 128
    tri = jnp.asarray(np.triu(np.ones((128, 128)), 1), BF16)
    kern = functools.partial(_topk_kernel, segments=segments)
    return pl.pallas_call(
        kern,
        grid=(B,),
        in_specs=[pl.BlockSpec((1, E, S), lambda b: (b, 0, 0)),
                  pl.BlockSpec((128, 128), lambda b: (0, 0))],
        out_specs=[pl.BlockSpec((1, E, S), lambda b: (b, 0, 0)),
                   pl.BlockSpec((1, E, 128), lambda b: (b, 0, 0))],
        out_shape=[jax.ShapeDtypeStruct((B, E, S), I32),
                   jax.ShapeDtypeStruct((B, E, 128), I32)],
        name="topk_positions",
        compiler_params=_cparams(("arbitrary",)),
    )(aff, tri)


def _slot_block(cap):
    sbs = min(SLOT_BLOCK, cap)
    assert sbs >= ROW_TILE or sbs == cap
    return sbs


def _div_pow2(x, d):
    assert d & (d - 1) == 0
    return lax.shift_right_logical(x, jnp.int32(d.bit_length() - 1))


def _gather_kernel(cnt_ref, pos_ref, aff_ref, h_ref, *refs, ctx_len, cap_l, cap_c):
    if cap_c:
        xl_ref, gl_ref, xc_ref, gc_ref, acc_ref, gacc_ref = refs
    else:
        xl_ref, gl_ref, acc_ref, gacc_ref = refs
    b = pl.program_id(0)
    e = pl.program_id(1)
    S = h_ref.shape[1]
    pos = pos_ref[0, pl.ds(e, 1), :]
    aff = aff_ref[0, pl.ds(e, 1), :]
    nchunk = (S - ctx_len) // ROW_TILE
    sbs = _slot_block(cap_l)
    acc_ref[...] = jnp.zeros_like(acc_ref)
    gacc_ref[...] = jnp.zeros_like(gacc_ref)
    nsb = cap_l // sbs
    for c in range(nchunk):
        lo = ctx_len + c * ROW_TILE
        c0 = cnt_ref[b, e, c]
        c1 = cnt_ref[b, e, c + 1]
        sb0 = jnp.minimum(_div_pow2(c0, sbs), nsb - 1)

        def visit(sb, lo=lo):
            base = pl.multiple_of(sb * sbs, sbs)
            slot = lax.broadcasted_iota(I32, (sbs, ROW_TILE), 0) + base
            hit = pos[:, lo:lo + ROW_TILE] == slot
            acc_ref[pl.ds(base, sbs), :] += _dot(jnp.where(hit, 1.0, 0.0).astype(BF16),
                                                 h_ref[0, lo:lo + ROW_TILE, :])
            gacc_ref[pl.ds(base, sbs), :] += jnp.sum(
                jnp.where(hit, aff[:, lo:lo + ROW_TILE], 0.0), axis=1, keepdims=True)

        visit(sb0)
        pl.when(c1 > (sb0 + 1) * sbs)(functools.partial(visit, sb0 + 1))
    xl_ref[0, 0] = acc_ref[...].astype(BF16)
    gl_ref[0, 0] = gacc_ref[...]
    if cap_c:
        slot_c = lax.broadcasted_iota(I32, (cap_c, ctx_len), 0)
        hit = pos[:, 0:ctx_len] == slot_c
        xc_ref[0, 0] = _dot(jnp.where(hit, 1.0, 0.0).astype(BF16),
                            h_ref[0, 0:ctx_len, :]).astype(BF16)
        gc_ref[0, 0] = jnp.sum(jnp.where(hit, aff[:, 0:ctx_len], 0.0), axis=1, keepdims=True)


def _gather_rows(cnt, posm, aff, h2, ctx_len, cap_l, cap_c):
    B, E, S = posm.shape
    kern = functools.partial(_gather_kernel, ctx_len=ctx_len, cap_l=cap_l, cap_c=cap_c)
    omap = lambda b, e, cnt_ref: (e, b, 0, 0)
    caps = [cap_l] + ([cap_c] if cap_c else [])
    out_specs = []
    out_shape = []
    for cap in caps:
        out_specs += [pl.BlockSpec((1, 1, cap, D_MODEL), omap), pl.BlockSpec((1, 1, cap, 1), omap)]
        out_shape += [jax.ShapeDtypeStruct((E, B, cap, D_MODEL), BF16),
                      jax.ShapeDtypeStruct((E, B, cap, 1), F32)]
    grid_spec = pltpu.PrefetchScalarGridSpec(
        num_scalar_prefetch=1,
        grid=(B, E),
        in_specs=[pl.BlockSpec((1, E, S), lambda b, e, cnt_ref: (b, 0, 0)),
                  pl.BlockSpec((1, E, S), lambda b, e, cnt_ref: (b, 0, 0)),
                  pl.BlockSpec((1, S, D_MODEL), lambda b, e, cnt_ref: (b, 0, 0))],
        out_specs=out_specs,
        scratch_shapes=[pltpu.VMEM((cap_l, D_MODEL), F32), pltpu.VMEM((cap_l, 1), F32)],
    )
    return pl.pallas_call(
        kern,
        grid_spec=grid_spec,
        out_shape=out_shape,
        name="gather_rows",
        compiler_params=_cparams(("arbitrary", "arbitrary")),
    )(cnt, posm, aff, h2)


def _expert_kernel(*refs, row_chunk, n_sets):
    x_refs = refs[:n_sets]
    g_refs = refs[n_sets:2 * n_sets]
    wg_ref, wu_ref, wd_ref = refs[2 * n_sets:2 * n_sets + 3]
    y_refs = refs[2 * n_sets + 3:3 * n_sets + 3]
    acc_refs = refs[3 * n_sets + 3:]
    f = pl.program_id(1)
    nf = pl.num_programs(1)
    wg = wg_ref[0, 0].astype(BF16)
    wu = wu_ref[0, 0].astype(BF16)
    wd = wd_ref[0, 0].astype(BF16)
    for x_ref, g_ref, y_ref, acc_ref in zip(x_refs, g_refs, y_refs, acc_refs):
        rows = x_ref.shape[1]
        step = min(row_chunk, rows)
        for r in range(rows // step):
            rs = slice(r * step, (r + 1) * step)
            x = x_ref[0, rs, :]
            g = _dot(x, wg)
            u = _dot(x, wu)
            hid = (g * jax.nn.sigmoid(g) * u).astype(BF16)
            y = _dot(hid, wd)

            @pl.when(f == 0)
            def _():
                acc_ref[rs, :] = y

            @pl.when(f > 0)
            def _():
                acc_ref[rs, :] += y

        @pl.when(f == nf - 1)
        def _():
            y_ref[0] = (acc_ref[...] * g_ref[0]).astype(BF16)


def _expert_mlp(x_sets, g_sets, w_gate, w_up, w_down, layer, row_chunk):
    E = N_EXPERTS
    nf = D_EXPERT // FF_TILE
    n_sets = len(x_sets)
    kern = functools.partial(_expert_kernel, row_chunk=row_chunk, n_sets=n_sets)
    xspec = lambda r: pl.BlockSpec((1, r, D_MODEL), lambda e, f: (e, 0, 0))
    gspec = lambda r: pl.BlockSpec((1, r, 1), lambda e, f: (e, 0, 0))
    return pl.pallas_call(
        kern,
        grid=(E, nf),
        in_specs=[xspec(xs.shape[1]) for xs in x_sets] + [gspec(xs.shape[1]) for xs in x_sets] + [
            pl.BlockSpec((1, 1, D_MODEL, FF_TILE), lambda e, f: (layer, e, 0, f)),
            pl.BlockSpec((1, 1, D_MODEL, FF_TILE), lambda e, f: (layer, e, 0, f)),
            pl.BlockSpec((1, 1, FF_TILE, D_MODEL), lambda e, f: (layer, e, f, 0))],
        out_specs=[xspec(xs.shape[1]) for xs in x_sets],
        out_shape=[jax.ShapeDtypeStruct(xs.shape, BF16) for xs in x_sets],
        scratch_shapes=[pltpu.VMEM(xs.shape[1:], F32) for xs in x_sets],
        name="expert_mlp",
        compiler_params=_cparams(("arbitrary", "arbitrary")),
    )(*x_sets, *g_sets, w_gate, w_up, w_down)


def _combine_kernel(cnt_ref, x_ref, pos_ref, *refs, cap_l, cap_c, final):
    if cap_c:
        yl_ref, yc_ref, mod_ref, fg_ref, o_ref, acc_ref = refs
    else:
        yl_ref, mod_ref, fg_ref, o_ref, acc_ref = refs
    b = pl.program_id(0)
    i = pl.program_id(1)
    row = jnp.where(i == 0, 4, b) if cap_c else b
    gate = mod_ref[pl.ds(row, 1), 5 * D_MODEL:6 * D_MODEL]
    pos = pos_ref[0]
    acc_ref[...] = jnp.zeros_like(acc_ref)
    sbs = _slot_block(cap_l)

    def add_latent():
        t = i - 1 if cap_c else i
        nsb = cap_l // sbs
        def product(e, sb):
            base = pl.multiple_of(sb * sbs, sbs)
            lane = lax.broadcasted_iota(I32, (ROW_TILE, sbs), 1) + base
            onehot = jnp.where(pos[:, e:e + 1] == lane, 1.0, 0.0).astype(BF16)
            return _dot(onehot, yl_ref[e, 0, pl.ds(base, sbs), :])

        first = [jnp.minimum(_div_pow2(cnt_ref[b, e, t], sbs), nsb - 1) for e in range(N_EXPERTS)]
        total = product(0, first[0])
        for e in range(1, N_EXPERTS):
            total = total + product(e, first[e])
        acc_ref[...] += total
        for e in range(N_EXPERTS):
            def crossing(e=e):
                acc_ref[...] += product(e, first[e] + 1)

            pl.when(cnt_ref[b, e, t + 1] > (first[e] + 1) * sbs)(crossing)

    def add_context():
        ncol = yc_ref.shape[1]
        lane = lax.broadcasted_iota(I32, (ROW_TILE, ncol), 1)
        for e in range(N_EXPERTS):
            pe = pos[:, e:e + 1]
            tgt = jnp.where(pe >= 0, pe + b * cap_c, -1)
            acc_ref[...] += _dot(jnp.where(tgt == lane, 1.0, 0.0).astype(BF16), yc_ref[e])

    if cap_c:
        pl.when(i == 0)(add_context)
        pl.when(i > 0)(add_latent)
    else:
        add_latent()
    x2 = x_ref[0] + gate * acc_ref[...]
    if final:
        var = jnp.mean(x2 * x2, axis=-1, keepdims=True)
        x2 = x2 * lax.rsqrt(var + EPS) * fg_ref[...]
    o_ref[0] = x2


def _combine(cnt, x1, pos_t, yl, yc, mod_l, final_g, cap_l, cap_c, final):
    B, S, _ = x1.shape
    E = N_EXPERTS
    rmap = lambda b, i, cnt_ref: (b, i, 0)
    const2 = lambda b, i, cnt_ref: (0, 0)
    kern = functools.partial(_combine_kernel, cap_l=cap_l, cap_c=cap_c, final=final)
    y_specs = [pl.BlockSpec((E, 1, cap_l, D_MODEL), lambda b, i, cnt_ref: (0, b, 0, 0),
                            pipeline_mode=pl.Buffered(1))]
    y_args = [yl]
    if cap_c:
        y_specs.append(pl.BlockSpec(yc.shape, lambda b, i, cnt_ref: (0, 0, 0),
                                    pipeline_mode=pl.Buffered(1)))
        y_args.append(yc)
    grid_spec = pltpu.PrefetchScalarGridSpec(
        num_scalar_prefetch=1,
        grid=(B, S // ROW_TILE),
        in_specs=[pl.BlockSpec((1, ROW_TILE, D_MODEL), rmap),
                  pl.BlockSpec((1, ROW_TILE, E), rmap)] + y_specs + [
                  pl.BlockSpec((8, N_MOD * D_MODEL), const2),
                  pl.BlockSpec((1, D_MODEL), const2)],
        out_specs=pl.BlockSpec((1, ROW_TILE, D_MODEL), rmap),
        scratch_shapes=[pltpu.VMEM((ROW_TILE, D_MODEL), F32)],
    )
    return pl.pallas_call(
        kern,
        grid_spec=grid_spec,
        out_shape=jax.ShapeDtypeStruct((B, S, D_MODEL), F32),
        name="combine",
        compiler_params=_cparams(("arbitrary", "arbitrary")),
    )(cnt, x1, pos_t, *y_args, mod_l, final_g)


def _rope_tables(n_lat, ctx_len):
    t = np.arange(n_lat)
    pos = np.stack([t // GRID_W, t % GRID_W], axis=-1).astype(np.float32)
    inv = (ROPE_BASE ** (-np.arange(ROPE_FREQS, dtype=np.float32) / ROPE_FREQS)).astype(np.float32)
    ang = pos[:, :, None] * inv
    cos = np.cos(ang).astype(np.float32)
    sin = np.sin(ang).astype(np.float32)
    cos64 = np.concatenate([cos[:, 0], cos[:, 0], cos[:, 1], cos[:, 1]], axis=-1)
    sin64 = np.concatenate([-sin[:, 0], sin[:, 0], -sin[:, 1], sin[:, 1]], axis=-1)
    cos_t = np.concatenate([np.ones((ctx_len, 64), np.float32), cos64], axis=0)
    sin_t = np.concatenate([np.zeros((ctx_len, 64), np.float32), sin64], axis=0)
    return (jnp.asarray(np.tile(cos_t, (1, 2))), jnp.asarray(np.tile(sin_t, (1, 2))))


def kernel(x, c, ctx, c_ctx, ada_w, ada_b, norm1_g, norm2_g, w_in, attn_sink, hgrn_lb,
           hgrn_norm_g, conv_w, w_o, router_w, exp_w_gate, exp_w_up, exp_w_down, final_norm_g):
    B, T, D = x.shape
    L = ctx.shape[1]
    depth = ada_w.shape[0]
    assert D == D_MODEL and L == ROW_TILE and T % ROW_TILE == 0 and B <= 4
    S = L + T
    cap_l = EC_CAPACITY * T // N_EXPERTS
    cap_c = EC_CAPACITY * L // N_EXPERTS
    assert cap_l % 16 == 0 and cap_c % 16 == 0 and (B * cap_c) % 16 == 0

    cos_t, sin_t = _rope_tables(T, L)
    gamma = jax.nn.softmax(hgrn_lb.astype(F32), axis=0)
    lb_all = jnp.cumsum(gamma, axis=0) - gamma[0]
    cvec = jnp.concatenate([c, jnp.zeros((4 - B, D), F32), c_ctx[None],
                            jnp.zeros((3, D), F32)], axis=0)
    mod = _modulation(cvec, ada_w, ada_b)
    xs = jnp.concatenate([ctx, x], axis=1)

    def interleave_heads(w, axis):
        shp = w.shape
        w = w.reshape(shp[:axis] + (N_KV, GQA_GROUP, HEAD_DIM) + shp[axis + 1:])
        return jnp.swapaxes(w, axis, axis + 1).reshape(shp)

    for l in range(depth):
        last = l == depth - 1
        w_in_l = w_in[l].astype(BF16)
        w_in_l = jnp.concatenate([w_in_l[:, :1024], interleave_heads(w_in_l[:, 1024:1536], 1),
                                  w_in_l[:, 1536:]], axis=1)
        w_o_l = w_o[l].astype(BF16)
        w_o_l = jnp.concatenate([interleave_heads(w_o_l[:ATT_W], 0), w_o_l[ATT_W:]], axis=0)
        q, kv, hg, hqg, cv = _in_projection(xs, mod[l], norm1_g[l][None], w_in_l, cos_t, sin_t)
        att = _attention(q, kv, attn_sink[l], L, skip_ctx=last)
        o2 = _hgrn(hg, hqg, lb_all[l], L)
        x1, h2, aff = _mix_out(xs, att, o2, hqg, cv, mod[l], hgrn_norm_g[l][None], conv_w[l],
                               w_o_l, norm2_g[l][None], router_w[l].T, skip_ctx=last)
        lm, cc = (0, 0) if last else (L, cap_c)
        segments = ((lm, T, cap_l),) if last else ((0, L, cap_c), (L, T, cap_l))
        posm, cnt = _topk_positions(aff, segments)
        gathered = _gather_rows(cnt, posm, aff, h2, lm, cap_l, cc)
        x_sets = [a.reshape(N_EXPERTS, -1, D) for a in gathered[0::2]]
        g_sets = [a.reshape(N_EXPERTS, -1, 1) for a in gathered[1::2]]
        ys = _expert_mlp(x_sets, g_sets, exp_w_gate, exp_w_up, exp_w_down, l, cap_l)
        yl = ys[0].reshape(N_EXPERTS, B, cap_l, D)
        xs = _combine(cnt, x1, jnp.swapaxes(posm, 1, 2), yl, None if last else ys[1], mod[l],
                      final_norm_g[None], cap_l, cc, final=last)
    return xs
```

```python
import functools

import numpy as np
import jax
import jax.numpy as jnp
from jax import lax
from jax.experimental import pallas as pl
from jax.experimental.pallas import tpu as pltpu

F32 = jnp.float32
BF16 = jnp.bfloat16
I32 = jnp.int32
HIGHEST = lax.Precision.HIGHEST

D_MODEL = 1024
GRID_W = 64
EPS = 1e-6
LB_FLOOR = 1e-30
N_MOD = 6
ATT_W = 512
HG_W = 256
CV_W = 256
HEAD_DIM = 64
N_Q = 8
N_KV = 2
GQA_GROUP = 4
KV_W = 128
ROPE_BASE = 10000.0
ROPE_FREQS = 16
HG_HEADS = 4
N_EXPERTS = 16
EC_CAPACITY = 2
D_EXPERT = 2048
IN_COLS = 2816

ROW_TILE = 256
ATT_BLOCK = 128
HG_CHUNK = 128
HG_LEVELS = 7
HG_BATCH = 2
SLOT_BLOCK = 256
FF_TILE = 512
MOD_TILE = 1536
VMEM_LIMIT = 56 * 1024 * 1024

NEG_BIG = -1e30


def _cparams(sem):
    return pltpu.CompilerParams(dimension_semantics=sem, vmem_limit_bytes=VMEM_LIMIT)


def _dot(a, b):
    return jnp.dot(a, b, preferred_element_type=F32)


def _dot_nt(a, b):
    return lax.dot_general(a, b, (((1,), (1,)), ((), ())), preferred_element_type=F32)


def _mod_kernel(a_ref, w_ref, b_ref, o_ref):
    a = a_ref[...]
    a = a * jax.nn.sigmoid(a)
    o_ref[0] = jnp.dot(a, w_ref[0], precision=HIGHEST, preferred_element_type=F32) + b_ref[0]


def _modulation(cvec, ada_w, ada_b):
    depth = ada_w.shape[0]
    ncol = ada_w.shape[2]
    return pl.pallas_call(
        _mod_kernel,
        grid=(depth, ncol // MOD_TILE),
        in_specs=[
            pl.BlockSpec((8, D_MODEL), lambda l, j: (0, 0)),
            pl.BlockSpec((1, D_MODEL, MOD_TILE), lambda l, j: (l, 0, j)),
            pl.BlockSpec((1, 1, MOD_TILE), lambda l, j: (l, 0, j)),
        ],
        out_specs=pl.BlockSpec((1, 8, MOD_TILE), lambda l, j: (l, 0, j)),
        out_shape=jax.ShapeDtypeStruct((depth, 8, ncol), F32),
        name="modulation",
        compiler_params=_cparams(("arbitrary", "arbitrary")),
    )(cvec, ada_w, ada_b.reshape(depth, 1, ncol))


def _swap_halves(x):
    n = x.shape[-1]
    lane = lax.broadcasted_iota(I32, x.shape, x.ndim - 1)
    up = pltpu.roll(x, n - ROPE_FREQS, x.ndim - 1)
    dn = pltpu.roll(x, ROPE_FREQS, x.ndim - 1)
    return jnp.where((lane % (2 * ROPE_FREQS)) < ROPE_FREQS, up, dn)


def _rmsnorm_mod(x, g, shift, scale):
    var = jnp.mean(x * x, axis=-1, keepdims=True)
    y = x * lax.rsqrt(var + EPS) * g
    return y * (1.0 + scale) + shift


def _inproj_kernel(x_ref, mod_ref, g_ref, w_ref, cos_ref, sin_ref,
                   q_ref, kv_ref, hg_ref, hqg_ref, cv_ref):
    b = pl.program_id(0)
    i = pl.program_id(1)
    row = jnp.where(i == 0, 4, b)
    shift = mod_ref[pl.ds(row, 1), 0:D_MODEL]
    scale = mod_ref[pl.ds(row, 1), D_MODEL:2 * D_MODEL]
    h = _rmsnorm_mod(x_ref[0], g_ref[...], shift, scale)
    p = _dot(h.astype(BF16), w_ref[...])
    cos2 = cos_ref[...]
    sin2 = sin_ref[...]
    k = p[:, 0:KV_W]
    k = k * cos2 + _swap_halves(k) * sin2
    kv_ref[0, :, 0:KV_W] = k.astype(BF16)
    kv_ref[0, :, KV_W:2 * KV_W] = p[:, KV_W:2 * KV_W].astype(BF16)
    hg_ref[0] = p[:, 256:1024]
    q = p[:, 1024:1536]
    cos8 = jnp.concatenate([cos2] * 4, axis=1)
    sin8 = jnp.concatenate([sin2] * 4, axis=1)
    q = (q * cos8 + _swap_halves(q) * sin8) * (HEAD_DIM ** -0.5)
    q_ref[0] = q.astype(BF16)
    hqg_ref[0] = p[:, 1536:2048]
    cv_ref[0] = p[:, 2048:2816]


def _in_projection(xs, mod_l, g, w_bf16, cos_t, sin_t):
    B, S, _ = xs.shape
    nt = S // ROW_TILE
    row_map = lambda b, i: (b, i, 0)
    const2 = lambda b, i: (0, 0)
    return pl.pallas_call(
        _inproj_kernel,
        grid=(B, nt),
        in_specs=[
            pl.BlockSpec((1, ROW_TILE, D_MODEL), row_map),
            pl.BlockSpec((8, N_MOD * D_MODEL), const2),
            pl.BlockSpec((1, D_MODEL), const2),
            pl.BlockSpec((D_MODEL, IN_COLS), const2),
            pl.BlockSpec((ROW_TILE, 2 * HEAD_DIM), lambda b, i: (i, 0)),
            pl.BlockSpec((ROW_TILE, 2 * HEAD_DIM), lambda b, i: (i, 0)),
        ],
        out_specs=[
            pl.BlockSpec((1, ROW_TILE, ATT_W), row_map),
            pl.BlockSpec((1, ROW_TILE, 2 * KV_W), row_map),
            pl.BlockSpec((1, ROW_TILE, 3 * HG_W), row_map),
            pl.BlockSpec((1, ROW_TILE, 2 * HG_W), row_map),
            pl.BlockSpec((1, ROW_TILE, 3 * CV_W), row_map),
        ],
        out_shape=[
            jax.ShapeDtypeStruct((B, S, ATT_W), BF16),
            jax.ShapeDtypeStruct((B, S, 2 * KV_W), BF16),
            jax.ShapeDtypeStruct((B, S, 3 * HG_W), F32),
            jax.ShapeDtypeStruct((B, S, 2 * HG_W), F32),
            jax.ShapeDtypeStruct((B, S, 3 * CV_W), F32),
        ],
        name="in_projection",
        compiler_params=_cparams(("arbitrary", "arbitrary")),
    )(xs, mod_l, g, w_bf16, cos_t, sin_t)


def _attn_kernel(sink_ref, q_ref, kp_ref, kc_ref, kn_ref, kx_ref, o_ref, *, blk0, nblk, nctx):
    n = pl.program_id(1) + blk0
    is_lat = n >= nctx
    has_prev = n > nctx
    has_next = n < nblk - 1
    W = ATT_BLOCK
    q = q_ref[0]
    qrows = jnp.concatenate([q[:, g * W:(g + 1) * W] for g in range(GQA_GROUP)], axis=0)
    kv_all = jnp.concatenate([kp_ref[0], kc_ref[0], kn_ref[0], kx_ref[0]], axis=0)
    nkeys = kv_all.shape[0]
    k_all = kv_all[:, 0:KV_W]
    v_ext = jnp.concatenate([kv_all[:, KV_W:2 * KV_W], jnp.ones((nkeys, KV_W), BF16)], axis=1)
    rows = GQA_GROUP * W
    ri = lax.broadcasted_iota(I32, (rows, W), 0) % W
    cj = lax.broadcasted_iota(I32, (rows, W), 1)
    m_prev = (cj >= ri) & has_prev
    m_cur = jnp.broadcast_to(is_lat, (rows, W))
    m_next = (cj <= ri) & (has_next & is_lat)
    grp = lax.broadcasted_iota(I32, (rows, 1), 0) // W
    low = cj < HEAD_DIM
    outs = []
    for h in range(N_KV):
        qh = jnp.where(low if h == 0 else jnp.logical_not(low), qrows, jnp.zeros_like(qrows))
        sink = jnp.zeros((rows, 1), F32)
        for g in range(GQA_GROUP):
            sink = jnp.where(grp == g, sink_ref[h * GQA_GROUP + g], sink)
        s = _dot_nt(qh, k_all)
        segs = [jnp.where(m_prev, s[:, 0:W], NEG_BIG),
                jnp.where(m_cur, s[:, W:2 * W], NEG_BIG),
                jnp.where(m_next, s[:, 2 * W:3 * W], NEG_BIG)]
        segs += [s[:, c:c + W] for c in range(3 * W, nkeys, W)]
        mx = segs[0]
        for sg in segs[1:]:
            mx = jnp.maximum(mx, sg)
        m = jnp.maximum(jnp.max(mx, axis=1, keepdims=True), sink)
        p = jnp.concatenate([jnp.exp(sg - m).astype(BF16) for sg in segs], axis=1)
        oe = _dot(p, v_ext)
        den = oe[:, KV_W:2 * KV_W] + jnp.exp(sink - m)
        outs.append(oe[:, 0:KV_W] / den)
    o = jnp.where(low, outs[0], outs[1])
    for g in range(GQA_GROUP):
        o_ref[0, :, g * W:(g + 1) * W] = o[g * W:(g + 1) * W].astype(BF16)


def _attention(q, kv, sink, ctx_len, skip_ctx):
    B, S, _ = q.shape
    nblk = S // ATT_BLOCK
    nctx = ctx_len // ATT_BLOCK
    blk0 = nctx if skip_ctx else 0
    blk = lambda f: (lambda b, n: (b, f(n + blk0), 0))
    kern = functools.partial(_attn_kernel, blk0=blk0, nblk=nblk, nctx=nctx)
    return pl.pallas_call(
        kern,
        grid=(B, nblk - blk0),
        in_specs=[
            pl.BlockSpec(memory_space=pltpu.SMEM),
            pl.BlockSpec((1, ATT_BLOCK, ATT_W), blk(lambda n: n)),
            pl.BlockSpec((1, ATT_BLOCK, 2 * KV_W), blk(lambda n: jnp.maximum(n - 1, 0))),
            pl.BlockSpec((1, ATT_BLOCK, 2 * KV_W), blk(lambda n: n)),
            pl.BlockSpec((1, ATT_BLOCK, 2 * KV_W), blk(lambda n: jnp.minimum(n + 1, nblk - 1))),
            pl.BlockSpec((1, ctx_len, 2 * KV_W), lambda b, n: (b, 0, 0)),
        ],
        out_specs=pl.BlockSpec((1, ATT_BLOCK, ATT_W), lambda b, n: (b, n, 0)),
        out_shape=jax.ShapeDtypeStruct((B, S - blk0 * ATT_BLOCK, ATT_W), BF16),
        name="attention",
        compiler_params=_cparams(("arbitrary", "arbitrary")),
    )(sink, q, kv, kv, kv, kv)


def _hgrn_constants():
    C = HG_CHUNK
    t = np.arange(C)[:, None]
    r = np.arange(C)[None, :]
    tri = np.stack([r <= t, r >= t]).astype(np.float32)
    x = t ^ r
    lvl = np.where(x > 0, np.floor(np.log2(np.maximum(x, 1))).astype(np.int32), HG_LEVELS)
    lvl_f = np.where(t >= r, lvl, -1).astype(np.int32)
    lvl_b = np.where(t <= r, lvl, -1).astype(np.int32)
    lvl2 = np.stack([np.tile(lvl_f, (1, HG_HEADS)), np.tile(lvl_b, (1, HG_HEADS))])
    return tri, lvl2


def _span_row(x, span, row):
    C = x.shape[0]
    if span >= 8:
        x3 = x.reshape(C // span, span, x.shape[1])
        return jnp.broadcast_to(x3[:, row:row + 1, :], x3.shape).reshape(x.shape)
    pos = lax.broadcasted_iota(I32, x.shape, 0) % span
    out = x
    for p in range(span):
        if p != row:
            out = jnp.where(pos == p, pltpu.roll(x, (p - row) % C, 0), out)
    return out


def _hgrn_chunk(v, z, qr, lb, tri, masks, head_of_lane, same_head, st_ref, sidx, backward):
    C = HG_CHUNK
    logf = jnp.log(jnp.maximum(lb, LB_FLOOR) + (1.0 - lb) * jax.nn.sigmoid(z))
    k = (1.0 - lb) * jax.nn.sigmoid(-z)
    q = qr * jax.nn.sigmoid(qr)
    hi = logf.astype(BF16)
    lo = (logf - hi.astype(F32)).astype(BF16)
    cs = _dot(tri, jnp.concatenate([hi, lo], axis=1))
    lam = cs[:, 0:HG_W] + cs[:, HG_W:2 * HG_W]
    tot = lam[0:1] if backward else lam[C - 1:C]
    f_tot = jnp.exp(tot)

    def per_head_rows(x):
        zero = jnp.zeros_like(x)
        return jnp.concatenate([jnp.where(head_of_lane == h, x, zero) for h in range(HG_HEADS)],
                               axis=0)

    a = jnp.where(masks[HG_LEVELS], _dot_nt(q.astype(BF16), per_head_rows(k.astype(BF16))), 0.0)
    for l in range(HG_LEVELS):
        m = 1 << l
        ref = _span_row(lam, 2 * m, m if backward else m - 1)
        fac = jnp.exp(-jnp.abs(lam - ref))
        s_l = _dot_nt((q * fac).astype(BF16), per_head_rows((k * fac).astype(BF16)))
        a = jnp.where(masks[l], s_l, a)
    st = st_ref[sidx]
    q_in = (q * jnp.exp(lam)).astype(BF16)
    o = _dot(a.astype(BF16), per_head_rows(v.astype(BF16))) + _dot_nt(q_in, st.astype(BF16))
    k_out = (k * jnp.exp(tot - lam)).astype(BF16)
    upd = _dot(v.T.astype(BF16), k_out)
    st_ref[sidx] = st * f_tot + jnp.where(same_head, upd, 0.0)
    return o


def _hgrn_kernel(vf_ref, zf_ref, qf_ref, vb_ref, zb_ref, qb_ref, lb_ref, tri_ref, lvl_ref,
                 of_ref, ob_ref, st_ref):
    j = pl.program_id(1)

    @pl.when(j == 0)
    def _():
        st_ref[...] = jnp.zeros_like(st_ref)

    head_of_lane = lax.broadcasted_iota(I32, (HG_CHUNK, HG_W), 1) // HEAD_DIM
    same_head = (lax.broadcasted_iota(I32, (HG_W, HG_W), 0) // HEAD_DIM
                 == lax.broadcasted_iota(I32, (HG_W, HG_W), 1) // HEAD_DIM)
    for d, (v_ref, z_ref, q_ref, o_ref) in enumerate(
            ((vf_ref, zf_ref, qf_ref, of_ref), (vb_ref, zb_ref, qb_ref, ob_ref))):
        lvl = lvl_ref[d]
        masks = [lvl == l for l in range(HG_LEVELS + 1)]
        for s in range(v_ref.shape[0]):
            o_ref[s] = _hgrn_chunk(v_ref[s], z_ref[s], q_ref[s], lb_ref[d], tri_ref[d], masks,
                                   head_of_lane, same_head, st_ref, 2 * s + d, backward=(d == 1))


def _hgrn(hg, hqg, lb_l, ctx_len):
    B, S, _ = hg.shape
    nc = S // HG_CHUNK
    nctx = ctx_len // HG_CHUNK
    tri_np, lvl_np = _hgrn_constants()
    tri = jnp.asarray(tri_np, BF16)
    lvl = jnp.asarray(lvl_np, I32)

    def back(j):
        return jnp.where(j < nctx, nctx - 1 - j, nc - 1 - (j - nctx))

    bs = HG_BATCH if B % HG_BATCH == 0 else 1
    blk = (bs, HG_CHUNK, HG_W)
    fwd = lambda col: (lambda b, j: (b, j, col))
    bwd = lambda col: (lambda b, j: (b, back(j), col))
    const3 = lambda b, j: (0, 0, 0)
    return pl.pallas_call(
        _hgrn_kernel,
        grid=(B // bs, nc),
        in_specs=[
            pl.BlockSpec(blk, fwd(0)), pl.BlockSpec(blk, fwd(1)), pl.BlockSpec(blk, fwd(0)),
            pl.BlockSpec(blk, bwd(0)), pl.BlockSpec(blk, bwd(2)), pl.BlockSpec(blk, bwd(0)),
            pl.BlockSpec((2, 1, HG_W), const3),
            pl.BlockSpec((2, HG_CHUNK, HG_CHUNK), const3),
            pl.BlockSpec((2, HG_CHUNK, HG_HEADS * HG_CHUNK), const3),
        ],
        out_specs=[pl.BlockSpec(blk, fwd(0)), pl.BlockSpec(blk, bwd(0))],
        out_shape=[jax.ShapeDtypeStruct((B, S, HG_W), F32)] * 2,
        scratch_shapes=[pltpu.VMEM((2 * bs, HG_W, HG_W), F32)],
        name="hgrn_scan",
        compiler_params=_cparams(("arbitrary", "arbitrary")),
    )(hg, hg, hqg, hg, hg, hqg, lb_l.reshape(2, 1, HG_W), tri, lvl)


def _mixout_kernel(x_ref, att_ref, of_ref, ob_ref, g_ref, cv_ref, cprev_ref, cnext_ref,
                   mod_ref, gain_ref, cw_ref, wo_ref, n2_ref, wr_ref, ones_ref,
                   x1_ref, h2_ref, aff_ref, *, tile0, ntile):
    b = pl.program_id(0)
    i = pl.program_id(1) + tile0
    row = jnp.where(i == 0, 4, b)
    R = ROW_TILE

    def modv(c):
        return mod_ref[pl.ds(row, 1), c * D_MODEL:(c + 1) * D_MODEL]

    o = of_ref[0] + ob_ref[0]
    sq = o * o
    sq_hi = sq.astype(BF16)
    sq_lo = (sq - sq_hi.astype(F32)).astype(BF16)
    ms = _dot(sq_hi, ones_ref[...]) + _dot(sq_lo, ones_ref[...])
    g = g_ref[0]
    hg = o * lax.rsqrt(ms * (1.0 / HEAD_DIM) + EPS) * gain_ref[...] * (g * jax.nn.sigmoid(g))
    cv = cv_ref[0]
    u = cv[:, CV_W:2 * CV_W] * cv[:, 2 * CV_W:3 * CV_W]
    up = cprev_ref[0]
    un = cnext_ref[0]
    u_prev_row = up[7:8, CV_W:2 * CV_W] * up[7:8, 2 * CV_W:3 * CV_W]
    u_next_row = un[0:1, CV_W:2 * CV_W] * un[0:1, 2 * CV_W:3 * CV_W]
    u_prev_row = jnp.where(i <= 1, 0.0, u_prev_row)
    u_next_row = jnp.where((i == 0) | (i == ntile - 1), 0.0, u_next_row)
    ridx = lax.broadcasted_iota(I32, (R, CV_W), 0)
    u_m1 = jnp.where(ridx == 0, u_prev_row, pltpu.roll(u, 1, 0))
    u_p1 = jnp.where(ridx == R - 1, u_next_row, pltpu.roll(u, R - 1, 0))
    cw = cw_ref[...]
    conv = cv[:, 0:CV_W] * (u_m1 * cw[0:1] + u * cw[1:2] + u_p1 * cw[2:3])
    mix = (_dot(att_ref[0], wo_ref[0:ATT_W])
           + _dot(hg.astype(BF16), wo_ref[ATT_W:ATT_W + HG_W])
           + _dot(conv.astype(BF16), wo_ref[ATT_W + HG_W:D_MODEL]))
    x1 = x_ref[0] + modv(2) * mix
    x1_ref[0] = x1
    h2 = _rmsnorm_mod(x1, n2_ref[...], modv(3), modv(4))
    h_hi = h2.astype(BF16)
    h2_ref[0] = h_hi
    h_lo = (h2 - h_hi.astype(F32)).astype(BF16)
    wr = wr_ref[...]
    w_hi = wr.astype(BF16)
    w_lo = (wr - w_hi.astype(F32)).astype(BF16)
    r1 = _dot_nt(jnp.concatenate([w_hi, w_lo], axis=0), h_hi)
    logits = r1[0:N_EXPERTS] + r1[N_EXPERTS:2 * N_EXPERTS] + _dot_nt(w_hi, h_lo)
    e = jnp.exp(logits - jnp.max(logits, axis=0, keepdims=True))
    aff_ref[0] = e / jnp.sum(e, axis=0, keepdims=True)


def _mix_out(xs, att, o2, hqg, cv, mod_l, gain, conv_w, wo_bf16, n2g, wr_t, skip_ctx):
    B, S, _ = xs.shape
    ntile = S // ROW_TILE
    tile0 = 1 if skip_ctx else 0
    sub = ROW_TILE // 8
    nsub = S // 8
    rmap = lambda b, i: (b, i + tile0, 0)
    omap = lambda b, i: (b, i, 0)
    s_out = S - tile0 * ROW_TILE
    const2 = lambda b, i: (0, 0)
    ones = jnp.asarray(np.kron(np.eye(HG_HEADS), np.ones((HEAD_DIM, HEAD_DIM))), BF16)
    kern = functools.partial(_mixout_kernel, tile0=tile0, ntile=ntile)
    return pl.pallas_call(
        kern,
        grid=(B, ntile - tile0),
        in_specs=[
            pl.BlockSpec((1, ROW_TILE, D_MODEL), rmap),
            pl.BlockSpec((1, ROW_TILE, ATT_W), omap),
            pl.BlockSpec((1, ROW_TILE, HG_W), rmap),
            pl.BlockSpec((1, ROW_TILE, HG_W), rmap),
            pl.BlockSpec((1, ROW_TILE, HG_W), lambda b, i: (b, i + tile0, 1)),
            pl.BlockSpec((1, ROW_TILE, 3 * CV_W), rmap),
            pl.BlockSpec((1, 8, 3 * CV_W),
                         lambda b, i: (b, jnp.maximum((i + tile0) * sub - 1, 0), 0)),
            pl.BlockSpec((1, 8, 3 * CV_W),
                         lambda b, i: (b, jnp.minimum((i + tile0 + 1) * sub, nsub - 1), 0)),
            pl.BlockSpec((8, N_MOD * D_MODEL), const2),
            pl.BlockSpec((1, HG_W), const2),
            pl.BlockSpec((3, CV_W), const2),
            pl.BlockSpec((D_MODEL, D_MODEL), const2),
            pl.BlockSpec((1, D_MODEL), const2),
            pl.BlockSpec((N_EXPERTS, D_MODEL), const2),
            pl.BlockSpec((HG_W, HG_W), const2),
        ],
        out_specs=[
            pl.BlockSpec((1, ROW_TILE, D_MODEL), omap),
            pl.BlockSpec((1, ROW_TILE, D_MODEL), omap),
            pl.BlockSpec((1, N_EXPERTS, ROW_TILE), lambda b, i: (b, 0, i)),
        ],
        out_shape=[
            jax.ShapeDtypeStruct((B, s_out, D_MODEL), F32),
            jax.ShapeDtypeStruct((B, s_out, D_MODEL), BF16),
            jax.ShapeDtypeStruct((B, N_EXPERTS, s_out), F32),
        ],
        name="mix_out",
        compiler_params=_cparams(("arbitrary", "arbitrary")),
    )(xs, att, o2[0], o2[1], hqg, cv, cv, cv, mod_l, gain, conv_w, wo_bf16, n2g, wr_t, ones)


def _topk_kernel(aff_ref, tri_ref, pos_ref, cnt_ref, *, segments):
    tri = tri_ref[...]

    def excl_cumsum(mask, n):
        carry = jnp.zeros((N_EXPERTS, 1), F32)
        parts = []
        carries = []
        for c in range(n // 128):
            carries.append(carry)
            blk = jnp.where(mask[:, c * 128:(c + 1) * 128], 1.0, 0.0)
            parts.append(_dot(blk.astype(BF16), tri) + carry)
            carry = carry + jnp.sum(blk, axis=1, keepdims=True)
        return jnp.concatenate(parts, axis=1), carries + [carry]

    for (lo, n, k) in segments:
        a = aff_ref[0, :, lo:lo + n]

        def body(it, thr):
            cand = thr | jnp.left_shift(jnp.int32(1), 30 - it)
            cnt = jnp.sum(jnp.where(a >= pltpu.bitcast(cand, F32), 1.0, 0.0), axis=1, keepdims=True)
            return jnp.where(cnt >= k, cand, thr)

        thr = lax.fori_loop(0, 31, body, jnp.zeros((N_EXPERTS, 1), I32))
        above = a >= pltpu.bitcast(thr + 1, F32)
        tied = jnp.logical_and(a >= pltpu.bitcast(thr, F32), jnp.logical_not(above))
        n_above = jnp.sum(jnp.where(above, 1.0, 0.0), axis=1, keepdims=True)
        rank_tied, _ = excl_cumsum(tied, n)
        sel = above | (tied & (rank_tied < (k - n_above)))
        pos, counts = excl_cumsum(sel, n)
        pos_ref[0, :, lo:lo + n] = jnp.where(sel, pos.astype(I32), -1)
    step = ROW_TILE // 128
    lane = lax.broadcasted_iota(I32, (N_EXPERTS, 128), 1)
    cnt = jnp.zeros((N_EXPERTS, 128), F32)
    for c, col in enumerate(counts[0::step]):
        cnt = jnp.where(lane == c, col, cnt)
    cnt_ref[0] = cnt.astype(I32)


def _topk_positions(aff, segments):
    B, E, S = aff.shape
    assert segments[-1][1] // ROW_TILE + 1 <= 128
    tri = jnp.asarray(np.triu(np.ones((128, 128)), 1), BF16)
    kern = functools.partial(_topk_kernel, segments=segments)
    return pl.pallas_call(
        kern,
        grid=(B,),
        in_specs=[pl.BlockSpec((1, E, S), lambda b: (b, 0, 0)),
                  pl.BlockSpec((128, 128), lambda b: (0, 0))],
        out_specs=[pl.BlockSpec((1, E, S), lambda b: (b, 0, 0)),
                   pl.BlockSpec((1, E, 128), lambda b: (b, 0, 0))],
        out_shape=[jax.ShapeDtypeStruct((B, E, S), I32),
                   jax.ShapeDtypeStruct((B, E, 128), I32)],
        name="topk_positions",
        compiler_params=_cparams(("arbitrary",)),
    )(aff, tri)


def _slot_block(cap):
    sbs = min(SLOT_BLOCK, cap)
    assert sbs >= ROW_TILE or sbs == cap
    return sbs


def _div_pow2(x, d):
    assert d & (d - 1) == 0
    return lax.shift_right_logical(x, jnp.int32(d.bit_length() - 1))


def _gather_kernel(cnt_ref, pos_ref, aff_ref, h_ref, *refs, ctx_len, cap_l, cap_c):
    if cap_c:
        xl_ref, gl_ref, xc_ref, gc_ref, acc_ref, gacc_ref = refs
    else:
        xl_ref, gl_ref, acc_ref, gacc_ref = refs
    b = pl.program_id(0)
    e = pl.program_id(1)
    S = h_ref.shape[1]
    pos = pos_ref[0, pl.ds(e, 1), :]
    aff = aff_ref[0, pl.ds(e, 1), :]
    nchunk = (S - ctx_len) // ROW_TILE
    sbs = _slot_block(cap_l)
    acc_ref[...] = jnp.zeros_like(acc_ref)
    gacc_ref[...] = jnp.zeros_like(gacc_ref)
    nsb = cap_l // sbs
    first = [jnp.minimum(_div_pow2(cnt_ref[b, e, c], sbs), nsb - 1) for c in range(nchunk)]
    slot_i32 = lax.broadcasted_iota(I32, (sbs, ROW_TILE), 0)
    slot_in_block = slot_i32.astype(F32).astype(BF16)
    one = jnp.ones((sbs, ROW_TILE), BF16)
    zero = jnp.zeros((sbs, ROW_TILE), BF16)

    def visit(c, sb):
        lo = ctx_len + c * ROW_TILE
        base = pl.multiple_of(sb * sbs, sbs)
        rel_i = pos[:, lo:lo + ROW_TILE] - base
        rel = rel_i.astype(F32)
        rel = jnp.where((rel >= 0.0) & (rel < float(sbs)), rel, -1.0).astype(BF16)
        acc_ref[pl.ds(base, sbs), :] += _dot(jnp.where(rel == slot_in_block, one, zero),
                                             h_ref[0, lo:lo + ROW_TILE, :])
        gacc_ref[pl.ds(base, sbs), :] += jnp.sum(
            jnp.where(rel_i == slot_i32, aff[:, lo:lo + ROW_TILE], 0.0), axis=1, keepdims=True)

    for c in range(nchunk):
        visit(c, first[c])
    for c in range(nchunk):
        pl.when(cnt_ref[b, e, c + 1] > (first[c] + 1) * sbs)(
            functools.partial(visit, c, first[c] + 1))
    xl_ref[0, 0] = acc_ref[...].astype(BF16)
    gl_ref[0, 0] = gacc_ref[...]
    if cap_c:
        slot_c = lax.broadcasted_iota(I32, (cap_c, ctx_len), 0)
        hit = pos[:, 0:ctx_len] == slot_c
        xc_ref[0, 0] = _dot(jnp.where(hit, 1.0, 0.0).astype(BF16),
                            h_ref[0, 0:ctx_len, :]).astype(BF16)
        gc_ref[0, 0] = jnp.sum(jnp.where(hit, aff[:, 0:ctx_len], 0.0), axis=1, keepdims=True)


def _gather_rows(cnt, posm, aff, h2, ctx_len, cap_l, cap_c):
    B, E, S = posm.shape
    kern = functools.partial(_gather_kernel, ctx_len=ctx_len, cap_l=cap_l, cap_c=cap_c)
    omap = lambda b, e, cnt_ref: (e, b, 0, 0)
    caps = [cap_l] + ([cap_c] if cap_c else [])
    out_specs = []
    out_shape = []
    for cap in caps:
        out_specs += [pl.BlockSpec((1, 1, cap, D_MODEL), omap), pl.BlockSpec((1, 1, cap, 1), omap)]
        out_shape += [jax.ShapeDtypeStruct((E, B, cap, D_MODEL), BF16),
                      jax.ShapeDtypeStruct((E, B, cap, 1), F32)]
    grid_spec = pltpu.PrefetchScalarGridSpec(
        num_scalar_prefetch=1,
        grid=(B, E),
        in_specs=[pl.BlockSpec((1, E, S), lambda b, e, cnt_ref: (b, 0, 0)),
                  pl.BlockSpec((1, E, S), lambda b, e, cnt_ref: (b, 0, 0)),
                  pl.BlockSpec((1, S, D_MODEL), lambda b, e, cnt_ref: (b, 0, 0))],
        out_specs=out_specs,
        scratch_shapes=[pltpu.VMEM((cap_l, D_MODEL), F32), pltpu.VMEM((cap_l, 1), F32)],
    )
    return pl.pallas_call(
        kern,
        grid_spec=grid_spec,
        out_shape=out_shape,
        name="gather_rows",
        compiler_params=_cparams(("arbitrary", "arbitrary")),
    )(cnt, posm, aff, h2)


def _expert_kernel(*refs, row_chunk, n_sets):
    x_refs = refs[:n_sets]
    g_refs = refs[n_sets:2 * n_sets]
    wg_ref, wu_ref, wd_ref = refs[2 * n_sets:2 * n_sets + 3]
    y_refs = refs[2 * n_sets + 3:3 * n_sets + 3]
    acc_refs = refs[3 * n_sets + 3:]
    f = pl.program_id(1)
    nf = pl.num_programs(1)
    wg = wg_ref[0, 0].astype(BF16)
    wu = wu_ref[0, 0].astype(BF16)
    wd = wd_ref[0, 0].astype(BF16)

    @pl.when(f == 0)
    def _():
        for acc_ref in acc_refs:
            acc_ref[...] = jnp.zeros_like(acc_ref)

    for x_ref, acc_ref in zip(x_refs, acc_refs):
        rows = x_ref.shape[1]
        step = min(row_chunk, rows)
        for r in range(rows // step):
            rs = slice(r * step, (r + 1) * step)
            x = x_ref[0, rs, :]
            g = _dot(x, wg)
            u = _dot(x, wu)
            hid = (g * jax.nn.sigmoid(g) * u).astype(BF16)
            acc_ref[rs, :] += _dot(hid, wd)

    @pl.when(f == nf - 1)
    def _():
        for g_ref, y_ref, acc_ref in zip(g_refs, y_refs, acc_refs):
            y_ref[0] = (acc_ref[...] * g_ref[0]).astype(BF16)


def _expert_mlp(x_sets, g_sets, w_gate, w_up, w_down, layer, row_chunk):
    E = N_EXPERTS
    nf = D_EXPERT // FF_TILE
    n_sets = len(x_sets)
    kern = functools.partial(_expert_kernel, row_chunk=row_chunk, n_sets=n_sets)
    xspec = lambda r: pl.BlockSpec((1, r, D_MODEL), lambda e, f: (e, 0, 0))
    gspec = lambda r: pl.BlockSpec((1, r, 1), lambda e, f: (e, 0, 0))
    return pl.pallas_call(
        kern,
        grid=(E, nf),
        in_specs=[xspec(xs.shape[1]) for xs in x_sets] + [gspec(xs.shape[1]) for xs in x_sets] + [
            pl.BlockSpec((1, 1, D_MODEL, FF_TILE), lambda e, f: (layer, e, 0, f)),
            pl.BlockSpec((1, 1, D_MODEL, FF_TILE), lambda e, f: (layer, e, 0, f)),
            pl.BlockSpec((1, 1, FF_TILE, D_MODEL), lambda e, f: (layer, e, f, 0))],
        out_specs=[xspec(xs.shape[1]) for xs in x_sets],
        out_shape=[jax.ShapeDtypeStruct(xs.shape, BF16) for xs in x_sets],
        scratch_shapes=[pltpu.VMEM(xs.shape[1:], F32) for xs in x_sets],
        name="expert_mlp",
        compiler_params=_cparams(("arbitrary", "arbitrary")),
    )(*x_sets, *g_sets, w_gate, w_up, w_down)


def _combine_kernel(cnt_ref, x_ref, pos_ref, *refs, cap_l, cap_c, final):
    if cap_c:
        yl_ref, yc_ref, mod_ref, fg_ref, o_ref, acc_ref = refs
    else:
        yl_ref, mod_ref, fg_ref, o_ref, acc_ref = refs
    b = pl.program_id(0)
    i = pl.program_id(1)
    row = jnp.where(i == 0, 4, b) if cap_c else b
    gate = mod_ref[pl.ds(row, 1), 5 * D_MODEL:6 * D_MODEL]
    pos = pos_ref[0]
    acc_ref[...] = jnp.zeros_like(acc_ref)
    sbs = _slot_block(cap_l)

    def add_latent():
        t = i - 1 if cap_c else i
        nsb = cap_l // sbs
        def product(e, sb):
            base = pl.multiple_of(sb * sbs, sbs)
            lane = lax.broadcasted_iota(I32, (ROW_TILE, sbs), 1) + base
            onehot = jnp.where(pos[:, e:e + 1] == lane, 1.0, 0.0).astype(BF16)
            return _dot(onehot, yl_ref[e, 0, pl.ds(base, sbs), :])

        first = [jnp.minimum(_div_pow2(cnt_ref[b, e, t], sbs), nsb - 1) for e in range(N_EXPERTS)]
        total = product(0, first[0])
        for e in range(1, N_EXPERTS):
            total = total + product(e, first[e])
        acc_ref[...] += total
        for e in range(N_EXPERTS):
            def crossing(e=e):
                acc_ref[...] += product(e, first[e] + 1)

            pl.when(cnt_ref[b, e, t + 1] > (first[e] + 1) * sbs)(crossing)

    def add_context():
        ncol = yc_ref.shape[1]
        lane = lax.broadcasted_iota(I32, (ROW_TILE, ncol), 1)
        for e in range(N_EXPERTS):
            pe = pos[:, e:e + 1]
            tgt = jnp.where(pe >= 0, pe + b * cap_c, -1)
            acc_ref[...] += _dot(jnp.where(tgt == lane, 1.0, 0.0).astype(BF16), yc_ref[e])

    if cap_c:
        pl.when(i == 0)(add_context)
        pl.when(i > 0)(add_latent)
    else:
        add_latent()
    x2 = x_ref[0] + gate * acc_ref[...]
    if final:
        var = jnp.mean(x2 * x2, axis=-1, keepdims=True)
        x2 = x2 * lax.rsqrt(var + EPS) * fg_ref[...]
    o_ref[0] = x2


def _combine(cnt, x1, pos_t, yl, yc, mod_l, final_g, cap_l, cap_c, final):
    B, S, _ = x1.shape
    E = N_EXPERTS
    rmap = lambda b, i, cnt_ref: (b, i, 0)
    const2 = lambda b, i, cnt_ref: (0, 0)
    kern = functools.partial(_combine_kernel, cap_l=cap_l, cap_c=cap_c, final=final)
    y_specs = [pl.BlockSpec((E, 1, cap_l, D_MODEL), lambda b, i, cnt_ref: (0, b, 0, 0),
                            pipeline_mode=pl.Buffered(1))]
    y_args = [yl]
    if cap_c:
        y_specs.append(pl.BlockSpec(yc.shape, lambda b, i, cnt_ref: (0, 0, 0),
                                    pipeline_mode=pl.Buffered(1)))
        y_args.append(yc)
    grid_spec = pltpu.PrefetchScalarGridSpec(
        num_scalar_prefetch=1,
        grid=(B, S // ROW_TILE),
        in_specs=[pl.BlockSpec((1, ROW_TILE, D_MODEL), rmap),
                  pl.BlockSpec((1, ROW_TILE, E), rmap)] + y_specs + [
                  pl.BlockSpec((8, N_MOD * D_MODEL), const2),
                  pl.BlockSpec((1, D_MODEL), const2)],
        out_specs=pl.BlockSpec((1, ROW_TILE, D_MODEL), rmap),
        scratch_shapes=[pltpu.VMEM((ROW_TILE, D_MODEL), F32)],
    )
    return pl.pallas_call(
        kern,
        grid_spec=grid_spec,
        out_shape=jax.ShapeDtypeStruct((B, S, D_MODEL), F32),
        name="combine",
        compiler_params=_cparams(("arbitrary", "arbitrary")),
    )(cnt, x1, pos_t, *y_args, mod_l, final_g)


def _rope_tables(n_lat, ctx_len):
    t = np.arange(n_lat)
    pos = np.stack([t // GRID_W, t % GRID_W], axis=-1).astype(np.float32)
    inv = (ROPE_BASE ** (-np.arange(ROPE_FREQS, dtype=np.float32) / ROPE_FREQS)).astype(np.float32)
    ang = pos[:, :, None] * inv
    cos = np.cos(ang).astype(np.float32)
    sin = np.sin(ang).astype(np.float32)
    cos64 = np.concatenate([cos[:, 0], cos[:, 0], cos[:, 1], cos[:, 1]], axis=-1)
    sin64 = np.concatenate([-sin[:, 0], sin[:, 0], -sin[:, 1], sin[:, 1]], axis=-1)
    cos_t = np.concatenate([np.ones((ctx_len, 64), np.float32), cos64], axis=0)
    sin_t = np.concatenate([np.zeros((ctx_len, 64), np.float32), sin64], axis=0)
    return (jnp.asarray(np.tile(cos_t, (1, 2))), jnp.asarray(np.tile(sin_t, (1, 2))))


def kernel(x, c, ctx, c_ctx, ada_w, ada_b, norm1_g, norm2_g, w_in, attn_sink, hgrn_lb,
           hgrn_norm_g, conv_w, w_o, router_w, exp_w_gate, exp_w_up, exp_w_down, final_norm_g):
    B, T, D = x.shape
    L = ctx.shape[1]
    depth = ada_w.shape[0]
    assert D == D_MODEL and L == ROW_TILE and T % ROW_TILE == 0 and B <= 4
    S = L + T
    cap_l = EC_CAPACITY * T // N_EXPERTS
    cap_c = EC_CAPACITY * L // N_EXPERTS
    assert cap_l % 16 == 0 and cap_c % 16 == 0 and (B * cap_c) % 16 == 0

    cos_t, sin_t = _rope_tables(T, L)
    gamma = jax.nn.softmax(hgrn_lb.astype(F32), axis=0)
    lb_all = jnp.cumsum(gamma, axis=0) - gamma[0]
    cvec = jnp.concatenate([c, jnp.zeros((4 - B, D), F32), c_ctx[None],
                            jnp.zeros((3, D), F32)], axis=0)
    mod = _modulation(cvec, ada_w, ada_b)
    xs = jnp.concatenate([ctx, x], axis=1)

    def interleave_heads(w, axis):
        shp = w.shape
        w = w.reshape(shp[:axis] + (N_KV, GQA_GROUP, HEAD_DIM) + shp[axis + 1:])
        return jnp.swapaxes(w, axis, axis + 1).reshape(shp)

    for l in range(depth):
        last = l == depth - 1
        w_in_l = w_in[l].astype(BF16)
        w_in_l = jnp.concatenate([w_in_l[:, :1024], interleave_heads(w_in_l[:, 1024:1536], 1),
                                  w_in_l[:, 1536:]], axis=1)
        w_o_l = w_o[l].astype(BF16)
        w_o_l = jnp.concatenate([interleave_heads(w_o_l[:ATT_W], 0), w_o_l[ATT_W:]], axis=0)
        q, kv, hg, hqg, cv = _in_projection(xs, mod[l], norm1_g[l][None], w_in_l, cos_t, sin_t)
        att = _attention(q, kv, attn_sink[l], L, skip_ctx=last)
        o2 = _hgrn(hg, hqg, lb_all[l], L)
        x1, h2, aff = _mix_out(xs, att, o2, hqg, cv, mod[l], hgrn_norm_g[l][None], conv_w[l],
                               w_o_l, norm2_g[l][None], router_w[l].T, skip_ctx=last)
        lm, cc = (0, 0) if last else (L, cap_c)
        segments = ((lm, T, cap_l),) if last else ((0, L, cap_c), (L, T, cap_l))
        posm, cnt = _topk_positions(aff, segments)
        gathered = _gather_rows(cnt, posm, aff, h2, lm, cap_l, cc)
        x_sets = [a.reshape(N_EXPERTS, -1, D) for a in gathered[0::2]]
        g_sets = [a.reshape(N_EXPERTS, -1, 1) for a in gathered[1::2]]
        ys = _expert_mlp(x_sets, g_sets, exp_w_gate, exp_w_up, exp_w_down, l, cap_l)
        yl = ys[0].reshape(N_EXPERTS, B, cap_l, D)
        xs = _combine(cnt, x1, jnp.swapaxes(posm, 1, 2), yl, None if last else ys[1], mod[l],
                      final_norm_g[None], cap_l, cc, final=last)
    return xs
```

```python
import functools

import numpy as np
import jax
import jax.numpy as jnp
from jax import lax
from jax.experimental import pallas as pl
from jax.experimental.pallas import tpu as pltpu

F32 = jnp.float32
BF16 = jnp.bfloat16
I32 = jnp.int32
HIGHEST = lax.Precision.HIGHEST

D_MODEL = 1024
GRID_W = 64
EPS = 1e-6
LB_FLOOR = 1e-30
N_MOD = 6
ATT_W = 512
HG_W = 256
CV_W = 256
HEAD_DIM = 64
N_Q = 8
N_KV = 2
GQA_GROUP = 4
KV_W = 128
ROPE_BASE = 10000.0
ROPE_FREQS = 16
HG_HEADS = 4
N_EXPERTS = 16
EC_CAPACITY = 2
D_EXPERT = 2048
IN_COLS = 2816

ROW_TILE = 256
ATT_BLOCK = 128
HG_CHUNK = 128
HG_LEVELS = 7
HG_BATCH = 2
HG_LOCAL = 32
HG_LOCAL_MAX_LOG = 80.0
SLOT_BLOCK = 256
FF_TILE = 512
MOD_TILE = 1536
VMEM_LIMIT = 56 * 1024 * 1024

NEG_BIG = -1e30


def _cparams(sem):
    return pltpu.CompilerParams(dimension_semantics=sem, vmem_limit_bytes=VMEM_LIMIT)


def _dot(a, b):
    return jnp.dot(a, b, preferred_element_type=F32)


def _dot_nt(a, b):
    return lax.dot_general(a, b, (((1,), (1,)), ((), ())), preferred_element_type=F32)


def _mod_kernel(a_ref, w_ref, b_ref, o_ref):
    a = a_ref[...]
    a = a * jax.nn.sigmoid(a)
    o_ref[0] = jnp.dot(a, w_ref[0], precision=HIGHEST, preferred_element_type=F32) + b_ref[0]


def _modulation(cvec, ada_w, ada_b):
    depth = ada_w.shape[0]
    ncol = ada_w.shape[2]
    return pl.pallas_call(
        _mod_kernel,
        grid=(depth, ncol // MOD_TILE),
        in_specs=[
            pl.BlockSpec((8, D_MODEL), lambda l, j: (0, 0)),
            pl.BlockSpec((1, D_MODEL, MOD_TILE), lambda l, j: (l, 0, j)),
            pl.BlockSpec((1, 1, MOD_TILE), lambda l, j: (l, 0, j)),
        ],
        out_specs=pl.BlockSpec((1, 8, MOD_TILE), lambda l, j: (l, 0, j)),
        out_shape=jax.ShapeDtypeStruct((depth, 8, ncol), F32),
        name="modulation",
        compiler_params=_cparams(("arbitrary", "arbitrary")),
    )(cvec, ada_w, ada_b.reshape(depth, 1, ncol))


def _swap_halves(x):
    n = x.shape[-1]
    lane = lax.broadcasted_iota(I32, x.shape, x.ndim - 1)
    up = pltpu.roll(x, n - ROPE_FREQS, x.ndim - 1)
    dn = pltpu.roll(x, ROPE_FREQS, x.ndim - 1)
    return jnp.where((lane % (2 * ROPE_FREQS)) < ROPE_FREQS, up, dn)


def _rmsnorm_mod(x, g, shift, scale):
    var = jnp.mean(x * x, axis=-1, keepdims=True)
    y = x * lax.rsqrt(var + EPS) * g
    return y * (1.0 + scale) + shift


def _row_sources(src):
    if isinstance(src, tuple):
        ctx, lat = src
        return ctx, lat, 1, ctx.shape[1] + lat.shape[1]
    return src, src, 0, src.shape[1]


def _inproj_kernel(xc_ref, x_ref, mod_ref, g_ref, w_ref, cos_ref, sin_ref,
                   q_ref, kv_ref, hg_ref, hqg_ref, cv_ref):
    b = pl.program_id(0)
    i = pl.program_id(1)
    row = jnp.where(i == 0, 4, b)
    shift = mod_ref[pl.ds(row, 1), 0:D_MODEL]
    scale = mod_ref[pl.ds(row, 1), D_MODEL:2 * D_MODEL]
    xin = jnp.where(i == 0, xc_ref[0], x_ref[0])
    h = _rmsnorm_mod(xin, g_ref[...], shift, scale)
    p = _dot(h.astype(BF16), w_ref[...])
    cos2 = cos_ref[...]
    sin2 = sin_ref[...]
    k = p[:, 0:KV_W]
    k = k * cos2 + _swap_halves(k) * sin2
    kv_ref[0, :, 0:KV_W] = k.astype(BF16)
    kv_ref[0, :, KV_W:2 * KV_W] = p[:, KV_W:2 * KV_W].astype(BF16)
    hg_ref[0] = p[:, 256:1024]
    q = p[:, 1024:1536]
    cos8 = jnp.concatenate([cos2] * 4, axis=1)
    sin8 = jnp.concatenate([sin2] * 4, axis=1)
    q = (q * cos8 + _swap_halves(q) * sin8) * (HEAD_DIM ** -0.5)
    q_ref[0] = q.astype(BF16)
    hqg_ref[0] = p[:, 1536:2048]
    cv_ref[0] = p[:, 2048:2816]


def _in_projection(src, mod_l, g, w_bf16, cos_t, sin_t):
    xc, xl, off, S = _row_sources(src)
    B = xc.shape[0]
    nt = S // ROW_TILE
    row_map = lambda b, i: (b, i, 0)
    const2 = lambda b, i: (0, 0)
    return pl.pallas_call(
        _inproj_kernel,
        grid=(B, nt),
        in_specs=[
            pl.BlockSpec((1, ROW_TILE, D_MODEL), lambda b, i: (b, 0, 0)),
            pl.BlockSpec((1, ROW_TILE, D_MODEL), lambda b, i: (b, jnp.maximum(i - off, 0), 0)),
            pl.BlockSpec((8, N_MOD * D_MODEL), const2),
            pl.BlockSpec((1, D_MODEL), const2),
            pl.BlockSpec((D_MODEL, IN_COLS), const2),
            pl.BlockSpec((ROW_TILE, 2 * HEAD_DIM), lambda b, i: (i, 0)),
            pl.BlockSpec((ROW_TILE, 2 * HEAD_DIM), lambda b, i: (i, 0)),
        ],
        out_specs=[
            pl.BlockSpec((1, ROW_TILE, ATT_W), row_map),
            pl.BlockSpec((1, ROW_TILE, 2 * KV_W), row_map),
            pl.BlockSpec((1, ROW_TILE, 3 * HG_W), row_map),
            pl.BlockSpec((1, ROW_TILE, 2 * HG_W), row_map),
            pl.BlockSpec((1, ROW_TILE, 3 * CV_W), row_map),
        ],
        out_shape=[
            jax.ShapeDtypeStruct((B, S, ATT_W), BF16),
            jax.ShapeDtypeStruct((B, S, 2 * KV_W), BF16),
            jax.ShapeDtypeStruct((B, S, 3 * HG_W), F32),
            jax.ShapeDtypeStruct((B, S, 2 * HG_W), F32),
            jax.ShapeDtypeStruct((B, S, 3 * CV_W), F32),
        ],
        name="in_projection",
        compiler_params=_cparams(("arbitrary", "arbitrary")),
    )(xc, xl, mod_l, g, w_bf16, cos_t, sin_t)


def _attn_kernel(sink_ref, q_ref, kp_ref, kc_ref, kn_ref, kx_ref, o_ref, *, blk0, nblk, nctx):
    n = pl.program_id(1) + blk0
    is_lat = n >= nctx
    has_prev = n > nctx
    has_next = n < nblk - 1
    W = ATT_BLOCK
    q = q_ref[0]
    qrows = jnp.concatenate([q[:, g * W:(g + 1) * W] for g in range(GQA_GROUP)], axis=0)
    kv_all = jnp.concatenate([kp_ref[0], kc_ref[0], kn_ref[0], kx_ref[0]], axis=0)
    nkeys = kv_all.shape[0]
    k_all = kv_all[:, 0:KV_W]
    v_ext = jnp.concatenate([kv_all[:, KV_W:2 * KV_W], jnp.ones((nkeys, KV_W), BF16)], axis=1)
    rows = GQA_GROUP * W
    ri = lax.broadcasted_iota(I32, (rows, W), 0) % W
    cj = lax.broadcasted_iota(I32, (rows, W), 1)
    m_prev = (cj >= ri) & has_prev
    m_cur = jnp.broadcast_to(is_lat, (rows, W))
    m_next = (cj <= ri) & (has_next & is_lat)
    grp = lax.broadcasted_iota(I32, (rows, 1), 0) // W
    low = cj < HEAD_DIM
    outs = []
    for h in range(N_KV):
        qh = jnp.where(low if h == 0 else jnp.logical_not(low), qrows, jnp.zeros_like(qrows))
        sink = jnp.zeros((rows, 1), F32)
        for g in range(GQA_GROUP):
            sink = jnp.where(grp == g, sink_ref[h * GQA_GROUP + g], sink)
        s = _dot_nt(qh, k_all)
        segs = [jnp.where(m_prev, s[:, 0:W], NEG_BIG),
                jnp.where(m_cur, s[:, W:2 * W], NEG_BIG),
                jnp.where(m_next, s[:, 2 * W:3 * W], NEG_BIG)]
        segs += [s[:, c:c + W] for c in range(3 * W, nkeys, W)]
        mx = segs[0]
        for sg in segs[1:]:
            mx = jnp.maximum(mx, sg)
        m = jnp.maximum(jnp.max(mx, axis=1, keepdims=True), sink)
        p = jnp.concatenate([jnp.exp(sg - m).astype(BF16) for sg in segs], axis=1)
        oe = _dot(p, v_ext)
        den = oe[:, KV_W:2 * KV_W] + jnp.exp(sink - m)
        outs.append(oe[:, 0:KV_W] / den)
    o = jnp.where(low, outs[0], outs[1])
    for g in range(GQA_GROUP):
        o_ref[0, :, g * W:(g + 1) * W] = o[g * W:(g + 1) * W].astype(BF16)


def _attention(q, kv, sink, ctx_len, skip_ctx):
    B, S, _ = q.shape
    nblk = S // ATT_BLOCK
    nctx = ctx_len // ATT_BLOCK
    blk0 = nctx if skip_ctx else 0
    blk = lambda f: (lambda b, n: (b, f(n + blk0), 0))
    kern = functools.partial(_attn_kernel, blk0=blk0, nblk=nblk, nctx=nctx)
    return pl.pallas_call(
        kern,
        grid=(B, nblk - blk0),
        in_specs=[
            pl.BlockSpec(memory_space=pltpu.SMEM),
            pl.BlockSpec((1, ATT_BLOCK, ATT_W), blk(lambda n: n)),
            pl.BlockSpec((1, ATT_BLOCK, 2 * KV_W), blk(lambda n: jnp.maximum(n - 1, 0))),
            pl.BlockSpec((1, ATT_BLOCK, 2 * KV_W), blk(lambda n: n)),
            pl.BlockSpec((1, ATT_BLOCK, 2 * KV_W), blk(lambda n: jnp.minimum(n + 1, nblk - 1))),
            pl.BlockSpec((1, ctx_len, 2 * KV_W), lambda b, n: (b, 0, 0)),
        ],
        out_specs=pl.BlockSpec((1, ATT_BLOCK, ATT_W), lambda b, n: (b, n, 0)),
        out_shape=jax.ShapeDtypeStruct((B, S - blk0 * ATT_BLOCK, ATT_W), BF16),
        name="attention",
        compiler_params=_cparams(("arbitrary", "arbitrary")),
    )(sink, q, kv, kv, kv, kv)


def _hgrn_constants():
    C = HG_CHUNK
    t = np.arange(C)[:, None]
    r = np.arange(C)[None, :]
    tri = np.stack([r <= t, r >= t]).astype(np.float32)
    x = t ^ r
    lvl = np.where(x > 0, np.floor(np.log2(np.maximum(x, 1))).astype(np.int32), HG_LEVELS)
    lvl_f = np.where(t >= r, lvl, -1).astype(np.int32)
    lvl_b = np.where(t <= r, lvl, -1).astype(np.int32)
    lvl2 = np.stack([np.tile(lvl_f, (1, HG_HEADS)), np.tile(lvl_b, (1, HG_HEADS))])
    return tri, lvl2


def _span_row(x, span, row):
    C = x.shape[0]
    if span >= 8:
        x3 = x.reshape(C // span, span, x.shape[1])
        return jnp.broadcast_to(x3[:, row:row + 1, :], x3.shape).reshape(x.shape)
    pos = lax.broadcasted_iota(I32, x.shape, 0) % span
    out = x
    for p in range(span):
        if p != row:
            out = jnp.where(pos == p, pltpu.roll(x, (p - row) % C, 0), out)
    return out


def _hgrn_prepare(v, z, qr, lb, tri, backward):
    C = HG_CHUNK
    logf = jnp.log(jnp.maximum(lb, LB_FLOOR) + (1.0 - lb) * jax.nn.sigmoid(z))
    k = (1.0 - lb) * jax.nn.sigmoid(-z)
    q = qr * jax.nn.sigmoid(qr)
    hi = logf.astype(BF16)
    lo = (logf - hi.astype(F32)).astype(BF16)
    cs = _dot(tri, jnp.concatenate([hi, lo], axis=1))
    lam = cs[:, 0:HG_W] + cs[:, HG_W:2 * HG_W]
    tot = lam[0:1] if backward else lam[C - 1:C]
    local = lam - _span_row(lam - logf, HG_LOCAL, HG_LOCAL - 1 if backward else 0)
    return dict(v=v, q=q, k=k, lam=lam, tot=tot, local=local)


def _hgrn_finish(g, head_masks, same_head, lvl, st_ref, sidx, backward, fast):
    C = HG_CHUNK
    q, k, v, lam, tot = g["q"], g["k"], g["v"], g["lam"], g["tot"]
    zero = jnp.zeros((C, HG_W), BF16)

    def per_head_rows(x):
        return jnp.concatenate([jnp.where(hm, x, zero) for hm in head_masks], axis=0)

    qb = q.astype(BF16)
    kb = k.astype(BF16)
    if fast:
        n_local = HG_LOCAL.bit_length() - 1
        a = jnp.where((lvl == HG_LEVELS) | ((lvl >= 0) & (lvl < n_local)),
                      _dot_nt((q * jnp.exp(g["local"])).astype(BF16),
                              per_head_rows((k * jnp.exp(-g["local"])).astype(BF16))), 0.0)
        levels = range(n_local, HG_LEVELS)
    else:
        a = jnp.where(lvl == HG_LEVELS, _dot_nt(qb, per_head_rows(kb)), 0.0)
        levels = range(HG_LEVELS)
    for l in levels:
        m = 1 << l
        ref = _span_row(lam, 2 * m, m if backward else m - 1)
        fac = jnp.exp(-jnp.abs(lam - ref)).astype(BF16)
        a = jnp.where(lvl == l, _dot_nt(qb * fac, per_head_rows(kb * fac)), a)
    st = st_ref[sidx]
    q_in = (q * jnp.exp(lam)).astype(BF16)
    o = _dot(a.astype(BF16), per_head_rows(v.astype(BF16))) + _dot_nt(q_in, st.astype(BF16))
    k_out = (k * jnp.exp(tot - lam)).astype(BF16)
    upd = _dot(v.T.astype(BF16), k_out)
    st_ref[sidx] = st * jnp.exp(tot) + jnp.where(same_head, upd, 0.0)
    return o


def _hgrn_kernel(vf_ref, zf_ref, qf_ref, vb_ref, zb_ref, qb_ref, lb_ref, tri_ref, lvl_ref,
                 of_ref, ob_ref, st_ref):
    j = pl.program_id(1)

    @pl.when(j == 0)
    def _():
        st_ref[...] = jnp.zeros_like(st_ref)

    head_id = (lax.broadcasted_iota(I32, (HG_CHUNK, HG_W), 1) // HEAD_DIM).astype(F32).astype(BF16)
    head_masks = [head_id == float(h) for h in range(HG_HEADS)]
    same_head = (lax.broadcasted_iota(I32, (HG_W, HG_W), 0) // HEAD_DIM
                 == lax.broadcasted_iota(I32, (HG_W, HG_W), 1) // HEAD_DIM)
    streams = []
    for d, (v_ref, z_ref, q_ref, o_ref) in enumerate(
            ((vf_ref, zf_ref, qf_ref, of_ref), (vb_ref, zb_ref, qb_ref, ob_ref))):
        for s in range(v_ref.shape[0]):
            g = _hgrn_prepare(v_ref[s], z_ref[s], q_ref[s], lb_ref[d], tri_ref[d], d == 1)
            streams.append((d, s, o_ref, g))
    worst = streams[0][3]["local"]
    for _, _, _, g in streams[1:]:
        worst = jnp.minimum(worst, g["local"])
    can_use_local = jnp.min(worst) >= -HG_LOCAL_MAX_LOG

    def run(fast):
        for d, s, o_ref, g in streams:
            o_ref[s] = _hgrn_finish(g, head_masks, same_head, lvl_ref[d], st_ref, 2 * s + d,
                                    backward=(d == 1), fast=fast)

    pl.when(can_use_local)(functools.partial(run, True))
    pl.when(jnp.logical_not(can_use_local))(functools.partial(run, False))


def _hgrn(hg, hqg, lb_l, ctx_len):
    B, S, _ = hg.shape
    nc = S // HG_CHUNK
    nctx = ctx_len // HG_CHUNK
    tri_np, lvl_np = _hgrn_constants()
    tri = jnp.asarray(tri_np, BF16)
    lvl = jnp.asarray(lvl_np, I32)

    def back(j):
        return jnp.where(j < nctx, nctx - 1 - j, nc - 1 - (j - nctx))

    bs = HG_BATCH if B % HG_BATCH == 0 else 1
    blk = (bs, HG_CHUNK, HG_W)
    fwd = lambda col: (lambda b, j: (b, j, col))
    bwd = lambda col: (lambda b, j: (b, back(j), col))
    const3 = lambda b, j: (0, 0, 0)
    return pl.pallas_call(
        _hgrn_kernel,
        grid=(B // bs, nc),
        in_specs=[
            pl.BlockSpec(blk, fwd(0)), pl.BlockSpec(blk, fwd(1)), pl.BlockSpec(blk, fwd(0)),
            pl.BlockSpec(blk, bwd(0)), pl.BlockSpec(blk, bwd(2)), pl.BlockSpec(blk, bwd(0)),
            pl.BlockSpec((2, 1, HG_W), const3),
            pl.BlockSpec((2, HG_CHUNK, HG_CHUNK), const3),
            pl.BlockSpec((2, HG_CHUNK, HG_HEADS * HG_CHUNK), const3),
        ],
        out_specs=[pl.BlockSpec(blk, fwd(0)), pl.BlockSpec(blk, bwd(0))],
        out_shape=[jax.ShapeDtypeStruct((B, S, HG_W), F32)] * 2,
        scratch_shapes=[pltpu.VMEM((2 * bs, HG_W, HG_W), F32)],
        name="hgrn_scan",
        compiler_params=_cparams(("arbitrary", "arbitrary")),
    )(hg, hg, hqg, hg, hg, hqg, lb_l.reshape(2, 1, HG_W), tri, lvl)


def _mixout_kernel(xc_ref, x_ref, att_ref, of_ref, ob_ref, g_ref, cv_ref, cprev_ref, cnext_ref,
                   mod_ref, gain_ref, cw_ref, wo_ref, n2_ref, wr_ref, ones_ref,
                   x1_ref, h2_ref, aff_ref, *, tile0, ntile):
    b = pl.program_id(0)
    i = pl.program_id(1) + tile0
    row = jnp.where(i == 0, 4, b)
    R = ROW_TILE

    def modv(c):
        return mod_ref[pl.ds(row, 1), c * D_MODEL:(c + 1) * D_MODEL]

    o = of_ref[0] + ob_ref[0]
    sq = o * o
    sq_hi = sq.astype(BF16)
    sq_lo = (sq - sq_hi.astype(F32)).astype(BF16)
    ms = _dot(sq_hi, ones_ref[...]) + _dot(sq_lo, ones_ref[...])
    g = g_ref[0]
    hg = o * lax.rsqrt(ms * (1.0 / HEAD_DIM) + EPS) * gain_ref[...] * (g * jax.nn.sigmoid(g))
    cv = cv_ref[0]
    u = cv[:, CV_W:2 * CV_W] * cv[:, 2 * CV_W:3 * CV_W]
    up = cprev_ref[0]
    un = cnext_ref[0]
    u_prev_row = up[7:8, CV_W:2 * CV_W] * up[7:8, 2 * CV_W:3 * CV_W]
    u_next_row = un[0:1, CV_W:2 * CV_W] * un[0:1, 2 * CV_W:3 * CV_W]
    u_prev_row = jnp.where(i <= 1, 0.0, u_prev_row)
    u_next_row = jnp.where((i == 0) | (i == ntile - 1), 0.0, u_next_row)
    ridx = lax.broadcasted_iota(I32, (R, CV_W), 0)
    u_m1 = jnp.where(ridx == 0, u_prev_row, pltpu.roll(u, 1, 0))
    u_p1 = jnp.where(ridx == R - 1, u_next_row, pltpu.roll(u, R - 1, 0))
    cw = cw_ref[...]
    conv = cv[:, 0:CV_W] * (u_m1 * cw[0:1] + u * cw[1:2] + u_p1 * cw[2:3])
    mix = (_dot(att_ref[0], wo_ref[0:ATT_W])
           + _dot(hg.astype(BF16), wo_ref[ATT_W:ATT_W + HG_W])
           + _dot(conv.astype(BF16), wo_ref[ATT_W + HG_W:D_MODEL]))
    x1 = jnp.where(i == 0, xc_ref[0], x_ref[0]) + modv(2) * mix
    x1_ref[0] = x1
    h2 = _rmsnorm_mod(x1, n2_ref[...], modv(3), modv(4))
    h_hi = h2.astype(BF16)
    h2_ref[0] = h_hi
    h_lo = (h2 - h_hi.astype(F32)).astype(BF16)
    wr = wr_ref[...]
    w_hi = wr.astype(BF16)
    w_lo = (wr - w_hi.astype(F32)).astype(BF16)
    r1 = _dot_nt(jnp.concatenate([w_hi, w_lo], axis=0), h_hi)
    logits = r1[0:N_EXPERTS] + r1[N_EXPERTS:2 * N_EXPERTS] + _dot_nt(w_hi, h_lo)
    e = jnp.exp(logits - jnp.max(logits, axis=0, keepdims=True))
    aff_ref[0] = e / jnp.sum(e, axis=0, keepdims=True)


def _mix_out(src, att, o2, hqg, cv, mod_l, gain, conv_w, wo_bf16, n2g, wr_t, skip_ctx):
    xc, xl, off, S = _row_sources(src)
    B = xc.shape[0]
    ntile = S // ROW_TILE
    tile0 = 1 if skip_ctx else 0
    sub = ROW_TILE // 8
    nsub = S // 8
    rmap = lambda b, i: (b, i + tile0, 0)
    omap = lambda b, i: (b, i, 0)
    s_out = S - tile0 * ROW_TILE
    const2 = lambda b, i: (0, 0)
    ones = jnp.asarray(np.kron(np.eye(HG_HEADS), np.ones((HEAD_DIM, HEAD_DIM))), BF16)
    kern = functools.partial(_mixout_kernel, tile0=tile0, ntile=ntile)
    return pl.pallas_call(
        kern,
        grid=(B, ntile - tile0),
        in_specs=[
            pl.BlockSpec((1, ROW_TILE, D_MODEL), lambda b, i: (b, 0, 0)),
            pl.BlockSpec((1, ROW_TILE, D_MODEL),
                         lambda b, i: (b, jnp.maximum(i + tile0 - off, 0), 0)),
            pl.BlockSpec((1, ROW_TILE, ATT_W), omap),
            pl.BlockSpec((1, ROW_TILE, HG_W), rmap),
            pl.BlockSpec((1, ROW_TILE, HG_W), rmap),
            pl.BlockSpec((1, ROW_TILE, HG_W), lambda b, i: (b, i + tile0, 1)),
            pl.BlockSpec((1, ROW_TILE, 3 * CV_W), rmap),
            pl.BlockSpec((1, 8, 3 * CV_W),
                         lambda b, i: (b, jnp.maximum((i + tile0) * sub - 1, 0), 0)),
            pl.BlockSpec((1, 8, 3 * CV_W),
                         lambda b, i: (b, jnp.minimum((i + tile0 + 1) * sub, nsub - 1), 0)),
            pl.BlockSpec((8, N_MOD * D_MODEL), const2),
            pl.BlockSpec((1, HG_W), const2),
            pl.BlockSpec((3, CV_W), const2),
            pl.BlockSpec((D_MODEL, D_MODEL), const2),
            pl.BlockSpec((1, D_MODEL), const2),
            pl.BlockSpec((N_EXPERTS, D_MODEL), const2),
            pl.BlockSpec((HG_W, HG_W), const2),
        ],
        out_specs=[
            pl.BlockSpec((1, ROW_TILE, D_MODEL), omap),
            pl.BlockSpec((1, ROW_TILE, D_MODEL), omap),
            pl.BlockSpec((1, N_EXPERTS, ROW_TILE), lambda b, i: (b, 0, i)),
        ],
        out_shape=[
            jax.ShapeDtypeStruct((B, s_out, D_MODEL), F32),
            jax.ShapeDtypeStruct((B, s_out, D_MODEL), BF16),
            jax.ShapeDtypeStruct((B, N_EXPERTS, s_out), F32),
        ],
        name="mix_out",
        compiler_params=_cparams(("arbitrary", "arbitrary")),
    )(xc, xl, att, o2[0], o2[1], hqg, cv, cv, cv, mod_l, gain, conv_w, wo_bf16, n2g, wr_t, ones)


def _topk_kernel(aff_ref, tri_ref, pos_ref, cnt_ref, *, segments):
    tri = tri_ref[...]

    def excl_cumsum(mask, n):
        carry = jnp.zeros((N_EXPERTS, 1), F32)
        parts = []
        carries = []
        for c in range(n // 128):
            carries.append(carry)
            blk = jnp.where(mask[:, c * 128:(c + 1) * 128], 1.0, 0.0)
            parts.append(_dot(blk.astype(BF16), tri) + carry)
            carry = carry + jnp.sum(blk, axis=1, keepdims=True)
        return jnp.concatenate(parts, axis=1), carries + [carry]

    for (lo, n, k) in segments:
        a = aff_ref[0, :, lo:lo + n]

        def body(it, thr):
            cand = thr | jnp.left_shift(jnp.int32(1), 30 - it)
            cnt = jnp.sum(jnp.where(a >= pltpu.bitcast(cand, F32), 1.0, 0.0), axis=1, keepdims=True)
            return jnp.where(cnt >= k, cand, thr)

        thr = lax.fori_loop(0, 31, body, jnp.zeros((N_EXPERTS, 1), I32))
        above = a >= pltpu.bitcast(thr + 1, F32)
        tied = jnp.logical_and(a >= pltpu.bitcast(thr, F32), jnp.logical_not(above))
        n_above = jnp.sum(jnp.where(above, 1.0, 0.0), axis=1, keepdims=True)
        rank_tied, _ = excl_cumsum(tied, n)
        sel = above | (tied & (rank_tied < (k - n_above)))
        pos, counts = excl_cumsum(sel, n)
        pos_ref[0, :, lo:lo + n] = jnp.where(sel, pos.astype(I32), -1)
    step = ROW_TILE // 128
    lane = lax.broadcasted_iota(I32, (N_EXPERTS, 128), 1)
    cnt = jnp.zeros((N_EXPERTS, 128), F32)
    for c, col in enumerate(counts[0::step]):
        cnt = jnp.where(lane == c, col, cnt)
    cnt_ref[0] = cnt.astype(I32)


def _topk_positions(aff, segments):
    B, E, S = aff.shape
    assert segments[-1][1] // ROW_TILE + 1 <= 128
    tri = jnp.asarray(np.triu(np.ones((128, 128)), 1), BF16)
    kern = functools.partial(_topk_kernel, segments=segments)
    return pl.pallas_call(
        kern,
        grid=(B,),
        in_specs=[pl.BlockSpec((1, E, S), lambda b: (b, 0, 0)),
                  pl.BlockSpec((128, 128), lambda b: (0, 0))],
        out_specs=[pl.BlockSpec((1, E, S), lambda b: (b, 0, 0)),
                   pl.BlockSpec((1, E, 128), lambda b: (b, 0, 0))],
        out_shape=[jax.ShapeDtypeStruct((B, E, S), I32),
                   jax.ShapeDtypeStruct((B, E, 128), I32)],
        name="topk_positions",
        compiler_params=_cparams(("arbitrary",)),
    )(aff, tri)


def _slot_block(cap):
    sbs = min(SLOT_BLOCK, cap)
    assert sbs >= ROW_TILE or sbs == cap
    return sbs


def _div_pow2(x, d):
    assert d & (d - 1) == 0
    return lax.shift_right_logical(x, jnp.int32(d.bit_length() - 1))


def _gather_kernel(cnt_ref, pos_ref, aff_ref, h_ref, *refs, ctx_len, cap_l, cap_c):
    if cap_c:
        xl_ref, gl_ref, xc_ref, gc_ref, acc_ref, gacc_ref = refs
    else:
        xl_ref, gl_ref, acc_ref, gacc_ref = refs
    b = pl.program_id(0)
    e = pl.program_id(1)
    S = h_ref.shape[1]
    pos = pos_ref[0, pl.ds(e, 1), :]
    aff = aff_ref[0, pl.ds(e, 1), :]
    nchunk = (S - ctx_len) // ROW_TILE
    sbs = _slot_block(cap_l)
    acc_ref[...] = jnp.zeros_like(acc_ref)
    gacc_ref[...] = jnp.zeros_like(gacc_ref)
    nsb = cap_l // sbs
    first = [jnp.minimum(_div_pow2(cnt_ref[b, e, c], sbs), nsb - 1) for c in range(nchunk)]
    slot_i32 = lax.broadcasted_iota(I32, (sbs, ROW_TILE), 0)
    slot_in_block = slot_i32.astype(F32).astype(BF16)
    one = jnp.ones((sbs, ROW_TILE), BF16)
    zero = jnp.zeros((sbs, ROW_TILE), BF16)

    def visit(c, sb):
        lo = ctx_len + c * ROW_TILE
        base = pl.multiple_of(sb * sbs, sbs)
        rel_i = pos[:, lo:lo + ROW_TILE] - base
        rel = rel_i.astype(F32)
        rel = jnp.where((rel >= 0.0) & (rel < float(sbs)), rel, -1.0).astype(BF16)
        acc_ref[pl.ds(base, sbs), :] += _dot(jnp.where(rel == slot_in_block, one, zero),
                                             h_ref[0, lo:lo + ROW_TILE, :])
        gacc_ref[pl.ds(base, sbs), :] += jnp.sum(
            jnp.where(rel_i == slot_i32, aff[:, lo:lo + ROW_TILE], 0.0), axis=1, keepdims=True)

    for c in range(nchunk):
        visit(c, first[c])
    for c in range(nchunk):
        pl.when(cnt_ref[b, e, c + 1] > (first[c] + 1) * sbs)(
            functools.partial(visit, c, first[c] + 1))
    xl_ref[0, 0] = acc_ref[...].astype(BF16)
    gl_ref[0, 0] = gacc_ref[...]
    if cap_c:
        slot_c = lax.broadcasted_iota(I32, (cap_c, ctx_len), 0)
        hit = pos[:, 0:ctx_len] == slot_c
        xc_ref[0, 0] = _dot(jnp.where(hit, 1.0, 0.0).astype(BF16),
                            h_ref[0, 0:ctx_len, :]).astype(BF16)
        gc_ref[0, 0] = jnp.sum(jnp.where(hit, aff[:, 0:ctx_len], 0.0), axis=1, keepdims=True)


def _gather_rows(cnt, posm, aff, h2, ctx_len, cap_l, cap_c):
    B, E, S = posm.shape
    kern = functools.partial(_gather_kernel, ctx_len=ctx_len, cap_l=cap_l, cap_c=cap_c)
    omap = lambda b, e, cnt_ref: (e, b, 0, 0)
    caps = [cap_l] + ([cap_c] if cap_c else [])
    out_specs = []
    out_shape = []
    for cap in caps:
        out_specs += [pl.BlockSpec((1, 1, cap, D_MODEL), omap), pl.BlockSpec((1, 1, cap, 1), omap)]
        out_shape += [jax.ShapeDtypeStruct((E, B, cap, D_MODEL), BF16),
                      jax.ShapeDtypeStruct((E, B, cap, 1), F32)]
    grid_spec = pltpu.PrefetchScalarGridSpec(
        num_scalar_prefetch=1,
        grid=(B, E),
        in_specs=[pl.BlockSpec((1, E, S), lambda b, e, cnt_ref: (b, 0, 0)),
                  pl.BlockSpec((1, E, S), lambda b, e, cnt_ref: (b, 0, 0)),
                  pl.BlockSpec((1, S, D_MODEL), lambda b, e, cnt_ref: (b, 0, 0))],
        out_specs=out_specs,
        scratch_shapes=[pltpu.VMEM((cap_l, D_MODEL), F32), pltpu.VMEM((cap_l, 1), F32)],
    )
    return pl.pallas_call(
        kern,
        grid_spec=grid_spec,
        out_shape=out_shape,
        name="gather_rows",
        compiler_params=_cparams(("arbitrary", "arbitrary")),
    )(cnt, posm, aff, h2)


def _expert_kernel(*refs, row_chunk, n_sets):
    x_refs = refs[:n_sets]
    g_refs = refs[n_sets:2 * n_sets]
    wg_ref, wu_ref, wd_ref = refs[2 * n_sets:2 * n_sets + 3]
    y_refs = refs[2 * n_sets + 3:3 * n_sets + 3]
    acc_refs = refs[3 * n_sets + 3:]
    f = pl.program_id(1)
    nf = pl.num_programs(1)
    wg = wg_ref[0, 0].astype(BF16)
    wu = wu_ref[0, 0].astype(BF16)
    wd = wd_ref[0, 0].astype(BF16)

    @pl.when(f == 0)
    def _():
        for acc_ref in acc_refs:
            acc_ref[...] = jnp.zeros_like(acc_ref)

    for x_ref, acc_ref in zip(x_refs, acc_refs):
        rows = x_ref.shape[1]
        step = min(row_chunk, rows)
        for r in range(rows // step):
            rs = slice(r * step, (r + 1) * step)
            x = x_ref[0, rs, :]
            g = _dot(x, wg)
            u = _dot(x, wu)
            hid = (g * jax.nn.sigmoid(g) * u).astype(BF16)
            acc_ref[rs, :] += _dot(hid, wd)

    @pl.when(f == nf - 1)
    def _():
        for g_ref, y_ref, acc_ref in zip(g_refs, y_refs, acc_refs):
            y_ref[0] = (acc_ref[...] * g_ref[0]).astype(BF16)


def _expert_mlp(x_sets, g_sets, w_gate, w_up, w_down, layer, row_chunk):
    E = N_EXPERTS
    nf = D_EXPERT // FF_TILE
    n_sets = len(x_sets)
    kern = functools.partial(_expert_kernel, row_chunk=row_chunk, n_sets=n_sets)
    xspec = lambda r: pl.BlockSpec((1, r, D_MODEL), lambda e, f: (e, 0, 0))
    gspec = lambda r: pl.BlockSpec((1, r, 1), lambda e, f: (e, 0, 0))
    return pl.pallas_call(
        kern,
        grid=(E, nf),
        in_specs=[xspec(xs.shape[1]) for xs in x_sets] + [gspec(xs.shape[1]) for xs in x_sets] + [
            pl.BlockSpec((1, 1, D_MODEL, FF_TILE), lambda e, f: (layer, e, 0, f)),
            pl.BlockSpec((1, 1, D_MODEL, FF_TILE), lambda e, f: (layer, e, 0, f)),
            pl.BlockSpec((1, 1, FF_TILE, D_MODEL), lambda e, f: (layer, e, f, 0))],
        out_specs=[xspec(xs.shape[1]) for xs in x_sets],
        out_shape=[jax.ShapeDtypeStruct(xs.shape, BF16) for xs in x_sets],
        scratch_shapes=[pltpu.VMEM(xs.shape[1:], F32) for xs in x_sets],
        name="expert_mlp",
        compiler_params=_cparams(("arbitrary", "arbitrary")),
    )(*x_sets, *g_sets, w_gate, w_up, w_down)


def _combine_kernel(cnt_ref, x_ref, pos_ref, *refs, cap_l, cap_c, final):
    if cap_c:
        yl_ref, yc_ref, mod_ref, fg_ref, o_ref, acc_ref = refs
    else:
        yl_ref, mod_ref, fg_ref, o_ref, acc_ref = refs
    b = pl.program_id(0)
    i = pl.program_id(1)
    row = jnp.where(i == 0, 4, b) if cap_c else b
    gate = mod_ref[pl.ds(row, 1), 5 * D_MODEL:6 * D_MODEL]
    pos = pos_ref[0]
    acc_ref[...] = jnp.zeros_like(acc_ref)
    sbs = _slot_block(cap_l)

    def add_latent():
        t = i - 1 if cap_c else i
        nsb = cap_l // sbs
        def product(e, sb):
            base = pl.multiple_of(sb * sbs, sbs)
            lane = lax.broadcasted_iota(I32, (ROW_TILE, sbs), 1) + base
            onehot = jnp.where(pos[:, e:e + 1] == lane, 1.0, 0.0).astype(BF16)
            return _dot(onehot, yl_ref[e, 0, pl.ds(base, sbs), :])

        first = [jnp.minimum(_div_pow2(cnt_ref[b, e, t], sbs), nsb - 1) for e in range(N_EXPERTS)]
        total = product(0, first[0])
        for e in range(1, N_EXPERTS):
            total = total + product(e, first[e])
        acc_ref[...] += total
        for e in range(N_EXPERTS):
            def crossing(e=e):
                acc_ref[...] += product(e, first[e] + 1)

            pl.when(cnt_ref[b, e, t + 1] > (first[e] + 1) * sbs)(crossing)

    def add_context():
        ncol = yc_ref.shape[1]
        lane = lax.broadcasted_iota(I32, (ROW_TILE, ncol), 1)
        for e in range(N_EXPERTS):
            pe = pos[:, e:e + 1]
            tgt = jnp.where(pe >= 0, pe + b * cap_c, -1)
            acc_ref[...] += _dot(jnp.where(tgt == lane, 1.0, 0.0).astype(BF16), yc_ref[e])

    if cap_c:
        pl.when(i == 0)(add_context)
        pl.when(i > 0)(add_latent)
    else:
        add_latent()
    x2 = x_ref[0] + gate * acc_ref[...]
    if final:
        var = jnp.mean(x2 * x2, axis=-1, keepdims=True)
        x2 = x2 * lax.rsqrt(var + EPS) * fg_ref[...]
    o_ref[0] = x2


def _combine(cnt, x1, pos_t, yl, yc, mod_l, final_g, cap_l, cap_c, final):
    B, S, _ = x1.shape
    E = N_EXPERTS
    rmap = lambda b, i, cnt_ref: (b, i, 0)
    const2 = lambda b, i, cnt_ref: (0, 0)
    kern = functools.partial(_combine_kernel, cap_l=cap_l, cap_c=cap_c, final=final)
    y_specs = [pl.BlockSpec((E, 1, cap_l, D_MODEL), lambda b, i, cnt_ref: (0, b, 0, 0),
                            pipeline_mode=pl.Buffered(1))]
    y_args = [yl]
    if cap_c:
        y_specs.append(pl.BlockSpec(yc.shape, lambda b, i, cnt_ref: (0, 0, 0),
                                    pipeline_mode=pl.Buffered(1)))
        y_args.append(yc)
    grid_spec = pltpu.PrefetchScalarGridSpec(
        num_scalar_prefetch=1,
        grid=(B, S // ROW_TILE),
        in_specs=[pl.BlockSpec((1, ROW_TILE, D_MODEL), rmap),
                  pl.BlockSpec((1, ROW_TILE, E), rmap)] + y_specs + [
                  pl.BlockSpec((8, N_MOD * D_MODEL), const2),
                  pl.BlockSpec((1, D_MODEL), const2)],
        out_specs=pl.BlockSpec((1, ROW_TILE, D_MODEL), rmap),
        scratch_shapes=[pltpu.VMEM((ROW_TILE, D_MODEL), F32)],
    )
    return pl.pallas_call(
        kern,
        grid_spec=grid_spec,
        out_shape=jax.ShapeDtypeStruct((B, S, D_MODEL), F32),
        name="combine",
        compiler_params=_cparams(("arbitrary", "arbitrary")),
    )(cnt, x1, pos_t, *y_args, mod_l, final_g)


def _rope_tables(n_lat, ctx_len):
    t = np.arange(n_lat)
    pos = np.stack([t // GRID_W, t % GRID_W], axis=-1).astype(np.float32)
    inv = (ROPE_BASE ** (-np.arange(ROPE_FREQS, dtype=np.float32) / ROPE_FREQS)).astype(np.float32)
    ang = pos[:, :, None] * inv
    cos = np.cos(ang).astype(np.float32)
    sin = np.sin(ang).astype(np.float32)
    cos64 = np.concatenate([cos[:, 0], cos[:, 0], cos[:, 1], cos[:, 1]], axis=-1)
    sin64 = np.concatenate([-sin[:, 0], sin[:, 0], -sin[:, 1], sin[:, 1]], axis=-1)
    cos_t = np.concatenate([np.ones((ctx_len, 64), np.float32), cos64], axis=0)
    sin_t = np.concatenate([np.zeros((ctx_len, 64), np.float32), sin64], axis=0)
    return (jnp.asarray(np.tile(cos_t, (1, 2))), jnp.asarray(np.tile(sin_t, (1, 2))))


def kernel(x, c, ctx, c_ctx, ada_w, ada_b, norm1_g, norm2_g, w_in, attn_sink, hgrn_lb,
           hgrn_norm_g, conv_w, w_o, router_w, exp_w_gate, exp_w_up, exp_w_down, final_norm_g):
    B, T, D = x.shape
    L = ctx.shape[1]
    depth = ada_w.shape[0]
    assert D == D_MODEL and L == ROW_TILE and T % ROW_TILE == 0 and B <= 4
    S = L + T
    cap_l = EC_CAPACITY * T // N_EXPERTS
    cap_c = EC_CAPACITY * L // N_EXPERTS
    assert cap_l % 16 == 0 and cap_c % 16 == 0 and (B * cap_c) % 16 == 0

    cos_t, sin_t = _rope_tables(T, L)
    gamma = jax.nn.softmax(hgrn_lb.astype(F32), axis=0)
    lb_all = jnp.cumsum(gamma, axis=0) - gamma[0]
    cvec = jnp.concatenate([c, jnp.zeros((4 - B, D), F32), c_ctx[None],
                            jnp.zeros((3, D), F32)], axis=0)
    mod = _modulation(cvec, ada_w, ada_b)
    xs = (ctx, x)

    def interleave_heads(w, axis):
        shp = w.shape
        w = w.reshape(shp[:axis] + (N_KV, GQA_GROUP, HEAD_DIM) + shp[axis + 1:])
        return jnp.swapaxes(w, axis, axis + 1).reshape(shp)

    for l in range(depth):
        last = l == depth - 1
        w_in_l = w_in[l].astype(BF16)
        w_in_l = jnp.concatenate([w_in_l[:, :1024], interleave_heads(w_in_l[:, 1024:1536], 1),
                                  w_in_l[:, 1536:]], axis=1)
        w_o_l = w_o[l].astype(BF16)
        w_o_l = jnp.concatenate([interleave_heads(w_o_l[:ATT_W], 0), w_o_l[ATT_W:]], axis=0)
        q, kv, hg, hqg, cv = _in_projection(xs, mod[l], norm1_g[l][None], w_in_l, cos_t, sin_t)
        att = _attention(q, kv, attn_sink[l], L, skip_ctx=last)
        o2 = _hgrn(hg, hqg, lb_all[l], L)
        x1, h2, aff = _mix_out(xs, att, o2, hqg, cv, mod[l], hgrn_norm_g[l][None], conv_w[l],
                               w_o_l, norm2_g[l][None], router_w[l].T, skip_ctx=last)
        lm, cc = (0, 0) if last else (L, cap_c)
        segments = ((lm, T, cap_l),) if last else ((0, L, cap_c), (L, T, cap_l))
        posm, cnt = _topk_positions(aff, segments)
        gathered = _gather_rows(cnt, posm, aff, h2, lm, cap_l, cc)
        x_sets = [a.reshape(N_EXPERTS, -1, D) for a in gathered[0::2]]
        g_sets = [a.reshape(N_EXPERTS, -1, 1) for a in gathered[1::2]]
        ys = _expert_mlp(x_sets, g_sets, exp_w_gate, exp_w_up, exp_w_down, l, cap_l)
        yl = ys[0].reshape(N_EXPERTS, B, cap_l, D)
        xs = _combine(cnt, x1, jnp.swapaxes(posm, 1, 2), yl, None if last else ys[1], mod[l],
                      final_norm_g[None], cap_l, cc, final=last)
    return xs
```

```python
import functools

import numpy as np
import jax
import jax.numpy as jnp
from jax import lax
from jax.experimental import pallas as pl
from jax.experimental.pallas import tpu as pltpu

F32 = jnp.float32
BF16 = jnp.bfloat16
I32 = jnp.int32
HIGHEST = lax.Precision.HIGHEST

D_MODEL = 1024
GRID_W = 64
EPS = 1e-6
LB_FLOOR = 1e-30
N_MOD = 6
ATT_W = 512
HG_W = 256
CV_W = 256
HEAD_DIM = 64
N_Q = 8
N_KV = 2
GQA_GROUP = 4
KV_W = 128
ROPE_BASE = 10000.0
ROPE_FREQS = 16
HG_HEADS = 4
N_EXPERTS = 16
EC_CAPACITY = 2
D_EXPERT = 2048
IN_COLS = 2816

ROW_TILE = 256
ATT_BLOCK = 128
HG_CHUNK = 128
HG_LEVELS = 7
HG_BATCH = 2
HG_LOCAL = 32
HG_LOCAL_MAX_LOG = 80.0
SLOT_BLOCK = 256
FF_TILE = 512
EXPERT_ROWS = 512
MOD_TILE = 1536
VMEM_LIMIT = 56 * 1024 * 1024

NEG_BIG = -1e30


def _cparams(sem):
    return pltpu.CompilerParams(dimension_semantics=sem, vmem_limit_bytes=VMEM_LIMIT)


def _dot(a, b):
    return jnp.dot(a, b, preferred_element_type=F32)


def _dot_nt(a, b):
    return lax.dot_general(a, b, (((1,), (1,)), ((), ())), preferred_element_type=F32)


def _mod_kernel(a_ref, w_ref, b_ref, o_ref):
    a = a_ref[...]
    a = a * jax.nn.sigmoid(a)
    o_ref[0] = jnp.dot(a, w_ref[0], precision=HIGHEST, preferred_element_type=F32) + b_ref[0]


def _modulation(cvec, ada_w, ada_b):
    depth = ada_w.shape[0]
    ncol = ada_w.shape[2]
    return pl.pallas_call(
        _mod_kernel,
        grid=(depth, ncol // MOD_TILE),
        in_specs=[
            pl.BlockSpec((8, D_MODEL), lambda l, j: (0, 0)),
            pl.BlockSpec((1, D_MODEL, MOD_TILE), lambda l, j: (l, 0, j)),
            pl.BlockSpec((1, 1, MOD_TILE), lambda l, j: (l, 0, j)),
        ],
        out_specs=pl.BlockSpec((1, 8, MOD_TILE), lambda l, j: (l, 0, j)),
        out_shape=jax.ShapeDtypeStruct((depth, 8, ncol), F32),
        name="modulation",
        compiler_params=_cparams(("arbitrary", "arbitrary")),
    )(cvec, ada_w, ada_b.reshape(depth, 1, ncol))


def _swap_halves(x):
    n = x.shape[-1]
    lane = lax.broadcasted_iota(I32, x.shape, x.ndim - 1)
    up = pltpu.roll(x, n - ROPE_FREQS, x.ndim - 1)
    dn = pltpu.roll(x, ROPE_FREQS, x.ndim - 1)
    return jnp.where((lane % (2 * ROPE_FREQS)) < ROPE_FREQS, up, dn)


def _rmsnorm_mod(x, g, shift, scale):
    var = jnp.mean(x * x, axis=-1, keepdims=True)
    y = x * lax.rsqrt(var + EPS) * g
    return y * (1.0 + scale) + shift


def _row_sources(src):
    if isinstance(src, tuple):
        ctx, lat = src
        return ctx, lat, 1, ctx.shape[1] + lat.shape[1]
    return src, src, 0, src.shape[1]


def _inproj_kernel(xc_ref, x_ref, mod_ref, g_ref, w_ref, cos_ref, sin_ref,
                   q_ref, kv_ref, hg_ref, hqg_ref, cv_ref):
    b = pl.program_id(0)
    i = pl.program_id(1)
    row = jnp.where(i == 0, 4, b)
    shift = mod_ref[pl.ds(row, 1), 0:D_MODEL]
    scale = mod_ref[pl.ds(row, 1), D_MODEL:2 * D_MODEL]
    xin = jnp.where(i == 0, xc_ref[0], x_ref[0])
    h = _rmsnorm_mod(xin, g_ref[...], shift, scale)
    p = _dot(h.astype(BF16), w_ref[...])
    cos2 = cos_ref[...]
    sin2 = sin_ref[...]
    k = p[:, 0:KV_W]
    k = k * cos2 + _swap_halves(k) * sin2
    kv_ref[0, :, 0:KV_W] = k.astype(BF16)
    kv_ref[0, :, KV_W:2 * KV_W] = p[:, KV_W:2 * KV_W].astype(BF16)
    hg_ref[0] = p[:, 256:1024]
    q = p[:, 1024:1536]
    cos8 = jnp.concatenate([cos2] * 4, axis=1)
    sin8 = jnp.concatenate([sin2] * 4, axis=1)
    q = (q * cos8 + _swap_halves(q) * sin8) * (HEAD_DIM ** -0.5)
    q_ref[0] = q.astype(BF16)
    hqg_ref[0] = p[:, 1536:2048]
    cv_ref[0] = p[:, 2048:2816]


def _in_projection(src, mod_l, g, w_bf16, cos_t, sin_t):
    xc, xl, off, S = _row_sources(src)
    B = xc.shape[0]
    nt = S // ROW_TILE
    row_map = lambda b, i: (b, i, 0)
    const2 = lambda b, i: (0, 0)
    return pl.pallas_call(
        _inproj_kernel,
        grid=(B, nt),
        in_specs=[
            pl.BlockSpec((1, ROW_TILE, D_MODEL), lambda b, i: (b, 0, 0)),
            pl.BlockSpec((1, ROW_TILE, D_MODEL), lambda b, i: (b, jnp.maximum(i - off, 0), 0)),
            pl.BlockSpec((8, N_MOD * D_MODEL), const2),
            pl.BlockSpec((1, D_MODEL), const2),
            pl.BlockSpec((D_MODEL, IN_COLS), const2),
            pl.BlockSpec((ROW_TILE, 2 * HEAD_DIM), lambda b, i: (i, 0)),
            pl.BlockSpec((ROW_TILE, 2 * HEAD_DIM), lambda b, i: (i, 0)),
        ],
        out_specs=[
            pl.BlockSpec((1, ROW_TILE, ATT_W), row_map),
            pl.BlockSpec((1, ROW_TILE, 2 * KV_W), row_map),
            pl.BlockSpec((1, ROW_TILE, 3 * HG_W), row_map),
            pl.BlockSpec((1, ROW_TILE, 2 * HG_W), row_map),
            pl.BlockSpec((1, ROW_TILE, 3 * CV_W), row_map),
        ],
        out_shape=[
            jax.ShapeDtypeStruct((B, S, ATT_W), BF16),
            jax.ShapeDtypeStruct((B, S, 2 * KV_W), BF16),
            jax.ShapeDtypeStruct((B, S, 3 * HG_W), F32),
            jax.ShapeDtypeStruct((B, S, 2 * HG_W), F32),
            jax.ShapeDtypeStruct((B, S, 3 * CV_W), F32),
        ],
        name="in_projection",
        compiler_params=_cparams(("arbitrary", "arbitrary")),
    )(xc, xl, mod_l, g, w_bf16, cos_t, sin_t)


def _attn_kernel(sink_ref, q_ref, kp_ref, kc_ref, kn_ref, kx_ref, o_ref, *, blk0, nblk, nctx):
    n = pl.program_id(1) + blk0
    is_lat = n >= nctx
    has_prev = n > nctx
    has_next = n < nblk - 1
    W = ATT_BLOCK
    q = q_ref[0]
    qrows = jnp.concatenate([q[:, g * W:(g + 1) * W] for g in range(GQA_GROUP)], axis=0)
    kv_all = jnp.concatenate([kp_ref[0], kc_ref[0], kn_ref[0], kx_ref[0]], axis=0)
    nkeys = kv_all.shape[0]
    k_all = kv_all[:, 0:KV_W]
    v_ext = jnp.concatenate([kv_all[:, KV_W:2 * KV_W], jnp.ones((nkeys, KV_W), BF16)], axis=1)
    rows = GQA_GROUP * W
    ri = lax.broadcasted_iota(I32, (rows, W), 0) % W
    cj = lax.broadcasted_iota(I32, (rows, W), 1)
    m_prev = (cj >= ri) & has_prev
    m_cur = jnp.broadcast_to(is_lat, (rows, W))
    m_next = (cj <= ri) & (has_next & is_lat)
    grp = lax.broadcasted_iota(I32, (rows, 1), 0) // W
    low = cj < HEAD_DIM
    outs = []
    for h in range(N_KV):
        qh = jnp.where(low if h == 0 else jnp.logical_not(low), qrows, jnp.zeros_like(qrows))
        sink = jnp.zeros((rows, 1), F32)
        for g in range(GQA_GROUP):
            sink = jnp.where(grp == g, sink_ref[h * GQA_GROUP + g], sink)
        s = _dot_nt(qh, k_all)
        segs = [jnp.where(m_prev, s[:, 0:W], NEG_BIG),
                jnp.where(m_cur, s[:, W:2 * W], NEG_BIG),
                jnp.where(m_next, s[:, 2 * W:3 * W], NEG_BIG)]
        segs += [s[:, c:c + W] for c in range(3 * W, nkeys, W)]
        mx = segs[0]
        for sg in segs[1:]:
            mx = jnp.maximum(mx, sg)
        m = jnp.maximum(jnp.max(mx, axis=1, keepdims=True), sink)
        p = jnp.concatenate([jnp.exp(sg - m).astype(BF16) for sg in segs], axis=1)
        oe = _dot(p, v_ext)
        den = oe[:, KV_W:2 * KV_W] + jnp.exp(sink - m)
        outs.append(oe[:, 0:KV_W] / den)
    o = jnp.where(low, outs[0], outs[1])
    for g in range(GQA_GROUP):
        o_ref[0, :, g * W:(g + 1) * W] = o[g * W:(g + 1) * W].astype(BF16)


def _attention(q, kv, sink, ctx_len, skip_ctx):
    B, S, _ = q.shape
    nblk = S // ATT_BLOCK
    nctx = ctx_len // ATT_BLOCK
    blk0 = nctx if skip_ctx else 0
    blk = lambda f: (lambda b, n: (b, f(n + blk0), 0))
    kern = functools.partial(_attn_kernel, blk0=blk0, nblk=nblk, nctx=nctx)
    return pl.pallas_call(
        kern,
        grid=(B, nblk - blk0),
        in_specs=[
            pl.BlockSpec(memory_space=pltpu.SMEM),
            pl.BlockSpec((1, ATT_BLOCK, ATT_W), blk(lambda n: n)),
            pl.BlockSpec((1, ATT_BLOCK, 2 * KV_W), blk(lambda n: jnp.maximum(n - 1, 0))),
            pl.BlockSpec((1, ATT_BLOCK, 2 * KV_W), blk(lambda n: n)),
            pl.BlockSpec((1, ATT_BLOCK, 2 * KV_W), blk(lambda n: jnp.minimum(n + 1, nblk - 1))),
            pl.BlockSpec((1, ctx_len, 2 * KV_W), lambda b, n: (b, 0, 0)),
        ],
        out_specs=pl.BlockSpec((1, ATT_BLOCK, ATT_W), lambda b, n: (b, n, 0)),
        out_shape=jax.ShapeDtypeStruct((B, S - blk0 * ATT_BLOCK, ATT_W), BF16),
        name="attention",
        compiler_params=_cparams(("arbitrary", "arbitrary")),
    )(sink, q, kv, kv, kv, kv)


def _hgrn_constants():
    C = HG_CHUNK
    t = np.arange(C)[:, None]
    r = np.arange(C)[None, :]
    tri = np.stack([r <= t, r >= t]).astype(np.float32)
    x = t ^ r
    lvl = np.where(x > 0, np.floor(np.log2(np.maximum(x, 1))).astype(np.int32), HG_LEVELS)
    lvl_f = np.where(t >= r, lvl, -1).astype(np.int32)
    lvl_b = np.where(t <= r, lvl, -1).astype(np.int32)
    lvl2 = np.stack([np.tile(lvl_f, (1, HG_HEADS)), np.tile(lvl_b, (1, HG_HEADS))])
    return tri, lvl2


def _span_row(x, span, row):
    C = x.shape[0]
    if span >= 8:
        x3 = x.reshape(C // span, span, x.shape[1])
        return jnp.broadcast_to(x3[:, row:row + 1, :], x3.shape).reshape(x.shape)
    pos = lax.broadcasted_iota(I32, x.shape, 0) % span
    out = x
    for p in range(span):
        if p != row:
            out = jnp.where(pos == p, pltpu.roll(x, (p - row) % C, 0), out)
    return out


def _hgrn_prepare(v, z, qr, lb, tri, backward):
    C = HG_CHUNK
    logf = jnp.log(jnp.maximum(lb, LB_FLOOR) + (1.0 - lb) * jax.nn.sigmoid(z))
    k = (1.0 - lb) * jax.nn.sigmoid(-z)
    q = qr * jax.nn.sigmoid(qr)
    hi = logf.astype(BF16)
    lo = (logf - hi.astype(F32)).astype(BF16)
    cs = _dot(tri, jnp.concatenate([hi, lo], axis=1))
    lam = cs[:, 0:HG_W] + cs[:, HG_W:2 * HG_W]
    tot = lam[0:1] if backward else lam[C - 1:C]
    local = lam - _span_row(lam, HG_LOCAL, HG_LOCAL // 2)
    return dict(v=v, q=q, k=k, lam=lam, tot=tot, local=local)


def _hgrn_finish(g, head_masks, same_head, lvl, st_ref, sidx, backward, fast):
    C = HG_CHUNK
    q, k, v, lam, tot = g["q"], g["k"], g["v"], g["lam"], g["tot"]
    zero = jnp.zeros((C, HG_W), BF16)

    def per_head_rows(x):
        return jnp.concatenate([jnp.where(hm, x, zero) for hm in head_masks], axis=0)

    qb = q.astype(BF16)
    kb = k.astype(BF16)
    if fast:
        n_local = HG_LOCAL.bit_length() - 1
        a = jnp.where((lvl == HG_LEVELS) | ((lvl >= 0) & (lvl < n_local)),
                      _dot_nt((q * jnp.exp(g["local"])).astype(BF16),
                              per_head_rows((k * jnp.exp(-g["local"])).astype(BF16))), 0.0)
        levels = range(n_local, HG_LEVELS)
    else:
        a = jnp.where(lvl == HG_LEVELS, _dot_nt(qb, per_head_rows(kb)), 0.0)
        levels = range(HG_LEVELS)
    for l in levels:
        m = 1 << l
        ref = _span_row(lam, 2 * m, m if backward else m - 1)
        fac = jnp.exp(-jnp.abs(lam - ref)).astype(BF16)
        a = jnp.where(lvl == l, _dot_nt(qb * fac, per_head_rows(kb * fac)), a)
    st = st_ref[sidx]
    q_in = (q * jnp.exp(lam)).astype(BF16)
    o = _dot(a.astype(BF16), per_head_rows(v.astype(BF16))) + _dot_nt(q_in, st.astype(BF16))
    k_out = (k * jnp.exp(tot - lam)).astype(BF16)
    upd = _dot(v.T.astype(BF16), k_out)
    st_ref[sidx] = st * jnp.exp(tot) + jnp.where(same_head, upd, 0.0)
    return o


def _hgrn_kernel(vf_ref, zf_ref, qf_ref, vb_ref, zb_ref, qb_ref, lb_ref, tri_ref, lvl_ref,
                 of_ref, ob_ref, st_ref):
    j = pl.program_id(1)

    @pl.when(j == 0)
    def _():
        st_ref[...] = jnp.zeros_like(st_ref)

    head_id = (lax.broadcasted_iota(I32, (HG_CHUNK, HG_W), 1) // HEAD_DIM).astype(F32).astype(BF16)
    head_masks = [head_id == float(h) for h in range(HG_HEADS)]
    same_head = (lax.broadcasted_iota(I32, (HG_W, HG_W), 0) // HEAD_DIM
                 == lax.broadcasted_iota(I32, (HG_W, HG_W), 1) // HEAD_DIM)
    streams = []
    for d, (v_ref, z_ref, q_ref, o_ref) in enumerate(
            ((vf_ref, zf_ref, qf_ref, of_ref), (vb_ref, zb_ref, qb_ref, ob_ref))):
        for s in range(v_ref.shape[0]):
            g = _hgrn_prepare(v_ref[s], z_ref[s], q_ref[s], lb_ref[d], tri_ref[d], d == 1)
            streams.append((d, s, o_ref, g))
    worst = jnp.abs(streams[0][3]["local"])
    for _, _, _, g in streams[1:]:
        worst = jnp.maximum(worst, jnp.abs(g["local"]))
    can_use_local = jnp.max(worst) <= HG_LOCAL_MAX_LOG

    def run(fast):
        for d, s, o_ref, g in streams:
            o_ref[s] = _hgrn_finish(g, head_masks, same_head, lvl_ref[d], st_ref, 2 * s + d,
                                    backward=(d == 1), fast=fast)

    pl.when(can_use_local)(functools.partial(run, True))
    pl.when(jnp.logical_not(can_use_local))(functools.partial(run, False))


def _hgrn(hg, hqg, lb_l, ctx_len):
    B, S, _ = hg.shape
    nc = S // HG_CHUNK
    nctx = ctx_len // HG_CHUNK
    tri_np, lvl_np = _hgrn_constants()
    tri = jnp.asarray(tri_np, BF16)
    lvl = jnp.asarray(lvl_np, I32)

    def back(j):
        return jnp.where(j < nctx, nctx - 1 - j, nc - 1 - (j - nctx))

    bs = HG_BATCH if B % HG_BATCH == 0 else 1
    blk = (bs, HG_CHUNK, HG_W)
    fwd = lambda col: (lambda b, j: (b, j, col))
    bwd = lambda col: (lambda b, j: (b, back(j), col))
    const3 = lambda b, j: (0, 0, 0)
    return pl.pallas_call(
        _hgrn_kernel,
        grid=(B // bs, nc),
        in_specs=[
            pl.BlockSpec(blk, fwd(0)), pl.BlockSpec(blk, fwd(1)), pl.BlockSpec(blk, fwd(0)),
            pl.BlockSpec(blk, bwd(0)), pl.BlockSpec(blk, bwd(2)), pl.BlockSpec(blk, bwd(0)),
            pl.BlockSpec((2, 1, HG_W), const3),
            pl.BlockSpec((2, HG_CHUNK, HG_CHUNK), const3),
            pl.BlockSpec((2, HG_CHUNK, HG_HEADS * HG_CHUNK), const3),
        ],
        out_specs=[pl.BlockSpec(blk, fwd(0)), pl.BlockSpec(blk, bwd(0))],
        out_shape=[jax.ShapeDtypeStruct((B, S, HG_W), F32)] * 2,
        scratch_shapes=[pltpu.VMEM((2 * bs, HG_W, HG_W), F32)],
        name="hgrn_scan",
        compiler_params=_cparams(("arbitrary", "arbitrary")),
    )(hg, hg, hqg, hg, hg, hqg, lb_l.reshape(2, 1, HG_W), tri, lvl)


def _mixout_kernel(xc_ref, x_ref, att_ref, of_ref, ob_ref, g_ref, cv_ref, cprev_ref, cnext_ref,
                   mod_ref, gain_ref, cw_ref, wo_ref, n2_ref, wr_ref, ones_ref,
                   x1_ref, h2_ref, aff_ref, *, tile0, ntile):
    b = pl.program_id(0)
    i = pl.program_id(1) + tile0
    row = jnp.where(i == 0, 4, b)
    R = ROW_TILE

    def modv(c):
        return mod_ref[pl.ds(row, 1), c * D_MODEL:(c + 1) * D_MODEL]

    o = of_ref[0] + ob_ref[0]
    sq = o * o
    sq_hi = sq.astype(BF16)
    sq_lo = (sq - sq_hi.astype(F32)).astype(BF16)
    ms = _dot(sq_hi, ones_ref[...]) + _dot(sq_lo, ones_ref[...])
    g = g_ref[0]
    hg = o * lax.rsqrt(ms * (1.0 / HEAD_DIM) + EPS) * gain_ref[...] * (g * jax.nn.sigmoid(g))
    cv = cv_ref[0]
    u = cv[:, CV_W:2 * CV_W] * cv[:, 2 * CV_W:3 * CV_W]
    up = cprev_ref[0]
    un = cnext_ref[0]
    u_prev_row = up[7:8, CV_W:2 * CV_W] * up[7:8, 2 * CV_W:3 * CV_W]
    u_next_row = un[0:1, CV_W:2 * CV_W] * un[0:1, 2 * CV_W:3 * CV_W]
    u_prev_row = jnp.where(i <= 1, 0.0, u_prev_row)
    u_next_row = jnp.where((i == 0) | (i == ntile - 1), 0.0, u_next_row)
    ridx = lax.broadcasted_iota(I32, (R, CV_W), 0)
    u_m1 = jnp.where(ridx == 0, u_prev_row, pltpu.roll(u, 1, 0))
    u_p1 = jnp.where(ridx == R - 1, u_next_row, pltpu.roll(u, R - 1, 0))
    cw = cw_ref[...]
    conv = cv[:, 0:CV_W] * (u_m1 * cw[0:1] + u * cw[1:2] + u_p1 * cw[2:3])
    mix = (_dot(att_ref[0], wo_ref[0:ATT_W])
           + _dot(hg.astype(BF16), wo_ref[ATT_W:ATT_W + HG_W])
           + _dot(conv.astype(BF16), wo_ref[ATT_W + HG_W:D_MODEL]))
    x1 = jnp.where(i == 0, xc_ref[0], x_ref[0]) + modv(2) * mix
    x1_ref[0] = x1
    h2 = _rmsnorm_mod(x1, n2_ref[...], modv(3), modv(4))
    h_hi = h2.astype(BF16)
    h2_ref[0] = h_hi
    h_lo = (h2 - h_hi.astype(F32)).astype(BF16)
    wr = wr_ref[...]
    w_hi = wr.astype(BF16)
    w_lo = (wr - w_hi.astype(F32)).astype(BF16)
    r1 = _dot_nt(jnp.concatenate([w_hi, w_lo], axis=0), h_hi)
    logits = r1[0:N_EXPERTS] + r1[N_EXPERTS:2 * N_EXPERTS] + _dot_nt(w_hi, h_lo)
    e = jnp.exp(logits - jnp.max(logits, axis=0, keepdims=True))
    aff_ref[0] = e / jnp.sum(e, axis=0, keepdims=True)


def _mix_out(src, att, o2, hqg, cv, mod_l, gain, conv_w, wo_bf16, n2g, wr_t, skip_ctx):
    xc, xl, off, S = _row_sources(src)
    B = xc.shape[0]
    ntile = S // ROW_TILE
    tile0 = 1 if skip_ctx else 0
    sub = ROW_TILE // 8
    nsub = S // 8
    rmap = lambda b, i: (b, i + tile0, 0)
    omap = lambda b, i: (b, i, 0)
    s_out = S - tile0 * ROW_TILE
    const2 = lambda b, i: (0, 0)
    ones = jnp.asarray(np.kron(np.eye(HG_HEADS), np.ones((HEAD_DIM, HEAD_DIM))), BF16)
    kern = functools.partial(_mixout_kernel, tile0=tile0, ntile=ntile)
    return pl.pallas_call(
        kern,
        grid=(B, ntile - tile0),
        in_specs=[
            pl.BlockSpec((1, ROW_TILE, D_MODEL), lambda b, i: (b, 0, 0)),
            pl.BlockSpec((1, ROW_TILE, D_MODEL),
                         lambda b, i: (b, jnp.maximum(i + tile0 - off, 0), 0)),
            pl.BlockSpec((1, ROW_TILE, ATT_W), omap),
            pl.BlockSpec((1, ROW_TILE, HG_W), rmap),
            pl.BlockSpec((1, ROW_TILE, HG_W), rmap),
            pl.BlockSpec((1, ROW_TILE, HG_W), lambda b, i: (b, i + tile0, 1)),
            pl.BlockSpec((1, ROW_TILE, 3 * CV_W), rmap),
            pl.BlockSpec((1, 8, 3 * CV_W),
                         lambda b, i: (b, jnp.maximum((i + tile0) * sub - 1, 0), 0)),
            pl.BlockSpec((1, 8, 3 * CV_W),
                         lambda b, i: (b, jnp.minimum((i + tile0 + 1) * sub, nsub - 1), 0)),
            pl.BlockSpec((8, N_MOD * D_MODEL), const2),
            pl.BlockSpec((1, HG_W), const2),
            pl.BlockSpec((3, CV_W), const2),
            pl.BlockSpec((D_MODEL, D_MODEL), const2),
            pl.BlockSpec((1, D_MODEL), const2),
            pl.BlockSpec((N_EXPERTS, D_MODEL), const2),
            pl.BlockSpec((HG_W, HG_W), const2),
        ],
        out_specs=[
            pl.BlockSpec((1, ROW_TILE, D_MODEL), omap),
            pl.BlockSpec((1, ROW_TILE, D_MODEL), omap),
            pl.BlockSpec((1, N_EXPERTS, ROW_TILE), lambda b, i: (b, 0, i)),
        ],
        out_shape=[
            jax.ShapeDtypeStruct((B, s_out, D_MODEL), F32),
            jax.ShapeDtypeStruct((B, s_out, D_MODEL), BF16),
            jax.ShapeDtypeStruct((B, N_EXPERTS, s_out), F32),
        ],
        name="mix_out",
        compiler_params=_cparams(("arbitrary", "arbitrary")),
    )(xc, xl, att, o2[0], o2[1], hqg, cv, cv, cv, mod_l, gain, conv_w, wo_bf16, n2g, wr_t, ones)


def _topk_kernel(aff_ref, tri_ref, pos_ref, cnt_ref, *, segments):
    tri = tri_ref[...]

    def excl_cumsum(mask, n):
        carry = jnp.zeros((N_EXPERTS, 1), F32)
        parts = []
        carries = []
        for c in range(n // 128):
            carries.append(carry)
            blk = jnp.where(mask[:, c * 128:(c + 1) * 128], 1.0, 0.0)
            parts.append(_dot(blk.astype(BF16), tri) + carry)
            carry = carry + jnp.sum(blk, axis=1, keepdims=True)
        return jnp.concatenate(parts, axis=1), carries + [carry]

    for (lo, n, k) in segments:
        a = aff_ref[0, :, lo:lo + n]

        def body(it, thr):
            cand = thr | jnp.left_shift(jnp.int32(1), 30 - it)
            cnt = jnp.sum(jnp.where(a >= pltpu.bitcast(cand, F32), 1.0, 0.0), axis=1, keepdims=True)
            return jnp.where(cnt >= k, cand, thr)

        thr = lax.fori_loop(0, 31, body, jnp.zeros((N_EXPERTS, 1), I32))
        above = a >= pltpu.bitcast(thr + 1, F32)
        tied = jnp.logical_and(a >= pltpu.bitcast(thr, F32), jnp.logical_not(above))
        n_above = jnp.sum(jnp.where(above, 1.0, 0.0), axis=1, keepdims=True)
        rank_tied, _ = excl_cumsum(tied, n)
        sel = above | (tied & (rank_tied < (k - n_above)))
        pos, counts = excl_cumsum(sel, n)
        pos_ref[0, :, lo:lo + n] = jnp.where(sel, pos.astype(I32), -1)
    step = ROW_TILE // 128
    lane = lax.broadcasted_iota(I32, (N_EXPERTS, 128), 1)
    cnt = jnp.zeros((N_EXPERTS, 128), F32)
    for c, col in enumerate(counts[0::step]):
        cnt = jnp.where(lane == c, col, cnt)
    cnt_ref[0] = cnt.astype(I32)


def _topk_positions(aff, segments):
    B, E, S = aff.shape
    assert segments[-1][1] // ROW_TILE + 1 <= 128
    tri = jnp.asarray(np.triu(np.ones((128, 128)), 1), BF16)
    kern = functools.partial(_topk_kernel, segments=segments)
    return pl.pallas_call(
        kern,
        grid=(B,),
        in_specs=[pl.BlockSpec((1, E, S), lambda b: (b, 0, 0)),
                  pl.BlockSpec((128, 128), lambda b: (0, 0))],
        out_specs=[pl.BlockSpec((1, E, S), lambda b: (b, 0, 0)),
                   pl.BlockSpec((1, E, 128), lambda b: (b, 0, 0))],
        out_shape=[jax.ShapeDtypeStruct((B, E, S), I32),
                   jax.ShapeDtypeStruct((B, E, 128), I32)],
        name="topk_positions",
        compiler_params=_cparams(("arbitrary",)),
    )(aff, tri)


def _slot_block(cap):
    sbs = min(SLOT_BLOCK, cap)
    assert sbs >= ROW_TILE or sbs == cap
    return sbs


def _div_pow2(x, d):
    assert d & (d - 1) == 0
    return lax.shift_right_logical(x, jnp.int32(d.bit_length() - 1))


def _gather_kernel(cnt_ref, pos_ref, aff_ref, h_ref, *refs, ctx_len, cap_l, cap_c):
    if cap_c:
        xl_ref, gl_ref, xc_ref, gc_ref, acc_ref, gacc_ref = refs
    else:
        xl_ref, gl_ref, acc_ref, gacc_ref = refs
    b = pl.program_id(0)
    e = pl.program_id(1)
    S = h_ref.shape[1]
    pos = pos_ref[0, pl.ds(e, 1), :]
    aff = aff_ref[0, pl.ds(e, 1), :]
    nchunk = (S - ctx_len) // ROW_TILE
    sbs = _slot_block(cap_l)
    acc_ref[...] = jnp.zeros_like(acc_ref)
    gacc_ref[...] = jnp.zeros_like(gacc_ref)
    nsb = cap_l // sbs
    first = [jnp.minimum(_div_pow2(cnt_ref[b, e, c], sbs), nsb - 1) for c in range(nchunk)]
    slot_i32 = lax.broadcasted_iota(I32, (sbs, ROW_TILE), 0)
    slot_in_block = slot_i32.astype(F32).astype(BF16)
    one = jnp.ones((sbs, ROW_TILE), BF16)
    zero = jnp.zeros((sbs, ROW_TILE), BF16)

    def visit(c, sb):
        lo = ctx_len + c * ROW_TILE
        base = pl.multiple_of(sb * sbs, sbs)
        rel_i = pos[:, lo:lo + ROW_TILE] - base
        rel = rel_i.astype(F32)
        rel = jnp.where((rel >= 0.0) & (rel < float(sbs)), rel, -1.0).astype(BF16)
        acc_ref[pl.ds(base, sbs), :] += _dot(jnp.where(rel == slot_in_block, one, zero),
                                             h_ref[0, lo:lo + ROW_TILE, :])
        gacc_ref[pl.ds(base, sbs), :] += jnp.sum(
            jnp.where(rel_i == slot_i32, aff[:, lo:lo + ROW_TILE], 0.0), axis=1, keepdims=True)

    for c in range(nchunk):
        visit(c, first[c])
    for c in range(nchunk):
        pl.when(cnt_ref[b, e, c + 1] > (first[c] + 1) * sbs)(
            functools.partial(visit, c, first[c] + 1))
    xl_ref[0, 0] = acc_ref[...].astype(BF16)
    gl_ref[0, 0] = gacc_ref[...]
    if cap_c:
        slot_c = lax.broadcasted_iota(I32, (cap_c, ctx_len), 0)
        hit = pos[:, 0:ctx_len] == slot_c
        xc_ref[0, 0] = _dot(jnp.where(hit, 1.0, 0.0).astype(BF16),
                            h_ref[0, 0:ctx_len, :]).astype(BF16)
        gc_ref[0, 0] = jnp.sum(jnp.where(hit, aff[:, 0:ctx_len], 0.0), axis=1, keepdims=True)


def _gather_rows(cnt, posm, aff, h2, ctx_len, cap_l, cap_c):
    B, E, S = posm.shape
    kern = functools.partial(_gather_kernel, ctx_len=ctx_len, cap_l=cap_l, cap_c=cap_c)
    omap = lambda b, e, cnt_ref: (e, b, 0, 0)
    caps = [cap_l] + ([cap_c] if cap_c else [])
    out_specs = []
    out_shape = []
    for cap in caps:
        out_specs += [pl.BlockSpec((1, 1, cap, D_MODEL), omap), pl.BlockSpec((1, 1, cap, 1), omap)]
        out_shape += [jax.ShapeDtypeStruct((E, B, cap, D_MODEL), BF16),
                      jax.ShapeDtypeStruct((E, B, cap, 1), F32)]
    grid_spec = pltpu.PrefetchScalarGridSpec(
        num_scalar_prefetch=1,
        grid=(B, E),
        in_specs=[pl.BlockSpec((1, E, S), lambda b, e, cnt_ref: (b, 0, 0)),
                  pl.BlockSpec((1, E, S), lambda b, e, cnt_ref: (b, 0, 0)),
                  pl.BlockSpec((1, S, D_MODEL), lambda b, e, cnt_ref: (b, 0, 0))],
        out_specs=out_specs,
        scratch_shapes=[pltpu.VMEM((cap_l, D_MODEL), F32), pltpu.VMEM((cap_l, 1), F32)],
    )
    return pl.pallas_call(
        kern,
        grid_spec=grid_spec,
        out_shape=out_shape,
        name="gather_rows",
        compiler_params=_cparams(("arbitrary", "arbitrary")),
    )(cnt, posm, aff, h2)


def _expert_kernel(*refs, row_chunk, n_sets):
    x_refs = refs[:n_sets]
    g_refs = refs[n_sets:2 * n_sets]
    wg_ref, wu_ref, wd_ref = refs[2 * n_sets:2 * n_sets + 3]
    y_refs = refs[2 * n_sets + 3:3 * n_sets + 3]
    acc_refs = refs[3 * n_sets + 3:]
    f = pl.program_id(1)
    nf = pl.num_programs(1)
    wg = wg_ref[0, 0].astype(BF16)
    wu = wu_ref[0, 0].astype(BF16)
    wd = wd_ref[0, 0].astype(BF16)

    @pl.when(f == 0)
    def _():
        for acc_ref in acc_refs:
            acc_ref[...] = jnp.zeros_like(acc_ref)

    for x_ref, acc_ref in zip(x_refs, acc_refs):
        rows = x_ref.shape[1]
        step = min(row_chunk, rows)
        for r in range(rows // step):
            rs = slice(r * step, (r + 1) * step)
            x = x_ref[0, rs, :]
            g = _dot(x, wg)
            u = _dot(x, wu)
            hid = (g * jax.nn.sigmoid(g) * u).astype(BF16)
            acc_ref[rs, :] += _dot(hid, wd)

    @pl.when(f == nf - 1)
    def _():
        for g_ref, y_ref, acc_ref in zip(g_refs, y_refs, acc_refs):
            y_ref[0] = (acc_ref[...] * g_ref[0]).astype(BF16)


def _expert_mlp(x_sets, g_sets, w_gate, w_up, w_down, layer, row_chunk):
    E = N_EXPERTS
    nf = D_EXPERT // FF_TILE
    n_sets = len(x_sets)
    kern = functools.partial(_expert_kernel, row_chunk=row_chunk, n_sets=n_sets)
    xspec = lambda r: pl.BlockSpec((1, r, D_MODEL), lambda e, f: (e, 0, 0))
    gspec = lambda r: pl.BlockSpec((1, r, 1), lambda e, f: (e, 0, 0))
    return pl.pallas_call(
        kern,
        grid=(E, nf),
        in_specs=[xspec(xs.shape[1]) for xs in x_sets] + [gspec(xs.shape[1]) for xs in x_sets] + [
            pl.BlockSpec((1, 1, D_MODEL, FF_TILE), lambda e, f: (layer, e, 0, f)),
            pl.BlockSpec((1, 1, D_MODEL, FF_TILE), lambda e, f: (layer, e, 0, f)),
            pl.BlockSpec((1, 1, FF_TILE, D_MODEL), lambda e, f: (layer, e, f, 0))],
        out_specs=[xspec(xs.shape[1]) for xs in x_sets],
        out_shape=[jax.ShapeDtypeStruct(xs.shape, BF16) for xs in x_sets],
        scratch_shapes=[pltpu.VMEM(xs.shape[1:], F32) for xs in x_sets],
        name="expert_mlp",
        compiler_params=_cparams(("arbitrary", "arbitrary")),
    )(*x_sets, *g_sets, w_gate, w_up, w_down)


def _combine_kernel(cnt_ref, x_ref, pos_ref, *refs, cap_l, cap_c, final):
    if cap_c:
        yl_ref, yc_ref, mod_ref, fg_ref, o_ref, acc_ref = refs
    else:
        yl_ref, mod_ref, fg_ref, o_ref, acc_ref = refs
    b = pl.program_id(0)
    i = pl.program_id(1)
    row = jnp.where(i == 0, 4, b) if cap_c else b
    gate = mod_ref[pl.ds(row, 1), 5 * D_MODEL:6 * D_MODEL]
    pos = pos_ref[0]
    acc_ref[...] = jnp.zeros_like(acc_ref)
    sbs = _slot_block(cap_l)

    def add_latent():
        t = i - 1 if cap_c else i
        nsb = cap_l // sbs
        def product(e, sb):
            base = pl.multiple_of(sb * sbs, sbs)
            lane = lax.broadcasted_iota(I32, (ROW_TILE, sbs), 1) + base
            onehot = jnp.where(pos[:, e:e + 1] == lane, 1.0, 0.0).astype(BF16)
            return _dot(onehot, yl_ref[e, 0, pl.ds(base, sbs), :])

        first = [jnp.minimum(_div_pow2(cnt_ref[b, e, t], sbs), nsb - 1) for e in range(N_EXPERTS)]
        total = product(0, first[0])
        for e in range(1, N_EXPERTS):
            total = total + product(e, first[e])
        acc_ref[...] += total
        for e in range(N_EXPERTS):
            def crossing(e=e):
                acc_ref[...] += product(e, first[e] + 1)

            pl.when(cnt_ref[b, e, t + 1] > (first[e] + 1) * sbs)(crossing)

    def add_context():
        ncol = yc_ref.shape[1]
        lane = lax.broadcasted_iota(I32, (ROW_TILE, ncol), 1)
        for e in range(N_EXPERTS):
            pe = pos[:, e:e + 1]
            tgt = jnp.where(pe >= 0, pe + b * cap_c, -1)
            acc_ref[...] += _dot(jnp.where(tgt == lane, 1.0, 0.0).astype(BF16), yc_ref[e])

    if cap_c:
        pl.when(i == 0)(add_context)
        pl.when(i > 0)(add_latent)
    else:
        add_latent()
    x2 = x_ref[0] + gate * acc_ref[...]
    if final:
        var = jnp.mean(x2 * x2, axis=-1, keepdims=True)
        x2 = x2 * lax.rsqrt(var + EPS) * fg_ref[...]
    o_ref[0] = x2


def _combine(cnt, x1, pos_t, yl, yc, mod_l, final_g, cap_l, cap_c, final):
    B, S, _ = x1.shape
    E = N_EXPERTS
    rmap = lambda b, i, cnt_ref: (b, i, 0)
    const2 = lambda b, i, cnt_ref: (0, 0)
    kern = functools.partial(_combine_kernel, cap_l=cap_l, cap_c=cap_c, final=final)
    y_specs = [pl.BlockSpec((E, 1, cap_l, D_MODEL), lambda b, i, cnt_ref: (0, b, 0, 0),
                            pipeline_mode=pl.Buffered(1))]
    y_args = [yl]
    if cap_c:
        y_specs.append(pl.BlockSpec(yc.shape, lambda b, i, cnt_ref: (0, 0, 0),
                                    pipeline_mode=pl.Buffered(1)))
        y_args.append(yc)
    grid_spec = pltpu.PrefetchScalarGridSpec(
        num_scalar_prefetch=1,
        grid=(B, S // ROW_TILE),
        in_specs=[pl.BlockSpec((1, ROW_TILE, D_MODEL), rmap),
                  pl.BlockSpec((1, ROW_TILE, E), rmap)] + y_specs + [
                  pl.BlockSpec((8, N_MOD * D_MODEL), const2),
                  pl.BlockSpec((1, D_MODEL), const2)],
        out_specs=pl.BlockSpec((1, ROW_TILE, D_MODEL), rmap),
        scratch_shapes=[pltpu.VMEM((ROW_TILE, D_MODEL), F32)],
    )
    return pl.pallas_call(
        kern,
        grid_spec=grid_spec,
        out_shape=jax.ShapeDtypeStruct((B, S, D_MODEL), F32),
        name="combine",
        compiler_params=_cparams(("arbitrary", "arbitrary")),
    )(cnt, x1, pos_t, *y_args, mod_l, final_g)


def _rope_tables(n_lat, ctx_len):
    t = np.arange(n_lat)
    pos = np.stack([t // GRID_W, t % GRID_W], axis=-1).astype(np.float32)
    inv = (ROPE_BASE ** (-np.arange(ROPE_FREQS, dtype=np.float32) / ROPE_FREQS)).astype(np.float32)
    ang = pos[:, :, None] * inv
    cos = np.cos(ang).astype(np.float32)
    sin = np.sin(ang).astype(np.float32)
    cos64 = np.concatenate([cos[:, 0], cos[:, 0], cos[:, 1], cos[:, 1]], axis=-1)
    sin64 = np.concatenate([-sin[:, 0], sin[:, 0], -sin[:, 1], sin[:, 1]], axis=-1)
    cos_t = np.concatenate([np.ones((ctx_len, 64), np.float32), cos64], axis=0)
    sin_t = np.concatenate([np.zeros((ctx_len, 64), np.float32), sin64], axis=0)
    return (jnp.asarray(np.tile(cos_t, (1, 2))), jnp.asarray(np.tile(sin_t, (1, 2))))


def kernel(x, c, ctx, c_ctx, ada_w, ada_b, norm1_g, norm2_g, w_in, attn_sink, hgrn_lb,
           hgrn_norm_g, conv_w, w_o, router_w, exp_w_gate, exp_w_up, exp_w_down, final_norm_g):
    B, T, D = x.shape
    L = ctx.shape[1]
    depth = ada_w.shape[0]
    assert D == D_MODEL and L == ROW_TILE and T % ROW_TILE == 0 and B <= 4
    S = L + T
    cap_l = EC_CAPACITY * T // N_EXPERTS
    cap_c = EC_CAPACITY * L // N_EXPERTS
    assert cap_l % 16 == 0 and cap_c % 16 == 0 and (B * cap_c) % 16 == 0

    cos_t, sin_t = _rope_tables(T, L)
    gamma = jax.nn.softmax(hgrn_lb.astype(F32), axis=0)
    lb_all = jnp.cumsum(gamma, axis=0) - gamma[0]
    cvec = jnp.concatenate([c, jnp.zeros((4 - B, D), F32), c_ctx[None],
                            jnp.zeros((3, D), F32)], axis=0)
    mod = _modulation(cvec, ada_w, ada_b)
    xs = (ctx, x)

    def interleave_heads(w, axis):
        shp = w.shape
        w = w.reshape(shp[:axis] + (N_KV, GQA_GROUP, HEAD_DIM) + shp[axis + 1:])
        return jnp.swapaxes(w, axis, axis + 1).reshape(shp)

    for l in range(depth):
        last = l == depth - 1
        w_in_l = w_in[l].astype(BF16)
        w_in_l = jnp.concatenate([w_in_l[:, :1024], interleave_heads(w_in_l[:, 1024:1536], 1),
                                  w_in_l[:, 1536:]], axis=1)
        w_o_l = w_o[l].astype(BF16)
        w_o_l = jnp.concatenate([interleave_heads(w_o_l[:ATT_W], 0), w_o_l[ATT_W:]], axis=0)
        q, kv, hg, hqg, cv = _in_projection(xs, mod[l], norm1_g[l][None], w_in_l, cos_t, sin_t)
        att = _attention(q, kv, attn_sink[l], L, skip_ctx=last)
        o2 = _hgrn(hg, hqg, lb_all[l], L)
        x1, h2, aff = _mix_out(xs, att, o2, hqg, cv, mod[l], hgrn_norm_g[l][None], conv_w[l],
                               w_o_l, norm2_g[l][None], router_w[l].T, skip_ctx=last)
        lm, cc = (0, 0) if last else (L, cap_c)
        segments = ((lm, T, cap_l),) if last else ((0, L, cap_c), (L, T, cap_l))
        posm, cnt = _topk_positions(aff, segments)
        gathered = _gather_rows(cnt, posm, aff, h2, lm, cap_l, cc)
        x_sets = [a.reshape(N_EXPERTS, -1, D) for a in gathered[0::2]]
        g_sets = [a.reshape(N_EXPERTS, -1, 1) for a in gathered[1::2]]
        ys = _expert_mlp(x_sets, g_sets, exp_w_gate, exp_w_up, exp_w_down, l, EXPERT_ROWS)
        yl = ys[0].reshape(N_EXPERTS, B, cap_l, D)
        xs = _combine(cnt, x1, jnp.swapaxes(posm, 1, 2), yl, None if last else ys[1], mod[l],
                      final_norm_g[None], cap_l, cc, final=last)
    return xs
```

```python
import functools

import numpy as np
import jax
import jax.numpy as jnp
from jax import lax
from jax.experimental import pallas as pl
from jax.experimental.pallas import tpu as pltpu

F32 = jnp.float32
BF16 = jnp.bfloat16
I32 = jnp.int32
HIGHEST = lax.Precision.HIGHEST

D_MODEL = 1024
GRID_W = 64
EPS = 1e-6
LB_FLOOR = 1e-30
N_MOD = 6
ATT_W = 512
HG_W = 256
CV_W = 256
HEAD_DIM = 64
N_Q = 8
N_KV = 2
GQA_GROUP = 4
KV_W = 128
ROPE_BASE = 10000.0
ROPE_FREQS = 16
HG_HEADS = 4
N_EXPERTS = 16
EC_CAPACITY = 2
D_EXPERT = 2048
IN_COLS = 2816

ROW_TILE = 256
ATT_BLOCK = 128
HG_CHUNK = 128
HG_LEVELS = 7
HG_BATCH = 2
HG_LOCAL = 32
HG_LOCAL_MAX_LOG = 80.0
SLOT_BLOCK = 256
FF_TILE = 512
EXPERT_ROWS = 512
MOD_TILE = 1536
VMEM_LIMIT = 56 * 1024 * 1024

NEG_BIG = -1e30


def _cparams(sem):
    return pltpu.CompilerParams(dimension_semantics=sem, vmem_limit_bytes=VMEM_LIMIT)


def _dot(a, b):
    return jnp.dot(a, b, preferred_element_type=F32)


def _dot_nt(a, b):
    return lax.dot_general(a, b, (((1,), (1,)), ((), ())), preferred_element_type=F32)


def _mod_kernel(a_ref, w_ref, b_ref, o_ref):
    a = a_ref[...]
    a = a * jax.nn.sigmoid(a)
    o_ref[0] = jnp.dot(a, w_ref[0], precision=HIGHEST, preferred_element_type=F32) + b_ref[0]


def _modulation(cvec, ada_w, ada_b):
    depth = ada_w.shape[0]
    ncol = ada_w.shape[2]
    return pl.pallas_call(
        _mod_kernel,
        grid=(depth, ncol // MOD_TILE),
        in_specs=[
            pl.BlockSpec((8, D_MODEL), lambda l, j: (0, 0)),
            pl.BlockSpec((1, D_MODEL, MOD_TILE), lambda l, j: (l, 0, j)),
            pl.BlockSpec((1, 1, MOD_TILE), lambda l, j: (l, 0, j)),
        ],
        out_specs=pl.BlockSpec((1, 8, MOD_TILE), lambda l, j: (l, 0, j)),
        out_shape=jax.ShapeDtypeStruct((depth, 8, ncol), F32),
        name="modulation",
        compiler_params=_cparams(("arbitrary", "arbitrary")),
    )(cvec, ada_w, ada_b.reshape(depth, 1, ncol))


def _swap_halves(x):
    n = x.shape[-1]
    lane = lax.broadcasted_iota(I32, x.shape, x.ndim - 1)
    up = pltpu.roll(x, n - ROPE_FREQS, x.ndim - 1)
    dn = pltpu.roll(x, ROPE_FREQS, x.ndim - 1)
    return jnp.where((lane % (2 * ROPE_FREQS)) < ROPE_FREQS, up, dn)


def _rmsnorm_mod(x, g, shift, scale):
    var = jnp.mean(x * x, axis=-1, keepdims=True)
    y = x * lax.rsqrt(var + EPS) * g
    return y * (1.0 + scale) + shift


def _row_sources(src):
    if isinstance(src, tuple):
        ctx, lat = src
        return ctx, lat, 1, ctx.shape[1] + lat.shape[1]
    return src, src, 0, src.shape[1]


def _inproj_kernel(xc_ref, x_ref, mod_ref, g_ref, w_ref, cos_ref, sin_ref,
                   q_ref, kv_ref, hg_ref, hqg_ref, cv_ref):
    b = pl.program_id(0)
    i = pl.program_id(1)
    row = jnp.where(i == 0, 4, b)
    shift = mod_ref[pl.ds(row, 1), 0:D_MODEL]
    scale = mod_ref[pl.ds(row, 1), D_MODEL:2 * D_MODEL]
    xin = jnp.where(i == 0, xc_ref[0], x_ref[0])
    h = _rmsnorm_mod(xin, g_ref[...], shift, scale)
    p = _dot(h.astype(BF16), w_ref[...])
    cos2 = cos_ref[...]
    sin2 = sin_ref[...]
    k = p[:, 0:KV_W]
    k = k * cos2 + _swap_halves(k) * sin2
    kv_ref[0, :, 0:KV_W] = k.astype(BF16)
    kv_ref[0, :, KV_W:2 * KV_W] = p[:, KV_W:2 * KV_W].astype(BF16)
    hg_ref[0] = p[:, 256:1024]
    q = p[:, 1024:1536]
    cos8 = jnp.concatenate([cos2] * 4, axis=1)
    sin8 = jnp.concatenate([sin2] * 4, axis=1)
    q = (q * cos8 + _swap_halves(q) * sin8) * (HEAD_DIM ** -0.5)
    q_ref[0] = q.astype(BF16)
    hqg_ref[0] = p[:, 1536:2048]
    cv_ref[0] = p[:, 2048:2816]


def _in_projection(src, mod_l, g, w_bf16, cos_t, sin_t):
    xc, xl, off, S = _row_sources(src)
    B = xc.shape[0]
    nt = S // ROW_TILE
    row_map = lambda b, i: (b, i, 0)
    const2 = lambda b, i: (0, 0)
    return pl.pallas_call(
        _inproj_kernel,
        grid=(B, nt),
        in_specs=[
            pl.BlockSpec((1, ROW_TILE, D_MODEL), lambda b, i: (b, 0, 0)),
            pl.BlockSpec((1, ROW_TILE, D_MODEL), lambda b, i: (b, jnp.maximum(i - off, 0), 0)),
            pl.BlockSpec((8, N_MOD * D_MODEL), const2),
            pl.BlockSpec((1, D_MODEL), const2),
            pl.BlockSpec((D_MODEL, IN_COLS), const2),
            pl.BlockSpec((ROW_TILE, 2 * HEAD_DIM), lambda b, i: (i, 0)),
            pl.BlockSpec((ROW_TILE, 2 * HEAD_DIM), lambda b, i: (i, 0)),
        ],
        out_specs=[
            pl.BlockSpec((1, ROW_TILE, ATT_W), row_map),
            pl.BlockSpec((1, ROW_TILE, 2 * KV_W), row_map),
            pl.BlockSpec((1, ROW_TILE, 3 * HG_W), row_map),
            pl.BlockSpec((1, ROW_TILE, 2 * HG_W), row_map),
            pl.BlockSpec((1, ROW_TILE, 3 * CV_W), row_map),
        ],
        out_shape=[
            jax.ShapeDtypeStruct((B, S, ATT_W), BF16),
            jax.ShapeDtypeStruct((B, S, 2 * KV_W), BF16),
            jax.ShapeDtypeStruct((B, S, 3 * HG_W), F32),
            jax.ShapeDtypeStruct((B, S, 2 * HG_W), F32),
            jax.ShapeDtypeStruct((B, S, 3 * CV_W), F32),
        ],
        name="in_projection",
        compiler_params=_cparams(("arbitrary", "arbitrary")),
    )(xc, xl, mod_l, g, w_bf16, cos_t, sin_t)


def _attn_kernel(sink_ref, q_ref, kp_ref, kc_ref, kn_ref, kx_ref, o_ref, *, blk0, nblk, nctx):
    n = pl.program_id(1) + blk0
    is_lat = n >= nctx
    has_prev = n > nctx
    has_next = n < nblk - 1
    W = ATT_BLOCK
    q = q_ref[0]
    qrows = jnp.concatenate([q[:, g * W:(g + 1) * W] for g in range(GQA_GROUP)], axis=0)
    kv_all = jnp.concatenate([kp_ref[0], kc_ref[0], kn_ref[0], kx_ref[0]], axis=0)
    nkeys = kv_all.shape[0]
    k_all = kv_all[:, 0:KV_W]
    v_ext = jnp.concatenate([kv_all[:, KV_W:2 * KV_W], jnp.ones((nkeys, KV_W), BF16)], axis=1)
    rows = GQA_GROUP * W
    ri = lax.broadcasted_iota(I32, (rows, W), 0) % W
    cj = lax.broadcasted_iota(I32, (rows, W), 1)
    m_prev = (cj >= ri) & has_prev
    m_cur = jnp.broadcast_to(is_lat, (rows, W))
    m_next = (cj <= ri) & (has_next & is_lat)
    grp = lax.broadcasted_iota(I32, (rows, 1), 0) // W
    low = cj < HEAD_DIM
    outs = []
    for h in range(N_KV):
        qh = jnp.where(low if h == 0 else jnp.logical_not(low), qrows, jnp.zeros_like(qrows))
        sink = jnp.zeros((rows, 1), F32)
        for g in range(GQA_GROUP):
            sink = jnp.where(grp == g, sink_ref[h * GQA_GROUP + g], sink)
        s = _dot_nt(qh, k_all)
        segs = [jnp.where(m_prev, s[:, 0:W], NEG_BIG),
                jnp.where(m_cur, s[:, W:2 * W], NEG_BIG),
                jnp.where(m_next, s[:, 2 * W:3 * W], NEG_BIG)]
        segs += [s[:, c:c + W] for c in range(3 * W, nkeys, W)]
        mx = segs[0]
        for sg in segs[1:]:
            mx = jnp.maximum(mx, sg)
        m = jnp.maximum(jnp.max(mx, axis=1, keepdims=True), sink)
        p = jnp.concatenate([jnp.exp(sg - m).astype(BF16) for sg in segs], axis=1)
        oe = _dot(p, v_ext)
        den = oe[:, KV_W:2 * KV_W] + jnp.exp(sink - m)
        outs.append(oe[:, 0:KV_W] / den)
    o = jnp.where(low, outs[0], outs[1])
    for g in range(GQA_GROUP):
        o_ref[0, :, g * W:(g + 1) * W] = o[g * W:(g + 1) * W].astype(BF16)


def _attention(q, kv, sink, ctx_len, skip_ctx):
    B, S, _ = q.shape
    nblk = S // ATT_BLOCK
    nctx = ctx_len // ATT_BLOCK
    blk0 = nctx if skip_ctx else 0
    blk = lambda f: (lambda b, n: (b, f(n + blk0), 0))
    kern = functools.partial(_attn_kernel, blk0=blk0, nblk=nblk, nctx=nctx)
    return pl.pallas_call(
        kern,
        grid=(B, nblk - blk0),
        in_specs=[
            pl.BlockSpec(memory_space=pltpu.SMEM),
            pl.BlockSpec((1, ATT_BLOCK, ATT_W), blk(lambda n: n)),
            pl.BlockSpec((1, ATT_BLOCK, 2 * KV_W), blk(lambda n: jnp.maximum(n - 1, 0))),
            pl.BlockSpec((1, ATT_BLOCK, 2 * KV_W), blk(lambda n: n)),
            pl.BlockSpec((1, ATT_BLOCK, 2 * KV_W), blk(lambda n: jnp.minimum(n + 1, nblk - 1))),
            pl.BlockSpec((1, ctx_len, 2 * KV_W), lambda b, n: (b, 0, 0)),
        ],
        out_specs=pl.BlockSpec((1, ATT_BLOCK, ATT_W), lambda b, n: (b, n, 0)),
        out_shape=jax.ShapeDtypeStruct((B, S - blk0 * ATT_BLOCK, ATT_W), BF16),
        name="attention",
        compiler_params=_cparams(("arbitrary", "arbitrary")),
    )(sink, q, kv, kv, kv, kv)


def _hgrn_constants():
    C = HG_CHUNK
    t = np.arange(C)[:, None]
    r = np.arange(C)[None, :]
    tri = np.stack([r <= t, r >= t]).astype(np.float32)
    x = t ^ r
    lvl = np.where(x > 0, np.floor(np.log2(np.maximum(x, 1))).astype(np.int32), HG_LEVELS)
    lvl_f = np.where(t >= r, lvl, -1).astype(np.int32)
    lvl_b = np.where(t <= r, lvl, -1).astype(np.int32)
    lvl2 = np.stack([np.tile(lvl_f, (1, HG_HEADS)), np.tile(lvl_b, (1, HG_HEADS))])
    return tri, lvl2


def _span_row(x, span, row):
    C = x.shape[0]
    if span >= 8:
        x3 = x.reshape(C // span, span, x.shape[1])
        return jnp.broadcast_to(x3[:, row:row + 1, :], x3.shape).reshape(x.shape)
    pos = lax.broadcasted_iota(I32, x.shape, 0) % span
    out = x
    for p in range(span):
        if p != row:
            out = jnp.where(pos == p, pltpu.roll(x, (p - row) % C, 0), out)
    return out


def _hgrn_prepare(v, z, qr, lb, tri, backward):
    C = HG_CHUNK
    logf = jnp.log(jnp.maximum(lb, LB_FLOOR) + (1.0 - lb) * jax.nn.sigmoid(z))
    k = (1.0 - lb) * jax.nn.sigmoid(-z)
    q = qr * jax.nn.sigmoid(qr)
    hi = logf.astype(BF16)
    lo = (logf - hi.astype(F32)).astype(BF16)
    cs = _dot(tri, jnp.concatenate([hi, lo], axis=1))
    lam = cs[:, 0:HG_W] + cs[:, HG_W:2 * HG_W]
    tot = lam[0:1] if backward else lam[C - 1:C]
    local = lam - _span_row(lam, HG_LOCAL, HG_LOCAL // 2)
    return dict(v=v, q=q, k=k, lam=lam, tot=tot, local=local)


def _hgrn_finish(g, head_masks, same_head, lvl, st_ref, sidx, backward, fast):
    C = HG_CHUNK
    q, k, v, lam, tot = g["q"], g["k"], g["v"], g["lam"], g["tot"]
    zero = jnp.zeros((C, HG_W), BF16)

    def per_head_rows(x):
        return jnp.concatenate([jnp.where(hm, x, zero) for hm in head_masks], axis=0)

    qb = q.astype(BF16)
    kb = k.astype(BF16)
    if fast:
        n_local = HG_LOCAL.bit_length() - 1
        a = jnp.where((lvl == HG_LEVELS) | ((lvl >= 0) & (lvl < n_local)),
                      _dot_nt((q * jnp.exp(g["local"])).astype(BF16),
                              per_head_rows((k * jnp.exp(-g["local"])).astype(BF16))), 0.0)
        levels = range(n_local, HG_LEVELS)
    else:
        a = jnp.where(lvl == HG_LEVELS, _dot_nt(qb, per_head_rows(kb)), 0.0)
        levels = range(HG_LEVELS)
    for l in levels:
        m = 1 << l
        ref = _span_row(lam, 2 * m, m if backward else m - 1)
        fac = jnp.exp(-jnp.abs(lam - ref)).astype(BF16)
        a = jnp.where(lvl == l, _dot_nt(qb * fac, per_head_rows(kb * fac)), a)
    st = st_ref[sidx]
    q_in = (q * jnp.exp(lam)).astype(BF16)
    o = _dot(a.astype(BF16), per_head_rows(v.astype(BF16))) + _dot_nt(q_in, st.astype(BF16))
    k_out = (k * jnp.exp(tot - lam)).astype(BF16)
    upd = _dot(v.T.astype(BF16), k_out)
    st_ref[sidx] = st * jnp.exp(tot) + jnp.where(same_head, upd, 0.0)
    return o


def _hgrn_kernel(vf_ref, zf_ref, qf_ref, vb_ref, zb_ref, qb_ref, lb_ref, tri_ref, lvl_ref,
                 of_ref, ob_ref, st_ref):
    j = pl.program_id(1)

    @pl.when(j == 0)
    def _():
        st_ref[...] = jnp.zeros_like(st_ref)

    head_id = (lax.broadcasted_iota(I32, (HG_CHUNK, HG_W), 1) // HEAD_DIM).astype(F32).astype(BF16)
    head_masks = [head_id == float(h) for h in range(HG_HEADS)]
    same_head = (lax.broadcasted_iota(I32, (HG_W, HG_W), 0) // HEAD_DIM
                 == lax.broadcasted_iota(I32, (HG_W, HG_W), 1) // HEAD_DIM)
    streams = []
    for d, (v_ref, z_ref, q_ref, o_ref) in enumerate(
            ((vf_ref, zf_ref, qf_ref, of_ref), (vb_ref, zb_ref, qb_ref, ob_ref))):
        for s in range(v_ref.shape[0]):
            g = _hgrn_prepare(v_ref[s], z_ref[s], q_ref[s], lb_ref[d], tri_ref[d], d == 1)
            streams.append((d, s, o_ref, g))
    worst = jnp.abs(streams[0][3]["local"])
    for _, _, _, g in streams[1:]:
        worst = jnp.maximum(worst, jnp.abs(g["local"]))
    can_use_local = jnp.max(worst) <= HG_LOCAL_MAX_LOG

    def run(fast):
        for d, s, o_ref, g in streams:
            o_ref[s] = _hgrn_finish(g, head_masks, same_head, lvl_ref[d], st_ref, 2 * s + d,
                                    backward=(d == 1), fast=fast)

    pl.when(can_use_local)(functools.partial(run, True))
    pl.when(jnp.logical_not(can_use_local))(functools.partial(run, False))


def _hgrn(hg, hqg, lb_l, ctx_len):
    B, S, _ = hg.shape
    nc = S // HG_CHUNK
    nctx = ctx_len // HG_CHUNK
    tri_np, lvl_np = _hgrn_constants()
    tri = jnp.asarray(tri_np, BF16)
    lvl = jnp.asarray(lvl_np, I32)

    def back(j):
        return jnp.where(j < nctx, nctx - 1 - j, nc - 1 - (j - nctx))

    bs = HG_BATCH if B % HG_BATCH == 0 else 1
    blk = (bs, HG_CHUNK, HG_W)
    fwd = lambda col: (lambda b, j: (b, j, col))
    bwd = lambda col: (lambda b, j: (b, back(j), col))
    const3 = lambda b, j: (0, 0, 0)
    return pl.pallas_call(
        _hgrn_kernel,
        grid=(B // bs, nc),
        in_specs=[
            pl.BlockSpec(blk, fwd(0)), pl.BlockSpec(blk, fwd(1)), pl.BlockSpec(blk, fwd(0)),
            pl.BlockSpec(blk, bwd(0)), pl.BlockSpec(blk, bwd(2)), pl.BlockSpec(blk, bwd(0)),
            pl.BlockSpec((2, 1, HG_W), const3),
            pl.BlockSpec((2, HG_CHUNK, HG_CHUNK), const3),
            pl.BlockSpec((2, HG_CHUNK, HG_HEADS * HG_CHUNK), const3),
        ],
        out_specs=[pl.BlockSpec(blk, fwd(0)), pl.BlockSpec(blk, bwd(0))],
        out_shape=[jax.ShapeDtypeStruct((B, S, HG_W), F32)] * 2,
        scratch_shapes=[pltpu.VMEM((2 * bs, HG_W, HG_W), F32)],
        name="hgrn_scan",
        compiler_params=_cparams(("arbitrary", "arbitrary")),
    )(hg, hg, hqg, hg, hg, hqg, lb_l.reshape(2, 1, HG_W), tri, lvl)


def _mixout_kernel(xc_ref, x_ref, att_ref, of_ref, ob_ref, g_ref, cv_ref, cprev_ref, cnext_ref,
                   mod_ref, gain_ref, cw_ref, wo_ref, n2_ref, wr_ref, ones_ref,
                   x1_ref, h2_ref, h2t_ref, aff_ref, *, tile0, ntile):
    b = pl.program_id(0)
    i = pl.program_id(1) + tile0
    row = jnp.where(i == 0, 4, b)
    R = ROW_TILE

    def modv(c):
        return mod_ref[pl.ds(row, 1), c * D_MODEL:(c + 1) * D_MODEL]

    o = of_ref[0] + ob_ref[0]
    sq = o * o
    sq_hi = sq.astype(BF16)
    sq_lo = (sq - sq_hi.astype(F32)).astype(BF16)
    ms = _dot(sq_hi, ones_ref[...]) + _dot(sq_lo, ones_ref[...])
    g = g_ref[0]
    hg = o * lax.rsqrt(ms * (1.0 / HEAD_DIM) + EPS) * gain_ref[...] * (g * jax.nn.sigmoid(g))
    cv = cv_ref[0]
    u = cv[:, CV_W:2 * CV_W] * cv[:, 2 * CV_W:3 * CV_W]
    up = cprev_ref[0]
    un = cnext_ref[0]
    u_prev_row = up[7:8, CV_W:2 * CV_W] * up[7:8, 2 * CV_W:3 * CV_W]
    u_next_row = un[0:1, CV_W:2 * CV_W] * un[0:1, 2 * CV_W:3 * CV_W]
    u_prev_row = jnp.where(i <= 1, 0.0, u_prev_row)
    u_next_row = jnp.where((i == 0) | (i == ntile - 1), 0.0, u_next_row)
    ridx = lax.broadcasted_iota(I32, (R, CV_W), 0)
    u_m1 = jnp.where(ridx == 0, u_prev_row, pltpu.roll(u, 1, 0))
    u_p1 = jnp.where(ridx == R - 1, u_next_row, pltpu.roll(u, R - 1, 0))
    cw = cw_ref[...]
    conv = cv[:, 0:CV_W] * (u_m1 * cw[0:1] + u * cw[1:2] + u_p1 * cw[2:3])
    mix = (_dot(att_ref[0], wo_ref[0:ATT_W])
           + _dot(hg.astype(BF16), wo_ref[ATT_W:ATT_W + HG_W])
           + _dot(conv.astype(BF16), wo_ref[ATT_W + HG_W:D_MODEL]))
    x1 = jnp.where(i == 0, xc_ref[0], x_ref[0]) + modv(2) * mix
    x1_ref[0] = x1
    h2 = _rmsnorm_mod(x1, n2_ref[...], modv(3), modv(4))
    h_hi = h2.astype(BF16)
    h2_ref[0] = h_hi
    h2t_ref[0] = h2.T.astype(BF16)
    h_lo = (h2 - h_hi.astype(F32)).astype(BF16)
    wr = wr_ref[...]
    w_hi = wr.astype(BF16)
    w_lo = (wr - w_hi.astype(F32)).astype(BF16)
    r1 = _dot_nt(jnp.concatenate([w_hi, w_lo], axis=0), h_hi)
    logits = r1[0:N_EXPERTS] + r1[N_EXPERTS:2 * N_EXPERTS] + _dot_nt(w_hi, h_lo)
    e = jnp.exp(logits - jnp.max(logits, axis=0, keepdims=True))
    aff_ref[0] = e / jnp.sum(e, axis=0, keepdims=True)


def _mix_out(src, att, o2, hqg, cv, mod_l, gain, conv_w, wo_bf16, n2g, wr_t, skip_ctx):
    xc, xl, off, S = _row_sources(src)
    B = xc.shape[0]
    ntile = S // ROW_TILE
    tile0 = 1 if skip_ctx else 0
    sub = ROW_TILE // 8
    nsub = S // 8
    rmap = lambda b, i: (b, i + tile0, 0)
    omap = lambda b, i: (b, i, 0)
    s_out = S - tile0 * ROW_TILE
    const2 = lambda b, i: (0, 0)
    ones = jnp.asarray(np.kron(np.eye(HG_HEADS), np.ones((HEAD_DIM, HEAD_DIM))), BF16)
    kern = functools.partial(_mixout_kernel, tile0=tile0, ntile=ntile)
    return pl.pallas_call(
        kern,
        grid=(B, ntile - tile0),
        in_specs=[
            pl.BlockSpec((1, ROW_TILE, D_MODEL), lambda b, i: (b, 0, 0)),
            pl.BlockSpec((1, ROW_TILE, D_MODEL),
                         lambda b, i: (b, jnp.maximum(i + tile0 - off, 0), 0)),
            pl.BlockSpec((1, ROW_TILE, ATT_W), omap),
            pl.BlockSpec((1, ROW_TILE, HG_W), rmap),
            pl.BlockSpec((1, ROW_TILE, HG_W), rmap),
            pl.BlockSpec((1, ROW_TILE, HG_W), lambda b, i: (b, i + tile0, 1)),
            pl.BlockSpec((1, ROW_TILE, 3 * CV_W), rmap),
            pl.BlockSpec((1, 8, 3 * CV_W),
                         lambda b, i: (b, jnp.maximum((i + tile0) * sub - 1, 0), 0)),
            pl.BlockSpec((1, 8, 3 * CV_W),
                         lambda b, i: (b, jnp.minimum((i + tile0 + 1) * sub, nsub - 1), 0)),
            pl.BlockSpec((8, N_MOD * D_MODEL), const2),
            pl.BlockSpec((1, HG_W), const2),
            pl.BlockSpec((3, CV_W), const2),
            pl.BlockSpec((D_MODEL, D_MODEL), const2),
            pl.BlockSpec((1, D_MODEL), const2),
            pl.BlockSpec((N_EXPERTS, D_MODEL), const2),
            pl.BlockSpec((HG_W, HG_W), const2),
        ],
        out_specs=[
            pl.BlockSpec((1, ROW_TILE, D_MODEL), omap),
            pl.BlockSpec((1, ROW_TILE, D_MODEL), omap),
            pl.BlockSpec((1, D_MODEL, ROW_TILE), lambda b, i: (b, 0, i)),
            pl.BlockSpec((1, N_EXPERTS, ROW_TILE), lambda b, i: (b, 0, i)),
        ],
        out_shape=[
            jax.ShapeDtypeStruct((B, s_out, D_MODEL), F32),
            jax.ShapeDtypeStruct((B, s_out, D_MODEL), BF16),
            jax.ShapeDtypeStruct((B, D_MODEL, s_out), BF16),
            jax.ShapeDtypeStruct((B, N_EXPERTS, s_out), F32),
        ],
        name="mix_out",
        compiler_params=_cparams(("arbitrary", "arbitrary")),
    )(xc, xl, att, o2[0], o2[1], hqg, cv, cv, cv, mod_l, gain, conv_w, wo_bf16, n2g, wr_t, ones)


def _topk_kernel(aff_ref, tri_ref, pos_ref, cnt_ref, *, segments):
    tri = tri_ref[...]

    def excl_cumsum(mask, n):
        carry = jnp.zeros((N_EXPERTS, 1), F32)
        parts = []
        carries = []
        for c in range(n // 128):
            carries.append(carry)
            blk = jnp.where(mask[:, c * 128:(c + 1) * 128], 1.0, 0.0)
            parts.append(_dot(blk.astype(BF16), tri) + carry)
            carry = carry + jnp.sum(blk, axis=1, keepdims=True)
        return jnp.concatenate(parts, axis=1), carries + [carry]

    for (lo, n, k) in segments:
        a = aff_ref[0, :, lo:lo + n]

        def body(it, thr):
            cand = thr | jnp.left_shift(jnp.int32(1), 30 - it)
            cnt = jnp.sum(jnp.where(a >= pltpu.bitcast(cand, F32), 1.0, 0.0), axis=1, keepdims=True)
            return jnp.where(cnt >= k, cand, thr)

        thr = lax.fori_loop(0, 31, body, jnp.zeros((N_EXPERTS, 1), I32))
        above = a >= pltpu.bitcast(thr + 1, F32)
        tied = jnp.logical_and(a >= pltpu.bitcast(thr, F32), jnp.logical_not(above))
        n_above = jnp.sum(jnp.where(above, 1.0, 0.0), axis=1, keepdims=True)
        rank_tied, _ = excl_cumsum(tied, n)
        sel = above | (tied & (rank_tied < (k - n_above)))
        pos, counts = excl_cumsum(sel, n)
        pos_ref[0, :, lo:lo + n] = jnp.where(sel, pos.astype(I32), -1)
    step = ROW_TILE // 128
    lane = lax.broadcasted_iota(I32, (N_EXPERTS, 128), 1)
    cnt = jnp.zeros((N_EXPERTS, 128), F32)
    for c, col in enumerate(counts[0::step]):
        cnt = jnp.where(lane == c, col, cnt)
    cnt_ref[0] = cnt.astype(I32)


def _topk_positions(aff, segments):
    B, E, S = aff.shape
    assert segments[-1][1] // ROW_TILE + 1 <= 128
    tri = jnp.asarray(np.triu(np.ones((128, 128)), 1), BF16)
    kern = functools.partial(_topk_kernel, segments=segments)
    return pl.pallas_call(
        kern,
        grid=(B,),
        in_specs=[pl.BlockSpec((1, E, S), lambda b: (b, 0, 0)),
                  pl.BlockSpec((128, 128), lambda b: (0, 0))],
        out_specs=[pl.BlockSpec((1, E, S), lambda b: (b, 0, 0)),
                   pl.BlockSpec((1, E, 128), lambda b: (b, 0, 0))],
        out_shape=[jax.ShapeDtypeStruct((B, E, S), I32),
                   jax.ShapeDtypeStruct((B, E, 128), I32)],
        name="topk_positions",
        compiler_params=_cparams(("arbitrary",)),
    )(aff, tri)


def _slot_block(cap):
    sbs = min(SLOT_BLOCK, cap)
    assert sbs >= ROW_TILE or sbs == cap
    return sbs


def _div_pow2(x, d):
    assert d & (d - 1) == 0
    return lax.shift_right_logical(x, jnp.int32(d.bit_length() - 1))


def _gather_kernel(cnt_ref, pos_ref, aff_ref, ht_ref, *refs, ctx_len, cap_l, cap_c):
    if cap_c:
        hc_ref, xl_ref, gl_ref, xc_ref, gc_ref, acc_ref, gacc_ref = refs
    else:
        xl_ref, gl_ref, acc_ref, gacc_ref = refs
    b = pl.program_id(0)
    e = pl.program_id(1)
    S = ht_ref.shape[2]
    pos = pos_ref[0, pl.ds(e, 1), :]
    aff = aff_ref[0, pl.ds(e, 1), :]
    nchunk = (S - ctx_len) // ROW_TILE
    sbs = _slot_block(cap_l)
    acc_ref[...] = jnp.zeros_like(acc_ref)
    gacc_ref[...] = jnp.zeros_like(gacc_ref)
    nsb = cap_l // sbs
    first = [jnp.minimum(_div_pow2(cnt_ref[b, e, c], sbs), nsb - 1) for c in range(nchunk)]
    slot_i32 = lax.broadcasted_iota(I32, (sbs, ROW_TILE), 0)
    slot_in_block = slot_i32.astype(F32).astype(BF16)
    one = jnp.ones((sbs, ROW_TILE), BF16)
    zero = jnp.zeros((sbs, ROW_TILE), BF16)

    def visit(c, sb):
        lo = ctx_len + c * ROW_TILE
        base = pl.multiple_of(sb * sbs, sbs)
        rel_i = pos[:, lo:lo + ROW_TILE] - base
        rel = rel_i.astype(F32)
        rel = jnp.where((rel >= 0.0) & (rel < float(sbs)), rel, -1.0).astype(BF16)
        onehot = jnp.where(rel == slot_in_block, one, zero)
        for r0 in range(0, D_MODEL, D_MODEL // 2):
            rows = slice(r0, r0 + D_MODEL // 2)
            acc_ref[sb, rows, :] += _dot_nt(ht_ref[0, rows, lo:lo + ROW_TILE], onehot)
        gacc_ref[pl.ds(base, sbs), :] += jnp.sum(
            jnp.where(rel_i == slot_i32, aff[:, lo:lo + ROW_TILE], 0.0), axis=1, keepdims=True)

    for c in range(nchunk):
        visit(c, first[c])
    for c in range(nchunk):
        pl.when(cnt_ref[b, e, c + 1] > (first[c] + 1) * sbs)(
            functools.partial(visit, c, first[c] + 1))
    xl_ref[0, 0] = acc_ref[...].astype(BF16)
    gl_ref[0, 0] = gacc_ref[...]
    if cap_c:
        slot_c = lax.broadcasted_iota(I32, (cap_c, ctx_len), 0)
        hit = pos[:, 0:ctx_len] == slot_c
        xc_ref[0, 0] = _dot(jnp.where(hit, 1.0, 0.0).astype(BF16), hc_ref[0]).astype(BF16)
        gc_ref[0, 0] = jnp.sum(jnp.where(hit, aff[:, 0:ctx_len], 0.0), axis=1, keepdims=True)


def _gather_rows(cnt, posm, aff, h2, h2t, ctx_len, cap_l, cap_c):
    B, E, S = posm.shape
    sbs = _slot_block(cap_l)
    nsb = cap_l // sbs
    kern = functools.partial(_gather_kernel, ctx_len=ctx_len, cap_l=cap_l, cap_c=cap_c)
    omap = lambda b, e, cnt_ref: (e, b, 0, 0)
    in_specs = [pl.BlockSpec((1, E, S), lambda b, e, cnt_ref: (b, 0, 0)),
                pl.BlockSpec((1, E, S), lambda b, e, cnt_ref: (b, 0, 0)),
                pl.BlockSpec((1, D_MODEL, S), lambda b, e, cnt_ref: (b, 0, 0))]
    args = [cnt, posm, aff, h2t]
    out_specs = [pl.BlockSpec((1, 1, nsb, D_MODEL, sbs), lambda b, e, cnt_ref: (e, b, 0, 0, 0)),
                 pl.BlockSpec((1, 1, cap_l, 1), omap)]
    out_shape = [jax.ShapeDtypeStruct((E, B, nsb, D_MODEL, sbs), BF16),
                 jax.ShapeDtypeStruct((E, B, cap_l, 1), F32)]
    if cap_c:
        in_specs.append(pl.BlockSpec((1, ctx_len, D_MODEL), lambda b, e, cnt_ref: (b, 0, 0)))
        args.append(h2)
        out_specs += [pl.BlockSpec((1, 1, cap_c, D_MODEL), omap),
                      pl.BlockSpec((1, 1, cap_c, 1), omap)]
        out_shape += [jax.ShapeDtypeStruct((E, B, cap_c, D_MODEL), BF16),
                      jax.ShapeDtypeStruct((E, B, cap_c, 1), F32)]
    grid_spec = pltpu.PrefetchScalarGridSpec(
        num_scalar_prefetch=1,
        grid=(B, E),
        in_specs=in_specs,
        out_specs=out_specs,
        scratch_shapes=[pltpu.VMEM((nsb, D_MODEL, sbs), F32), pltpu.VMEM((cap_l, 1), F32)],
    )
    return pl.pallas_call(
        kern,
        grid_spec=grid_spec,
        out_shape=out_shape,
        name="gather_rows",
        compiler_params=_cparams(("arbitrary", "arbitrary")),
    )(*args)


def _expert_kernel(*refs, row_chunk, has_ctx):
    if has_ctx:
        (xt_ref, gl_ref, xc_ref, gc_ref, wg_ref, wu_ref, wd_ref, yt_ref, yc_ref,
         xs_ref, accl_ref, accc_ref) = refs
    else:
        xt_ref, gl_ref, wg_ref, wu_ref, wd_ref, yt_ref, xs_ref, accl_ref = refs
    f = pl.program_id(1)
    nf = pl.num_programs(1)
    _, nb, nsb, _, sbs = xt_ref.shape
    blocks = [(b, sb, slice((b * nsb + sb) * sbs, (b * nsb + sb + 1) * sbs))
              for b in range(nb) for sb in range(nsb)]
    wg = wg_ref[0, 0].astype(BF16)
    wu = wu_ref[0, 0].astype(BF16)
    wd = wd_ref[0, 0].astype(BF16)

    @pl.when(f == 0)
    def _():
        accl_ref[...] = jnp.zeros_like(accl_ref)
        if has_ctx:
            accc_ref[...] = jnp.zeros_like(accc_ref)
        for b, sb, rs in blocks:
            xs_ref[rs, :] = xt_ref[0, b, sb].astype(F32).T.astype(BF16)

    def mlp_rows(x):
        g = _dot(x, wg)
        u = _dot(x, wu)
        hid = (g * jax.nn.sigmoid(g) * u).astype(BF16)
        return _dot(hid, wd)

    rows = xs_ref.shape[0]
    step = min(row_chunk, rows)
    for r in range(rows // step):
        rs = slice(r * step, (r + 1) * step)
        accl_ref[rs, :] += mlp_rows(xs_ref[rs, :])
    if has_ctx:
        accc_ref[...] += mlp_rows(xc_ref[0])

    @pl.when(f == nf - 1)
    def _():
        for b, sb, rs in blocks:
            yt_ref[0, b, sb] = (accl_ref[rs, :] * gl_ref[0, rs, :]).T.astype(BF16)
        if has_ctx:
            yc_ref[0] = (accc_ref[...] * gc_ref[0]).astype(BF16)


def _expert_mlp(xt, gl, xc, gc, w_gate, w_up, w_down, layer, row_chunk):
    E, nb, nsb, _, sbs = xt.shape
    rows = nb * nsb * sbs
    nf = D_EXPERT // FF_TILE
    has_ctx = xc is not None
    kern = functools.partial(_expert_kernel, row_chunk=row_chunk, has_ctx=has_ctx)
    tspec = pl.BlockSpec((1, nb, nsb, D_MODEL, sbs), lambda e, f: (e, 0, 0, 0, 0))
    in_specs = [tspec, pl.BlockSpec((1, rows, 1), lambda e, f: (e, 0, 0))]
    args = [xt, gl]
    out_specs = [tspec]
    out_shape = [jax.ShapeDtypeStruct(xt.shape, BF16)]
    scratch = [pltpu.VMEM((rows, D_MODEL), BF16), pltpu.VMEM((rows, D_MODEL), F32)]
    if has_ctx:
        rc = xc.shape[1]
        cspec = pl.BlockSpec((1, rc, D_MODEL), lambda e, f: (e, 0, 0))
        in_specs += [cspec, pl.BlockSpec((1, rc, 1), lambda e, f: (e, 0, 0))]
        args += [xc, gc]
        out_specs.append(cspec)
        out_shape.append(jax.ShapeDtypeStruct(xc.shape, BF16))
        scratch.append(pltpu.VMEM((rc, D_MODEL), F32))
    in_specs += [pl.BlockSpec((1, 1, D_MODEL, FF_TILE), lambda e, f: (layer, e, 0, f)),
                 pl.BlockSpec((1, 1, D_MODEL, FF_TILE), lambda e, f: (layer, e, 0, f)),
                 pl.BlockSpec((1, 1, FF_TILE, D_MODEL), lambda e, f: (layer, e, f, 0))]
    return pl.pallas_call(
        kern,
        grid=(E, nf),
        in_specs=in_specs,
        out_specs=out_specs,
        out_shape=out_shape,
        scratch_shapes=scratch,
        name="expert_mlp",
        compiler_params=_cparams(("arbitrary", "arbitrary")),
    )(*args, w_gate, w_up, w_down)


def _combine_kernel(cnt_ref, x_ref, pos_ref, *refs, cap_l, cap_c, final):
    if cap_c:
        yl_ref, yc_ref, mod_ref, fg_ref, o_ref, acc_ref, acct_ref = refs
    else:
        yl_ref, mod_ref, fg_ref, o_ref, acc_ref, acct_ref = refs
    b = pl.program_id(0)
    i = pl.program_id(1)
    row = jnp.where(i == 0, 4, b) if cap_c else b
    gate = mod_ref[pl.ds(row, 1), 5 * D_MODEL:6 * D_MODEL]
    pos = pos_ref[0]
    acc_ref[...] = jnp.zeros_like(acc_ref)
    acct_ref[...] = jnp.zeros_like(acct_ref)
    sbs = _slot_block(cap_l)

    def add_latent():
        t = i - 1 if cap_c else i
        nsb = cap_l // sbs
        half = D_MODEL // 2

        def products(e, sb):
            lane = lax.broadcasted_iota(I32, (ROW_TILE, sbs), 1) + sb * sbs
            onehot = jnp.where(pos[:, e:e + 1] == lane, 1.0, 0.0).astype(BF16)
            return [_dot_nt(yl_ref[e, 0, sb, r0:r0 + half, :], onehot) for r0 in (0, half)]

        first = [jnp.minimum(_div_pow2(cnt_ref[b, e, t], sbs), nsb - 1) for e in range(N_EXPERTS)]
        totals = products(0, first[0])
        for e in range(1, N_EXPERTS):
            totals = [tot + p for tot, p in zip(totals, products(e, first[e]))]
        acct_ref[0:half, :] += totals[0]
        acct_ref[half:D_MODEL, :] += totals[1]
        for e in range(N_EXPERTS):
            def crossing(e=e):
                p = products(e, first[e] + 1)
                acct_ref[0:half, :] += p[0]
                acct_ref[half:D_MODEL, :] += p[1]

            pl.when(cnt_ref[b, e, t + 1] > (first[e] + 1) * sbs)(crossing)

    def add_context():
        ncol = yc_ref.shape[1]
        lane = lax.broadcasted_iota(I32, (ROW_TILE, ncol), 1)
        for e in range(N_EXPERTS):
            pe = pos[:, e:e + 1]
            tgt = jnp.where(pe >= 0, pe + b * cap_c, -1)
            acc_ref[...] += _dot(jnp.where(tgt == lane, 1.0, 0.0).astype(BF16), yc_ref[e])

    if cap_c:
        pl.when(i == 0)(add_context)
        pl.when(i > 0)(add_latent)
    else:
        add_latent()
    x2 = x_ref[0] + gate * (acc_ref[...] + acct_ref[...].T)
    if final:
        var = jnp.mean(x2 * x2, axis=-1, keepdims=True)
        x2 = x2 * lax.rsqrt(var + EPS) * fg_ref[...]
    o_ref[0] = x2


def _combine(cnt, x1, pos_t, yl, yc, mod_l, final_g, cap_l, cap_c, final):
    B, S, _ = x1.shape
    E = N_EXPERTS
    rmap = lambda b, i, cnt_ref: (b, i, 0)
    const2 = lambda b, i, cnt_ref: (0, 0)
    kern = functools.partial(_combine_kernel, cap_l=cap_l, cap_c=cap_c, final=final)
    y_specs = [pl.BlockSpec((E, 1) + yl.shape[2:], lambda b, i, cnt_ref: (0, b, 0, 0, 0),
                            pipeline_mode=pl.Buffered(1))]
    y_args = [yl]
    if cap_c:
        y_specs.append(pl.BlockSpec(yc.shape, lambda b, i, cnt_ref: (0, 0, 0),
                                    pipeline_mode=pl.Buffered(1)))
        y_args.append(yc)
    grid_spec = pltpu.PrefetchScalarGridSpec(
        num_scalar_prefetch=1,
        grid=(B, S // ROW_TILE),
        in_specs=[pl.BlockSpec((1, ROW_TILE, D_MODEL), rmap),
                  pl.BlockSpec((1, ROW_TILE, E), rmap)] + y_specs + [
                  pl.BlockSpec((8, N_MOD * D_MODEL), const2),
                  pl.BlockSpec((1, D_MODEL), const2)],
        out_specs=pl.BlockSpec((1, ROW_TILE, D_MODEL), rmap),
        scratch_shapes=[pltpu.VMEM((ROW_TILE, D_MODEL), F32), pltpu.VMEM((D_MODEL, ROW_TILE), F32)],
    )
    return pl.pallas_call(
        kern,
        grid_spec=grid_spec,
        out_shape=jax.ShapeDtypeStruct((B, S, D_MODEL), F32),
        name="combine",
        compiler_params=_cparams(("arbitrary", "arbitrary")),
    )(cnt, x1, pos_t, *y_args, mod_l, final_g)


def _rope_tables(n_lat, ctx_len):
    t = np.arange(n_lat)
    pos = np.stack([t // GRID_W, t % GRID_W], axis=-1).astype(np.float32)
    inv = (ROPE_BASE ** (-np.arange(ROPE_FREQS, dtype=np.float32) / ROPE_FREQS)).astype(np.float32)
    ang = pos[:, :, None] * inv
    cos = np.cos(ang).astype(np.float32)
    sin = np.sin(ang).astype(np.float32)
    cos64 = np.concatenate([cos[:, 0], cos[:, 0], cos[:, 1], cos[:, 1]], axis=-1)
    sin64 = np.concatenate([-sin[:, 0], sin[:, 0], -sin[:, 1], sin[:, 1]], axis=-1)
    cos_t = np.concatenate([np.ones((ctx_len, 64), np.float32), cos64], axis=0)
    sin_t = np.concatenate([np.zeros((ctx_len, 64), np.float32), sin64], axis=0)
    return (jnp.asarray(np.tile(cos_t, (1, 2))), jnp.asarray(np.tile(sin_t, (1, 2))))


def kernel(x, c, ctx, c_ctx, ada_w, ada_b, norm1_g, norm2_g, w_in, attn_sink, hgrn_lb,
           hgrn_norm_g, conv_w, w_o, router_w, exp_w_gate, exp_w_up, exp_w_down, final_norm_g):
    B, T, D = x.shape
    L = ctx.shape[1]
    depth = ada_w.shape[0]
    assert D == D_MODEL and L == ROW_TILE and T % ROW_TILE == 0 and B <= 4
    S = L + T
    cap_l = EC_CAPACITY * T // N_EXPERTS
    cap_c = EC_CAPACITY * L // N_EXPERTS
    assert cap_l % 16 == 0 and cap_c % 16 == 0 and (B * cap_c) % 16 == 0

    cos_t, sin_t = _rope_tables(T, L)
    gamma = jax.nn.softmax(hgrn_lb.astype(F32), axis=0)
    lb_all = jnp.cumsum(gamma, axis=0) - gamma[0]
    cvec = jnp.concatenate([c, jnp.zeros((4 - B, D), F32), c_ctx[None],
                            jnp.zeros((3, D), F32)], axis=0)
    mod = _modulation(cvec, ada_w, ada_b)
    xs = (ctx, x)

    def interleave_heads(w, axis):
        shp = w.shape
        w = w.reshape(shp[:axis] + (N_KV, GQA_GROUP, HEAD_DIM) + shp[axis + 1:])
        return jnp.swapaxes(w, axis, axis + 1).reshape(shp)

    for l in range(depth):
        last = l == depth - 1
        w_in_l = w_in[l].astype(BF16)
        w_in_l = jnp.concatenate([w_in_l[:, :1024], interleave_heads(w_in_l[:, 1024:1536], 1),
                                  w_in_l[:, 1536:]], axis=1)
        w_o_l = w_o[l].astype(BF16)
        w_o_l = jnp.concatenate([interleave_heads(w_o_l[:ATT_W], 0), w_o_l[ATT_W:]], axis=0)
        q, kv, hg, hqg, cv = _in_projection(xs, mod[l], norm1_g[l][None], w_in_l, cos_t, sin_t)
        att = _attention(q, kv, attn_sink[l], L, skip_ctx=last)
        o2 = _hgrn(hg, hqg, lb_all[l], L)
        x1, h2, h2t, aff = _mix_out(xs, att, o2, hqg, cv, mod[l], hgrn_norm_g[l][None],
                                    conv_w[l], w_o_l, norm2_g[l][None], router_w[l].T,
                                    skip_ctx=last)
        lm, cc = (0, 0) if last else (L, cap_c)
        segments = ((lm, T, cap_l),) if last else ((0, L, cap_c), (L, T, cap_l))
        posm, cnt = _topk_positions(aff, segments)
        gathered = _gather_rows(cnt, posm, aff, h2, h2t, lm, cap_l, cc)
        xt, gl = gathered[0], gathered[1].reshape(N_EXPERTS, -1, 1)
        xc, gc = (None, None) if last else (gathered[2].reshape(N_EXPERTS, -1, D),
                                            gathered[3].reshape(N_EXPERTS, -1, 1))
        ys = _expert_mlp(xt, gl, xc, gc, exp_w_gate, exp_w_up, exp_w_down, l, EXPERT_ROWS)
        xs = _combine(cnt, x1, jnp.swapaxes(posm, 1, 2), ys[0], None if last else ys[1], mod[l],
                      final_norm_g[None], cap_l, cc, final=last)
    return xs
```

```python
import functools

import numpy as np
import jax
import jax.numpy as jnp
from jax import lax
from jax.experimental import pallas as pl
from jax.experimental.pallas import tpu as pltpu

F32 = jnp.float32
BF16 = jnp.bfloat16
I32 = jnp.int32
HIGHEST = lax.Precision.HIGHEST

D_MODEL = 1024
GRID_W = 64
EPS = 1e-6
LB_FLOOR = 1e-30
N_MOD = 6
ATT_W = 512
HG_W = 256
CV_W = 256
HEAD_DIM = 64
N_Q = 8
N_KV = 2
GQA_GROUP = 4
KV_W = 128
ROPE_BASE = 10000.0
ROPE_FREQS = 16
HG_HEADS = 4
N_EXPERTS = 16
EC_CAPACITY = 2
D_EXPERT = 2048
IN_COLS = 2816

ROW_TILE = 256
ATT_BLOCK = 128
HG_CHUNK = 128
HG_LEVELS = 7
HG_BATCH = 2
HG_LOCAL = 32
HG_LOCAL_MAX_LOG = 80.0
SLOT_BLOCK = 256
PACK_GRANULE = 16
PACK_CHUNK = 512
FF_TILE = 512
EXPERT_ROWS = 512
MOD_TILE = 1536
VMEM_LIMIT = 56 * 1024 * 1024

NEG_BIG = -1e30


def _cparams(sem):
    return pltpu.CompilerParams(dimension_semantics=sem, vmem_limit_bytes=VMEM_LIMIT)


def _dot(a, b):
    return jnp.dot(a, b, preferred_element_type=F32)


def _dot_nt(a, b):
    return lax.dot_general(a, b, (((1,), (1,)), ((), ())), preferred_element_type=F32)


def _mod_kernel(a_ref, w_ref, b_ref, o_ref):
    a = a_ref[...]
    a = a * jax.nn.sigmoid(a)
    o_ref[0] = jnp.dot(a, w_ref[0], precision=HIGHEST, preferred_element_type=F32) + b_ref[0]


def _modulation(cvec, ada_w, ada_b):
    depth = ada_w.shape[0]
    ncol = ada_w.shape[2]
    return pl.pallas_call(
        _mod_kernel,
        grid=(depth, ncol // MOD_TILE),
        in_specs=[
            pl.BlockSpec((8, D_MODEL), lambda l, j: (0, 0)),
            pl.BlockSpec((1, D_MODEL, MOD_TILE), lambda l, j: (l, 0, j)),
            pl.BlockSpec((1, 1, MOD_TILE), lambda l, j: (l, 0, j)),
        ],
        out_specs=pl.BlockSpec((1, 8, MOD_TILE), lambda l, j: (l, 0, j)),
        out_shape=jax.ShapeDtypeStruct((depth, 8, ncol), F32),
        name="modulation",
        compiler_params=_cparams(("arbitrary", "arbitrary")),
    )(cvec, ada_w, ada_b.reshape(depth, 1, ncol))


def _swap_halves(x):
    n = x.shape[-1]
    lane = lax.broadcasted_iota(I32, x.shape, x.ndim - 1)
    up = pltpu.roll(x, n - ROPE_FREQS, x.ndim - 1)
    dn = pltpu.roll(x, ROPE_FREQS, x.ndim - 1)
    return jnp.where((lane % (2 * ROPE_FREQS)) < ROPE_FREQS, up, dn)


def _rmsnorm_mod(x, g, shift, scale):
    var = jnp.mean(x * x, axis=-1, keepdims=True)
    y = x * lax.rsqrt(var + EPS) * g
    return y * (1.0 + scale) + shift


def _row_sources(src):
    if isinstance(src, tuple):
        ctx, lat = src
        return ctx, lat, 1, ctx.shape[1] + lat.shape[1]
    return src, src, 0, src.shape[1]


def _inproj_kernel(xc_ref, x_ref, mod_ref, g_ref, w_ref, cos_ref, sin_ref,
                   q_ref, kv_ref, hg_ref, hqg_ref, cv_ref):
    b = pl.program_id(0)
    i = pl.program_id(1)
    row = jnp.where(i == 0, 4, b)
    shift = mod_ref[pl.ds(row, 1), 0:D_MODEL]
    scale = mod_ref[pl.ds(row, 1), D_MODEL:2 * D_MODEL]
    xin = jnp.where(i == 0, xc_ref[0], x_ref[0])
    h = _rmsnorm_mod(xin, g_ref[...], shift, scale)
    p = _dot(h.astype(BF16), w_ref[...])
    cos2 = cos_ref[...]
    sin2 = sin_ref[...]
    k = p[:, 0:KV_W]
    k = k * cos2 + _swap_halves(k) * sin2
    kv_ref[0, :, 0:KV_W] = k.astype(BF16)
    kv_ref[0, :, KV_W:2 * KV_W] = p[:, KV_W:2 * KV_W].astype(BF16)
    hg_ref[0] = p[:, 256:1024]
    q = p[:, 1024:1536]
    cos8 = jnp.concatenate([cos2] * 4, axis=1)
    sin8 = jnp.concatenate([sin2] * 4, axis=1)
    q = (q * cos8 + _swap_halves(q) * sin8) * (HEAD_DIM ** -0.5)
    q_ref[0] = q.astype(BF16)
    hqg_ref[0] = p[:, 1536:2048]
    cv_ref[0] = p[:, 2048:2816]


def _in_projection(src, mod_l, g, w_bf16, cos_t, sin_t):
    xc, xl, off, S = _row_sources(src)
    B = xc.shape[0]
    nt = S // ROW_TILE
    row_map = lambda b, i: (b, i, 0)
    const2 = lambda b, i: (0, 0)
    return pl.pallas_call(
        _inproj_kernel,
        grid=(B, nt),
        in_specs=[
            pl.BlockSpec((1, ROW_TILE, D_MODEL), lambda b, i: (b, 0, 0)),
            pl.BlockSpec((1, ROW_TILE, D_MODEL), lambda b, i: (b, jnp.maximum(i - off, 0), 0)),
            pl.BlockSpec((8, N_MOD * D_MODEL), const2),
            pl.BlockSpec((1, D_MODEL), const2),
            pl.BlockSpec((D_MODEL, IN_COLS), const2),
            pl.BlockSpec((ROW_TILE, 2 * HEAD_DIM), lambda b, i: (i, 0)),
            pl.BlockSpec((ROW_TILE, 2 * HEAD_DIM), lambda b, i: (i, 0)),
        ],
        out_specs=[
            pl.BlockSpec((1, ROW_TILE, ATT_W), row_map),
            pl.BlockSpec((1, ROW_TILE, 2 * KV_W), row_map),
            pl.BlockSpec((1, ROW_TILE, 3 * HG_W), row_map),
            pl.BlockSpec((1, ROW_TILE, 2 * HG_W), row_map),
            pl.BlockSpec((1, ROW_TILE, 3 * CV_W), row_map),
        ],
        out_shape=[
            jax.ShapeDtypeStruct((B, S, ATT_W), BF16),
            jax.ShapeDtypeStruct((B, S, 2 * KV_W), BF16),
            jax.ShapeDtypeStruct((B, S, 3 * HG_W), F32),
            jax.ShapeDtypeStruct((B, S, 2 * HG_W), F32),
            jax.ShapeDtypeStruct((B, S, 3 * CV_W), F32),
        ],
        name="in_projection",
        compiler_params=_cparams(("arbitrary", "arbitrary")),
    )(xc, xl, mod_l, g, w_bf16, cos_t, sin_t)


def _attn_kernel(sink_ref, q_ref, kp_ref, kc_ref, kn_ref, kx_ref, o_ref, *, blk0, nblk, nctx):
    n = pl.program_id(1) + blk0
    is_lat = n >= nctx
    has_prev = n > nctx
    has_next = n < nblk - 1
    W = ATT_BLOCK
    q = q_ref[0]
    qrows = jnp.concatenate([q[:, g * W:(g + 1) * W] for g in range(GQA_GROUP)], axis=0)
    kv_all = jnp.concatenate([kp_ref[0], kc_ref[0], kn_ref[0], kx_ref[0]], axis=0)
    nkeys = kv_all.shape[0]
    k_all = kv_all[:, 0:KV_W]
    v_ext = jnp.concatenate([kv_all[:, KV_W:2 * KV_W], jnp.ones((nkeys, KV_W), BF16)], axis=1)
    rows = GQA_GROUP * W
    ri = lax.broadcasted_iota(I32, (rows, W), 0) % W
    cj = lax.broadcasted_iota(I32, (rows, W), 1)
    m_prev = (cj >= ri) & has_prev
    m_cur = jnp.broadcast_to(is_lat, (rows, W))
    m_next = (cj <= ri) & (has_next & is_lat)
    grp = lax.broadcasted_iota(I32, (rows, 1), 0) // W
    low = cj < HEAD_DIM
    outs = []
    for h in range(N_KV):
        qh = jnp.where(low if h == 0 else jnp.logical_not(low), qrows, jnp.zeros_like(qrows))
        sink = jnp.zeros((rows, 1), F32)
        for g in range(GQA_GROUP):
            sink = jnp.where(grp == g, sink_ref[h * GQA_GROUP + g], sink)
        s = _dot_nt(qh, k_all)
        segs = [jnp.where(m_prev, s[:, 0:W], NEG_BIG),
                jnp.where(m_cur, s[:, W:2 * W], NEG_BIG),
                jnp.where(m_next, s[:, 2 * W:3 * W], NEG_BIG)]
        segs += [s[:, c:c + W] for c in range(3 * W, nkeys, W)]
        mx = segs[0]
        for sg in segs[1:]:
            mx = jnp.maximum(mx, sg)
        m = jnp.maximum(jnp.max(mx, axis=1, keepdims=True), sink)
        p = jnp.concatenate([jnp.exp(sg - m).astype(BF16) for sg in segs], axis=1)
        oe = _dot(p, v_ext)
        den = oe[:, KV_W:2 * KV_W] + jnp.exp(sink - m)
        outs.append(oe[:, 0:KV_W] / den)
    o = jnp.where(low, outs[0], outs[1])
    for g in range(GQA_GROUP):
        o_ref[0, :, g * W:(g + 1) * W] = o[g * W:(g + 1) * W].astype(BF16)


def _attention(q, kv, sink, ctx_len, skip_ctx):
    B, S, _ = q.shape
    nblk = S // ATT_BLOCK
    nctx = ctx_len // ATT_BLOCK
    blk0 = nctx if skip_ctx else 0
    blk = lambda f: (lambda b, n: (b, f(n + blk0), 0))
    kern = functools.partial(_attn_kernel, blk0=blk0, nblk=nblk, nctx=nctx)
    return pl.pallas_call(
        kern,
        grid=(B, nblk - blk0),
        in_specs=[
            pl.BlockSpec(memory_space=pltpu.SMEM),
            pl.BlockSpec((1, ATT_BLOCK, ATT_W), blk(lambda n: n)),
            pl.BlockSpec((1, ATT_BLOCK, 2 * KV_W), blk(lambda n: jnp.maximum(n - 1, 0))),
            pl.BlockSpec((1, ATT_BLOCK, 2 * KV_W), blk(lambda n: n)),
            pl.BlockSpec((1, ATT_BLOCK, 2 * KV_W), blk(lambda n: jnp.minimum(n + 1, nblk - 1))),
            pl.BlockSpec((1, ctx_len, 2 * KV_W), lambda b, n: (b, 0, 0)),
        ],
        out_specs=pl.BlockSpec((1, ATT_BLOCK, ATT_W), lambda b, n: (b, n, 0)),
        out_shape=jax.ShapeDtypeStruct((B, S - blk0 * ATT_BLOCK, ATT_W), BF16),
        name="attention",
        compiler_params=_cparams(("arbitrary", "arbitrary")),
    )(sink, q, kv, kv, kv, kv)


def _hgrn_constants():
    C = HG_CHUNK
    t = np.arange(C)[:, None]
    r = np.arange(C)[None, :]
    tri = np.stack([r <= t, r >= t]).astype(np.float32)
    x = t ^ r
    lvl = np.where(x > 0, np.floor(np.log2(np.maximum(x, 1))).astype(np.int32), HG_LEVELS)
    lvl_f = np.where(t >= r, lvl, -1).astype(np.int32)
    lvl_b = np.where(t <= r, lvl, -1).astype(np.int32)
    lvl2 = np.stack([np.tile(lvl_f, (1, HG_HEADS)), np.tile(lvl_b, (1, HG_HEADS))])
    return tri, lvl2


def _span_row(x, span, row):
    C = x.shape[0]
    if span >= 8:
        x3 = x.reshape(C // span, span, x.shape[1])
        return jnp.broadcast_to(x3[:, row:row + 1, :], x3.shape).reshape(x.shape)
    pos = lax.broadcasted_iota(I32, x.shape, 0) % span
    out = x
    for p in range(span):
        if p != row:
            out = jnp.where(pos == p, pltpu.roll(x, (p - row) % C, 0), out)
    return out


def _hgrn_prepare(v, z, qr, lb, tri, backward):
    C = HG_CHUNK
    logf = jnp.log(jnp.maximum(lb, LB_FLOOR) + (1.0 - lb) * jax.nn.sigmoid(z))
    k = (1.0 - lb) * jax.nn.sigmoid(-z)
    q = qr * jax.nn.sigmoid(qr)
    hi = logf.astype(BF16)
    lo = (logf - hi.astype(F32)).astype(BF16)
    cs = _dot(tri, jnp.concatenate([hi, lo], axis=1))
    lam = cs[:, 0:HG_W] + cs[:, HG_W:2 * HG_W]
    tot = lam[0:1] if backward else lam[C - 1:C]
    local = lam - _span_row(lam, HG_LOCAL, HG_LOCAL // 2)
    return dict(v=v, q=q, k=k, lam=lam, tot=tot, local=local)


def _hgrn_finish(g, head_masks, same_head, lvl, st_ref, sidx, backward, fast):
    C = HG_CHUNK
    q, k, v, lam, tot = g["q"], g["k"], g["v"], g["lam"], g["tot"]
    zero = jnp.zeros((C, HG_W), BF16)

    def per_head_rows(x):
        return jnp.concatenate([jnp.where(hm, x, zero) for hm in head_masks], axis=0)

    qb = q.astype(BF16)
    kb = k.astype(BF16)
    if fast:
        n_local = HG_LOCAL.bit_length() - 1
        a = jnp.where((lvl == HG_LEVELS) | ((lvl >= 0) & (lvl < n_local)),
                      _dot_nt((q * jnp.exp(g["local"])).astype(BF16),
                              per_head_rows((k * jnp.exp(-g["local"])).astype(BF16))), 0.0)
        levels = range(n_local, HG_LEVELS)
    else:
        a = jnp.where(lvl == HG_LEVELS, _dot_nt(qb, per_head_rows(kb)), 0.0)
        levels = range(HG_LEVELS)
    for l in levels:
        m = 1 << l
        ref = _span_row(lam, 2 * m, m if backward else m - 1)
        fac = jnp.exp(-jnp.abs(lam - ref)).astype(BF16)
        a = jnp.where(lvl == l, _dot_nt(qb * fac, per_head_rows(kb * fac)), a)
    st = st_ref[sidx]
    q_in = (q * jnp.exp(lam)).astype(BF16)
    o = _dot(a.astype(BF16), per_head_rows(v.astype(BF16))) + _dot_nt(q_in, st.astype(BF16))
    k_out = (k * jnp.exp(tot - lam)).astype(BF16)
    upd = _dot(v.T.astype(BF16), k_out)
    st_ref[sidx] = st * jnp.exp(tot) + jnp.where(same_head, upd, 0.0)
    return o


def _hgrn_kernel(vf_ref, zf_ref, qf_ref, vb_ref, zb_ref, qb_ref, lb_ref, tri_ref, lvl_ref,
                 of_ref, ob_ref, st_ref):
    j = pl.program_id(1)

    @pl.when(j == 0)
    def _():
        st_ref[...] = jnp.zeros_like(st_ref)

    head_id = (lax.broadcasted_iota(I32, (HG_CHUNK, HG_W), 1) // HEAD_DIM).astype(F32).astype(BF16)
    head_masks = [head_id == float(h) for h in range(HG_HEADS)]
    same_head = (lax.broadcasted_iota(I32, (HG_W, HG_W), 0) // HEAD_DIM
                 == lax.broadcasted_iota(I32, (HG_W, HG_W), 1) // HEAD_DIM)
    streams = []
    for d, (v_ref, z_ref, q_ref, o_ref) in enumerate(
            ((vf_ref, zf_ref, qf_ref, of_ref), (vb_ref, zb_ref, qb_ref, ob_ref))):
        for s in range(v_ref.shape[0]):
            g = _hgrn_prepare(v_ref[s], z_ref[s], q_ref[s], lb_ref[d], tri_ref[d], d == 1)
            streams.append((d, s, o_ref, g))
    worst = jnp.abs(streams[0][3]["local"])
    for _, _, _, g in streams[1:]:
        worst = jnp.maximum(worst, jnp.abs(g["local"]))
    can_use_local = jnp.max(worst) <= HG_LOCAL_MAX_LOG

    def run(fast):
        for d, s, o_ref, g in streams:
            o_ref[s] = _hgrn_finish(g, head_masks, same_head, lvl_ref[d], st_ref, 2 * s + d,
                                    backward=(d == 1), fast=fast)

    pl.when(can_use_local)(functools.partial(run, True))
    pl.when(jnp.logical_not(can_use_local))(functools.partial(run, False))


def _hgrn(hg, hqg, lb_l, ctx_len):
    B, S, _ = hg.shape
    nc = S // HG_CHUNK
    nctx = ctx_len // HG_CHUNK
    tri_np, lvl_np = _hgrn_constants()
    tri = jnp.asarray(tri_np, BF16)
    lvl = jnp.asarray(lvl_np, I32)

    def back(j):
        return jnp.where(j < nctx, nctx - 1 - j, nc - 1 - (j - nctx))

    bs = HG_BATCH if B % HG_BATCH == 0 else 1
    blk = (bs, HG_CHUNK, HG_W)
    fwd = lambda col: (lambda b, j: (b, j, col))
    bwd = lambda col: (lambda b, j: (b, back(j), col))
    const3 = lambda b, j: (0, 0, 0)
    return pl.pallas_call(
        _hgrn_kernel,
        grid=(B // bs, nc),
        in_specs=[
            pl.BlockSpec(blk, fwd(0)), pl.BlockSpec(blk, fwd(1)), pl.BlockSpec(blk, fwd(0)),
            pl.BlockSpec(blk, bwd(0)), pl.BlockSpec(blk, bwd(2)), pl.BlockSpec(blk, bwd(0)),
            pl.BlockSpec((2, 1, HG_W), const3),
            pl.BlockSpec((2, HG_CHUNK, HG_CHUNK), const3),
            pl.BlockSpec((2, HG_CHUNK, HG_HEADS * HG_CHUNK), const3),
        ],
        out_specs=[pl.BlockSpec(blk, fwd(0)), pl.BlockSpec(blk, bwd(0))],
        out_shape=[jax.ShapeDtypeStruct((B, S, HG_W), F32)] * 2,
        scratch_shapes=[pltpu.VMEM((2 * bs, HG_W, HG_W), F32)],
        name="hgrn_scan",
        compiler_params=_cparams(("arbitrary", "arbitrary")),
    )(hg, hg, hqg, hg, hg, hqg, lb_l.reshape(2, 1, HG_W), tri, lvl)


def _mixout_kernel(xc_ref, x_ref, att_ref, of_ref, ob_ref, g_ref, cv_ref, cprev_ref, cnext_ref,
                   mod_ref, gain_ref, cw_ref, wo_ref, n2_ref, wr_ref, ones_ref,
                   x1_ref, h2_ref, aff_ref, *, tile0, ntile):
    b = pl.program_id(0)
    i = pl.program_id(1) + tile0
    row = jnp.where(i == 0, 4, b)
    R = ROW_TILE

    def modv(c):
        return mod_ref[pl.ds(row, 1), c * D_MODEL:(c + 1) * D_MODEL]

    o = of_ref[0] + ob_ref[0]
    sq = o * o
    sq_hi = sq.astype(BF16)
    sq_lo = (sq - sq_hi.astype(F32)).astype(BF16)
    ms = _dot(sq_hi, ones_ref[...]) + _dot(sq_lo, ones_ref[...])
    g = g_ref[0]
    hg = o * lax.rsqrt(ms * (1.0 / HEAD_DIM) + EPS) * gain_ref[...] * (g * jax.nn.sigmoid(g))
    cv = cv_ref[0]
    u = cv[:, CV_W:2 * CV_W] * cv[:, 2 * CV_W:3 * CV_W]
    up = cprev_ref[0]
    un = cnext_ref[0]
    u_prev_row = up[7:8, CV_W:2 * CV_W] * up[7:8, 2 * CV_W:3 * CV_W]
    u_next_row = un[0:1, CV_W:2 * CV_W] * un[0:1, 2 * CV_W:3 * CV_W]
    u_prev_row = jnp.where(i <= 1, 0.0, u_prev_row)
    u_next_row = jnp.where((i == 0) | (i == ntile - 1), 0.0, u_next_row)
    ridx = lax.broadcasted_iota(I32, (R, CV_W), 0)
    u_m1 = jnp.where(ridx == 0, u_prev_row, pltpu.roll(u, 1, 0))
    u_p1 = jnp.where(ridx == R - 1, u_next_row, pltpu.roll(u, R - 1, 0))
    cw = cw_ref[...]
    conv = cv[:, 0:CV_W] * (u_m1 * cw[0:1] + u * cw[1:2] + u_p1 * cw[2:3])
    mix = (_dot(att_ref[0], wo_ref[0:ATT_W])
           + _dot(hg.astype(BF16), wo_ref[ATT_W:ATT_W + HG_W])
           + _dot(conv.astype(BF16), wo_ref[ATT_W + HG_W:D_MODEL]))
    x1 = jnp.where(i == 0, xc_ref[0], x_ref[0]) + modv(2) * mix
    x1_ref[0] = x1
    h2 = _rmsnorm_mod(x1, n2_ref[...], modv(3), modv(4))
    h_hi = h2.astype(BF16)
    h2_ref[0] = h_hi
    h_lo = (h2 - h_hi.astype(F32)).astype(BF16)
    wr = wr_ref[...]
    w_hi = wr.astype(BF16)
    w_lo = (wr - w_hi.astype(F32)).astype(BF16)
    r1 = _dot_nt(jnp.concatenate([w_hi, w_lo], axis=0), h_hi)
    logits = r1[0:N_EXPERTS] + r1[N_EXPERTS:2 * N_EXPERTS] + _dot_nt(w_hi, h_lo)
    e = jnp.exp(logits - jnp.max(logits, axis=0, keepdims=True))
    aff_ref[0] = e / jnp.sum(e, axis=0, keepdims=True)


def _mix_out(src, att, o2, hqg, cv, mod_l, gain, conv_w, wo_bf16, n2g, wr_t, skip_ctx):
    xc, xl, off, S = _row_sources(src)
    B = xc.shape[0]
    ntile = S // ROW_TILE
    tile0 = 1 if skip_ctx else 0
    sub = ROW_TILE // 8
    nsub = S // 8
    rmap = lambda b, i: (b, i + tile0, 0)
    omap = lambda b, i: (b, i, 0)
    s_out = S - tile0 * ROW_TILE
    const2 = lambda b, i: (0, 0)
    ones = jnp.asarray(np.kron(np.eye(HG_HEADS), np.ones((HEAD_DIM, HEAD_DIM))), BF16)
    kern = functools.partial(_mixout_kernel, tile0=tile0, ntile=ntile)
    return pl.pallas_call(
        kern,
        grid=(B, ntile - tile0),
        in_specs=[
            pl.BlockSpec((1, ROW_TILE, D_MODEL), lambda b, i: (b, 0, 0)),
            pl.BlockSpec((1, ROW_TILE, D_MODEL),
                         lambda b, i: (b, jnp.maximum(i + tile0 - off, 0), 0)),
            pl.BlockSpec((1, ROW_TILE, ATT_W), omap),
            pl.BlockSpec((1, ROW_TILE, HG_W), rmap),
            pl.BlockSpec((1, ROW_TILE, HG_W), rmap),
            pl.BlockSpec((1, ROW_TILE, HG_W), lambda b, i: (b, i + tile0, 1)),
            pl.BlockSpec((1, ROW_TILE, 3 * CV_W), rmap),
            pl.BlockSpec((1, 8, 3 * CV_W),
                         lambda b, i: (b, jnp.maximum((i + tile0) * sub - 1, 0), 0)),
            pl.BlockSpec((1, 8, 3 * CV_W),
                         lambda b, i: (b, jnp.minimum((i + tile0 + 1) * sub, nsub - 1), 0)),
            pl.BlockSpec((8, N_MOD * D_MODEL), const2),
            pl.BlockSpec((1, HG_W), const2),
            pl.BlockSpec((3, CV_W), const2),
            pl.BlockSpec((D_MODEL, D_MODEL), const2),
            pl.BlockSpec((1, D_MODEL), const2),
            pl.BlockSpec((N_EXPERTS, D_MODEL), const2),
            pl.BlockSpec((HG_W, HG_W), const2),
        ],
        out_specs=[
            pl.BlockSpec((1, ROW_TILE, D_MODEL), omap),
            pl.BlockSpec((1, ROW_TILE, D_MODEL), omap),
            pl.BlockSpec((1, N_EXPERTS, ROW_TILE), lambda b, i: (b, 0, i)),
        ],
        out_shape=[
            jax.ShapeDtypeStruct((B, s_out, D_MODEL), F32),
            jax.ShapeDtypeStruct((B, s_out, D_MODEL), BF16),
            jax.ShapeDtypeStruct((B, N_EXPERTS, s_out), F32),
        ],
        name="mix_out",
        compiler_params=_cparams(("arbitrary", "arbitrary")),
    )(xc, xl, att, o2[0], o2[1], hqg, cv, cv, cv, mod_l, gain, conv_w, wo_bf16, n2g, wr_t, ones)


def _topk_kernel(aff_ref, tri_ref, pos_ref, cnt_ref, *, segments):
    tri = tri_ref[...]

    def excl_cumsum(mask, n):
        carry = jnp.zeros((N_EXPERTS, 1), F32)
        parts = []
        carries = []
        for c in range(n // 128):
            carries.append(carry)
            blk = jnp.where(mask[:, c * 128:(c + 1) * 128], 1.0, 0.0)
            parts.append(_dot(blk.astype(BF16), tri) + carry)
            carry = carry + jnp.sum(blk, axis=1, keepdims=True)
        return jnp.concatenate(parts, axis=1), carries + [carry]

    for (lo, n, k) in segments:
        a = aff_ref[0, :, lo:lo + n]

        def body(it, thr):
            cand = thr | jnp.left_shift(jnp.int32(1), 30 - it)
            cnt = jnp.sum(jnp.where(a >= pltpu.bitcast(cand, F32), 1.0, 0.0), axis=1, keepdims=True)
            return jnp.where(cnt >= k, cand, thr)

        thr = lax.fori_loop(0, 31, body, jnp.zeros((N_EXPERTS, 1), I32))
        above = a >= pltpu.bitcast(thr + 1, F32)
        tied = jnp.logical_and(a >= pltpu.bitcast(thr, F32), jnp.logical_not(above))
        n_above = jnp.sum(jnp.where(above, 1.0, 0.0), axis=1, keepdims=True)
        rank_tied, _ = excl_cumsum(tied, n)
        sel = above | (tied & (rank_tied < (k - n_above)))
        pos, counts = excl_cumsum(sel, n)
        pos_ref[0, :, lo:lo + n] = jnp.where(sel, pos.astype(I32), -1)
    step = ROW_TILE // 128
    lane = lax.broadcasted_iota(I32, (N_EXPERTS, 128), 1)
    cnt = jnp.zeros((N_EXPERTS, 128), F32)
    for c, col in enumerate(counts[0::step]):
        cnt = jnp.where(lane == c, col, cnt)
    cnt_ref[0] = cnt.astype(I32)


def _topk_positions(aff, segments):
    B, E, S = aff.shape
    assert segments[-1][1] // ROW_TILE + 1 <= 128
    tri = jnp.asarray(np.triu(np.ones((128, 128)), 1), BF16)
    kern = functools.partial(_topk_kernel, segments=segments)
    return pl.pallas_call(
        kern,
        grid=(B,),
        in_specs=[pl.BlockSpec((1, E, S), lambda b: (b, 0, 0)),
                  pl.BlockSpec((128, 128), lambda b: (0, 0))],
        out_specs=[pl.BlockSpec((1, E, S), lambda b: (b, 0, 0)),
                   pl.BlockSpec((1, E, 128), lambda b: (b, 0, 0))],
        out_shape=[jax.ShapeDtypeStruct((B, E, S), I32),
                   jax.ShapeDtypeStruct((B, E, 128), I32)],
        name="topk_positions",
        compiler_params=_cparams(("arbitrary",)),
    )(aff, tri)


def _slot_block(cap):
    sbs = min(SLOT_BLOCK, cap)
    assert sbs >= ROW_TILE or sbs == cap
    return sbs


def _div_pow2(x, d):
    assert d & (d - 1) == 0
    return lax.shift_right_logical(x, jnp.int32(d.bit_length() - 1))


def _gather_kernel(cnt_ref, pos_ref, aff_ref, h_ref, *refs, ctx_len, cap_l, cap_c):
    if cap_c:
        xl_ref, gl_ref, il_ref, xc_ref, gc_ref, acc_ref, gacc_ref, iacc_ref = refs
    else:
        xl_ref, gl_ref, il_ref, acc_ref, gacc_ref, iacc_ref = refs
    b = pl.program_id(0)
    e = pl.program_id(1)
    S = h_ref.shape[1]
    pos = pos_ref[0, pl.ds(e, 1), :]
    aff = aff_ref[0, pl.ds(e, 1), :]
    nchunk = (S - ctx_len) // ROW_TILE
    sbs = _slot_block(cap_l)
    acc_ref[...] = jnp.zeros_like(acc_ref)
    gacc_ref[...] = jnp.zeros_like(gacc_ref)
    iacc_ref[...] = jnp.zeros_like(iacc_ref)
    nsb = cap_l // sbs
    first = [jnp.minimum(_div_pow2(cnt_ref[b, e, c], sbs), nsb - 1) for c in range(nchunk)]
    slot_i32 = lax.broadcasted_iota(I32, (sbs, ROW_TILE), 0)
    slot_in_block = slot_i32.astype(F32).astype(BF16)
    one = jnp.ones((sbs, ROW_TILE), BF16)
    zero = jnp.zeros((sbs, ROW_TILE), BF16)

    def visit(c, sb):
        lo = ctx_len + c * ROW_TILE
        base = pl.multiple_of(sb * sbs, sbs)
        rel_i = pos[:, lo:lo + ROW_TILE] - base
        rel = rel_i.astype(F32)
        rel = jnp.where((rel >= 0.0) & (rel < float(sbs)), rel, -1.0).astype(BF16)
        acc_ref[pl.ds(base, sbs), :] += _dot(jnp.where(rel == slot_in_block, one, zero),
                                             h_ref[0, lo:lo + ROW_TILE, :])
        hit = rel_i == slot_i32
        gacc_ref[pl.ds(base, sbs), :] += jnp.sum(
            jnp.where(hit, aff[:, lo:lo + ROW_TILE], 0.0), axis=1, keepdims=True)
        tok = (lax.broadcasted_iota(I32, (1, ROW_TILE), 1) + c * ROW_TILE).astype(F32)
        iacc_ref[pl.ds(base, sbs), :] += jnp.sum(jnp.where(hit, tok, 0.0), axis=1, keepdims=True)

    for c in range(nchunk):
        visit(c, first[c])
    for c in range(nchunk):
        pl.when(cnt_ref[b, e, c + 1] > (first[c] + 1) * sbs)(
            functools.partial(visit, c, first[c] + 1))
    xl_ref[0, 0] = acc_ref[...].astype(BF16)
    gl_ref[0, 0] = gacc_ref[...]
    il_ref[0, 0] = iacc_ref[...].astype(I32)
    if cap_c:
        slot_c = lax.broadcasted_iota(I32, (cap_c, ctx_len), 0)
        hit = pos[:, 0:ctx_len] == slot_c
        xc_ref[0, 0] = _dot(jnp.where(hit, 1.0, 0.0).astype(BF16),
                            h_ref[0, 0:ctx_len, :]).astype(BF16)
        gc_ref[0, 0] = jnp.sum(jnp.where(hit, aff[:, 0:ctx_len], 0.0), axis=1, keepdims=True)


def _gather_rows(cnt, posm, aff, h2, ctx_len, cap_l, cap_c):
    B, E, S = posm.shape
    kern = functools.partial(_gather_kernel, ctx_len=ctx_len, cap_l=cap_l, cap_c=cap_c)
    omap = lambda b, e, cnt_ref: (e, b, 0, 0)
    caps = [cap_l] + ([cap_c] if cap_c else [])
    out_specs = []
    out_shape = []
    for cap in caps:
        out_specs += [pl.BlockSpec((1, 1, cap, D_MODEL), omap), pl.BlockSpec((1, 1, cap, 1), omap)]
        out_shape += [jax.ShapeDtypeStruct((E, B, cap, D_MODEL), BF16),
                      jax.ShapeDtypeStruct((E, B, cap, 1), F32)]
        if cap is caps[0]:
            out_specs.append(pl.BlockSpec((1, 1, cap, 1), omap))
            out_shape.append(jax.ShapeDtypeStruct((E, B, cap, 1), I32))
    grid_spec = pltpu.PrefetchScalarGridSpec(
        num_scalar_prefetch=1,
        grid=(B, E),
        in_specs=[pl.BlockSpec((1, E, S), lambda b, e, cnt_ref: (b, 0, 0)),
                  pl.BlockSpec((1, E, S), lambda b, e, cnt_ref: (b, 0, 0)),
                  pl.BlockSpec((1, S, D_MODEL), lambda b, e, cnt_ref: (b, 0, 0))],
        out_specs=out_specs,
        scratch_shapes=[pltpu.VMEM((cap_l, D_MODEL), F32), pltpu.VMEM((cap_l, 1), F32),
                        pltpu.VMEM((cap_l, 1), F32)],
    )
    return pl.pallas_call(
        kern,
        grid_spec=grid_spec,
        out_shape=out_shape,
        name="gather_rows",
        compiler_params=_cparams(("arbitrary", "arbitrary")),
    )(cnt, posm, aff, h2)


def _expert_kernel(*refs, row_chunk, n_sets):
    x_refs = refs[:n_sets]
    g_refs = refs[n_sets:2 * n_sets]
    wg_ref, wu_ref, wd_ref = refs[2 * n_sets:2 * n_sets + 3]
    y_refs = refs[2 * n_sets + 3:3 * n_sets + 3]
    acc_refs = refs[3 * n_sets + 3:]
    f = pl.program_id(1)
    nf = pl.num_programs(1)
    wg = wg_ref[0, 0].astype(BF16)
    wu = wu_ref[0, 0].astype(BF16)
    wd = wd_ref[0, 0].astype(BF16)

    @pl.when(f == 0)
    def _():
        for acc_ref in acc_refs:
            acc_ref[...] = jnp.zeros_like(acc_ref)

    for x_ref, acc_ref in zip(x_refs, acc_refs):
        rows = x_ref.shape[1]
        step = min(row_chunk, rows)
        for r in range(rows // step):
            rs = slice(r * step, (r + 1) * step)
            x = x_ref[0, rs, :]
            g = _dot(x, wg)
            u = _dot(x, wu)
            hid = (g * jax.nn.sigmoid(g) * u).astype(BF16)
            acc_ref[rs, :] += _dot(hid, wd)

    @pl.when(f == nf - 1)
    def _():
        for g_ref, y_ref, acc_ref in zip(g_refs, y_refs, acc_refs):
            y_ref[0] = (acc_ref[...] * g_ref[0]).astype(BF16)


def _expert_mlp(x_sets, g_sets, w_gate, w_up, w_down, layer, row_chunk):
    E = N_EXPERTS
    nf = D_EXPERT // FF_TILE
    n_sets = len(x_sets)
    kern = functools.partial(_expert_kernel, row_chunk=row_chunk, n_sets=n_sets)
    xspec = lambda r: pl.BlockSpec((1, r, D_MODEL), lambda e, f: (e, 0, 0))
    gspec = lambda r: pl.BlockSpec((1, r, 1), lambda e, f: (e, 0, 0))
    return pl.pallas_call(
        kern,
        grid=(E, nf),
        in_specs=[xspec(xs.shape[1]) for xs in x_sets] + [gspec(xs.shape[1]) for xs in x_sets] + [
            pl.BlockSpec((1, 1, D_MODEL, FF_TILE), lambda e, f: (layer, e, 0, f)),
            pl.BlockSpec((1, 1, D_MODEL, FF_TILE), lambda e, f: (layer, e, 0, f)),
            pl.BlockSpec((1, 1, FF_TILE, D_MODEL), lambda e, f: (layer, e, f, 0))],
        out_specs=[xspec(xs.shape[1]) for xs in x_sets],
        out_shape=[jax.ShapeDtypeStruct(xs.shape, BF16) for xs in x_sets],
        scratch_shapes=[pltpu.VMEM(xs.shape[1:], F32) for xs in x_sets],
        name="expert_mlp",
        compiler_params=_cparams(("arbitrary", "arbitrary")),
    )(*x_sets, *g_sets, w_gate, w_up, w_down)


def _combine_kernel(cnt_ref, x_ref, pos_ref, *refs, cap_l, cap_c, final):
    if cap_c:
        yl_ref, il_ref, yc_ref, mod_ref, fg_ref, o_ref, acc_ref, pk_ref, tk_ref = refs
    else:
        yl_ref, il_ref, mod_ref, fg_ref, o_ref, acc_ref, pk_ref, tk_ref = refs
    b = pl.program_id(0)
    i = pl.program_id(1)
    row = jnp.where(i == 0, 4, b) if cap_c else b
    gate = mod_ref[pl.ds(row, 1), 5 * D_MODEL:6 * D_MODEL]
    pos = pos_ref[0]
    acc_ref[...] = jnp.zeros_like(acc_ref)
    GR = PACK_GRANULE
    KC = PACK_CHUNK

    @pl.when((b == 0) & (i == 0))
    def _():
        pk_ref[...] = jnp.zeros_like(pk_ref)
        tk_ref[...] = jnp.full(tk_ref.shape, -1, I32)

    def add_latent():
        t = i - 1 if cap_c else i
        total = 0
        for e in range(N_EXPERTS):
            g0 = _div_pow2(cnt_ref[b, e, t], GR)
            n = _div_pow2(cnt_ref[b, e, t + 1] + (GR - 1), GR) - g0

            def copy_granule(k, carry, e=e, src0=g0 * GR, dst0=total):
                s = pl.multiple_of(src0 + k * GR, GR)
                d = pl.multiple_of(dst0 + k * GR, GR)
                pk_ref[pl.ds(d, GR), :] = yl_ref[e, 0, pl.ds(s, GR), :]
                tk_ref[pl.ds(d, GR), :] = il_ref[e, 0, pl.ds(s, GR), :]
                return carry

            lax.fori_loop(0, n, copy_granule, 0)
            total = total + n * GR
        tk_ref[pl.ds(pl.multiple_of(total, GR), KC), :] = jnp.full((KC, 1), -1, I32)
        lane = lax.broadcasted_iota(I32, (KC, ROW_TILE), 1) + t * ROW_TILE

        def chunk_product(kc, carry):
            r0 = pl.multiple_of(kc * KC, KC)
            hit = tk_ref[pl.ds(r0, KC), :] == lane
            onehot = jnp.where(hit, 1.0, 0.0).T.astype(BF16)
            acc_ref[...] += _dot(onehot, pk_ref[pl.ds(r0, KC), :])
            return carry

        lax.fori_loop(0, _div_pow2(total + (KC - 1), KC), chunk_product, 0)

    def add_context():
        ncol = yc_ref.shape[1]
        lane = lax.broadcasted_iota(I32, (ROW_TILE, ncol), 1)
        for e in range(N_EXPERTS):
            pe = pos[:, e:e + 1]
            tgt = jnp.where(pe >= 0, pe + b * cap_c, -1)
            acc_ref[...] += _dot(jnp.where(tgt == lane, 1.0, 0.0).astype(BF16), yc_ref[e])

    if cap_c:
        pl.when(i == 0)(add_context)
        pl.when(i > 0)(add_latent)
    else:
        add_latent()
    x2 = x_ref[0] + gate * acc_ref[...]
    if final:
        var = jnp.mean(x2 * x2, axis=-1, keepdims=True)
        x2 = x2 * lax.rsqrt(var + EPS) * fg_ref[...]
    o_ref[0] = x2


def _combine(cnt, x1, pos_t, yl, il, yc, mod_l, final_g, cap_l, cap_c, final):
    B, S, _ = x1.shape
    E = N_EXPERTS
    rmap = lambda b, i, cnt_ref: (b, i, 0)
    const2 = lambda b, i, cnt_ref: (0, 0)
    kern = functools.partial(_combine_kernel, cap_l=cap_l, cap_c=cap_c, final=final)
    per_sample = lambda b, i, cnt_ref: (0, b, 0, 0)
    y_specs = [pl.BlockSpec((E, 1, cap_l, D_MODEL), per_sample, pipeline_mode=pl.Buffered(1)),
               pl.BlockSpec((E, 1, cap_l, 1), per_sample, pipeline_mode=pl.Buffered(1))]
    y_args = [yl, il]
    pack_rows = E * (ROW_TILE + 2 * PACK_GRANULE) + PACK_CHUNK
    if cap_c:
        y_specs.append(pl.BlockSpec(yc.shape, lambda b, i, cnt_ref: (0, 0, 0),
                                    pipeline_mode=pl.Buffered(1)))
        y_args.append(yc)
    grid_spec = pltpu.PrefetchScalarGridSpec(
        num_scalar_prefetch=1,
        grid=(B, S // ROW_TILE),
        in_specs=[pl.BlockSpec((1, ROW_TILE, D_MODEL), rmap),
                  pl.BlockSpec((1, ROW_TILE, E), rmap)] + y_specs + [
                  pl.BlockSpec((8, N_MOD * D_MODEL), const2),
                  pl.BlockSpec((1, D_MODEL), const2)],
        out_specs=pl.BlockSpec((1, ROW_TILE, D_MODEL), rmap),
        scratch_shapes=[pltpu.VMEM((ROW_TILE, D_MODEL), F32),
                        pltpu.VMEM((pack_rows, D_MODEL), BF16),
                        pltpu.VMEM((pack_rows, 1), I32)],
    )
    return pl.pallas_call(
        kern,
        grid_spec=grid_spec,
        out_shape=jax.ShapeDtypeStruct((B, S, D_MODEL), F32),
        name="combine",
        compiler_params=_cparams(("arbitrary", "arbitrary")),
    )(cnt, x1, pos_t, *y_args, mod_l, final_g)


def _rope_tables(n_lat, ctx_len):
    t = np.arange(n_lat)
    pos = np.stack([t // GRID_W, t % GRID_W], axis=-1).astype(np.float32)
    inv = (ROPE_BASE ** (-np.arange(ROPE_FREQS, dtype=np.float32) / ROPE_FREQS)).astype(np.float32)
    ang = pos[:, :, None] * inv
    cos = np.cos(ang).astype(np.float32)
    sin = np.sin(ang).astype(np.float32)
    cos64 = np.concatenate([cos[:, 0], cos[:, 0], cos[:, 1], cos[:, 1]], axis=-1)
    sin64 = np.concatenate([-sin[:, 0], sin[:, 0], -sin[:, 1], sin[:, 1]], axis=-1)
    cos_t = np.concatenate([np.ones((ctx_len, 64), np.float32), cos64], axis=0)
    sin_t = np.concatenate([np.zeros((ctx_len, 64), np.float32), sin64], axis=0)
    return (jnp.asarray(np.tile(cos_t, (1, 2))), jnp.asarray(np.tile(sin_t, (1, 2))))


def kernel(x, c, ctx, c_ctx, ada_w, ada_b, norm1_g, norm2_g, w_in, attn_sink, hgrn_lb,
           hgrn_norm_g, conv_w, w_o, router_w, exp_w_gate, exp_w_up, exp_w_down, final_norm_g):
    B, T, D = x.shape
    L = ctx.shape[1]
    depth = ada_w.shape[0]
    assert D == D_MODEL and L == ROW_TILE and T % ROW_TILE == 0 and B <= 4
    S = L + T
    cap_l = EC_CAPACITY * T // N_EXPERTS
    cap_c = EC_CAPACITY * L // N_EXPERTS
    assert cap_l % 16 == 0 and cap_c % 16 == 0 and (B * cap_c) % 16 == 0

    cos_t, sin_t = _rope_tables(T, L)
    gamma = jax.nn.softmax(hgrn_lb.astype(F32), axis=0)
    lb_all = jnp.cumsum(gamma, axis=0) - gamma[0]
    cvec = jnp.concatenate([c, jnp.zeros((4 - B, D), F32), c_ctx[None],
                            jnp.zeros((3, D), F32)], axis=0)
    mod = _modulation(cvec, ada_w, ada_b)
    xs = (ctx, x)

    def interleave_heads(w, axis):
        shp = w.shape
        w = w.reshape(shp[:axis] + (N_KV, GQA_GROUP, HEAD_DIM) + shp[axis + 1:])
        return jnp.swapaxes(w, axis, axis + 1).reshape(shp)

    for l in range(depth):
        last = l == depth - 1
        w_in_l = w_in[l].astype(BF16)
        w_in_l = jnp.concatenate([w_in_l[:, :1024], interleave_heads(w_in_l[:, 1024:1536], 1),
                                  w_in_l[:, 1536:]], axis=1)
        w_o_l = w_o[l].astype(BF16)
        w_o_l = jnp.concatenate([interleave_heads(w_o_l[:ATT_W], 0), w_o_l[ATT_W:]], axis=0)
        q, kv, hg, hqg, cv = _in_projection(xs, mod[l], norm1_g[l][None], w_in_l, cos_t, sin_t)
        att = _attention(q, kv, attn_sink[l], L, skip_ctx=last)
        o2 = _hgrn(hg, hqg, lb_all[l], L)
        x1, h2, aff = _mix_out(xs, att, o2, hqg, cv, mod[l], hgrn_norm_g[l][None], conv_w[l],
                               w_o_l, norm2_g[l][None], router_w[l].T, skip_ctx=last)
        lm, cc = (0, 0) if last else (L, cap_c)
        segments = ((lm, T, cap_l),) if last else ((0, L, cap_c), (L, T, cap_l))
        posm, cnt = _topk_positions(aff, segments)
        gathered = _gather_rows(cnt, posm, aff, h2, lm, cap_l, cc)
        il = gathered[2]
        x_sets = [a.reshape(N_EXPERTS, -1, D) for a in gathered[0::3]]
        g_sets = [a.reshape(N_EXPERTS, -1, 1) for a in gathered[1::3]]
        ys = _expert_mlp(x_sets, g_sets, exp_w_gate, exp_w_up, exp_w_down, l, EXPERT_ROWS)
        yl = ys[0].reshape(N_EXPERTS, B, cap_l, D)
        xs = _combine(cnt, x1, jnp.swapaxes(posm, 1, 2), yl, il, None if last else ys[1], mod[l],
                      final_norm_g[None], cap_l, cc, final=last)
    return xs
```

```python
import functools

import numpy as np
import jax
import jax.numpy as jnp
from jax import lax
from jax.experimental import pallas as pl
from jax.experimental.pallas import tpu as pltpu

F32 = jnp.float32
BF16 = jnp.bfloat16
I32 = jnp.int32
HIGHEST = lax.Precision.HIGHEST

D_MODEL = 1024
GRID_W = 64
EPS = 1e-6
LB_FLOOR = 1e-30
N_MOD = 6
ATT_W = 512
HG_W = 256
CV_W = 256
HEAD_DIM = 64
N_Q = 8
N_KV = 2
GQA_GROUP = 4
KV_W = 128
ROPE_BASE = 10000.0
ROPE_FREQS = 16
HG_HEADS = 4
N_EXPERTS = 16
EC_CAPACITY = 2
D_EXPERT = 2048
IN_COLS = 2816

ROW_TILE = 256
ATT_BLOCK = 128
HG_CHUNK = 128
HG_LEVELS = 7
HG_BATCH = 2
HG_LOCAL = 32
HG_LOCAL_MAX_LOG = 80.0
SLOT_BLOCK = 256
PACK_GRANULE = 16
PACK_FIRST = 1024
PACK_CHUNK = 512
FF_TILE = 512
EXPERT_ROWS = 512
MOD_TILE = 1536
VMEM_LIMIT = 56 * 1024 * 1024

NEG_BIG = -1e30


def _cparams(sem):
    return pltpu.CompilerParams(dimension_semantics=sem, vmem_limit_bytes=VMEM_LIMIT)


def _dot(a, b):
    return jnp.dot(a, b, preferred_element_type=F32)


def _dot_nt(a, b):
    return lax.dot_general(a, b, (((1,), (1,)), ((), ())), preferred_element_type=F32)


def _mod_kernel(a_ref, w_ref, b_ref, o_ref):
    a = a_ref[...]
    a = a * jax.nn.sigmoid(a)
    o_ref[0] = jnp.dot(a, w_ref[0], precision=HIGHEST, preferred_element_type=F32) + b_ref[0]


def _modulation(cvec, ada_w, ada_b):
    depth = ada_w.shape[0]
    ncol = ada_w.shape[2]
    return pl.pallas_call(
        _mod_kernel,
        grid=(depth, ncol // MOD_TILE),
        in_specs=[
            pl.BlockSpec((8, D_MODEL), lambda l, j: (0, 0)),
            pl.BlockSpec((1, D_MODEL, MOD_TILE), lambda l, j: (l, 0, j)),
            pl.BlockSpec((1, 1, MOD_TILE), lambda l, j: (l, 0, j)),
        ],
        out_specs=pl.BlockSpec((1, 8, MOD_TILE), lambda l, j: (l, 0, j)),
        out_shape=jax.ShapeDtypeStruct((depth, 8, ncol), F32),
        name="modulation",
        compiler_params=_cparams(("arbitrary", "arbitrary")),
    )(cvec, ada_w, ada_b.reshape(depth, 1, ncol))


def _swap_halves(x):
    n = x.shape[-1]
    lane = lax.broadcasted_iota(I32, x.shape, x.ndim - 1)
    up = pltpu.roll(x, n - ROPE_FREQS, x.ndim - 1)
    dn = pltpu.roll(x, ROPE_FREQS, x.ndim - 1)
    return jnp.where((lane % (2 * ROPE_FREQS)) < ROPE_FREQS, up, dn)


def _rmsnorm_mod(x, g, shift, scale):
    var = jnp.mean(x * x, axis=-1, keepdims=True)
    y = x * lax.rsqrt(var + EPS) * g
    return y * (1.0 + scale) + shift


def _row_sources(src):
    if isinstance(src, tuple):
        ctx, lat = src
        return ctx, lat, 1, ctx.shape[1] + lat.shape[1]
    return src, src, 0, src.shape[1]


def _inproj_kernel(xc_ref, x_ref, mod_ref, g_ref, w_ref, cos_ref, sin_ref,
                   q_ref, kv_ref, hg_ref, hqg_ref, cv_ref):
    b = pl.program_id(0)
    i = pl.program_id(1)
    row = jnp.where(i == 0, 4, b)
    shift = mod_ref[pl.ds(row, 1), 0:D_MODEL]
    scale = mod_ref[pl.ds(row, 1), D_MODEL:2 * D_MODEL]
    xin = jnp.where(i == 0, xc_ref[0], x_ref[0])
    h = _rmsnorm_mod(xin, g_ref[...], shift, scale)
    p = _dot(h.astype(BF16), w_ref[...])
    cos2 = cos_ref[...]
    sin2 = sin_ref[...]
    k = p[:, 0:KV_W]
    k = k * cos2 + _swap_halves(k) * sin2
    kv_ref[0, :, 0:KV_W] = k.astype(BF16)
    kv_ref[0, :, KV_W:2 * KV_W] = p[:, KV_W:2 * KV_W].astype(BF16)
    hg_ref[0] = p[:, 256:1024]
    q = p[:, 1024:1536]
    cos8 = jnp.concatenate([cos2] * 4, axis=1)
    sin8 = jnp.concatenate([sin2] * 4, axis=1)
    q = (q * cos8 + _swap_halves(q) * sin8) * (HEAD_DIM ** -0.5)
    q_ref[0] = q.astype(BF16)
    hqg_ref[0] = p[:, 1536:2048]
    cv_ref[0] = p[:, 2048:2816]


def _in_projection(src, mod_l, g, w_bf16, cos_t, sin_t):
    xc, xl, off, S = _row_sources(src)
    B = xc.shape[0]
    nt = S // ROW_TILE
    row_map = lambda b, i: (b, i, 0)
    const2 = lambda b, i: (0, 0)
    return pl.pallas_call(
        _inproj_kernel,
        grid=(B, nt),
        in_specs=[
            pl.BlockSpec((1, ROW_TILE, D_MODEL), lambda b, i: (b, 0, 0)),
            pl.BlockSpec((1, ROW_TILE, D_MODEL), lambda b, i: (b, jnp.maximum(i - off, 0), 0)),
            pl.BlockSpec((8, N_MOD * D_MODEL), const2),
            pl.BlockSpec((1, D_MODEL), const2),
            pl.BlockSpec((D_MODEL, IN_COLS), const2),
            pl.BlockSpec((ROW_TILE, 2 * HEAD_DIM), lambda b, i: (i, 0)),
            pl.BlockSpec((ROW_TILE, 2 * HEAD_DIM), lambda b, i: (i, 0)),
        ],
        out_specs=[
            pl.BlockSpec((1, ROW_TILE, ATT_W), row_map),
            pl.BlockSpec((1, ROW_TILE, 2 * KV_W), row_map),
            pl.BlockSpec((1, ROW_TILE, 3 * HG_W), row_map),
            pl.BlockSpec((1, ROW_TILE, 2 * HG_W), row_map),
            pl.BlockSpec((1, ROW_TILE, 3 * CV_W), row_map),
        ],
        out_shape=[
            jax.ShapeDtypeStruct((B, S, ATT_W), BF16),
            jax.ShapeDtypeStruct((B, S, 2 * KV_W), BF16),
            jax.ShapeDtypeStruct((B, S, 3 * HG_W), F32),
            jax.ShapeDtypeStruct((B, S, 2 * HG_W), F32),
            jax.ShapeDtypeStruct((B, S, 3 * CV_W), F32),
        ],
        name="in_projection",
        compiler_params=_cparams(("arbitrary", "arbitrary")),
    )(xc, xl, mod_l, g, w_bf16, cos_t, sin_t)


def _attn_kernel(sink_ref, q_ref, kp_ref, kc_ref, kn_ref, kx_ref, o_ref, *, blk0, nblk, nctx):
    n = pl.program_id(1) + blk0
    is_lat = n >= nctx
    has_prev = n > nctx
    has_next = n < nblk - 1
    W = ATT_BLOCK
    q = q_ref[0]
    qrows = jnp.concatenate([q[:, g * W:(g + 1) * W] for g in range(GQA_GROUP)], axis=0)
    kv_all = jnp.concatenate([kp_ref[0], kc_ref[0], kn_ref[0], kx_ref[0]], axis=0)
    nkeys = kv_all.shape[0]
    k_all = kv_all[:, 0:KV_W]
    v_ext = jnp.concatenate([kv_all[:, KV_W:2 * KV_W], jnp.ones((nkeys, KV_W), BF16)], axis=1)
    rows = GQA_GROUP * W
    ri = lax.broadcasted_iota(I32, (rows, W), 0) % W
    cj = lax.broadcasted_iota(I32, (rows, W), 1)
    m_prev = (cj >= ri) & has_prev
    m_cur = jnp.broadcast_to(is_lat, (rows, W))
    m_next = (cj <= ri) & (has_next & is_lat)
    grp = lax.broadcasted_iota(I32, (rows, 1), 0) // W
    low = cj < HEAD_DIM
    outs = []
    for h in range(N_KV):
        qh = jnp.where(low if h == 0 else jnp.logical_not(low), qrows, jnp.zeros_like(qrows))
        sink = jnp.zeros((rows, 1), F32)
        for g in range(GQA_GROUP):
            sink = jnp.where(grp == g, sink_ref[h * GQA_GROUP + g], sink)
        s = _dot_nt(qh, k_all)
        segs = [jnp.where(m_prev, s[:, 0:W], NEG_BIG),
                jnp.where(m_cur, s[:, W:2 * W], NEG_BIG),
                jnp.where(m_next, s[:, 2 * W:3 * W], NEG_BIG)]
        segs += [s[:, c:c + W] for c in range(3 * W, nkeys, W)]
        mx = segs[0]
        for sg in segs[1:]:
            mx = jnp.maximum(mx, sg)
        m = jnp.maximum(jnp.max(mx, axis=1, keepdims=True), sink)
        p = jnp.concatenate([jnp.exp(sg - m).astype(BF16) for sg in segs], axis=1)
        oe = _dot(p, v_ext)
        den = oe[:, KV_W:2 * KV_W] + jnp.exp(sink - m)
        outs.append(oe[:, 0:KV_W] / den)
    o = jnp.where(low, outs[0], outs[1])
    for g in range(GQA_GROUP):
        o_ref[0, :, g * W:(g + 1) * W] = o[g * W:(g + 1) * W].astype(BF16)


def _attention(q, kv, sink, ctx_len, skip_ctx):
    B, S, _ = q.shape
    nblk = S // ATT_BLOCK
    nctx = ctx_len // ATT_BLOCK
    blk0 = nctx if skip_ctx else 0
    blk = lambda f: (lambda b, n: (b, f(n + blk0), 0))
    kern = functools.partial(_attn_kernel, blk0=blk0, nblk=nblk, nctx=nctx)
    return pl.pallas_call(
        kern,
        grid=(B, nblk - blk0),
        in_specs=[
            pl.BlockSpec(memory_space=pltpu.SMEM),
            pl.BlockSpec((1, ATT_BLOCK, ATT_W), blk(lambda n: n)),
            pl.BlockSpec((1, ATT_BLOCK, 2 * KV_W), blk(lambda n: jnp.maximum(n - 1, 0))),
            pl.BlockSpec((1, ATT_BLOCK, 2 * KV_W), blk(lambda n: n)),
            pl.BlockSpec((1, ATT_BLOCK, 2 * KV_W), blk(lambda n: jnp.minimum(n + 1, nblk - 1))),
            pl.BlockSpec((1, ctx_len, 2 * KV_W), lambda b, n: (b, 0, 0)),
        ],
        out_specs=pl.BlockSpec((1, ATT_BLOCK, ATT_W), lambda b, n: (b, n, 0)),
        out_shape=jax.ShapeDtypeStruct((B, S - blk0 * ATT_BLOCK, ATT_W), BF16),
        name="attention",
        compiler_params=_cparams(("arbitrary", "arbitrary")),
    )(sink, q, kv, kv, kv, kv)


def _hgrn_constants():
    C = HG_CHUNK
    t = np.arange(C)[:, None]
    r = np.arange(C)[None, :]
    tri = np.stack([r <= t, r >= t]).astype(np.float32)
    x = t ^ r
    lvl = np.where(x > 0, np.floor(np.log2(np.maximum(x, 1))).astype(np.int32), HG_LEVELS)
    lvl_f = np.where(t >= r, lvl, -1).astype(np.int32)
    lvl_b = np.where(t <= r, lvl, -1).astype(np.int32)
    lvl2 = np.stack([np.tile(lvl_f, (1, HG_HEADS)), np.tile(lvl_b, (1, HG_HEADS))])
    return tri, lvl2


def _span_row(x, span, row):
    C = x.shape[0]
    if span >= 8:
        x3 = x.reshape(C // span, span, x.shape[1])
        return jnp.broadcast_to(x3[:, row:row + 1, :], x3.shape).reshape(x.shape)
    pos = lax.broadcasted_iota(I32, x.shape, 0) % span
    out = x
    for p in range(span):
        if p != row:
            out = jnp.where(pos == p, pltpu.roll(x, (p - row) % C, 0), out)
    return out


def _hgrn_prepare(v, z, qr, lb, tri, backward):
    C = HG_CHUNK
    logf = jnp.log(jnp.maximum(lb, LB_FLOOR) + (1.0 - lb) * jax.nn.sigmoid(z))
    k = (1.0 - lb) * jax.nn.sigmoid(-z)
    q = qr * jax.nn.sigmoid(qr)
    hi = logf.astype(BF16)
    lo = (logf - hi.astype(F32)).astype(BF16)
    cs = _dot(tri, jnp.concatenate([hi, lo], axis=1))
    lam = cs[:, 0:HG_W] + cs[:, HG_W:2 * HG_W]
    tot = lam[0:1] if backward else lam[C - 1:C]
    local = lam - _span_row(lam, HG_LOCAL, HG_LOCAL // 2)
    return dict(v=v, q=q, k=k, lam=lam, tot=tot, local=local)


def _hgrn_finish(g, head_masks, same_head, lvl, st_ref, sidx, backward, fast):
    C = HG_CHUNK
    q, k, v, lam, tot = g["q"], g["k"], g["v"], g["lam"], g["tot"]
    zero = jnp.zeros((C, HG_W), BF16)

    def per_head_rows(x):
        return jnp.concatenate([jnp.where(hm, x, zero) for hm in head_masks], axis=0)

    qb = q.astype(BF16)
    kb = k.astype(BF16)
    if fast:
        n_local = HG_LOCAL.bit_length() - 1
        a = jnp.where((lvl == HG_LEVELS) | ((lvl >= 0) & (lvl < n_local)),
                      _dot_nt((q * jnp.exp(g["local"])).astype(BF16),
                              per_head_rows((k * jnp.exp(-g["local"])).astype(BF16))), 0.0)
        levels = range(n_local, HG_LEVELS)
    else:
        a = jnp.where(lvl == HG_LEVELS, _dot_nt(qb, per_head_rows(kb)), 0.0)
        levels = range(HG_LEVELS)
    for l in levels:
        m = 1 << l
        ref = _span_row(lam, 2 * m, m if backward else m - 1)
        fac = jnp.exp(-jnp.abs(lam - ref)).astype(BF16)
        a = jnp.where(lvl == l, _dot_nt(qb * fac, per_head_rows(kb * fac)), a)
    st = st_ref[sidx]
    q_in = (q * jnp.exp(lam)).astype(BF16)
    o = _dot(a.astype(BF16), per_head_rows(v.astype(BF16))) + _dot_nt(q_in, st.astype(BF16))
    k_out = (k * jnp.exp(tot - lam)).astype(BF16)
    upd = _dot(v.T.astype(BF16), k_out)
    st_ref[sidx] = st * jnp.exp(tot) + jnp.where(same_head, upd, 0.0)
    return o


def _hgrn_kernel(vf_ref, zf_ref, qf_ref, vb_ref, zb_ref, qb_ref, lb_ref, tri_ref, lvl_ref,
                 of_ref, ob_ref, st_ref):
    j = pl.program_id(1)

    @pl.when(j == 0)
    def _():
        st_ref[...] = jnp.zeros_like(st_ref)

    head_id = (lax.broadcasted_iota(I32, (HG_CHUNK, HG_W), 1) // HEAD_DIM).astype(F32).astype(BF16)
    head_masks = [head_id == float(h) for h in range(HG_HEADS)]
    same_head = (lax.broadcasted_iota(I32, (HG_W, HG_W), 0) // HEAD_DIM
                 == lax.broadcasted_iota(I32, (HG_W, HG_W), 1) // HEAD_DIM)
    streams = []
    for d, (v_ref, z_ref, q_ref, o_ref) in enumerate(
            ((vf_ref, zf_ref, qf_ref, of_ref), (vb_ref, zb_ref, qb_ref, ob_ref))):
        for s in range(v_ref.shape[0]):
            g = _hgrn_prepare(v_ref[s], z_ref[s], q_ref[s], lb_ref[d], tri_ref[d], d == 1)
            streams.append((d, s, o_ref, g))
    worst = jnp.abs(streams[0][3]["local"])
    for _, _, _, g in streams[1:]:
        worst = jnp.maximum(worst, jnp.abs(g["local"]))
    can_use_local = jnp.max(worst) <= HG_LOCAL_MAX_LOG

    def run(fast):
        for d, s, o_ref, g in streams:
            o_ref[s] = _hgrn_finish(g, head_masks, same_head, lvl_ref[d], st_ref, 2 * s + d,
                                    backward=(d == 1), fast=fast)

    pl.when(can_use_local)(functools.partial(run, True))
    pl.when(jnp.logical_not(can_use_local))(functools.partial(run, False))


def _hgrn(hg, hqg, lb_l, ctx_len):
    B, S, _ = hg.shape
    nc = S // HG_CHUNK
    nctx = ctx_len // HG_CHUNK
    tri_np, lvl_np = _hgrn_constants()
    tri = jnp.asarray(tri_np, BF16)
    lvl = jnp.asarray(lvl_np, I32)

    def back(j):
        return jnp.where(j < nctx, nctx - 1 - j, nc - 1 - (j - nctx))

    bs = HG_BATCH if B % HG_BATCH == 0 else 1
    blk = (bs, HG_CHUNK, HG_W)
    fwd = lambda col: (lambda b, j: (b, j, col))
    bwd = lambda col: (lambda b, j: (b, back(j), col))
    const3 = lambda b, j: (0, 0, 0)
    return pl.pallas_call(
        _hgrn_kernel,
        grid=(B // bs, nc),
        in_specs=[
            pl.BlockSpec(blk, fwd(0)), pl.BlockSpec(blk, fwd(1)), pl.BlockSpec(blk, fwd(0)),
            pl.BlockSpec(blk, bwd(0)), pl.BlockSpec(blk, bwd(2)), pl.BlockSpec(blk, bwd(0)),
            pl.BlockSpec((2, 1, HG_W), const3),
            pl.BlockSpec((2, HG_CHUNK, HG_CHUNK), const3),
            pl.BlockSpec((2, HG_CHUNK, HG_HEADS * HG_CHUNK), const3),
        ],
        out_specs=[pl.BlockSpec(blk, fwd(0)), pl.BlockSpec(blk, bwd(0))],
        out_shape=[jax.ShapeDtypeStruct((B, S, HG_W), F32)] * 2,
        scratch_shapes=[pltpu.VMEM((2 * bs, HG_W, HG_W), F32)],
        name="hgrn_scan",
        compiler_params=_cparams(("arbitrary", "arbitrary")),
    )(hg, hg, hqg, hg, hg, hqg, lb_l.reshape(2, 1, HG_W), tri, lvl)


def _mixout_kernel(xc_ref, x_ref, att_ref, of_ref, ob_ref, g_ref, cv_ref, cprev_ref, cnext_ref,
                   mod_ref, gain_ref, cw_ref, wo_ref, n2_ref, wr_ref, ones_ref,
                   x1_ref, h2_ref, aff_ref, *, tile0, ntile):
    b = pl.program_id(0)
    i = pl.program_id(1) + tile0
    row = jnp.where(i == 0, 4, b)
    R = ROW_TILE

    def modv(c):
        return mod_ref[pl.ds(row, 1), c * D_MODEL:(c + 1) * D_MODEL]

    o = of_ref[0] + ob_ref[0]
    sq = o * o
    sq_hi = sq.astype(BF16)
    sq_lo = (sq - sq_hi.astype(F32)).astype(BF16)
    ms = _dot(sq_hi, ones_ref[...]) + _dot(sq_lo, ones_ref[...])
    g = g_ref[0]
    hg = o * lax.rsqrt(ms * (1.0 / HEAD_DIM) + EPS) * gain_ref[...] * (g * jax.nn.sigmoid(g))
    cv = cv_ref[0]
    u = cv[:, CV_W:2 * CV_W] * cv[:, 2 * CV_W:3 * CV_W]
    up = cprev_ref[0]
    un = cnext_ref[0]
    u_prev_row = up[7:8, CV_W:2 * CV_W] * up[7:8, 2 * CV_W:3 * CV_W]
    u_next_row = un[0:1, CV_W:2 * CV_W] * un[0:1, 2 * CV_W:3 * CV_W]
    u_prev_row = jnp.where(i <= 1, 0.0, u_prev_row)
    u_next_row = jnp.where((i == 0) | (i == ntile - 1), 0.0, u_next_row)
    ridx = lax.broadcasted_iota(I32, (R, CV_W), 0)
    u_m1 = jnp.where(ridx == 0, u_prev_row, pltpu.roll(u, 1, 0))
    u_p1 = jnp.where(ridx == R - 1, u_next_row, pltpu.roll(u, R - 1, 0))
    cw = cw_ref[...]
    conv = cv[:, 0:CV_W] * (u_m1 * cw[0:1] + u * cw[1:2] + u_p1 * cw[2:3])
    mix = (_dot(att_ref[0], wo_ref[0:ATT_W])
           + _dot(hg.astype(BF16), wo_ref[ATT_W:ATT_W + HG_W])
           + _dot(conv.astype(BF16), wo_ref[ATT_W + HG_W:D_MODEL]))
    x1 = jnp.where(i == 0, xc_ref[0], x_ref[0]) + modv(2) * mix
    x1_ref[0] = x1
    h2 = _rmsnorm_mod(x1, n2_ref[...], modv(3), modv(4))
    h_hi = h2.astype(BF16)
    h2_ref[0] = h_hi
    h_lo = (h2 - h_hi.astype(F32)).astype(BF16)
    wr = wr_ref[...]
    w_hi = wr.astype(BF16)
    w_lo = (wr - w_hi.astype(F32)).astype(BF16)
    r1 = _dot_nt(jnp.concatenate([w_hi, w_lo], axis=0), h_hi)
    logits = r1[0:N_EXPERTS] + r1[N_EXPERTS:2 * N_EXPERTS] + _dot_nt(w_hi, h_lo)
    e = jnp.exp(logits - jnp.max(logits, axis=0, keepdims=True))
    aff_ref[0] = e / jnp.sum(e, axis=0, keepdims=True)


def _mix_out(src, att, o2, hqg, cv, mod_l, gain, conv_w, wo_bf16, n2g, wr_t, skip_ctx):
    xc, xl, off, S = _row_sources(src)
    B = xc.shape[0]
    ntile = S // ROW_TILE
    tile0 = 1 if skip_ctx else 0
    sub = ROW_TILE // 8
    nsub = S // 8
    rmap = lambda b, i: (b, i + tile0, 0)
    omap = lambda b, i: (b, i, 0)
    s_out = S - tile0 * ROW_TILE
    const2 = lambda b, i: (0, 0)
    ones = jnp.asarray(np.kron(np.eye(HG_HEADS), np.ones((HEAD_DIM, HEAD_DIM))), BF16)
    kern = functools.partial(_mixout_kernel, tile0=tile0, ntile=ntile)
    return pl.pallas_call(
        kern,
        grid=(B, ntile - tile0),
        in_specs=[
            pl.BlockSpec((1, ROW_TILE, D_MODEL), lambda b, i: (b, 0, 0)),
            pl.BlockSpec((1, ROW_TILE, D_MODEL),
                         lambda b, i: (b, jnp.maximum(i + tile0 - off, 0), 0)),
            pl.BlockSpec((1, ROW_TILE, ATT_W), omap),
            pl.BlockSpec((1, ROW_TILE, HG_W), rmap),
            pl.BlockSpec((1, ROW_TILE, HG_W), rmap),
            pl.BlockSpec((1, ROW_TILE, HG_W), lambda b, i: (b, i + tile0, 1)),
            pl.BlockSpec((1, ROW_TILE, 3 * CV_W), rmap),
            pl.BlockSpec((1, 8, 3 * CV_W),
                         lambda b, i: (b, jnp.maximum((i + tile0) * sub - 1, 0), 0)),
            pl.BlockSpec((1, 8, 3 * CV_W),
                         lambda b, i: (b, jnp.minimum((i + tile0 + 1) * sub, nsub - 1), 0)),
            pl.BlockSpec((8, N_MOD * D_MODEL), const2),
            pl.BlockSpec((1, HG_W), const2),
            pl.BlockSpec((3, CV_W), const2),
            pl.BlockSpec((D_MODEL, D_MODEL), const2),
            pl.BlockSpec((1, D_MODEL), const2),
            pl.BlockSpec((N_EXPERTS, D_MODEL), const2),
            pl.BlockSpec((HG_W, HG_W), const2),
        ],
        out_specs=[
            pl.BlockSpec((1, ROW_TILE, D_MODEL), omap),
            pl.BlockSpec((1, ROW_TILE, D_MODEL), omap),
            pl.BlockSpec((1, N_EXPERTS, ROW_TILE), lambda b, i: (b, 0, i)),
        ],
        out_shape=[
            jax.ShapeDtypeStruct((B, s_out, D_MODEL), F32),
            jax.ShapeDtypeStruct((B, s_out, D_MODEL), BF16),
            jax.ShapeDtypeStruct((B, N_EXPERTS, s_out), F32),
        ],
        name="mix_out",
        compiler_params=_cparams(("arbitrary", "arbitrary")),
    )(xc, xl, att, o2[0], o2[1], hqg, cv, cv, cv, mod_l, gain, conv_w, wo_bf16, n2g, wr_t, ones)


def _topk_kernel(aff_ref, tri_ref, pos_ref, cnt_ref, *, segments):
    tri = tri_ref[...]

    def excl_cumsum(mask, n):
        carry = jnp.zeros((N_EXPERTS, 1), F32)
        parts = []
        carries = []
        for c in range(n // 128):
            carries.append(carry)
            blk = jnp.where(mask[:, c * 128:(c + 1) * 128], 1.0, 0.0)
            parts.append(_dot(blk.astype(BF16), tri) + carry)
            carry = carry + jnp.sum(blk, axis=1, keepdims=True)
        return jnp.concatenate(parts, axis=1), carries + [carry]

    for (lo, n, k) in segments:
        a = aff_ref[0, :, lo:lo + n]

        def body(it, thr):
            cand = thr | jnp.left_shift(jnp.int32(1), 30 - it)
            cnt = jnp.sum(jnp.where(a >= pltpu.bitcast(cand, F32), 1.0, 0.0), axis=1, keepdims=True)
            return jnp.where(cnt >= k, cand, thr)

        thr = lax.fori_loop(0, 31, body, jnp.zeros((N_EXPERTS, 1), I32))
        above = a >= pltpu.bitcast(thr + 1, F32)
        tied = jnp.logical_and(a >= pltpu.bitcast(thr, F32), jnp.logical_not(above))
        n_above = jnp.sum(jnp.where(above, 1.0, 0.0), axis=1, keepdims=True)
        rank_tied, _ = excl_cumsum(tied, n)
        sel = above | (tied & (rank_tied < (k - n_above)))
        pos, counts = excl_cumsum(sel, n)
        pos_ref[0, :, lo:lo + n] = jnp.where(sel, pos.astype(I32), -1)
    step = ROW_TILE // 128
    lane = lax.broadcasted_iota(I32, (N_EXPERTS, 128), 1)
    cnt = jnp.zeros((N_EXPERTS, 128), F32)
    for c, col in enumerate(counts[0::step]):
        cnt = jnp.where(lane == c, col, cnt)
    cnt_ref[0] = cnt.astype(I32)


def _topk_positions(aff, segments):
    B, E, S = aff.shape
    assert segments[-1][1] // ROW_TILE + 1 <= 128
    tri = jnp.asarray(np.triu(np.ones((128, 128)), 1), BF16)
    kern = functools.partial(_topk_kernel, segments=segments)
    return pl.pallas_call(
        kern,
        grid=(B,),
        in_specs=[pl.BlockSpec((1, E, S), lambda b: (b, 0, 0)),
                  pl.BlockSpec((128, 128), lambda b: (0, 0))],
        out_specs=[pl.BlockSpec((1, E, S), lambda b: (b, 0, 0)),
                   pl.BlockSpec((1, E, 128), lambda b: (b, 0, 0))],
        out_shape=[jax.ShapeDtypeStruct((B, E, S), I32),
                   jax.ShapeDtypeStruct((B, E, 128), I32)],
        name="topk_positions",
        compiler_params=_cparams(("arbitrary",)),
    )(aff, tri)


def _slot_block(cap):
    sbs = min(SLOT_BLOCK, cap)
    assert sbs >= ROW_TILE or sbs == cap
    return sbs


def _div_pow2(x, d):
    assert d & (d - 1) == 0
    return lax.shift_right_logical(x, jnp.int32(d.bit_length() - 1))


def _gather_kernel(cnt_ref, pos_ref, aff_ref, h_ref, *refs, ctx_len, cap_l, cap_c):
    if cap_c:
        xl_ref, gl_ref, il_ref, xc_ref, gc_ref, acc_ref, gacc_ref, iacc_ref = refs
    else:
        xl_ref, gl_ref, il_ref, acc_ref, gacc_ref, iacc_ref = refs
    b = pl.program_id(0)
    e = pl.program_id(1)
    S = h_ref.shape[1]
    pos = pos_ref[0, pl.ds(e, 1), :]
    aff = aff_ref[0, pl.ds(e, 1), :]
    nchunk = (S - ctx_len) // ROW_TILE
    sbs = _slot_block(cap_l)
    acc_ref[...] = jnp.zeros_like(acc_ref)
    gacc_ref[...] = jnp.zeros_like(gacc_ref)
    iacc_ref[...] = jnp.zeros_like(iacc_ref)
    nsb = cap_l // sbs
    first = [jnp.minimum(_div_pow2(cnt_ref[b, e, c], sbs), nsb - 1) for c in range(nchunk)]
    slot_i32 = lax.broadcasted_iota(I32, (sbs, ROW_TILE), 0)
    slot_in_block = slot_i32.astype(F32).astype(BF16)
    one = jnp.ones((sbs, ROW_TILE), BF16)
    zero = jnp.zeros((sbs, ROW_TILE), BF16)

    def visit(c, sb):
        lo = ctx_len + c * ROW_TILE
        base = pl.multiple_of(sb * sbs, sbs)
        rel_i = pos[:, lo:lo + ROW_TILE] - base
        rel = rel_i.astype(F32)
        rel = jnp.where((rel >= 0.0) & (rel < float(sbs)), rel, -1.0).astype(BF16)
        acc_ref[pl.ds(base, sbs), :] += _dot(jnp.where(rel == slot_in_block, one, zero),
                                             h_ref[0, lo:lo + ROW_TILE, :])
        hit = rel_i == slot_i32
        gacc_ref[pl.ds(base, sbs), :] += jnp.sum(
            jnp.where(hit, aff[:, lo:lo + ROW_TILE], 0.0), axis=1, keepdims=True)
        tok = (lax.broadcasted_iota(I32, (1, ROW_TILE), 1) + c * ROW_TILE).astype(F32)
        iacc_ref[pl.ds(base, sbs), :] += jnp.sum(jnp.where(hit, tok, 0.0), axis=1, keepdims=True)

    for c in range(nchunk):
        visit(c, first[c])
    for c in range(nchunk):
        pl.when(cnt_ref[b, e, c + 1] > (first[c] + 1) * sbs)(
            functools.partial(visit, c, first[c] + 1))
    xl_ref[0, 0] = acc_ref[...].astype(BF16)
    gl_ref[0, 0] = gacc_ref[...]
    il_ref[0, 0] = iacc_ref[...].astype(I32)
    if cap_c:
        slot_c = lax.broadcasted_iota(I32, (cap_c, ctx_len), 0)
        hit = pos[:, 0:ctx_len] == slot_c
        xc_ref[0, 0] = _dot(jnp.where(hit, 1.0, 0.0).astype(BF16),
                            h_ref[0, 0:ctx_len, :]).astype(BF16)
        gc_ref[0, 0] = jnp.sum(jnp.where(hit, aff[:, 0:ctx_len], 0.0), axis=1, keepdims=True)


def _gather_rows(cnt, posm, aff, h2, ctx_len, cap_l, cap_c):
    B, E, S = posm.shape
    kern = functools.partial(_gather_kernel, ctx_len=ctx_len, cap_l=cap_l, cap_c=cap_c)
    omap = lambda b, e, cnt_ref: (e, b, 0, 0)
    caps = [cap_l] + ([cap_c] if cap_c else [])
    out_specs = []
    out_shape = []
    for cap in caps:
        out_specs += [pl.BlockSpec((1, 1, cap, D_MODEL), omap), pl.BlockSpec((1, 1, cap, 1), omap)]
        out_shape += [jax.ShapeDtypeStruct((E, B, cap, D_MODEL), BF16),
                      jax.ShapeDtypeStruct((E, B, cap, 1), F32)]
        if cap is caps[0]:
            out_specs.append(pl.BlockSpec((1, 1, cap, 1), omap))
            out_shape.append(jax.ShapeDtypeStruct((E, B, cap, 1), I32))
    grid_spec = pltpu.PrefetchScalarGridSpec(
        num_scalar_prefetch=1,
        grid=(B, E),
        in_specs=[pl.BlockSpec((1, E, S), lambda b, e, cnt_ref: (b, 0, 0)),
                  pl.BlockSpec((1, E, S), lambda b, e, cnt_ref: (b, 0, 0)),
                  pl.BlockSpec((1, S, D_MODEL), lambda b, e, cnt_ref: (b, 0, 0))],
        out_specs=out_specs,
        scratch_shapes=[pltpu.VMEM((cap_l, D_MODEL), F32), pltpu.VMEM((cap_l, 1), F32),
                        pltpu.VMEM((cap_l, 1), F32)],
    )
    return pl.pallas_call(
        kern,
        grid_spec=grid_spec,
        out_shape=out_shape,
        name="gather_rows",
        compiler_params=_cparams(("arbitrary", "arbitrary")),
    )(cnt, posm, aff, h2)


def _expert_kernel(*refs, row_chunk, n_sets):
    x_refs = refs[:n_sets]
    g_refs = refs[n_sets:2 * n_sets]
    wg_ref, wu_ref, wd_ref = refs[2 * n_sets:2 * n_sets + 3]
    y_refs = refs[2 * n_sets + 3:3 * n_sets + 3]
    acc_refs = refs[3 * n_sets + 3:]
    f = pl.program_id(1)
    nf = pl.num_programs(1)
    wg = wg_ref[0, 0].astype(BF16)
    wu = wu_ref[0, 0].astype(BF16)
    wd = wd_ref[0, 0].astype(BF16)

    @pl.when(f == 0)
    def _():
        for acc_ref in acc_refs:
            acc_ref[...] = jnp.zeros_like(acc_ref)

    for x_ref, acc_ref in zip(x_refs, acc_refs):
        rows = x_ref.shape[1]
        step = min(row_chunk, rows)
        for r in range(rows // step):
            rs = slice(r * step, (r + 1) * step)
            x = x_ref[0, rs, :]
            g = _dot(x, wg)
            u = _dot(x, wu)
            hid = (g * jax.nn.sigmoid(g) * u).astype(BF16)
            acc_ref[rs, :] += _dot(hid, wd)

    @pl.when(f == nf - 1)
    def _():
        for g_ref, y_ref, acc_ref in zip(g_refs, y_refs, acc_refs):
            y_ref[0] = (acc_ref[...] * g_ref[0]).astype(BF16)


def _expert_mlp(x_sets, g_sets, w_gate, w_up, w_down, layer, row_chunk):
    E = N_EXPERTS
    nf = D_EXPERT // FF_TILE
    n_sets = len(x_sets)
    kern = functools.partial(_expert_kernel, row_chunk=row_chunk, n_sets=n_sets)
    xspec = lambda r: pl.BlockSpec((1, r, D_MODEL), lambda e, f: (e, 0, 0))
    gspec = lambda r: pl.BlockSpec((1, r, 1), lambda e, f: (e, 0, 0))
    return pl.pallas_call(
        kern,
        grid=(E, nf),
        in_specs=[xspec(xs.shape[1]) for xs in x_sets] + [gspec(xs.shape[1]) for xs in x_sets] + [
            pl.BlockSpec((1, 1, D_MODEL, FF_TILE), lambda e, f: (layer, e, 0, f)),
            pl.BlockSpec((1, 1, D_MODEL, FF_TILE), lambda e, f: (layer, e, 0, f)),
            pl.BlockSpec((1, 1, FF_TILE, D_MODEL), lambda e, f: (layer, e, f, 0))],
        out_specs=[xspec(xs.shape[1]) for xs in x_sets],
        out_shape=[jax.ShapeDtypeStruct(xs.shape, BF16) for xs in x_sets],
        scratch_shapes=[pltpu.VMEM(xs.shape[1:], F32) for xs in x_sets],
        name="expert_mlp",
        compiler_params=_cparams(("arbitrary", "arbitrary")),
    )(*x_sets, *g_sets, w_gate, w_up, w_down)


def _combine_kernel(cnt_ref, x_ref, pos_ref, *refs, cap_l, cap_c, final):
    if cap_c:
        yl_ref, il_ref, yc_ref, mod_ref, fg_ref, o_ref, acc_ref, pk_ref, tk_ref = refs
    else:
        yl_ref, il_ref, mod_ref, fg_ref, o_ref, acc_ref, pk_ref, tk_ref = refs
    b = pl.program_id(0)
    i = pl.program_id(1)
    row = jnp.where(i == 0, 4, b) if cap_c else b
    gate = mod_ref[pl.ds(row, 1), 5 * D_MODEL:6 * D_MODEL]
    pos = pos_ref[0]
    acc_ref[...] = jnp.zeros_like(acc_ref)
    GR = PACK_GRANULE
    KC = PACK_CHUNK

    @pl.when((b == 0) & (i == 0))
    def _():
        pk_ref[...] = jnp.zeros_like(pk_ref)
        tk_ref[...] = jnp.full(tk_ref.shape, -1, I32)

    def add_latent():
        t = i - 1 if cap_c else i
        total = 0
        for e in range(N_EXPERTS):
            g0 = _div_pow2(cnt_ref[b, e, t], GR)
            n = _div_pow2(cnt_ref[b, e, t + 1] + (GR - 1), GR) - g0

            def copy_granule(k, carry, e=e, src0=g0 * GR, dst0=total):
                s = pl.multiple_of(src0 + k * GR, GR)
                d = pl.multiple_of(dst0 + k * GR, GR)
                pk_ref[pl.ds(d, GR), :] = yl_ref[e, 0, pl.ds(s, GR), :]
                tk_ref[pl.ds(d, GR), :] = il_ref[e, 0, pl.ds(s, GR), :]
                return carry

            lax.fori_loop(0, n, copy_granule, 0)
            total = total + n * GR
        tk_ref[pl.ds(pl.multiple_of(total, GR), PACK_FIRST), :] = jnp.full((PACK_FIRST, 1), -1, I32)

        def product(r0, rows):
            lane = lax.broadcasted_iota(I32, (rows, ROW_TILE), 1) + t * ROW_TILE
            hit = tk_ref[pl.ds(r0, rows), :] == lane
            return _dot(jnp.where(hit, 1.0, 0.0).T.astype(BF16), pk_ref[pl.ds(r0, rows), :])

        acc_ref[...] += product(0, PACK_FIRST)

        def chunk_product(kc, carry):
            acc_ref[...] += product(pl.multiple_of(PACK_FIRST + kc * KC, KC), KC)
            return carry

        extra = jnp.maximum(total - PACK_FIRST, 0)
        lax.fori_loop(0, _div_pow2(extra + (KC - 1), KC), chunk_product, 0)

    def add_context():
        ncol = yc_ref.shape[1]
        lane = lax.broadcasted_iota(I32, (ROW_TILE, ncol), 1)
        for e in range(N_EXPERTS):
            pe = pos[:, e:e + 1]
            tgt = jnp.where(pe >= 0, pe + b * cap_c, -1)
            acc_ref[...] += _dot(jnp.where(tgt == lane, 1.0, 0.0).astype(BF16), yc_ref[e])

    if cap_c:
        pl.when(i == 0)(add_context)
        pl.when(i > 0)(add_latent)
    else:
        add_latent()
    x2 = x_ref[0] + gate * acc_ref[...]
    if final:
        var = jnp.mean(x2 * x2, axis=-1, keepdims=True)
        x2 = x2 * lax.rsqrt(var + EPS) * fg_ref[...]
    o_ref[0] = x2


def _combine(cnt, x1, pos_t, yl, il, yc, mod_l, final_g, cap_l, cap_c, final):
    B, S, _ = x1.shape
    E = N_EXPERTS
    rmap = lambda b, i, cnt_ref: (b, i, 0)
    const2 = lambda b, i, cnt_ref: (0, 0)
    kern = functools.partial(_combine_kernel, cap_l=cap_l, cap_c=cap_c, final=final)
    per_sample = lambda b, i, cnt_ref: (0, b, 0, 0)
    y_specs = [pl.BlockSpec((E, 1, cap_l, D_MODEL), per_sample, pipeline_mode=pl.Buffered(1)),
               pl.BlockSpec((E, 1, cap_l, 1), per_sample, pipeline_mode=pl.Buffered(1))]
    y_args = [yl, il]
    pack_rows = E * (ROW_TILE + 2 * PACK_GRANULE) + PACK_FIRST
    if cap_c:
        y_specs.append(pl.BlockSpec(yc.shape, lambda b, i, cnt_ref: (0, 0, 0),
                                    pipeline_mode=pl.Buffered(1)))
        y_args.append(yc)
    grid_spec = pltpu.PrefetchScalarGridSpec(
        num_scalar_prefetch=1,
        grid=(B, S // ROW_TILE),
        in_specs=[pl.BlockSpec((1, ROW_TILE, D_MODEL), rmap),
                  pl.BlockSpec((1, ROW_TILE, E), rmap)] + y_specs + [
                  pl.BlockSpec((8, N_MOD * D_MODEL), const2),
                  pl.BlockSpec((1, D_MODEL), const2)],
        out_specs=pl.BlockSpec((1, ROW_TILE, D_MODEL), rmap),
        scratch_shapes=[pltpu.VMEM((ROW_TILE, D_MODEL), F32),
                        pltpu.VMEM((pack_rows, D_MODEL), BF16),
                        pltpu.VMEM((pack_rows, 1), I32)],
    )
    return pl.pallas_call(
        kern,
        grid_spec=grid_spec,
        out_shape=jax.ShapeDtypeStruct((B, S, D_MODEL), F32),
        name="combine",
        compiler_params=_cparams(("arbitrary", "arbitrary")),
    )(cnt, x1, pos_t, *y_args, mod_l, final_g)


def _rope_tables(n_lat, ctx_len):
    t = np.arange(n_lat)
    pos = np.stack([t // GRID_W, t % GRID_W], axis=-1).astype(np.float32)
    inv = (ROPE_BASE ** (-np.arange(ROPE_FREQS, dtype=np.float32) / ROPE_FREQS)).astype(np.float32)
    ang = pos[:, :, None] * inv
    cos = np.cos(ang).astype(np.float32)
    sin = np.sin(ang).astype(np.float32)
    cos64 = np.concatenate([cos[:, 0], cos[:, 0], cos[:, 1], cos[:, 1]], axis=-1)
    sin64 = np.concatenate([-sin[:, 0], sin[:, 0], -sin[:, 1], sin[:, 1]], axis=-1)
    cos_t = np.concatenate([np.ones((ctx_len, 64), np.float32), cos64], axis=0)
    sin_t = np.concatenate([np.zeros((ctx_len, 64), np.float32), sin64], axis=0)
    return (jnp.asarray(np.tile(cos_t, (1, 2))), jnp.asarray(np.tile(sin_t, (1, 2))))


def kernel(x, c, ctx, c_ctx, ada_w, ada_b, norm1_g, norm2_g, w_in, attn_sink, hgrn_lb,
           hgrn_norm_g, conv_w, w_o, router_w, exp_w_gate, exp_w_up, exp_w_down, final_norm_g):
    B, T, D = x.shape
    L = ctx.shape[1]
    depth = ada_w.shape[0]
    assert D == D_MODEL and L == ROW_TILE and T % ROW_TILE == 0 and B <= 4
    S = L + T
    cap_l = EC_CAPACITY * T // N_EXPERTS
    cap_c = EC_CAPACITY * L // N_EXPERTS
    assert cap_l % 16 == 0 and cap_c % 16 == 0 and (B * cap_c) % 16 == 0

    cos_t, sin_t = _rope_tables(T, L)
    gamma = jax.nn.softmax(hgrn_lb.astype(F32), axis=0)
    lb_all = jnp.cumsum(gamma, axis=0) - gamma[0]
    cvec = jnp.concatenate([c, jnp.zeros((4 - B, D), F32), c_ctx[None],
                            jnp.zeros((3, D), F32)], axis=0)
    mod = _modulation(cvec, ada_w, ada_b)
    xs = (ctx, x)

    def interleave_heads(w, axis):
        shp = w.shape
        w = w.reshape(shp[:axis] + (N_KV, GQA_GROUP, HEAD_DIM) + shp[axis + 1:])
        return jnp.swapaxes(w, axis, axis + 1).reshape(shp)

    for l in range(depth):
        last = l == depth - 1
        w_in_l = w_in[l].astype(BF16)
        w_in_l = jnp.concatenate([w_in_l[:, :1024], interleave_heads(w_in_l[:, 1024:1536], 1),
                                  w_in_l[:, 1536:]], axis=1)
        w_o_l = w_o[l].astype(BF16)
        w_o_l = jnp.concatenate([interleave_heads(w_o_l[:ATT_W], 0), w_o_l[ATT_W:]], axis=0)
        q, kv, hg, hqg, cv = _in_projection(xs, mod[l], norm1_g[l][None], w_in_l, cos_t, sin_t)
        att = _attention(q, kv, attn_sink[l], L, skip_ctx=last)
        o2 = _hgrn(hg, hqg, lb_all[l], L)
        x1, h2, aff = _mix_out(xs, att, o2, hqg, cv, mod[l], hgrn_norm_g[l][None], conv_w[l],
                               w_o_l, norm2_g[l][None], router_w[l].T, skip_ctx=last)
        lm, cc = (0, 0) if last else (L, cap_c)
        segments = ((lm, T, cap_l),) if last else ((0, L, cap_c), (L, T, cap_l))
        posm, cnt = _topk_positions(aff, segments)
        gathered = _gather_rows(cnt, posm, aff, h2, lm, cap_l, cc)
        il = gathered[2]
        x_sets = [a.reshape(N_EXPERTS, -1, D) for a in gathered[0::3]]
        g_sets = [a.reshape(N_EXPERTS, -1, 1) for a in gathered[1::3]]
        ys = _expert_mlp(x_sets, g_sets, exp_w_gate, exp_w_up, exp_w_down, l, EXPERT_ROWS)
        yl = ys[0].reshape(N_EXPERTS, B, cap_l, D)
        xs = _combine(cnt, x1, jnp.swapaxes(posm, 1, 2), yl, il, None if last else ys[1], mod[l],
                      final_norm_g[None], cap_l, cc, final=last)
    return xs
```

```python
import functools

import numpy as np
import jax
import jax.numpy as jnp
from jax import lax
from jax.experimental import pallas as pl
from jax.experimental.pallas import tpu as pltpu

F32 = jnp.float32
BF16 = jnp.bfloat16
I32 = jnp.int32
HIGHEST = lax.Precision.HIGHEST

D_MODEL = 1024
GRID_W = 64
EPS = 1e-6
LB_FLOOR = 1e-30
N_MOD = 6
ATT_W = 512
HG_W = 256
CV_W = 256
HEAD_DIM = 64
N_Q = 8
N_KV = 2
GQA_GROUP = 4
KV_W = 128
ROPE_BASE = 10000.0
ROPE_FREQS = 16
HG_HEADS = 4
N_EXPERTS = 16
EC_CAPACITY = 2
D_EXPERT = 2048
IN_COLS = 2816

ROW_TILE = 256
ATT_BLOCK = 128
HG_CHUNK = 128
HG_LEVELS = 7
HG_BATCH = 2
HG_LOCAL = 64
HG_LOCAL_MAX_LOG = 115.0
SLOT_BLOCK = 256
PACK_GRANULE = 16
PACK_FIRST = 1024
PACK_CHUNK = 512
FF_TILE = 512
EXPERT_ROWS = 512
MOD_TILE = 1536
VMEM_LIMIT = 56 * 1024 * 1024

NEG_BIG = -1e30
LOG2E = 1.4426950408889634


def _cparams(sem):
    return pltpu.CompilerParams(dimension_semantics=sem, vmem_limit_bytes=VMEM_LIMIT)


def _dot(a, b):
    return jnp.dot(a, b, preferred_element_type=F32)


def _dot_nt(a, b):
    return lax.dot_general(a, b, (((1,), (1,)), ((), ())), preferred_element_type=F32)


def _mod_kernel(a_ref, w_ref, b_ref, o_ref):
    a = a_ref[...]
    a = a * jax.nn.sigmoid(a)
    o_ref[0] = jnp.dot(a, w_ref[0], precision=HIGHEST, preferred_element_type=F32) + b_ref[0]


def _modulation(cvec, ada_w, ada_b):
    depth = ada_w.shape[0]
    ncol = ada_w.shape[2]
    return pl.pallas_call(
        _mod_kernel,
        grid=(depth, ncol // MOD_TILE),
        in_specs=[
            pl.BlockSpec((8, D_MODEL), lambda l, j: (0, 0)),
            pl.BlockSpec((1, D_MODEL, MOD_TILE), lambda l, j: (l, 0, j)),
            pl.BlockSpec((1, 1, MOD_TILE), lambda l, j: (l, 0, j)),
        ],
        out_specs=pl.BlockSpec((1, 8, MOD_TILE), lambda l, j: (l, 0, j)),
        out_shape=jax.ShapeDtypeStruct((depth, 8, ncol), F32),
        name="modulation",
        compiler_params=_cparams(("arbitrary", "arbitrary")),
    )(cvec, ada_w, ada_b.reshape(depth, 1, ncol))


def _swap_halves(x):
    n = x.shape[-1]
    lane = lax.broadcasted_iota(I32, x.shape, x.ndim - 1)
    up = pltpu.roll(x, n - ROPE_FREQS, x.ndim - 1)
    dn = pltpu.roll(x, ROPE_FREQS, x.ndim - 1)
    return jnp.where((lane % (2 * ROPE_FREQS)) < ROPE_FREQS, up, dn)


def _rmsnorm_mod(x, g, shift, scale):
    var = jnp.mean(x * x, axis=-1, keepdims=True)
    y = x * lax.rsqrt(var + EPS) * g
    return y * (1.0 + scale) + shift


def _row_sources(src):
    if isinstance(src, tuple):
        ctx, lat = src
        return ctx, lat, 1, ctx.shape[1] + lat.shape[1]
    return src, src, 0, src.shape[1]


def _inproj_kernel(xc_ref, x_ref, mod_ref, g_ref, w_ref, cos_ref, sin_ref,
                   q_ref, kv_ref, hg_ref, hqg_ref, cv_ref):
    b = pl.program_id(0)
    i = pl.program_id(1)
    row = jnp.where(i == 0, 4, b)
    shift = mod_ref[pl.ds(row, 1), 0:D_MODEL]
    scale = mod_ref[pl.ds(row, 1), D_MODEL:2 * D_MODEL]
    xin = jnp.where(i == 0, xc_ref[0], x_ref[0])
    h = _rmsnorm_mod(xin, g_ref[...], shift, scale)
    p = _dot(h.astype(BF16), w_ref[...])
    cos2 = cos_ref[...]
    sin2 = sin_ref[...]
    k = p[:, 0:KV_W]
    k = k * cos2 + _swap_halves(k) * sin2
    kv_ref[0, :, 0:KV_W] = k.astype(BF16)
    kv_ref[0, :, KV_W:2 * KV_W] = p[:, KV_W:2 * KV_W].astype(BF16)
    hg_ref[0] = p[:, 256:1024]
    q = p[:, 1024:1536]
    cos8 = jnp.concatenate([cos2] * 4, axis=1)
    sin8 = jnp.concatenate([sin2] * 4, axis=1)
    q = (q * cos8 + _swap_halves(q) * sin8) * (HEAD_DIM ** -0.5 * LOG2E)
    q_ref[0] = q.astype(BF16)
    hqg_ref[0] = p[:, 1536:2048]
    cv_ref[0] = p[:, 2048:2816]


def _in_projection(src, mod_l, g, w_bf16, cos_t, sin_t):
    xc, xl, off, S = _row_sources(src)
    B = xc.shape[0]
    nt = S // ROW_TILE
    row_map = lambda b, i: (b, i, 0)
    const2 = lambda b, i: (0, 0)
    return pl.pallas_call(
        _inproj_kernel,
        grid=(B, nt),
        in_specs=[
            pl.BlockSpec((1, ROW_TILE, D_MODEL), lambda b, i: (b, 0, 0)),
            pl.BlockSpec((1, ROW_TILE, D_MODEL), lambda b, i: (b, jnp.maximum(i - off, 0), 0)),
            pl.BlockSpec((8, N_MOD * D_MODEL), const2),
            pl.BlockSpec((1, D_MODEL), const2),
            pl.BlockSpec((D_MODEL, IN_COLS), const2),
            pl.BlockSpec((ROW_TILE, 2 * HEAD_DIM), lambda b, i: (i, 0)),
            pl.BlockSpec((ROW_TILE, 2 * HEAD_DIM), lambda b, i: (i, 0)),
        ],
        out_specs=[
            pl.BlockSpec((1, ROW_TILE, ATT_W), row_map),
            pl.BlockSpec((1, ROW_TILE, 2 * KV_W), row_map),
            pl.BlockSpec((1, ROW_TILE, 3 * HG_W), row_map),
            pl.BlockSpec((1, ROW_TILE, 2 * HG_W), row_map),
            pl.BlockSpec((1, ROW_TILE, 3 * CV_W), row_map),
        ],
        out_shape=[
            jax.ShapeDtypeStruct((B, S, ATT_W), BF16),
            jax.ShapeDtypeStruct((B, S, 2 * KV_W), BF16),
            jax.ShapeDtypeStruct((B, S, 3 * HG_W), F32),
            jax.ShapeDtypeStruct((B, S, 2 * HG_W), F32),
            jax.ShapeDtypeStruct((B, S, 3 * CV_W), F32),
        ],
        name="in_projection",
        compiler_params=_cparams(("arbitrary", "arbitrary")),
    )(xc, xl, mod_l, g, w_bf16, cos_t, sin_t)


def _attn_kernel(sink_ref, q_ref, kp_ref, kc_ref, kn_ref, kx_ref, o_ref, *, blk0, nblk, nctx):
    n = pl.program_id(1) + blk0
    is_lat = n >= nctx
    has_prev = n > nctx
    has_next = n < nblk - 1
    W = ATT_BLOCK
    q = q_ref[0]
    qrows = jnp.concatenate([q[:, g * W:(g + 1) * W] for g in range(GQA_GROUP)], axis=0)
    kv_all = jnp.concatenate([kp_ref[0], kc_ref[0], kn_ref[0], kx_ref[0]], axis=0)
    nkeys = kv_all.shape[0]
    k_all = kv_all[:, 0:KV_W]
    v_ext = jnp.concatenate([kv_all[:, KV_W:2 * KV_W], jnp.ones((nkeys, KV_W), BF16)], axis=1)
    rows = GQA_GROUP * W
    ri = lax.broadcasted_iota(I32, (rows, W), 0) % W
    cj = lax.broadcasted_iota(I32, (rows, W), 1)
    m_prev = (cj >= ri) & has_prev
    m_cur = jnp.broadcast_to(is_lat, (rows, W))
    m_next = (cj <= ri) & (has_next & is_lat)
    grp = lax.broadcasted_iota(I32, (rows, 1), 0) // W
    low = cj < HEAD_DIM
    outs = []
    for h in range(N_KV):
        qh = jnp.where(low if h == 0 else jnp.logical_not(low), qrows, jnp.zeros_like(qrows))
        sink = jnp.zeros((rows, 1), F32)
        for g in range(GQA_GROUP):
            sink = jnp.where(grp == g, sink_ref[h * GQA_GROUP + g] * LOG2E, sink)
        s = _dot_nt(qh, k_all)
        segs = [jnp.where(m_prev, s[:, 0:W], NEG_BIG),
                jnp.where(m_cur, s[:, W:2 * W], NEG_BIG),
                jnp.where(m_next, s[:, 2 * W:3 * W], NEG_BIG)]
        segs += [s[:, c:c + W] for c in range(3 * W, nkeys, W)]
        mx = segs[0]
        for sg in segs[1:]:
            mx = jnp.maximum(mx, sg)
        m = jnp.maximum(jnp.max(mx, axis=1, keepdims=True), sink)
        p = jnp.concatenate([jnp.exp2(sg - m).astype(BF16) for sg in segs], axis=1)
        oe = _dot(p, v_ext)
        den = oe[:, KV_W:2 * KV_W] + jnp.exp2(sink - m)
        outs.append(oe[:, 0:KV_W] / den)
    o = jnp.where(low, outs[0], outs[1])
    for g in range(GQA_GROUP):
        o_ref[0, :, g * W:(g + 1) * W] = o[g * W:(g + 1) * W].astype(BF16)


def _attention(q, kv, sink, ctx_len, skip_ctx):
    B, S, _ = q.shape
    nblk = S // ATT_BLOCK
    nctx = ctx_len // ATT_BLOCK
    blk0 = nctx if skip_ctx else 0
    blk = lambda f: (lambda b, n: (b, f(n + blk0), 0))
    kern = functools.partial(_attn_kernel, blk0=blk0, nblk=nblk, nctx=nctx)
    return pl.pallas_call(
        kern,
        grid=(B, nblk - blk0),
        in_specs=[
            pl.BlockSpec(memory_space=pltpu.SMEM),
            pl.BlockSpec((1, ATT_BLOCK, ATT_W), blk(lambda n: n)),
            pl.BlockSpec((1, ATT_BLOCK, 2 * KV_W), blk(lambda n: jnp.maximum(n - 1, 0))),
            pl.BlockSpec((1, ATT_BLOCK, 2 * KV_W), blk(lambda n: n)),
            pl.BlockSpec((1, ATT_BLOCK, 2 * KV_W), blk(lambda n: jnp.minimum(n + 1, nblk - 1))),
            pl.BlockSpec((1, ctx_len, 2 * KV_W), lambda b, n: (b, 0, 0)),
        ],
        out_specs=pl.BlockSpec((1, ATT_BLOCK, ATT_W), lambda b, n: (b, n, 0)),
        out_shape=jax.ShapeDtypeStruct((B, S - blk0 * ATT_BLOCK, ATT_W), BF16),
        name="attention",
        compiler_params=_cparams(("arbitrary", "arbitrary")),
    )(sink, q, kv, kv, kv, kv)


def _hgrn_constants():
    C = HG_CHUNK
    t = np.arange(C)[:, None]
    r = np.arange(C)[None, :]
    tri = np.stack([r <= t, r >= t]).astype(np.float32)
    x = t ^ r
    lvl = np.where(x > 0, np.floor(np.log2(np.maximum(x, 1))).astype(np.int32), HG_LEVELS)
    lvl_f = np.where(t >= r, lvl, -1).astype(np.int32)
    lvl_b = np.where(t <= r, lvl, -1).astype(np.int32)
    lvl2 = np.stack([np.tile(lvl_f, (1, HG_HEADS)), np.tile(lvl_b, (1, HG_HEADS))])
    return tri, lvl2


def _span_row(x, span, row):
    C = x.shape[0]
    if span >= 8:
        x3 = x.reshape(C // span, span, x.shape[1])
        return jnp.broadcast_to(x3[:, row:row + 1, :], x3.shape).reshape(x.shape)
    pos = lax.broadcasted_iota(I32, x.shape, 0) % span
    out = x
    for p in range(span):
        if p != row:
            out = jnp.where(pos == p, pltpu.roll(x, (p - row) % C, 0), out)
    return out


def _hgrn_prepare(v, z, qr, lb, tri, backward):
    C = HG_CHUNK
    logf = jnp.log(jnp.maximum(lb, LB_FLOOR) + (1.0 - lb) * jax.nn.sigmoid(z)) * LOG2E
    k = (1.0 - lb) * jax.nn.sigmoid(-z)
    q = qr * jax.nn.sigmoid(qr)
    hi = logf.astype(BF16)
    lo = (logf - hi.astype(F32)).astype(BF16)
    cs = _dot(tri, jnp.concatenate([hi, lo], axis=1))
    lam = cs[:, 0:HG_W] + cs[:, HG_W:2 * HG_W]
    tot = lam[0:1] if backward else lam[C - 1:C]
    local = lam - _span_row(lam, HG_LOCAL, HG_LOCAL // 2)
    return dict(v=v, q=q, k=k, lam=lam, tot=tot, local=local)


def _hgrn_finish(g, head_masks, same_head, lvl, st_ref, sidx, backward, fast):
    C = HG_CHUNK
    q, k, v, lam, tot = g["q"], g["k"], g["v"], g["lam"], g["tot"]
    zero = jnp.zeros((C, HG_W), BF16)

    def per_head_rows(x):
        return jnp.concatenate([jnp.where(hm, x, zero) for hm in head_masks], axis=0)

    qb = q.astype(BF16)
    kb = k.astype(BF16)
    if fast:
        n_local = HG_LOCAL.bit_length() - 1
        a = jnp.where((lvl == HG_LEVELS) | ((lvl >= 0) & (lvl < n_local)),
                      _dot_nt((q * jnp.exp2(g["local"])).astype(BF16),
                              per_head_rows((k * jnp.exp2(-g["local"])).astype(BF16))), 0.0)
        levels = range(n_local, HG_LEVELS)
    else:
        a = jnp.where(lvl == HG_LEVELS, _dot_nt(qb, per_head_rows(kb)), 0.0)
        levels = range(HG_LEVELS)
    for l in levels:
        m = 1 << l
        ref = _span_row(lam, 2 * m, m if backward else m - 1)
        fac = jnp.exp2(-jnp.abs(lam - ref)).astype(BF16)
        a = jnp.where(lvl == l, _dot_nt(qb * fac, per_head_rows(kb * fac)), a)
    st = st_ref[sidx]
    q_in = (q * jnp.exp2(lam)).astype(BF16)
    o = _dot(a.astype(BF16), per_head_rows(v.astype(BF16))) + _dot_nt(q_in, st.astype(BF16))
    k_out = (k * jnp.exp2(tot - lam)).astype(BF16)
    upd = _dot(v.T.astype(BF16), k_out)
    st_ref[sidx] = st * jnp.exp2(tot) + jnp.where(same_head, upd, 0.0)
    return o


def _hgrn_kernel(vf_ref, zf_ref, qf_ref, vb_ref, zb_ref, qb_ref, lb_ref, tri_ref, lvl_ref,
                 of_ref, ob_ref, st_ref):
    j = pl.program_id(1)

    @pl.when(j == 0)
    def _():
        st_ref[...] = jnp.zeros_like(st_ref)

    head_id = (lax.broadcasted_iota(I32, (HG_CHUNK, HG_W), 1) // HEAD_DIM).astype(F32).astype(BF16)
    head_masks = [head_id == float(h) for h in range(HG_HEADS)]
    same_head = (lax.broadcasted_iota(I32, (HG_W, HG_W), 0) // HEAD_DIM
                 == lax.broadcasted_iota(I32, (HG_W, HG_W), 1) // HEAD_DIM)
    streams = []
    for d, (v_ref, z_ref, q_ref, o_ref) in enumerate(
            ((vf_ref, zf_ref, qf_ref, of_ref), (vb_ref, zb_ref, qb_ref, ob_ref))):
        for s in range(v_ref.shape[0]):
            g = _hgrn_prepare(v_ref[s], z_ref[s], q_ref[s], lb_ref[d], tri_ref[d], d == 1)
            streams.append((d, s, o_ref, g))
    worst = jnp.abs(streams[0][3]["local"])
    for _, _, _, g in streams[1:]:
        worst = jnp.maximum(worst, jnp.abs(g["local"]))
    can_use_local = jnp.max(worst) <= HG_LOCAL_MAX_LOG

    def run(fast):
        for d, s, o_ref, g in streams:
            o_ref[s] = _hgrn_finish(g, head_masks, same_head, lvl_ref[d], st_ref, 2 * s + d,
                                    backward=(d == 1), fast=fast)

    pl.when(can_use_local)(functools.partial(run, True))
    pl.when(jnp.logical_not(can_use_local))(functools.partial(run, False))


def _hgrn(hg, hqg, lb_l, ctx_len):
    B, S, _ = hg.shape
    nc = S // HG_CHUNK
    nctx = ctx_len // HG_CHUNK
    tri_np, lvl_np = _hgrn_constants()
    tri = jnp.asarray(tri_np, BF16)
    lvl = jnp.asarray(lvl_np, I32)

    def back(j):
        return jnp.where(j < nctx, nctx - 1 - j, nc - 1 - (j - nctx))

    bs = HG_BATCH if B % HG_BATCH == 0 else 1
    blk = (bs, HG_CHUNK, HG_W)
    fwd = lambda col: (lambda b, j: (b, j, col))
    bwd = lambda col: (lambda b, j: (b, back(j), col))
    const3 = lambda b, j: (0, 0, 0)
    return pl.pallas_call(
        _hgrn_kernel,
        grid=(B // bs, nc),
        in_specs=[
            pl.BlockSpec(blk, fwd(0)), pl.BlockSpec(blk, fwd(1)), pl.BlockSpec(blk, fwd(0)),
            pl.BlockSpec(blk, bwd(0)), pl.BlockSpec(blk, bwd(2)), pl.BlockSpec(blk, bwd(0)),
            pl.BlockSpec((2, 1, HG_W), const3),
            pl.BlockSpec((2, HG_CHUNK, HG_CHUNK), const3),
            pl.BlockSpec((2, HG_CHUNK, HG_HEADS * HG_CHUNK), const3),
        ],
        out_specs=[pl.BlockSpec(blk, fwd(0)), pl.BlockSpec(blk, bwd(0))],
        out_shape=[jax.ShapeDtypeStruct((B, S, HG_W), F32)] * 2,
        scratch_shapes=[pltpu.VMEM((2 * bs, HG_W, HG_W), F32)],
        name="hgrn_scan",
        compiler_params=_cparams(("arbitrary", "arbitrary")),
    )(hg, hg, hqg, hg, hg, hqg, lb_l.reshape(2, 1, HG_W), tri, lvl)


def _mixout_kernel(xc_ref, x_ref, att_ref, of_ref, ob_ref, g_ref, cv_ref, cprev_ref, cnext_ref,
                   mod_ref, gain_ref, cw_ref, wo_ref, n2_ref, wr_ref, ones_ref,
                   x1_ref, h2_ref, aff_ref, *, tile0, ntile):
    b = pl.program_id(0)
    i = pl.program_id(1) + tile0
    row = jnp.where(i == 0, 4, b)
    R = ROW_TILE

    def modv(c):
        return mod_ref[pl.ds(row, 1), c * D_MODEL:(c + 1) * D_MODEL]

    o = of_ref[0] + ob_ref[0]
    sq = o * o
    sq_hi = sq.astype(BF16)
    sq_lo = (sq - sq_hi.astype(F32)).astype(BF16)
    ms = _dot(sq_hi, ones_ref[...]) + _dot(sq_lo, ones_ref[...])
    g = g_ref[0]
    hg = o * lax.rsqrt(ms * (1.0 / HEAD_DIM) + EPS) * gain_ref[...] * (g * jax.nn.sigmoid(g))
    cv = cv_ref[0]
    u = cv[:, CV_W:2 * CV_W] * cv[:, 2 * CV_W:3 * CV_W]
    up = cprev_ref[0]
    un = cnext_ref[0]
    u_prev_row = up[7:8, CV_W:2 * CV_W] * up[7:8, 2 * CV_W:3 * CV_W]
    u_next_row = un[0:1, CV_W:2 * CV_W] * un[0:1, 2 * CV_W:3 * CV_W]
    u_prev_row = jnp.where(i <= 1, 0.0, u_prev_row)
    u_next_row = jnp.where((i == 0) | (i == ntile - 1), 0.0, u_next_row)
    ridx = lax.broadcasted_iota(I32, (R, CV_W), 0)
    u_m1 = jnp.where(ridx == 0, u_prev_row, pltpu.roll(u, 1, 0))
    u_p1 = jnp.where(ridx == R - 1, u_next_row, pltpu.roll(u, R - 1, 0))
    cw = cw_ref[...]
    conv = cv[:, 0:CV_W] * (u_m1 * cw[0:1] + u * cw[1:2] + u_p1 * cw[2:3])
    mix = (_dot(att_ref[0], wo_ref[0:ATT_W])
           + _dot(hg.astype(BF16), wo_ref[ATT_W:ATT_W + HG_W])
           + _dot(conv.astype(BF16), wo_ref[ATT_W + HG_W:D_MODEL]))
    x1 = jnp.where(i == 0, xc_ref[0], x_ref[0]) + modv(2) * mix
    x1_ref[0] = x1
    h2 = _rmsnorm_mod(x1, n2_ref[...], modv(3), modv(4))
    h_hi = h2.astype(BF16)
    h2_ref[0] = h_hi
    h_lo = (h2 - h_hi.astype(F32)).astype(BF16)
    wr = wr_ref[...]
    w_hi = wr.astype(BF16)
    w_lo = (wr - w_hi.astype(F32)).astype(BF16)
    r1 = _dot_nt(jnp.concatenate([w_hi, w_lo], axis=0), h_hi)
    logits = r1[0:N_EXPERTS] + r1[N_EXPERTS:2 * N_EXPERTS] + _dot_nt(w_hi, h_lo)
    e = jnp.exp(logits - jnp.max(logits, axis=0, keepdims=True))
    aff_ref[0] = e / jnp.sum(e, axis=0, keepdims=True)


def _mix_out(src, att, o2, hqg, cv, mod_l, gain, conv_w, wo_bf16, n2g, wr_t, skip_ctx):
    xc, xl, off, S = _row_sources(src)
    B = xc.shape[0]
    ntile = S // ROW_TILE
    tile0 = 1 if skip_ctx else 0
    sub = ROW_TILE // 8
    nsub = S // 8
    rmap = lambda b, i: (b, i + tile0, 0)
    omap = lambda b, i: (b, i, 0)
    s_out = S - tile0 * ROW_TILE
    const2 = lambda b, i: (0, 0)
    ones = jnp.asarray(np.kron(np.eye(HG_HEADS), np.ones((HEAD_DIM, HEAD_DIM))), BF16)
    kern = functools.partial(_mixout_kernel, tile0=tile0, ntile=ntile)
    return pl.pallas_call(
        kern,
        grid=(B, ntile - tile0),
        in_specs=[
            pl.BlockSpec((1, ROW_TILE, D_MODEL), lambda b, i: (b, 0, 0)),
            pl.BlockSpec((1, ROW_TILE, D_MODEL),
                         lambda b, i: (b, jnp.maximum(i + tile0 - off, 0), 0)),
            pl.BlockSpec((1, ROW_TILE, ATT_W), omap),
            pl.BlockSpec((1, ROW_TILE, HG_W), rmap),
            pl.BlockSpec((1, ROW_TILE, HG_W), rmap),
            pl.BlockSpec((1, ROW_TILE, HG_W), lambda b, i: (b, i + tile0, 1)),
            pl.BlockSpec((1, ROW_TILE, 3 * CV_W), rmap),
            pl.BlockSpec((1, 8, 3 * CV_W),
                         lambda b, i: (b, jnp.maximum((i + tile0) * sub - 1, 0), 0)),
            pl.BlockSpec((1, 8, 3 * CV_W),
                         lambda b, i: (b, jnp.minimum((i + tile0 + 1) * sub, nsub - 1), 0)),
            pl.BlockSpec((8, N_MOD * D_MODEL), const2),
            pl.BlockSpec((1, HG_W), const2),
            pl.BlockSpec((3, CV_W), const2),
            pl.BlockSpec((D_MODEL, D_MODEL), const2),
            pl.BlockSpec((1, D_MODEL), const2),
            pl.BlockSpec((N_EXPERTS, D_MODEL), const2),
            pl.BlockSpec((HG_W, HG_W), const2),
        ],
        out_specs=[
            pl.BlockSpec((1, ROW_TILE, D_MODEL), omap),
            pl.BlockSpec((1, ROW_TILE, D_MODEL), omap),
            pl.BlockSpec((1, N_EXPERTS, ROW_TILE), lambda b, i: (b, 0, i)),
        ],
        out_shape=[
            jax.ShapeDtypeStruct((B, s_out, D_MODEL), F32),
            jax.ShapeDtypeStruct((B, s_out, D_MODEL), BF16),
            jax.ShapeDtypeStruct((B, N_EXPERTS, s_out), F32),
        ],
        name="mix_out",
        compiler_params=_cparams(("arbitrary", "arbitrary")),
    )(xc, xl, att, o2[0], o2[1], hqg, cv, cv, cv, mod_l, gain, conv_w, wo_bf16, n2g, wr_t, ones)


def _topk_kernel(aff_ref, tri_ref, pos_ref, cnt_ref, *, segments):
    tri = tri_ref[...]

    def excl_cumsum(mask, n):
        carry = jnp.zeros((N_EXPERTS, 1), F32)
        parts = []
        carries = []
        for c in range(n // 128):
            carries.append(carry)
            blk = jnp.where(mask[:, c * 128:(c + 1) * 128], 1.0, 0.0)
            parts.append(_dot(blk.astype(BF16), tri) + carry)
            carry = carry + jnp.sum(blk, axis=1, keepdims=True)
        return jnp.concatenate(parts, axis=1), carries + [carry]

    for (lo, n, k) in segments:
        a = aff_ref[0, :, lo:lo + n]

        def body(it, thr):
            cand = thr | jnp.left_shift(jnp.int32(1), 30 - it)
            cnt = jnp.sum(jnp.where(a >= pltpu.bitcast(cand, F32), 1.0, 0.0), axis=1, keepdims=True)
            return jnp.where(cnt >= k, cand, thr)

        thr = lax.fori_loop(0, 31, body, jnp.zeros((N_EXPERTS, 1), I32))
        above = a >= pltpu.bitcast(thr + 1, F32)
        tied = jnp.logical_and(a >= pltpu.bitcast(thr, F32), jnp.logical_not(above))
        n_above = jnp.sum(jnp.where(above, 1.0, 0.0), axis=1, keepdims=True)
        rank_tied, _ = excl_cumsum(tied, n)
        sel = above | (tied & (rank_tied < (k - n_above)))
        pos, counts = excl_cumsum(sel, n)
        pos_ref[0, :, lo:lo + n] = jnp.where(sel, pos.astype(I32), -1)
    step = ROW_TILE // 128
    lane = lax.broadcasted_iota(I32, (N_EXPERTS, 128), 1)
    cnt = jnp.zeros((N_EXPERTS, 128), F32)
    for c, col in enumerate(counts[0::step]):
        cnt = jnp.where(lane == c, col, cnt)
    cnt_ref[0] = cnt.astype(I32)


def _topk_positions(aff, segments):
    B, E, S = aff.shape
    assert segments[-1][1] // ROW_TILE + 1 <= 128
    tri = jnp.asarray(np.triu(np.ones((128, 128)), 1), BF16)
    kern = functools.partial(_topk_kernel, segments=segments)
    return pl.pallas_call(
        kern,
        grid=(B,),
        in_specs=[pl.BlockSpec((1, E, S), lambda b: (b, 0, 0)),
                  pl.BlockSpec((128, 128), lambda b: (0, 0))],
        out_specs=[pl.BlockSpec((1, E, S), lambda b: (b, 0, 0)),
                   pl.BlockSpec((1, E, 128), lambda b: (b, 0, 0))],
        out_shape=[jax.ShapeDtypeStruct((B, E, S), I32),
                   jax.ShapeDtypeStruct((B, E, 128), I32)],
        name="topk_positions",
        compiler_params=_cparams(("arbitrary",)),
    )(aff, tri)


def _slot_block(cap):
    sbs = min(SLOT_BLOCK, cap)
    assert sbs >= ROW_TILE or sbs == cap
    return sbs


def _div_pow2(x, d):
    assert d & (d - 1) == 0
    return lax.shift_right_logical(x, jnp.int32(d.bit_length() - 1))


def _gather_kernel(cnt_ref, pos_ref, aff_ref, h_ref, *refs, ctx_len, cap_l, cap_c):
    if cap_c:
        xl_ref, gl_ref, il_ref, xc_ref, gc_ref, acc_ref, gacc_ref, iacc_ref = refs
    else:
        xl_ref, gl_ref, il_ref, acc_ref, gacc_ref, iacc_ref = refs
    b = pl.program_id(0)
    e = pl.program_id(1)
    S = h_ref.shape[1]
    pos = pos_ref[0, pl.ds(e, 1), :]
    aff = aff_ref[0, pl.ds(e, 1), :]
    nchunk = (S - ctx_len) // ROW_TILE
    sbs = _slot_block(cap_l)
    acc_ref[...] = jnp.zeros_like(acc_ref)
    gacc_ref[...] = jnp.zeros_like(gacc_ref)
    iacc_ref[...] = jnp.zeros_like(iacc_ref)
    nsb = cap_l // sbs
    first = [jnp.minimum(_div_pow2(cnt_ref[b, e, c], sbs), nsb - 1) for c in range(nchunk)]
    slot_i32 = lax.broadcasted_iota(I32, (sbs, ROW_TILE), 0)
    slot_in_block = slot_i32.astype(F32).astype(BF16)
    one = jnp.ones((sbs, ROW_TILE), BF16)
    zero = jnp.zeros((sbs, ROW_TILE), BF16)

    def visit(c, sb):
        lo = ctx_len + c * ROW_TILE
        base = pl.multiple_of(sb * sbs, sbs)
        rel_i = pos[:, lo:lo + ROW_TILE] - base
        rel = rel_i.astype(F32)
        rel = jnp.where((rel >= 0.0) & (rel < float(sbs)), rel, -1.0).astype(BF16)
        acc_ref[pl.ds(base, sbs), :] += _dot(jnp.where(rel == slot_in_block, one, zero),
                                             h_ref[0, lo:lo + ROW_TILE, :])
        hit = rel_i == slot_i32
        gacc_ref[pl.ds(base, sbs), :] += jnp.sum(
            jnp.where(hit, aff[:, lo:lo + ROW_TILE], 0.0), axis=1, keepdims=True)
        tok = (lax.broadcasted_iota(I32, (1, ROW_TILE), 1) + c * ROW_TILE).astype(F32)
        iacc_ref[pl.ds(base, sbs), :] += jnp.sum(jnp.where(hit, tok, 0.0), axis=1, keepdims=True)

    for c in range(nchunk):
        visit(c, first[c])
    for c in range(nchunk):
        pl.when(cnt_ref[b, e, c + 1] > (first[c] + 1) * sbs)(
            functools.partial(visit, c, first[c] + 1))
    xl_ref[0, 0] = acc_ref[...].astype(BF16)
    gl_ref[0, 0] = gacc_ref[...]
    il_ref[0, 0] = iacc_ref[...].astype(I32)
    if cap_c:
        slot_c = lax.broadcasted_iota(I32, (cap_c, ctx_len), 0)
        hit = pos[:, 0:ctx_len] == slot_c
        xc_ref[0, 0] = _dot(jnp.where(hit, 1.0, 0.0).astype(BF16),
                            h_ref[0, 0:ctx_len, :]).astype(BF16)
        gc_ref[0, 0] = jnp.sum(jnp.where(hit, aff[:, 0:ctx_len], 0.0), axis=1, keepdims=True)


def _gather_rows(cnt, posm, aff, h2, ctx_len, cap_l, cap_c):
    B, E, S = posm.shape
    kern = functools.partial(_gather_kernel, ctx_len=ctx_len, cap_l=cap_l, cap_c=cap_c)
    omap = lambda b, e, cnt_ref: (e, b, 0, 0)
    caps = [cap_l] + ([cap_c] if cap_c else [])
    out_specs = []
    out_shape = []
    for cap in caps:
        out_specs += [pl.BlockSpec((1, 1, cap, D_MODEL), omap), pl.BlockSpec((1, 1, cap, 1), omap)]
        out_shape += [jax.ShapeDtypeStruct((E, B, cap, D_MODEL), BF16),
                      jax.ShapeDtypeStruct((E, B, cap, 1), F32)]
        if cap is caps[0]:
            out_specs.append(pl.BlockSpec((1, 1, cap, 1), omap))
            out_shape.append(jax.ShapeDtypeStruct((E, B, cap, 1), I32))
    grid_spec = pltpu.PrefetchScalarGridSpec(
        num_scalar_prefetch=1,
        grid=(B, E),
        in_specs=[pl.BlockSpec((1, E, S), lambda b, e, cnt_ref: (b, 0, 0)),
                  pl.BlockSpec((1, E, S), lambda b, e, cnt_ref: (b, 0, 0)),
                  pl.BlockSpec((1, S, D_MODEL), lambda b, e, cnt_ref: (b, 0, 0))],
        out_specs=out_specs,
        scratch_shapes=[pltpu.VMEM((cap_l, D_MODEL), F32), pltpu.VMEM((cap_l, 1), F32),
                        pltpu.VMEM((cap_l, 1), F32)],
    )
    return pl.pallas_call(
        kern,
        grid_spec=grid_spec,
        out_shape=out_shape,
        name="gather_rows",
        compiler_params=_cparams(("arbitrary", "arbitrary")),
    )(cnt, posm, aff, h2)


def _expert_kernel(*refs, row_chunk, n_sets):
    x_refs = refs[:n_sets]
    g_refs = refs[n_sets:2 * n_sets]
    wg_ref, wu_ref, wd_ref = refs[2 * n_sets:2 * n_sets + 3]
    y_refs = refs[2 * n_sets + 3:3 * n_sets + 3]
    acc_refs = refs[3 * n_sets + 3:]
    f = pl.program_id(1)
    nf = pl.num_programs(1)
    wg = wg_ref[0, 0].astype(BF16)
    wu = wu_ref[0, 0].astype(BF16)
    wd = wd_ref[0, 0].astype(BF16)

    @pl.when(f == 0)
    def _():
        for acc_ref in acc_refs:
            acc_ref[...] = jnp.zeros_like(acc_ref)

    for x_ref, acc_ref in zip(x_refs, acc_refs):
        rows = x_ref.shape[1]
        step = min(row_chunk, rows)
        for r in range(rows // step):
            rs = slice(r * step, (r + 1) * step)
            x = x_ref[0, rs, :]
            g = _dot(x, wg)
            u = _dot(x, wu)
            hid = (g * jax.nn.sigmoid(g) * u).astype(BF16)
            acc_ref[rs, :] += _dot(hid, wd)

    @pl.when(f == nf - 1)
    def _():
        for g_ref, y_ref, acc_ref in zip(g_refs, y_refs, acc_refs):
            y_ref[0] = (acc_ref[...] * g_ref[0]).astype(BF16)


def _expert_mlp(x_sets, g_sets, w_gate, w_up, w_down, layer, row_chunk):
    E = N_EXPERTS
    nf = D_EXPERT // FF_TILE
    n_sets = len(x_sets)
    kern = functools.partial(_expert_kernel, row_chunk=row_chunk, n_sets=n_sets)
    xspec = lambda r: pl.BlockSpec((1, r, D_MODEL), lambda e, f: (e, 0, 0))
    gspec = lambda r: pl.BlockSpec((1, r, 1), lambda e, f: (e, 0, 0))
    return pl.pallas_call(
        kern,
        grid=(E, nf),
        in_specs=[xspec(xs.shape[1]) for xs in x_sets] + [gspec(xs.shape[1]) for xs in x_sets] + [
            pl.BlockSpec((1, 1, D_MODEL, FF_TILE), lambda e, f: (layer, e, 0, f)),
            pl.BlockSpec((1, 1, D_MODEL, FF_TILE), lambda e, f: (layer, e, 0, f)),
            pl.BlockSpec((1, 1, FF_TILE, D_MODEL), lambda e, f: (layer, e, f, 0))],
        out_specs=[xspec(xs.shape[1]) for xs in x_sets],
        out_shape=[jax.ShapeDtypeStruct(xs.shape, BF16) for xs in x_sets],
        scratch_shapes=[pltpu.VMEM(xs.shape[1:], F32) for xs in x_sets],
        name="expert_mlp",
        compiler_params=_cparams(("arbitrary", "arbitrary")),
    )(*x_sets, *g_sets, w_gate, w_up, w_down)


def _combine_kernel(cnt_ref, x_ref, pos_ref, *refs, cap_l, cap_c, final):
    if cap_c:
        yl_ref, il_ref, yc_ref, mod_ref, fg_ref, o_ref, acc_ref, pk_ref, tk_ref = refs
    else:
        yl_ref, il_ref, mod_ref, fg_ref, o_ref, acc_ref, pk_ref, tk_ref = refs
    b = pl.program_id(0)
    i = pl.program_id(1)
    row = jnp.where(i == 0, 4, b) if cap_c else b
    gate = mod_ref[pl.ds(row, 1), 5 * D_MODEL:6 * D_MODEL]
    pos = pos_ref[0]
    acc_ref[...] = jnp.zeros_like(acc_ref)
    GR = PACK_GRANULE
    KC = PACK_CHUNK

    @pl.when((b == 0) & (i == 0))
    def _():
        pk_ref[...] = jnp.zeros_like(pk_ref)
        tk_ref[...] = jnp.full(tk_ref.shape, -1, I32)

    def add_latent():
        t = i - 1 if cap_c else i
        total = 0
        for e in range(N_EXPERTS):
            g0 = _div_pow2(cnt_ref[b, e, t], GR)
            n = _div_pow2(cnt_ref[b, e, t + 1] + (GR - 1), GR) - g0

            def copy_granule(k, carry, e=e, src0=g0 * GR, dst0=total):
                s = pl.multiple_of(src0 + k * GR, GR)
                d = pl.multiple_of(dst0 + k * GR, GR)
                pk_ref[pl.ds(d, GR), :] = yl_ref[e, 0, pl.ds(s, GR), :]
                tk_ref[pl.ds(d, GR), :] = il_ref[e, 0, pl.ds(s, GR), :]
                return carry

            lax.fori_loop(0, n, copy_granule, 0)
            total = total + n * GR
        tk_ref[pl.ds(pl.multiple_of(total, GR), PACK_FIRST), :] = jnp.full((PACK_FIRST, 1), -1, I32)

        def product(r0, rows):
            lane = lax.broadcasted_iota(I32, (rows, ROW_TILE), 1) + t * ROW_TILE
            hit = tk_ref[pl.ds(r0, rows), :] == lane
            return _dot(jnp.where(hit, 1.0, 0.0).T.astype(BF16), pk_ref[pl.ds(r0, rows), :])

        acc_ref[...] += product(0, PACK_FIRST)

        def chunk_product(kc, carry):
            acc_ref[...] += product(pl.multiple_of(PACK_FIRST + kc * KC, KC), KC)
            return carry

        extra = jnp.maximum(total - PACK_FIRST, 0)
        lax.fori_loop(0, _div_pow2(extra + (KC - 1), KC), chunk_product, 0)

    def add_context():
        ncol = yc_ref.shape[1]
        lane = lax.broadcasted_iota(I32, (ROW_TILE, ncol), 1)
        for e in range(N_EXPERTS):
            pe = pos[:, e:e + 1]
            tgt = jnp.where(pe >= 0, pe + b * cap_c, -1)
            acc_ref[...] += _dot(jnp.where(tgt == lane, 1.0, 0.0).astype(BF16), yc_ref[e])

    if cap_c:
        pl.when(i == 0)(add_context)
        pl.when(i > 0)(add_latent)
    else:
        add_latent()
    x2 = x_ref[0] + gate * acc_ref[...]
    if final:
        var = jnp.mean(x2 * x2, axis=-1, keepdims=True)
        x2 = x2 * lax.rsqrt(var + EPS) * fg_ref[...]
    o_ref[0] = x2


def _combine(cnt, x1, pos_t, yl, il, yc, mod_l, final_g, cap_l, cap_c, final):
    B, S, _ = x1.shape
    E = N_EXPERTS
    rmap = lambda b, i, cnt_ref: (b, i, 0)
    const2 = lambda b, i, cnt_ref: (0, 0)
    kern = functools.partial(_combine_kernel, cap_l=cap_l, cap_c=cap_c, final=final)
    per_sample = lambda b, i, cnt_ref: (0, b, 0, 0)
    y_specs = [pl.BlockSpec((E, 1, cap_l, D_MODEL), per_sample, pipeline_mode=pl.Buffered(1)),
               pl.BlockSpec((E, 1, cap_l, 1), per_sample, pipeline_mode=pl.Buffered(1))]
    y_args = [yl, il]
    pack_rows = E * (ROW_TILE + 2 * PACK_GRANULE) + PACK_FIRST
    if cap_c:
        y_specs.append(pl.BlockSpec(yc.shape, lambda b, i, cnt_ref: (0, 0, 0),
                                    pipeline_mode=pl.Buffered(1)))
        y_args.append(yc)
    grid_spec = pltpu.PrefetchScalarGridSpec(
        num_scalar_prefetch=1,
        grid=(B, S // ROW_TILE),
        in_specs=[pl.BlockSpec((1, ROW_TILE, D_MODEL), rmap),
                  pl.BlockSpec((1, ROW_TILE, E), rmap)] + y_specs + [
                  pl.BlockSpec((8, N_MOD * D_MODEL), const2),
                  pl.BlockSpec((1, D_MODEL), const2)],
        out_specs=pl.BlockSpec((1, ROW_TILE, D_MODEL), rmap),
        scratch_shapes=[pltpu.VMEM((ROW_TILE, D_MODEL), F32),
                        pltpu.VMEM((pack_rows, D_MODEL), BF16),
                        pltpu.VMEM((pack_rows, 1), I32)],
    )
    return pl.pallas_call(
        kern,
        grid_spec=grid_spec,
        out_shape=jax.ShapeDtypeStruct((B, S, D_MODEL), F32),
        name="combine",
        compiler_params=_cparams(("arbitrary", "arbitrary")),
    )(cnt, x1, pos_t, *y_args, mod_l, final_g)


def _rope_tables(n_lat, ctx_len):
    t = np.arange(n_lat)
    pos = np.stack([t // GRID_W, t % GRID_W], axis=-1).astype(np.float32)
    inv = (ROPE_BASE ** (-np.arange(ROPE_FREQS, dtype=np.float32) / ROPE_FREQS)).astype(np.float32)
    ang = pos[:, :, None] * inv
    cos = np.cos(ang).astype(np.float32)
    sin = np.sin(ang).astype(np.float32)
    cos64 = np.concatenate([cos[:, 0], cos[:, 0], cos[:, 1], cos[:, 1]], axis=-1)
    sin64 = np.concatenate([-sin[:, 0], sin[:, 0], -sin[:, 1], sin[:, 1]], axis=-1)
    cos_t = np.concatenate([np.ones((ctx_len, 64), np.float32), cos64], axis=0)
    sin_t = np.concatenate([np.zeros((ctx_len, 64), np.float32), sin64], axis=0)
    return (jnp.asarray(np.tile(cos_t, (1, 2))), jnp.asarray(np.tile(sin_t, (1, 2))))


def kernel(x, c, ctx, c_ctx, ada_w, ada_b, norm1_g, norm2_g, w_in, attn_sink, hgrn_lb,
           hgrn_norm_g, conv_w, w_o, router_w, exp_w_gate, exp_w_up, exp_w_down, final_norm_g):
    B, T, D = x.shape
    L = ctx.shape[1]
    depth = ada_w.shape[0]
    assert D == D_MODEL and L == ROW_TILE and T % ROW_TILE == 0 and B <= 4
    S = L + T
    cap_l = EC_CAPACITY * T // N_EXPERTS
    cap_c = EC_CAPACITY * L // N_EXPERTS
    assert cap_l % 16 == 0 and cap_c % 16 == 0 and (B * cap_c) % 16 == 0

    cos_t, sin_t = _rope_tables(T, L)
    gamma = jax.nn.softmax(hgrn_lb.astype(F32), axis=0)
    lb_all = jnp.cumsum(gamma, axis=0) - gamma[0]
    cvec = jnp.concatenate([c, jnp.zeros((4 - B, D), F32), c_ctx[None],
                            jnp.zeros((3, D), F32)], axis=0)
    mod = _modulation(cvec, ada_w, ada_b)
    xs = (ctx, x)

    def interleave_heads(w, axis):
        shp = w.shape
        w = w.reshape(shp[:axis] + (N_KV, GQA_GROUP, HEAD_DIM) + shp[axis + 1:])
        return jnp.swapaxes(w, axis, axis + 1).reshape(shp)

    for l in range(depth):
        last = l == depth - 1
        w_in_l = w_in[l].astype(BF16)
        w_in_l = jnp.concatenate([w_in_l[:, :1024], interleave_heads(w_in_l[:, 1024:1536], 1),
                                  w_in_l[:, 1536:]], axis=1)
        w_o_l = w_o[l].astype(BF16)
        w_o_l = jnp.concatenate([interleave_heads(w_o_l[:ATT_W], 0), w_o_l[ATT_W:]], axis=0)
        q, kv, hg, hqg, cv = _in_projection(xs, mod[l], norm1_g[l][None], w_in_l, cos_t, sin_t)
        att = _attention(q, kv, attn_sink[l], L, skip_ctx=last)
        o2 = _hgrn(hg, hqg, lb_all[l], L)
        x1, h2, aff = _mix_out(xs, att, o2, hqg, cv, mod[l], hgrn_norm_g[l][None], conv_w[l],
                               w_o_l, norm2_g[l][None], router_w[l].T, skip_ctx=last)
        lm, cc = (0, 0) if last else (L, cap_c)
        segments = ((lm, T, cap_l),) if last else ((0, L, cap_c), (L, T, cap_l))
        posm, cnt = _topk_positions(aff, segments)
        gathered = _gather_rows(cnt, posm, aff, h2, lm, cap_l, cc)
        il = gathered[2]
        x_sets = [a.reshape(N_EXPERTS, -1, D) for a in gathered[0::3]]
        g_sets = [a.reshape(N_EXPERTS, -1, 1) for a in gathered[1::3]]
        ys = _expert_mlp(x_sets, g_sets, exp_w_gate, exp_w_up, exp_w_down, l, EXPERT_ROWS)
        yl = ys[0].reshape(N_EXPERTS, B, cap_l, D)
        xs = _combine(cnt, x1, jnp.swapaxes(posm, 1, 2), yl, il, None if last else ys[1], mod[l],
                      final_norm_g[None], cap_l, cc, final=last)
    return xs
```

```python
import functools

import numpy as np
import jax
import jax.numpy as jnp
from jax import lax
from jax.experimental import pallas as pl
from jax.experimental.pallas import tpu as pltpu

F32 = jnp.float32
BF16 = jnp.bfloat16
I32 = jnp.int32
HIGHEST = lax.Precision.HIGHEST

D_MODEL = 1024
GRID_W = 64
EPS = 1e-6
LB_FLOOR = 1e-30
N_MOD = 6
ATT_W = 512
HG_W = 256
CV_W = 256
HEAD_DIM = 64
N_Q = 8
N_KV = 2
GQA_GROUP = 4
KV_W = 128
ROPE_BASE = 10000.0
ROPE_FREQS = 16
HG_HEADS = 4
N_EXPERTS = 16
EC_CAPACITY = 2
D_EXPERT = 2048
IN_COLS = 2816

ROW_TILE = 256
ATT_BLOCK = 128
HG_CHUNK = 128
HG_LEVELS = 7
HG_BATCH = 2
HG_LOCAL = 64
HG_LOCAL_MAX_LOG = 115.0
GATHER_GROUP = 8
GATHER_WINDOW = 64
PACK_GRANULE = 16
PACK_FIRST = 1024
PACK_CHUNK = 512
FF_TILE = 512
EXPERT_ROWS = 512
MOD_TILE = 1536
VMEM_LIMIT = 56 * 1024 * 1024

NEG_BIG = -1e30
LOG2E = 1.4426950408889634


def _cparams(sem):
    return pltpu.CompilerParams(dimension_semantics=sem, vmem_limit_bytes=VMEM_LIMIT)


def _dot(a, b):
    return jnp.dot(a, b, preferred_element_type=F32)


def _dot_nt(a, b):
    return lax.dot_general(a, b, (((1,), (1,)), ((), ())), preferred_element_type=F32)


def _mod_kernel(a_ref, w_ref, b_ref, o_ref):
    a = a_ref[...]
    a = a * jax.nn.sigmoid(a)
    o_ref[0] = jnp.dot(a, w_ref[0], precision=HIGHEST, preferred_element_type=F32) + b_ref[0]


def _modulation(cvec, ada_w, ada_b):
    depth = ada_w.shape[0]
    ncol = ada_w.shape[2]
    return pl.pallas_call(
        _mod_kernel,
        grid=(depth, ncol // MOD_TILE),
        in_specs=[
            pl.BlockSpec((8, D_MODEL), lambda l, j: (0, 0)),
            pl.BlockSpec((1, D_MODEL, MOD_TILE), lambda l, j: (l, 0, j)),
            pl.BlockSpec((1, 1, MOD_TILE), lambda l, j: (l, 0, j)),
        ],
        out_specs=pl.BlockSpec((1, 8, MOD_TILE), lambda l, j: (l, 0, j)),
        out_shape=jax.ShapeDtypeStruct((depth, 8, ncol), F32),
        name="modulation",
        compiler_params=_cparams(("arbitrary", "arbitrary")),
    )(cvec, ada_w, ada_b.reshape(depth, 1, ncol))


def _swap_halves(x):
    n = x.shape[-1]
    lane = lax.broadcasted_iota(I32, x.shape, x.ndim - 1)
    up = pltpu.roll(x, n - ROPE_FREQS, x.ndim - 1)
    dn = pltpu.roll(x, ROPE_FREQS, x.ndim - 1)
    return jnp.where((lane % (2 * ROPE_FREQS)) < ROPE_FREQS, up, dn)


def _rmsnorm_mod(x, g, shift, scale):
    var = jnp.mean(x * x, axis=-1, keepdims=True)
    y = x * lax.rsqrt(var + EPS) * g
    return y * (1.0 + scale) + shift


def _row_sources(src):
    if isinstance(src, tuple):
        ctx, lat = src
        return ctx, lat, 1, ctx.shape[1] + lat.shape[1]
    return src, src, 0, src.shape[1]


def _inproj_kernel(xc_ref, x_ref, mod_ref, g_ref, w_ref, cos_ref, sin_ref,
                   q_ref, kv_ref, hg_ref, hqg_ref, cv_ref):
    b = pl.program_id(0)
    i = pl.program_id(1)
    row = jnp.where(i == 0, 4, b)
    shift = mod_ref[pl.ds(row, 1), 0:D_MODEL]
    scale = mod_ref[pl.ds(row, 1), D_MODEL:2 * D_MODEL]
    xin = jnp.where(i == 0, xc_ref[0], x_ref[0])
    h = _rmsnorm_mod(xin, g_ref[...], shift, scale)
    p = _dot(h.astype(BF16), w_ref[...])
    cos2 = cos_ref[...]
    sin2 = sin_ref[...]
    k = p[:, 0:KV_W]
    k = k * cos2 + _swap_halves(k) * sin2
    kv_ref[0, :, 0:KV_W] = k.astype(BF16)
    kv_ref[0, :, KV_W:2 * KV_W] = p[:, KV_W:2 * KV_W].astype(BF16)
    hg_ref[0] = p[:, 256:1024]
    q = p[:, 1024:1536]
    cos8 = jnp.concatenate([cos2] * 4, axis=1)
    sin8 = jnp.concatenate([sin2] * 4, axis=1)
    q = (q * cos8 + _swap_halves(q) * sin8) * (HEAD_DIM ** -0.5 * LOG2E)
    q_ref[0] = q.astype(BF16)
    hqg_ref[0] = p[:, 1536:2048]
    cv_ref[0] = p[:, 2048:2816]


def _in_projection(src, mod_l, g, w_bf16, cos_t, sin_t):
    xc, xl, off, S = _row_sources(src)
    B = xc.shape[0]
    nt = S // ROW_TILE
    row_map = lambda b, i: (b, i, 0)
    const2 = lambda b, i: (0, 0)
    return pl.pallas_call(
        _inproj_kernel,
        grid=(B, nt),
        in_specs=[
            pl.BlockSpec((1, ROW_TILE, D_MODEL), lambda b, i: (b, 0, 0)),
            pl.BlockSpec((1, ROW_TILE, D_MODEL), lambda b, i: (b, jnp.maximum(i - off, 0), 0)),
            pl.BlockSpec((8, N_MOD * D_MODEL), const2),
            pl.BlockSpec((1, D_MODEL), const2),
            pl.BlockSpec((D_MODEL, IN_COLS), const2),
            pl.BlockSpec((ROW_TILE, 2 * HEAD_DIM), lambda b, i: (i, 0)),
            pl.BlockSpec((ROW_TILE, 2 * HEAD_DIM), lambda b, i: (i, 0)),
        ],
        out_specs=[
            pl.BlockSpec((1, ROW_TILE, ATT_W), row_map),
            pl.BlockSpec((1, ROW_TILE, 2 * KV_W), row_map),
            pl.BlockSpec((1, ROW_TILE, 3 * HG_W), row_map),
            pl.BlockSpec((1, ROW_TILE, 2 * HG_W), row_map),
            pl.BlockSpec((1, ROW_TILE, 3 * CV_W), row_map),
        ],
        out_shape=[
            jax.ShapeDtypeStruct((B, S, ATT_W), BF16),
            jax.ShapeDtypeStruct((B, S, 2 * KV_W), BF16),
            jax.ShapeDtypeStruct((B, S, 3 * HG_W), F32),
            jax.ShapeDtypeStruct((B, S, 2 * HG_W), F32),
            jax.ShapeDtypeStruct((B, S, 3 * CV_W), F32),
        ],
        name="in_projection",
        compiler_params=_cparams(("arbitrary", "arbitrary")),
    )(xc, xl, mod_l, g, w_bf16, cos_t, sin_t)


def _attn_kernel(sink_ref, q_ref, kp_ref, kc_ref, kn_ref, kx_ref, o_ref, *, blk0, nblk, nctx):
    n = pl.program_id(1) + blk0
    is_lat = n >= nctx
    has_prev = n > nctx
    has_next = n < nblk - 1
    W = ATT_BLOCK
    q = q_ref[0]
    qrows = jnp.concatenate([q[:, g * W:(g + 1) * W] for g in range(GQA_GROUP)], axis=0)
    kv_all = jnp.concatenate([kp_ref[0], kc_ref[0], kn_ref[0], kx_ref[0]], axis=0)
    nkeys = kv_all.shape[0]
    k_all = kv_all[:, 0:KV_W]
    v_ext = jnp.concatenate([kv_all[:, KV_W:2 * KV_W], jnp.ones((nkeys, KV_W), BF16)], axis=1)
    rows = GQA_GROUP * W
    ri = lax.broadcasted_iota(I32, (rows, W), 0) % W
    cj = lax.broadcasted_iota(I32, (rows, W), 1)
    m_prev = (cj >= ri) & has_prev
    m_cur = jnp.broadcast_to(is_lat, (rows, W))
    m_next = (cj <= ri) & (has_next & is_lat)
    grp = lax.broadcasted_iota(I32, (rows, 1), 0) // W
    low = cj < HEAD_DIM
    outs = []
    for h in range(N_KV):
        qh = jnp.where(low if h == 0 else jnp.logical_not(low), qrows, jnp.zeros_like(qrows))
        sink = jnp.zeros((rows, 1), F32)
        for g in range(GQA_GROUP):
            sink = jnp.where(grp == g, sink_ref[h * GQA_GROUP + g] * LOG2E, sink)
        s = _dot_nt(qh, k_all)
        segs = [jnp.where(m_prev, s[:, 0:W], NEG_BIG),
                jnp.where(m_cur, s[:, W:2 * W], NEG_BIG),
                jnp.where(m_next, s[:, 2 * W:3 * W], NEG_BIG)]
        segs += [s[:, c:c + W] for c in range(3 * W, nkeys, W)]
        mx = segs[0]
        for sg in segs[1:]:
            mx = jnp.maximum(mx, sg)
        m = jnp.maximum(jnp.max(mx, axis=1, keepdims=True), sink)
        p = jnp.concatenate([jnp.exp2(sg - m).astype(BF16) for sg in segs], axis=1)
        oe = _dot(p, v_ext)
        den = oe[:, KV_W:2 * KV_W] + jnp.exp2(sink - m)
        outs.append(oe[:, 0:KV_W] / den)
    o = jnp.where(low, outs[0], outs[1])
    for g in range(GQA_GROUP):
        o_ref[0, :, g * W:(g + 1) * W] = o[g * W:(g + 1) * W].astype(BF16)


def _attention(q, kv, sink, ctx_len, skip_ctx):
    B, S, _ = q.shape
    nblk = S // ATT_BLOCK
    nctx = ctx_len // ATT_BLOCK
    blk0 = nctx if skip_ctx else 0
    blk = lambda f: (lambda b, n: (b, f(n + blk0), 0))
    kern = functools.partial(_attn_kernel, blk0=blk0, nblk=nblk, nctx=nctx)
    return pl.pallas_call(
        kern,
        grid=(B, nblk - blk0),
        in_specs=[
            pl.BlockSpec(memory_space=pltpu.SMEM),
            pl.BlockSpec((1, ATT_BLOCK, ATT_W), blk(lambda n: n)),
            pl.BlockSpec((1, ATT_BLOCK, 2 * KV_W), blk(lambda n: jnp.maximum(n - 1, 0))),
            pl.BlockSpec((1, ATT_BLOCK, 2 * KV_W), blk(lambda n: n)),
            pl.BlockSpec((1, ATT_BLOCK, 2 * KV_W), blk(lambda n: jnp.minimum(n + 1, nblk - 1))),
            pl.BlockSpec((1, ctx_len, 2 * KV_W), lambda b, n: (b, 0, 0)),
        ],
        out_specs=pl.BlockSpec((1, ATT_BLOCK, ATT_W), lambda b, n: (b, n, 0)),
        out_shape=jax.ShapeDtypeStruct((B, S - blk0 * ATT_BLOCK, ATT_W), BF16),
        name="attention",
        compiler_params=_cparams(("arbitrary", "arbitrary")),
    )(sink, q, kv, kv, kv, kv)


def _hgrn_constants():
    C = HG_CHUNK
    t = np.arange(C)[:, None]
    r = np.arange(C)[None, :]
    tri = np.stack([r <= t, r >= t]).astype(np.float32)
    x = t ^ r
    lvl = np.where(x > 0, np.floor(np.log2(np.maximum(x, 1))).astype(np.int32), HG_LEVELS)
    lvl_f = np.where(t >= r, lvl, -1).astype(np.int32)
    lvl_b = np.where(t <= r, lvl, -1).astype(np.int32)
    lvl2 = np.stack([np.tile(lvl_f, (1, HG_HEADS)), np.tile(lvl_b, (1, HG_HEADS))])
    return tri, lvl2


def _span_row(x, span, row):
    C = x.shape[0]
    if span >= 8:
        x3 = x.reshape(C // span, span, x.shape[1])
        return jnp.broadcast_to(x3[:, row:row + 1, :], x3.shape).reshape(x.shape)
    pos = lax.broadcasted_iota(I32, x.shape, 0) % span
    out = x
    for p in range(span):
        if p != row:
            out = jnp.where(pos == p, pltpu.roll(x, (p - row) % C, 0), out)
    return out


def _hgrn_prepare(v, z, qr, lb, tri, backward):
    C = HG_CHUNK
    logf = jnp.log(jnp.maximum(lb, LB_FLOOR) + (1.0 - lb) * jax.nn.sigmoid(z)) * LOG2E
    k = (1.0 - lb) * jax.nn.sigmoid(-z)
    q = qr * jax.nn.sigmoid(qr)
    hi = logf.astype(BF16)
    lo = (logf - hi.astype(F32)).astype(BF16)
    cs = _dot(tri, jnp.concatenate([hi, lo], axis=1))
    lam = cs[:, 0:HG_W] + cs[:, HG_W:2 * HG_W]
    tot = lam[0:1] if backward else lam[C - 1:C]
    local = lam - _span_row(lam, HG_LOCAL, HG_LOCAL // 2)
    return dict(v=v, q=q, k=k, lam=lam, tot=tot, local=local)


def _hgrn_finish(g, head_masks, same_head, lvl, st_ref, sidx, backward, fast):
    C = HG_CHUNK
    q, k, v, lam, tot = g["q"], g["k"], g["v"], g["lam"], g["tot"]
    zero = jnp.zeros((C, HG_W), BF16)

    def per_head_rows(x):
        return jnp.concatenate([jnp.where(hm, x, zero) for hm in head_masks], axis=0)

    qb = q.astype(BF16)
    kb = k.astype(BF16)
    if fast:
        n_local = HG_LOCAL.bit_length() - 1
        a = jnp.where((lvl == HG_LEVELS) | ((lvl >= 0) & (lvl < n_local)),
                      _dot_nt((q * jnp.exp2(g["local"])).astype(BF16),
                              per_head_rows((k * jnp.exp2(-g["local"])).astype(BF16))), 0.0)
        levels = range(n_local, HG_LEVELS)
    else:
        a = jnp.where(lvl == HG_LEVELS, _dot_nt(qb, per_head_rows(kb)), 0.0)
        levels = range(HG_LEVELS)
    for l in levels:
        m = 1 << l
        ref = _span_row(lam, 2 * m, m if backward else m - 1)
        fac = jnp.exp2(-jnp.abs(lam - ref)).astype(BF16)
        a = jnp.where(lvl == l, _dot_nt(qb * fac, per_head_rows(kb * fac)), a)
    st = st_ref[sidx]
    q_in = (q * jnp.exp2(lam)).astype(BF16)
    o = _dot(a.astype(BF16), per_head_rows(v.astype(BF16))) + _dot_nt(q_in, st.astype(BF16))
    k_out = (k * jnp.exp2(tot - lam)).astype(BF16)
    upd = _dot(v.T.astype(BF16), k_out)
    st_ref[sidx] = st * jnp.exp2(tot) + jnp.where(same_head, upd, 0.0)
    return o


def _hgrn_kernel(vf_ref, zf_ref, qf_ref, vb_ref, zb_ref, qb_ref, lb_ref, tri_ref, lvl_ref,
                 of_ref, ob_ref, st_ref):
    j = pl.program_id(1)

    @pl.when(j == 0)
    def _():
        st_ref[...] = jnp.zeros_like(st_ref)

    head_id = (lax.broadcasted_iota(I32, (HG_CHUNK, HG_W), 1) // HEAD_DIM).astype(F32).astype(BF16)
    head_masks = [head_id == float(h) for h in range(HG_HEADS)]
    same_head = (lax.broadcasted_iota(I32, (HG_W, HG_W), 0) // HEAD_DIM
                 == lax.broadcasted_iota(I32, (HG_W, HG_W), 1) // HEAD_DIM)
    streams = []
    for d, (v_ref, z_ref, q_ref, o_ref) in enumerate(
            ((vf_ref, zf_ref, qf_ref, of_ref), (vb_ref, zb_ref, qb_ref, ob_ref))):
        for s in range(v_ref.shape[0]):
            g = _hgrn_prepare(v_ref[s], z_ref[s], q_ref[s], lb_ref[d], tri_ref[d], d == 1)
            streams.append((d, s, o_ref, g))
    worst = jnp.abs(streams[0][3]["local"])
    for _, _, _, g in streams[1:]:
        worst = jnp.maximum(worst, jnp.abs(g["local"]))
    can_use_local = jnp.max(worst) <= HG_LOCAL_MAX_LOG

    def run(fast):
        for d, s, o_ref, g in streams:
            o_ref[s] = _hgrn_finish(g, head_masks, same_head, lvl_ref[d], st_ref, 2 * s + d,
                                    backward=(d == 1), fast=fast)

    pl.when(can_use_local)(functools.partial(run, True))
    pl.when(jnp.logical_not(can_use_local))(functools.partial(run, False))


def _hgrn(hg, hqg, lb_l, ctx_len):
    B, S, _ = hg.shape
    nc = S // HG_CHUNK
    nctx = ctx_len // HG_CHUNK
    tri_np, lvl_np = _hgrn_constants()
    tri = jnp.asarray(tri_np, BF16)
    lvl = jnp.asarray(lvl_np, I32)

    def back(j):
        return jnp.where(j < nctx, nctx - 1 - j, nc - 1 - (j - nctx))

    bs = HG_BATCH if B % HG_BATCH == 0 else 1
    blk = (bs, HG_CHUNK, HG_W)
    fwd = lambda col: (lambda b, j: (b, j, col))
    bwd = lambda col: (lambda b, j: (b, back(j), col))
    const3 = lambda b, j: (0, 0, 0)
    return pl.pallas_call(
        _hgrn_kernel,
        grid=(B // bs, nc),
        in_specs=[
            pl.BlockSpec(blk, fwd(0)), pl.BlockSpec(blk, fwd(1)), pl.BlockSpec(blk, fwd(0)),
            pl.BlockSpec(blk, bwd(0)), pl.BlockSpec(blk, bwd(2)), pl.BlockSpec(blk, bwd(0)),
            pl.BlockSpec((2, 1, HG_W), const3),
            pl.BlockSpec((2, HG_CHUNK, HG_CHUNK), const3),
            pl.BlockSpec((2, HG_CHUNK, HG_HEADS * HG_CHUNK), const3),
        ],
        out_specs=[pl.BlockSpec(blk, fwd(0)), pl.BlockSpec(blk, bwd(0))],
        out_shape=[jax.ShapeDtypeStruct((B, S, HG_W), F32)] * 2,
        scratch_shapes=[pltpu.VMEM((2 * bs, HG_W, HG_W), F32)],
        name="hgrn_scan",
        compiler_params=_cparams(("arbitrary", "arbitrary")),
    )(hg, hg, hqg, hg, hg, hqg, lb_l.reshape(2, 1, HG_W), tri, lvl)


def _mixout_kernel(xc_ref, x_ref, att_ref, of_ref, ob_ref, g_ref, cv_ref, cprev_ref, cnext_ref,
                   mod_ref, gain_ref, cw_ref, wo_ref, n2_ref, wr_ref, ones_ref,
                   x1_ref, h2_ref, aff_ref, *, tile0, ntile):
    b = pl.program_id(0)
    i = pl.program_id(1) + tile0
    row = jnp.where(i == 0, 4, b)
    R = ROW_TILE

    def modv(c):
        return mod_ref[pl.ds(row, 1), c * D_MODEL:(c + 1) * D_MODEL]

    o = of_ref[0] + ob_ref[0]
    sq = o * o
    sq_hi = sq.astype(BF16)
    sq_lo = (sq - sq_hi.astype(F32)).astype(BF16)
    ms = _dot(sq_hi, ones_ref[...]) + _dot(sq_lo, ones_ref[...])
    g = g_ref[0]
    hg = o * lax.rsqrt(ms * (1.0 / HEAD_DIM) + EPS) * gain_ref[...] * (g * jax.nn.sigmoid(g))
    cv = cv_ref[0]
    u = cv[:, CV_W:2 * CV_W] * cv[:, 2 * CV_W:3 * CV_W]
    up = cprev_ref[0]
    un = cnext_ref[0]
    u_prev_row = up[7:8, CV_W:2 * CV_W] * up[7:8, 2 * CV_W:3 * CV_W]
    u_next_row = un[0:1, CV_W:2 * CV_W] * un[0:1, 2 * CV_W:3 * CV_W]
    u_prev_row = jnp.where(i <= 1, 0.0, u_prev_row)
    u_next_row = jnp.where((i == 0) | (i == ntile - 1), 0.0, u_next_row)
    ridx = lax.broadcasted_iota(I32, (R, CV_W), 0)
    u_m1 = jnp.where(ridx == 0, u_prev_row, pltpu.roll(u, 1, 0))
    u_p1 = jnp.where(ridx == R - 1, u_next_row, pltpu.roll(u, R - 1, 0))
    cw = cw_ref[...]
    conv = cv[:, 0:CV_W] * (u_m1 * cw[0:1] + u * cw[1:2] + u_p1 * cw[2:3])
    mix = (_dot(att_ref[0], wo_ref[0:ATT_W])
           + _dot(hg.astype(BF16), wo_ref[ATT_W:ATT_W + HG_W])
           + _dot(conv.astype(BF16), wo_ref[ATT_W + HG_W:D_MODEL]))
    x1 = jnp.where(i == 0, xc_ref[0], x_ref[0]) + modv(2) * mix
    x1_ref[0] = x1
    h2 = _rmsnorm_mod(x1, n2_ref[...], modv(3), modv(4))
    h_hi = h2.astype(BF16)
    h2_ref[0] = h_hi
    h_lo = (h2 - h_hi.astype(F32)).astype(BF16)
    wr = wr_ref[...]
    w_hi = wr.astype(BF16)
    w_lo = (wr - w_hi.astype(F32)).astype(BF16)
    r1 = _dot_nt(jnp.concatenate([w_hi, w_lo], axis=0), h_hi)
    logits = r1[0:N_EXPERTS] + r1[N_EXPERTS:2 * N_EXPERTS] + _dot_nt(w_hi, h_lo)
    e = jnp.exp(logits - jnp.max(logits, axis=0, keepdims=True))
    aff_ref[0] = e / jnp.sum(e, axis=0, keepdims=True)


def _mix_out(src, att, o2, hqg, cv, mod_l, gain, conv_w, wo_bf16, n2g, wr_t, skip_ctx):
    xc, xl, off, S = _row_sources(src)
    B = xc.shape[0]
    ntile = S // ROW_TILE
    tile0 = 1 if skip_ctx else 0
    sub = ROW_TILE // 8
    nsub = S // 8
    rmap = lambda b, i: (b, i + tile0, 0)
    omap = lambda b, i: (b, i, 0)
    s_out = S - tile0 * ROW_TILE
    const2 = lambda b, i: (0, 0)
    ones = jnp.asarray(np.kron(np.eye(HG_HEADS), np.ones((HEAD_DIM, HEAD_DIM))), BF16)
    kern = functools.partial(_mixout_kernel, tile0=tile0, ntile=ntile)
    return pl.pallas_call(
        kern,
        grid=(B, ntile - tile0),
        in_specs=[
            pl.BlockSpec((1, ROW_TILE, D_MODEL), lambda b, i: (b, 0, 0)),
            pl.BlockSpec((1, ROW_TILE, D_MODEL),
                         lambda b, i: (b, jnp.maximum(i + tile0 - off, 0), 0)),
            pl.BlockSpec((1, ROW_TILE, ATT_W), omap),
            pl.BlockSpec((1, ROW_TILE, HG_W), rmap),
            pl.BlockSpec((1, ROW_TILE, HG_W), rmap),
            pl.BlockSpec((1, ROW_TILE, HG_W), lambda b, i: (b, i + tile0, 1)),
            pl.BlockSpec((1, ROW_TILE, 3 * CV_W), rmap),
            pl.BlockSpec((1, 8, 3 * CV_W),
                         lambda b, i: (b, jnp.maximum((i + tile0) * sub - 1, 0), 0)),
            pl.BlockSpec((1, 8, 3 * CV_W),
                         lambda b, i: (b, jnp.minimum((i + tile0 + 1) * sub, nsub - 1), 0)),
            pl.BlockSpec((8, N_MOD * D_MODEL), const2),
            pl.BlockSpec((1, HG_W), const2),
            pl.BlockSpec((3, CV_W), const2),
            pl.BlockSpec((D_MODEL, D_MODEL), const2),
            pl.BlockSpec((1, D_MODEL), const2),
            pl.BlockSpec((N_EXPERTS, D_MODEL), const2),
            pl.BlockSpec((HG_W, HG_W), const2),
        ],
        out_specs=[
            pl.BlockSpec((1, ROW_TILE, D_MODEL), omap),
            pl.BlockSpec((1, ROW_TILE, D_MODEL), omap),
            pl.BlockSpec((1, N_EXPERTS, ROW_TILE), lambda b, i: (b, 0, i)),
        ],
        out_shape=[
            jax.ShapeDtypeStruct((B, s_out, D_MODEL), F32),
            jax.ShapeDtypeStruct((B, s_out, D_MODEL), BF16),
            jax.ShapeDtypeStruct((B, N_EXPERTS, s_out), F32),
        ],
        name="mix_out",
        compiler_params=_cparams(("arbitrary", "arbitrary")),
    )(xc, xl, att, o2[0], o2[1], hqg, cv, cv, cv, mod_l, gain, conv_w, wo_bf16, n2g, wr_t, ones)


def _topk_kernel(aff_ref, tri_ref, pos_ref, cnt_ref, *, segments):
    tri = tri_ref[...]

    def excl_cumsum(mask, n):
        carry = jnp.zeros((N_EXPERTS, 1), F32)
        parts = []
        carries = []
        for c in range(n // 128):
            carries.append(carry)
            blk = jnp.where(mask[:, c * 128:(c + 1) * 128], 1.0, 0.0)
            parts.append(_dot(blk.astype(BF16), tri) + carry)
            carry = carry + jnp.sum(blk, axis=1, keepdims=True)
        return jnp.concatenate(parts, axis=1), carries + [carry]

    for (lo, n, k) in segments:
        a = aff_ref[0, :, lo:lo + n]

        def body(it, thr):
            cand = thr | jnp.left_shift(jnp.int32(1), 30 - it)
            cnt = jnp.sum(jnp.where(a >= pltpu.bitcast(cand, F32), 1.0, 0.0), axis=1, keepdims=True)
            return jnp.where(cnt >= k, cand, thr)

        thr = lax.fori_loop(0, 31, body, jnp.zeros((N_EXPERTS, 1), I32))
        above = a >= pltpu.bitcast(thr + 1, F32)
        tied = jnp.logical_and(a >= pltpu.bitcast(thr, F32), jnp.logical_not(above))
        n_above = jnp.sum(jnp.where(above, 1.0, 0.0), axis=1, keepdims=True)
        rank_tied, _ = excl_cumsum(tied, n)
        sel = above | (tied & (rank_tied < (k - n_above)))
        pos, counts = excl_cumsum(sel, n)
        pos_ref[0, :, lo:lo + n] = jnp.where(sel, pos.astype(I32), -1)
    step = ROW_TILE // 128
    lane = lax.broadcasted_iota(I32, (N_EXPERTS, 128), 1)
    cnt = jnp.zeros((N_EXPERTS, 128), F32)
    for c, col in enumerate(counts[0::step]):
        cnt = jnp.where(lane == c, col, cnt)
    cnt_ref[0] = cnt.astype(I32)


def _topk_positions(aff, segments):
    B, E, S = aff.shape
    assert segments[-1][1] // ROW_TILE + 1 <= 128
    tri = jnp.asarray(np.triu(np.ones((128, 128)), 1), BF16)
    kern = functools.partial(_topk_kernel, segments=segments)
    return pl.pallas_call(
        kern,
        grid=(B,),
        in_specs=[pl.BlockSpec((1, E, S), lambda b: (b, 0, 0)),
                  pl.BlockSpec((128, 128), lambda b: (0, 0))],
        out_specs=[pl.BlockSpec((1, E, S), lambda b: (b, 0, 0)),
                   pl.BlockSpec((1, E, 128), lambda b: (b, 0, 0))],
        out_shape=[jax.ShapeDtypeStruct((B, E, S), I32),
                   jax.ShapeDtypeStruct((B, E, 128), I32)],
        name="topk_positions",
        compiler_params=_cparams(("arbitrary",)),
    )(aff, tri)


def _div_pow2(x, d):
    assert d & (d - 1) == 0
    return lax.shift_right_logical(x, jnp.int32(d.bit_length() - 1))


def _gather_kernel(cnt_ref, pos_ref, aff_ref, h_ref, *refs, cap_l, cap_c):
    if cap_c:
        xl_ref, gl_ref, il_ref, xc_ref, gc_ref = refs
    else:
        xl_ref, gl_ref, il_ref = refs
    b = pl.program_id(0)
    e0 = pl.program_id(1) * GATHER_GROUP
    j = pl.program_id(2)
    G = GATHER_GROUP
    GR = PACK_GRANULE
    W = min(GATHER_WINDOW, cap_l)
    pos = pos_ref[0]
    aff = aff_ref[0]
    h = h_ref[0]

    @pl.when(j == 0)
    def _():
        xl_ref[...] = jnp.zeros_like(xl_ref)
        gl_ref[...] = jnp.zeros_like(gl_ref)
        il_ref[...] = jnp.zeros_like(il_ref)

    def add_rows(g, start, onehot_f32, rows):
        a = pl.multiple_of(start, GR)
        xl_ref[g, 0, pl.ds(a, W), :] += rows.astype(BF16)
        gl_ref[g, 0, pl.ds(a, W), :] += jnp.sum(onehot_f32 * aff[g:g + 1, :], axis=1, keepdims=True)
        il_ref[g, 0, pl.ds(a, W), :] += jnp.sum(onehot_f32 * tok, axis=1,
                                                keepdims=True).astype(I32)

    def latent():
        c = j - 1 if cap_c else j
        slot_i32 = lax.broadcasted_iota(I32, (W, ROW_TILE), 0)
        starts = [jnp.minimum(_div_pow2(cnt_ref[b, e0 + g, c], GR) * GR, cap_l - W)
                  for g in range(G)]
        hits = [(pos[g:g + 1, :] - starts[g]) == slot_i32 for g in range(G)]
        onehots = [jnp.where(hit, 1.0, 0.0) for hit in hits]
        prod = _dot(jnp.concatenate(onehots, axis=0).astype(BF16), h)
        for g in range(G):
            add_rows(g, starts[g], onehots[g], prod[g * W:(g + 1) * W])
        over = [cnt_ref[b, e0 + g, c + 1] - (starts[g] + W) for g in range(G)]
        worst = over[0]
        for o in over[1:]:
            worst = jnp.maximum(worst, o)

        @pl.when(worst > 0)
        def _():
            for g in range(G):
                def extra_window(k, carry, g=g):
                    first_slot = starts[g] + (k + 1) * W
                    start = jnp.minimum(first_slot, cap_l - W)
                    pg = pos[g:g + 1, :]
                    hit = ((pg - start) == slot_i32) & (pg >= first_slot)
                    onehot = jnp.where(hit, 1.0, 0.0)
                    add_rows(g, start, onehot, _dot(onehot.astype(BF16), h))
                    return carry

                lax.fori_loop(0, _div_pow2(jnp.maximum(over[g], 0) + (W - 1), W), extra_window, 0)

    tok = None
    if cap_c:
        tok = (lax.broadcasted_iota(I32, (1, ROW_TILE), 1) + (j - 1) * ROW_TILE).astype(F32)

        @pl.when(j == 0)
        def _():
            slot_c = lax.broadcasted_iota(I32, (cap_c, ROW_TILE), 0)
            onehots = [jnp.where(pos[g:g + 1, :] == slot_c, 1.0, 0.0) for g in range(G)]
            prod = _dot(jnp.concatenate(onehots, axis=0).astype(BF16), h)
            for g in range(G):
                xc_ref[g, 0] = prod[g * cap_c:(g + 1) * cap_c].astype(BF16)
                gc_ref[g, 0] = jnp.sum(onehots[g] * aff[g:g + 1, :], axis=1, keepdims=True)

        pl.when(j > 0)(latent)
    else:
        tok = (lax.broadcasted_iota(I32, (1, ROW_TILE), 1) + j * ROW_TILE).astype(F32)
        latent()


def _gather_rows(cnt, posm, aff, h2, cap_l, cap_c):
    B, E, S = posm.shape
    G = GATHER_GROUP
    assert min(GATHER_WINDOW, cap_l) % PACK_GRANULE == 0 and cap_l % PACK_GRANULE == 0
    kern = functools.partial(_gather_kernel, cap_l=cap_l, cap_c=cap_c)
    omap = lambda b, g, j, cnt_ref: (g, b, 0, 0)
    caps = [cap_l] + ([cap_c] if cap_c else [])
    out_specs = []
    out_shape = []
    for cap in caps:
        out_specs += [pl.BlockSpec((G, 1, cap, D_MODEL), omap), pl.BlockSpec((G, 1, cap, 1), omap)]
        out_shape += [jax.ShapeDtypeStruct((E, B, cap, D_MODEL), BF16),
                      jax.ShapeDtypeStruct((E, B, cap, 1), F32)]
        if cap is caps[0]:
            out_specs.append(pl.BlockSpec((G, 1, cap, 1), omap))
            out_shape.append(jax.ShapeDtypeStruct((E, B, cap, 1), I32))
    tile_map = lambda b, g, j, cnt_ref: (b, g, j)
    grid_spec = pltpu.PrefetchScalarGridSpec(
        num_scalar_prefetch=1,
        grid=(B, E // G, S // ROW_TILE),
        in_specs=[pl.BlockSpec((1, G, ROW_TILE), tile_map),
                  pl.BlockSpec((1, G, ROW_TILE), tile_map),
                  pl.BlockSpec((1, ROW_TILE, D_MODEL), lambda b, g, j, cnt_ref: (b, j, 0))],
        out_specs=out_specs,
    )
    return pl.pallas_call(
        kern,
        grid_spec=grid_spec,
        out_shape=out_shape,
        name="gather_rows",
        compiler_params=_cparams(("arbitrary", "arbitrary", "arbitrary")),
    )(cnt, posm, aff, h2)


def _expert_kernel(*refs, row_chunk, n_sets):
    x_refs = refs[:n_sets]
    g_refs = refs[n_sets:2 * n_sets]
    wg_ref, wu_ref, wd_ref = refs[2 * n_sets:2 * n_sets + 3]
    y_refs = refs[2 * n_sets + 3:3 * n_sets + 3]
    acc_refs = refs[3 * n_sets + 3:]
    f = pl.program_id(1)
    nf = pl.num_programs(1)
    wg = wg_ref[0, 0].astype(BF16)
    wu = wu_ref[0, 0].astype(BF16)
    wd = wd_ref[0, 0].astype(BF16)

    @pl.when(f == 0)
    def _():
        for acc_ref in acc_refs:
            acc_ref[...] = jnp.zeros_like(acc_ref)

    for x_ref, acc_ref in zip(x_refs, acc_refs):
        rows = x_ref.shape[1]
        step = min(row_chunk, rows)
        for r in range(rows // step):
            rs = slice(r * step, (r + 1) * step)
            x = x_ref[0, rs, :]
            g = _dot(x, wg)
            u = _dot(x, wu)
            hid = (g * jax.nn.sigmoid(g) * u).astype(BF16)
            acc_ref[rs, :] += _dot(hid, wd)

    @pl.when(f == nf - 1)
    def _():
        for g_ref, y_ref, acc_ref in zip(g_refs, y_refs, acc_refs):
            y_ref[0] = (acc_ref[...] * g_ref[0]).astype(BF16)


def _expert_mlp(x_sets, g_sets, w_gate, w_up, w_down, layer, row_chunk):
    E = N_EXPERTS
    nf = D_EXPERT // FF_TILE
    n_sets = len(x_sets)
    kern = functools.partial(_expert_kernel, row_chunk=row_chunk, n_sets=n_sets)
    xspec = lambda r: pl.BlockSpec((1, r, D_MODEL), lambda e, f: (e, 0, 0))
    gspec = lambda r: pl.BlockSpec((1, r, 1), lambda e, f: (e, 0, 0))
    return pl.pallas_call(
        kern,
        grid=(E, nf),
        in_specs=[xspec(xs.shape[1]) for xs in x_sets] + [gspec(xs.shape[1]) for xs in x_sets] + [
            pl.BlockSpec((1, 1, D_MODEL, FF_TILE), lambda e, f: (layer, e, 0, f)),
            pl.BlockSpec((1, 1, D_MODEL, FF_TILE), lambda e, f: (layer, e, 0, f)),
            pl.BlockSpec((1, 1, FF_TILE, D_MODEL), lambda e, f: (layer, e, f, 0))],
        out_specs=[xspec(xs.shape[1]) for xs in x_sets],
        out_shape=[jax.ShapeDtypeStruct(xs.shape, BF16) for xs in x_sets],
        scratch_shapes=[pltpu.VMEM(xs.shape[1:], F32) for xs in x_sets],
        name="expert_mlp",
        compiler_params=_cparams(("arbitrary", "arbitrary")),
    )(*x_sets, *g_sets, w_gate, w_up, w_down)


def _combine_kernel(cnt_ref, x_ref, pos_ref, *refs, cap_l, cap_c, final):
    if cap_c:
        yl_ref, il_ref, yc_ref, mod_ref, fg_ref, o_ref, acc_ref, pk_ref, tk_ref = refs
    else:
        yl_ref, il_ref, mod_ref, fg_ref, o_ref, acc_ref, pk_ref, tk_ref = refs
    b = pl.program_id(0)
    i = pl.program_id(1)
    row = jnp.where(i == 0, 4, b) if cap_c else b
    gate = mod_ref[pl.ds(row, 1), 5 * D_MODEL:6 * D_MODEL]
    pos = pos_ref[0]
    acc_ref[...] = jnp.zeros_like(acc_ref)
    GR = PACK_GRANULE
    KC = PACK_CHUNK

    @pl.when((b == 0) & (i == 0))
    def _():
        pk_ref[...] = jnp.zeros_like(pk_ref)
        tk_ref[...] = jnp.full(tk_ref.shape, -1, I32)

    def add_latent():
        t = i - 1 if cap_c else i
        total = 0
        for e in range(N_EXPERTS):
            g0 = _div_pow2(cnt_ref[b, e, t], GR)
            n = _div_pow2(cnt_ref[b, e, t + 1] + (GR - 1), GR) - g0

            def copy_granule(k, carry, e=e, src0=g0 * GR, dst0=total):
                s = pl.multiple_of(src0 + k * GR, GR)
                d = pl.multiple_of(dst0 + k * GR, GR)
                pk_ref[pl.ds(d, GR), :] = yl_ref[e, 0, pl.ds(s, GR), :]
                tk_ref[pl.ds(d, GR), :] = il_ref[e, 0, pl.ds(s, GR), :]
                return carry

            lax.fori_loop(0, n, copy_granule, 0)
            total = total + n * GR
        tk_ref[pl.ds(pl.multiple_of(total, GR), PACK_FIRST), :] = jnp.full((PACK_FIRST, 1), -1, I32)

        def product(r0, rows):
            lane = lax.broadcasted_iota(I32, (rows, ROW_TILE), 1) + t * ROW_TILE
            hit = tk_ref[pl.ds(r0, rows), :] == lane
            return _dot(jnp.where(hit, 1.0, 0.0).T.astype(BF16), pk_ref[pl.ds(r0, rows), :])

        acc_ref[...] += product(0, PACK_FIRST)

        def chunk_product(kc, carry):
            acc_ref[...] += product(pl.multiple_of(PACK_FIRST + kc * KC, KC), KC)
            return carry

        extra = jnp.maximum(total - PACK_FIRST, 0)
        lax.fori_loop(0, _div_pow2(extra + (KC - 1), KC), chunk_product, 0)

    def add_context():
        ncol = yc_ref.shape[1]
        lane = lax.broadcasted_iota(I32, (ROW_TILE, ncol), 1)
        for e in range(N_EXPERTS):
            pe = pos[:, e:e + 1]
            tgt = jnp.where(pe >= 0, pe + b * cap_c, -1)
            acc_ref[...] += _dot(jnp.where(tgt == lane, 1.0, 0.0).astype(BF16), yc_ref[e])

    if cap_c:
        pl.when(i == 0)(add_context)
        pl.when(i > 0)(add_latent)
    else:
        add_latent()
    x2 = x_ref[0] + gate * acc_ref[...]
    if final:
        var = jnp.mean(x2 * x2, axis=-1, keepdims=True)
        x2 = x2 * lax.rsqrt(var + EPS) * fg_ref[...]
    o_ref[0] = x2


def _combine(cnt, x1, pos_t, yl, il, yc, mod_l, final_g, cap_l, cap_c, final):
    B, S, _ = x1.shape
    E = N_EXPERTS
    rmap = lambda b, i, cnt_ref: (b, i, 0)
    const2 = lambda b, i, cnt_ref: (0, 0)
    kern = functools.partial(_combine_kernel, cap_l=cap_l, cap_c=cap_c, final=final)
    per_sample = lambda b, i, cnt_ref: (0, b, 0, 0)
    y_specs = [pl.BlockSpec((E, 1, cap_l, D_MODEL), per_sample, pipeline_mode=pl.Buffered(1)),
               pl.BlockSpec((E, 1, cap_l, 1), per_sample, pipeline_mode=pl.Buffered(1))]
    y_args = [yl, il]
    pack_rows = E * (ROW_TILE + 2 * PACK_GRANULE) + PACK_FIRST
    if cap_c:
        y_specs.append(pl.BlockSpec(yc.shape, lambda b, i, cnt_ref: (0, 0, 0),
                                    pipeline_mode=pl.Buffered(1)))
        y_args.append(yc)
    grid_spec = pltpu.PrefetchScalarGridSpec(
        num_scalar_prefetch=1,
        grid=(B, S // ROW_TILE),
        in_specs=[pl.BlockSpec((1, ROW_TILE, D_MODEL), rmap),
                  pl.BlockSpec((1, ROW_TILE, E), rmap)] + y_specs + [
                  pl.BlockSpec((8, N_MOD * D_MODEL), const2),
                  pl.BlockSpec((1, D_MODEL), const2)],
        out_specs=pl.BlockSpec((1, ROW_TILE, D_MODEL), rmap),
        scratch_shapes=[pltpu.VMEM((ROW_TILE, D_MODEL), F32),
                        pltpu.VMEM((pack_rows, D_MODEL), BF16),
                        pltpu.VMEM((pack_rows, 1), I32)],
    )
    return pl.pallas_call(
        kern,
        grid_spec=grid_spec,
        out_shape=jax.ShapeDtypeStruct((B, S, D_MODEL), F32),
        name="combine",
        compiler_params=_cparams(("arbitrary", "arbitrary")),
    )(cnt, x1, pos_t, *y_args, mod_l, final_g)


def _rope_tables(n_lat, ctx_len):
    t = np.arange(n_lat)
    pos = np.stack([t // GRID_W, t % GRID_W], axis=-1).astype(np.float32)
    inv = (ROPE_BASE ** (-np.arange(ROPE_FREQS, dtype=np.float32) / ROPE_FREQS)).astype(np.float32)
    ang = pos[:, :, None] * inv
    cos = np.cos(ang).astype(np.float32)
    sin = np.sin(ang).astype(np.float32)
    cos64 = np.concatenate([cos[:, 0], cos[:, 0], cos[:, 1], cos[:, 1]], axis=-1)
    sin64 = np.concatenate([-sin[:, 0], sin[:, 0], -sin[:, 1], sin[:, 1]], axis=-1)
    cos_t = np.concatenate([np.ones((ctx_len, 64), np.float32), cos64], axis=0)
    sin_t = np.concatenate([np.zeros((ctx_len, 64), np.float32), sin64], axis=0)
    return (jnp.asarray(np.tile(cos_t, (1, 2))), jnp.asarray(np.tile(sin_t, (1, 2))))


def kernel(x, c, ctx, c_ctx, ada_w, ada_b, norm1_g, norm2_g, w_in, attn_sink, hgrn_lb,
           hgrn_norm_g, conv_w, w_o, router_w, exp_w_gate, exp_w_up, exp_w_down, final_norm_g):
    B, T, D = x.shape
    L = ctx.shape[1]
    depth = ada_w.shape[0]
    assert D == D_MODEL and L == ROW_TILE and T % ROW_TILE == 0 and B <= 4
    S = L + T
    cap_l = EC_CAPACITY * T // N_EXPERTS
    cap_c = EC_CAPACITY * L // N_EXPERTS
    assert cap_l % 16 == 0 and cap_c % 16 == 0 and (B * cap_c) % 16 == 0

    cos_t, sin_t = _rope_tables(T, L)
    gamma = jax.nn.softmax(hgrn_lb.astype(F32), axis=0)
    lb_all = jnp.cumsum(gamma, axis=0) - gamma[0]
    cvec = jnp.concatenate([c, jnp.zeros((4 - B, D), F32), c_ctx[None],
                            jnp.zeros((3, D), F32)], axis=0)
    mod = _modulation(cvec, ada_w, ada_b)
    xs = (ctx, x)

    def interleave_heads(w, axis):
        shp = w.shape
        w = w.reshape(shp[:axis] + (N_KV, GQA_GROUP, HEAD_DIM) + shp[axis + 1:])
        return jnp.swapaxes(w, axis, axis + 1).reshape(shp)

    for l in range(depth):
        last = l == depth - 1
        w_in_l = w_in[l].astype(BF16)
        w_in_l = jnp.concatenate([w_in_l[:, :1024], interleave_heads(w_in_l[:, 1024:1536], 1),
                                  w_in_l[:, 1536:]], axis=1)
        w_o_l = w_o[l].astype(BF16)
        w_o_l = jnp.concatenate([interleave_heads(w_o_l[:ATT_W], 0), w_o_l[ATT_W:]], axis=0)
        q, kv, hg, hqg, cv = _in_projection(xs, mod[l], norm1_g[l][None], w_in_l, cos_t, sin_t)
        att = _attention(q, kv, attn_sink[l], L, skip_ctx=last)
        o2 = _hgrn(hg, hqg, lb_all[l], L)
        x1, h2, aff = _mix_out(xs, att, o2, hqg, cv, mod[l], hgrn_norm_g[l][None], conv_w[l],
                               w_o_l, norm2_g[l][None], router_w[l].T, skip_ctx=last)
        lm, cc = (0, 0) if last else (L, cap_c)
        segments = ((lm, T, cap_l),) if last else ((0, L, cap_c), (L, T, cap_l))
        posm, cnt = _topk_positions(aff, segments)
        gathered = _gather_rows(cnt, posm, aff, h2, cap_l, cc)
        il = gathered[2]
        x_sets = [a.reshape(N_EXPERTS, -1, D) for a in gathered[0::3]]
        g_sets = [a.reshape(N_EXPERTS, -1, 1) for a in gathered[1::3]]
        ys = _expert_mlp(x_sets, g_sets, exp_w_gate, exp_w_up, exp_w_down, l, EXPERT_ROWS)
        yl = ys[0].reshape(N_EXPERTS, B, cap_l, D)
        xs = _combine(cnt, x1, jnp.swapaxes(posm, 1, 2), yl, il, None if last else ys[1], mod[l],
                      final_norm_g[None], cap_l, cc, final=last)
    return xs
```

```python
import functools

import numpy as np
import jax
import jax.numpy as jnp
from jax import lax
from jax.experimental import pallas as pl
from jax.experimental.pallas import tpu as pltpu

F32 = jnp.float32
BF16 = jnp.bfloat16
I32 = jnp.int32
HIGHEST = lax.Precision.HIGHEST

D_MODEL = 1024
GRID_W = 64
EPS = 1e-6
LB_FLOOR = 1e-30
N_MOD = 6
ATT_W = 512
HG_W = 256
CV_W = 256
HEAD_DIM = 64
N_Q = 8
N_KV = 2
GQA_GROUP = 4
KV_W = 128
ROPE_BASE = 10000.0
ROPE_FREQS = 16
HG_HEADS = 4
N_EXPERTS = 16
EC_CAPACITY = 2
D_EXPERT = 2048
IN_COLS = 2816

ROW_TILE = 256
ATT_BLOCK = 128
HG_CHUNK = 128
HG_LEVELS = 7
HG_BATCH = 2
HG_LOCAL = 64
HG_LOCAL_MAX_LOG = 115.0
GATHER_GROUP = 8
GATHER_WINDOW = 64
PACK_GRANULE = 16
FF_TILE = 512
EXPERT_ROWS = 512
MOD_TILE = 1536
VMEM_LIMIT = 56 * 1024 * 1024

NEG_BIG = -1e30
LOG2E = 1.4426950408889634


def _cparams(sem):
    return pltpu.CompilerParams(dimension_semantics=sem, vmem_limit_bytes=VMEM_LIMIT)


def _dot(a, b):
    return jnp.dot(a, b, preferred_element_type=F32)


def _dot_nt(a, b):
    return lax.dot_general(a, b, (((1,), (1,)), ((), ())), preferred_element_type=F32)


def _mod_kernel(a_ref, w_ref, b_ref, o_ref):
    a = a_ref[...]
    a = a * jax.nn.sigmoid(a)
    o_ref[0] = jnp.dot(a, w_ref[0], precision=HIGHEST, preferred_element_type=F32) + b_ref[0]


def _modulation(cvec, ada_w, ada_b):
    depth = ada_w.shape[0]
    ncol = ada_w.shape[2]
    return pl.pallas_call(
        _mod_kernel,
        grid=(depth, ncol // MOD_TILE),
        in_specs=[
            pl.BlockSpec((8, D_MODEL), lambda l, j: (0, 0)),
            pl.BlockSpec((1, D_MODEL, MOD_TILE), lambda l, j: (l, 0, j)),
            pl.BlockSpec((1, 1, MOD_TILE), lambda l, j: (l, 0, j)),
        ],
        out_specs=pl.BlockSpec((1, 8, MOD_TILE), lambda l, j: (l, 0, j)),
        out_shape=jax.ShapeDtypeStruct((depth, 8, ncol), F32),
        name="modulation",
        compiler_params=_cparams(("arbitrary", "arbitrary")),
    )(cvec, ada_w, ada_b.reshape(depth, 1, ncol))


def _swap_halves(x):
    n = x.shape[-1]
    lane = lax.broadcasted_iota(I32, x.shape, x.ndim - 1)
    up = pltpu.roll(x, n - ROPE_FREQS, x.ndim - 1)
    dn = pltpu.roll(x, ROPE_FREQS, x.ndim - 1)
    return jnp.where((lane % (2 * ROPE_FREQS)) < ROPE_FREQS, up, dn)


def _rmsnorm_mod(x, g, shift, scale):
    var = jnp.mean(x * x, axis=-1, keepdims=True)
    y = x * lax.rsqrt(var + EPS) * g
    return y * (1.0 + scale) + shift


def _row_sources(src):
    if isinstance(src, tuple):
        ctx, lat = src
        return ctx, lat, 1, ctx.shape[1] + lat.shape[1]
    return src, src, 0, src.shape[1]


def _inproj_kernel(xc_ref, x_ref, mod_ref, g_ref, w_ref, cos_ref, sin_ref,
                   q_ref, kv_ref, hg_ref, hqg_ref, cv_ref):
    b = pl.program_id(0)
    i = pl.program_id(1)
    row = jnp.where(i == 0, 4, b)
    shift = mod_ref[pl.ds(row, 1), 0:D_MODEL]
    scale = mod_ref[pl.ds(row, 1), D_MODEL:2 * D_MODEL]
    xin = jnp.where(i == 0, xc_ref[0], x_ref[0])
    h = _rmsnorm_mod(xin, g_ref[...], shift, scale)
    p = _dot(h.astype(BF16), w_ref[...])
    cos2 = cos_ref[...]
    sin2 = sin_ref[...]
    k = p[:, 0:KV_W]
    k = k * cos2 + _swap_halves(k) * sin2
    kv_ref[0, :, 0:KV_W] = k.astype(BF16)
    kv_ref[0, :, KV_W:2 * KV_W] = p[:, KV_W:2 * KV_W].astype(BF16)
    hg_ref[0] = p[:, 256:1024]
    q = p[:, 1024:1536]
    cos8 = jnp.concatenate([cos2] * 4, axis=1)
    sin8 = jnp.concatenate([sin2] * 4, axis=1)
    q = (q * cos8 + _swap_halves(q) * sin8) * (HEAD_DIM ** -0.5 * LOG2E)
    q_ref[0] = q.astype(BF16)
    hqg_ref[0] = p[:, 1536:2048]
    cv_ref[0] = p[:, 2048:2816]


def _in_projection(src, mod_l, g, w_bf16, cos_t, sin_t):
    xc, xl, off, S = _row_sources(src)
    B = xc.shape[0]
    nt = S // ROW_TILE
    row_map = lambda b, i: (b, i, 0)
    const2 = lambda b, i: (0, 0)
    return pl.pallas_call(
        _inproj_kernel,
        grid=(B, nt),
        in_specs=[
            pl.BlockSpec((1, ROW_TILE, D_MODEL), lambda b, i: (b, 0, 0)),
            pl.BlockSpec((1, ROW_TILE, D_MODEL), lambda b, i: (b, jnp.maximum(i - off, 0), 0)),
            pl.BlockSpec((8, N_MOD * D_MODEL), const2),
            pl.BlockSpec((1, D_MODEL), const2),
            pl.BlockSpec((D_MODEL, IN_COLS), const2),
            pl.BlockSpec((ROW_TILE, 2 * HEAD_DIM), lambda b, i: (i, 0)),
            pl.BlockSpec((ROW_TILE, 2 * HEAD_DIM), lambda b, i: (i, 0)),
        ],
        out_specs=[
            pl.BlockSpec((1, ROW_TILE, ATT_W), row_map),
            pl.BlockSpec((1, ROW_TILE, 2 * KV_W), row_map),
            pl.BlockSpec((1, ROW_TILE, 3 * HG_W), row_map),
            pl.BlockSpec((1, ROW_TILE, 2 * HG_W), row_map),
            pl.BlockSpec((1, ROW_TILE, 3 * CV_W), row_map),
        ],
        out_shape=[
            jax.ShapeDtypeStruct((B, S, ATT_W), BF16),
            jax.ShapeDtypeStruct((B, S, 2 * KV_W), BF16),
            jax.ShapeDtypeStruct((B, S, 3 * HG_W), F32),
            jax.ShapeDtypeStruct((B, S, 2 * HG_W), F32),
            jax.ShapeDtypeStruct((B, S, 3 * CV_W), F32),
        ],
        name="in_projection",
        compiler_params=_cparams(("arbitrary", "arbitrary")),
    )(xc, xl, mod_l, g, w_bf16, cos_t, sin_t)


def _attn_kernel(sink_ref, q_ref, kp_ref, kc_ref, kn_ref, kx_ref, o_ref, *, blk0, nblk, nctx):
    n = pl.program_id(1) + blk0
    is_lat = n >= nctx
    has_prev = n > nctx
    has_next = n < nblk - 1
    W = ATT_BLOCK
    q = q_ref[0]
    qrows = jnp.concatenate([q[:, g * W:(g + 1) * W] for g in range(GQA_GROUP)], axis=0)
    kv_all = jnp.concatenate([kp_ref[0], kc_ref[0], kn_ref[0], kx_ref[0]], axis=0)
    nkeys = kv_all.shape[0]
    k_all = kv_all[:, 0:KV_W]
    v_ext = jnp.concatenate([kv_all[:, KV_W:2 * KV_W], jnp.ones((nkeys, KV_W), BF16)], axis=1)
    rows = GQA_GROUP * W
    ri = lax.broadcasted_iota(I32, (rows, W), 0) % W
    cj = lax.broadcasted_iota(I32, (rows, W), 1)
    m_prev = (cj >= ri) & has_prev
    m_cur = jnp.broadcast_to(is_lat, (rows, W))
    m_next = (cj <= ri) & (has_next & is_lat)
    grp = lax.broadcasted_iota(I32, (rows, 1), 0) // W
    low = cj < HEAD_DIM
    outs = []
    for h in range(N_KV):
        qh = jnp.where(low if h == 0 else jnp.logical_not(low), qrows, jnp.zeros_like(qrows))
        sink = jnp.zeros((rows, 1), F32)
        for g in range(GQA_GROUP):
            sink = jnp.where(grp == g, sink_ref[h * GQA_GROUP + g] * LOG2E, sink)
        s = _dot_nt(qh, k_all)
        segs = [jnp.where(m_prev, s[:, 0:W], NEG_BIG),
                jnp.where(m_cur, s[:, W:2 * W], NEG_BIG),
                jnp.where(m_next, s[:, 2 * W:3 * W], NEG_BIG)]
        segs += [s[:, c:c + W] for c in range(3 * W, nkeys, W)]
        mx = segs[0]
        for sg in segs[1:]:
            mx = jnp.maximum(mx, sg)
        m = jnp.maximum(jnp.max(mx, axis=1, keepdims=True), sink)
        p = jnp.concatenate([jnp.exp2(sg - m).astype(BF16) for sg in segs], axis=1)
        oe = _dot(p, v_ext)
        den = oe[:, KV_W:2 * KV_W] + jnp.exp2(sink - m)
        outs.append(oe[:, 0:KV_W] / den)
    o = jnp.where(low, outs[0], outs[1])
    for g in range(GQA_GROUP):
        o_ref[0, :, g * W:(g + 1) * W] = o[g * W:(g + 1) * W].astype(BF16)


def _attention(q, kv, sink, ctx_len, skip_ctx):
    B, S, _ = q.shape
    nblk = S // ATT_BLOCK
    nctx = ctx_len // ATT_BLOCK
    blk0 = nctx if skip_ctx else 0
    blk = lambda f: (lambda b, n: (b, f(n + blk0), 0))
    kern = functools.partial(_attn_kernel, blk0=blk0, nblk=nblk, nctx=nctx)
    return pl.pallas_call(
        kern,
        grid=(B, nblk - blk0),
        in_specs=[
            pl.BlockSpec(memory_space=pltpu.SMEM),
            pl.BlockSpec((1, ATT_BLOCK, ATT_W), blk(lambda n: n)),
            pl.BlockSpec((1, ATT_BLOCK, 2 * KV_W), blk(lambda n: jnp.maximum(n - 1, 0))),
            pl.BlockSpec((1, ATT_BLOCK, 2 * KV_W), blk(lambda n: n)),
            pl.BlockSpec((1, ATT_BLOCK, 2 * KV_W), blk(lambda n: jnp.minimum(n + 1, nblk - 1))),
            pl.BlockSpec((1, ctx_len, 2 * KV_W), lambda b, n: (b, 0, 0)),
        ],
        out_specs=pl.BlockSpec((1, ATT_BLOCK, ATT_W), lambda b, n: (b, n, 0)),
        out_shape=jax.ShapeDtypeStruct((B, S - blk0 * ATT_BLOCK, ATT_W), BF16),
        name="attention",
        compiler_params=_cparams(("arbitrary", "arbitrary")),
    )(sink, q, kv, kv, kv, kv)


def _hgrn_constants():
    C = HG_CHUNK
    t = np.arange(C)[:, None]
    r = np.arange(C)[None, :]
    tri = np.stack([r <= t, r >= t]).astype(np.float32)
    x = t ^ r
    lvl = np.where(x > 0, np.floor(np.log2(np.maximum(x, 1))).astype(np.int32), HG_LEVELS)
    lvl_f = np.where(t >= r, lvl, -1).astype(np.int32)
    lvl_b = np.where(t <= r, lvl, -1).astype(np.int32)
    lvl2 = np.stack([np.tile(lvl_f, (1, HG_HEADS)), np.tile(lvl_b, (1, HG_HEADS))])
    return tri, lvl2


def _span_row(x, span, row):
    C = x.shape[0]
    if span >= 8:
        x3 = x.reshape(C // span, span, x.shape[1])
        return jnp.broadcast_to(x3[:, row:row + 1, :], x3.shape).reshape(x.shape)
    pos = lax.broadcasted_iota(I32, x.shape, 0) % span
    out = x
    for p in range(span):
        if p != row:
            out = jnp.where(pos == p, pltpu.roll(x, (p - row) % C, 0), out)
    return out


def _hgrn_prepare(v, z, qr, lb, tri, backward):
    C = HG_CHUNK
    logf = jnp.log(jnp.maximum(lb, LB_FLOOR) + (1.0 - lb) * jax.nn.sigmoid(z)) * LOG2E
    k = (1.0 - lb) * jax.nn.sigmoid(-z)
    q = qr * jax.nn.sigmoid(qr)
    hi = logf.astype(BF16)
    lo = (logf - hi.astype(F32)).astype(BF16)
    cs = _dot(tri, jnp.concatenate([hi, lo], axis=1))
    lam = cs[:, 0:HG_W] + cs[:, HG_W:2 * HG_W]
    tot = lam[0:1] if backward else lam[C - 1:C]
    local = lam - _span_row(lam, HG_LOCAL, HG_LOCAL // 2)
    return dict(v=v, q=q, k=k, lam=lam, tot=tot, local=local)


def _hgrn_finish(g, head_masks, same_head, lvl, st_ref, sidx, backward, fast):
    C = HG_CHUNK
    q, k, v, lam, tot = g["q"], g["k"], g["v"], g["lam"], g["tot"]
    zero = jnp.zeros((C, HG_W), BF16)

    def per_head_rows(x):
        return jnp.concatenate([jnp.where(hm, x, zero) for hm in head_masks], axis=0)

    qb = q.astype(BF16)
    kb = k.astype(BF16)
    if fast:
        n_local = HG_LOCAL.bit_length() - 1
        a = jnp.where((lvl == HG_LEVELS) | ((lvl >= 0) & (lvl < n_local)),
                      _dot_nt((q * jnp.exp2(g["local"])).astype(BF16),
                              per_head_rows((k * jnp.exp2(-g["local"])).astype(BF16))), 0.0)
        levels = range(n_local, HG_LEVELS)
    else:
        a = jnp.where(lvl == HG_LEVELS, _dot_nt(qb, per_head_rows(kb)), 0.0)
        levels = range(HG_LEVELS)
    for l in levels:
        m = 1 << l
        ref = _span_row(lam, 2 * m, m if backward else m - 1)
        fac = jnp.exp2(-jnp.abs(lam - ref)).astype(BF16)
        a = jnp.where(lvl == l, _dot_nt(qb * fac, per_head_rows(kb * fac)), a)
    st = st_ref[sidx]
    q_in = (q * jnp.exp2(lam)).astype(BF16)
    o = _dot(a.astype(BF16), per_head_rows(v.astype(BF16))) + _dot_nt(q_in, st.astype(BF16))
    k_out = (k * jnp.exp2(tot - lam)).astype(BF16)
    upd = _dot(v.T.astype(BF16), k_out)
    st_ref[sidx] = st * jnp.exp2(tot) + jnp.where(same_head, upd, 0.0)
    return o


def _hgrn_kernel(vf_ref, zf_ref, qf_ref, vb_ref, zb_ref, qb_ref, lb_ref, tri_ref, lvl_ref,
                 of_ref, ob_ref, st_ref):
    j = pl.program_id(1)

    @pl.when(j == 0)
    def _():
        st_ref[...] = jnp.zeros_like(st_ref)

    head_id = (lax.broadcasted_iota(I32, (HG_CHUNK, HG_W), 1) // HEAD_DIM).astype(F32).astype(BF16)
    head_masks = [head_id == float(h) for h in range(HG_HEADS)]
    same_head = (lax.broadcasted_iota(I32, (HG_W, HG_W), 0) // HEAD_DIM
                 == lax.broadcasted_iota(I32, (HG_W, HG_W), 1) // HEAD_DIM)
    streams = []
    for d, (v_ref, z_ref, q_ref, o_ref) in enumerate(
            ((vf_ref, zf_ref, qf_ref, of_ref), (vb_ref, zb_ref, qb_ref, ob_ref))):
        for s in range(v_ref.shape[0]):
            g = _hgrn_prepare(v_ref[s], z_ref[s], q_ref[s], lb_ref[d], tri_ref[d], d == 1)
            streams.append((d, s, o_ref, g))
    worst = jnp.abs(streams[0][3]["local"])
    for _, _, _, g in streams[1:]:
        worst = jnp.maximum(worst, jnp.abs(g["local"]))
    can_use_local = jnp.max(worst) <= HG_LOCAL_MAX_LOG

    def run(fast):
        for d, s, o_ref, g in streams:
            o_ref[s] = _hgrn_finish(g, head_masks, same_head, lvl_ref[d], st_ref, 2 * s + d,
                                    backward=(d == 1), fast=fast)

    pl.when(can_use_local)(functools.partial(run, True))
    pl.when(jnp.logical_not(can_use_local))(functools.partial(run, False))


def _hgrn(hg, hqg, lb_l, ctx_len):
    B, S, _ = hg.shape
    nc = S // HG_CHUNK
    nctx = ctx_len // HG_CHUNK
    tri_np, lvl_np = _hgrn_constants()
    tri = jnp.asarray(tri_np, BF16)
    lvl = jnp.asarray(lvl_np, I32)

    def back(j):
        return jnp.where(j < nctx, nctx - 1 - j, nc - 1 - (j - nctx))

    bs = HG_BATCH if B % HG_BATCH == 0 else 1
    blk = (bs, HG_CHUNK, HG_W)
    fwd = lambda col: (lambda b, j: (b, j, col))
    bwd = lambda col: (lambda b, j: (b, back(j), col))
    const3 = lambda b, j: (0, 0, 0)
    return pl.pallas_call(
        _hgrn_kernel,
        grid=(B // bs, nc),
        in_specs=[
            pl.BlockSpec(blk, fwd(0)), pl.BlockSpec(blk, fwd(1)), pl.BlockSpec(blk, fwd(0)),
            pl.BlockSpec(blk, bwd(0)), pl.BlockSpec(blk, bwd(2)), pl.BlockSpec(blk, bwd(0)),
            pl.BlockSpec((2, 1, HG_W), const3),
            pl.BlockSpec((2, HG_CHUNK, HG_CHUNK), const3),
            pl.BlockSpec((2, HG_CHUNK, HG_HEADS * HG_CHUNK), const3),
        ],
        out_specs=[pl.BlockSpec(blk, fwd(0)), pl.BlockSpec(blk, bwd(0))],
        out_shape=[jax.ShapeDtypeStruct((B, S, HG_W), F32)] * 2,
        scratch_shapes=[pltpu.VMEM((2 * bs, HG_W, HG_W), F32)],
        name="hgrn_scan",
        compiler_params=_cparams(("arbitrary", "arbitrary")),
    )(hg, hg, hqg, hg, hg, hqg, lb_l.reshape(2, 1, HG_W), tri, lvl)


def _mixout_kernel(xc_ref, x_ref, att_ref, of_ref, ob_ref, g_ref, cv_ref, cprev_ref, cnext_ref,
                   mod_ref, gain_ref, cw_ref, wo_ref, n2_ref, wr_ref, ones_ref,
                   x1_ref, h2_ref, aff_ref, *, tile0, ntile):
    b = pl.program_id(0)
    i = pl.program_id(1) + tile0
    row = jnp.where(i == 0, 4, b)
    R = ROW_TILE

    def modv(c):
        return mod_ref[pl.ds(row, 1), c * D_MODEL:(c + 1) * D_MODEL]

    o = of_ref[0] + ob_ref[0]
    sq = o * o
    sq_hi = sq.astype(BF16)
    sq_lo = (sq - sq_hi.astype(F32)).astype(BF16)
    ms = _dot(sq_hi, ones_ref[...]) + _dot(sq_lo, ones_ref[...])
    g = g_ref[0]
    hg = o * lax.rsqrt(ms * (1.0 / HEAD_DIM) + EPS) * gain_ref[...] * (g * jax.nn.sigmoid(g))
    cv = cv_ref[0]
    u = cv[:, CV_W:2 * CV_W] * cv[:, 2 * CV_W:3 * CV_W]
    up = cprev_ref[0]
    un = cnext_ref[0]
    u_prev_row = up[7:8, CV_W:2 * CV_W] * up[7:8, 2 * CV_W:3 * CV_W]
    u_next_row = un[0:1, CV_W:2 * CV_W] * un[0:1, 2 * CV_W:3 * CV_W]
    u_prev_row = jnp.where(i <= 1, 0.0, u_prev_row)
    u_next_row = jnp.where((i == 0) | (i == ntile - 1), 0.0, u_next_row)
    ridx = lax.broadcasted_iota(I32, (R, CV_W), 0)
    u_m1 = jnp.where(ridx == 0, u_prev_row, pltpu.roll(u, 1, 0))
    u_p1 = jnp.where(ridx == R - 1, u_next_row, pltpu.roll(u, R - 1, 0))
    cw = cw_ref[...]
    conv = cv[:, 0:CV_W] * (u_m1 * cw[0:1] + u * cw[1:2] + u_p1 * cw[2:3])
    mix = (_dot(att_ref[0], wo_ref[0:ATT_W])
           + _dot(hg.astype(BF16), wo_ref[ATT_W:ATT_W + HG_W])
           + _dot(conv.astype(BF16), wo_ref[ATT_W + HG_W:D_MODEL]))
    x1 = jnp.where(i == 0, xc_ref[0], x_ref[0]) + modv(2) * mix
    x1_ref[0] = x1
    h2 = _rmsnorm_mod(x1, n2_ref[...], modv(3), modv(4))
    h_hi = h2.astype(BF16)
    h2_ref[0] = h_hi
    h_lo = (h2 - h_hi.astype(F32)).astype(BF16)
    wr = wr_ref[...]
    w_hi = wr.astype(BF16)
    w_lo = (wr - w_hi.astype(F32)).astype(BF16)
    r1 = _dot_nt(jnp.concatenate([w_hi, w_lo], axis=0), h_hi)
    logits = r1[0:N_EXPERTS] + r1[N_EXPERTS:2 * N_EXPERTS] + _dot_nt(w_hi, h_lo)
    e = jnp.exp(logits - jnp.max(logits, axis=0, keepdims=True))
    aff_ref[0] = e / jnp.sum(e, axis=0, keepdims=True)


def _mix_out(src, att, o2, hqg, cv, mod_l, gain, conv_w, wo_bf16, n2g, wr_t, skip_ctx):
    xc, xl, off, S = _row_sources(src)
    B = xc.shape[0]
    ntile = S // ROW_TILE
    tile0 = 1 if skip_ctx else 0
    sub = ROW_TILE // 8
    nsub = S // 8
    rmap = lambda b, i: (b, i + tile0, 0)
    omap = lambda b, i: (b, i, 0)
    s_out = S - tile0 * ROW_TILE
    const2 = lambda b, i: (0, 0)
    ones = jnp.asarray(np.kron(np.eye(HG_HEADS), np.ones((HEAD_DIM, HEAD_DIM))), BF16)
    kern = functools.partial(_mixout_kernel, tile0=tile0, ntile=ntile)
    return pl.pallas_call(
        kern,
        grid=(B, ntile - tile0),
        in_specs=[
            pl.BlockSpec((1, ROW_TILE, D_MODEL), lambda b, i: (b, 0, 0)),
            pl.BlockSpec((1, ROW_TILE, D_MODEL),
                         lambda b, i: (b, jnp.maximum(i + tile0 - off, 0), 0)),
            pl.BlockSpec((1, ROW_TILE, ATT_W), omap),
            pl.BlockSpec((1, ROW_TILE, HG_W), rmap),
            pl.BlockSpec((1, ROW_TILE, HG_W), rmap),
            pl.BlockSpec((1, ROW_TILE, HG_W), lambda b, i: (b, i + tile0, 1)),
            pl.BlockSpec((1, ROW_TILE, 3 * CV_W), rmap),
            pl.BlockSpec((1, 8, 3 * CV_W),
                         lambda b, i: (b, jnp.maximum((i + tile0) * sub - 1, 0), 0)),
            pl.BlockSpec((1, 8, 3 * CV_W),
                         lambda b, i: (b, jnp.minimum((i + tile0 + 1) * sub, nsub - 1), 0)),
            pl.BlockSpec((8, N_MOD * D_MODEL), const2),
            pl.BlockSpec((1, HG_W), const2),
            pl.BlockSpec((3, CV_W), const2),
            pl.BlockSpec((D_MODEL, D_MODEL), const2),
            pl.BlockSpec((1, D_MODEL), const2),
            pl.BlockSpec((N_EXPERTS, D_MODEL), const2),
            pl.BlockSpec((HG_W, HG_W), const2),
        ],
        out_specs=[
            pl.BlockSpec((1, ROW_TILE, D_MODEL), omap),
            pl.BlockSpec((1, ROW_TILE, D_MODEL), omap),
            pl.BlockSpec((1, N_EXPERTS, ROW_TILE), lambda b, i: (b, 0, i)),
        ],
        out_shape=[
            jax.ShapeDtypeStruct((B, s_out, D_MODEL), F32),
            jax.ShapeDtypeStruct((B, s_out, D_MODEL), BF16),
            jax.ShapeDtypeStruct((B, N_EXPERTS, s_out), F32),
        ],
        name="mix_out",
        compiler_params=_cparams(("arbitrary", "arbitrary")),
    )(xc, xl, att, o2[0], o2[1], hqg, cv, cv, cv, mod_l, gain, conv_w, wo_bf16, n2g, wr_t, ones)


def _topk_kernel(aff_ref, tri_ref, pos_ref, cnt_ref, *, segments):
    tri = tri_ref[...]

    def excl_cumsum(mask, n):
        carry = jnp.zeros((N_EXPERTS, 1), F32)
        parts = []
        carries = []
        for c in range(n // 128):
            carries.append(carry)
            blk = jnp.where(mask[:, c * 128:(c + 1) * 128], 1.0, 0.0)
            parts.append(_dot(blk.astype(BF16), tri) + carry)
            carry = carry + jnp.sum(blk, axis=1, keepdims=True)
        return jnp.concatenate(parts, axis=1), carries + [carry]

    for (lo, n, k) in segments:
        a = aff_ref[0, :, lo:lo + n]

        def body(it, thr):
            cand = thr | jnp.left_shift(jnp.int32(1), 30 - it)
            cnt = jnp.sum(jnp.where(a >= pltpu.bitcast(cand, F32), 1.0, 0.0), axis=1, keepdims=True)
            return jnp.where(cnt >= k, cand, thr)

        thr = lax.fori_loop(0, 31, body, jnp.zeros((N_EXPERTS, 1), I32))
        above = a >= pltpu.bitcast(thr + 1, F32)
        tied = jnp.logical_and(a >= pltpu.bitcast(thr, F32), jnp.logical_not(above))
        n_above = jnp.sum(jnp.where(above, 1.0, 0.0), axis=1, keepdims=True)
        rank_tied, _ = excl_cumsum(tied, n)
        sel = above | (tied & (rank_tied < (k - n_above)))
        pos, counts = excl_cumsum(sel, n)
        pos_ref[0, :, lo:lo + n] = jnp.where(sel, pos.astype(I32), -1)
    step = ROW_TILE // 128
    lane = lax.broadcasted_iota(I32, (N_EXPERTS, 128), 1)
    cnt = jnp.zeros((N_EXPERTS, 128), F32)
    for c, col in enumerate(counts[0::step]):
        cnt = jnp.where(lane == c, col, cnt)
    cnt_ref[0] = cnt.astype(I32)


def _topk_positions(aff, segments):
    B, E, S = aff.shape
    assert segments[-1][1] // ROW_TILE + 1 <= 128
    tri = jnp.asarray(np.triu(np.ones((128, 128)), 1), BF16)
    kern = functools.partial(_topk_kernel, segments=segments)
    return pl.pallas_call(
        kern,
        grid=(B,),
        in_specs=[pl.BlockSpec((1, E, S), lambda b: (b, 0, 0)),
                  pl.BlockSpec((128, 128), lambda b: (0, 0))],
        out_specs=[pl.BlockSpec((1, E, S), lambda b: (b, 0, 0)),
                   pl.BlockSpec((1, E, 128), lambda b: (b, 0, 0))],
        out_shape=[jax.ShapeDtypeStruct((B, E, S), I32),
                   jax.ShapeDtypeStruct((B, E, 128), I32)],
        name="topk_positions",
        compiler_params=_cparams(("arbitrary",)),
    )(aff, tri)


def _div_pow2(x, d):
    assert d & (d - 1) == 0
    return lax.shift_right_logical(x, jnp.int32(d.bit_length() - 1))


def _gather_kernel(cnt_ref, pos_ref, aff_ref, h_ref, *refs, cap_l, cap_c):
    if cap_c:
        xl_ref, gl_ref, il_ref, xc_ref, gc_ref = refs
    else:
        xl_ref, gl_ref, il_ref = refs
    b = pl.program_id(0)
    e0 = pl.program_id(1) * GATHER_GROUP
    j = pl.program_id(2)
    G = GATHER_GROUP
    GR = PACK_GRANULE
    W = min(GATHER_WINDOW, cap_l)
    pos = pos_ref[0]
    aff = aff_ref[0]
    h = h_ref[0]

    @pl.when(j == 0)
    def _():
        xl_ref[...] = jnp.zeros_like(xl_ref)
        gl_ref[...] = jnp.zeros_like(gl_ref)
        il_ref[...] = jnp.zeros_like(il_ref)

    def add_rows(g, start, onehot_f32, rows):
        a = pl.multiple_of(start, GR)
        xl_ref[g, 0, pl.ds(a, W), :] += rows.astype(BF16)
        gl_ref[g, 0, pl.ds(a, W), :] += jnp.sum(onehot_f32 * aff[g:g + 1, :], axis=1, keepdims=True)
        il_ref[g, 0, pl.ds(a, W), :] += jnp.sum(onehot_f32 * tok, axis=1,
                                                keepdims=True).astype(I32)

    def latent():
        c = j - 1 if cap_c else j
        slot_i32 = lax.broadcasted_iota(I32, (W, ROW_TILE), 0)
        starts = [jnp.minimum(_div_pow2(cnt_ref[b, e0 + g, c], GR) * GR, cap_l - W)
                  for g in range(G)]
        hits = [(pos[g:g + 1, :] - starts[g]) == slot_i32 for g in range(G)]
        onehots = [jnp.where(hit, 1.0, 0.0) for hit in hits]
        prod = _dot(jnp.concatenate(onehots, axis=0).astype(BF16), h)
        for g in range(G):
            add_rows(g, starts[g], onehots[g], prod[g * W:(g + 1) * W])
        over = [cnt_ref[b, e0 + g, c + 1] - (starts[g] + W) for g in range(G)]
        worst = over[0]
        for o in over[1:]:
            worst = jnp.maximum(worst, o)

        @pl.when(worst > 0)
        def _():
            for g in range(G):
                def extra_window(k, carry, g=g):
                    first_slot = starts[g] + (k + 1) * W
                    start = jnp.minimum(first_slot, cap_l - W)
                    pg = pos[g:g + 1, :]
                    hit = ((pg - start) == slot_i32) & (pg >= first_slot)
                    onehot = jnp.where(hit, 1.0, 0.0)
                    add_rows(g, start, onehot, _dot(onehot.astype(BF16), h))
                    return carry

                lax.fori_loop(0, _div_pow2(jnp.maximum(over[g], 0) + (W - 1), W), extra_window, 0)

    tok = None
    if cap_c:
        tok = (lax.broadcasted_iota(I32, (1, ROW_TILE), 1) + (j - 1) * ROW_TILE).astype(F32)

        @pl.when(j == 0)
        def _():
            slot_c = lax.broadcasted_iota(I32, (cap_c, ROW_TILE), 0)
            onehots = [jnp.where(pos[g:g + 1, :] == slot_c, 1.0, 0.0) for g in range(G)]
            prod = _dot(jnp.concatenate(onehots, axis=0).astype(BF16), h)
            for g in range(G):
                xc_ref[g, 0] = prod[g * cap_c:(g + 1) * cap_c].astype(BF16)
                gc_ref[g, 0] = jnp.sum(onehots[g] * aff[g:g + 1, :], axis=1, keepdims=True)

        pl.when(j > 0)(latent)
    else:
        tok = (lax.broadcasted_iota(I32, (1, ROW_TILE), 1) + j * ROW_TILE).astype(F32)
        latent()


def _gather_rows(cnt, posm, aff, h2, cap_l, cap_c):
    B, E, S = posm.shape
    G = GATHER_GROUP
    assert min(GATHER_WINDOW, cap_l) % PACK_GRANULE == 0 and cap_l % PACK_GRANULE == 0
    kern = functools.partial(_gather_kernel, cap_l=cap_l, cap_c=cap_c)
    omap = lambda b, g, j, cnt_ref: (g, b, 0, 0)
    caps = [cap_l] + ([cap_c] if cap_c else [])
    out_specs = []
    out_shape = []
    for cap in caps:
        out_specs += [pl.BlockSpec((G, 1, cap, D_MODEL), omap), pl.BlockSpec((G, 1, cap, 1), omap)]
        out_shape += [jax.ShapeDtypeStruct((E, B, cap, D_MODEL), BF16),
                      jax.ShapeDtypeStruct((E, B, cap, 1), F32)]
        if cap is caps[0]:
            out_specs.append(pl.BlockSpec((G, 1, cap, 1), omap))
            out_shape.append(jax.ShapeDtypeStruct((E, B, cap, 1), I32))
    tile_map = lambda b, g, j, cnt_ref: (b, g, j)
    grid_spec = pltpu.PrefetchScalarGridSpec(
        num_scalar_prefetch=1,
        grid=(B, E // G, S // ROW_TILE),
        in_specs=[pl.BlockSpec((1, G, ROW_TILE), tile_map),
                  pl.BlockSpec((1, G, ROW_TILE), tile_map),
                  pl.BlockSpec((1, ROW_TILE, D_MODEL), lambda b, g, j, cnt_ref: (b, j, 0))],
        out_specs=out_specs,
    )
    return pl.pallas_call(
        kern,
        grid_spec=grid_spec,
        out_shape=out_shape,
        name="gather_rows",
        compiler_params=_cparams(("arbitrary", "arbitrary", "arbitrary")),
    )(cnt, posm, aff, h2)


def _expert_kernel(*refs, row_chunk, n_sets):
    x_refs = refs[:n_sets]
    g_refs = refs[n_sets:2 * n_sets]
    wg_ref, wu_ref, wd_ref = refs[2 * n_sets:2 * n_sets + 3]
    y_refs = refs[2 * n_sets + 3:3 * n_sets + 3]
    acc_refs = refs[3 * n_sets + 3:]
    f = pl.program_id(1)
    nf = pl.num_programs(1)
    wg = wg_ref[0, 0].astype(BF16)
    wu = wu_ref[0, 0].astype(BF16)
    wd = wd_ref[0, 0].astype(BF16)

    @pl.when(f == 0)
    def _():
        for acc_ref in acc_refs:
            acc_ref[...] = jnp.zeros_like(acc_ref)

    for x_ref, acc_ref in zip(x_refs, acc_refs):
        rows = x_ref.shape[1]
        step = min(row_chunk, rows)
        for r in range(rows // step):
            rs = slice(r * step, (r + 1) * step)
            x = x_ref[0, rs, :]
            g = _dot(x, wg)
            u = _dot(x, wu)
            hid = (g * jax.nn.sigmoid(g) * u).astype(BF16)
            acc_ref[rs, :] += _dot(hid, wd)

    @pl.when(f == nf - 1)
    def _():
        for g_ref, y_ref, acc_ref in zip(g_refs, y_refs, acc_refs):
            y_ref[0] = (acc_ref[...] * g_ref[0]).astype(BF16)


def _expert_mlp(x_sets, g_sets, w_gate, w_up, w_down, layer, row_chunk):
    E = N_EXPERTS
    nf = D_EXPERT // FF_TILE
    n_sets = len(x_sets)
    kern = functools.partial(_expert_kernel, row_chunk=row_chunk, n_sets=n_sets)
    xspec = lambda r: pl.BlockSpec((1, r, D_MODEL), lambda e, f: (e, 0, 0))
    gspec = lambda r: pl.BlockSpec((1, r, 1), lambda e, f: (e, 0, 0))
    return pl.pallas_call(
        kern,
        grid=(E, nf),
        in_specs=[xspec(xs.shape[1]) for xs in x_sets] + [gspec(xs.shape[1]) for xs in x_sets] + [
            pl.BlockSpec((1, 1, D_MODEL, FF_TILE), lambda e, f: (layer, e, 0, f)),
            pl.BlockSpec((1, 1, D_MODEL, FF_TILE), lambda e, f: (layer, e, 0, f)),
            pl.BlockSpec((1, 1, FF_TILE, D_MODEL), lambda e, f: (layer, e, f, 0))],
        out_specs=[xspec(xs.shape[1]) for xs in x_sets],
        out_shape=[jax.ShapeDtypeStruct(xs.shape, BF16) for xs in x_sets],
        scratch_shapes=[pltpu.VMEM(xs.shape[1:], F32) for xs in x_sets],
        name="expert_mlp",
        compiler_params=_cparams(("arbitrary", "arbitrary")),
    )(*x_sets, *g_sets, w_gate, w_up, w_down)


def _combine_kernel(cnt_ref, x_ref, pos_ref, *refs, cap_l, cap_c, final):
    if cap_c:
        yl_ref, il_ref, yc_ref, mod_ref, fg_ref, o_ref, acc_ref, pk_ref, tk_ref = refs
    else:
        yl_ref, il_ref, mod_ref, fg_ref, o_ref, acc_ref, pk_ref, tk_ref = refs
    b = pl.program_id(0)
    i = pl.program_id(1)
    row = jnp.where(i == 0, 4, b) if cap_c else b
    gate = mod_ref[pl.ds(row, 1), 5 * D_MODEL:6 * D_MODEL]
    pos = pos_ref[0]
    acc_ref[...] = jnp.zeros_like(acc_ref)
    GR = PACK_GRANULE
    W = min(GATHER_WINDOW, cap_l)

    def add_latent():
        t = i - 1 if cap_c else i
        starts = [jnp.minimum(_div_pow2(cnt_ref[b, e, t], GR) * GR, cap_l - W)
                  for e in range(N_EXPERTS)]
        for e in range(N_EXPERTS):
            a = pl.multiple_of(starts[e], GR)
            pk_ref[e * W:(e + 1) * W, :] = yl_ref[e, 0, pl.ds(a, W), :]
            tk_ref[e * W:(e + 1) * W, :] = il_ref[e, 0, pl.ds(a, W), :]

        def product(tok_ids, rows):
            lane = lax.broadcasted_iota(I32, (tok_ids.shape[0], ROW_TILE), 1) + t * ROW_TILE
            onehot = jnp.where(tok_ids == lane, 1.0, 0.0)
            return _dot(onehot.T.astype(BF16), rows)

        acc_ref[...] += product(tk_ref[...], pk_ref[...])
        over = [cnt_ref[b, e, t + 1] - (starts[e] + W) for e in range(N_EXPERTS)]
        worst = over[0]
        for o in over[1:]:
            worst = jnp.maximum(worst, o)

        @pl.when(worst > 0)
        def _():
            wx = min(2 * W, cap_l)
            slot = lax.broadcasted_iota(I32, (wx, 1), 0)
            for e in range(N_EXPERTS):
                def extra_window(k, carry, e=e):
                    first_slot = starts[e] + W + k * wx
                    a = pl.multiple_of(jnp.minimum(first_slot, cap_l - wx), GR)
                    ids = jnp.where(slot + a >= first_slot, il_ref[e, 0, pl.ds(a, wx), :], -1)
                    acc_ref[...] += product(ids, yl_ref[e, 0, pl.ds(a, wx), :])
                    return carry

                lax.fori_loop(0, _div_pow2(jnp.maximum(over[e], 0) + (wx - 1), wx),
                              extra_window, 0)

    def add_context():
        ncol = yc_ref.shape[1]
        lane = lax.broadcasted_iota(I32, (ROW_TILE, ncol), 1)
        for e in range(N_EXPERTS):
            pe = pos[:, e:e + 1]
            tgt = jnp.where(pe >= 0, pe + b * cap_c, -1)
            acc_ref[...] += _dot(jnp.where(tgt == lane, 1.0, 0.0).astype(BF16), yc_ref[e])

    if cap_c:
        pl.when(i == 0)(add_context)
        pl.when(i > 0)(add_latent)
    else:
        add_latent()
    x2 = x_ref[0] + gate * acc_ref[...]
    if final:
        var = jnp.mean(x2 * x2, axis=-1, keepdims=True)
        x2 = x2 * lax.rsqrt(var + EPS) * fg_ref[...]
    o_ref[0] = x2


def _combine(cnt, x1, pos_t, yl, il, yc, mod_l, final_g, cap_l, cap_c, final):
    B, S, _ = x1.shape
    E = N_EXPERTS
    rmap = lambda b, i, cnt_ref: (b, i, 0)
    const2 = lambda b, i, cnt_ref: (0, 0)
    kern = functools.partial(_combine_kernel, cap_l=cap_l, cap_c=cap_c, final=final)
    per_sample = lambda b, i, cnt_ref: (0, b, 0, 0)
    y_specs = [pl.BlockSpec((E, 1, cap_l, D_MODEL), per_sample, pipeline_mode=pl.Buffered(1)),
               pl.BlockSpec((E, 1, cap_l, 1), per_sample, pipeline_mode=pl.Buffered(1))]
    y_args = [yl, il]
    pack_rows = E * min(GATHER_WINDOW, cap_l)
    if cap_c:
        y_specs.append(pl.BlockSpec(yc.shape, lambda b, i, cnt_ref: (0, 0, 0),
                                    pipeline_mode=pl.Buffered(1)))
        y_args.append(yc)
    grid_spec = pltpu.PrefetchScalarGridSpec(
        num_scalar_prefetch=1,
        grid=(B, S // ROW_TILE),
        in_specs=[pl.BlockSpec((1, ROW_TILE, D_MODEL), rmap),
                  pl.BlockSpec((1, ROW_TILE, E), rmap)] + y_specs + [
                  pl.BlockSpec((8, N_MOD * D_MODEL), const2),
                  pl.BlockSpec((1, D_MODEL), const2)],
        out_specs=pl.BlockSpec((1, ROW_TILE, D_MODEL), rmap),
        scratch_shapes=[pltpu.VMEM((ROW_TILE, D_MODEL), F32),
                        pltpu.VMEM((pack_rows, D_MODEL), BF16),
                        pltpu.VMEM((pack_rows, 1), I32)],
    )
    return pl.pallas_call(
        kern,
        grid_spec=grid_spec,
        out_shape=jax.ShapeDtypeStruct((B, S, D_MODEL), F32),
        name="combine",
        compiler_params=_cparams(("arbitrary", "arbitrary")),
    )(cnt, x1, pos_t, *y_args, mod_l, final_g)


def _rope_tables(n_lat, ctx_len):
    t = np.arange(n_lat)
    pos = np.stack([t // GRID_W, t % GRID_W], axis=-1).astype(np.float32)
    inv = (ROPE_BASE ** (-np.arange(ROPE_FREQS, dtype=np.float32) / ROPE_FREQS)).astype(np.float32)
    ang = pos[:, :, None] * inv
    cos = np.cos(ang).astype(np.float32)
    sin = np.sin(ang).astype(np.float32)
    cos64 = np.concatenate([cos[:, 0], cos[:, 0], cos[:, 1], cos[:, 1]], axis=-1)
    sin64 = np.concatenate([-sin[:, 0], sin[:, 0], -sin[:, 1], sin[:, 1]], axis=-1)
    cos_t = np.concatenate([np.ones((ctx_len, 64), np.float32), cos64], axis=0)
    sin_t = np.concatenate([np.zeros((ctx_len, 64), np.float32), sin64], axis=0)
    return (jnp.asarray(np.tile(cos_t, (1, 2))), jnp.asarray(np.tile(sin_t, (1, 2))))


def kernel(x, c, ctx, c_ctx, ada_w, ada_b, norm1_g, norm2_g, w_in, attn_sink, hgrn_lb,
           hgrn_norm_g, conv_w, w_o, router_w, exp_w_gate, exp_w_up, exp_w_down, final_norm_g):
    B, T, D = x.shape
    L = ctx.shape[1]
    depth = ada_w.shape[0]
    assert D == D_MODEL and L == ROW_TILE and T % ROW_TILE == 0 and B <= 4
    S = L + T
    cap_l = EC_CAPACITY * T // N_EXPERTS
    cap_c = EC_CAPACITY * L // N_EXPERTS
    assert cap_l % 16 == 0 and cap_c % 16 == 0 and (B * cap_c) % 16 == 0

    cos_t, sin_t = _rope_tables(T, L)
    gamma = jax.nn.softmax(hgrn_lb.astype(F32), axis=0)
    lb_all = jnp.cumsum(gamma, axis=0) - gamma[0]
    cvec = jnp.concatenate([c, jnp.zeros((4 - B, D), F32), c_ctx[None],
                            jnp.zeros((3, D), F32)], axis=0)
    mod = _modulation(cvec, ada_w, ada_b)
    xs = (ctx, x)

    def interleave_heads(w, axis):
        shp = w.shape
        w = w.reshape(shp[:axis] + (N_KV, GQA_GROUP, HEAD_DIM) + shp[axis + 1:])
        return jnp.swapaxes(w, axis, axis + 1).reshape(shp)

    for l in range(depth):
        last = l == depth - 1
        w_in_l = w_in[l].astype(BF16)
        w_in_l = jnp.concatenate([w_in_l[:, :1024], interleave_heads(w_in_l[:, 1024:1536], 1),
                                  w_in_l[:, 1536:]], axis=1)
        w_o_l = w_o[l].astype(BF16)
        w_o_l = jnp.concatenate([interleave_heads(w_o_l[:ATT_W], 0), w_o_l[ATT_W:]], axis=0)
        q, kv, hg, hqg, cv = _in_projection(xs, mod[l], norm1_g[l][None], w_in_l, cos_t, sin_t)
        att = _attention(q, kv, attn_sink[l], L, skip_ctx=last)
        o2 = _hgrn(hg, hqg, lb_all[l], L)
        x1, h2, aff = _mix_out(xs, att, o2, hqg, cv, mod[l], hgrn_norm_g[l][None], conv_w[l],
                               w_o_l, norm2_g[l][None], router_w[l].T, skip_ctx=last)
        lm, cc = (0, 0) if last else (L, cap_c)
        segments = ((lm, T, cap_l),) if last else ((0, L, cap_c), (L, T, cap_l))
        posm, cnt = _topk_positions(aff, segments)
        gathered = _gather_rows(cnt, posm, aff, h2, cap_l, cc)
        il = gathered[2]
        x_sets = [a.reshape(N_EXPERTS, -1, D) for a in gathered[0::3]]
        g_sets = [a.reshape(N_EXPERTS, -1, 1) for a in gathered[1::3]]
        ys = _expert_mlp(x_sets, g_sets, exp_w_gate, exp_w_up, exp_w_down, l, EXPERT_ROWS)
        yl = ys[0].reshape(N_EXPERTS, B, cap_l, D)
        xs = _combine(cnt, x1, jnp.swapaxes(posm, 1, 2), yl, il, None if last else ys[1], mod[l],
                      final_norm_g[None], cap_l, cc, final=last)
    return xs
```

```python
import functools

import numpy as np
import jax
import jax.numpy as jnp
from jax import lax
from jax.experimental import pallas as pl
from jax.experimental.pallas import tpu as pltpu

F32 = jnp.float32
BF16 = jnp.bfloat16
I32 = jnp.int32
HIGHEST = lax.Precision.HIGHEST

D_MODEL = 1024
GRID_W = 64
EPS = 1e-6
LB_FLOOR = 1e-30
N_MOD = 6
ATT_W = 512
HG_W = 256
CV_W = 256
HEAD_DIM = 64
N_Q = 8
N_KV = 2
GQA_GROUP = 4
KV_W = 128
ROPE_BASE = 10000.0
ROPE_FREQS = 16
HG_HEADS = 4
N_EXPERTS = 16
EC_CAPACITY = 2
D_EXPERT = 2048
IN_COLS = 2816

ROW_TILE = 256
ATT_BLOCK = 128
HG_CHUNK = 128
HG_LEVELS = 7
HG_BATCH = 2
HG_LOCAL = 64
HG_LOCAL_MAX_LOG = 115.0
GATHER_GROUP = 16
GATHER_WINDOW = 64
PACK_GRANULE = 16
FF_TILE = 512
EXPERT_ROWS = 512
MOD_TILE = 1536
VMEM_LIMIT = 56 * 1024 * 1024

NEG_BIG = -1e30
LOG2E = 1.4426950408889634


def _cparams(sem):
    return pltpu.CompilerParams(dimension_semantics=sem, vmem_limit_bytes=VMEM_LIMIT)


def _dot(a, b):
    return jnp.dot(a, b, preferred_element_type=F32)


def _dot_nt(a, b):
    return lax.dot_general(a, b, (((1,), (1,)), ((), ())), preferred_element_type=F32)


def _mod_kernel(a_ref, w_ref, b_ref, o_ref):
    a = a_ref[...]
    a = a * jax.nn.sigmoid(a)
    o_ref[0] = jnp.dot(a, w_ref[0], precision=HIGHEST, preferred_element_type=F32) + b_ref[0]


def _modulation(cvec, ada_w, ada_b):
    depth = ada_w.shape[0]
    ncol = ada_w.shape[2]
    return pl.pallas_call(
        _mod_kernel,
        grid=(depth, ncol // MOD_TILE),
        in_specs=[
            pl.BlockSpec((8, D_MODEL), lambda l, j: (0, 0)),
            pl.BlockSpec((1, D_MODEL, MOD_TILE), lambda l, j: (l, 0, j)),
            pl.BlockSpec((1, 1, MOD_TILE), lambda l, j: (l, 0, j)),
        ],
        out_specs=pl.BlockSpec((1, 8, MOD_TILE), lambda l, j: (l, 0, j)),
        out_shape=jax.ShapeDtypeStruct((depth, 8, ncol), F32),
        name="modulation",
        compiler_params=_cparams(("arbitrary", "arbitrary")),
    )(cvec, ada_w, ada_b.reshape(depth, 1, ncol))


def _swap_halves(x):
    n = x.shape[-1]
    lane = lax.broadcasted_iota(I32, x.shape, x.ndim - 1)
    up = pltpu.roll(x, n - ROPE_FREQS, x.ndim - 1)
    dn = pltpu.roll(x, ROPE_FREQS, x.ndim - 1)
    return jnp.where((lane % (2 * ROPE_FREQS)) < ROPE_FREQS, up, dn)


def _rmsnorm_mod(x, g, shift, scale):
    var = jnp.mean(x * x, axis=-1, keepdims=True)
    y = x * lax.rsqrt(var + EPS) * g
    return y * (1.0 + scale) + shift


def _row_sources(src):
    if isinstance(src, tuple):
        ctx, lat = src
        return ctx, lat, 1, ctx.shape[1] + lat.shape[1]
    return src, src, 0, src.shape[1]


def _inproj_kernel(xc_ref, x_ref, mod_ref, g_ref, w_ref, cos_ref, sin_ref,
                   q_ref, kv_ref, hg_ref, hqg_ref, cv_ref):
    b = pl.program_id(0)
    i = pl.program_id(1)
    row = jnp.where(i == 0, 4, b)
    shift = mod_ref[pl.ds(row, 1), 0:D_MODEL]
    scale = mod_ref[pl.ds(row, 1), D_MODEL:2 * D_MODEL]
    xin = jnp.where(i == 0, xc_ref[0], x_ref[0])
    h = _rmsnorm_mod(xin, g_ref[...], shift, scale)
    p = _dot(h.astype(BF16), w_ref[...])
    cos2 = cos_ref[...]
    sin2 = sin_ref[...]
    k = p[:, 0:KV_W]
    k = k * cos2 + _swap_halves(k) * sin2
    kv_ref[0, :, 0:KV_W] = k.astype(BF16)
    kv_ref[0, :, KV_W:2 * KV_W] = p[:, KV_W:2 * KV_W].astype(BF16)
    hg_ref[0] = p[:, 256:1024]
    q = p[:, 1024:1536]
    cos8 = jnp.concatenate([cos2] * 4, axis=1)
    sin8 = jnp.concatenate([sin2] * 4, axis=1)
    q = (q * cos8 + _swap_halves(q) * sin8) * (HEAD_DIM ** -0.5 * LOG2E)
    q_ref[0] = q.astype(BF16)
    hqg_ref[0] = p[:, 1536:2048]
    cv_ref[0] = p[:, 2048:2816]


def _in_projection(src, mod_l, g, w_bf16, cos_t, sin_t):
    xc, xl, off, S = _row_sources(src)
    B = xc.shape[0]
    nt = S // ROW_TILE
    row_map = lambda b, i: (b, i, 0)
    const2 = lambda b, i: (0, 0)
    return pl.pallas_call(
        _inproj_kernel,
        grid=(B, nt),
        in_specs=[
            pl.BlockSpec((1, ROW_TILE, D_MODEL), lambda b, i: (b, 0, 0)),
            pl.BlockSpec((1, ROW_TILE, D_MODEL), lambda b, i: (b, jnp.maximum(i - off, 0), 0)),
            pl.BlockSpec((8, N_MOD * D_MODEL), const2),
            pl.BlockSpec((1, D_MODEL), const2),
            pl.BlockSpec((D_MODEL, IN_COLS), const2),
            pl.BlockSpec((ROW_TILE, 2 * HEAD_DIM), lambda b, i: (i, 0)),
            pl.BlockSpec((ROW_TILE, 2 * HEAD_DIM), lambda b, i: (i, 0)),
        ],
        out_specs=[
            pl.BlockSpec((1, ROW_TILE, ATT_W), row_map),
            pl.BlockSpec((1, ROW_TILE, 2 * KV_W), row_map),
            pl.BlockSpec((1, ROW_TILE, 3 * HG_W), row_map),
            pl.BlockSpec((1, ROW_TILE, 2 * HG_W), row_map),
            pl.BlockSpec((1, ROW_TILE, 3 * CV_W), row_map),
        ],
        out_shape=[
            jax.ShapeDtypeStruct((B, S, ATT_W), BF16),
            jax.ShapeDtypeStruct((B, S, 2 * KV_W), BF16),
            jax.ShapeDtypeStruct((B, S, 3 * HG_W), F32),
            jax.ShapeDtypeStruct((B, S, 2 * HG_W), F32),
            jax.ShapeDtypeStruct((B, S, 3 * CV_W), F32),
        ],
        name="in_projection",
        compiler_params=_cparams(("arbitrary", "arbitrary")),
    )(xc, xl, mod_l, g, w_bf16, cos_t, sin_t)


def _attn_kernel(sink_ref, q_ref, kp_ref, kc_ref, kn_ref, kx_ref, o_ref, *, blk0, nblk, nctx):
    n = pl.program_id(1) + blk0
    is_lat = n >= nctx
    has_prev = n > nctx
    has_next = n < nblk - 1
    W = ATT_BLOCK
    q = q_ref[0]
    qrows = jnp.concatenate([q[:, g * W:(g + 1) * W] for g in range(GQA_GROUP)], axis=0)
    kv_all = jnp.concatenate([kp_ref[0], kc_ref[0], kn_ref[0], kx_ref[0]], axis=0)
    nkeys = kv_all.shape[0]
    k_all = kv_all[:, 0:KV_W]
    v_ext = jnp.concatenate([kv_all[:, KV_W:2 * KV_W], jnp.ones((nkeys, KV_W), BF16)], axis=1)
    rows = GQA_GROUP * W
    ri = lax.broadcasted_iota(I32, (rows, W), 0) % W
    cj = lax.broadcasted_iota(I32, (rows, W), 1)
    m_prev = (cj >= ri) & has_prev
    m_cur = jnp.broadcast_to(is_lat, (rows, W))
    m_next = (cj <= ri) & (has_next & is_lat)
    grp = lax.broadcasted_iota(I32, (rows, 1), 0) // W
    low = cj < HEAD_DIM
    outs = []
    for h in range(N_KV):
        qh = jnp.where(low if h == 0 else jnp.logical_not(low), qrows, jnp.zeros_like(qrows))
        sink = jnp.zeros((rows, 1), F32)
        for g in range(GQA_GROUP):
            sink = jnp.where(grp == g, sink_ref[h * GQA_GROUP + g] * LOG2E, sink)
        s = _dot_nt(qh, k_all)
        segs = [jnp.where(m_prev, s[:, 0:W], NEG_BIG),
                jnp.where(m_cur, s[:, W:2 * W], NEG_BIG),
                jnp.where(m_next, s[:, 2 * W:3 * W], NEG_BIG)]
        segs += [s[:, c:c + W] for c in range(3 * W, nkeys, W)]
        mx = segs[0]
        for sg in segs[1:]:
            mx = jnp.maximum(mx, sg)
        m = jnp.maximum(jnp.max(mx, axis=1, keepdims=True), sink)
        p = jnp.concatenate([jnp.exp2(sg - m).astype(BF16) for sg in segs], axis=1)
        oe = _dot(p, v_ext)
        den = oe[:, KV_W:2 * KV_W] + jnp.exp2(sink - m)
        outs.append(oe[:, 0:KV_W] / den)
    o = jnp.where(low, outs[0], outs[1])
    for g in range(GQA_GROUP):
        o_ref[0, :, g * W:(g + 1) * W] = o[g * W:(g + 1) * W].astype(BF16)


def _attention(q, kv, sink, ctx_len, skip_ctx):
    B, S, _ = q.shape
    nblk = S // ATT_BLOCK
    nctx = ctx_len // ATT_BLOCK
    blk0 = nctx if skip_ctx else 0
    blk = lambda f: (lambda b, n: (b, f(n + blk0), 0))
    kern = functools.partial(_attn_kernel, blk0=blk0, nblk=nblk, nctx=nctx)
    return pl.pallas_call(
        kern,
        grid=(B, nblk - blk0),
        in_specs=[
            pl.BlockSpec(memory_space=pltpu.SMEM),
            pl.BlockSpec((1, ATT_BLOCK, ATT_W), blk(lambda n: n)),
            pl.BlockSpec((1, ATT_BLOCK, 2 * KV_W), blk(lambda n: jnp.maximum(n - 1, 0))),
            pl.BlockSpec((1, ATT_BLOCK, 2 * KV_W), blk(lambda n: n)),
            pl.BlockSpec((1, ATT_BLOCK, 2 * KV_W), blk(lambda n: jnp.minimum(n + 1, nblk - 1))),
            pl.BlockSpec((1, ctx_len, 2 * KV_W), lambda b, n: (b, 0, 0)),
        ],
        out_specs=pl.BlockSpec((1, ATT_BLOCK, ATT_W), lambda b, n: (b, n, 0)),
        out_shape=jax.ShapeDtypeStruct((B, S - blk0 * ATT_BLOCK, ATT_W), BF16),
        name="attention",
        compiler_params=_cparams(("arbitrary", "arbitrary")),
    )(sink, q, kv, kv, kv, kv)


def _hgrn_constants():
    C = HG_CHUNK
    t = np.arange(C)[:, None]
    r = np.arange(C)[None, :]
    tri = np.stack([r <= t, r >= t]).astype(np.float32)
    x = t ^ r
    lvl = np.where(x > 0, np.floor(np.log2(np.maximum(x, 1))).astype(np.int32), HG_LEVELS)
    lvl_f = np.where(t >= r, lvl, -1).astype(np.int32)
    lvl_b = np.where(t <= r, lvl, -1).astype(np.int32)
    lvl2 = np.stack([np.tile(lvl_f, (1, HG_HEADS)), np.tile(lvl_b, (1, HG_HEADS))])
    return tri, lvl2


def _span_row(x, span, row):
    C = x.shape[0]
    if span >= 8:
        x3 = x.reshape(C // span, span, x.shape[1])
        return jnp.broadcast_to(x3[:, row:row + 1, :], x3.shape).reshape(x.shape)
    pos = lax.broadcasted_iota(I32, x.shape, 0) % span
    out = x
    for p in range(span):
        if p != row:
            out = jnp.where(pos == p, pltpu.roll(x, (p - row) % C, 0), out)
    return out


def _hgrn_prepare(v, z, qr, lb, tri, backward):
    C = HG_CHUNK
    logf = jnp.log(jnp.maximum(lb, LB_FLOOR) + (1.0 - lb) * jax.nn.sigmoid(z)) * LOG2E
    k = (1.0 - lb) * jax.nn.sigmoid(-z)
    q = qr * jax.nn.sigmoid(qr)
    hi = logf.astype(BF16)
    lo = (logf - hi.astype(F32)).astype(BF16)
    cs = _dot(tri, jnp.concatenate([hi, lo], axis=1))
    lam = cs[:, 0:HG_W] + cs[:, HG_W:2 * HG_W]
    tot = lam[0:1] if backward else lam[C - 1:C]
    local = lam - _span_row(lam, HG_LOCAL, HG_LOCAL // 2)
    return dict(v=v, q=q, k=k, lam=lam, tot=tot, local=local)


def _hgrn_finish(g, head_masks, same_head, lvl, st_ref, sidx, backward, fast):
    C = HG_CHUNK
    q, k, v, lam, tot = g["q"], g["k"], g["v"], g["lam"], g["tot"]
    zero = jnp.zeros((C, HG_W), BF16)

    def per_head_rows(x):
        return jnp.concatenate([jnp.where(hm, x, zero) for hm in head_masks], axis=0)

    qb = q.astype(BF16)
    kb = k.astype(BF16)
    if fast:
        n_local = HG_LOCAL.bit_length() - 1
        a = jnp.where((lvl == HG_LEVELS) | ((lvl >= 0) & (lvl < n_local)),
                      _dot_nt((q * jnp.exp2(g["local"])).astype(BF16),
                              per_head_rows((k * jnp.exp2(-g["local"])).astype(BF16))), 0.0)
        levels = range(n_local, HG_LEVELS)
    else:
        a = jnp.where(lvl == HG_LEVELS, _dot_nt(qb, per_head_rows(kb)), 0.0)
        levels = range(HG_LEVELS)
    for l in levels:
        m = 1 << l
        ref = _span_row(lam, 2 * m, m if backward else m - 1)
        fac = jnp.exp2(-jnp.abs(lam - ref)).astype(BF16)
        a = jnp.where(lvl == l, _dot_nt(qb * fac, per_head_rows(kb * fac)), a)
    st = st_ref[sidx]
    q_in = (q * jnp.exp2(lam)).astype(BF16)
    o = _dot(a.astype(BF16), per_head_rows(v.astype(BF16))) + _dot_nt(q_in, st.astype(BF16))
    k_out = (k * jnp.exp2(tot - lam)).astype(BF16)
    upd = _dot(v.T.astype(BF16), k_out)
    st_ref[sidx] = st * jnp.exp2(tot) + jnp.where(same_head, upd, 0.0)
    return o


def _hgrn_kernel(vf_ref, zf_ref, qf_ref, vb_ref, zb_ref, qb_ref, lb_ref, tri_ref, lvl_ref,
                 of_ref, ob_ref, st_ref):
    j = pl.program_id(1)

    @pl.when(j == 0)
    def _():
        st_ref[...] = jnp.zeros_like(st_ref)

    head_id = (lax.broadcasted_iota(I32, (HG_CHUNK, HG_W), 1) // HEAD_DIM).astype(F32).astype(BF16)
    head_masks = [head_id == float(h) for h in range(HG_HEADS)]
    same_head = (lax.broadcasted_iota(I32, (HG_W, HG_W), 0) // HEAD_DIM
                 == lax.broadcasted_iota(I32, (HG_W, HG_W), 1) // HEAD_DIM)
    streams = []
    for d, (v_ref, z_ref, q_ref, o_ref) in enumerate(
            ((vf_ref, zf_ref, qf_ref, of_ref), (vb_ref, zb_ref, qb_ref, ob_ref))):
        for s in range(v_ref.shape[0]):
            g = _hgrn_prepare(v_ref[s], z_ref[s], q_ref[s], lb_ref[d], tri_ref[d], d == 1)
            streams.append((d, s, o_ref, g))
    worst = jnp.abs(streams[0][3]["local"])
    for _, _, _, g in streams[1:]:
        worst = jnp.maximum(worst, jnp.abs(g["local"]))
    can_use_local = jnp.max(worst) <= HG_LOCAL_MAX_LOG

    def run(fast):
        for d, s, o_ref, g in streams:
            o_ref[s] = _hgrn_finish(g, head_masks, same_head, lvl_ref[d], st_ref, 2 * s + d,
                                    backward=(d == 1), fast=fast)

    pl.when(can_use_local)(functools.partial(run, True))
    pl.when(jnp.logical_not(can_use_local))(functools.partial(run, False))


def _hgrn(hg, hqg, lb_l, ctx_len):
    B, S, _ = hg.shape
    nc = S // HG_CHUNK
    nctx = ctx_len // HG_CHUNK
    tri_np, lvl_np = _hgrn_constants()
    tri = jnp.asarray(tri_np, BF16)
    lvl = jnp.asarray(lvl_np, I32)

    def back(j):
        return jnp.where(j < nctx, nctx - 1 - j, nc - 1 - (j - nctx))

    bs = HG_BATCH if B % HG_BATCH == 0 else 1
    blk = (bs, HG_CHUNK, HG_W)
    fwd = lambda col: (lambda b, j: (b, j, col))
    bwd = lambda col: (lambda b, j: (b, back(j), col))
    const3 = lambda b, j: (0, 0, 0)
    return pl.pallas_call(
        _hgrn_kernel,
        grid=(B // bs, nc),
        in_specs=[
            pl.BlockSpec(blk, fwd(0)), pl.BlockSpec(blk, fwd(1)), pl.BlockSpec(blk, fwd(0)),
            pl.BlockSpec(blk, bwd(0)), pl.BlockSpec(blk, bwd(2)), pl.BlockSpec(blk, bwd(0)),
            pl.BlockSpec((2, 1, HG_W), const3),
            pl.BlockSpec((2, HG_CHUNK, HG_CHUNK), const3),
            pl.BlockSpec((2, HG_CHUNK, HG_HEADS * HG_CHUNK), const3),
        ],
        out_specs=[pl.BlockSpec(blk, fwd(0)), pl.BlockSpec(blk, bwd(0))],
        out_shape=[jax.ShapeDtypeStruct((B, S, HG_W), F32)] * 2,
        scratch_shapes=[pltpu.VMEM((2 * bs, HG_W, HG_W), F32)],
        name="hgrn_scan",
        compiler_params=_cparams(("arbitrary", "arbitrary")),
    )(hg, hg, hqg, hg, hg, hqg, lb_l.reshape(2, 1, HG_W), tri, lvl)


def _mixout_kernel(xc_ref, x_ref, att_ref, of_ref, ob_ref, g_ref, cv_ref, cprev_ref, cnext_ref,
                   mod_ref, gain_ref, cw_ref, wo_ref, n2_ref, wr_ref, ones_ref,
                   x1_ref, h2_ref, aff_ref, *, tile0, ntile):
    b = pl.program_id(0)
    i = pl.program_id(1) + tile0
    row = jnp.where(i == 0, 4, b)
    R = ROW_TILE

    def modv(c):
        return mod_ref[pl.ds(row, 1), c * D_MODEL:(c + 1) * D_MODEL]

    o = of_ref[0] + ob_ref[0]
    sq = o * o
    sq_hi = sq.astype(BF16)
    sq_lo = (sq - sq_hi.astype(F32)).astype(BF16)
    ms = _dot(sq_hi, ones_ref[...]) + _dot(sq_lo, ones_ref[...])
    g = g_ref[0]
    hg = o * lax.rsqrt(ms * (1.0 / HEAD_DIM) + EPS) * gain_ref[...] * (g * jax.nn.sigmoid(g))
    cv = cv_ref[0]
    u = cv[:, CV_W:2 * CV_W] * cv[:, 2 * CV_W:3 * CV_W]
    up = cprev_ref[0]
    un = cnext_ref[0]
    u_prev_row = up[7:8, CV_W:2 * CV_W] * up[7:8, 2 * CV_W:3 * CV_W]
    u_next_row = un[0:1, CV_W:2 * CV_W] * un[0:1, 2 * CV_W:3 * CV_W]
    u_prev_row = jnp.where(i <= 1, 0.0, u_prev_row)
    u_next_row = jnp.where((i == 0) | (i == ntile - 1), 0.0, u_next_row)
    ridx = lax.broadcasted_iota(I32, (R, CV_W), 0)
    u_m1 = jnp.where(ridx == 0, u_prev_row, pltpu.roll(u, 1, 0))
    u_p1 = jnp.where(ridx == R - 1, u_next_row, pltpu.roll(u, R - 1, 0))
    cw = cw_ref[...]
    conv = cv[:, 0:CV_W] * (u_m1 * cw[0:1] + u * cw[1:2] + u_p1 * cw[2:3])
    mix = (_dot(att_ref[0], wo_ref[0:ATT_W])
           + _dot(hg.astype(BF16), wo_ref[ATT_W:ATT_W + HG_W])
           + _dot(conv.astype(BF16), wo_ref[ATT_W + HG_W:D_MODEL]))
    x1 = jnp.where(i == 0, xc_ref[0], x_ref[0]) + modv(2) * mix
    x1_ref[0] = x1
    h2 = _rmsnorm_mod(x1, n2_ref[...], modv(3), modv(4))
    h_hi = h2.astype(BF16)
    h2_ref[0] = h_hi
    h_lo = (h2 - h_hi.astype(F32)).astype(BF16)
    wr = wr_ref[...]
    w_hi = wr.astype(BF16)
    w_lo = (wr - w_hi.astype(F32)).astype(BF16)
    r1 = _dot_nt(jnp.concatenate([w_hi, w_lo], axis=0), h_hi)
    logits = r1[0:N_EXPERTS] + r1[N_EXPERTS:2 * N_EXPERTS] + _dot_nt(w_hi, h_lo)
    e = jnp.exp(logits - jnp.max(logits, axis=0, keepdims=True))
    aff_ref[0] = e / jnp.sum(e, axis=0, keepdims=True)


def _mix_out(src, att, o2, hqg, cv, mod_l, gain, conv_w, wo_bf16, n2g, wr_t, skip_ctx):
    xc, xl, off, S = _row_sources(src)
    B = xc.shape[0]
    ntile = S // ROW_TILE
    tile0 = 1 if skip_ctx else 0
    sub = ROW_TILE // 8
    nsub = S // 8
    rmap = lambda b, i: (b, i + tile0, 0)
    omap = lambda b, i: (b, i, 0)
    s_out = S - tile0 * ROW_TILE
    const2 = lambda b, i: (0, 0)
    ones = jnp.asarray(np.kron(np.eye(HG_HEADS), np.ones((HEAD_DIM, HEAD_DIM))), BF16)
    kern = functools.partial(_mixout_kernel, tile0=tile0, ntile=ntile)
    return pl.pallas_call(
        kern,
        grid=(B, ntile - tile0),
        in_specs=[
            pl.BlockSpec((1, ROW_TILE, D_MODEL), lambda b, i: (b, 0, 0)),
            pl.BlockSpec((1, ROW_TILE, D_MODEL),
                         lambda b, i: (b, jnp.maximum(i + tile0 - off, 0), 0)),
            pl.BlockSpec((1, ROW_TILE, ATT_W), omap),
            pl.BlockSpec((1, ROW_TILE, HG_W), rmap),
            pl.BlockSpec((1, ROW_TILE, HG_W), rmap),
            pl.BlockSpec((1, ROW_TILE, HG_W), lambda b, i: (b, i + tile0, 1)),
            pl.BlockSpec((1, ROW_TILE, 3 * CV_W), rmap),
            pl.BlockSpec((1, 8, 3 * CV_W),
                         lambda b, i: (b, jnp.maximum((i + tile0) * sub - 1, 0), 0)),
            pl.BlockSpec((1, 8, 3 * CV_W),
                         lambda b, i: (b, jnp.minimum((i + tile0 + 1) * sub, nsub - 1), 0)),
            pl.BlockSpec((8, N_MOD * D_MODEL), const2),
            pl.BlockSpec((1, HG_W), const2),
            pl.BlockSpec((3, CV_W), const2),
            pl.BlockSpec((D_MODEL, D_MODEL), const2),
            pl.BlockSpec((1, D_MODEL), const2),
            pl.BlockSpec((N_EXPERTS, D_MODEL), const2),
            pl.BlockSpec((HG_W, HG_W), const2),
        ],
        out_specs=[
            pl.BlockSpec((1, ROW_TILE, D_MODEL), omap),
            pl.BlockSpec((1, ROW_TILE, D_MODEL), omap),
            pl.BlockSpec((1, N_EXPERTS, ROW_TILE), lambda b, i: (b, 0, i)),
        ],
        out_shape=[
            jax.ShapeDtypeStruct((B, s_out, D_MODEL), F32),
            jax.ShapeDtypeStruct((B, s_out, D_MODEL), BF16),
            jax.ShapeDtypeStruct((B, N_EXPERTS, s_out), F32),
        ],
        name="mix_out",
        compiler_params=_cparams(("arbitrary", "arbitrary")),
    )(xc, xl, att, o2[0], o2[1], hqg, cv, cv, cv, mod_l, gain, conv_w, wo_bf16, n2g, wr_t, ones)


def _topk_kernel(aff_ref, tri_ref, pos_ref, cnt_ref, *, segments):
    tri = tri_ref[...]

    def excl_cumsum(mask, n):
        carry = jnp.zeros((N_EXPERTS, 1), F32)
        parts = []
        carries = []
        for c in range(n // 128):
            carries.append(carry)
            blk = jnp.where(mask[:, c * 128:(c + 1) * 128], 1.0, 0.0)
            parts.append(_dot(blk.astype(BF16), tri) + carry)
            carry = carry + jnp.sum(blk, axis=1, keepdims=True)
        return jnp.concatenate(parts, axis=1), carries + [carry]

    for (lo, n, k) in segments:
        a = aff_ref[0, :, lo:lo + n]

        def body(it, thr):
            cand = thr | jnp.left_shift(jnp.int32(1), 30 - it)
            cnt = jnp.sum(jnp.where(a >= pltpu.bitcast(cand, F32), 1.0, 0.0), axis=1, keepdims=True)
            return jnp.where(cnt >= k, cand, thr)

        thr = lax.fori_loop(0, 31, body, jnp.zeros((N_EXPERTS, 1), I32))
        above = a >= pltpu.bitcast(thr + 1, F32)
        tied = jnp.logical_and(a >= pltpu.bitcast(thr, F32), jnp.logical_not(above))
        n_above = jnp.sum(jnp.where(above, 1.0, 0.0), axis=1, keepdims=True)
        rank_tied, _ = excl_cumsum(tied, n)
        sel = above | (tied & (rank_tied < (k - n_above)))
        pos, counts = excl_cumsum(sel, n)
        pos_ref[0, :, lo:lo + n] = jnp.where(sel, pos.astype(I32), -1)
    step = ROW_TILE // 128
    lane = lax.broadcasted_iota(I32, (N_EXPERTS, 128), 1)
    cnt = jnp.zeros((N_EXPERTS, 128), F32)
    for c, col in enumerate(counts[0::step]):
        cnt = jnp.where(lane == c, col, cnt)
    cnt_ref[0] = cnt.astype(I32)


def _topk_positions(aff, segments):
    B, E, S = aff.shape
    assert segments[-1][1] // ROW_TILE + 1 <= 128
    tri = jnp.asarray(np.triu(np.ones((128, 128)), 1), BF16)
    kern = functools.partial(_topk_kernel, segments=segments)
    return pl.pallas_call(
        kern,
        grid=(B,),
        in_specs=[pl.BlockSpec((1, E, S), lambda b: (b, 0, 0)),
                  pl.BlockSpec((128, 128), lambda b: (0, 0))],
        out_specs=[pl.BlockSpec((1, E, S), lambda b: (b, 0, 0)),
                   pl.BlockSpec((1, E, 128), lambda b: (b, 0, 0))],
        out_shape=[jax.ShapeDtypeStruct((B, E, S), I32),
                   jax.ShapeDtypeStruct((B, E, 128), I32)],
        name="topk_positions",
        compiler_params=_cparams(("arbitrary",)),
    )(aff, tri)


def _div_pow2(x, d):
    assert d & (d - 1) == 0
    return lax.shift_right_logical(x, jnp.int32(d.bit_length() - 1))


def _gather_kernel(cnt_ref, pos_ref, aff_ref, h_ref, *refs, cap_l, cap_c):
    if cap_c:
        xl_ref, gi_ref, xc_ref, gc_ref = refs
    else:
        xl_ref, gi_ref = refs
    b = pl.program_id(0)
    e0 = pl.program_id(1) * GATHER_GROUP
    j = pl.program_id(2)
    G = GATHER_GROUP
    GR = PACK_GRANULE
    W = min(GATHER_WINDOW, cap_l)
    pos = pos_ref[0]
    aff = aff_ref[0]
    h = h_ref[0]

    @pl.when(j == 0)
    def _():
        xl_ref[...] = jnp.zeros_like(xl_ref)
        gi_ref[...] = jnp.zeros_like(gi_ref)

    first_lane = lax.broadcasted_iota(I32, (W, 2), 1) == 0

    def add_rows(g, start, onehot_f32, rows):
        a = pl.multiple_of(start, GR)
        xl_ref[g, 0, pl.ds(a, W), :] += rows.astype(BF16)
        gate = jnp.sum(onehot_f32 * aff[g:g + 1, :], axis=1, keepdims=True)
        tok_id = jnp.sum(onehot_f32 * tok, axis=1, keepdims=True)
        gi_ref[g, 0, pl.ds(a, W), :] += jnp.where(first_lane, gate, tok_id)

    def latent():
        c = j - 1 if cap_c else j
        slot_i32 = lax.broadcasted_iota(I32, (W, ROW_TILE), 0)
        starts = [jnp.minimum(_div_pow2(cnt_ref[b, e0 + g, c], GR) * GR, cap_l - W)
                  for g in range(G)]
        hits = [(pos[g:g + 1, :] - starts[g]) == slot_i32 for g in range(G)]
        onehots = [jnp.where(hit, 1.0, 0.0) for hit in hits]
        prod = _dot(jnp.concatenate(onehots, axis=0).astype(BF16), h)
        for g in range(G):
            add_rows(g, starts[g], onehots[g], prod[g * W:(g + 1) * W])
        over = [cnt_ref[b, e0 + g, c + 1] - (starts[g] + W) for g in range(G)]
        worst = over[0]
        for o in over[1:]:
            worst = jnp.maximum(worst, o)

        @pl.when(worst > 0)
        def _():
            for g in range(G):
                def extra_window(k, carry, g=g):
                    first_slot = starts[g] + (k + 1) * W
                    start = jnp.minimum(first_slot, cap_l - W)
                    pg = pos[g:g + 1, :]
                    hit = ((pg - start) == slot_i32) & (pg >= first_slot)
                    onehot = jnp.where(hit, 1.0, 0.0)
                    add_rows(g, start, onehot, _dot(onehot.astype(BF16), h))
                    return carry

                lax.fori_loop(0, _div_pow2(jnp.maximum(over[g], 0) + (W - 1), W), extra_window, 0)

    tok = None
    if cap_c:
        tok = (lax.broadcasted_iota(I32, (1, ROW_TILE), 1) + (j - 1) * ROW_TILE).astype(F32)

        @pl.when(j == 0)
        def _():
            slot_c = lax.broadcasted_iota(I32, (cap_c, ROW_TILE), 0)
            onehots = [jnp.where(pos[g:g + 1, :] == slot_c, 1.0, 0.0) for g in range(G)]
            prod = _dot(jnp.concatenate(onehots, axis=0).astype(BF16), h)
            for g in range(G):
                xc_ref[g, 0] = prod[g * cap_c:(g + 1) * cap_c].astype(BF16)
                gc_ref[g, 0] = jnp.sum(onehots[g] * aff[g:g + 1, :], axis=1, keepdims=True)

        pl.when(j > 0)(latent)
    else:
        tok = (lax.broadcasted_iota(I32, (1, ROW_TILE), 1) + j * ROW_TILE).astype(F32)
        latent()


def _gather_rows(cnt, posm, aff, h2, cap_l, cap_c):
    B, E, S = posm.shape
    G = GATHER_GROUP
    assert min(GATHER_WINDOW, cap_l) % PACK_GRANULE == 0 and cap_l % PACK_GRANULE == 0
    kern = functools.partial(_gather_kernel, cap_l=cap_l, cap_c=cap_c)
    omap = lambda b, g, j, cnt_ref: (g, b, 0, 0)
    out_specs = [pl.BlockSpec((G, 1, cap_l, D_MODEL), omap), pl.BlockSpec((G, 1, cap_l, 2), omap)]
    out_shape = [jax.ShapeDtypeStruct((E, B, cap_l, D_MODEL), BF16),
                 jax.ShapeDtypeStruct((E, B, cap_l, 2), F32)]
    if cap_c:
        out_specs += [pl.BlockSpec((G, 1, cap_c, D_MODEL), omap),
                      pl.BlockSpec((G, 1, cap_c, 1), omap)]
        out_shape += [jax.ShapeDtypeStruct((E, B, cap_c, D_MODEL), BF16),
                      jax.ShapeDtypeStruct((E, B, cap_c, 1), F32)]
    tile_map = lambda b, g, j, cnt_ref: (b, g, j)
    grid_spec = pltpu.PrefetchScalarGridSpec(
        num_scalar_prefetch=1,
        grid=(B, E // G, S // ROW_TILE),
        in_specs=[pl.BlockSpec((1, G, ROW_TILE), tile_map),
                  pl.BlockSpec((1, G, ROW_TILE), tile_map),
                  pl.BlockSpec((1, ROW_TILE, D_MODEL), lambda b, g, j, cnt_ref: (b, j, 0))],
        out_specs=out_specs,
    )
    return pl.pallas_call(
        kern,
        grid_spec=grid_spec,
        out_shape=out_shape,
        name="gather_rows",
        compiler_params=_cparams(("arbitrary", "arbitrary", "arbitrary")),
    )(cnt, posm, aff, h2)


def _expert_kernel(*refs, row_chunk, n_sets):
    x_refs = refs[:n_sets]
    g_refs = refs[n_sets:2 * n_sets]
    wg_ref, wu_ref, wd_ref = refs[2 * n_sets:2 * n_sets + 3]
    y_refs = refs[2 * n_sets + 3:3 * n_sets + 3]
    acc_refs = refs[3 * n_sets + 3:]
    f = pl.program_id(1)
    nf = pl.num_programs(1)
    wg = wg_ref[0, 0].astype(BF16)
    wu = wu_ref[0, 0].astype(BF16)
    wd = wd_ref[0, 0].astype(BF16)

    @pl.when(f == 0)
    def _():
        for acc_ref in acc_refs:
            acc_ref[...] = jnp.zeros_like(acc_ref)

    for x_ref, acc_ref in zip(x_refs, acc_refs):
        rows = x_ref.shape[1]
        step = min(row_chunk, rows)
        for r in range(rows // step):
            rs = slice(r * step, (r + 1) * step)
            x = x_ref[0, rs, :]
            g = _dot(x, wg)
            u = _dot(x, wu)
            hid = (g * jax.nn.sigmoid(g) * u).astype(BF16)
            acc_ref[rs, :] += _dot(hid, wd)

    @pl.when(f == nf - 1)
    def _():
        for g_ref, y_ref, acc_ref in zip(g_refs, y_refs, acc_refs):
            y_ref[0] = (acc_ref[...] * g_ref[0][:, 0:1]).astype(BF16)


def _expert_mlp(x_sets, g_sets, w_gate, w_up, w_down, layer, row_chunk):
    E = N_EXPERTS
    nf = D_EXPERT // FF_TILE
    n_sets = len(x_sets)
    kern = functools.partial(_expert_kernel, row_chunk=row_chunk, n_sets=n_sets)
    xspec = lambda r: pl.BlockSpec((1, r, D_MODEL), lambda e, f: (e, 0, 0))
    gspec = lambda gs: pl.BlockSpec((1,) + gs.shape[1:], lambda e, f: (e, 0, 0))
    return pl.pallas_call(
        kern,
        grid=(E, nf),
        in_specs=[xspec(xs.shape[1]) for xs in x_sets] + [gspec(gs) for gs in g_sets] + [
            pl.BlockSpec((1, 1, D_MODEL, FF_TILE), lambda e, f: (layer, e, 0, f)),
            pl.BlockSpec((1, 1, D_MODEL, FF_TILE), lambda e, f: (layer, e, 0, f)),
            pl.BlockSpec((1, 1, FF_TILE, D_MODEL), lambda e, f: (layer, e, f, 0))],
        out_specs=[xspec(xs.shape[1]) for xs in x_sets],
        out_shape=[jax.ShapeDtypeStruct(xs.shape, BF16) for xs in x_sets],
        scratch_shapes=[pltpu.VMEM(xs.shape[1:], F32) for xs in x_sets],
        name="expert_mlp",
        compiler_params=_cparams(("arbitrary", "arbitrary")),
    )(*x_sets, *g_sets, w_gate, w_up, w_down)


def _combine_kernel(cnt_ref, x_ref, pos_ref, *refs, cap_l, cap_c, final):
    if cap_c:
        yl_ref, il_ref, yc_ref, mod_ref, fg_ref, o_ref, acc_ref, pk_ref, tk_ref = refs
    else:
        yl_ref, il_ref, mod_ref, fg_ref, o_ref, acc_ref, pk_ref, tk_ref = refs
    b = pl.program_id(0)
    i = pl.program_id(1)
    row = jnp.where(i == 0, 4, b) if cap_c else b
    gate = mod_ref[pl.ds(row, 1), 5 * D_MODEL:6 * D_MODEL]
    pos = pos_ref[0]
    acc_ref[...] = jnp.zeros_like(acc_ref)
    GR = PACK_GRANULE
    W = min(GATHER_WINDOW, cap_l)

    def add_latent():
        t = i - 1 if cap_c else i
        starts = [jnp.minimum(_div_pow2(cnt_ref[b, e, t], GR) * GR, cap_l - W)
                  for e in range(N_EXPERTS)]
        for e in range(N_EXPERTS):
            a = pl.multiple_of(starts[e], GR)
            pk_ref[e * W:(e + 1) * W, :] = yl_ref[e, 0, pl.ds(a, W), :]
            tk_ref[e * W:(e + 1) * W, :] = il_ref[e, 0, pl.ds(a, W), 1:2].astype(I32)

        def product(tok_ids, rows):
            lane = lax.broadcasted_iota(I32, (tok_ids.shape[0], ROW_TILE), 1) + t * ROW_TILE
            onehot = jnp.where(tok_ids == lane, 1.0, 0.0)
            return _dot(onehot.T.astype(BF16), rows)

        acc_ref[...] += product(tk_ref[...], pk_ref[...])
        over = [cnt_ref[b, e, t + 1] - (starts[e] + W) for e in range(N_EXPERTS)]
        worst = over[0]
        for o in over[1:]:
            worst = jnp.maximum(worst, o)

        @pl.when(worst > 0)
        def _():
            wx = min(2 * W, cap_l)
            slot = lax.broadcasted_iota(I32, (wx, 1), 0)
            for e in range(N_EXPERTS):
                def extra_window(k, carry, e=e):
                    first_slot = starts[e] + W + k * wx
                    a = pl.multiple_of(jnp.minimum(first_slot, cap_l - wx), GR)
                    ids = jnp.where(slot + a >= first_slot,
                                    il_ref[e, 0, pl.ds(a, wx), 1:2].astype(I32), -1)
                    acc_ref[...] += product(ids, yl_ref[e, 0, pl.ds(a, wx), :])
                    return carry

                lax.fori_loop(0, _div_pow2(jnp.maximum(over[e], 0) + (wx - 1), wx),
                              extra_window, 0)

    def add_context():
        ncol = yc_ref.shape[1]
        lane = lax.broadcasted_iota(I32, (ROW_TILE, ncol), 1)
        for e in range(N_EXPERTS):
            pe = pos[:, e:e + 1]
            tgt = jnp.where(pe >= 0, pe + b * cap_c, -1)
            acc_ref[...] += _dot(jnp.where(tgt == lane, 1.0, 0.0).astype(BF16), yc_ref[e])

    if cap_c:
        pl.when(i == 0)(add_context)
        pl.when(i > 0)(add_latent)
    else:
        add_latent()
    x2 = x_ref[0] + gate * acc_ref[...]
    if final:
        var = jnp.mean(x2 * x2, axis=-1, keepdims=True)
        x2 = x2 * lax.rsqrt(var + EPS) * fg_ref[...]
    o_ref[0] = x2


def _combine(cnt, x1, pos_t, yl, il, yc, mod_l, final_g, cap_l, cap_c, final):
    B, S, _ = x1.shape
    E = N_EXPERTS
    rmap = lambda b, i, cnt_ref: (b, i, 0)
    const2 = lambda b, i, cnt_ref: (0, 0)
    kern = functools.partial(_combine_kernel, cap_l=cap_l, cap_c=cap_c, final=final)
    per_sample = lambda b, i, cnt_ref: (0, b, 0, 0)
    y_specs = [pl.BlockSpec((E, 1, cap_l, D_MODEL), per_sample, pipeline_mode=pl.Buffered(1)),
               pl.BlockSpec((E, 1, cap_l, 2), per_sample, pipeline_mode=pl.Buffered(1))]
    y_args = [yl, il]
    pack_rows = E * min(GATHER_WINDOW, cap_l)
    if cap_c:
        y_specs.append(pl.BlockSpec(yc.shape, lambda b, i, cnt_ref: (0, 0, 0),
                                    pipeline_mode=pl.Buffered(1)))
        y_args.append(yc)
    grid_spec = pltpu.PrefetchScalarGridSpec(
        num_scalar_prefetch=1,
        grid=(B, S // ROW_TILE),
        in_specs=[pl.BlockSpec((1, ROW_TILE, D_MODEL), rmap),
                  pl.BlockSpec((1, ROW_TILE, E), rmap)] + y_specs + [
                  pl.BlockSpec((8, N_MOD * D_MODEL), const2),
                  pl.BlockSpec((1, D_MODEL), const2)],
        out_specs=pl.BlockSpec((1, ROW_TILE, D_MODEL), rmap),
        scratch_shapes=[pltpu.VMEM((ROW_TILE, D_MODEL), F32),
                        pltpu.VMEM((pack_rows, D_MODEL), BF16),
                        pltpu.VMEM((pack_rows, 1), I32)],
    )
    return pl.pallas_call(
        kern,
        grid_spec=grid_spec,
        out_shape=jax.ShapeDtypeStruct((B, S, D_MODEL), F32),
        name="combine",
        compiler_params=_cparams(("arbitrary", "arbitrary")),
    )(cnt, x1, pos_t, *y_args, mod_l, final_g)


def _rope_tables(n_lat, ctx_len):
    t = np.arange(n_lat)
    pos = np.stack([t // GRID_W, t % GRID_W], axis=-1).astype(np.float32)
    inv = (ROPE_BASE ** (-np.arange(ROPE_FREQS, dtype=np.float32) / ROPE_FREQS)).astype(np.float32)
    ang = pos[:, :, None] * inv
    cos = np.cos(ang).astype(np.float32)
    sin = np.sin(ang).astype(np.float32)
    cos64 = np.concatenate([cos[:, 0], cos[:, 0], cos[:, 1], cos[:, 1]], axis=-1)
    sin64 = np.concatenate([-sin[:, 0], sin[:, 0], -sin[:, 1], sin[:, 1]], axis=-1)
    cos_t = np.concatenate([np.ones((ctx_len, 64), np.float32), cos64], axis=0)
    sin_t = np.concatenate([np.zeros((ctx_len, 64), np.float32), sin64], axis=0)
    return (jnp.asarray(np.tile(cos_t, (1, 2))), jnp.asarray(np.tile(sin_t, (1, 2))))


def kernel(x, c, ctx, c_ctx, ada_w, ada_b, norm1_g, norm2_g, w_in, attn_sink, hgrn_lb,
           hgrn_norm_g, conv_w, w_o, router_w, exp_w_gate, exp_w_up, exp_w_down, final_norm_g):
    B, T, D = x.shape
    L = ctx.shape[1]
    depth = ada_w.shape[0]
    assert D == D_MODEL and L == ROW_TILE and T % ROW_TILE == 0 and B <= 4
    S = L + T
    cap_l = EC_CAPACITY * T // N_EXPERTS
    cap_c = EC_CAPACITY * L // N_EXPERTS
    assert cap_l % 16 == 0 and cap_c % 16 == 0 and (B * cap_c) % 16 == 0

    cos_t, sin_t = _rope_tables(T, L)
    gamma = jax.nn.softmax(hgrn_lb.astype(F32), axis=0)
    lb_all = jnp.cumsum(gamma, axis=0) - gamma[0]
    cvec = jnp.concatenate([c, jnp.zeros((4 - B, D), F32), c_ctx[None],
                            jnp.zeros((3, D), F32)], axis=0)
    mod = _modulation(cvec, ada_w, ada_b)
    xs = (ctx, x)

    def interleave_heads(w, axis):
        shp = w.shape
        w = w.reshape(shp[:axis] + (N_KV, GQA_GROUP, HEAD_DIM) + shp[axis + 1:])
        return jnp.swapaxes(w, axis, axis + 1).reshape(shp)

    for l in range(depth):
        last = l == depth - 1
        w_in_l = w_in[l].astype(BF16)
        w_in_l = jnp.concatenate([w_in_l[:, :1024], interleave_heads(w_in_l[:, 1024:1536], 1),
                                  w_in_l[:, 1536:]], axis=1)
        w_o_l = w_o[l].astype(BF16)
        w_o_l = jnp.concatenate([interleave_heads(w_o_l[:ATT_W], 0), w_o_l[ATT_W:]], axis=0)
        q, kv, hg, hqg, cv = _in_projection(xs, mod[l], norm1_g[l][None], w_in_l, cos_t, sin_t)
        att = _attention(q, kv, attn_sink[l], L, skip_ctx=last)
        o2 = _hgrn(hg, hqg, lb_all[l], L)
        x1, h2, aff = _mix_out(xs, att, o2, hqg, cv, mod[l], hgrn_norm_g[l][None], conv_w[l],
                               w_o_l, norm2_g[l][None], router_w[l].T, skip_ctx=last)
        lm, cc = (0, 0) if last else (L, cap_c)
        segments = ((lm, T, cap_l),) if last else ((0, L, cap_c), (L, T, cap_l))
        posm, cnt = _topk_positions(aff, segments)
        gathered = _gather_rows(cnt, posm, aff, h2, cap_l, cc)
        il = gathered[1]
        x_sets = [a.reshape(N_EXPERTS, -1, D) for a in gathered[0::2]]
        g_sets = [a.reshape(N_EXPERTS, -1, a.shape[-1]) for a in gathered[1::2]]
        ys = _expert_mlp(x_sets, g_sets, exp_w_gate, exp_w_up, exp_w_down, l, EXPERT_ROWS)
        yl = ys[0].reshape(N_EXPERTS, B, cap_l, D)
        xs = _combine(cnt, x1, jnp.swapaxes(posm, 1, 2), yl, il, None if last else ys[1], mod[l],
                      final_norm_g[None], cap_l, cc, final=last)
    return xs
```

```python
import functools

import numpy as np
import jax
import jax.numpy as jnp
from jax import lax
from jax.experimental import pallas as pl
from jax.experimental.pallas import tpu as pltpu

F32 = jnp.float32
BF16 = jnp.bfloat16
I32 = jnp.int32
HIGHEST = lax.Precision.HIGHEST

D_MODEL = 1024
GRID_W = 64
EPS = 1e-6
LB_FLOOR = 1e-30
N_MOD = 6
ATT_W = 512
HG_W = 256
CV_W = 256
HEAD_DIM = 64
N_Q = 8
N_KV = 2
GQA_GROUP = 4
KV_W = 128
ROPE_BASE = 10000.0
ROPE_FREQS = 16
HG_HEADS = 4
N_EXPERTS = 16
EC_CAPACITY = 2
D_EXPERT = 2048
IN_COLS = 2816

ROW_TILE = 256
ATT_BLOCK = 128
HG_CHUNK = 128
HG_LEVELS = 7
HG_BATCH = 4
HG_LOCAL = 64
HG_LOCAL_MAX_LOG = 115.0
GATHER_GROUP = 16
GATHER_WINDOW = 64
PACK_GRANULE = 16
FF_TILE = 512
EXPERT_ROWS = 512
MOD_TILE = 1536
VMEM_LIMIT = 56 * 1024 * 1024

NEG_BIG = -1e30
LOG2E = 1.4426950408889634


def _cparams(sem):
    return pltpu.CompilerParams(dimension_semantics=sem, vmem_limit_bytes=VMEM_LIMIT)


def _dot(a, b):
    return jnp.dot(a, b, preferred_element_type=F32)


def _dot_nt(a, b):
    return lax.dot_general(a, b, (((1,), (1,)), ((), ())), preferred_element_type=F32)


def _mod_kernel(a_ref, w_ref, b_ref, o_ref):
    a = a_ref[...]
    a = a * jax.nn.sigmoid(a)
    o_ref[0] = jnp.dot(a, w_ref[0], precision=HIGHEST, preferred_element_type=F32) + b_ref[0]


def _modulation(cvec, ada_w, ada_b):
    depth = ada_w.shape[0]
    ncol = ada_w.shape[2]
    return pl.pallas_call(
        _mod_kernel,
        grid=(depth, ncol // MOD_TILE),
        in_specs=[
            pl.BlockSpec((8, D_MODEL), lambda l, j: (0, 0)),
            pl.BlockSpec((1, D_MODEL, MOD_TILE), lambda l, j: (l, 0, j)),
            pl.BlockSpec((1, 1, MOD_TILE), lambda l, j: (l, 0, j)),
        ],
        out_specs=pl.BlockSpec((1, 8, MOD_TILE), lambda l, j: (l, 0, j)),
        out_shape=jax.ShapeDtypeStruct((depth, 8, ncol), F32),
        name="modulation",
        compiler_params=_cparams(("arbitrary", "arbitrary")),
    )(cvec, ada_w, ada_b.reshape(depth, 1, ncol))


def _swap_halves(x):
    n = x.shape[-1]
    lane = lax.broadcasted_iota(I32, x.shape, x.ndim - 1)
    up = pltpu.roll(x, n - ROPE_FREQS, x.ndim - 1)
    dn = pltpu.roll(x, ROPE_FREQS, x.ndim - 1)
    return jnp.where((lane % (2 * ROPE_FREQS)) < ROPE_FREQS, up, dn)


def _rmsnorm_mod(x, g, shift, scale):
    var = jnp.mean(x * x, axis=-1, keepdims=True)
    y = x * lax.rsqrt(var + EPS) * g
    return y * (1.0 + scale) + shift


def _row_sources(src):
    if isinstance(src, tuple):
        ctx, lat = src
        return ctx, lat, 1, ctx.shape[1] + lat.shape[1]
    return src, src, 0, src.shape[1]


def _inproj_kernel(xc_ref, x_ref, mod_ref, g_ref, w_ref, cos_ref, sin_ref,
                   q_ref, kv_ref, hg_ref, hqg_ref, cv_ref):
    b = pl.program_id(0)
    i = pl.program_id(1)
    row = jnp.where(i == 0, 4, b)
    shift = mod_ref[pl.ds(row, 1), 0:D_MODEL]
    scale = mod_ref[pl.ds(row, 1), D_MODEL:2 * D_MODEL]
    xin = jnp.where(i == 0, xc_ref[0], x_ref[0])
    h = _rmsnorm_mod(xin, g_ref[...], shift, scale)
    p = _dot(h.astype(BF16), w_ref[...])
    cos2 = cos_ref[...]
    sin2 = sin_ref[...]
    k = p[:, 0:KV_W]
    k = k * cos2 + _swap_halves(k) * sin2
    kv_ref[0, :, 0:KV_W] = k.astype(BF16)
    kv_ref[0, :, KV_W:2 * KV_W] = p[:, KV_W:2 * KV_W].astype(BF16)
    hg_ref[0] = p[:, 256:1024]
    q = p[:, 1024:1536]
    cos8 = jnp.concatenate([cos2] * 4, axis=1)
    sin8 = jnp.concatenate([sin2] * 4, axis=1)
    q = (q * cos8 + _swap_halves(q) * sin8) * (HEAD_DIM ** -0.5 * LOG2E)
    q_ref[0] = q.astype(BF16)
    hqg_ref[0] = p[:, 1536:2048]
    cv_ref[0] = p[:, 2048:2816]


def _in_projection(src, mod_l, g, w_bf16, cos_t, sin_t):
    xc, xl, off, S = _row_sources(src)
    B = xc.shape[0]
    nt = S // ROW_TILE
    row_map = lambda b, i: (b, i, 0)
    const2 = lambda b, i: (0, 0)
    return pl.pallas_call(
        _inproj_kernel,
        grid=(B, nt),
        in_specs=[
            pl.BlockSpec((1, ROW_TILE, D_MODEL), lambda b, i: (b, 0, 0)),
            pl.BlockSpec((1, ROW_TILE, D_MODEL), lambda b, i: (b, jnp.maximum(i - off, 0), 0)),
            pl.BlockSpec((8, N_MOD * D_MODEL), const2),
            pl.BlockSpec((1, D_MODEL), const2),
            pl.BlockSpec((D_MODEL, IN_COLS), const2),
            pl.BlockSpec((ROW_TILE, 2 * HEAD_DIM), lambda b, i: (i, 0)),
            pl.BlockSpec((ROW_TILE, 2 * HEAD_DIM), lambda b, i: (i, 0)),
        ],
        out_specs=[
            pl.BlockSpec((1, ROW_TILE, ATT_W), row_map),
            pl.BlockSpec((1, ROW_TILE, 2 * KV_W), row_map),
            pl.BlockSpec((1, ROW_TILE, 3 * HG_W), row_map),
            pl.BlockSpec((1, ROW_TILE, 2 * HG_W), row_map),
            pl.BlockSpec((1, ROW_TILE, 3 * CV_W), row_map),
        ],
        out_shape=[
            jax.ShapeDtypeStruct((B, S, ATT_W), BF16),
            jax.ShapeDtypeStruct((B, S, 2 * KV_W), BF16),
            jax.ShapeDtypeStruct((B, S, 3 * HG_W), F32),
            jax.ShapeDtypeStruct((B, S, 2 * HG_W), F32),
            jax.ShapeDtypeStruct((B, S, 3 * CV_W), F32),
        ],
        name="in_projection",
        compiler_params=_cparams(("arbitrary", "arbitrary")),
    )(xc, xl, mod_l, g, w_bf16, cos_t, sin_t)


def _attn_kernel(sink_ref, q_ref, kp_ref, kc_ref, kn_ref, kx_ref, o_ref, *, blk0, nblk, nctx):
    n = pl.program_id(1) + blk0
    is_lat = n >= nctx
    has_prev = n > nctx
    has_next = n < nblk - 1
    W = ATT_BLOCK
    q = q_ref[0]
    qrows = jnp.concatenate([q[:, g * W:(g + 1) * W] for g in range(GQA_GROUP)], axis=0)
    kv_all = jnp.concatenate([kp_ref[0], kc_ref[0], kn_ref[0], kx_ref[0]], axis=0)
    nkeys = kv_all.shape[0]
    k_all = kv_all[:, 0:KV_W]
    v_ext = jnp.concatenate([kv_all[:, KV_W:2 * KV_W], jnp.ones((nkeys, KV_W), BF16)], axis=1)
    rows = GQA_GROUP * W
    ri = lax.broadcasted_iota(I32, (rows, W), 0) % W
    cj = lax.broadcasted_iota(I32, (rows, W), 1)
    m_prev = (cj >= ri) & has_prev
    m_cur = jnp.broadcast_to(is_lat, (rows, W))
    m_next = (cj <= ri) & (has_next & is_lat)
    grp = lax.broadcasted_iota(I32, (rows, 1), 0) // W
    low = cj < HEAD_DIM
    outs = []
    for h in range(N_KV):
        qh = jnp.where(low if h == 0 else jnp.logical_not(low), qrows, jnp.zeros_like(qrows))
        sink = jnp.zeros((rows, 1), F32)
        for g in range(GQA_GROUP):
            sink = jnp.where(grp == g, sink_ref[h * GQA_GROUP + g] * LOG2E, sink)
        s = _dot_nt(qh, k_all)
        segs = [jnp.where(m_prev, s[:, 0:W], NEG_BIG),
                jnp.where(m_cur, s[:, W:2 * W], NEG_BIG),
                jnp.where(m_next, s[:, 2 * W:3 * W], NEG_BIG)]
        segs += [s[:, c:c + W] for c in range(3 * W, nkeys, W)]
        mx = segs[0]
        for sg in segs[1:]:
            mx = jnp.maximum(mx, sg)
        m = jnp.maximum(jnp.max(mx, axis=1, keepdims=True), sink)
        p = jnp.concatenate([jnp.exp2(sg - m).astype(BF16) for sg in segs], axis=1)
        oe = _dot(p, v_ext)
        den = oe[:, KV_W:2 * KV_W] + jnp.exp2(sink - m)
        outs.append(oe[:, 0:KV_W] / den)
    o = jnp.where(low, outs[0], outs[1])
    for g in range(GQA_GROUP):
        o_ref[0, :, g * W:(g + 1) * W] = o[g * W:(g + 1) * W].astype(BF16)


def _attention(q, kv, sink, ctx_len, skip_ctx):
    B, S, _ = q.shape
    nblk = S // ATT_BLOCK
    nctx = ctx_len // ATT_BLOCK
    blk0 = nctx if skip_ctx else 0
    blk = lambda f: (lambda b, n: (b, f(n + blk0), 0))
    kern = functools.partial(_attn_kernel, blk0=blk0, nblk=nblk, nctx=nctx)
    return pl.pallas_call(
        kern,
        grid=(B, nblk - blk0),
        in_specs=[
            pl.BlockSpec(memory_space=pltpu.SMEM),
            pl.BlockSpec((1, ATT_BLOCK, ATT_W), blk(lambda n: n)),
            pl.BlockSpec((1, ATT_BLOCK, 2 * KV_W), blk(lambda n: jnp.maximum(n - 1, 0))),
            pl.BlockSpec((1, ATT_BLOCK, 2 * KV_W), blk(lambda n: n)),
            pl.BlockSpec((1, ATT_BLOCK, 2 * KV_W), blk(lambda n: jnp.minimum(n + 1, nblk - 1))),
            pl.BlockSpec((1, ctx_len, 2 * KV_W), lambda b, n: (b, 0, 0)),
        ],
        out_specs=pl.BlockSpec((1, ATT_BLOCK, ATT_W), lambda b, n: (b, n, 0)),
        out_shape=jax.ShapeDtypeStruct((B, S - blk0 * ATT_BLOCK, ATT_W), BF16),
        name="attention",
        compiler_params=_cparams(("arbitrary", "arbitrary")),
    )(sink, q, kv, kv, kv, kv)


def _hgrn_constants():
    C = HG_CHUNK
    t = np.arange(C)[:, None]
    r = np.arange(C)[None, :]
    tri = np.stack([r <= t, r >= t]).astype(np.float32)
    x = t ^ r
    lvl = np.where(x > 0, np.floor(np.log2(np.maximum(x, 1))).astype(np.int32), HG_LEVELS)
    lvl_f = np.where(t >= r, lvl, -1).astype(np.int32)
    lvl_b = np.where(t <= r, lvl, -1).astype(np.int32)
    lvl2 = np.stack([np.tile(lvl_f, (1, HG_HEADS)), np.tile(lvl_b, (1, HG_HEADS))])
    return tri, lvl2


def _span_row(x, span, row):
    C = x.shape[0]
    if span >= 8:
        x3 = x.reshape(C // span, span, x.shape[1])
        return jnp.broadcast_to(x3[:, row:row + 1, :], x3.shape).reshape(x.shape)
    pos = lax.broadcasted_iota(I32, x.shape, 0) % span
    out = x
    for p in range(span):
        if p != row:
            out = jnp.where(pos == p, pltpu.roll(x, (p - row) % C, 0), out)
    return out


def _hgrn_prepare(v, z, qr, lb, tri, backward):
    C = HG_CHUNK
    logf = jnp.log(jnp.maximum(lb, LB_FLOOR) + (1.0 - lb) * jax.nn.sigmoid(z)) * LOG2E
    k = (1.0 - lb) * jax.nn.sigmoid(-z)
    q = qr * jax.nn.sigmoid(qr)
    hi = logf.astype(BF16)
    lo = (logf - hi.astype(F32)).astype(BF16)
    cs = _dot(tri, jnp.concatenate([hi, lo], axis=1))
    lam = cs[:, 0:HG_W] + cs[:, HG_W:2 * HG_W]
    tot = lam[0:1] if backward else lam[C - 1:C]
    local = lam - _span_row(lam, HG_LOCAL, HG_LOCAL // 2)
    return dict(v=v, q=q, k=k, lam=lam, tot=tot, local=local)


def _hgrn_finish(g, head_masks, same_head, lvl, st_ref, sidx, backward, fast):
    C = HG_CHUNK
    q, k, v, lam, tot = g["q"], g["k"], g["v"], g["lam"], g["tot"]
    zero = jnp.zeros((C, HG_W), BF16)

    def per_head_rows(x):
        return jnp.concatenate([jnp.where(hm, x, zero) for hm in head_masks], axis=0)

    qb = q.astype(BF16)
    kb = k.astype(BF16)
    if fast:
        n_local = HG_LOCAL.bit_length() - 1
        a = jnp.where((lvl == HG_LEVELS) | ((lvl >= 0) & (lvl < n_local)),
                      _dot_nt((q * jnp.exp2(g["local"])).astype(BF16),
                              per_head_rows((k * jnp.exp2(-g["local"])).astype(BF16))), 0.0)
        levels = range(n_local, HG_LEVELS)
    else:
        a = jnp.where(lvl == HG_LEVELS, _dot_nt(qb, per_head_rows(kb)), 0.0)
        levels = range(HG_LEVELS)
    for l in levels:
        m = 1 << l
        ref = _span_row(lam, 2 * m, m if backward else m - 1)
        fac = jnp.exp2(-jnp.abs(lam - ref)).astype(BF16)
        a = jnp.where(lvl == l, _dot_nt(qb * fac, per_head_rows(kb * fac)), a)
    st = st_ref[sidx]
    q_in = (q * jnp.exp2(lam)).astype(BF16)
    o = _dot(a.astype(BF16), per_head_rows(v.astype(BF16))) + _dot_nt(q_in, st.astype(BF16))
    k_out = (k * jnp.exp2(tot - lam)).astype(BF16)
    upd = _dot(v.T.astype(BF16), k_out)
    st_ref[sidx] = st * jnp.exp2(tot) + jnp.where(same_head, upd, 0.0)
    return o


def _hgrn_kernel(vf_ref, zf_ref, qf_ref, vb_ref, zb_ref, qb_ref, lb_ref, tri_ref, lvl_ref,
                 of_ref, ob_ref, st_ref):
    j = pl.program_id(1)

    @pl.when(j == 0)
    def _():
        st_ref[...] = jnp.zeros_like(st_ref)

    head_id = (lax.broadcasted_iota(I32, (HG_CHUNK, HG_W), 1) // HEAD_DIM).astype(F32).astype(BF16)
    head_masks = [head_id == float(h) for h in range(HG_HEADS)]
    same_head = (lax.broadcasted_iota(I32, (HG_W, HG_W), 0) // HEAD_DIM
                 == lax.broadcasted_iota(I32, (HG_W, HG_W), 1) // HEAD_DIM)
    streams = []
    for d, (v_ref, z_ref, q_ref, o_ref) in enumerate(
            ((vf_ref, zf_ref, qf_ref, of_ref), (vb_ref, zb_ref, qb_ref, ob_ref))):
        for s in range(v_ref.shape[0]):
            g = _hgrn_prepare(v_ref[s], z_ref[s], q_ref[s], lb_ref[d], tri_ref[d], d == 1)
            streams.append((d, s, o_ref, g))
    worst = jnp.abs(streams[0][3]["local"])
    for _, _, _, g in streams[1:]:
        worst = jnp.maximum(worst, jnp.abs(g["local"]))
    can_use_local = jnp.max(worst) <= HG_LOCAL_MAX_LOG

    def run(fast):
        for d, s, o_ref, g in streams:
            o_ref[s] = _hgrn_finish(g, head_masks, same_head, lvl_ref[d], st_ref, 2 * s + d,
                                    backward=(d == 1), fast=fast)

    pl.when(can_use_local)(functools.partial(run, True))
    pl.when(jnp.logical_not(can_use_local))(functools.partial(run, False))


def _hgrn(hg, hqg, lb_l, ctx_len):
    B, S, _ = hg.shape
    nc = S // HG_CHUNK
    nctx = ctx_len // HG_CHUNK
    tri_np, lvl_np = _hgrn_constants()
    tri = jnp.asarray(tri_np, BF16)
    lvl = jnp.asarray(lvl_np, I32)

    def back(j):
        return jnp.where(j < nctx, nctx - 1 - j, nc - 1 - (j - nctx))

    bs = HG_BATCH if B % HG_BATCH == 0 else 1
    blk = (bs, HG_CHUNK, HG_W)
    fwd = lambda col: (lambda b, j: (b, j, col))
    bwd = lambda col: (lambda b, j: (b, back(j), col))
    const3 = lambda b, j: (0, 0, 0)
    return pl.pallas_call(
        _hgrn_kernel,
        grid=(B // bs, nc),
        in_specs=[
            pl.BlockSpec(blk, fwd(0)), pl.BlockSpec(blk, fwd(1)), pl.BlockSpec(blk, fwd(0)),
            pl.BlockSpec(blk, bwd(0)), pl.BlockSpec(blk, bwd(2)), pl.BlockSpec(blk, bwd(0)),
            pl.BlockSpec((2, 1, HG_W), const3),
            pl.BlockSpec((2, HG_CHUNK, HG_CHUNK), const3),
            pl.BlockSpec((2, HG_CHUNK, HG_HEADS * HG_CHUNK), const3),
        ],
        out_specs=[pl.BlockSpec(blk, fwd(0)), pl.BlockSpec(blk, bwd(0))],
        out_shape=[jax.ShapeDtypeStruct((B, S, HG_W), F32)] * 2,
        scratch_shapes=[pltpu.VMEM((2 * bs, HG_W, HG_W), F32)],
        name="hgrn_scan",
        compiler_params=_cparams(("arbitrary", "arbitrary")),
    )(hg, hg, hqg, hg, hg, hqg, lb_l.reshape(2, 1, HG_W), tri, lvl)


def _mixout_kernel(xc_ref, x_ref, att_ref, of_ref, ob_ref, g_ref, cv_ref, cprev_ref, cnext_ref,
                   mod_ref, gain_ref, cw_ref, wo_ref, n2_ref, wr_ref, ones_ref,
                   x1_ref, h2_ref, aff_ref, *, tile0, ntile):
    b = pl.program_id(0)
    i = pl.program_id(1) + tile0
    row = jnp.where(i == 0, 4, b)
    R = ROW_TILE

    def modv(c):
        return mod_ref[pl.ds(row, 1), c * D_MODEL:(c + 1) * D_MODEL]

    o = of_ref[0] + ob_ref[0]
    sq = o * o
    sq_hi = sq.astype(BF16)
    sq_lo = (sq - sq_hi.astype(F32)).astype(BF16)
    ms = _dot(sq_hi, ones_ref[...]) + _dot(sq_lo, ones_ref[...])
    g = g_ref[0]
    hg = o * lax.rsqrt(ms * (1.0 / HEAD_DIM) + EPS) * gain_ref[...] * (g * jax.nn.sigmoid(g))
    cv = cv_ref[0]
    u = cv[:, CV_W:2 * CV_W] * cv[:, 2 * CV_W:3 * CV_W]
    up = cprev_ref[0]
    un = cnext_ref[0]
    u_prev_row = up[7:8, CV_W:2 * CV_W] * up[7:8, 2 * CV_W:3 * CV_W]
    u_next_row = un[0:1, CV_W:2 * CV_W] * un[0:1, 2 * CV_W:3 * CV_W]
    u_prev_row = jnp.where(i <= 1, 0.0, u_prev_row)
    u_next_row = jnp.where((i == 0) | (i == ntile - 1), 0.0, u_next_row)
    ridx = lax.broadcasted_iota(I32, (R, CV_W), 0)
    u_m1 = jnp.where(ridx == 0, u_prev_row, pltpu.roll(u, 1, 0))
    u_p1 = jnp.where(ridx == R - 1, u_next_row, pltpu.roll(u, R - 1, 0))
    cw = cw_ref[...]
    conv = cv[:, 0:CV_W] * (u_m1 * cw[0:1] + u * cw[1:2] + u_p1 * cw[2:3])
    mix = (_dot(att_ref[0], wo_ref[0:ATT_W])
           + _dot(hg.astype(BF16), wo_ref[ATT_W:ATT_W + HG_W])
           + _dot(conv.astype(BF16), wo_ref[ATT_W + HG_W:D_MODEL]))
    x1 = jnp.where(i == 0, xc_ref[0], x_ref[0]) + modv(2) * mix
    x1_ref[0] = x1
    h2 = _rmsnorm_mod(x1, n2_ref[...], modv(3), modv(4))
    h_hi = h2.astype(BF16)
    h2_ref[0] = h_hi
    h_lo = (h2 - h_hi.astype(F32)).astype(BF16)
    wr = wr_ref[...]
    w_hi = wr.astype(BF16)
    w_lo = (wr - w_hi.astype(F32)).astype(BF16)
    r1 = _dot_nt(jnp.concatenate([w_hi, w_lo], axis=0), h_hi)
    logits = r1[0:N_EXPERTS] + r1[N_EXPERTS:2 * N_EXPERTS] + _dot_nt(w_hi, h_lo)
    e = jnp.exp(logits - jnp.max(logits, axis=0, keepdims=True))
    aff_ref[0] = e / jnp.sum(e, axis=0, keepdims=True)


def _mix_out(src, att, o2, hqg, cv, mod_l, gain, conv_w, wo_bf16, n2g, wr_t, skip_ctx):
    xc, xl, off, S = _row_sources(src)
    B = xc.shape[0]
    ntile = S // ROW_TILE
    tile0 = 1 if skip_ctx else 0
    sub = ROW_TILE // 8
    nsub = S // 8
    rmap = lambda b, i: (b, i + tile0, 0)
    omap = lambda b, i: (b, i, 0)
    s_out = S - tile0 * ROW_TILE
    const2 = lambda b, i: (0, 0)
    ones = jnp.asarray(np.kron(np.eye(HG_HEADS), np.ones((HEAD_DIM, HEAD_DIM))), BF16)
    kern = functools.partial(_mixout_kernel, tile0=tile0, ntile=ntile)
    return pl.pallas_call(
        kern,
        grid=(B, ntile - tile0),
        in_specs=[
            pl.BlockSpec((1, ROW_TILE, D_MODEL), lambda b, i: (b, 0, 0)),
            pl.BlockSpec((1, ROW_TILE, D_MODEL),
                         lambda b, i: (b, jnp.maximum(i + tile0 - off, 0), 0)),
            pl.BlockSpec((1, ROW_TILE, ATT_W), omap),
            pl.BlockSpec((1, ROW_TILE, HG_W), rmap),
            pl.BlockSpec((1, ROW_TILE, HG_W), rmap),
            pl.BlockSpec((1, ROW_TILE, HG_W), lambda b, i: (b, i + tile0, 1)),
            pl.BlockSpec((1, ROW_TILE, 3 * CV_W), rmap),
            pl.BlockSpec((1, 8, 3 * CV_W),
                         lambda b, i: (b, jnp.maximum((i + tile0) * sub - 1, 0), 0)),
            pl.BlockSpec((1, 8, 3 * CV_W),
                         lambda b, i: (b, jnp.minimum((i + tile0 + 1) * sub, nsub - 1), 0)),
            pl.BlockSpec((8, N_MOD * D_MODEL), const2),
            pl.BlockSpec((1, HG_W), const2),
            pl.BlockSpec((3, CV_W), const2),
            pl.BlockSpec((D_MODEL, D_MODEL), const2),
            pl.BlockSpec((1, D_MODEL), const2),
            pl.BlockSpec((N_EXPERTS, D_MODEL), const2),
            pl.BlockSpec((HG_W, HG_W), const2),
        ],
        out_specs=[
            pl.BlockSpec((1, ROW_TILE, D_MODEL), omap),
            pl.BlockSpec((1, ROW_TILE, D_MODEL), omap),
            pl.BlockSpec((1, N_EXPERTS, ROW_TILE), lambda b, i: (b, 0, i)),
        ],
        out_shape=[
            jax.ShapeDtypeStruct((B, s_out, D_MODEL), F32),
            jax.ShapeDtypeStruct((B, s_out, D_MODEL), BF16),
            jax.ShapeDtypeStruct((B, N_EXPERTS, s_out), F32),
        ],
        name="mix_out",
        compiler_params=_cparams(("arbitrary", "arbitrary")),
    )(xc, xl, att, o2[0], o2[1], hqg, cv, cv, cv, mod_l, gain, conv_w, wo_bf16, n2g, wr_t, ones)


def _topk_kernel(aff_ref, tri_ref, pos_ref, cnt_ref, *, segments):
    tri = tri_ref[...]

    def excl_cumsum(mask, n):
        carry = jnp.zeros((N_EXPERTS, 1), F32)
        parts = []
        carries = []
        for c in range(n // 128):
            carries.append(carry)
            blk = jnp.where(mask[:, c * 128:(c + 1) * 128], 1.0, 0.0)
            parts.append(_dot(blk.astype(BF16), tri) + carry)
            carry = carry + jnp.sum(blk, axis=1, keepdims=True)
        return jnp.concatenate(parts, axis=1), carries + [carry]

    for (lo, n, k) in segments:
        a = aff_ref[0, :, lo:lo + n]

        def body(it, thr):
            cand = thr | jnp.left_shift(jnp.int32(1), 30 - it)
            cnt = jnp.sum(jnp.where(a >= pltpu.bitcast(cand, F32), 1.0, 0.0), axis=1, keepdims=True)
            return jnp.where(cnt >= k, cand, thr)

        thr = lax.fori_loop(0, 31, body, jnp.zeros((N_EXPERTS, 1), I32))
        above = a >= pltpu.bitcast(thr + 1, F32)
        tied = jnp.logical_and(a >= pltpu.bitcast(thr, F32), jnp.logical_not(above))
        n_above = jnp.sum(jnp.where(above, 1.0, 0.0), axis=1, keepdims=True)
        rank_tied, _ = excl_cumsum(tied, n)
        sel = above | (tied & (rank_tied < (k - n_above)))
        pos, counts = excl_cumsum(sel, n)
        pos_ref[0, :, lo:lo + n] = jnp.where(sel, pos.astype(I32), -1)
    step = ROW_TILE // 128
    lane = lax.broadcasted_iota(I32, (N_EXPERTS, 128), 1)
    cnt = jnp.zeros((N_EXPERTS, 128), F32)
    for c, col in enumerate(counts[0::step]):
        cnt = jnp.where(lane == c, col, cnt)
    cnt_ref[0] = cnt.astype(I32)


def _topk_positions(aff, segments):
    B, E, S = aff.shape
    assert segments[-1][1] // ROW_TILE + 1 <= 128
    tri = jnp.asarray(np.triu(np.ones((128, 128)), 1), BF16)
    kern = functools.partial(_topk_kernel, segments=segments)
    return pl.pallas_call(
        kern,
        grid=(B,),
        in_specs=[pl.BlockSpec((1, E, S), lambda b: (b, 0, 0)),
                  pl.BlockSpec((128, 128), lambda b: (0, 0))],
        out_specs=[pl.BlockSpec((1, E, S), lambda b: (b, 0, 0)),
                   pl.BlockSpec((1, E, 128), lambda b: (b, 0, 0))],
        out_shape=[jax.ShapeDtypeStruct((B, E, S), I32),
                   jax.ShapeDtypeStruct((B, E, 128), I32)],
        name="topk_positions",
        compiler_params=_cparams(("arbitrary",)),
    )(aff, tri)


def _div_pow2(x, d):
    assert d & (d - 1) == 0
    return lax.shift_right_logical(x, jnp.int32(d.bit_length() - 1))


def _gather_kernel(cnt_ref, pos_ref, aff_ref, h_ref, *refs, cap_l, cap_c):
    if cap_c:
        xl_ref, gi_ref, xc_ref, gc_ref = refs
    else:
        xl_ref, gi_ref = refs
    b = pl.program_id(0)
    e0 = pl.program_id(1) * GATHER_GROUP
    j = pl.program_id(2)
    G = GATHER_GROUP
    GR = PACK_GRANULE
    W = min(GATHER_WINDOW, cap_l)
    pos = pos_ref[0]
    aff = aff_ref[0]
    h = h_ref[0]

    @pl.when(j == 0)
    def _():
        xl_ref[...] = jnp.zeros_like(xl_ref)
        gi_ref[...] = jnp.zeros_like(gi_ref)

    first_lane = lax.broadcasted_iota(I32, (W, 2), 1) == 0

    def add_rows(g, start, onehot_f32, rows):
        a = pl.multiple_of(start, GR)
        xl_ref[g, 0, pl.ds(a, W), :] += rows.astype(BF16)
        gate = jnp.sum(onehot_f32 * aff[g:g + 1, :], axis=1, keepdims=True)
        tok_id = jnp.sum(onehot_f32 * tok, axis=1, keepdims=True)
        gi_ref[g, 0, pl.ds(a, W), :] += jnp.where(first_lane, gate, tok_id)

    def latent():
        c = j - 1 if cap_c else j
        slot_i32 = lax.broadcasted_iota(I32, (W, ROW_TILE), 0)
        starts = [jnp.minimum(_div_pow2(cnt_ref[b, e0 + g, c], GR) * GR, cap_l - W)
                  for g in range(G)]
        hits = [(pos[g:g + 1, :] - starts[g]) == slot_i32 for g in range(G)]
        onehots = [jnp.where(hit, 1.0, 0.0) for hit in hits]
        prod = _dot(jnp.concatenate(onehots, axis=0).astype(BF16), h)
        for g in range(G):
            add_rows(g, starts[g], onehots[g], prod[g * W:(g + 1) * W])
        over = [cnt_ref[b, e0 + g, c + 1] - (starts[g] + W) for g in range(G)]
        worst = over[0]
        for o in over[1:]:
            worst = jnp.maximum(worst, o)

        @pl.when(worst > 0)
        def _():
            for g in range(G):
                def extra_window(k, carry, g=g):
                    first_slot = starts[g] + (k + 1) * W
                    start = jnp.minimum(first_slot, cap_l - W)
                    pg = pos[g:g + 1, :]
                    hit = ((pg - start) == slot_i32) & (pg >= first_slot)
                    onehot = jnp.where(hit, 1.0, 0.0)
                    add_rows(g, start, onehot, _dot(onehot.astype(BF16), h))
                    return carry

                lax.fori_loop(0, _div_pow2(jnp.maximum(over[g], 0) + (W - 1), W), extra_window, 0)

    tok = None
    if cap_c:
        tok = (lax.broadcasted_iota(I32, (1, ROW_TILE), 1) + (j - 1) * ROW_TILE).astype(F32)

        @pl.when(j == 0)
        def _():
            slot_c = lax.broadcasted_iota(I32, (cap_c, ROW_TILE), 0)
            onehots = [jnp.where(pos[g:g + 1, :] == slot_c, 1.0, 0.0) for g in range(G)]
            prod = _dot(jnp.concatenate(onehots, axis=0).astype(BF16), h)
            for g in range(G):
                xc_ref[g, 0] = prod[g * cap_c:(g + 1) * cap_c].astype(BF16)
                gc_ref[g, 0] = jnp.sum(onehots[g] * aff[g:g + 1, :], axis=1, keepdims=True)

        pl.when(j > 0)(latent)
    else:
        tok = (lax.broadcasted_iota(I32, (1, ROW_TILE), 1) + j * ROW_TILE).astype(F32)
        latent()


def _gather_rows(cnt, posm, aff, h2, cap_l, cap_c):
    B, E, S = posm.shape
    G = GATHER_GROUP
    assert min(GATHER_WINDOW, cap_l) % PACK_GRANULE == 0 and cap_l % PACK_GRANULE == 0
    kern = functools.partial(_gather_kernel, cap_l=cap_l, cap_c=cap_c)
    omap = lambda b, g, j, cnt_ref: (g, b, 0, 0)
    out_specs = [pl.BlockSpec((G, 1, cap_l, D_MODEL), omap), pl.BlockSpec((G, 1, cap_l, 2), omap)]
    out_shape = [jax.ShapeDtypeStruct((E, B, cap_l, D_MODEL), BF16),
                 jax.ShapeDtypeStruct((E, B, cap_l, 2), F32)]
    if cap_c:
        out_specs += [pl.BlockSpec((G, 1, cap_c, D_MODEL), omap),
                      pl.BlockSpec((G, 1, cap_c, 1), omap)]
        out_shape += [jax.ShapeDtypeStruct((E, B, cap_c, D_MODEL), BF16),
                      jax.ShapeDtypeStruct((E, B, cap_c, 1), F32)]
    tile_map = lambda b, g, j, cnt_ref: (b, g, j)
    grid_spec = pltpu.PrefetchScalarGridSpec(
        num_scalar_prefetch=1,
        grid=(B, E // G, S // ROW_TILE),
        in_specs=[pl.BlockSpec((1, G, ROW_TILE), tile_map),
                  pl.BlockSpec((1, G, ROW_TILE), tile_map),
                  pl.BlockSpec((1, ROW_TILE, D_MODEL), lambda b, g, j, cnt_ref: (b, j, 0))],
        out_specs=out_specs,
    )
    return pl.pallas_call(
        kern,
        grid_spec=grid_spec,
        out_shape=out_shape,
        name="gather_rows",
        compiler_params=_cparams(("arbitrary", "arbitrary", "arbitrary")),
    )(cnt, posm, aff, h2)


def _expert_kernel(*refs, row_chunk, n_sets):
    x_refs = refs[:n_sets]
    g_refs = refs[n_sets:2 * n_sets]
    wg_ref, wu_ref, wd_ref = refs[2 * n_sets:2 * n_sets + 3]
    y_refs = refs[2 * n_sets + 3:3 * n_sets + 3]
    acc_refs = refs[3 * n_sets + 3:]
    f = pl.program_id(1)
    nf = pl.num_programs(1)
    wg = wg_ref[0, 0].astype(BF16)
    wu = wu_ref[0, 0].astype(BF16)
    wd = wd_ref[0, 0].astype(BF16)

    def hidden_tile(first):
        for x_ref, acc_ref in zip(x_refs, acc_refs):
            rows = x_ref.shape[1]
            step = min(row_chunk, rows)
            for r in range(rows // step):
                rs = slice(r * step, (r + 1) * step)
                x = x_ref[0, rs, :]
                g = _dot(x, wg)
                u = _dot(x, wu)
                y = _dot((g * jax.nn.sigmoid(g) * u).astype(BF16), wd)
                if first:
                    acc_ref[rs, :] = y
                else:
                    acc_ref[rs, :] += y

    pl.when(f == 0)(functools.partial(hidden_tile, True))
    pl.when(f > 0)(functools.partial(hidden_tile, False))

    @pl.when(f == nf - 1)
    def _():
        for g_ref, y_ref, acc_ref in zip(g_refs, y_refs, acc_refs):
            y_ref[0] = (acc_ref[...] * g_ref[0][:, 0:1]).astype(BF16)


def _expert_mlp(x_sets, g_sets, w_gate, w_up, w_down, layer, row_chunk):
    E = N_EXPERTS
    nf = D_EXPERT // FF_TILE
    n_sets = len(x_sets)
    kern = functools.partial(_expert_kernel, row_chunk=row_chunk, n_sets=n_sets)
    xspec = lambda r: pl.BlockSpec((1, r, D_MODEL), lambda e, f: (e, 0, 0))
    gspec = lambda gs: pl.BlockSpec((1,) + gs.shape[1:], lambda e, f: (e, 0, 0))
    return pl.pallas_call(
        kern,
        grid=(E, nf),
        in_specs=[xspec(xs.shape[1]) for xs in x_sets] + [gspec(gs) for gs in g_sets] + [
            pl.BlockSpec((1, 1, D_MODEL, FF_TILE), lambda e, f: (layer, e, 0, f)),
            pl.BlockSpec((1, 1, D_MODEL, FF_TILE), lambda e, f: (layer, e, 0, f)),
            pl.BlockSpec((1, 1, FF_TILE, D_MODEL), lambda e, f: (layer, e, f, 0))],
        out_specs=[xspec(xs.shape[1]) for xs in x_sets],
        out_shape=[jax.ShapeDtypeStruct(xs.shape, BF16) for xs in x_sets],
        scratch_shapes=[pltpu.VMEM(xs.shape[1:], F32) for xs in x_sets],
        name="expert_mlp",
        compiler_params=_cparams(("arbitrary", "arbitrary")),
    )(*x_sets, *g_sets, w_gate, w_up, w_down)


def _combine_kernel(cnt_ref, x_ref, pos_ref, *refs, cap_l, cap_c, final):
    if cap_c:
        yl_ref, il_ref, yc_ref, mod_ref, fg_ref, o_ref, acc_ref, pk_ref, tk_ref = refs
    else:
        yl_ref, il_ref, mod_ref, fg_ref, o_ref, acc_ref, pk_ref, tk_ref = refs
    b = pl.program_id(0)
    i = pl.program_id(1)
    row = jnp.where(i == 0, 4, b) if cap_c else b
    gate = mod_ref[pl.ds(row, 1), 5 * D_MODEL:6 * D_MODEL]
    pos = pos_ref[0]
    acc_ref[...] = jnp.zeros_like(acc_ref)
    GR = PACK_GRANULE
    W = min(GATHER_WINDOW, cap_l)

    def add_latent():
        t = i - 1 if cap_c else i
        starts = [jnp.minimum(_div_pow2(cnt_ref[b, e, t], GR) * GR, cap_l - W)
                  for e in range(N_EXPERTS)]
        for e in range(N_EXPERTS):
            a = pl.multiple_of(starts[e], GR)
            pk_ref[e * W:(e + 1) * W, :] = yl_ref[e, 0, pl.ds(a, W), :]
            tk_ref[e * W:(e + 1) * W, :] = il_ref[e, 0, pl.ds(a, W), 1:2].astype(I32)

        def product(tok_ids, rows):
            lane = lax.broadcasted_iota(I32, (tok_ids.shape[0], ROW_TILE), 1) + t * ROW_TILE
            onehot = jnp.where(tok_ids == lane, 1.0, 0.0)
            return _dot(onehot.T.astype(BF16), rows)

        acc_ref[...] += product(tk_ref[...], pk_ref[...])
        over = [cnt_ref[b, e, t + 1] - (starts[e] + W) for e in range(N_EXPERTS)]
        worst = over[0]
        for o in over[1:]:
            worst = jnp.maximum(worst, o)

        @pl.when(worst > 0)
        def _():
            wx = min(2 * W, cap_l)
            slot = lax.broadcasted_iota(I32, (wx, 1), 0)
            for e in range(N_EXPERTS):
                def extra_window(k, carry, e=e):
                    first_slot = starts[e] + W + k * wx
                    a = pl.multiple_of(jnp.minimum(first_slot, cap_l - wx), GR)
                    ids = jnp.where(slot + a >= first_slot,
                                    il_ref[e, 0, pl.ds(a, wx), 1:2].astype(I32), -1)
                    acc_ref[...] += product(ids, yl_ref[e, 0, pl.ds(a, wx), :])
                    return carry

                lax.fori_loop(0, _div_pow2(jnp.maximum(over[e], 0) + (wx - 1), wx),
                              extra_window, 0)

    def add_context():
        ncol = yc_ref.shape[1]
        lane = lax.broadcasted_iota(I32, (ROW_TILE, ncol), 1)
        for e in range(N_EXPERTS):
            pe = pos[:, e:e + 1]
            tgt = jnp.where(pe >= 0, pe + b * cap_c, -1)
            acc_ref[...] += _dot(jnp.where(tgt == lane, 1.0, 0.0).astype(BF16), yc_ref[e])

    if cap_c:
        pl.when(i == 0)(add_context)
        pl.when(i > 0)(add_latent)
    else:
        add_latent()
    x2 = x_ref[0] + gate * acc_ref[...]
    if final:
        var = jnp.mean(x2 * x2, axis=-1, keepdims=True)
        x2 = x2 * lax.rsqrt(var + EPS) * fg_ref[...]
    o_ref[0] = x2


def _combine(cnt, x1, pos_t, yl, il, yc, mod_l, final_g, cap_l, cap_c, final):
    B, S, _ = x1.shape
    E = N_EXPERTS
    rmap = lambda b, i, cnt_ref: (b, i, 0)
    const2 = lambda b, i, cnt_ref: (0, 0)
    kern = functools.partial(_combine_kernel, cap_l=cap_l, cap_c=cap_c, final=final)
    per_sample = lambda b, i, cnt_ref: (0, b, 0, 0)
    y_specs = [pl.BlockSpec((E, 1, cap_l, D_MODEL), per_sample, pipeline_mode=pl.Buffered(1)),
               pl.BlockSpec((E, 1, cap_l, 2), per_sample, pipeline_mode=pl.Buffered(1))]
    y_args = [yl, il]
    pack_rows = E * min(GATHER_WINDOW, cap_l)
    if cap_c:
        y_specs.append(pl.BlockSpec(yc.shape, lambda b, i, cnt_ref: (0, 0, 0),
                                    pipeline_mode=pl.Buffered(1)))
        y_args.append(yc)
    grid_spec = pltpu.PrefetchScalarGridSpec(
        num_scalar_prefetch=1,
        grid=(B, S // ROW_TILE),
        in_specs=[pl.BlockSpec((1, ROW_TILE, D_MODEL), rmap),
                  pl.BlockSpec((1, ROW_TILE, E), rmap)] + y_specs + [
                  pl.BlockSpec((8, N_MOD * D_MODEL), const2),
                  pl.BlockSpec((1, D_MODEL), const2)],
        out_specs=pl.BlockSpec((1, ROW_TILE, D_MODEL), rmap),
        scratch_shapes=[pltpu.VMEM((ROW_TILE, D_MODEL), F32),
                        pltpu.VMEM((pack_rows, D_MODEL), BF16),
                        pltpu.VMEM((pack_rows, 1), I32)],
    )
    return pl.pallas_call(
        kern,
        grid_spec=grid_spec,
        out_shape=jax.ShapeDtypeStruct((B, S, D_MODEL), F32),
        name="combine",
        compiler_params=_cparams(("arbitrary", "arbitrary")),
    )(cnt, x1, pos_t, *y_args, mod_l, final_g)


def _rope_tables(n_lat, ctx_len):
    t = np.arange(n_lat)
    pos = np.stack([t // GRID_W, t % GRID_W], axis=-1).astype(np.float32)
    inv = (ROPE_BASE ** (-np.arange(ROPE_FREQS, dtype=np.float32) / ROPE_FREQS)).astype(np.float32)
    ang = pos[:, :, None] * inv
    cos = np.cos(ang).astype(np.float32)
    sin = np.sin(ang).astype(np.float32)
    cos64 = np.concatenate([cos[:, 0], cos[:, 0], cos[:, 1], cos[:, 1]], axis=-1)
    sin64 = np.concatenate([-sin[:, 0], sin[:, 0], -sin[:, 1], sin[:, 1]], axis=-1)
    cos_t = np.concatenate([np.ones((ctx_len, 64), np.float32), cos64], axis=0)
    sin_t = np.concatenate([np.zeros((ctx_len, 64), np.float32), sin64], axis=0)
    return (jnp.asarray(np.tile(cos_t, (1, 2))), jnp.asarray(np.tile(sin_t, (1, 2))))


def kernel(x, c, ctx, c_ctx, ada_w, ada_b, norm1_g, norm2_g, w_in, attn_sink, hgrn_lb,
           hgrn_norm_g, conv_w, w_o, router_w, exp_w_gate, exp_w_up, exp_w_down, final_norm_g):
    B, T, D = x.shape
    L = ctx.shape[1]
    depth = ada_w.shape[0]
    assert D == D_MODEL and L == ROW_TILE and T % ROW_TILE == 0 and B <= 4
    S = L + T
    cap_l = EC_CAPACITY * T // N_EXPERTS
    cap_c = EC_CAPACITY * L // N_EXPERTS
    assert cap_l % 16 == 0 and cap_c % 16 == 0 and (B * cap_c) % 16 == 0

    cos_t, sin_t = _rope_tables(T, L)
    gamma = jax.nn.softmax(hgrn_lb.astype(F32), axis=0)
    lb_all = jnp.cumsum(gamma, axis=0) - gamma[0]
    cvec = jnp.concatenate([c, jnp.zeros((4 - B, D), F32), c_ctx[None],
                            jnp.zeros((3, D), F32)], axis=0)
    mod = _modulation(cvec, ada_w, ada_b)
    xs = (ctx, x)

    def interleave_heads(w, axis):
        shp = w.shape
        w = w.reshape(shp[:axis] + (N_KV, GQA_GROUP, HEAD_DIM) + shp[axis + 1:])
        return jnp.swapaxes(w, axis, axis + 1).reshape(shp)

    for l in range(depth):
        last = l == depth - 1
        w_in_l = w_in[l].astype(BF16)
        w_in_l = jnp.concatenate([w_in_l[:, :1024], interleave_heads(w_in_l[:, 1024:1536], 1),
                                  w_in_l[:, 1536:]], axis=1)
        w_o_l = w_o[l].astype(BF16)
        w_o_l = jnp.concatenate([interleave_heads(w_o_l[:ATT_W], 0), w_o_l[ATT_W:]], axis=0)
        q, kv, hg, hqg, cv = _in_projection(xs, mod[l], norm1_g[l][None], w_in_l, cos_t, sin_t)
        att = _attention(q, kv, attn_sink[l], L, skip_ctx=last)
        o2 = _hgrn(hg, hqg, lb_all[l], L)
        x1, h2, aff = _mix_out(xs, att, o2, hqg, cv, mod[l], hgrn_norm_g[l][None], conv_w[l],
                               w_o_l, norm2_g[l][None], router_w[l].T, skip_ctx=last)
        lm, cc = (0, 0) if last else (L, cap_c)
        segments = ((lm, T, cap_l),) if last else ((0, L, cap_c), (L, T, cap_l))
        posm, cnt = _topk_positions(aff, segments)
        gathered = _gather_rows(cnt, posm, aff, h2, cap_l, cc)
        il = gathered[1]
        x_sets = [a.reshape(N_EXPERTS, -1, D) for a in gathered[0::2]]
        g_sets = [a.reshape(N_EXPERTS, -1, a.shape[-1]) for a in gathered[1::2]]
        ys = _expert_mlp(x_sets, g_sets, exp_w_gate, exp_w_up, exp_w_down, l, EXPERT_ROWS)
        yl = ys[0].reshape(N_EXPERTS, B, cap_l, D)
        xs = _combine(cnt, x1, jnp.swapaxes(posm, 1, 2), yl, il, None if last else ys[1], mod[l],
                      final_norm_g[None], cap_l, cc, final=last)
    return xs
```

```python
import functools

import numpy as np
import jax
import jax.numpy as jnp
from jax import lax
from jax.experimental import pallas as pl
from jax.experimental.pallas import tpu as pltpu

F32 = jnp.float32
BF16 = jnp.bfloat16
I32 = jnp.int32
HIGHEST = lax.Precision.HIGHEST

D_MODEL = 1024
GRID_W = 64
EPS = 1e-6
LB_FLOOR = 1e-30
N_MOD = 6
ATT_W = 512
HG_W = 256
CV_W = 256
HEAD_DIM = 64
N_Q = 8
N_KV = 2
GQA_GROUP = 4
KV_W = 128
ROPE_BASE = 10000.0
ROPE_FREQS = 16
HG_HEADS = 4
N_EXPERTS = 16
EC_CAPACITY = 2
D_EXPERT = 2048
IN_COLS = 2816

ROW_TILE = 256
ATT_BLOCK = 128
HG_CHUNK = 128
HG_LEVELS = 7
HG_BATCH = 4
HG_LOCAL = 64
HG_LOCAL_FIRST = 32
HG_LOCAL_MAX_LOG = 115.0
GATHER_GROUP = 16
GATHER_WINDOW = 64
PACK_GRANULE = 16
FF_TILE = 512
EXPERT_ROWS = 512
MOD_TILE = 1536
VMEM_LIMIT = 56 * 1024 * 1024

NEG_BIG = -1e30
LOG2E = 1.4426950408889634


def _cparams(sem):
    return pltpu.CompilerParams(dimension_semantics=sem, vmem_limit_bytes=VMEM_LIMIT)


def _dot(a, b):
    return jnp.dot(a, b, preferred_element_type=F32)


def _dot_nt(a, b):
    return lax.dot_general(a, b, (((1,), (1,)), ((), ())), preferred_element_type=F32)


def _mod_kernel(a_ref, w_ref, b_ref, o_ref):
    a = a_ref[...]
    a = a * jax.nn.sigmoid(a)
    o_ref[0] = jnp.dot(a, w_ref[0], precision=HIGHEST, preferred_element_type=F32) + b_ref[0]


def _modulation(cvec, ada_w, ada_b):
    depth = ada_w.shape[0]
    ncol = ada_w.shape[2]
    return pl.pallas_call(
        _mod_kernel,
        grid=(depth, ncol // MOD_TILE),
        in_specs=[
            pl.BlockSpec((8, D_MODEL), lambda l, j: (0, 0)),
            pl.BlockSpec((1, D_MODEL, MOD_TILE), lambda l, j: (l, 0, j)),
            pl.BlockSpec((1, 1, MOD_TILE), lambda l, j: (l, 0, j)),
        ],
        out_specs=pl.BlockSpec((1, 8, MOD_TILE), lambda l, j: (l, 0, j)),
        out_shape=jax.ShapeDtypeStruct((depth, 8, ncol), F32),
        name="modulation",
        compiler_params=_cparams(("arbitrary", "arbitrary")),
    )(cvec, ada_w, ada_b.reshape(depth, 1, ncol))


def _swap_halves(x):
    n = x.shape[-1]
    lane = lax.broadcasted_iota(I32, x.shape, x.ndim - 1)
    up = pltpu.roll(x, n - ROPE_FREQS, x.ndim - 1)
    dn = pltpu.roll(x, ROPE_FREQS, x.ndim - 1)
    return jnp.where((lane % (2 * ROPE_FREQS)) < ROPE_FREQS, up, dn)


def _rmsnorm_mod(x, g, shift, scale):
    var = jnp.mean(x * x, axis=-1, keepdims=True)
    y = x * lax.rsqrt(var + EPS) * g
    return y * (1.0 + scale) + shift


def _row_sources(src):
    if isinstance(src, tuple):
        ctx, lat = src
        return ctx, lat, 1, ctx.shape[1] + lat.shape[1]
    return src, src, 0, src.shape[1]


def _inproj_kernel(xc_ref, x_ref, mod_ref, g_ref, w_ref, cos_ref, sin_ref,
                   q_ref, kv_ref, hg_ref, hqg_ref, cv_ref):
    b = pl.program_id(0)
    i = pl.program_id(1)
    row = jnp.where(i == 0, 4, b)
    shift = mod_ref[pl.ds(row, 1), 0:D_MODEL]
    scale = mod_ref[pl.ds(row, 1), D_MODEL:2 * D_MODEL]
    xin = jnp.where(i == 0, xc_ref[0], x_ref[0])
    h = _rmsnorm_mod(xin, g_ref[...], shift, scale)
    p = _dot(h.astype(BF16), w_ref[...])
    cos2 = cos_ref[...]
    sin2 = sin_ref[...]
    k = p[:, 0:KV_W]
    k = k * cos2 + _swap_halves(k) * sin2
    kv_ref[0, :, 0:KV_W] = k.astype(BF16)
    kv_ref[0, :, KV_W:2 * KV_W] = p[:, KV_W:2 * KV_W].astype(BF16)
    hg_ref[0] = p[:, 256:1024]
    q = p[:, 1024:1536]
    cos8 = jnp.concatenate([cos2] * 4, axis=1)
    sin8 = jnp.concatenate([sin2] * 4, axis=1)
    q = (q * cos8 + _swap_halves(q) * sin8) * (HEAD_DIM ** -0.5 * LOG2E)
    q_ref[0] = q.astype(BF16)
    hqg_ref[0] = p[:, 1536:2048]
    cv_ref[0] = p[:, 2048:2816]


def _in_projection(src, mod_l, g, w_bf16, cos_t, sin_t):
    xc, xl, off, S = _row_sources(src)
    B = xc.shape[0]
    nt = S // ROW_TILE
    row_map = lambda b, i: (b, i, 0)
    const2 = lambda b, i: (0, 0)
    return pl.pallas_call(
        _inproj_kernel,
        grid=(B, nt),
        in_specs=[
            pl.BlockSpec((1, ROW_TILE, D_MODEL), lambda b, i: (b, 0, 0)),
            pl.BlockSpec((1, ROW_TILE, D_MODEL), lambda b, i: (b, jnp.maximum(i - off, 0), 0)),
            pl.BlockSpec((8, N_MOD * D_MODEL), const2),
            pl.BlockSpec((1, D_MODEL), const2),
            pl.BlockSpec((D_MODEL, IN_COLS), const2),
            pl.BlockSpec((ROW_TILE, 2 * HEAD_DIM), lambda b, i: (i, 0)),
            pl.BlockSpec((ROW_TILE, 2 * HEAD_DIM), lambda b, i: (i, 0)),
        ],
        out_specs=[
            pl.BlockSpec((1, ROW_TILE, ATT_W), row_map),
            pl.BlockSpec((1, ROW_TILE, 2 * KV_W), row_map),
            pl.BlockSpec((1, ROW_TILE, 3 * HG_W), row_map),
            pl.BlockSpec((1, ROW_TILE, 2 * HG_W), row_map),
            pl.BlockSpec((1, ROW_TILE, 3 * CV_W), row_map),
        ],
        out_shape=[
            jax.ShapeDtypeStruct((B, S, ATT_W), BF16),
            jax.ShapeDtypeStruct((B, S, 2 * KV_W), BF16),
            jax.ShapeDtypeStruct((B, S, 3 * HG_W), F32),
            jax.ShapeDtypeStruct((B, S, 2 * HG_W), F32),
            jax.ShapeDtypeStruct((B, S, 3 * CV_W), F32),
        ],
        name="in_projection",
        compiler_params=_cparams(("arbitrary", "arbitrary")),
    )(xc, xl, mod_l, g, w_bf16, cos_t, sin_t)


def _attn_kernel(sink_ref, q_ref, kp_ref, kc_ref, kn_ref, kx_ref, o_ref, *, blk0, nblk, nctx):
    n = pl.program_id(1) + blk0
    is_lat = n >= nctx
    has_prev = n > nctx
    has_next = n < nblk - 1
    W = ATT_BLOCK
    q = q_ref[0]
    qrows = jnp.concatenate([q[:, g * W:(g + 1) * W] for g in range(GQA_GROUP)], axis=0)
    kv_all = jnp.concatenate([kp_ref[0], kc_ref[0], kn_ref[0], kx_ref[0]], axis=0)
    nkeys = kv_all.shape[0]
    k_all = kv_all[:, 0:KV_W]
    v_ext = jnp.concatenate([kv_all[:, KV_W:2 * KV_W], jnp.ones((nkeys, KV_W), BF16)], axis=1)
    rows = GQA_GROUP * W
    ri = lax.broadcasted_iota(I32, (rows, W), 0) % W
    cj = lax.broadcasted_iota(I32, (rows, W), 1)
    m_prev = (cj >= ri) & has_prev
    m_cur = jnp.broadcast_to(is_lat, (rows, W))
    m_next = (cj <= ri) & (has_next & is_lat)
    grp = lax.broadcasted_iota(I32, (rows, 1), 0) // W
    low = cj < HEAD_DIM
    outs = []
    for h in range(N_KV):
        qh = jnp.where(low if h == 0 else jnp.logical_not(low), qrows, jnp.zeros_like(qrows))
        sink = jnp.zeros((rows, 1), F32)
        for g in range(GQA_GROUP):
            sink = jnp.where(grp == g, sink_ref[h * GQA_GROUP + g] * LOG2E, sink)
        s = _dot_nt(qh, k_all)
        segs = [jnp.where(m_prev, s[:, 0:W], NEG_BIG),
                jnp.where(m_cur, s[:, W:2 * W], NEG_BIG),
                jnp.where(m_next, s[:, 2 * W:3 * W], NEG_BIG)]
        segs += [s[:, c:c + W] for c in range(3 * W, nkeys, W)]
        mx = segs[0]
        for sg in segs[1:]:
            mx = jnp.maximum(mx, sg)
        m = jnp.maximum(jnp.max(mx, axis=1, keepdims=True), sink)
        p = jnp.concatenate([jnp.exp2(sg - m).astype(BF16) for sg in segs], axis=1)
        oe = _dot(p, v_ext)
        den = oe[:, KV_W:2 * KV_W] + jnp.exp2(sink - m)
        outs.append(oe[:, 0:KV_W] / den)
    o = jnp.where(low, outs[0], outs[1])
    for g in range(GQA_GROUP):
        o_ref[0, :, g * W:(g + 1) * W] = o[g * W:(g + 1) * W].astype(BF16)


def _attention(q, kv, sink, ctx_len, skip_ctx):
    B, S, _ = q.shape
    nblk = S // ATT_BLOCK
    nctx = ctx_len // ATT_BLOCK
    blk0 = nctx if skip_ctx else 0
    blk = lambda f: (lambda b, n: (b, f(n + blk0), 0))
    kern = functools.partial(_attn_kernel, blk0=blk0, nblk=nblk, nctx=nctx)
    return pl.pallas_call(
        kern,
        grid=(B, nblk - blk0),
        in_specs=[
            pl.BlockSpec(memory_space=pltpu.SMEM),
            pl.BlockSpec((1, ATT_BLOCK, ATT_W), blk(lambda n: n)),
            pl.BlockSpec((1, ATT_BLOCK, 2 * KV_W), blk(lambda n: jnp.maximum(n - 1, 0))),
            pl.BlockSpec((1, ATT_BLOCK, 2 * KV_W), blk(lambda n: n)),
            pl.BlockSpec((1, ATT_BLOCK, 2 * KV_W), blk(lambda n: jnp.minimum(n + 1, nblk - 1))),
            pl.BlockSpec((1, ctx_len, 2 * KV_W), lambda b, n: (b, 0, 0)),
        ],
        out_specs=pl.BlockSpec((1, ATT_BLOCK, ATT_W), lambda b, n: (b, n, 0)),
        out_shape=jax.ShapeDtypeStruct((B, S - blk0 * ATT_BLOCK, ATT_W), BF16),
        name="attention",
        compiler_params=_cparams(("arbitrary", "arbitrary")),
    )(sink, q, kv, kv, kv, kv)


def _hgrn_constants():
    C = HG_CHUNK
    t = np.arange(C)[:, None]
    r = np.arange(C)[None, :]
    tri = np.stack([r <= t, r >= t]).astype(np.float32)
    x = t ^ r
    lvl = np.where(x > 0, np.floor(np.log2(np.maximum(x, 1))).astype(np.int32), HG_LEVELS)
    lvl_f = np.where(t >= r, lvl, -1).astype(np.int32)
    lvl_b = np.where(t <= r, lvl, -1).astype(np.int32)
    lvl2 = np.stack([np.tile(lvl_f, (1, HG_HEADS)), np.tile(lvl_b, (1, HG_HEADS))])
    return tri, lvl2


def _span_row(x, span, row):
    C = x.shape[0]
    if span >= 8:
        x3 = x.reshape(C // span, span, x.shape[1])
        return jnp.broadcast_to(x3[:, row:row + 1, :], x3.shape).reshape(x.shape)
    pos = lax.broadcasted_iota(I32, x.shape, 0) % span
    out = x
    for p in range(span):
        if p != row:
            out = jnp.where(pos == p, pltpu.roll(x, (p - row) % C, 0), out)
    return out


def _hgrn_prepare(v, z, qr, lb, tri, backward, n_local):
    C = HG_CHUNK
    logf = jnp.log(jnp.maximum(lb, LB_FLOOR) + (1.0 - lb) * jax.nn.sigmoid(z)) * LOG2E
    k = (1.0 - lb) * jax.nn.sigmoid(-z)
    q = qr * jax.nn.sigmoid(qr)
    hi = logf.astype(BF16)
    lo = (logf - hi.astype(F32)).astype(BF16)
    cs = _dot(tri, jnp.concatenate([hi, lo], axis=1))
    lam = cs[:, 0:HG_W] + cs[:, HG_W:2 * HG_W]
    tot = lam[0:1] if backward else lam[C - 1:C]
    local = lam - _span_row(lam, n_local, n_local // 2)
    return dict(v=v, q=q, k=k, lam=lam, tot=tot, local=local)


def _hgrn_finish(g, head_masks, same_head, lvl, st_ref, sidx, backward, fast, n_local):
    C = HG_CHUNK
    q, k, v, lam, tot = g["q"], g["k"], g["v"], g["lam"], g["tot"]
    zero = jnp.zeros((C, HG_W), BF16)

    def per_head_rows(x):
        return jnp.concatenate([jnp.where(hm, x, zero) for hm in head_masks], axis=0)

    qb = q.astype(BF16)
    kb = k.astype(BF16)
    if fast:
        local_levels = n_local.bit_length() - 1
        a = jnp.where((lvl == HG_LEVELS) | ((lvl >= 0) & (lvl < local_levels)),
                      _dot_nt((q * jnp.exp2(g["local"])).astype(BF16),
                              per_head_rows((k * jnp.exp2(-g["local"])).astype(BF16))), 0.0)
        levels = range(local_levels, HG_LEVELS)
    else:
        a = jnp.where(lvl == HG_LEVELS, _dot_nt(qb, per_head_rows(kb)), 0.0)
        levels = range(HG_LEVELS)
    for l in levels:
        m = 1 << l
        ref = _span_row(lam, 2 * m, m if backward else m - 1)
        fac = jnp.exp2(-jnp.abs(lam - ref)).astype(BF16)
        a = jnp.where(lvl == l, _dot_nt(qb * fac, per_head_rows(kb * fac)), a)
    st = st_ref[sidx]
    q_in = (q * jnp.exp2(lam)).astype(BF16)
    o = _dot(a.astype(BF16), per_head_rows(v.astype(BF16))) + _dot_nt(q_in, st.astype(BF16))
    k_out = (k * jnp.exp2(tot - lam)).astype(BF16)
    upd = _dot(v.T.astype(BF16), k_out)
    st_ref[sidx] = st * jnp.exp2(tot) + jnp.where(same_head, upd, 0.0)
    return o


def _hgrn_kernel(vf_ref, zf_ref, qf_ref, vb_ref, zb_ref, qb_ref, lb_ref, tri_ref, lvl_ref,
                 of_ref, ob_ref, st_ref, *, n_local):
    j = pl.program_id(1)

    @pl.when(j == 0)
    def _():
        st_ref[...] = jnp.zeros_like(st_ref)

    head_id = (lax.broadcasted_iota(I32, (HG_CHUNK, HG_W), 1) // HEAD_DIM).astype(F32).astype(BF16)
    head_masks = [head_id == float(h) for h in range(HG_HEADS)]
    same_head = (lax.broadcasted_iota(I32, (HG_W, HG_W), 0) // HEAD_DIM
                 == lax.broadcasted_iota(I32, (HG_W, HG_W), 1) // HEAD_DIM)
    streams = []
    for d, (v_ref, z_ref, q_ref, o_ref) in enumerate(
            ((vf_ref, zf_ref, qf_ref, of_ref), (vb_ref, zb_ref, qb_ref, ob_ref))):
        for s in range(v_ref.shape[0]):
            g = _hgrn_prepare(v_ref[s], z_ref[s], q_ref[s], lb_ref[d], tri_ref[d], d == 1, n_local)
            streams.append((d, s, o_ref, g))
    worst = jnp.abs(streams[0][3]["local"])
    for _, _, _, g in streams[1:]:
        worst = jnp.maximum(worst, jnp.abs(g["local"]))
    can_use_local = jnp.max(worst) <= HG_LOCAL_MAX_LOG

    def run(fast):
        for d, s, o_ref, g in streams:
            o_ref[s] = _hgrn_finish(g, head_masks, same_head, lvl_ref[d], st_ref, 2 * s + d,
                                    backward=(d == 1), fast=fast, n_local=n_local)

    pl.when(can_use_local)(functools.partial(run, True))
    pl.when(jnp.logical_not(can_use_local))(functools.partial(run, False))


def _hgrn(hg, hqg, lb_l, ctx_len, n_local):
    B, S, _ = hg.shape
    nc = S // HG_CHUNK
    nctx = ctx_len // HG_CHUNK
    tri_np, lvl_np = _hgrn_constants()
    tri = jnp.asarray(tri_np, BF16)
    lvl = jnp.asarray(lvl_np, I32)

    def back(j):
        return jnp.where(j < nctx, nctx - 1 - j, nc - 1 - (j - nctx))

    bs = HG_BATCH if B % HG_BATCH == 0 else 1
    blk = (bs, HG_CHUNK, HG_W)
    fwd = lambda col: (lambda b, j: (b, j, col))
    bwd = lambda col: (lambda b, j: (b, back(j), col))
    const3 = lambda b, j: (0, 0, 0)
    return pl.pallas_call(
        functools.partial(_hgrn_kernel, n_local=n_local),
        grid=(B // bs, nc),
        in_specs=[
            pl.BlockSpec(blk, fwd(0)), pl.BlockSpec(blk, fwd(1)), pl.BlockSpec(blk, fwd(0)),
            pl.BlockSpec(blk, bwd(0)), pl.BlockSpec(blk, bwd(2)), pl.BlockSpec(blk, bwd(0)),
            pl.BlockSpec((2, 1, HG_W), const3),
            pl.BlockSpec((2, HG_CHUNK, HG_CHUNK), const3),
            pl.BlockSpec((2, HG_CHUNK, HG_HEADS * HG_CHUNK), const3),
        ],
        out_specs=[pl.BlockSpec(blk, fwd(0)), pl.BlockSpec(blk, bwd(0))],
        out_shape=[jax.ShapeDtypeStruct((B, S, HG_W), F32)] * 2,
        scratch_shapes=[pltpu.VMEM((2 * bs, HG_W, HG_W), F32)],
        name="hgrn_scan",
        compiler_params=_cparams(("arbitrary", "arbitrary")),
    )(hg, hg, hqg, hg, hg, hqg, lb_l.reshape(2, 1, HG_W), tri, lvl)


def _mixout_kernel(xc_ref, x_ref, att_ref, of_ref, ob_ref, g_ref, cv_ref, cprev_ref, cnext_ref,
                   mod_ref, gain_ref, cw_ref, wo_ref, n2_ref, wr_ref, ones_ref,
                   x1_ref, h2_ref, aff_ref, *, tile0, ntile):
    b = pl.program_id(0)
    i = pl.program_id(1) + tile0
    row = jnp.where(i == 0, 4, b)
    R = ROW_TILE

    def modv(c):
        return mod_ref[pl.ds(row, 1), c * D_MODEL:(c + 1) * D_MODEL]

    o = of_ref[0] + ob_ref[0]
    sq = o * o
    sq_hi = sq.astype(BF16)
    sq_lo = (sq - sq_hi.astype(F32)).astype(BF16)
    ms = _dot(sq_hi, ones_ref[...]) + _dot(sq_lo, ones_ref[...])
    g = g_ref[0]
    hg = o * lax.rsqrt(ms * (1.0 / HEAD_DIM) + EPS) * gain_ref[...] * (g * jax.nn.sigmoid(g))
    cv = cv_ref[0]
    u = cv[:, CV_W:2 * CV_W] * cv[:, 2 * CV_W:3 * CV_W]
    up = cprev_ref[0]
    un = cnext_ref[0]
    u_prev_row = up[7:8, CV_W:2 * CV_W] * up[7:8, 2 * CV_W:3 * CV_W]
    u_next_row = un[0:1, CV_W:2 * CV_W] * un[0:1, 2 * CV_W:3 * CV_W]
    u_prev_row = jnp.where(i <= 1, 0.0, u_prev_row)
    u_next_row = jnp.where((i == 0) | (i == ntile - 1), 0.0, u_next_row)
    ridx = lax.broadcasted_iota(I32, (R, CV_W), 0)
    u_m1 = jnp.where(ridx == 0, u_prev_row, pltpu.roll(u, 1, 0))
    u_p1 = jnp.where(ridx == R - 1, u_next_row, pltpu.roll(u, R - 1, 0))
    cw = cw_ref[...]
    conv = cv[:, 0:CV_W] * (u_m1 * cw[0:1] + u * cw[1:2] + u_p1 * cw[2:3])
    mix = (_dot(att_ref[0], wo_ref[0:ATT_W])
           + _dot(hg.astype(BF16), wo_ref[ATT_W:ATT_W + HG_W])
           + _dot(conv.astype(BF16), wo_ref[ATT_W + HG_W:D_MODEL]))
    x1 = jnp.where(i == 0, xc_ref[0], x_ref[0]) + modv(2) * mix
    x1_ref[0] = x1
    h2 = _rmsnorm_mod(x1, n2_ref[...], modv(3), modv(4))
    h_hi = h2.astype(BF16)
    h2_ref[0] = h_hi
    h_lo = (h2 - h_hi.astype(F32)).astype(BF16)
    wr = wr_ref[...]
    w_hi = wr.astype(BF16)
    w_lo = (wr - w_hi.astype(F32)).astype(BF16)
    r1 = _dot_nt(jnp.concatenate([w_hi, w_lo], axis=0), h_hi)
    logits = r1[0:N_EXPERTS] + r1[N_EXPERTS:2 * N_EXPERTS] + _dot_nt(w_hi, h_lo)
    e = jnp.exp(logits - jnp.max(logits, axis=0, keepdims=True))
    aff_ref[0] = e / jnp.sum(e, axis=0, keepdims=True)


def _mix_out(src, att, o2, hqg, cv, mod_l, gain, conv_w, wo_bf16, n2g, wr_t, skip_ctx):
    xc, xl, off, S = _row_sources(src)
    B = xc.shape[0]
    ntile = S // ROW_TILE
    tile0 = 1 if skip_ctx else 0
    sub = ROW_TILE // 8
    nsub = S // 8
    rmap = lambda b, i: (b, i + tile0, 0)
    omap = lambda b, i: (b, i, 0)
    s_out = S - tile0 * ROW_TILE
    const2 = lambda b, i: (0, 0)
    ones = jnp.asarray(np.kron(np.eye(HG_HEADS), np.ones((HEAD_DIM, HEAD_DIM))), BF16)
    kern = functools.partial(_mixout_kernel, tile0=tile0, ntile=ntile)
    return pl.pallas_call(
        kern,
        grid=(B, ntile - tile0),
        in_specs=[
            pl.BlockSpec((1, ROW_TILE, D_MODEL), lambda b, i: (b, 0, 0)),
            pl.BlockSpec((1, ROW_TILE, D_MODEL),
                         lambda b, i: (b, jnp.maximum(i + tile0 - off, 0), 0)),
            pl.BlockSpec((1, ROW_TILE, ATT_W), omap),
            pl.BlockSpec((1, ROW_TILE, HG_W), rmap),
            pl.BlockSpec((1, ROW_TILE, HG_W), rmap),
            pl.BlockSpec((1, ROW_TILE, HG_W), lambda b, i: (b, i + tile0, 1)),
            pl.BlockSpec((1, ROW_TILE, 3 * CV_W), rmap),
            pl.BlockSpec((1, 8, 3 * CV_W),
                         lambda b, i: (b, jnp.maximum((i + tile0) * sub - 1, 0), 0)),
            pl.BlockSpec((1, 8, 3 * CV_W),
                         lambda b, i: (b, jnp.minimum((i + tile0 + 1) * sub, nsub - 1), 0)),
            pl.BlockSpec((8, N_MOD * D_MODEL), const2),
            pl.BlockSpec((1, HG_W), const2),
            pl.BlockSpec((3, CV_W), const2),
            pl.BlockSpec((D_MODEL, D_MODEL), const2),
            pl.BlockSpec((1, D_MODEL), const2),
            pl.BlockSpec((N_EXPERTS, D_MODEL), const2),
            pl.BlockSpec((HG_W, HG_W), const2),
        ],
        out_specs=[
            pl.BlockSpec((1, ROW_TILE, D_MODEL), omap),
            pl.BlockSpec((1, ROW_TILE, D_MODEL), omap),
            pl.BlockSpec((1, N_EXPERTS, ROW_TILE), lambda b, i: (b, 0, i)),
        ],
        out_shape=[
            jax.ShapeDtypeStruct((B, s_out, D_MODEL), F32),
            jax.ShapeDtypeStruct((B, s_out, D_MODEL), BF16),
            jax.ShapeDtypeStruct((B, N_EXPERTS, s_out), F32),
        ],
        name="mix_out",
        compiler_params=_cparams(("arbitrary", "arbitrary")),
    )(xc, xl, att, o2[0], o2[1], hqg, cv, cv, cv, mod_l, gain, conv_w, wo_bf16, n2g, wr_t, ones)


def _topk_kernel(aff_ref, tri_ref, pos_ref, cnt_ref, *, segments):
    tri = tri_ref[...]

    def excl_cumsum(mask, n):
        carry = jnp.zeros((N_EXPERTS, 1), F32)
        parts = []
        carries = []
        for c in range(n // 128):
            carries.append(carry)
            blk = jnp.where(mask[:, c * 128:(c + 1) * 128], 1.0, 0.0)
            parts.append(_dot(blk.astype(BF16), tri) + carry)
            carry = carry + jnp.sum(blk, axis=1, keepdims=True)
        return jnp.concatenate(parts, axis=1), carries + [carry]

    for (lo, n, k) in segments:
        a = aff_ref[0, :, lo:lo + n]

        def body(it, thr):
            cand = thr | jnp.left_shift(jnp.int32(1), 30 - it)
            cnt = jnp.sum(jnp.where(a >= pltpu.bitcast(cand, F32), 1.0, 0.0), axis=1, keepdims=True)
            return jnp.where(cnt >= k, cand, thr)

        thr = lax.fori_loop(0, 31, body, jnp.zeros((N_EXPERTS, 1), I32))
        above = a >= pltpu.bitcast(thr + 1, F32)
        tied = jnp.logical_and(a >= pltpu.bitcast(thr, F32), jnp.logical_not(above))
        n_above = jnp.sum(jnp.where(above, 1.0, 0.0), axis=1, keepdims=True)
        rank_tied, _ = excl_cumsum(tied, n)
        sel = above | (tied & (rank_tied < (k - n_above)))
        pos, counts = excl_cumsum(sel, n)
        pos_ref[0, :, lo:lo + n] = jnp.where(sel, pos.astype(I32), -1)
    step = ROW_TILE // 128
    lane = lax.broadcasted_iota(I32, (N_EXPERTS, 128), 1)
    cnt = jnp.zeros((N_EXPERTS, 128), F32)
    for c, col in enumerate(counts[0::step]):
        cnt = jnp.where(lane == c, col, cnt)
    cnt_ref[0] = cnt.astype(I32)


def _topk_positions(aff, segments):
    B, E, S = aff.shape
    assert segments[-1][1] // ROW_TILE + 1 <= 128
    tri = jnp.asarray(np.triu(np.ones((128, 128)), 1), BF16)
    kern = functools.partial(_topk_kernel, segments=segments)
    return pl.pallas_call(
        kern,
        grid=(B,),
        in_specs=[pl.BlockSpec((1, E, S), lambda b: (b, 0, 0)),
                  pl.BlockSpec((128, 128), lambda b: (0, 0))],
        out_specs=[pl.BlockSpec((1, E, S), lambda b: (b, 0, 0)),
                   pl.BlockSpec((1, E, 128), lambda b: (b, 0, 0))],
        out_shape=[jax.ShapeDtypeStruct((B, E, S), I32),
                   jax.ShapeDtypeStruct((B, E, 128), I32)],
        name="topk_positions",
        compiler_params=_cparams(("arbitrary",)),
    )(aff, tri)


def _div_pow2(x, d):
    assert d & (d - 1) == 0
    return lax.shift_right_logical(x, jnp.int32(d.bit_length() - 1))


def _gather_kernel(cnt_ref, pos_ref, aff_ref, h_ref, *refs, cap_l, cap_c):
    if cap_c:
        xl_ref, gi_ref, xc_ref, gc_ref = refs
    else:
        xl_ref, gi_ref = refs
    b = pl.program_id(0)
    e0 = pl.program_id(1) * GATHER_GROUP
    j = pl.program_id(2)
    G = GATHER_GROUP
    GR = PACK_GRANULE
    W = min(GATHER_WINDOW, cap_l)
    pos = pos_ref[0]
    aff = aff_ref[0]
    h = h_ref[0]

    @pl.when(j == 0)
    def _():
        xl_ref[...] = jnp.zeros_like(xl_ref)
        gi_ref[...] = jnp.zeros_like(gi_ref)

    first_lane = lax.broadcasted_iota(I32, (W, 2), 1) == 0

    def add_rows(g, start, onehot_f32, rows):
        a = pl.multiple_of(start, GR)
        xl_ref[g, 0, pl.ds(a, W), :] += rows.astype(BF16)
        gate = jnp.sum(onehot_f32 * aff[g:g + 1, :], axis=1, keepdims=True)
        tok_id = jnp.sum(onehot_f32 * tok, axis=1, keepdims=True)
        gi_ref[g, 0, pl.ds(a, W), :] += jnp.where(first_lane, gate, tok_id)

    def latent():
        c = j - 1 if cap_c else j
        slot_i32 = lax.broadcasted_iota(I32, (W, ROW_TILE), 0)
        starts = [jnp.minimum(_div_pow2(cnt_ref[b, e0 + g, c], GR) * GR, cap_l - W)
                  for g in range(G)]
        hits = [(pos[g:g + 1, :] - starts[g]) == slot_i32 for g in range(G)]
        onehots = [jnp.where(hit, 1.0, 0.0) for hit in hits]
        prod = _dot(jnp.concatenate(onehots, axis=0).astype(BF16), h)
        for g in range(G):
            add_rows(g, starts[g], onehots[g], prod[g * W:(g + 1) * W])
        over = [cnt_ref[b, e0 + g, c + 1] - (starts[g] + W) for g in range(G)]
        worst = over[0]
        for o in over[1:]:
            worst = jnp.maximum(worst, o)

        @pl.when(worst > 0)
        def _():
            for g in range(G):
                def extra_window(k, carry, g=g):
                    first_slot = starts[g] + (k + 1) * W
                    start = jnp.minimum(first_slot, cap_l - W)
                    pg = pos[g:g + 1, :]
                    hit = ((pg - start) == slot_i32) & (pg >= first_slot)
                    onehot = jnp.where(hit, 1.0, 0.0)
                    add_rows(g, start, onehot, _dot(onehot.astype(BF16), h))
                    return carry

                lax.fori_loop(0, _div_pow2(jnp.maximum(over[g], 0) + (W - 1), W), extra_window, 0)

    tok = None
    if cap_c:
        tok = (lax.broadcasted_iota(I32, (1, ROW_TILE), 1) + (j - 1) * ROW_TILE).astype(F32)

        @pl.when(j == 0)
        def _():
            slot_c = lax.broadcasted_iota(I32, (cap_c, ROW_TILE), 0)
            onehots = [jnp.where(pos[g:g + 1, :] == slot_c, 1.0, 0.0) for g in range(G)]
            prod = _dot(jnp.concatenate(onehots, axis=0).astype(BF16), h)
            for g in range(G):
                xc_ref[g, 0] = prod[g * cap_c:(g + 1) * cap_c].astype(BF16)
                gc_ref[g, 0] = jnp.sum(onehots[g] * aff[g:g + 1, :], axis=1, keepdims=True)

        pl.when(j > 0)(latent)
    else:
        tok = (lax.broadcasted_iota(I32, (1, ROW_TILE), 1) + j * ROW_TILE).astype(F32)
        latent()


def _gather_rows(cnt, posm, aff, h2, cap_l, cap_c):
    B, E, S = posm.shape
    G = GATHER_GROUP
    assert min(GATHER_WINDOW, cap_l) % PACK_GRANULE == 0 and cap_l % PACK_GRANULE == 0
    kern = functools.partial(_gather_kernel, cap_l=cap_l, cap_c=cap_c)
    omap = lambda b, g, j, cnt_ref: (g, b, 0, 0)
    out_specs = [pl.BlockSpec((G, 1, cap_l, D_MODEL), omap), pl.BlockSpec((G, 1, cap_l, 2), omap)]
    out_shape = [jax.ShapeDtypeStruct((E, B, cap_l, D_MODEL), BF16),
                 jax.ShapeDtypeStruct((E, B, cap_l, 2), F32)]
    if cap_c:
        out_specs += [pl.BlockSpec((G, 1, cap_c, D_MODEL), omap),
                      pl.BlockSpec((G, 1, cap_c, 1), omap)]
        out_shape += [jax.ShapeDtypeStruct((E, B, cap_c, D_MODEL), BF16),
                      jax.ShapeDtypeStruct((E, B, cap_c, 1), F32)]
    tile_map = lambda b, g, j, cnt_ref: (b, g, j)
    grid_spec = pltpu.PrefetchScalarGridSpec(
        num_scalar_prefetch=1,
        grid=(B, E // G, S // ROW_TILE),
        in_specs=[pl.BlockSpec((1, G, ROW_TILE), tile_map),
                  pl.BlockSpec((1, G, ROW_TILE), tile_map),
                  pl.BlockSpec((1, ROW_TILE, D_MODEL), lambda b, g, j, cnt_ref: (b, j, 0))],
        out_specs=out_specs,
    )
    return pl.pallas_call(
        kern,
        grid_spec=grid_spec,
        out_shape=out_shape,
        name="gather_rows",
        compiler_params=_cparams(("arbitrary", "arbitrary", "arbitrary")),
    )(cnt, posm, aff, h2)


def _expert_kernel(*refs, row_chunk, n_sets):
    x_refs = refs[:n_sets]
    g_refs = refs[n_sets:2 * n_sets]
    wg_ref, wu_ref, wd_ref = refs[2 * n_sets:2 * n_sets + 3]
    y_refs = refs[2 * n_sets + 3:3 * n_sets + 3]
    acc_refs = refs[3 * n_sets + 3:]
    f = pl.program_id(1)
    nf = pl.num_programs(1)
    wg = wg_ref[0, 0].astype(BF16)
    wu = wu_ref[0, 0].astype(BF16)
    wd = wd_ref[0, 0].astype(BF16)

    def hidden_tile(first):
        for x_ref, acc_ref in zip(x_refs, acc_refs):
            rows = x_ref.shape[1]
            step = min(row_chunk, rows)
            for r in range(rows // step):
                rs = slice(r * step, (r + 1) * step)
                x = x_ref[0, rs, :]
                g = _dot(x, wg)
                u = _dot(x, wu)
                y = _dot((g * jax.nn.sigmoid(g) * u).astype(BF16), wd)
                if first:
                    acc_ref[rs, :] = y
                else:
                    acc_ref[rs, :] += y

    pl.when(f == 0)(functools.partial(hidden_tile, True))
    pl.when(f > 0)(functools.partial(hidden_tile, False))

    @pl.when(f == nf - 1)
    def _():
        for g_ref, y_ref, acc_ref in zip(g_refs, y_refs, acc_refs):
            y_ref[0] = (acc_ref[...] * g_ref[0][:, 0:1]).astype(BF16)


def _expert_mlp(x_sets, g_sets, w_gate, w_up, w_down, layer, row_chunk):
    E = N_EXPERTS
    nf = D_EXPERT // FF_TILE
    n_sets = len(x_sets)
    kern = functools.partial(_expert_kernel, row_chunk=row_chunk, n_sets=n_sets)
    xspec = lambda r: pl.BlockSpec((1, r, D_MODEL), lambda e, f: (e, 0, 0))
    gspec = lambda gs: pl.BlockSpec((1,) + gs.shape[1:], lambda e, f: (e, 0, 0))
    return pl.pallas_call(
        kern,
        grid=(E, nf),
        in_specs=[xspec(xs.shape[1]) for xs in x_sets] + [gspec(gs) for gs in g_sets] + [
            pl.BlockSpec((1, 1, D_MODEL, FF_TILE), lambda e, f: (layer, e, 0, f)),
            pl.BlockSpec((1, 1, D_MODEL, FF_TILE), lambda e, f: (layer, e, 0, f)),
            pl.BlockSpec((1, 1, FF_TILE, D_MODEL), lambda e, f: (layer, e, f, 0))],
        out_specs=[xspec(xs.shape[1]) for xs in x_sets],
        out_shape=[jax.ShapeDtypeStruct(xs.shape, BF16) for xs in x_sets],
        scratch_shapes=[pltpu.VMEM(xs.shape[1:], F32) for xs in x_sets],
        name="expert_mlp",
        compiler_params=_cparams(("arbitrary", "arbitrary")),
    )(*x_sets, *g_sets, w_gate, w_up, w_down)


def _combine_kernel(cnt_ref, x_ref, pos_ref, *refs, cap_l, cap_c, final):
    if cap_c:
        yl_ref, il_ref, yc_ref, mod_ref, fg_ref, o_ref, acc_ref, pk_ref, tk_ref = refs
    else:
        yl_ref, il_ref, mod_ref, fg_ref, o_ref, acc_ref, pk_ref, tk_ref = refs
    b = pl.program_id(0)
    i = pl.program_id(1)
    row = jnp.where(i == 0, 4, b) if cap_c else b
    gate = mod_ref[pl.ds(row, 1), 5 * D_MODEL:6 * D_MODEL]
    pos = pos_ref[0]
    acc_ref[...] = jnp.zeros_like(acc_ref)
    GR = PACK_GRANULE
    W = min(GATHER_WINDOW, cap_l)

    def add_latent():
        t = i - 1 if cap_c else i
        starts = [jnp.minimum(_div_pow2(cnt_ref[b, e, t], GR) * GR, cap_l - W)
                  for e in range(N_EXPERTS)]
        for e in range(N_EXPERTS):
            a = pl.multiple_of(starts[e], GR)
            pk_ref[e * W:(e + 1) * W, :] = yl_ref[e, 0, pl.ds(a, W), :]
            tk_ref[e * W:(e + 1) * W, :] = il_ref[e, 0, pl.ds(a, W), 1:2].astype(I32)

        def product(tok_ids, rows):
            lane = lax.broadcasted_iota(I32, (tok_ids.shape[0], ROW_TILE), 1) + t * ROW_TILE
            onehot = jnp.where(tok_ids == lane, 1.0, 0.0)
            return _dot(onehot.T.astype(BF16), rows)

        acc_ref[...] += product(tk_ref[...], pk_ref[...])
        over = [cnt_ref[b, e, t + 1] - (starts[e] + W) for e in range(N_EXPERTS)]
        worst = over[0]
        for o in over[1:]:
            worst = jnp.maximum(worst, o)

        @pl.when(worst > 0)
        def _():
            wx = min(2 * W, cap_l)
            slot = lax.broadcasted_iota(I32, (wx, 1), 0)
            for e in range(N_EXPERTS):
                def extra_window(k, carry, e=e):
                    first_slot = starts[e] + W + k * wx
                    a = pl.multiple_of(jnp.minimum(first_slot, cap_l - wx), GR)
                    ids = jnp.where(slot + a >= first_slot,
                                    il_ref[e, 0, pl.ds(a, wx), 1:2].astype(I32), -1)
                    acc_ref[...] += product(ids, yl_ref[e, 0, pl.ds(a, wx), :])
                    return carry

                lax.fori_loop(0, _div_pow2(jnp.maximum(over[e], 0) + (wx - 1), wx),
                              extra_window, 0)

    def add_context():
        ncol = yc_ref.shape[1]
        lane = lax.broadcasted_iota(I32, (ROW_TILE, ncol), 1)
        for e in range(N_EXPERTS):
            pe = pos[:, e:e + 1]
            tgt = jnp.where(pe >= 0, pe + b * cap_c, -1)
            acc_ref[...] += _dot(jnp.where(tgt == lane, 1.0, 0.0).astype(BF16), yc_ref[e])

    if cap_c:
        pl.when(i == 0)(add_context)
        pl.when(i > 0)(add_latent)
    else:
        add_latent()
    x2 = x_ref[0] + gate * acc_ref[...]
    if final:
        var = jnp.mean(x2 * x2, axis=-1, keepdims=True)
        x2 = x2 * lax.rsqrt(var + EPS) * fg_ref[...]
    o_ref[0] = x2


def _combine(cnt, x1, pos_t, yl, il, yc, mod_l, final_g, cap_l, cap_c, final):
    B, S, _ = x1.shape
    E = N_EXPERTS
    rmap = lambda b, i, cnt_ref: (b, i, 0)
    const2 = lambda b, i, cnt_ref: (0, 0)
    kern = functools.partial(_combine_kernel, cap_l=cap_l, cap_c=cap_c, final=final)
    per_sample = lambda b, i, cnt_ref: (0, b, 0, 0)
    y_specs = [pl.BlockSpec((E, 1, cap_l, D_MODEL), per_sample, pipeline_mode=pl.Buffered(1)),
               pl.BlockSpec((E, 1, cap_l, 2), per_sample, pipeline_mode=pl.Buffered(1))]
    y_args = [yl, il]
    pack_rows = E * min(GATHER_WINDOW, cap_l)
    if cap_c:
        y_specs.append(pl.BlockSpec(yc.shape, lambda b, i, cnt_ref: (0, 0, 0),
                                    pipeline_mode=pl.Buffered(1)))
        y_args.append(yc)
    grid_spec = pltpu.PrefetchScalarGridSpec(
        num_scalar_prefetch=1,
        grid=(B, S // ROW_TILE),
        in_specs=[pl.BlockSpec((1, ROW_TILE, D_MODEL), rmap),
                  pl.BlockSpec((1, ROW_TILE, E), rmap)] + y_specs + [
                  pl.BlockSpec((8, N_MOD * D_MODEL), const2),
                  pl.BlockSpec((1, D_MODEL), const2)],
        out_specs=pl.BlockSpec((1, ROW_TILE, D_MODEL), rmap),
        scratch_shapes=[pltpu.VMEM((ROW_TILE, D_MODEL), F32),
                        pltpu.VMEM((pack_rows, D_MODEL), BF16),
                        pltpu.VMEM((pack_rows, 1), I32)],
    )
    return pl.pallas_call(
        kern,
        grid_spec=grid_spec,
        out_shape=jax.ShapeDtypeStruct((B, S, D_MODEL), F32),
        name="combine",
        compiler_params=_cparams(("arbitrary", "arbitrary")),
    )(cnt, x1, pos_t, *y_args, mod_l, final_g)


def _rope_tables(n_lat, ctx_len):
    t = np.arange(n_lat)
    pos = np.stack([t // GRID_W, t % GRID_W], axis=-1).astype(np.float32)
    inv = (ROPE_BASE ** (-np.arange(ROPE_FREQS, dtype=np.float32) / ROPE_FREQS)).astype(np.float32)
    ang = pos[:, :, None] * inv
    cos = np.cos(ang).astype(np.float32)
    sin = np.sin(ang).astype(np.float32)
    cos64 = np.concatenate([cos[:, 0], cos[:, 0], cos[:, 1], cos[:, 1]], axis=-1)
    sin64 = np.concatenate([-sin[:, 0], sin[:, 0], -sin[:, 1], sin[:, 1]], axis=-1)
    cos_t = np.concatenate([np.ones((ctx_len, 64), np.float32), cos64], axis=0)
    sin_t = np.concatenate([np.zeros((ctx_len, 64), np.float32), sin64], axis=0)
    return (jnp.asarray(np.tile(cos_t, (1, 2))), jnp.asarray(np.tile(sin_t, (1, 2))))


def kernel(x, c, ctx, c_ctx, ada_w, ada_b, norm1_g, norm2_g, w_in, attn_sink, hgrn_lb,
           hgrn_norm_g, conv_w, w_o, router_w, exp_w_gate, exp_w_up, exp_w_down, final_norm_g):
    B, T, D = x.shape
    L = ctx.shape[1]
    depth = ada_w.shape[0]
    assert D == D_MODEL and L == ROW_TILE and T % ROW_TILE == 0 and B <= 4
    S = L + T
    cap_l = EC_CAPACITY * T // N_EXPERTS
    cap_c = EC_CAPACITY * L // N_EXPERTS
    assert cap_l % 16 == 0 and cap_c % 16 == 0 and (B * cap_c) % 16 == 0

    cos_t, sin_t = _rope_tables(T, L)
    gamma = jax.nn.softmax(hgrn_lb.astype(F32), axis=0)
    lb_all = jnp.cumsum(gamma, axis=0) - gamma[0]
    cvec = jnp.concatenate([c, jnp.zeros((4 - B, D), F32), c_ctx[None],
                            jnp.zeros((3, D), F32)], axis=0)
    mod = _modulation(cvec, ada_w, ada_b)
    xs = (ctx, x)

    def interleave_heads(w, axis):
        shp = w.shape
        w = w.reshape(shp[:axis] + (N_KV, GQA_GROUP, HEAD_DIM) + shp[axis + 1:])
        return jnp.swapaxes(w, axis, axis + 1).reshape(shp)

    for l in range(depth):
        last = l == depth - 1
        w_in_l = w_in[l].astype(BF16)
        w_in_l = jnp.concatenate([w_in_l[:, :1024], interleave_heads(w_in_l[:, 1024:1536], 1),
                                  w_in_l[:, 1536:]], axis=1)
        w_o_l = w_o[l].astype(BF16)
        w_o_l = jnp.concatenate([interleave_heads(w_o_l[:ATT_W], 0), w_o_l[ATT_W:]], axis=0)
        q, kv, hg, hqg, cv = _in_projection(xs, mod[l], norm1_g[l][None], w_in_l, cos_t, sin_t)
        att = _attention(q, kv, attn_sink[l], L, skip_ctx=last)
        o2 = _hgrn(hg, hqg, lb_all[l], L, HG_LOCAL_FIRST if l == 0 else HG_LOCAL)
        x1, h2, aff = _mix_out(xs, att, o2, hqg, cv, mod[l], hgrn_norm_g[l][None], conv_w[l],
                               w_o_l, norm2_g[l][None], router_w[l].T, skip_ctx=last)
        lm, cc = (0, 0) if last else (L, cap_c)
        segments = ((lm, T, cap_l),) if last else ((0, L, cap_c), (L, T, cap_l))
        posm, cnt = _topk_positions(aff, segments)
        gathered = _gather_rows(cnt, posm, aff, h2, cap_l, cc)
        il = gathered[1]
        x_sets = [a.reshape(N_EXPERTS, -1, D) for a in gathered[0::2]]
        g_sets = [a.reshape(N_EXPERTS, -1, a.shape[-1]) for a in gathered[1::2]]
        ys = _expert_mlp(x_sets, g_sets, exp_w_gate, exp_w_up, exp_w_down, l, EXPERT_ROWS)
        yl = ys[0].reshape(N_EXPERTS, B, cap_l, D)
        xs = _combine(cnt, x1, jnp.swapaxes(posm, 1, 2), yl, il, None if last else ys[1], mod[l],
                      final_norm_g[None], cap_l, cc, final=last)
    return xs
```

```python
import functools

import numpy as np
import jax
import jax.numpy as jnp
from jax import lax
from jax.experimental import pallas as pl
from jax.experimental.pallas import tpu as pltpu

F32 = jnp.float32
BF16 = jnp.bfloat16
I32 = jnp.int32
HIGHEST = lax.Precision.HIGHEST

D_MODEL = 1024
GRID_W = 64
EPS = 1e-6
LB_FLOOR = 1e-30
N_MOD = 6
ATT_W = 512
HG_W = 256
CV_W = 256
HEAD_DIM = 64
N_Q = 8
N_KV = 2
GQA_GROUP = 4
KV_W = 128
ROPE_BASE = 10000.0
ROPE_FREQS = 16
HG_HEADS = 4
N_EXPERTS = 16
EC_CAPACITY = 2
D_EXPERT = 2048
IN_COLS = 2816

ROW_TILE = 256
ATT_BLOCK = 128
HG_CHUNK = 128
HG_LEVELS = 7
HG_BATCH = 4
HG_LOCAL = 64
HG_LOCAL_FIRST = 32
HG_LOCAL_MAX_LOG = 115.0
GATHER_GROUP = 16
GATHER_WINDOW = 64
PACK_GRANULE = 16
FF_TILE = 512
EXPERT_ROWS = 512
MOD_TILE = 1536
VMEM_LIMIT = 56 * 1024 * 1024

NEG_BIG = -1e30
LOG2E = 1.4426950408889634


def _cparams(sem):
    return pltpu.CompilerParams(dimension_semantics=sem, vmem_limit_bytes=VMEM_LIMIT)


def _dot(a, b):
    return jnp.dot(a, b, preferred_element_type=F32)


def _dot_nt(a, b):
    return lax.dot_general(a, b, (((1,), (1,)), ((), ())), preferred_element_type=F32)


def _mod_kernel(a_ref, w_ref, b_ref, o_ref):
    a = a_ref[...]
    a = a * jax.nn.sigmoid(a)
    o_ref[0] = jnp.dot(a, w_ref[0], precision=HIGHEST, preferred_element_type=F32) + b_ref[0]


def _modulation(cvec, ada_w, ada_b):
    depth = ada_w.shape[0]
    ncol = ada_w.shape[2]
    return pl.pallas_call(
        _mod_kernel,
        grid=(depth, ncol // MOD_TILE),
        in_specs=[
            pl.BlockSpec((8, D_MODEL), lambda l, j: (0, 0)),
            pl.BlockSpec((1, D_MODEL, MOD_TILE), lambda l, j: (l, 0, j)),
            pl.BlockSpec((1, 1, MOD_TILE), lambda l, j: (l, 0, j)),
        ],
        out_specs=pl.BlockSpec((1, 8, MOD_TILE), lambda l, j: (l, 0, j)),
        out_shape=jax.ShapeDtypeStruct((depth, 8, ncol), F32),
        name="modulation",
        compiler_params=_cparams(("arbitrary", "arbitrary")),
    )(cvec, ada_w, ada_b.reshape(depth, 1, ncol))


def _swap_halves(x):
    n = x.shape[-1]
    lane = lax.broadcasted_iota(I32, x.shape, x.ndim - 1)
    up = pltpu.roll(x, n - ROPE_FREQS, x.ndim - 1)
    dn = pltpu.roll(x, ROPE_FREQS, x.ndim - 1)
    return jnp.where((lane % (2 * ROPE_FREQS)) < ROPE_FREQS, up, dn)


def _rmsnorm_mod(x, g, shift, scale):
    var = jnp.mean(x * x, axis=-1, keepdims=True)
    y = x * lax.rsqrt(var + EPS) * g
    return y * (1.0 + scale) + shift


def _row_sources(src):
    if isinstance(src, tuple):
        ctx, lat = src
        return ctx, lat, 1, ctx.shape[1] + lat.shape[1]
    return src, src, 0, src.shape[1]


def _inproj_kernel(xc_ref, x_ref, mod_ref, g_ref, w_ref, cos_ref, sin_ref,
                   q_ref, kv_ref, hg_ref, hqg_ref, cv_ref):
    b = pl.program_id(0)
    i = pl.program_id(1)
    row = jnp.where(i == 0, 4, b)
    shift = mod_ref[pl.ds(row, 1), 0:D_MODEL]
    scale = mod_ref[pl.ds(row, 1), D_MODEL:2 * D_MODEL]
    xin = jnp.where(i == 0, xc_ref[0], x_ref[0])
    h = _rmsnorm_mod(xin, g_ref[...], shift, scale)
    p = _dot(h.astype(BF16), w_ref[...])
    cos2 = cos_ref[...]
    sin2 = sin_ref[...]
    k = p[:, 0:KV_W]
    k = k * cos2 + _swap_halves(k) * sin2
    kv_ref[0, :, 0:KV_W] = k.astype(BF16)
    kv_ref[0, :, KV_W:2 * KV_W] = p[:, KV_W:2 * KV_W].astype(BF16)
    hg_ref[0] = p[:, 256:1024]
    q = p[:, 1024:1536]
    cos8 = jnp.concatenate([cos2] * 4, axis=1)
    sin8 = jnp.concatenate([sin2] * 4, axis=1)
    q = (q * cos8 + _swap_halves(q) * sin8) * (HEAD_DIM ** -0.5 * LOG2E)
    q_ref[0] = q.astype(BF16)
    hqg_ref[0] = p[:, 1536:2048]
    cv_ref[0] = p[:, 2048:2816]


def _in_projection(src, mod_l, g, w_bf16, cos_t, sin_t):
    xc, xl, off, S = _row_sources(src)
    B = xc.shape[0]
    nt = S // ROW_TILE
    row_map = lambda b, i: (b, i, 0)
    const2 = lambda b, i: (0, 0)
    return pl.pallas_call(
        _inproj_kernel,
        grid=(B, nt),
        in_specs=[
            pl.BlockSpec((1, ROW_TILE, D_MODEL), lambda b, i: (b, 0, 0)),
            pl.BlockSpec((1, ROW_TILE, D_MODEL), lambda b, i: (b, jnp.maximum(i - off, 0), 0)),
            pl.BlockSpec((8, N_MOD * D_MODEL), const2),
            pl.BlockSpec((1, D_MODEL), const2),
            pl.BlockSpec((D_MODEL, IN_COLS), const2),
            pl.BlockSpec((ROW_TILE, 2 * HEAD_DIM), lambda b, i: (i, 0)),
            pl.BlockSpec((ROW_TILE, 2 * HEAD_DIM), lambda b, i: (i, 0)),
        ],
        out_specs=[
            pl.BlockSpec((1, ROW_TILE, ATT_W), row_map),
            pl.BlockSpec((1, ROW_TILE, 2 * KV_W), row_map),
            pl.BlockSpec((1, ROW_TILE, 3 * HG_W), row_map),
            pl.BlockSpec((1, ROW_TILE, 2 * HG_W), row_map),
            pl.BlockSpec((1, ROW_TILE, 3 * CV_W), row_map),
        ],
        out_shape=[
            jax.ShapeDtypeStruct((B, S, ATT_W), BF16),
            jax.ShapeDtypeStruct((B, S, 2 * KV_W), BF16),
            jax.ShapeDtypeStruct((B, S, 3 * HG_W), F32),
            jax.ShapeDtypeStruct((B, S, 2 * HG_W), F32),
            jax.ShapeDtypeStruct((B, S, 3 * CV_W), F32),
        ],
        name="in_projection",
        compiler_params=_cparams(("arbitrary", "arbitrary")),
    )(xc, xl, mod_l, g, w_bf16, cos_t, sin_t)


def _attn_kernel(sink_ref, q_ref, kp_ref, kc_ref, kn_ref, kx_ref, o_ref, *, blk0, nblk, nctx):
    n = pl.program_id(1) + blk0
    is_lat = n >= nctx
    has_prev = n > nctx
    has_next = n < nblk - 1
    W = ATT_BLOCK
    q = q_ref[0]
    qrows = jnp.concatenate([q[:, g * W:(g + 1) * W] for g in range(GQA_GROUP)], axis=0)
    kv_all = jnp.concatenate([kp_ref[0], kc_ref[0], kn_ref[0], kx_ref[0]], axis=0)
    nkeys = kv_all.shape[0]
    k_all = kv_all[:, 0:KV_W]
    v_ext = jnp.concatenate([kv_all[:, KV_W:2 * KV_W], jnp.ones((nkeys, KV_W), BF16)], axis=1)
    rows = GQA_GROUP * W
    ri = lax.broadcasted_iota(I32, (rows, W), 0) % W
    cj = lax.broadcasted_iota(I32, (rows, W), 1)
    m_prev = (cj >= ri) & has_prev
    m_cur = jnp.broadcast_to(is_lat, (rows, W))
    m_next = (cj <= ri) & (has_next & is_lat)
    grp = lax.broadcasted_iota(I32, (rows, 1), 0) // W
    low = cj < HEAD_DIM
    outs = []
    for h in range(N_KV):
        qh = jnp.where(low if h == 0 else jnp.logical_not(low), qrows, jnp.zeros_like(qrows))
        sink = jnp.zeros((rows, 1), F32)
        for g in range(GQA_GROUP):
            sink = jnp.where(grp == g, sink_ref[h * GQA_GROUP + g] * LOG2E, sink)
        s = _dot_nt(qh, k_all)
        segs = [jnp.where(m_prev, s[:, 0:W], NEG_BIG),
                jnp.where(m_cur, s[:, W:2 * W], NEG_BIG),
                jnp.where(m_next, s[:, 2 * W:3 * W], NEG_BIG)]
        segs += [s[:, c:c + W] for c in range(3 * W, nkeys, W)]
        mx = segs[0]
        for sg in segs[1:]:
            mx = jnp.maximum(mx, sg)
        m = jnp.maximum(jnp.max(mx, axis=1, keepdims=True), sink)
        p = jnp.concatenate([jnp.exp2(sg - m).astype(BF16) for sg in segs], axis=1)
        oe = _dot(p, v_ext)
        den = oe[:, KV_W:2 * KV_W] + jnp.exp2(sink - m)
        outs.append(oe[:, 0:KV_W] / den)
    o = jnp.where(low, outs[0], outs[1])
    for g in range(GQA_GROUP):
        o_ref[0, :, g * W:(g + 1) * W] = o[g * W:(g + 1) * W].astype(BF16)


def _attention(q, kv, sink, ctx_len, skip_ctx):
    B, S, _ = q.shape
    nblk = S // ATT_BLOCK
    nctx = ctx_len // ATT_BLOCK
    blk0 = nctx if skip_ctx else 0
    blk = lambda f: (lambda b, n: (b, f(n + blk0), 0))
    kern = functools.partial(_attn_kernel, blk0=blk0, nblk=nblk, nctx=nctx)
    return pl.pallas_call(
        kern,
        grid=(B, nblk - blk0),
        in_specs=[
            pl.BlockSpec(memory_space=pltpu.SMEM),
            pl.BlockSpec((1, ATT_BLOCK, ATT_W), blk(lambda n: n)),
            pl.BlockSpec((1, ATT_BLOCK, 2 * KV_W), blk(lambda n: jnp.maximum(n - 1, 0))),
            pl.BlockSpec((1, ATT_BLOCK, 2 * KV_W), blk(lambda n: n)),
            pl.BlockSpec((1, ATT_BLOCK, 2 * KV_W), blk(lambda n: jnp.minimum(n + 1, nblk - 1))),
            pl.BlockSpec((1, ctx_len, 2 * KV_W), lambda b, n: (b, 0, 0)),
        ],
        out_specs=pl.BlockSpec((1, ATT_BLOCK, ATT_W), lambda b, n: (b, n, 0)),
        out_shape=jax.ShapeDtypeStruct((B, S - blk0 * ATT_BLOCK, ATT_W), BF16),
        name="attention",
        compiler_params=_cparams(("arbitrary", "arbitrary")),
    )(sink, q, kv, kv, kv, kv)


def _hgrn_constants():
    C = HG_CHUNK
    t = np.arange(C)[:, None]
    r = np.arange(C)[None, :]
    tri = np.stack([r <= t, r >= t]).astype(np.float32)
    x = t ^ r
    lvl = np.where(x > 0, np.floor(np.log2(np.maximum(x, 1))).astype(np.int32), HG_LEVELS)
    lvl_f = np.where(t >= r, lvl, -1).astype(np.int32)
    lvl_b = np.where(t <= r, lvl, -1).astype(np.int32)
    lvl2 = np.stack([np.tile(lvl_f, (1, HG_HEADS)), np.tile(lvl_b, (1, HG_HEADS))])
    return tri, lvl2


def _span_row(x, span, row):
    C = x.shape[0]
    if span >= 8:
        x3 = x.reshape(C // span, span, x.shape[1])
        return jnp.broadcast_to(x3[:, row:row + 1, :], x3.shape).reshape(x.shape)
    pos = lax.broadcasted_iota(I32, x.shape, 0) % span
    out = x
    for p in range(span):
        if p != row:
            out = jnp.where(pos == p, pltpu.roll(x, (p - row) % C, 0), out)
    return out


def _hgrn_prepare(v, z, qr, lb, tri, backward, n_local):
    C = HG_CHUNK
    logf = jnp.log(jnp.maximum(lb, LB_FLOOR) + (1.0 - lb) * jax.nn.sigmoid(z)) * LOG2E
    k = (1.0 - lb) * jax.nn.sigmoid(-z)
    q = qr * jax.nn.sigmoid(qr)
    hi = logf.astype(BF16)
    lo = (logf - hi.astype(F32)).astype(BF16)
    cs = _dot(tri, jnp.concatenate([hi, lo], axis=1))
    lam = cs[:, 0:HG_W] + cs[:, HG_W:2 * HG_W]
    tot = lam[0:1] if backward else lam[C - 1:C]
    local = lam - _span_row(lam, n_local, n_local // 2)
    return dict(v=v, q=q, k=k, lam=lam, tot=tot, local=local)


def _hgrn_finish(g, head_masks, same_head, lvl, st_ref, sidx, backward, fast, n_local):
    q, k, v, lam, tot = g["q"], g["k"], g["v"], g["lam"], g["tot"]

    def per_head_rows(x):
        return jnp.concatenate([x * hm for hm in head_masks], axis=0)

    qb = q.astype(BF16)
    kb = k.astype(BF16)
    if fast:
        local_levels = n_local.bit_length() - 1
        a = jnp.where((lvl == HG_LEVELS) | ((lvl >= 0) & (lvl < local_levels)),
                      _dot_nt((q * jnp.exp2(g["local"])).astype(BF16),
                              per_head_rows((k * jnp.exp2(-g["local"])).astype(BF16))), 0.0)
        levels = range(local_levels, HG_LEVELS)
    else:
        a = jnp.where(lvl == HG_LEVELS, _dot_nt(qb, per_head_rows(kb)), 0.0)
        levels = range(HG_LEVELS)
    for l in levels:
        m = 1 << l
        ref = _span_row(lam, 2 * m, m if backward else m - 1)
        fac = jnp.exp2(-jnp.abs(lam - ref)).astype(BF16)
        a = jnp.where(lvl == l, _dot_nt(qb * fac, per_head_rows(kb * fac)), a)
    st = st_ref[sidx]
    q_in = (q * jnp.exp2(lam)).astype(BF16)
    o = _dot(a.astype(BF16), per_head_rows(v.astype(BF16))) + _dot_nt(q_in, st.astype(BF16))
    k_out = (k * jnp.exp2(tot - lam)).astype(BF16)
    upd = _dot(v.T.astype(BF16), k_out)
    st_ref[sidx] = st * jnp.exp2(tot) + jnp.where(same_head, upd, 0.0)
    return o


def _hgrn_kernel(vf_ref, zf_ref, qf_ref, vb_ref, zb_ref, qb_ref, lb_ref, tri_ref, lvl_ref,
                 of_ref, ob_ref, st_ref, *, n_local):
    j = pl.program_id(1)

    @pl.when(j == 0)
    def _():
        st_ref[...] = jnp.zeros_like(st_ref)

    head_id = lax.broadcasted_iota(I32, (HG_CHUNK, HG_W), 1) // HEAD_DIM
    head_masks = [jnp.where(head_id == h, 1.0, 0.0).astype(BF16) for h in range(HG_HEADS)]
    same_head = (lax.broadcasted_iota(I32, (HG_W, HG_W), 0) // HEAD_DIM
                 == lax.broadcasted_iota(I32, (HG_W, HG_W), 1) // HEAD_DIM)
    streams = []
    for d, (v_ref, z_ref, q_ref, o_ref) in enumerate(
            ((vf_ref, zf_ref, qf_ref, of_ref), (vb_ref, zb_ref, qb_ref, ob_ref))):
        for s in range(v_ref.shape[0]):
            g = _hgrn_prepare(v_ref[s], z_ref[s], q_ref[s], lb_ref[d], tri_ref[d], d == 1, n_local)
            streams.append((d, s, o_ref, g))
    worst = jnp.abs(streams[0][3]["local"])
    for _, _, _, g in streams[1:]:
        worst = jnp.maximum(worst, jnp.abs(g["local"]))
    can_use_local = jnp.max(worst) <= HG_LOCAL_MAX_LOG

    def run(fast):
        for d, s, o_ref, g in streams:
            o_ref[s] = _hgrn_finish(g, head_masks, same_head, lvl_ref[d], st_ref, 2 * s + d,
                                    backward=(d == 1), fast=fast, n_local=n_local)

    pl.when(can_use_local)(functools.partial(run, True))
    pl.when(jnp.logical_not(can_use_local))(functools.partial(run, False))


def _hgrn(hg, hqg, lb_l, ctx_len, n_local):
    B, S, _ = hg.shape
    nc = S // HG_CHUNK
    nctx = ctx_len // HG_CHUNK
    tri_np, lvl_np = _hgrn_constants()
    tri = jnp.asarray(tri_np, BF16)
    lvl = jnp.asarray(lvl_np, I32)

    def back(j):
        return jnp.where(j < nctx, nctx - 1 - j, nc - 1 - (j - nctx))

    bs = HG_BATCH if B % HG_BATCH == 0 else 1
    blk = (bs, HG_CHUNK, HG_W)
    fwd = lambda col: (lambda b, j: (b, j, col))
    bwd = lambda col: (lambda b, j: (b, back(j), col))
    const3 = lambda b, j: (0, 0, 0)
    return pl.pallas_call(
        functools.partial(_hgrn_kernel, n_local=n_local),
        grid=(B // bs, nc),
        in_specs=[
            pl.BlockSpec(blk, fwd(0)), pl.BlockSpec(blk, fwd(1)), pl.BlockSpec(blk, fwd(0)),
            pl.BlockSpec(blk, bwd(0)), pl.BlockSpec(blk, bwd(2)), pl.BlockSpec(blk, bwd(0)),
            pl.BlockSpec((2, 1, HG_W), const3),
            pl.BlockSpec((2, HG_CHUNK, HG_CHUNK), const3),
            pl.BlockSpec((2, HG_CHUNK, HG_HEADS * HG_CHUNK), const3),
        ],
        out_specs=[pl.BlockSpec(blk, fwd(0)), pl.BlockSpec(blk, bwd(0))],
        out_shape=[jax.ShapeDtypeStruct((B, S, HG_W), F32)] * 2,
        scratch_shapes=[pltpu.VMEM((2 * bs, HG_W, HG_W), F32)],
        name="hgrn_scan",
        compiler_params=_cparams(("arbitrary", "arbitrary")),
    )(hg, hg, hqg, hg, hg, hqg, lb_l.reshape(2, 1, HG_W), tri, lvl)


def _mixout_kernel(xc_ref, x_ref, att_ref, of_ref, ob_ref, g_ref, cv_ref, cprev_ref, cnext_ref,
                   mod_ref, gain_ref, cw_ref, wo_ref, n2_ref, wr_ref, ones_ref,
                   x1_ref, h2_ref, aff_ref, *, tile0, ntile):
    b = pl.program_id(0)
    i = pl.program_id(1) + tile0
    row = jnp.where(i == 0, 4, b)
    R = ROW_TILE

    def modv(c):
        return mod_ref[pl.ds(row, 1), c * D_MODEL:(c + 1) * D_MODEL]

    o = of_ref[0] + ob_ref[0]
    sq = o * o
    sq_hi = sq.astype(BF16)
    sq_lo = (sq - sq_hi.astype(F32)).astype(BF16)
    ms = _dot(sq_hi, ones_ref[...]) + _dot(sq_lo, ones_ref[...])
    g = g_ref[0]
    hg = o * lax.rsqrt(ms * (1.0 / HEAD_DIM) + EPS) * gain_ref[...] * (g * jax.nn.sigmoid(g))
    cv = cv_ref[0]
    u = cv[:, CV_W:2 * CV_W] * cv[:, 2 * CV_W:3 * CV_W]
    up = cprev_ref[0]
    un = cnext_ref[0]
    u_prev_row = up[7:8, CV_W:2 * CV_W] * up[7:8, 2 * CV_W:3 * CV_W]
    u_next_row = un[0:1, CV_W:2 * CV_W] * un[0:1, 2 * CV_W:3 * CV_W]
    u_prev_row = jnp.where(i <= 1, 0.0, u_prev_row)
    u_next_row = jnp.where((i == 0) | (i == ntile - 1), 0.0, u_next_row)
    ridx = lax.broadcasted_iota(I32, (R, CV_W), 0)
    u_m1 = jnp.where(ridx == 0, u_prev_row, pltpu.roll(u, 1, 0))
    u_p1 = jnp.where(ridx == R - 1, u_next_row, pltpu.roll(u, R - 1, 0))
    cw = cw_ref[...]
    conv = cv[:, 0:CV_W] * (u_m1 * cw[0:1] + u * cw[1:2] + u_p1 * cw[2:3])
    mix = (_dot(att_ref[0], wo_ref[0:ATT_W])
           + _dot(hg.astype(BF16), wo_ref[ATT_W:ATT_W + HG_W])
           + _dot(conv.astype(BF16), wo_ref[ATT_W + HG_W:D_MODEL]))
    x1 = jnp.where(i == 0, xc_ref[0], x_ref[0]) + modv(2) * mix
    x1_ref[0] = x1
    h2 = _rmsnorm_mod(x1, n2_ref[...], modv(3), modv(4))
    h_hi = h2.astype(BF16)
    h2_ref[0] = h_hi
    h_lo = (h2 - h_hi.astype(F32)).astype(BF16)
    wr = wr_ref[...]
    w_hi = wr.astype(BF16)
    w_lo = (wr - w_hi.astype(F32)).astype(BF16)
    r1 = _dot_nt(jnp.concatenate([w_hi, w_lo], axis=0), h_hi)
    logits = r1[0:N_EXPERTS] + r1[N_EXPERTS:2 * N_EXPERTS] + _dot_nt(w_hi, h_lo)
    e = jnp.exp(logits - jnp.max(logits, axis=0, keepdims=True))
    aff_ref[0] = e / jnp.sum(e, axis=0, keepdims=True)


def _mix_out(src, att, o2, hqg, cv, mod_l, gain, conv_w, wo_bf16, n2g, wr_t, skip_ctx):
    xc, xl, off, S = _row_sources(src)
    B = xc.shape[0]
    ntile = S // ROW_TILE
    tile0 = 1 if skip_ctx else 0
    sub = ROW_TILE // 8
    nsub = S // 8
    rmap = lambda b, i: (b, i + tile0, 0)
    omap = lambda b, i: (b, i, 0)
    s_out = S - tile0 * ROW_TILE
    const2 = lambda b, i: (0, 0)
    ones = jnp.asarray(np.kron(np.eye(HG_HEADS), np.ones((HEAD_DIM, HEAD_DIM))), BF16)
    kern = functools.partial(_mixout_kernel, tile0=tile0, ntile=ntile)
    return pl.pallas_call(
        kern,
        grid=(B, ntile - tile0),
        in_specs=[
            pl.BlockSpec((1, ROW_TILE, D_MODEL), lambda b, i: (b, 0, 0)),
            pl.BlockSpec((1, ROW_TILE, D_MODEL),
                         lambda b, i: (b, jnp.maximum(i + tile0 - off, 0), 0)),
            pl.BlockSpec((1, ROW_TILE, ATT_W), omap),
            pl.BlockSpec((1, ROW_TILE, HG_W), rmap),
            pl.BlockSpec((1, ROW_TILE, HG_W), rmap),
            pl.BlockSpec((1, ROW_TILE, HG_W), lambda b, i: (b, i + tile0, 1)),
            pl.BlockSpec((1, ROW_TILE, 3 * CV_W), rmap),
            pl.BlockSpec((1, 8, 3 * CV_W),
                         lambda b, i: (b, jnp.maximum((i + tile0) * sub - 1, 0), 0)),
            pl.BlockSpec((1, 8, 3 * CV_W),
                         lambda b, i: (b, jnp.minimum((i + tile0 + 1) * sub, nsub - 1), 0)),
            pl.BlockSpec((8, N_MOD * D_MODEL), const2),
            pl.BlockSpec((1, HG_W), const2),
            pl.BlockSpec((3, CV_W), const2),
            pl.BlockSpec((D_MODEL, D_MODEL), const2),
            pl.BlockSpec((1, D_MODEL), const2),
            pl.BlockSpec((N_EXPERTS, D_MODEL), const2),
            pl.BlockSpec((HG_W, HG_W), const2),
        ],
        out_specs=[
            pl.BlockSpec((1, ROW_TILE, D_MODEL), omap),
            pl.BlockSpec((1, ROW_TILE, D_MODEL), omap),
            pl.BlockSpec((1, N_EXPERTS, ROW_TILE), lambda b, i: (b, 0, i)),
        ],
        out_shape=[
            jax.ShapeDtypeStruct((B, s_out, D_MODEL), F32),
            jax.ShapeDtypeStruct((B, s_out, D_MODEL), BF16),
            jax.ShapeDtypeStruct((B, N_EXPERTS, s_out), F32),
        ],
        name="mix_out",
        compiler_params=_cparams(("arbitrary", "arbitrary")),
    )(xc, xl, att, o2[0], o2[1], hqg, cv, cv, cv, mod_l, gain, conv_w, wo_bf16, n2g, wr_t, ones)


def _topk_kernel(aff_ref, tri_ref, pos_ref, cnt_ref, *, segments):
    tri = tri_ref[...]

    def excl_cumsum(mask, n):
        carry = jnp.zeros((N_EXPERTS, 1), F32)
        parts = []
        carries = []
        for c in range(n // 128):
            carries.append(carry)
            blk = jnp.where(mask[:, c * 128:(c + 1) * 128], 1.0, 0.0)
            parts.append(_dot(blk.astype(BF16), tri) + carry)
            carry = carry + jnp.sum(blk, axis=1, keepdims=True)
        return jnp.concatenate(parts, axis=1), carries + [carry]

    for (lo, n, k) in segments:
        a = aff_ref[0, :, lo:lo + n]

        def body(it, thr):
            cand = thr | jnp.left_shift(jnp.int32(1), 30 - it)
            cnt = jnp.sum(jnp.where(a >= pltpu.bitcast(cand, F32), 1.0, 0.0), axis=1, keepdims=True)
            return jnp.where(cnt >= k, cand, thr)

        thr = lax.fori_loop(0, 31, body, jnp.zeros((N_EXPERTS, 1), I32))
        above = a >= pltpu.bitcast(thr + 1, F32)
        tied = jnp.logical_and(a >= pltpu.bitcast(thr, F32), jnp.logical_not(above))
        n_above = jnp.sum(jnp.where(above, 1.0, 0.0), axis=1, keepdims=True)
        rank_tied, _ = excl_cumsum(tied, n)
        sel = above | (tied & (rank_tied < (k - n_above)))
        pos, counts = excl_cumsum(sel, n)
        pos_ref[0, :, lo:lo + n] = jnp.where(sel, pos.astype(I32), -1)
    step = ROW_TILE // 128
    lane = lax.broadcasted_iota(I32, (N_EXPERTS, 128), 1)
    cnt = jnp.zeros((N_EXPERTS, 128), F32)
    for c, col in enumerate(counts[0::step]):
        cnt = jnp.where(lane == c, col, cnt)
    cnt_ref[0] = cnt.astype(I32)


def _topk_positions(aff, segments):
    B, E, S = aff.shape
    assert segments[-1][1] // ROW_TILE + 1 <= 128
    tri = jnp.asarray(np.triu(np.ones((128, 128)), 1), BF16)
    kern = functools.partial(_topk_kernel, segments=segments)
    return pl.pallas_call(
        kern,
        grid=(B,),
        in_specs=[pl.BlockSpec((1, E, S), lambda b: (b, 0, 0)),
                  pl.BlockSpec((128, 128), lambda b: (0, 0))],
        out_specs=[pl.BlockSpec((1, E, S), lambda b: (b, 0, 0)),
                   pl.BlockSpec((1, E, 128), lambda b: (b, 0, 0))],
        out_shape=[jax.ShapeDtypeStruct((B, E, S), I32),
                   jax.ShapeDtypeStruct((B, E, 128), I32)],
        name="topk_positions",
        compiler_params=_cparams(("arbitrary",)),
    )(aff, tri)


def _div_pow2(x, d):
    assert d & (d - 1) == 0
    return lax.shift_right_logical(x, jnp.int32(d.bit_length() - 1))


def _gather_kernel(cnt_ref, pos_ref, aff_ref, h_ref, *refs, cap_l, cap_c):
    if cap_c:
        xl_ref, gi_ref, xc_ref, gc_ref = refs
    else:
        xl_ref, gi_ref = refs
    b = pl.program_id(0)
    e0 = pl.program_id(1) * GATHER_GROUP
    j = pl.program_id(2)
    G = GATHER_GROUP
    GR = PACK_GRANULE
    W = min(GATHER_WINDOW, cap_l)
    pos = pos_ref[0]
    aff = aff_ref[0]
    h = h_ref[0]

    @pl.when(j == 0)
    def _():
        xl_ref[...] = jnp.zeros_like(xl_ref)
        gi_ref[...] = jnp.zeros_like(gi_ref)

    first_lane = lax.broadcasted_iota(I32, (W, 2), 1) == 0

    def add_rows(g, start, onehot_f32, rows):
        a = pl.multiple_of(start, GR)
        xl_ref[g, 0, pl.ds(a, W), :] += rows.astype(BF16)
        gate = jnp.sum(onehot_f32 * aff[g:g + 1, :], axis=1, keepdims=True)
        tok_id = jnp.sum(onehot_f32 * tok, axis=1, keepdims=True)
        gi_ref[g, 0, pl.ds(a, W), :] += jnp.where(first_lane, gate, tok_id)

    def latent():
        c = j - 1 if cap_c else j
        slot_i32 = lax.broadcasted_iota(I32, (W, ROW_TILE), 0)
        starts = [jnp.minimum(_div_pow2(cnt_ref[b, e0 + g, c], GR) * GR, cap_l - W)
                  for g in range(G)]
        hits = [(pos[g:g + 1, :] - starts[g]) == slot_i32 for g in range(G)]
        onehots = [jnp.where(hit, 1.0, 0.0) for hit in hits]
        prod = _dot(jnp.concatenate(onehots, axis=0).astype(BF16), h)
        for g in range(G):
            add_rows(g, starts[g], onehots[g], prod[g * W:(g + 1) * W])
        over = [cnt_ref[b, e0 + g, c + 1] - (starts[g] + W) for g in range(G)]
        worst = over[0]
        for o in over[1:]:
            worst = jnp.maximum(worst, o)

        @pl.when(worst > 0)
        def _():
            for g in range(G):
                def extra_window(k, carry, g=g):
                    first_slot = starts[g] + (k + 1) * W
                    start = jnp.minimum(first_slot, cap_l - W)
                    pg = pos[g:g + 1, :]
                    hit = ((pg - start) == slot_i32) & (pg >= first_slot)
                    onehot = jnp.where(hit, 1.0, 0.0)
                    add_rows(g, start, onehot, _dot(onehot.astype(BF16), h))
                    return carry

                lax.fori_loop(0, _div_pow2(jnp.maximum(over[g], 0) + (W - 1), W), extra_window, 0)

    tok = None
    if cap_c:
        tok = (lax.broadcasted_iota(I32, (1, ROW_TILE), 1) + (j - 1) * ROW_TILE).astype(F32)

        @pl.when(j == 0)
        def _():
            slot_c = lax.broadcasted_iota(I32, (cap_c, ROW_TILE), 0)
            onehots = [jnp.where(pos[g:g + 1, :] == slot_c, 1.0, 0.0) for g in range(G)]
            prod = _dot(jnp.concatenate(onehots, axis=0).astype(BF16), h)
            for g in range(G):
                xc_ref[g, 0] = prod[g * cap_c:(g + 1) * cap_c].astype(BF16)
                gc_ref[g, 0] = jnp.sum(onehots[g] * aff[g:g + 1, :], axis=1, keepdims=True)

        pl.when(j > 0)(latent)
    else:
        tok = (lax.broadcasted_iota(I32, (1, ROW_TILE), 1) + j * ROW_TILE).astype(F32)
        latent()


def _gather_rows(cnt, posm, aff, h2, cap_l, cap_c):
    B, E, S = posm.shape
    G = GATHER_GROUP
    assert min(GATHER_WINDOW, cap_l) % PACK_GRANULE == 0 and cap_l % PACK_GRANULE == 0
    kern = functools.partial(_gather_kernel, cap_l=cap_l, cap_c=cap_c)
    omap = lambda b, g, j, cnt_ref: (g, b, 0, 0)
    out_specs = [pl.BlockSpec((G, 1, cap_l, D_MODEL), omap), pl.BlockSpec((G, 1, cap_l, 2), omap)]
    out_shape = [jax.ShapeDtypeStruct((E, B, cap_l, D_MODEL), BF16),
                 jax.ShapeDtypeStruct((E, B, cap_l, 2), F32)]
    if cap_c:
        out_specs += [pl.BlockSpec((G, 1, cap_c, D_MODEL), omap),
                      pl.BlockSpec((G, 1, cap_c, 1), omap)]
        out_shape += [jax.ShapeDtypeStruct((E, B, cap_c, D_MODEL), BF16),
                      jax.ShapeDtypeStruct((E, B, cap_c, 1), F32)]
    tile_map = lambda b, g, j, cnt_ref: (b, g, j)
    grid_spec = pltpu.PrefetchScalarGridSpec(
        num_scalar_prefetch=1,
        grid=(B, E // G, S // ROW_TILE),
        in_specs=[pl.BlockSpec((1, G, ROW_TILE), tile_map),
                  pl.BlockSpec((1, G, ROW_TILE), tile_map),
                  pl.BlockSpec((1, ROW_TILE, D_MODEL), lambda b, g, j, cnt_ref: (b, j, 0))],
        out_specs=out_specs,
    )
    return pl.pallas_call(
        kern,
        grid_spec=grid_spec,
        out_shape=out_shape,
        name="gather_rows",
        compiler_params=_cparams(("arbitrary", "arbitrary", "arbitrary")),
    )(cnt, posm, aff, h2)


def _expert_kernel(*refs, row_chunk, n_sets):
    x_refs = refs[:n_sets]
    g_refs = refs[n_sets:2 * n_sets]
    wg_ref, wu_ref, wd_ref = refs[2 * n_sets:2 * n_sets + 3]
    y_refs = refs[2 * n_sets + 3:3 * n_sets + 3]
    acc_refs = refs[3 * n_sets + 3:]
    f = pl.program_id(1)
    nf = D_EXPERT // FF_TILE
    wg = wg_ref[0, 0].astype(BF16)
    wu = wu_ref[0, 0].astype(BF16)
    wd = wd_ref[0, 0].astype(BF16)

    def hidden_tile(first, last):
        for x_ref, g_ref, y_ref, acc_ref in zip(x_refs, g_refs, y_refs, acc_refs):
            rows = x_ref.shape[1]
            step = min(row_chunk, rows)
            for r in range(rows // step):
                rs = slice(r * step, (r + 1) * step)
                x = x_ref[0, rs, :]
                g = _dot(x, wg)
                u = _dot(x, wu)
                y = _dot((g * jax.nn.sigmoid(g) * u).astype(BF16), wd)
                if not first:
                    y = acc_ref[rs, :] + y
                if last:
                    y_ref[0, rs, :] = (y * g_ref[0, rs, 0:1]).astype(BF16)
                else:
                    acc_ref[rs, :] = y

    if nf == 1:
        hidden_tile(True, True)
    else:
        pl.when(f == 0)(functools.partial(hidden_tile, True, False))
        if nf > 2:
            pl.when((f > 0) & (f < nf - 1))(functools.partial(hidden_tile, False, False))
        pl.when(f == nf - 1)(functools.partial(hidden_tile, False, True))


def _expert_mlp(x_sets, g_sets, w_gate, w_up, w_down, layer, row_chunk):
    E = N_EXPERTS
    nf = D_EXPERT // FF_TILE
    n_sets = len(x_sets)
    kern = functools.partial(_expert_kernel, row_chunk=row_chunk, n_sets=n_sets)
    xspec = lambda r: pl.BlockSpec((1, r, D_MODEL), lambda e, f: (e, 0, 0))
    gspec = lambda gs: pl.BlockSpec((1,) + gs.shape[1:], lambda e, f: (e, 0, 0))
    return pl.pallas_call(
        kern,
        grid=(E, nf),
        in_specs=[xspec(xs.shape[1]) for xs in x_sets] + [gspec(gs) for gs in g_sets] + [
            pl.BlockSpec((1, 1, D_MODEL, FF_TILE), lambda e, f: (layer, e, 0, f)),
            pl.BlockSpec((1, 1, D_MODEL, FF_TILE), lambda e, f: (layer, e, 0, f)),
            pl.BlockSpec((1, 1, FF_TILE, D_MODEL), lambda e, f: (layer, e, f, 0))],
        out_specs=[xspec(xs.shape[1]) for xs in x_sets],
        out_shape=[jax.ShapeDtypeStruct(xs.shape, BF16) for xs in x_sets],
        scratch_shapes=[pltpu.VMEM(xs.shape[1:], F32) for xs in x_sets],
        name="expert_mlp",
        compiler_params=_cparams(("arbitrary", "arbitrary")),
    )(*x_sets, *g_sets, w_gate, w_up, w_down)


def _combine_kernel(cnt_ref, x_ref, pos_ref, *refs, cap_l, cap_c, final):
    if cap_c:
        yl_ref, il_ref, yc_ref, mod_ref, fg_ref, o_ref, acc_ref, pk_ref, tk_ref = refs
    else:
        yl_ref, il_ref, mod_ref, fg_ref, o_ref, acc_ref, pk_ref, tk_ref = refs
    b = pl.program_id(0)
    i = pl.program_id(1)
    row = jnp.where(i == 0, 4, b) if cap_c else b
    gate = mod_ref[pl.ds(row, 1), 5 * D_MODEL:6 * D_MODEL]
    pos = pos_ref[0]
    acc_ref[...] = jnp.zeros_like(acc_ref)
    GR = PACK_GRANULE
    W = min(GATHER_WINDOW, cap_l)

    def add_latent():
        t = i - 1 if cap_c else i
        starts = [jnp.minimum(_div_pow2(cnt_ref[b, e, t], GR) * GR, cap_l - W)
                  for e in range(N_EXPERTS)]
        for e in range(N_EXPERTS):
            a = pl.multiple_of(starts[e], GR)
            pk_ref[e * W:(e + 1) * W, :] = yl_ref[e, 0, pl.ds(a, W), :]
            tk_ref[e * W:(e + 1) * W, :] = il_ref[e, 0, pl.ds(a, W), 1:2].astype(I32)

        def product(tok_ids, rows):
            lane = lax.broadcasted_iota(I32, (tok_ids.shape[0], ROW_TILE), 1) + t * ROW_TILE
            onehot = jnp.where(tok_ids == lane, 1.0, 0.0)
            return _dot(onehot.T.astype(BF16), rows)

        acc_ref[...] += product(tk_ref[...], pk_ref[...])
        over = [cnt_ref[b, e, t + 1] - (starts[e] + W) for e in range(N_EXPERTS)]
        worst = over[0]
        for o in over[1:]:
            worst = jnp.maximum(worst, o)

        @pl.when(worst > 0)
        def _():
            wx = min(2 * W, cap_l)
            slot = lax.broadcasted_iota(I32, (wx, 1), 0)
            for e in range(N_EXPERTS):
                def extra_window(k, carry, e=e):
                    first_slot = starts[e] + W + k * wx
                    a = pl.multiple_of(jnp.minimum(first_slot, cap_l - wx), GR)
                    ids = jnp.where(slot + a >= first_slot,
                                    il_ref[e, 0, pl.ds(a, wx), 1:2].astype(I32), -1)
                    acc_ref[...] += product(ids, yl_ref[e, 0, pl.ds(a, wx), :])
                    return carry

                lax.fori_loop(0, _div_pow2(jnp.maximum(over[e], 0) + (wx - 1), wx),
                              extra_window, 0)

    def add_context():
        ncol = yc_ref.shape[1]
        lane = lax.broadcasted_iota(I32, (ROW_TILE, ncol), 1)
        for e in range(N_EXPERTS):
            pe = pos[:, e:e + 1]
            tgt = jnp.where(pe >= 0, pe + b * cap_c, -1)
            acc_ref[...] += _dot(jnp.where(tgt == lane, 1.0, 0.0).astype(BF16), yc_ref[e])

    if cap_c:
        pl.when(i == 0)(add_context)
        pl.when(i > 0)(add_latent)
    else:
        add_latent()
    x2 = x_ref[0] + gate * acc_ref[...]
    if final:
        var = jnp.mean(x2 * x2, axis=-1, keepdims=True)
        x2 = x2 * lax.rsqrt(var + EPS) * fg_ref[...]
    o_ref[0] = x2


def _combine(cnt, x1, pos_t, yl, il, yc, mod_l, final_g, cap_l, cap_c, final):
    B, S, _ = x1.shape
    E = N_EXPERTS
    rmap = lambda b, i, cnt_ref: (b, i, 0)
    const2 = lambda b, i, cnt_ref: (0, 0)
    kern = functools.partial(_combine_kernel, cap_l=cap_l, cap_c=cap_c, final=final)
    per_sample = lambda b, i, cnt_ref: (0, b, 0, 0)
    y_specs = [pl.BlockSpec((E, 1, cap_l, D_MODEL), per_sample, pipeline_mode=pl.Buffered(1)),
               pl.BlockSpec((E, 1, cap_l, 2), per_sample, pipeline_mode=pl.Buffered(1))]
    y_args = [yl, il]
    pack_rows = E * min(GATHER_WINDOW, cap_l)
    if cap_c:
        y_specs.append(pl.BlockSpec(yc.shape, lambda b, i, cnt_ref: (0, 0, 0),
                                    pipeline_mode=pl.Buffered(1)))
        y_args.append(yc)
    grid_spec = pltpu.PrefetchScalarGridSpec(
        num_scalar_prefetch=1,
        grid=(B, S // ROW_TILE),
        in_specs=[pl.BlockSpec((1, ROW_TILE, D_MODEL), rmap),
                  pl.BlockSpec((1, ROW_TILE, E), rmap)] + y_specs + [
                  pl.BlockSpec((8, N_MOD * D_MODEL), const2),
                  pl.BlockSpec((1, D_MODEL), const2)],
        out_specs=pl.BlockSpec((1, ROW_TILE, D_MODEL), rmap),
        scratch_shapes=[pltpu.VMEM((ROW_TILE, D_MODEL), F32),
                        pltpu.VMEM((pack_rows, D_MODEL), BF16),
                        pltpu.VMEM((pack_rows, 1), I32)],
    )
    return pl.pallas_call(
        kern,
        grid_spec=grid_spec,
        out_shape=jax.ShapeDtypeStruct((B, S, D_MODEL), F32),
        name="combine",
        compiler_params=_cparams(("arbitrary", "arbitrary")),
    )(cnt, x1, pos_t, *y_args, mod_l, final_g)


def _rope_tables(n_lat, ctx_len):
    t = np.arange(n_lat)
    pos = np.stack([t // GRID_W, t % GRID_W], axis=-1).astype(np.float32)
    inv = (ROPE_BASE ** (-np.arange(ROPE_FREQS, dtype=np.float32) / ROPE_FREQS)).astype(np.float32)
    ang = pos[:, :, None] * inv
    cos = np.cos(ang).astype(np.float32)
    sin = np.sin(ang).astype(np.float32)
    cos64 = np.concatenate([cos[:, 0], cos[:, 0], cos[:, 1], cos[:, 1]], axis=-1)
    sin64 = np.concatenate([-sin[:, 0], sin[:, 0], -sin[:, 1], sin[:, 1]], axis=-1)
    cos_t = np.concatenate([np.ones((ctx_len, 64), np.float32), cos64], axis=0)
    sin_t = np.concatenate([np.zeros((ctx_len, 64), np.float32), sin64], axis=0)
    return (jnp.asarray(np.tile(cos_t, (1, 2))), jnp.asarray(np.tile(sin_t, (1, 2))))


def kernel(x, c, ctx, c_ctx, ada_w, ada_b, norm1_g, norm2_g, w_in, attn_sink, hgrn_lb,
           hgrn_norm_g, conv_w, w_o, router_w, exp_w_gate, exp_w_up, exp_w_down, final_norm_g):
    B, T, D = x.shape
    L = ctx.shape[1]
    depth = ada_w.shape[0]
    assert D == D_MODEL and L == ROW_TILE and T % ROW_TILE == 0 and B <= 4
    S = L + T
    cap_l = EC_CAPACITY * T // N_EXPERTS
    cap_c = EC_CAPACITY * L // N_EXPERTS
    assert cap_l % 16 == 0 and cap_c % 16 == 0 and (B * cap_c) % 16 == 0

    cos_t, sin_t = _rope_tables(T, L)
    gamma = jax.nn.softmax(hgrn_lb.astype(F32), axis=0)
    lb_all = jnp.cumsum(gamma, axis=0) - gamma[0]
    cvec = jnp.concatenate([c, jnp.zeros((4 - B, D), F32), c_ctx[None],
                            jnp.zeros((3, D), F32)], axis=0)
    mod = _modulation(cvec, ada_w, ada_b)
    xs = (ctx, x)

    def interleave_heads(w, axis):
        shp = w.shape
        w = w.reshape(shp[:axis] + (N_KV, GQA_GROUP, HEAD_DIM) + shp[axis + 1:])
        return jnp.swapaxes(w, axis, axis + 1).reshape(shp)

    for l in range(depth):
        last = l == depth - 1
        w_in_l = w_in[l].astype(BF16)
        w_in_l = jnp.concatenate([w_in_l[:, :1024], interleave_heads(w_in_l[:, 1024:1536], 1),
                                  w_in_l[:, 1536:]], axis=1)
        w_o_l = w_o[l].astype(BF16)
        w_o_l = jnp.concatenate([interleave_heads(w_o_l[:ATT_W], 0), w_o_l[ATT_W:]], axis=0)
        q, kv, hg, hqg, cv = _in_projection(xs, mod[l], norm1_g[l][None], w_in_l, cos_t, sin_t)
        att = _attention(q, kv, attn_sink[l], L, skip_ctx=last)
        o2 = _hgrn(hg, hqg, lb_all[l], L, HG_LOCAL_FIRST if l == 0 else HG_LOCAL)
        x1, h2, aff = _mix_out(xs, att, o2, hqg, cv, mod[l], hgrn_norm_g[l][None], conv_w[l],
                               w_o_l, norm2_g[l][None], router_w[l].T, skip_ctx=last)
        lm, cc = (0, 0) if last else (L, cap_c)
        segments = ((lm, T, cap_l),) if last else ((0, L, cap_c), (L, T, cap_l))
        posm, cnt = _topk_positions(aff, segments)
        gathered = _gather_rows(cnt, posm, aff, h2, cap_l, cc)
        il = gathered[1]
        x_sets = [a.reshape(N_EXPERTS, -1, D) for a in gathered[0::2]]
        g_sets = [a.reshape(N_EXPERTS, -1, a.shape[-1]) for a in gathered[1::2]]
        ys = _expert_mlp(x_sets, g_sets, exp_w_gate, exp_w_up, exp_w_down, l, EXPERT_ROWS)
        yl = ys[0].reshape(N_EXPERTS, B, cap_l, D)
        xs = _combine(cnt, x1, jnp.swapaxes(posm, 1, 2), yl, il, None if last else ys[1], mod[l],
                      final_norm_g[None], cap_l, cc, final=last)
    return xs
```

```python
import functools

import numpy as np
import jax
import jax.numpy as jnp
from jax import lax
from jax.experimental import pallas as pl
from jax.experimental.pallas import tpu as pltpu

F32 = jnp.float32
BF16 = jnp.bfloat16
I32 = jnp.int32
HIGHEST = lax.Precision.HIGHEST

D_MODEL = 1024
GRID_W = 64
EPS = 1e-6
LB_FLOOR = 1e-30
N_MOD = 6
ATT_W = 512
HG_W = 256
CV_W = 256
HEAD_DIM = 64
N_Q = 8
N_KV = 2
GQA_GROUP = 4
KV_W = 128
ROPE_BASE = 10000.0
ROPE_FREQS = 16
HG_HEADS = 4
N_EXPERTS = 16
EC_CAPACITY = 2
D_EXPERT = 2048
IN_COLS = 2816

ROW_TILE = 256
ATT_BLOCK = 128
HG_CHUNK = 128
HG_LEVELS = 7
HG_BATCH = 4
HG_LOCAL = 64
HG_LOCAL_FIRST = 32
HG_LOCAL_MAX_LOG = 115.0
GATHER_GROUP = 16
GATHER_WINDOW = 64
PACK_GRANULE = 16
FF_TILE = 512
EXPERT_ROWS = 1024
MOD_TILE = 1536
VMEM_LIMIT = 56 * 1024 * 1024

NEG_BIG = -1e30
LOG2E = 1.4426950408889634


def _cparams(sem):
    return pltpu.CompilerParams(dimension_semantics=sem, vmem_limit_bytes=VMEM_LIMIT)


def _dot(a, b):
    return jnp.dot(a, b, preferred_element_type=F32)


def _dot_nt(a, b):
    return lax.dot_general(a, b, (((1,), (1,)), ((), ())), preferred_element_type=F32)


def _mod_kernel(a_ref, w_ref, b_ref, o_ref):
    a = a_ref[...]
    a = a * jax.nn.sigmoid(a)
    o_ref[0] = jnp.dot(a, w_ref[0], precision=HIGHEST, preferred_element_type=F32) + b_ref[0]


def _modulation(cvec, ada_w, ada_b):
    depth = ada_w.shape[0]
    ncol = ada_w.shape[2]
    return pl.pallas_call(
        _mod_kernel,
        grid=(depth, ncol // MOD_TILE),
        in_specs=[
            pl.BlockSpec((8, D_MODEL), lambda l, j: (0, 0)),
            pl.BlockSpec((1, D_MODEL, MOD_TILE), lambda l, j: (l, 0, j)),
            pl.BlockSpec((1, 1, MOD_TILE), lambda l, j: (l, 0, j)),
        ],
        out_specs=pl.BlockSpec((1, 8, MOD_TILE), lambda l, j: (l, 0, j)),
        out_shape=jax.ShapeDtypeStruct((depth, 8, ncol), F32),
        name="modulation",
        compiler_params=_cparams(("arbitrary", "arbitrary")),
    )(cvec, ada_w, ada_b.reshape(depth, 1, ncol))


def _swap_halves(x):
    n = x.shape[-1]
    lane = lax.broadcasted_iota(I32, x.shape, x.ndim - 1)
    up = pltpu.roll(x, n - ROPE_FREQS, x.ndim - 1)
    dn = pltpu.roll(x, ROPE_FREQS, x.ndim - 1)
    return jnp.where((lane % (2 * ROPE_FREQS)) < ROPE_FREQS, up, dn)


def _rmsnorm_mod(x, g, shift, scale):
    var = jnp.mean(x * x, axis=-1, keepdims=True)
    y = x * lax.rsqrt(var + EPS) * g
    return y * (1.0 + scale) + shift


def _row_sources(src):
    if isinstance(src, tuple):
        ctx, lat = src
        return ctx, lat, 1, ctx.shape[1] + lat.shape[1]
    return src, src, 0, src.shape[1]


def _inproj_kernel(xc_ref, x_ref, mod_ref, g_ref, w_ref, cos_ref, sin_ref,
                   q_ref, kv_ref, hg_ref, hqg_ref, cv_ref):
    b = pl.program_id(0)
    i = pl.program_id(1)
    row = jnp.where(i == 0, 4, b)
    shift = mod_ref[pl.ds(row, 1), 0:D_MODEL]
    scale = mod_ref[pl.ds(row, 1), D_MODEL:2 * D_MODEL]
    xin = jnp.where(i == 0, xc_ref[0], x_ref[0])
    h = _rmsnorm_mod(xin, g_ref[...], shift, scale)
    p = _dot(h.astype(BF16), w_ref[...])
    cos2 = cos_ref[...]
    sin2 = sin_ref[...]
    k = p[:, 0:KV_W]
    k = k * cos2 + _swap_halves(k) * sin2
    kv_ref[0, :, 0:KV_W] = k.astype(BF16)
    kv_ref[0, :, KV_W:2 * KV_W] = p[:, KV_W:2 * KV_W].astype(BF16)
    hg_ref[0] = p[:, 256:1024]
    q = p[:, 1024:1536]
    cos8 = jnp.concatenate([cos2] * 4, axis=1)
    sin8 = jnp.concatenate([sin2] * 4, axis=1)
    q = (q * cos8 + _swap_halves(q) * sin8) * (HEAD_DIM ** -0.5 * LOG2E)
    q_ref[0] = q.astype(BF16)
    hqg_ref[0] = p[:, 1536:2048]
    cv_ref[0] = p[:, 2048:2816]


def _in_projection(src, mod_l, g, w_bf16, cos_t, sin_t):
    xc, xl, off, S = _row_sources(src)
    B = xc.shape[0]
    nt = S // ROW_TILE
    row_map = lambda b, i: (b, i, 0)
    const2 = lambda b, i: (0, 0)
    return pl.pallas_call(
        _inproj_kernel,
        grid=(B, nt),
        in_specs=[
            pl.BlockSpec((1, ROW_TILE, D_MODEL), lambda b, i: (b, 0, 0)),
            pl.BlockSpec((1, ROW_TILE, D_MODEL), lambda b, i: (b, jnp.maximum(i - off, 0), 0)),
            pl.BlockSpec((8, N_MOD * D_MODEL), const2),
            pl.BlockSpec((1, D_MODEL), const2),
            pl.BlockSpec((D_MODEL, IN_COLS), const2),
            pl.BlockSpec((ROW_TILE, 2 * HEAD_DIM), lambda b, i: (i, 0)),
            pl.BlockSpec((ROW_TILE, 2 * HEAD_DIM), lambda b, i: (i, 0)),
        ],
        out_specs=[
            pl.BlockSpec((1, ROW_TILE, ATT_W), row_map),
            pl.BlockSpec((1, ROW_TILE, 2 * KV_W), row_map),
            pl.BlockSpec((1, ROW_TILE, 3 * HG_W), row_map),
            pl.BlockSpec((1, ROW_TILE, 2 * HG_W), row_map),
            pl.BlockSpec((1, ROW_TILE, 3 * CV_W), row_map),
        ],
        out_shape=[
            jax.ShapeDtypeStruct((B, S, ATT_W), BF16),
            jax.ShapeDtypeStruct((B, S, 2 * KV_W), BF16),
            jax.ShapeDtypeStruct((B, S, 3 * HG_W), F32),
            jax.ShapeDtypeStruct((B, S, 2 * HG_W), F32),
            jax.ShapeDtypeStruct((B, S, 3 * CV_W), F32),
        ],
        name="in_projection",
        compiler_params=_cparams(("arbitrary", "arbitrary")),
    )(xc, xl, mod_l, g, w_bf16, cos_t, sin_t)


def _attn_kernel(sink_ref, q_ref, kp_ref, kc_ref, kn_ref, kx_ref, o_ref, *, blk0, nblk, nctx):
    n = pl.program_id(1) + blk0
    is_lat = n >= nctx
    has_prev = n > nctx
    has_next = n < nblk - 1
    W = ATT_BLOCK
    q = q_ref[0]
    qrows = jnp.concatenate([q[:, g * W:(g + 1) * W] for g in range(GQA_GROUP)], axis=0)
    kv_all = jnp.concatenate([kp_ref[0], kc_ref[0], kn_ref[0], kx_ref[0]], axis=0)
    nkeys = kv_all.shape[0]
    k_all = kv_all[:, 0:KV_W]
    v_ext = jnp.concatenate([kv_all[:, KV_W:2 * KV_W], jnp.ones((nkeys, KV_W), BF16)], axis=1)
    rows = GQA_GROUP * W
    ri = lax.broadcasted_iota(I32, (rows, W), 0) % W
    cj = lax.broadcasted_iota(I32, (rows, W), 1)
    m_prev = (cj >= ri) & has_prev
    m_cur = jnp.broadcast_to(is_lat, (rows, W))
    m_next = (cj <= ri) & (has_next & is_lat)
    grp = lax.broadcasted_iota(I32, (rows, 1), 0) // W
    low = cj < HEAD_DIM
    outs = []
    for h in range(N_KV):
        qh = jnp.where(low if h == 0 else jnp.logical_not(low), qrows, jnp.zeros_like(qrows))
        sink = jnp.zeros((rows, 1), F32)
        for g in range(GQA_GROUP):
            sink = jnp.where(grp == g, sink_ref[h * GQA_GROUP + g] * LOG2E, sink)
        s = _dot_nt(qh, k_all)
        segs = [jnp.where(m_prev, s[:, 0:W], NEG_BIG),
                jnp.where(m_cur, s[:, W:2 * W], NEG_BIG),
                jnp.where(m_next, s[:, 2 * W:3 * W], NEG_BIG)]
        segs += [s[:, c:c + W] for c in range(3 * W, nkeys, W)]
        mx = segs[0]
        for sg in segs[1:]:
            mx = jnp.maximum(mx, sg)
        m = jnp.maximum(jnp.max(mx, axis=1, keepdims=True), sink)
        p = jnp.concatenate([jnp.exp2(sg - m).astype(BF16) for sg in segs], axis=1)
        oe = _dot(p, v_ext)
        den = oe[:, KV_W:2 * KV_W] + jnp.exp2(sink - m)
        outs.append(oe[:, 0:KV_W] / den)
    o = jnp.where(low, outs[0], outs[1])
    for g in range(GQA_GROUP):
        o_ref[0, :, g * W:(g + 1) * W] = o[g * W:(g + 1) * W].astype(BF16)


def _attention(q, kv, sink, ctx_len, skip_ctx):
    B, S, _ = q.shape
    nblk = S // ATT_BLOCK
    nctx = ctx_len // ATT_BLOCK
    blk0 = nctx if skip_ctx else 0
    blk = lambda f: (lambda b, n: (b, f(n + blk0), 0))
    kern = functools.partial(_attn_kernel, blk0=blk0, nblk=nblk, nctx=nctx)
    return pl.pallas_call(
        kern,
        grid=(B, nblk - blk0),
        in_specs=[
            pl.BlockSpec(memory_space=pltpu.SMEM),
            pl.BlockSpec((1, ATT_BLOCK, ATT_W), blk(lambda n: n)),
            pl.BlockSpec((1, ATT_BLOCK, 2 * KV_W), blk(lambda n: jnp.maximum(n - 1, 0))),
            pl.BlockSpec((1, ATT_BLOCK, 2 * KV_W), blk(lambda n: n)),
            pl.BlockSpec((1, ATT_BLOCK, 2 * KV_W), blk(lambda n: jnp.minimum(n + 1, nblk - 1))),
            pl.BlockSpec((1, ctx_len, 2 * KV_W), lambda b, n: (b, 0, 0)),
        ],
        out_specs=pl.BlockSpec((1, ATT_BLOCK, ATT_W), lambda b, n: (b, n, 0)),
        out_shape=jax.ShapeDtypeStruct((B, S - blk0 * ATT_BLOCK, ATT_W), BF16),
        name="attention",
        compiler_params=_cparams(("arbitrary", "arbitrary")),
    )(sink, q, kv, kv, kv, kv)


def _hgrn_constants():
    C = HG_CHUNK
    t = np.arange(C)[:, None]
    r = np.arange(C)[None, :]
    tri = np.stack([r <= t, r >= t]).astype(np.float32)
    x = t ^ r
    lvl = np.where(x > 0, np.floor(np.log2(np.maximum(x, 1))).astype(np.int32), HG_LEVELS)
    lvl_f = np.where(t >= r, lvl, -1).astype(np.int32)
    lvl_b = np.where(t <= r, lvl, -1).astype(np.int32)
    lvl2 = np.stack([np.tile(lvl_f, (1, HG_HEADS)), np.tile(lvl_b, (1, HG_HEADS))])
    return tri, lvl2


def _span_row(x, span, row):
    C = x.shape[0]
    if span >= 8:
        x3 = x.reshape(C // span, span, x.shape[1])
        return jnp.broadcast_to(x3[:, row:row + 1, :], x3.shape).reshape(x.shape)
    pos = lax.broadcasted_iota(I32, x.shape, 0) % span
    out = x
    for p in range(span):
        if p != row:
            out = jnp.where(pos == p, pltpu.roll(x, (p - row) % C, 0), out)
    return out


def _hgrn_prepare(v, z, qr, lb, tri, backward, n_local):
    C = HG_CHUNK
    logf = jnp.log(jnp.maximum(lb, LB_FLOOR) + (1.0 - lb) * jax.nn.sigmoid(z)) * LOG2E
    k = (1.0 - lb) * jax.nn.sigmoid(-z)
    q = qr * jax.nn.sigmoid(qr)
    hi = logf.astype(BF16)
    lo = (logf - hi.astype(F32)).astype(BF16)
    cs = _dot(tri, jnp.concatenate([hi, lo], axis=1))
    lam = cs[:, 0:HG_W] + cs[:, HG_W:2 * HG_W]
    tot = lam[0:1] if backward else lam[C - 1:C]
    local = lam - _span_row(lam, n_local, n_local // 2)
    return dict(v=v, q=q, k=k, lam=lam, tot=tot, local=local)


def _hgrn_finish(g, head_masks, same_head, lvl, st_ref, sidx, backward, fast, n_local):
    C = HG_CHUNK
    q, k, v, lam, tot = g["q"], g["k"], g["v"], g["lam"], g["tot"]
    zero = jnp.zeros((C, HG_W), BF16)

    def per_head_rows(x):
        return jnp.concatenate([jnp.where(hm, x, zero) for hm in head_masks], axis=0)

    qb = q.astype(BF16)
    kb = k.astype(BF16)
    if fast:
        local_levels = n_local.bit_length() - 1
        a = jnp.where((lvl == HG_LEVELS) | ((lvl >= 0) & (lvl < local_levels)),
                      _dot_nt((q * jnp.exp2(g["local"])).astype(BF16),
                              per_head_rows((k * jnp.exp2(-g["local"])).astype(BF16))), 0.0)
        levels = range(local_levels, HG_LEVELS)
    else:
        a = jnp.where(lvl == HG_LEVELS, _dot_nt(qb, per_head_rows(kb)), 0.0)
        levels = range(HG_LEVELS)
    for l in levels:
        m = 1 << l
        ref = _span_row(lam, 2 * m, m if backward else m - 1)
        fac = jnp.exp2(-jnp.abs(lam - ref)).astype(BF16)
        a = jnp.where(lvl == l, _dot_nt(qb * fac, per_head_rows(kb * fac)), a)
    st = st_ref[sidx]
    q_in = (q * jnp.exp2(lam)).astype(BF16)
    o = _dot(a.astype(BF16), per_head_rows(v.astype(BF16))) + _dot_nt(q_in, st.astype(BF16))
    k_out = (k * jnp.exp2(tot - lam)).astype(BF16)
    upd = _dot(v.T.astype(BF16), k_out)
    st_ref[sidx] = st * jnp.exp2(tot) + jnp.where(same_head, upd, 0.0)
    return o


def _hgrn_kernel(vf_ref, zf_ref, qf_ref, vb_ref, zb_ref, qb_ref, lb_ref, tri_ref, lvl_ref,
                 of_ref, ob_ref, st_ref, *, n_local):
    j = pl.program_id(1)

    @pl.when(j == 0)
    def _():
        st_ref[...] = jnp.zeros_like(st_ref)

    head_id = (lax.broadcasted_iota(I32, (HG_CHUNK, HG_W), 1) // HEAD_DIM).astype(F32).astype(BF16)
    head_masks = [head_id == float(h) for h in range(HG_HEADS)]
    same_head = (lax.broadcasted_iota(I32, (HG_W, HG_W), 0) // HEAD_DIM
                 == lax.broadcasted_iota(I32, (HG_W, HG_W), 1) // HEAD_DIM)
    streams = []
    for d, (v_ref, z_ref, q_ref, o_ref) in enumerate(
            ((vf_ref, zf_ref, qf_ref, of_ref), (vb_ref, zb_ref, qb_ref, ob_ref))):
        for s in range(v_ref.shape[0]):
            g = _hgrn_prepare(v_ref[s], z_ref[s], q_ref[s], lb_ref[d], tri_ref[d], d == 1, n_local)
            streams.append((d, s, o_ref, g))
    worst = jnp.abs(streams[0][3]["local"])
    for _, _, _, g in streams[1:]:
        worst = jnp.maximum(worst, jnp.abs(g["local"]))
    can_use_local = jnp.max(worst) <= HG_LOCAL_MAX_LOG

    def run(fast):
        for d, s, o_ref, g in streams:
            o_ref[s] = _hgrn_finish(g, head_masks, same_head, lvl_ref[d], st_ref, 2 * s + d,
                                    backward=(d == 1), fast=fast, n_local=n_local)

    pl.when(can_use_local)(functools.partial(run, True))
    pl.when(jnp.logical_not(can_use_local))(functools.partial(run, False))


def _hgrn(hg, hqg, lb_l, ctx_len, n_local):
    B, S, _ = hg.shape
    nc = S // HG_CHUNK
    nctx = ctx_len // HG_CHUNK
    tri_np, lvl_np = _hgrn_constants()
    tri = jnp.asarray(tri_np, BF16)
    lvl = jnp.asarray(lvl_np, I32)

    def back(j):
        return jnp.where(j < nctx, nctx - 1 - j, nc - 1 - (j - nctx))

    bs = HG_BATCH if B % HG_BATCH == 0 else 1
    blk = (bs, HG_CHUNK, HG_W)
    fwd = lambda col: (lambda b, j: (b, j, col))
    bwd = lambda col: (lambda b, j: (b, back(j), col))
    const3 = lambda b, j: (0, 0, 0)
    return pl.pallas_call(
        functools.partial(_hgrn_kernel, n_local=n_local),
        grid=(B // bs, nc),
        in_specs=[
            pl.BlockSpec(blk, fwd(0)), pl.BlockSpec(blk, fwd(1)), pl.BlockSpec(blk, fwd(0)),
            pl.BlockSpec(blk, bwd(0)), pl.BlockSpec(blk, bwd(2)), pl.BlockSpec(blk, bwd(0)),
            pl.BlockSpec((2, 1, HG_W), const3),
            pl.BlockSpec((2, HG_CHUNK, HG_CHUNK), const3),
            pl.BlockSpec((2, HG_CHUNK, HG_HEADS * HG_CHUNK), const3),
        ],
        out_specs=[pl.BlockSpec(blk, fwd(0)), pl.BlockSpec(blk, bwd(0))],
        out_shape=[jax.ShapeDtypeStruct((B, S, HG_W), F32)] * 2,
        scratch_shapes=[pltpu.VMEM((2 * bs, HG_W, HG_W), F32)],
        name="hgrn_scan",
        compiler_params=_cparams(("arbitrary", "arbitrary")),
    )(hg, hg, hqg, hg, hg, hqg, lb_l.reshape(2, 1, HG_W), tri, lvl)


def _mixout_kernel(xc_ref, x_ref, att_ref, of_ref, ob_ref, g_ref, cv_ref, cprev_ref, cnext_ref,
                   mod_ref, gain_ref, cw_ref, wo_ref, n2_ref, wr_ref, ones_ref,
                   x1_ref, h2_ref, aff_ref, *, tile0, ntile):
    b = pl.program_id(0)
    i = pl.program_id(1) + tile0
    row = jnp.where(i == 0, 4, b)
    R = ROW_TILE

    def modv(c):
        return mod_ref[pl.ds(row, 1), c * D_MODEL:(c + 1) * D_MODEL]

    o = of_ref[0] + ob_ref[0]
    sq = o * o
    sq_hi = sq.astype(BF16)
    sq_lo = (sq - sq_hi.astype(F32)).astype(BF16)
    ms = _dot(sq_hi, ones_ref[...]) + _dot(sq_lo, ones_ref[...])
    g = g_ref[0]
    hg = o * lax.rsqrt(ms * (1.0 / HEAD_DIM) + EPS) * gain_ref[...] * (g * jax.nn.sigmoid(g))
    cv = cv_ref[0]
    u = cv[:, CV_W:2 * CV_W] * cv[:, 2 * CV_W:3 * CV_W]
    up = cprev_ref[0]
    un = cnext_ref[0]
    u_prev_row = up[7:8, CV_W:2 * CV_W] * up[7:8, 2 * CV_W:3 * CV_W]
    u_next_row = un[0:1, CV_W:2 * CV_W] * un[0:1, 2 * CV_W:3 * CV_W]
    u_prev_row = jnp.where(i <= 1, 0.0, u_prev_row)
    u_next_row = jnp.where((i == 0) | (i == ntile - 1), 0.0, u_next_row)
    ridx = lax.broadcasted_iota(I32, (R, CV_W), 0)
    u_m1 = jnp.where(ridx == 0, u_prev_row, pltpu.roll(u, 1, 0))
    u_p1 = jnp.where(ridx == R - 1, u_next_row, pltpu.roll(u, R - 1, 0))
    cw = cw_ref[...]
    conv = cv[:, 0:CV_W] * (u_m1 * cw[0:1] + u * cw[1:2] + u_p1 * cw[2:3])
    mix = (_dot(att_ref[0], wo_ref[0:ATT_W])
           + _dot(hg.astype(BF16), wo_ref[ATT_W:ATT_W + HG_W])
           + _dot(conv.astype(BF16), wo_ref[ATT_W + HG_W:D_MODEL]))
    x1 = jnp.where(i == 0, xc_ref[0], x_ref[0]) + modv(2) * mix
    x1_ref[0] = x1
    h2 = _rmsnorm_mod(x1, n2_ref[...], modv(3), modv(4))
    h_hi = h2.astype(BF16)
    h2_ref[0] = h_hi
    h_lo = (h2 - h_hi.astype(F32)).astype(BF16)
    wr = wr_ref[...]
    w_hi = wr.astype(BF16)
    w_lo = (wr - w_hi.astype(F32)).astype(BF16)
    r1 = _dot_nt(jnp.concatenate([w_hi, w_lo], axis=0), h_hi)
    logits = r1[0:N_EXPERTS] + r1[N_EXPERTS:2 * N_EXPERTS] + _dot_nt(w_hi, h_lo)
    e = jnp.exp(logits - jnp.max(logits, axis=0, keepdims=True))
    aff_ref[0] = e / jnp.sum(e, axis=0, keepdims=True)


def _mix_out(src, att, o2, hqg, cv, mod_l, gain, conv_w, wo_bf16, n2g, wr_t, skip_ctx):
    xc, xl, off, S = _row_sources(src)
    B = xc.shape[0]
    ntile = S // ROW_TILE
    tile0 = 1 if skip_ctx else 0
    sub = ROW_TILE // 8
    nsub = S // 8
    rmap = lambda b, i: (b, i + tile0, 0)
    omap = lambda b, i: (b, i, 0)
    s_out = S - tile0 * ROW_TILE
    const2 = lambda b, i: (0, 0)
    ones = jnp.asarray(np.kron(np.eye(HG_HEADS), np.ones((HEAD_DIM, HEAD_DIM))), BF16)
    kern = functools.partial(_mixout_kernel, tile0=tile0, ntile=ntile)
    return pl.pallas_call(
        kern,
        grid=(B, ntile - tile0),
        in_specs=[
            pl.BlockSpec((1, ROW_TILE, D_MODEL), lambda b, i: (b, 0, 0)),
            pl.BlockSpec((1, ROW_TILE, D_MODEL),
                         lambda b, i: (b, jnp.maximum(i + tile0 - off, 0), 0)),
            pl.BlockSpec((1, ROW_TILE, ATT_W), omap),
            pl.BlockSpec((1, ROW_TILE, HG_W), rmap),
            pl.BlockSpec((1, ROW_TILE, HG_W), rmap),
            pl.BlockSpec((1, ROW_TILE, HG_W), lambda b, i: (b, i + tile0, 1)),
            pl.BlockSpec((1, ROW_TILE, 3 * CV_W), rmap),
            pl.BlockSpec((1, 8, 3 * CV_W),
                         lambda b, i: (b, jnp.maximum((i + tile0) * sub - 1, 0), 0)),
            pl.BlockSpec((1, 8, 3 * CV_W),
                         lambda b, i: (b, jnp.minimum((i + tile0 + 1) * sub, nsub - 1), 0)),
            pl.BlockSpec((8, N_MOD * D_MODEL), const2),
            pl.BlockSpec((1, HG_W), const2),
            pl.BlockSpec((3, CV_W), const2),
            pl.BlockSpec((D_MODEL, D_MODEL), const2),
            pl.BlockSpec((1, D_MODEL), const2),
            pl.BlockSpec((N_EXPERTS, D_MODEL), const2),
            pl.BlockSpec((HG_W, HG_W), const2),
        ],
        out_specs=[
            pl.BlockSpec((1, ROW_TILE, D_MODEL), omap),
            pl.BlockSpec((1, ROW_TILE, D_MODEL), omap),
            pl.BlockSpec((1, N_EXPERTS, ROW_TILE), lambda b, i: (b, 0, i)),
        ],
        out_shape=[
            jax.ShapeDtypeStruct((B, s_out, D_MODEL), F32),
            jax.ShapeDtypeStruct((B, s_out, D_MODEL), BF16),
            jax.ShapeDtypeStruct((B, N_EXPERTS, s_out), F32),
        ],
        name="mix_out",
        compiler_params=_cparams(("arbitrary", "arbitrary")),
    )(xc, xl, att, o2[0], o2[1], hqg, cv, cv, cv, mod_l, gain, conv_w, wo_bf16, n2g, wr_t, ones)


def _topk_kernel(aff_ref, tri_ref, pos_ref, cnt_ref, *, segments):
    tri = tri_ref[...]

    def excl_cumsum(mask, n):
        carry = jnp.zeros((N_EXPERTS, 1), F32)
        parts = []
        carries = []
        for c in range(n // 128):
            carries.append(carry)
            blk = jnp.where(mask[:, c * 128:(c + 1) * 128], 1.0, 0.0)
            parts.append(_dot(blk.astype(BF16), tri) + carry)
            carry = carry + jnp.sum(blk, axis=1, keepdims=True)
        return jnp.concatenate(parts, axis=1), carries + [carry]

    for (lo, n, k) in segments:
        a = aff_ref[0, :, lo:lo + n]

        def body(it, thr):
            cand = thr | jnp.left_shift(jnp.int32(1), 30 - it)
            cnt = jnp.sum(jnp.where(a >= pltpu.bitcast(cand, F32), 1.0, 0.0), axis=1, keepdims=True)
            return jnp.where(cnt >= k, cand, thr)

        thr = lax.fori_loop(0, 31, body, jnp.zeros((N_EXPERTS, 1), I32))
        above = a >= pltpu.bitcast(thr + 1, F32)
        tied = jnp.logical_and(a >= pltpu.bitcast(thr, F32), jnp.logical_not(above))
        n_above = jnp.sum(jnp.where(above, 1.0, 0.0), axis=1, keepdims=True)
        rank_tied, _ = excl_cumsum(tied, n)
        sel = above | (tied & (rank_tied < (k - n_above)))
        pos, counts = excl_cumsum(sel, n)
        pos_ref[0, :, lo:lo + n] = jnp.where(sel, pos.astype(I32), -1)
    step = ROW_TILE // 128
    lane = lax.broadcasted_iota(I32, (N_EXPERTS, 128), 1)
    cnt = jnp.zeros((N_EXPERTS, 128), F32)
    for c, col in enumerate(counts[0::step]):
        cnt = jnp.where(lane == c, col, cnt)
    cnt_ref[0] = cnt.astype(I32)


def _topk_positions(aff, segments):
    B, E, S = aff.shape
    assert segments[-1][1] // ROW_TILE + 1 <= 128
    tri = jnp.asarray(np.triu(np.ones((128, 128)), 1), BF16)
    kern = functools.partial(_topk_kernel, segments=segments)
    return pl.pallas_call(
        kern,
        grid=(B,),
        in_specs=[pl.BlockSpec((1, E, S), lambda b: (b, 0, 0)),
                  pl.BlockSpec((128, 128), lambda b: (0, 0))],
        out_specs=[pl.BlockSpec((1, E, S), lambda b: (b, 0, 0)),
                   pl.BlockSpec((1, E, 128), lambda b: (b, 0, 0))],
        out_shape=[jax.ShapeDtypeStruct((B, E, S), I32),
                   jax.ShapeDtypeStruct((B, E, 128), I32)],
        name="topk_positions",
        compiler_params=_cparams(("arbitrary",)),
    )(aff, tri)


def _div_pow2(x, d):
    assert d & (d - 1) == 0
    return lax.shift_right_logical(x, jnp.int32(d.bit_length() - 1))


def _gather_kernel(cnt_ref, pos_ref, aff_ref, h_ref, *refs, cap_l, cap_c):
    if cap_c:
        xl_ref, gi_ref, xc_ref, gc_ref = refs
    else:
        xl_ref, gi_ref = refs
    b = pl.program_id(0)
    e0 = pl.program_id(1) * GATHER_GROUP
    j = pl.program_id(2)
    G = GATHER_GROUP
    GR = PACK_GRANULE
    W = min(GATHER_WINDOW, cap_l)
    pos = pos_ref[0]
    aff = aff_ref[0]
    h = h_ref[0]

    @pl.when(j == 0)
    def _():
        xl_ref[...] = jnp.zeros_like(xl_ref)
        gi_ref[...] = jnp.zeros_like(gi_ref)

    first_lane = lax.broadcasted_iota(I32, (W, 2), 1) == 0

    def add_rows(g, start, onehot_f32, rows):
        a = pl.multiple_of(start, GR)
        xl_ref[g, 0, pl.ds(a, W), :] += rows.astype(BF16)
        gate = jnp.sum(onehot_f32 * aff[g:g + 1, :], axis=1, keepdims=True)
        tok_id = jnp.sum(onehot_f32 * tok, axis=1, keepdims=True)
        gi_ref[g, 0, pl.ds(a, W), :] += jnp.where(first_lane, gate, tok_id)

    def latent():
        c = j - 1 if cap_c else j
        slot_i32 = lax.broadcasted_iota(I32, (W, ROW_TILE), 0)
        starts = [jnp.minimum(_div_pow2(cnt_ref[b, e0 + g, c], GR) * GR, cap_l - W)
                  for g in range(G)]
        hits = [(pos[g:g + 1, :] - starts[g]) == slot_i32 for g in range(G)]
        onehots = [jnp.where(hit, 1.0, 0.0) for hit in hits]
        prod = _dot(jnp.concatenate(onehots, axis=0).astype(BF16), h)
        for g in range(G):
            add_rows(g, starts[g], onehots[g], prod[g * W:(g + 1) * W])
        over = [cnt_ref[b, e0 + g, c + 1] - (starts[g] + W) for g in range(G)]
        worst = over[0]
        for o in over[1:]:
            worst = jnp.maximum(worst, o)

        @pl.when(worst > 0)
        def _():
            for g in range(G):
                def extra_window(k, carry, g=g):
                    first_slot = starts[g] + (k + 1) * W
                    start = jnp.minimum(first_slot, cap_l - W)
                    pg = pos[g:g + 1, :]
                    hit = ((pg - start) == slot_i32) & (pg >= first_slot)
                    onehot = jnp.where(hit, 1.0, 0.0)
                    add_rows(g, start, onehot, _dot(onehot.astype(BF16), h))
                    return carry

                lax.fori_loop(0, _div_pow2(jnp.maximum(over[g], 0) + (W - 1), W), extra_window, 0)

    tok = None
    if cap_c:
        tok = (lax.broadcasted_iota(I32, (1, ROW_TILE), 1) + (j - 1) * ROW_TILE).astype(F32)

        @pl.when(j == 0)
        def _():
            slot_c = lax.broadcasted_iota(I32, (cap_c, ROW_TILE), 0)
            onehots = [jnp.where(pos[g:g + 1, :] == slot_c, 1.0, 0.0) for g in range(G)]
            prod = _dot(jnp.concatenate(onehots, axis=0).astype(BF16), h)
            for g in range(G):
                xc_ref[g, 0] = prod[g * cap_c:(g + 1) * cap_c].astype(BF16)
                gc_ref[g, 0] = jnp.sum(onehots[g] * aff[g:g + 1, :], axis=1, keepdims=True)

        pl.when(j > 0)(latent)
    else:
        tok = (lax.broadcasted_iota(I32, (1, ROW_TILE), 1) + j * ROW_TILE).astype(F32)
        latent()


def _gather_rows(cnt, posm, aff, h2, cap_l, cap_c):
    B, E, S = posm.shape
    G = GATHER_GROUP
    assert min(GATHER_WINDOW, cap_l) % PACK_GRANULE == 0 and cap_l % PACK_GRANULE == 0
    kern = functools.partial(_gather_kernel, cap_l=cap_l, cap_c=cap_c)
    omap = lambda b, g, j, cnt_ref: (g, b, 0, 0)
    out_specs = [pl.BlockSpec((G, 1, cap_l, D_MODEL), omap), pl.BlockSpec((G, 1, cap_l, 2), omap)]
    out_shape = [jax.ShapeDtypeStruct((E, B, cap_l, D_MODEL), BF16),
                 jax.ShapeDtypeStruct((E, B, cap_l, 2), F32)]
    if cap_c:
        out_specs += [pl.BlockSpec((G, 1, cap_c, D_MODEL), omap),
                      pl.BlockSpec((G, 1, cap_c, 1), omap)]
        out_shape += [jax.ShapeDtypeStruct((E, B, cap_c, D_MODEL), BF16),
                      jax.ShapeDtypeStruct((E, B, cap_c, 1), F32)]
    tile_map = lambda b, g, j, cnt_ref: (b, g, j)
    grid_spec = pltpu.PrefetchScalarGridSpec(
        num_scalar_prefetch=1,
        grid=(B, E // G, S // ROW_TILE),
        in_specs=[pl.BlockSpec((1, G, ROW_TILE), tile_map),
                  pl.BlockSpec((1, G, ROW_TILE), tile_map),
                  pl.BlockSpec((1, ROW_TILE, D_MODEL), lambda b, g, j, cnt_ref: (b, j, 0))],
        out_specs=out_specs,
    )
    return pl.pallas_call(
        kern,
        grid_spec=grid_spec,
        out_shape=out_shape,
        name="gather_rows",
        compiler_params=_cparams(("arbitrary", "arbitrary", "arbitrary")),
    )(cnt, posm, aff, h2)


def _expert_kernel(*refs, row_chunk, n_sets):
    x_refs = refs[:n_sets]
    g_refs = refs[n_sets:2 * n_sets]
    wg_ref, wu_ref, wd_ref = refs[2 * n_sets:2 * n_sets + 3]
    y_refs = refs[2 * n_sets + 3:3 * n_sets + 3]
    acc_refs = refs[3 * n_sets + 3:]
    f = pl.program_id(1)
    nf = pl.num_programs(1)
    wg = wg_ref[0, 0].astype(BF16)
    wu = wu_ref[0, 0].astype(BF16)
    wd = wd_ref[0, 0].astype(BF16)

    def hidden_tile(first):
        for x_ref, acc_ref in zip(x_refs, acc_refs):
            rows = x_ref.shape[1]
            step = min(row_chunk, rows)
            for r in range(rows // step):
                rs = slice(r * step, (r + 1) * step)
                x = x_ref[0, rs, :]
                g = _dot(x, wg)
                u = _dot(x, wu)
                y = _dot((g * jax.nn.sigmoid(g) * u).astype(BF16), wd)
                if first:
                    acc_ref[rs, :] = y
                else:
                    acc_ref[rs, :] += y

    pl.when(f == 0)(functools.partial(hidden_tile, True))
    pl.when(f > 0)(functools.partial(hidden_tile, False))

    @pl.when(f == nf - 1)
    def _():
        for g_ref, y_ref, acc_ref in zip(g_refs, y_refs, acc_refs):
            y_ref[0] = (acc_ref[...] * g_ref[0][:, 0:1]).astype(BF16)


def _expert_mlp(x_sets, g_sets, w_gate, w_up, w_down, layer, row_chunk):
    E = N_EXPERTS
    nf = D_EXPERT // FF_TILE
    n_sets = len(x_sets)
    kern = functools.partial(_expert_kernel, row_chunk=row_chunk, n_sets=n_sets)
    xspec = lambda r: pl.BlockSpec((1, r, D_MODEL), lambda e, f: (e, 0, 0))
    gspec = lambda gs: pl.BlockSpec((1,) + gs.shape[1:], lambda e, f: (e, 0, 0))
    return pl.pallas_call(
        kern,
        grid=(E, nf),
        in_specs=[xspec(xs.shape[1]) for xs in x_sets] + [gspec(gs) for gs in g_sets] + [
            pl.BlockSpec((1, 1, D_MODEL, FF_TILE), lambda e, f: (layer, e, 0, f)),
            pl.BlockSpec((1, 1, D_MODEL, FF_TILE), lambda e, f: (layer, e, 0, f)),
            pl.BlockSpec((1, 1, FF_TILE, D_MODEL), lambda e, f: (layer, e, f, 0))],
        out_specs=[xspec(xs.shape[1]) for xs in x_sets],
        out_shape=[jax.ShapeDtypeStruct(xs.shape, BF16) for xs in x_sets],
        scratch_shapes=[pltpu.VMEM(xs.shape[1:], F32) for xs in x_sets],
        name="expert_mlp",
        compiler_params=_cparams(("arbitrary", "arbitrary")),
    )(*x_sets, *g_sets, w_gate, w_up, w_down)


def _combine_kernel(cnt_ref, x_ref, pos_ref, *refs, cap_l, cap_c, final):
    if cap_c:
        yl_ref, il_ref, yc_ref, mod_ref, fg_ref, o_ref, acc_ref, pk_ref, tk_ref = refs
    else:
        yl_ref, il_ref, mod_ref, fg_ref, o_ref, acc_ref, pk_ref, tk_ref = refs
    b = pl.program_id(0)
    i = pl.program_id(1)
    row = jnp.where(i == 0, 4, b) if cap_c else b
    gate = mod_ref[pl.ds(row, 1), 5 * D_MODEL:6 * D_MODEL]
    pos = pos_ref[0]
    acc_ref[...] = jnp.zeros_like(acc_ref)
    GR = PACK_GRANULE
    W = min(GATHER_WINDOW, cap_l)

    def add_latent():
        t = i - 1 if cap_c else i
        starts = [jnp.minimum(_div_pow2(cnt_ref[b, e, t], GR) * GR, cap_l - W)
                  for e in range(N_EXPERTS)]
        for e in range(N_EXPERTS):
            a = pl.multiple_of(starts[e], GR)
            pk_ref[e * W:(e + 1) * W, :] = yl_ref[e, 0, pl.ds(a, W), :]
            tk_ref[e * W:(e + 1) * W, :] = il_ref[e, 0, pl.ds(a, W), 1:2].astype(I32)

        def product(tok_ids, rows):
            lane = lax.broadcasted_iota(I32, (tok_ids.shape[0], ROW_TILE), 1) + t * ROW_TILE
            onehot = jnp.where(tok_ids == lane, 1.0, 0.0)
            return _dot(onehot.T.astype(BF16), rows)

        acc_ref[...] += product(tk_ref[...], pk_ref[...])
        over = [cnt_ref[b, e, t + 1] - (starts[e] + W) for e in range(N_EXPERTS)]
        worst = over[0]
        for o in over[1:]:
            worst = jnp.maximum(worst, o)

        @pl.when(worst > 0)
        def _():
            wx = min(2 * W, cap_l)
            slot = lax.broadcasted_iota(I32, (wx, 1), 0)
            for e in range(N_EXPERTS):
                def extra_window(k, carry, e=e):
                    first_slot = starts[e] + W + k * wx
                    a = pl.multiple_of(jnp.minimum(first_slot, cap_l - wx), GR)
                    ids = jnp.where(slot + a >= first_slot,
                                    il_ref[e, 0, pl.ds(a, wx), 1:2].astype(I32), -1)
                    acc_ref[...] += product(ids, yl_ref[e, 0, pl.ds(a, wx), :])
                    return carry

                lax.fori_loop(0, _div_pow2(jnp.maximum(over[e], 0) + (wx - 1), wx),
                              extra_window, 0)

    def add_context():
        ncol = yc_ref.shape[1]
        lane = lax.broadcasted_iota(I32, (ROW_TILE, ncol), 1)
        for e in range(N_EXPERTS):
            pe = pos[:, e:e + 1]
            tgt = jnp.where(pe >= 0, pe + b * cap_c, -1)
            acc_ref[...] += _dot(jnp.where(tgt == lane, 1.0, 0.0).astype(BF16), yc_ref[e])

    if cap_c:
        pl.when(i == 0)(add_context)
        pl.when(i > 0)(add_latent)
    else:
        add_latent()
    x2 = x_ref[0] + gate * acc_ref[...]
    if final:
        var = jnp.mean(x2 * x2, axis=-1, keepdims=True)
        x2 = x2 * lax.rsqrt(var + EPS) * fg_ref[...]
    o_ref[0] = x2


def _combine(cnt, x1, pos_t, yl, il, yc, mod_l, final_g, cap_l, cap_c, final):
    B, S, _ = x1.shape
    E = N_EXPERTS
    rmap = lambda b, i, cnt_ref: (b, i, 0)
    const2 = lambda b, i, cnt_ref: (0, 0)
    kern = functools.partial(_combine_kernel, cap_l=cap_l, cap_c=cap_c, final=final)
    per_sample = lambda b, i, cnt_ref: (0, b, 0, 0)
    y_specs = [pl.BlockSpec((E, 1, cap_l, D_MODEL), per_sample, pipeline_mode=pl.Buffered(1)),
               pl.BlockSpec((E, 1, cap_l, 2), per_sample, pipeline_mode=pl.Buffered(1))]
    y_args = [yl, il]
    pack_rows = E * min(GATHER_WINDOW, cap_l)
    if cap_c:
        y_specs.append(pl.BlockSpec(yc.shape, lambda b, i, cnt_ref: (0, 0, 0),
                                    pipeline_mode=pl.Buffered(1)))
        y_args.append(yc)
    grid_spec = pltpu.PrefetchScalarGridSpec(
        num_scalar_prefetch=1,
        grid=(B, S // ROW_TILE),
        in_specs=[pl.BlockSpec((1, ROW_TILE, D_MODEL), rmap),
                  pl.BlockSpec((1, ROW_TILE, E), rmap)] + y_specs + [
                  pl.BlockSpec((8, N_MOD * D_MODEL), const2),
                  pl.BlockSpec((1, D_MODEL), const2)],
        out_specs=pl.BlockSpec((1, ROW_TILE, D_MODEL), rmap),
        scratch_shapes=[pltpu.VMEM((ROW_TILE, D_MODEL), F32),
                        pltpu.VMEM((pack_rows, D_MODEL), BF16),
                        pltpu.VMEM((pack_rows, 1), I32)],
    )
    return pl.pallas_call(
        kern,
        grid_spec=grid_spec,
        out_shape=jax.ShapeDtypeStruct((B, S, D_MODEL), F32),
        name="combine",
        compiler_params=_cparams(("arbitrary", "arbitrary")),
    )(cnt, x1, pos_t, *y_args, mod_l, final_g)


def _rope_tables(n_lat, ctx_len):
    t = np.arange(n_lat)
    pos = np.stack([t // GRID_W, t % GRID_W], axis=-1).astype(np.float32)
    inv = (ROPE_BASE ** (-np.arange(ROPE_FREQS, dtype=np.float32) / ROPE_FREQS)).astype(np.float32)
    ang = pos[:, :, None] * inv
    cos = np.cos(ang).astype(np.float32)
    sin = np.sin(ang).astype(np.float32)
    cos64 = np.concatenate([cos[:, 0], cos[:, 0], cos[:, 1], cos[:, 1]], axis=-1)
    sin64 = np.concatenate([-sin[:, 0], sin[:, 0], -sin[:, 1], sin[:, 1]], axis=-1)
    cos_t = np.concatenate([np.ones((ctx_len, 64), np.float32), cos64], axis=0)
    sin_t = np.concatenate([np.zeros((ctx_len, 64), np.float32), sin64], axis=0)
    return (jnp.asarray(np.tile(cos_t, (1, 2))), jnp.asarray(np.tile(sin_t, (1, 2))))


def kernel(x, c, ctx, c_ctx, ada_w, ada_b, norm1_g, norm2_g, w_in, attn_sink, hgrn_lb,
           hgrn_norm_g, conv_w, w_o, router_w, exp_w_gate, exp_w_up, exp_w_down, final_norm_g):
    B, T, D = x.shape
    L = ctx.shape[1]
    depth = ada_w.shape[0]
    assert D == D_MODEL and L == ROW_TILE and T % ROW_TILE == 0 and B <= 4
    S = L + T
    cap_l = EC_CAPACITY * T // N_EXPERTS
    cap_c = EC_CAPACITY * L // N_EXPERTS
    assert cap_l % 16 == 0 and cap_c % 16 == 0 and (B * cap_c) % 16 == 0

    cos_t, sin_t = _rope_tables(T, L)
    gamma = jax.nn.softmax(hgrn_lb.astype(F32), axis=0)
    lb_all = jnp.cumsum(gamma, axis=0) - gamma[0]
    cvec = jnp.concatenate([c, jnp.zeros((4 - B, D), F32), c_ctx[None],
                            jnp.zeros((3, D), F32)], axis=0)
    mod = _modulation(cvec, ada_w, ada_b)
    xs = (ctx, x)

    def interleave_heads(w, axis):
        shp = w.shape
        w = w.reshape(shp[:axis] + (N_KV, GQA_GROUP, HEAD_DIM) + shp[axis + 1:])
        return jnp.swapaxes(w, axis, axis + 1).reshape(shp)

    for l in range(depth):
        last = l == depth - 1
        w_in_l = w_in[l].astype(BF16)
        w_in_l = jnp.concatenate([w_in_l[:, :1024], interleave_heads(w_in_l[:, 1024:1536], 1),
                                  w_in_l[:, 1536:]], axis=1)
        w_o_l = w_o[l].astype(BF16)
        w_o_l = jnp.concatenate([interleave_heads(w_o_l[:ATT_W], 0), w_o_l[ATT_W:]], axis=0)
        q, kv, hg, hqg, cv = _in_projection(xs, mod[l], norm1_g[l][None], w_in_l, cos_t, sin_t)
        att = _attention(q, kv, attn_sink[l], L, skip_ctx=last)
        o2 = _hgrn(hg, hqg, lb_all[l], L, HG_LOCAL_FIRST if l == 0 else HG_LOCAL)
        x1, h2, aff = _mix_out(xs, att, o2, hqg, cv, mod[l], hgrn_norm_g[l][None], conv_w[l],
                               w_o_l, norm2_g[l][None], router_w[l].T, skip_ctx=last)
        lm, cc = (0, 0) if last else (L, cap_c)
        segments = ((lm, T, cap_l),) if last else ((0, L, cap_c), (L, T, cap_l))
        posm, cnt = _topk_positions(aff, segments)
        gathered = _gather_rows(cnt, posm, aff, h2, cap_l, cc)
        il = gathered[1]
        x_sets = [a.reshape(N_EXPERTS, -1, D) for a in gathered[0::2]]
        g_sets = [a.reshape(N_EXPERTS, -1, a.shape[-1]) for a in gathered[1::2]]
        ys = _expert_mlp(x_sets, g_sets, exp_w_gate, exp_w_up, exp_w_down, l, EXPERT_ROWS)
        yl = ys[0].reshape(N_EXPERTS, B, cap_l, D)
        xs = _combine(cnt, x1, jnp.swapaxes(posm, 1, 2), yl, il, None if last else ys[1], mod[l],
                      final_norm_g[None], cap_l, cc, final=last)
    return xs
```

```python
import functools

import numpy as np
import jax
import jax.numpy as jnp
from jax import lax
from jax.experimental import pallas as pl
from jax.experimental.pallas import tpu as pltpu

F32 = jnp.float32
BF16 = jnp.bfloat16
I32 = jnp.int32
HIGHEST = lax.Precision.HIGHEST

D_MODEL = 1024
GRID_W = 64
EPS = 1e-6
LB_FLOOR = 1e-30
N_MOD = 6
ATT_W = 512
HG_W = 256
CV_W = 256
HEAD_DIM = 64
N_Q = 8
N_KV = 2
GQA_GROUP = 4
KV_W = 128
ROPE_BASE = 10000.0
ROPE_FREQS = 16
HG_HEADS = 4
N_EXPERTS = 16
EC_CAPACITY = 2
D_EXPERT = 2048
IN_COLS = 2816

ROW_TILE = 256
ATT_BLOCK = 128
HG_CHUNK = 128
HG_LEVELS = 7
HG_BATCH = 4
HG_LOCAL = 64
HG_LOCAL_FIRST = 32
HG_LOCAL_MAX_LOG = 115.0
GATHER_GROUP = 16
GATHER_WINDOW = 64
PACK_GRANULE = 16
FF_TILE = 512
EXPERT_ROWS = 2048
MOD_TILE = 1536
VMEM_LIMIT = 56 * 1024 * 1024

NEG_BIG = -1e30
LOG2E = 1.4426950408889634


def _cparams(sem):
    return pltpu.CompilerParams(dimension_semantics=sem, vmem_limit_bytes=VMEM_LIMIT)


def _dot(a, b):
    return jnp.dot(a, b, preferred_element_type=F32)


def _dot_nt(a, b):
    return lax.dot_general(a, b, (((1,), (1,)), ((), ())), preferred_element_type=F32)


def _mod_kernel(a_ref, w_ref, b_ref, o_ref):
    a = a_ref[...]
    a = a * jax.nn.sigmoid(a)
    o_ref[0] = jnp.dot(a, w_ref[0], precision=HIGHEST, preferred_element_type=F32) + b_ref[0]


def _modulation(cvec, ada_w, ada_b):
    depth = ada_w.shape[0]
    ncol = ada_w.shape[2]
    return pl.pallas_call(
        _mod_kernel,
        grid=(depth, ncol // MOD_TILE),
        in_specs=[
            pl.BlockSpec((8, D_MODEL), lambda l, j: (0, 0)),
            pl.BlockSpec((1, D_MODEL, MOD_TILE), lambda l, j: (l, 0, j)),
            pl.BlockSpec((1, 1, MOD_TILE), lambda l, j: (l, 0, j)),
        ],
        out_specs=pl.BlockSpec((1, 8, MOD_TILE), lambda l, j: (l, 0, j)),
        out_shape=jax.ShapeDtypeStruct((depth, 8, ncol), F32),
        name="modulation",
        compiler_params=_cparams(("arbitrary", "arbitrary")),
    )(cvec, ada_w, ada_b.reshape(depth, 1, ncol))


def _swap_halves(x):
    n = x.shape[-1]
    lane = lax.broadcasted_iota(I32, x.shape, x.ndim - 1)
    up = pltpu.roll(x, n - ROPE_FREQS, x.ndim - 1)
    dn = pltpu.roll(x, ROPE_FREQS, x.ndim - 1)
    return jnp.where((lane % (2 * ROPE_FREQS)) < ROPE_FREQS, up, dn)


def _rmsnorm_mod(x, g, shift, scale):
    var = jnp.mean(x * x, axis=-1, keepdims=True)
    y = x * lax.rsqrt(var + EPS) * g
    return y * (1.0 + scale) + shift


def _row_sources(src):
    if isinstance(src, tuple):
        ctx, lat = src
        return ctx, lat, 1, ctx.shape[1] + lat.shape[1]
    return src, src, 0, src.shape[1]


def _inproj_kernel(xc_ref, x_ref, mod_ref, g_ref, w_ref, cos_ref, sin_ref,
                   q_ref, kv_ref, hg_ref, hqg_ref, cv_ref):
    b = pl.program_id(0)
    i = pl.program_id(1)
    row = jnp.where(i == 0, 4, b)
    shift = mod_ref[pl.ds(row, 1), 0:D_MODEL]
    scale = mod_ref[pl.ds(row, 1), D_MODEL:2 * D_MODEL]
    xin = jnp.where(i == 0, xc_ref[0], x_ref[0])
    h = _rmsnorm_mod(xin, g_ref[...], shift, scale)
    p = _dot(h.astype(BF16), w_ref[...])
    cos2 = cos_ref[...]
    sin2 = sin_ref[...]
    k = p[:, 0:KV_W]
    k = k * cos2 + _swap_halves(k) * sin2
    kv_ref[0, :, 0:KV_W] = k.astype(BF16)
    kv_ref[0, :, KV_W:2 * KV_W] = p[:, KV_W:2 * KV_W].astype(BF16)
    hg_ref[0] = p[:, 256:1024]
    q = p[:, 1024:1536]
    cos8 = jnp.concatenate([cos2] * 4, axis=1)
    sin8 = jnp.concatenate([sin2] * 4, axis=1)
    q = (q * cos8 + _swap_halves(q) * sin8) * (HEAD_DIM ** -0.5 * LOG2E)
    q_ref[0] = q.astype(BF16)
    hqg_ref[0] = p[:, 1536:2048]
    cv_ref[0] = p[:, 2048:2816]


def _in_projection(src, mod_l, g, w_bf16, cos_t, sin_t):
    xc, xl, off, S = _row_sources(src)
    B = xc.shape[0]
    nt = S // ROW_TILE
    row_map = lambda b, i: (b, i, 0)
    const2 = lambda b, i: (0, 0)
    return pl.pallas_call(
        _inproj_kernel,
        grid=(B, nt),
        in_specs=[
            pl.BlockSpec((1, ROW_TILE, D_MODEL), lambda b, i: (b, 0, 0)),
            pl.BlockSpec((1, ROW_TILE, D_MODEL), lambda b, i: (b, jnp.maximum(i - off, 0), 0)),
            pl.BlockSpec((8, N_MOD * D_MODEL), const2),
            pl.BlockSpec((1, D_MODEL), const2),
            pl.BlockSpec((D_MODEL, IN_COLS), const2),
            pl.BlockSpec((ROW_TILE, 2 * HEAD_DIM), lambda b, i: (i, 0)),
            pl.BlockSpec((ROW_TILE, 2 * HEAD_DIM), lambda b, i: (i, 0)),
        ],
        out_specs=[
            pl.BlockSpec((1, ROW_TILE, ATT_W), row_map),
            pl.BlockSpec((1, ROW_TILE, 2 * KV_W), row_map),
            pl.BlockSpec((1, ROW_TILE, 3 * HG_W), row_map),
            pl.BlockSpec((1, ROW_TILE, 2 * HG_W), row_map),
            pl.BlockSpec((1, ROW_TILE, 3 * CV_W), row_map),
        ],
        out_shape=[
            jax.ShapeDtypeStruct((B, S, ATT_W), BF16),
            jax.ShapeDtypeStruct((B, S, 2 * KV_W), BF16),
            jax.ShapeDtypeStruct((B, S, 3 * HG_W), F32),
            jax.ShapeDtypeStruct((B, S, 2 * HG_W), F32),
            jax.ShapeDtypeStruct((B, S, 3 * CV_W), F32),
        ],
        name="in_projection",
        compiler_params=_cparams(("arbitrary", "arbitrary")),
    )(xc, xl, mod_l, g, w_bf16, cos_t, sin_t)


def _attn_kernel(sink_ref, q_ref, kp_ref, kc_ref, kn_ref, kx_ref, o_ref, *, blk0, nblk, nctx):
    n = pl.program_id(1) + blk0
    is_lat = n >= nctx
    has_prev = n > nctx
    has_next = n < nblk - 1
    W = ATT_BLOCK
    q = q_ref[0]
    qrows = jnp.concatenate([q[:, g * W:(g + 1) * W] for g in range(GQA_GROUP)], axis=0)
    kv_all = jnp.concatenate([kp_ref[0], kc_ref[0], kn_ref[0], kx_ref[0]], axis=0)
    nkeys = kv_all.shape[0]
    k_all = kv_all[:, 0:KV_W]
    v_ext = jnp.concatenate([kv_all[:, KV_W:2 * KV_W], jnp.ones((nkeys, KV_W), BF16)], axis=1)
    rows = GQA_GROUP * W
    ri = lax.broadcasted_iota(I32, (rows, W), 0) % W
    cj = lax.broadcasted_iota(I32, (rows, W), 1)
    m_prev = (cj >= ri) & has_prev
    m_cur = jnp.broadcast_to(is_lat, (rows, W))
    m_next = (cj <= ri) & (has_next & is_lat)
    grp = lax.broadcasted_iota(I32, (rows, 1), 0) // W
    low = cj < HEAD_DIM
    outs = []
    for h in range(N_KV):
        qh = jnp.where(low if h == 0 else jnp.logical_not(low), qrows, jnp.zeros_like(qrows))
        sink = jnp.zeros((rows, 1), F32)
        for g in range(GQA_GROUP):
            sink = jnp.where(grp == g, sink_ref[h * GQA_GROUP + g] * LOG2E, sink)
        s = _dot_nt(qh, k_all)
        segs = [jnp.where(m_prev, s[:, 0:W], NEG_BIG),
                jnp.where(m_cur, s[:, W:2 * W], NEG_BIG),
                jnp.where(m_next, s[:, 2 * W:3 * W], NEG_BIG)]
        segs += [s[:, c:c + W] for c in range(3 * W, nkeys, W)]
        mx = segs[0]
        for sg in segs[1:]:
            mx = jnp.maximum(mx, sg)
        m = jnp.maximum(jnp.max(mx, axis=1, keepdims=True), sink)
        p = jnp.concatenate([jnp.exp2(sg - m).astype(BF16) for sg in segs], axis=1)
        oe = _dot(p, v_ext)
        den = oe[:, KV_W:2 * KV_W] + jnp.exp2(sink - m)
        outs.append(oe[:, 0:KV_W] / den)
    o = jnp.where(low, outs[0], outs[1])
    for g in range(GQA_GROUP):
        o_ref[0, :, g * W:(g + 1) * W] = o[g * W:(g + 1) * W].astype(BF16)


def _attention(q, kv, sink, ctx_len, skip_ctx):
    B, S, _ = q.shape
    nblk = S // ATT_BLOCK
    nctx = ctx_len // ATT_BLOCK
    blk0 = nctx if skip_ctx else 0
    blk = lambda f: (lambda b, n: (b, f(n + blk0), 0))
    kern = functools.partial(_attn_kernel, blk0=blk0, nblk=nblk, nctx=nctx)
    return pl.pallas_call(
        kern,
        grid=(B, nblk - blk0),
        in_specs=[
            pl.BlockSpec(memory_space=pltpu.SMEM),
            pl.BlockSpec((1, ATT_BLOCK, ATT_W), blk(lambda n: n)),
            pl.BlockSpec((1, ATT_BLOCK, 2 * KV_W), blk(lambda n: jnp.maximum(n - 1, 0))),
            pl.BlockSpec((1, ATT_BLOCK, 2 * KV_W), blk(lambda n: n)),
            pl.BlockSpec((1, ATT_BLOCK, 2 * KV_W), blk(lambda n: jnp.minimum(n + 1, nblk - 1))),
            pl.BlockSpec((1, ctx_len, 2 * KV_W), lambda b, n: (b, 0, 0)),
        ],
        out_specs=pl.BlockSpec((1, ATT_BLOCK, ATT_W), lambda b, n: (b, n, 0)),
        out_shape=jax.ShapeDtypeStruct((B, S - blk0 * ATT_BLOCK, ATT_W), BF16),
        name="attention",
        compiler_params=_cparams(("arbitrary", "arbitrary")),
    )(sink, q, kv, kv, kv, kv)


def _hgrn_constants():
    C = HG_CHUNK
    t = np.arange(C)[:, None]
    r = np.arange(C)[None, :]
    tri = np.stack([r <= t, r >= t]).astype(np.float32)
    x = t ^ r
    lvl = np.where(x > 0, np.floor(np.log2(np.maximum(x, 1))).astype(np.int32), HG_LEVELS)
    lvl_f = np.where(t >= r, lvl, -1).astype(np.int32)
    lvl_b = np.where(t <= r, lvl, -1).astype(np.int32)
    lvl2 = np.stack([np.tile(lvl_f, (1, HG_HEADS)), np.tile(lvl_b, (1, HG_HEADS))])
    return tri, lvl2


def _span_row(x, span, row):
    C = x.shape[0]
    if span >= 8:
        x3 = x.reshape(C // span, span, x.shape[1])
        return jnp.broadcast_to(x3[:, row:row + 1, :], x3.shape).reshape(x.shape)
    pos = lax.broadcasted_iota(I32, x.shape, 0) % span
    out = x
    for p in range(span):
        if p != row:
            out = jnp.where(pos == p, pltpu.roll(x, (p - row) % C, 0), out)
    return out


def _hgrn_prepare(v, z, qr, lb, tri, backward, n_local):
    C = HG_CHUNK
    logf = jnp.log(jnp.maximum(lb, LB_FLOOR) + (1.0 - lb) * jax.nn.sigmoid(z)) * LOG2E
    k = (1.0 - lb) * jax.nn.sigmoid(-z)
    q = qr * jax.nn.sigmoid(qr)
    hi = logf.astype(BF16)
    lo = (logf - hi.astype(F32)).astype(BF16)
    cs = _dot(tri, jnp.concatenate([hi, lo], axis=1))
    lam = cs[:, 0:HG_W] + cs[:, HG_W:2 * HG_W]
    tot = lam[0:1] if backward else lam[C - 1:C]
    local = lam - _span_row(lam, n_local, n_local // 2)
    return dict(v=v, q=q, k=k, lam=lam, tot=tot, local=local)


def _hgrn_finish(g, head_masks, same_head, lvl, st_ref, sidx, backward, fast, n_local):
    C = HG_CHUNK
    q, k, v, lam, tot = g["q"], g["k"], g["v"], g["lam"], g["tot"]
    zero = jnp.zeros((C, HG_W), BF16)

    def per_head_rows(x):
        return jnp.concatenate([jnp.where(hm, x, zero) for hm in head_masks], axis=0)

    qb = q.astype(BF16)
    kb = k.astype(BF16)
    if fast:
        local_levels = n_local.bit_length() - 1
        a = jnp.where((lvl == HG_LEVELS) | ((lvl >= 0) & (lvl < local_levels)),
                      _dot_nt((q * jnp.exp2(g["local"])).astype(BF16),
                              per_head_rows((k * jnp.exp2(-g["local"])).astype(BF16))), 0.0)
        levels = range(local_levels, HG_LEVELS)
    else:
        a = jnp.where(lvl == HG_LEVELS, _dot_nt(qb, per_head_rows(kb)), 0.0)
        levels = range(HG_LEVELS)
    for l in levels:
        m = 1 << l
        ref = _span_row(lam, 2 * m, m if backward else m - 1)
        fac = jnp.exp2(-jnp.abs(lam - ref)).astype(BF16)
        a = jnp.where(lvl == l, _dot_nt(qb * fac, per_head_rows(kb * fac)), a)
    st = st_ref[sidx]
    q_in = (q * jnp.exp2(lam)).astype(BF16)
    o = _dot(a.astype(BF16), per_head_rows(v.astype(BF16))) + _dot_nt(q_in, st.astype(BF16))
    k_out = (k * jnp.exp2(tot - lam)).astype(BF16)
    upd = _dot(v.T.astype(BF16), k_out)
    st_ref[sidx] = st * jnp.exp2(tot) + jnp.where(same_head, upd, 0.0)
    return o


def _hgrn_kernel(vf_ref, zf_ref, qf_ref, vb_ref, zb_ref, qb_ref, lb_ref, tri_ref, lvl_ref,
                 of_ref, ob_ref, st_ref, *, n_local):
    j = pl.program_id(1)

    @pl.when(j == 0)
    def _():
        st_ref[...] = jnp.zeros_like(st_ref)

    head_id = (lax.broadcasted_iota(I32, (HG_CHUNK, HG_W), 1) // HEAD_DIM).astype(F32).astype(BF16)
    head_masks = [head_id == float(h) for h in range(HG_HEADS)]
    same_head = (lax.broadcasted_iota(I32, (HG_W, HG_W), 0) // HEAD_DIM
                 == lax.broadcasted_iota(I32, (HG_W, HG_W), 1) // HEAD_DIM)
    streams = []
    for d, (v_ref, z_ref, q_ref, o_ref) in enumerate(
            ((vf_ref, zf_ref, qf_ref, of_ref), (vb_ref, zb_ref, qb_ref, ob_ref))):
        for s in range(v_ref.shape[0]):
            g = _hgrn_prepare(v_ref[s], z_ref[s], q_ref[s], lb_ref[d], tri_ref[d], d == 1, n_local)
            streams.append((d, s, o_ref, g))
    worst = jnp.abs(streams[0][3]["local"])
    for _, _, _, g in streams[1:]:
        worst = jnp.maximum(worst, jnp.abs(g["local"]))
    can_use_local = jnp.max(worst) <= HG_LOCAL_MAX_LOG

    def run(fast):
        for d, s, o_ref, g in streams:
            o_ref[s] = _hgrn_finish(g, head_masks, same_head, lvl_ref[d], st_ref, 2 * s + d,
                                    backward=(d == 1), fast=fast, n_local=n_local)

    pl.when(can_use_local)(functools.partial(run, True))
    pl.when(jnp.logical_not(can_use_local))(functools.partial(run, False))


def _hgrn(hg, hqg, lb_l, ctx_len, n_local):
    B, S, _ = hg.shape
    nc = S // HG_CHUNK
    nctx = ctx_len // HG_CHUNK
    tri_np, lvl_np = _hgrn_constants()
    tri = jnp.asarray(tri_np, BF16)
    lvl = jnp.asarray(lvl_np, I32)

    def back(j):
        return jnp.where(j < nctx, nctx - 1 - j, nc - 1 - (j - nctx))

    bs = HG_BATCH if B % HG_BATCH == 0 else 1
    blk = (bs, HG_CHUNK, HG_W)
    fwd = lambda col: (lambda b, j: (b, j, col))
    bwd = lambda col: (lambda b, j: (b, back(j), col))
    const3 = lambda b, j: (0, 0, 0)
    return pl.pallas_call(
        functools.partial(_hgrn_kernel, n_local=n_local),
        grid=(B // bs, nc),
        in_specs=[
            pl.BlockSpec(blk, fwd(0)), pl.BlockSpec(blk, fwd(1)), pl.BlockSpec(blk, fwd(0)),
            pl.BlockSpec(blk, bwd(0)), pl.BlockSpec(blk, bwd(2)), pl.BlockSpec(blk, bwd(0)),
            pl.BlockSpec((2, 1, HG_W), const3),
            pl.BlockSpec((2, HG_CHUNK, HG_CHUNK), const3),
            pl.BlockSpec((2, HG_CHUNK, HG_HEADS * HG_CHUNK), const3),
        ],
        out_specs=[pl.BlockSpec(blk, fwd(0)), pl.BlockSpec(blk, bwd(0))],
        out_shape=[jax.ShapeDtypeStruct((B, S, HG_W), F32)] * 2,
        scratch_shapes=[pltpu.VMEM((2 * bs, HG_W, HG_W), F32)],
        name="hgrn_scan",
        compiler_params=_cparams(("arbitrary", "arbitrary")),
    )(hg, hg, hqg, hg, hg, hqg, lb_l.reshape(2, 1, HG_W), tri, lvl)


def _mixout_kernel(xc_ref, x_ref, att_ref, of_ref, ob_ref, g_ref, cv_ref, cprev_ref, cnext_ref,
                   mod_ref, gain_ref, cw_ref, wo_ref, n2_ref, wr_ref, ones_ref,
                   x1_ref, h2_ref, aff_ref, *, tile0, ntile):
    b = pl.program_id(0)
    i = pl.program_id(1) + tile0
    row = jnp.where(i == 0, 4, b)
    R = ROW_TILE

    def modv(c):
        return mod_ref[pl.ds(row, 1), c * D_MODEL:(c + 1) * D_MODEL]

    o = of_ref[0] + ob_ref[0]
    sq = o * o
    sq_hi = sq.astype(BF16)
    sq_lo = (sq - sq_hi.astype(F32)).astype(BF16)
    ms = _dot(sq_hi, ones_ref[...]) + _dot(sq_lo, ones_ref[...])
    g = g_ref[0]
    hg = o * lax.rsqrt(ms * (1.0 / HEAD_DIM) + EPS) * gain_ref[...] * (g * jax.nn.sigmoid(g))
    cv = cv_ref[0]
    u = cv[:, CV_W:2 * CV_W] * cv[:, 2 * CV_W:3 * CV_W]
    up = cprev_ref[0]
    un = cnext_ref[0]
    u_prev_row = up[7:8, CV_W:2 * CV_W] * up[7:8, 2 * CV_W:3 * CV_W]
    u_next_row = un[0:1, CV_W:2 * CV_W] * un[0:1, 2 * CV_W:3 * CV_W]
    u_prev_row = jnp.where(i <= 1, 0.0, u_prev_row)
    u_next_row = jnp.where((i == 0) | (i == ntile - 1), 0.0, u_next_row)
    ridx = lax.broadcasted_iota(I32, (R, CV_W), 0)
    u_m1 = jnp.where(ridx == 0, u_prev_row, pltpu.roll(u, 1, 0))
    u_p1 = jnp.where(ridx == R - 1, u_next_row, pltpu.roll(u, R - 1, 0))
    cw = cw_ref[...]
    conv = cv[:, 0:CV_W] * (u_m1 * cw[0:1] + u * cw[1:2] + u_p1 * cw[2:3])
    mix = (_dot(att_ref[0], wo_ref[0:ATT_W])
           + _dot(hg.astype(BF16), wo_ref[ATT_W:ATT_W + HG_W])
           + _dot(conv.astype(BF16), wo_ref[ATT_W + HG_W:D_MODEL]))
    x1 = jnp.where(i == 0, xc_ref[0], x_ref[0]) + modv(2) * mix
    x1_ref[0] = x1
    h2 = _rmsnorm_mod(x1, n2_ref[...], modv(3), modv(4))
    h_hi = h2.astype(BF16)
    h2_ref[0] = h_hi
    h_lo = (h2 - h_hi.astype(F32)).astype(BF16)
    wr = wr_ref[...]
    w_hi = wr.astype(BF16)
    w_lo = (wr - w_hi.astype(F32)).astype(BF16)
    r1 = _dot_nt(jnp.concatenate([w_hi, w_lo], axis=0), h_hi)
    logits = r1[0:N_EXPERTS] + r1[N_EXPERTS:2 * N_EXPERTS] + _dot_nt(w_hi, h_lo)
    e = jnp.exp(logits - jnp.max(logits, axis=0, keepdims=True))
    aff_ref[0] = e / jnp.sum(e, axis=0, keepdims=True)


def _mix_out(src, att, o2, hqg, cv, mod_l, gain, conv_w, wo_bf16, n2g, wr_t, skip_ctx):
    xc, xl, off, S = _row_sources(src)
    B = xc.shape[0]
    ntile = S // ROW_TILE
    tile0 = 1 if skip_ctx else 0
    sub = ROW_TILE // 8
    nsub = S // 8
    rmap = lambda b, i: (b, i + tile0, 0)
    omap = lambda b, i: (b, i, 0)
    s_out = S - tile0 * ROW_TILE
    const2 = lambda b, i: (0, 0)
    ones = jnp.asarray(np.kron(np.eye(HG_HEADS), np.ones((HEAD_DIM, HEAD_DIM))), BF16)
    kern = functools.partial(_mixout_kernel, tile0=tile0, ntile=ntile)
    return pl.pallas_call(
        kern,
        grid=(B, ntile - tile0),
        in_specs=[
            pl.BlockSpec((1, ROW_TILE, D_MODEL), lambda b, i: (b, 0, 0)),
            pl.BlockSpec((1, ROW_TILE, D_MODEL),
                         lambda b, i: (b, jnp.maximum(i + tile0 - off, 0), 0)),
            pl.BlockSpec((1, ROW_TILE, ATT_W), omap),
            pl.BlockSpec((1, ROW_TILE, HG_W), rmap),
            pl.BlockSpec((1, ROW_TILE, HG_W), rmap),
            pl.BlockSpec((1, ROW_TILE, HG_W), lambda b, i: (b, i + tile0, 1)),
            pl.BlockSpec((1, ROW_TILE, 3 * CV_W), rmap),
            pl.BlockSpec((1, 8, 3 * CV_W),
                         lambda b, i: (b, jnp.maximum((i + tile0) * sub - 1, 0), 0)),
            pl.BlockSpec((1, 8, 3 * CV_W),
                         lambda b, i: (b, jnp.minimum((i + tile0 + 1) * sub, nsub - 1), 0)),
            pl.BlockSpec((8, N_MOD * D_MODEL), const2),
            pl.BlockSpec((1, HG_W), const2),
            pl.BlockSpec((3, CV_W), const2),
            pl.BlockSpec((D_MODEL, D_MODEL), const2),
            pl.BlockSpec((1, D_MODEL), const2),
            pl.BlockSpec((N_EXPERTS, D_MODEL), const2),
            pl.BlockSpec((HG_W, HG_W), const2),
        ],
        out_specs=[
            pl.BlockSpec((1, ROW_TILE, D_MODEL), omap),
            pl.BlockSpec((1, ROW_TILE, D_MODEL), omap),
            pl.BlockSpec((1, N_EXPERTS, ROW_TILE), lambda b, i: (b, 0, i)),
        ],
        out_shape=[
            jax.ShapeDtypeStruct((B, s_out, D_MODEL), F32),
            jax.ShapeDtypeStruct((B, s_out, D_MODEL), BF16),
            jax.ShapeDtypeStruct((B, N_EXPERTS, s_out), F32),
        ],
        name="mix_out",
        compiler_params=_cparams(("arbitrary", "arbitrary")),
    )(xc, xl, att, o2[0], o2[1], hqg, cv, cv, cv, mod_l, gain, conv_w, wo_bf16, n2g, wr_t, ones)


def _topk_kernel(aff_ref, tri_ref, pos_ref, cnt_ref, *, segments):
    tri = tri_ref[...]

    def excl_cumsum(mask, n):
        carry = jnp.zeros((N_EXPERTS, 1), F32)
        parts = []
        carries = []
        for c in range(n // 128):
            carries.append(carry)
            blk = jnp.where(mask[:, c * 128:(c + 1) * 128], 1.0, 0.0)
            parts.append(_dot(blk.astype(BF16), tri) + carry)
            carry = carry + jnp.sum(blk, axis=1, keepdims=True)
        return jnp.concatenate(parts, axis=1), carries + [carry]

    for (lo, n, k) in segments:
        a = aff_ref[0, :, lo:lo + n]

        def body(it, thr):
            cand = thr | jnp.left_shift(jnp.int32(1), 30 - it)
            cnt = jnp.sum(jnp.where(a >= pltpu.bitcast(cand, F32), 1.0, 0.0), axis=1, keepdims=True)
            return jnp.where(cnt >= k, cand, thr)

        thr = lax.fori_loop(0, 31, body, jnp.zeros((N_EXPERTS, 1), I32))
        above = a >= pltpu.bitcast(thr + 1, F32)
        tied = jnp.logical_and(a >= pltpu.bitcast(thr, F32), jnp.logical_not(above))
        n_above = jnp.sum(jnp.where(above, 1.0, 0.0), axis=1, keepdims=True)
        rank_tied, _ = excl_cumsum(tied, n)
        sel = above | (tied & (rank_tied < (k - n_above)))
        pos, counts = excl_cumsum(sel, n)
        pos_ref[0, :, lo:lo + n] = jnp.where(sel, pos.astype(I32), -1)
    step = ROW_TILE // 128
    lane = lax.broadcasted_iota(I32, (N_EXPERTS, 128), 1)
    cnt = jnp.zeros((N_EXPERTS, 128), F32)
    for c, col in enumerate(counts[0::step]):
        cnt = jnp.where(lane == c, col, cnt)
    cnt_ref[0] = cnt.astype(I32)


def _topk_positions(aff, segments):
    B, E, S = aff.shape
    assert segments[-1][1] // ROW_TILE + 1 <= 128
    tri = jnp.asarray(np.triu(np.ones((128, 128)), 1), BF16)
    kern = functools.partial(_topk_kernel, segments=segments)
    return pl.pallas_call(
        kern,
        grid=(B,),
        in_specs=[pl.BlockSpec((1, E, S), lambda b: (b, 0, 0)),
                  pl.BlockSpec((128, 128), lambda b: (0, 0))],
        out_specs=[pl.BlockSpec((1, E, S), lambda b: (b, 0, 0)),
                   pl.BlockSpec((1, E, 128), lambda b: (b, 0, 0))],
        out_shape=[jax.ShapeDtypeStruct((B, E, S), I32),
                   jax.ShapeDtypeStruct((B, E, 128), I32)],
        name="topk_positions",
        compiler_params=_cparams(("arbitrary",)),
    )(aff, tri)


def _div_pow2(x, d):
    assert d & (d - 1) == 0
    return lax.shift_right_logical(x, jnp.int32(d.bit_length() - 1))


def _gather_kernel(cnt_ref, pos_ref, aff_ref, h_ref, *refs, cap_l, cap_c):
    if cap_c:
        xl_ref, gi_ref, xc_ref, gc_ref = refs
    else:
        xl_ref, gi_ref = refs
    b = pl.program_id(0)
    e0 = pl.program_id(1) * GATHER_GROUP
    j = pl.program_id(2)
    G = GATHER_GROUP
    GR = PACK_GRANULE
    W = min(GATHER_WINDOW, cap_l)
    pos = pos_ref[0]
    aff = aff_ref[0]
    h = h_ref[0]

    @pl.when(j == 0)
    def _():
        xl_ref[...] = jnp.zeros_like(xl_ref)
        gi_ref[...] = jnp.zeros_like(gi_ref)

    first_lane = lax.broadcasted_iota(I32, (W, 2), 1) == 0

    def add_rows(g, start, onehot_f32, rows):
        a = pl.multiple_of(start, GR)
        xl_ref[g, 0, pl.ds(a, W), :] += rows.astype(BF16)
        gate = jnp.sum(onehot_f32 * aff[g:g + 1, :], axis=1, keepdims=True)
        tok_id = jnp.sum(onehot_f32 * tok, axis=1, keepdims=True)
        gi_ref[g, 0, pl.ds(a, W), :] += jnp.where(first_lane, gate, tok_id)

    def latent():
        c = j - 1 if cap_c else j
        slot_i32 = lax.broadcasted_iota(I32, (W, ROW_TILE), 0)
        starts = [jnp.minimum(_div_pow2(cnt_ref[b, e0 + g, c], GR) * GR, cap_l - W)
                  for g in range(G)]
        hits = [(pos[g:g + 1, :] - starts[g]) == slot_i32 for g in range(G)]
        onehots = [jnp.where(hit, 1.0, 0.0) for hit in hits]
        prod = _dot(jnp.concatenate(onehots, axis=0).astype(BF16), h)
        for g in range(G):
            add_rows(g, starts[g], onehots[g], prod[g * W:(g + 1) * W])
        over = [cnt_ref[b, e0 + g, c + 1] - (starts[g] + W) for g in range(G)]
        worst = over[0]
        for o in over[1:]:
            worst = jnp.maximum(worst, o)

        @pl.when(worst > 0)
        def _():
            for g in range(G):
                def extra_window(k, carry, g=g):
                    first_slot = starts[g] + (k + 1) * W
                    start = jnp.minimum(first_slot, cap_l - W)
                    pg = pos[g:g + 1, :]
                    hit = ((pg - start) == slot_i32) & (pg >= first_slot)
                    onehot = jnp.where(hit, 1.0, 0.0)
                    add_rows(g, start, onehot, _dot(onehot.astype(BF16), h))
                    return carry

                lax.fori_loop(0, _div_pow2(jnp.maximum(over[g], 0) + (W - 1), W), extra_window, 0)

    tok = None
    if cap_c:
        tok = (lax.broadcasted_iota(I32, (1, ROW_TILE), 1) + (j - 1) * ROW_TILE).astype(F32)

        @pl.when(j == 0)
        def _():
            slot_c = lax.broadcasted_iota(I32, (cap_c, ROW_TILE), 0)
            onehots = [jnp.where(pos[g:g + 1, :] == slot_c, 1.0, 0.0) for g in range(G)]
            prod = _dot(jnp.concatenate(onehots, axis=0).astype(BF16), h)
            for g in range(G):
                xc_ref[g, 0] = prod[g * cap_c:(g + 1) * cap_c].astype(BF16)
                gc_ref[g, 0] = jnp.sum(onehots[g] * aff[g:g + 1, :], axis=1, keepdims=True)

        pl.when(j > 0)(latent)
    else:
        tok = (lax.broadcasted_iota(I32, (1, ROW_TILE), 1) + j * ROW_TILE).astype(F32)
        latent()


def _gather_rows(cnt, posm, aff, h2, cap_l, cap_c):
    B, E, S = posm.shape
    G = GATHER_GROUP
    assert min(GATHER_WINDOW, cap_l) % PACK_GRANULE == 0 and cap_l % PACK_GRANULE == 0
    kern = functools.partial(_gather_kernel, cap_l=cap_l, cap_c=cap_c)
    omap = lambda b, g, j, cnt_ref: (g, b, 0, 0)
    out_specs = [pl.BlockSpec((G, 1, cap_l, D_MODEL), omap), pl.BlockSpec((G, 1, cap_l, 2), omap)]
    out_shape = [jax.ShapeDtypeStruct((E, B, cap_l, D_MODEL), BF16),
                 jax.ShapeDtypeStruct((E, B, cap_l, 2), F32)]
    if cap_c:
        out_specs += [pl.BlockSpec((G, 1, cap_c, D_MODEL), omap),
                      pl.BlockSpec((G, 1, cap_c, 1), omap)]
        out_shape += [jax.ShapeDtypeStruct((E, B, cap_c, D_MODEL), BF16),
                      jax.ShapeDtypeStruct((E, B, cap_c, 1), F32)]
    tile_map = lambda b, g, j, cnt_ref: (b, g, j)
    grid_spec = pltpu.PrefetchScalarGridSpec(
        num_scalar_prefetch=1,
        grid=(B, E // G, S // ROW_TILE),
        in_specs=[pl.BlockSpec((1, G, ROW_TILE), tile_map),
                  pl.BlockSpec((1, G, ROW_TILE), tile_map),
                  pl.BlockSpec((1, ROW_TILE, D_MODEL), lambda b, g, j, cnt_ref: (b, j, 0))],
        out_specs=out_specs,
    )
    return pl.pallas_call(
        kern,
        grid_spec=grid_spec,
        out_shape=out_shape,
        name="gather_rows",
        compiler_params=_cparams(("arbitrary", "arbitrary", "arbitrary")),
    )(cnt, posm, aff, h2)


def _expert_kernel(*refs, row_chunk, n_sets):
    x_refs = refs[:n_sets]
    g_refs = refs[n_sets:2 * n_sets]
    wg_ref, wu_ref, wd_ref = refs[2 * n_sets:2 * n_sets + 3]
    y_refs = refs[2 * n_sets + 3:3 * n_sets + 3]
    acc_refs = refs[3 * n_sets + 3:]
    f = pl.program_id(1)
    nf = pl.num_programs(1)
    wg = wg_ref[0, 0].astype(BF16)
    wu = wu_ref[0, 0].astype(BF16)
    wd = wd_ref[0, 0].astype(BF16)

    def hidden_tile(first):
        for x_ref, acc_ref in zip(x_refs, acc_refs):
            rows = x_ref.shape[1]
            step = min(row_chunk, rows)
            for r in range(rows // step):
                rs = slice(r * step, (r + 1) * step)
                x = x_ref[0, rs, :]
                g = _dot(x, wg)
                u = _dot(x, wu)
                y = _dot((g * jax.nn.sigmoid(g) * u).astype(BF16), wd)
                if first:
                    acc_ref[rs, :] = y
                else:
                    acc_ref[rs, :] += y

    pl.when(f == 0)(functools.partial(hidden_tile, True))
    pl.when(f > 0)(functools.partial(hidden_tile, False))

    @pl.when(f == nf - 1)
    def _():
        for g_ref, y_ref, acc_ref in zip(g_refs, y_refs, acc_refs):
            y_ref[0] = (acc_ref[...] * g_ref[0][:, 0:1]).astype(BF16)


def _expert_mlp(x_sets, g_sets, w_gate, w_up, w_down, layer, row_chunk):
    E = N_EXPERTS
    nf = D_EXPERT // FF_TILE
    n_sets = len(x_sets)
    kern = functools.partial(_expert_kernel, row_chunk=row_chunk, n_sets=n_sets)
    xspec = lambda r: pl.BlockSpec((1, r, D_MODEL), lambda e, f: (e, 0, 0))
    gspec = lambda gs: pl.BlockSpec((1,) + gs.shape[1:], lambda e, f: (e, 0, 0))
    return pl.pallas_call(
        kern,
        grid=(E, nf),
        in_specs=[xspec(xs.shape[1]) for xs in x_sets] + [gspec(gs) for gs in g_sets] + [
            pl.BlockSpec((1, 1, D_MODEL, FF_TILE), lambda e, f: (layer, e, 0, f)),
            pl.BlockSpec((1, 1, D_MODEL, FF_TILE), lambda e, f: (layer, e, 0, f)),
            pl.BlockSpec((1, 1, FF_TILE, D_MODEL), lambda e, f: (layer, e, f, 0))],
        out_specs=[xspec(xs.shape[1]) for xs in x_sets],
        out_shape=[jax.ShapeDtypeStruct(xs.shape, BF16) for xs in x_sets],
        scratch_shapes=[pltpu.VMEM(xs.shape[1:], F32) for xs in x_sets],
        name="expert_mlp",
        compiler_params=_cparams(("arbitrary", "arbitrary")),
    )(*x_sets, *g_sets, w_gate, w_up, w_down)


def _combine_kernel(cnt_ref, x_ref, pos_ref, *refs, cap_l, cap_c, final):
    if cap_c:
        yl_ref, il_ref, yc_ref, mod_ref, fg_ref, o_ref, acc_ref, pk_ref, tk_ref = refs
    else:
        yl_ref, il_ref, mod_ref, fg_ref, o_ref, acc_ref, pk_ref, tk_ref = refs
    b = pl.program_id(0)
    i = pl.program_id(1)
    row = jnp.where(i == 0, 4, b) if cap_c else b
    gate = mod_ref[pl.ds(row, 1), 5 * D_MODEL:6 * D_MODEL]
    pos = pos_ref[0]
    acc_ref[...] = jnp.zeros_like(acc_ref)
    GR = PACK_GRANULE
    W = min(GATHER_WINDOW, cap_l)

    def add_latent():
        t = i - 1 if cap_c else i
        starts = [jnp.minimum(_div_pow2(cnt_ref[b, e, t], GR) * GR, cap_l - W)
                  for e in range(N_EXPERTS)]
        for e in range(N_EXPERTS):
            a = pl.multiple_of(starts[e], GR)
            pk_ref[e * W:(e + 1) * W, :] = yl_ref[e, 0, pl.ds(a, W), :]
            tk_ref[e * W:(e + 1) * W, :] = il_ref[e, 0, pl.ds(a, W), 1:2].astype(I32)

        def product(tok_ids, rows):
            lane = lax.broadcasted_iota(I32, (tok_ids.shape[0], ROW_TILE), 1) + t * ROW_TILE
            onehot = jnp.where(tok_ids == lane, 1.0, 0.0)
            return _dot(onehot.T.astype(BF16), rows)

        acc_ref[...] += product(tk_ref[...], pk_ref[...])
        over = [cnt_ref[b, e, t + 1] - (starts[e] + W) for e in range(N_EXPERTS)]
        worst = over[0]
        for o in over[1:]:
            worst = jnp.maximum(worst, o)

        @pl.when(worst > 0)
        def _():
            wx = min(2 * W, cap_l)
            slot = lax.broadcasted_iota(I32, (wx, 1), 0)
            for e in range(N_EXPERTS):
                def extra_window(k, carry, e=e):
                    first_slot = starts[e] + W + k * wx
                    a = pl.multiple_of(jnp.minimum(first_slot, cap_l - wx), GR)
                    ids = jnp.where(slot + a >= first_slot,
                                    il_ref[e, 0, pl.ds(a, wx), 1:2].astype(I32), -1)
                    acc_ref[...] += product(ids, yl_ref[e, 0, pl.ds(a, wx), :])
                    return carry

                lax.fori_loop(0, _div_pow2(jnp.maximum(over[e], 0) + (wx - 1), wx),
                              extra_window, 0)

    def add_context():
        ncol = yc_ref.shape[1]
        lane = lax.broadcasted_iota(I32, (ROW_TILE, ncol), 1)
        for e in range(N_EXPERTS):
            pe = pos[:, e:e + 1]
            tgt = jnp.where(pe >= 0, pe + b * cap_c, -1)
            acc_ref[...] += _dot(jnp.where(tgt == lane, 1.0, 0.0).astype(BF16), yc_ref[e])

    if cap_c:
        pl.when(i == 0)(add_context)
        pl.when(i > 0)(add_latent)
    else:
        add_latent()
    x2 = x_ref[0] + gate * acc_ref[...]
    if final:
        var = jnp.mean(x2 * x2, axis=-1, keepdims=True)
        x2 = x2 * lax.rsqrt(var + EPS) * fg_ref[...]
    o_ref[0] = x2


def _combine(cnt, x1, pos_t, yl, il, yc, mod_l, final_g, cap_l, cap_c, final):
    B, S, _ = x1.shape
    E = N_EXPERTS
    rmap = lambda b, i, cnt_ref: (b, i, 0)
    const2 = lambda b, i, cnt_ref: (0, 0)
    kern = functools.partial(_combine_kernel, cap_l=cap_l, cap_c=cap_c, final=final)
    per_sample = lambda b, i, cnt_ref: (0, b, 0, 0)
    y_specs = [pl.BlockSpec((E, 1, cap_l, D_MODEL), per_sample, pipeline_mode=pl.Buffered(1)),
               pl.BlockSpec((E, 1, cap_l, 2), per_sample, pipeline_mode=pl.Buffered(1))]
    y_args = [yl, il]
    pack_rows = E * min(GATHER_WINDOW, cap_l)
    if cap_c:
        y_specs.append(pl.BlockSpec(yc.shape, lambda b, i, cnt_ref: (0, 0, 0),
                                    pipeline_mode=pl.Buffered(1)))
        y_args.append(yc)
    grid_spec = pltpu.PrefetchScalarGridSpec(
        num_scalar_prefetch=1,
        grid=(B, S // ROW_TILE),
        in_specs=[pl.BlockSpec((1, ROW_TILE, D_MODEL), rmap),
                  pl.BlockSpec((1, ROW_TILE, E), rmap)] + y_specs + [
                  pl.BlockSpec((8, N_MOD * D_MODEL), const2),
                  pl.BlockSpec((1, D_MODEL), const2)],
        out_specs=pl.BlockSpec((1, ROW_TILE, D_MODEL), rmap),
        scratch_shapes=[pltpu.VMEM((ROW_TILE, D_MODEL), F32),
                        pltpu.VMEM((pack_rows, D_MODEL), BF16),
                        pltpu.VMEM((pack_rows, 1), I32)],
    )
    return pl.pallas_call(
        kern,
        grid_spec=grid_spec,
        out_shape=jax.ShapeDtypeStruct((B, S, D_MODEL), F32),
        name="combine",
        compiler_params=_cparams(("arbitrary", "arbitrary")),
    )(cnt, x1, pos_t, *y_args, mod_l, final_g)


def _rope_tables(n_lat, ctx_len):
    t = np.arange(n_lat)
    pos = np.stack([t // GRID_W, t % GRID_W], axis=-1).astype(np.float32)
    inv = (ROPE_BASE ** (-np.arange(ROPE_FREQS, dtype=np.float32) / ROPE_FREQS)).astype(np.float32)
    ang = pos[:, :, None] * inv
    cos = np.cos(ang).astype(np.float32)
    sin = np.sin(ang).astype(np.float32)
    cos64 = np.concatenate([cos[:, 0], cos[:, 0], cos[:, 1], cos[:, 1]], axis=-1)
    sin64 = np.concatenate([-sin[:, 0], sin[:, 0], -sin[:, 1], sin[:, 1]], axis=-1)
    cos_t = np.concatenate([np.ones((ctx_len, 64), np.float32), cos64], axis=0)
    sin_t = np.concatenate([np.zeros((ctx_len, 64), np.float32), sin64], axis=0)
    return (jnp.asarray(np.tile(cos_t, (1, 2))), jnp.asarray(np.tile(sin_t, (1, 2))))


def kernel(x, c, ctx, c_ctx, ada_w, ada_b, norm1_g, norm2_g, w_in, attn_sink, hgrn_lb,
           hgrn_norm_g, conv_w, w_o, router_w, exp_w_gate, exp_w_up, exp_w_down, final_norm_g):
    B, T, D = x.shape
    L = ctx.shape[1]
    depth = ada_w.shape[0]
    assert D == D_MODEL and L == ROW_TILE and T % ROW_TILE == 0 and B <= 4
    S = L + T
    cap_l = EC_CAPACITY * T // N_EXPERTS
    cap_c = EC_CAPACITY * L // N_EXPERTS
    assert cap_l % 16 == 0 and cap_c % 16 == 0 and (B * cap_c) % 16 == 0

    cos_t, sin_t = _rope_tables(T, L)
    gamma = jax.nn.softmax(hgrn_lb.astype(F32), axis=0)
    lb_all = jnp.cumsum(gamma, axis=0) - gamma[0]
    cvec = jnp.concatenate([c, jnp.zeros((4 - B, D), F32), c_ctx[None],
                            jnp.zeros((3, D), F32)], axis=0)
    mod = _modulation(cvec, ada_w, ada_b)
    xs = (ctx, x)

    def interleave_heads(w, axis):
        shp = w.shape
        w = w.reshape(shp[:axis] + (N_KV, GQA_GROUP, HEAD_DIM) + shp[axis + 1:])
        return jnp.swapaxes(w, axis, axis + 1).reshape(shp)

    for l in range(depth):
        last = l == depth - 1
        w_in_l = w_in[l].astype(BF16)
        w_in_l = jnp.concatenate([w_in_l[:, :1024], interleave_heads(w_in_l[:, 1024:1536], 1),
                                  w_in_l[:, 1536:]], axis=1)
        w_o_l = w_o[l].astype(BF16)
        w_o_l = jnp.concatenate([interleave_heads(w_o_l[:ATT_W], 0), w_o_l[ATT_W:]], axis=0)
        q, kv, hg, hqg, cv = _in_projection(xs, mod[l], norm1_g[l][None], w_in_l, cos_t, sin_t)
        att = _attention(q, kv, attn_sink[l], L, skip_ctx=last)
        o2 = _hgrn(hg, hqg, lb_all[l], L, HG_LOCAL_FIRST if l == 0 else HG_LOCAL)
        x1, h2, aff = _mix_out(xs, att, o2, hqg, cv, mod[l], hgrn_norm_g[l][None], conv_w[l],
                               w_o_l, norm2_g[l][None], router_w[l].T, skip_ctx=last)
        lm, cc = (0, 0) if last else (L, cap_c)
        segments = ((lm, T, cap_l),) if last else ((0, L, cap_c), (L, T, cap_l))
        posm, cnt = _topk_positions(aff, segments)
        gathered = _gather_rows(cnt, posm, aff, h2, cap_l, cc)
        il = gathered[1]
        x_sets = [a.reshape(N_EXPERTS, -1, D) for a in gathered[0::2]]
        g_sets = [a.reshape(N_EXPERTS, -1, a.shape[-1]) for a in gathered[1::2]]
        ys = _expert_mlp(x_sets, g_sets, exp_w_gate, exp_w_up, exp_w_down, l, EXPERT_ROWS)
        yl = ys[0].reshape(N_EXPERTS, B, cap_l, D)
        xs = _combine(cnt, x1, jnp.swapaxes(posm, 1, 2), yl, il, None if last else ys[1], mod[l],
                      final_norm_g[None], cap_l, cc, final=last)
    return xs
```

```python
import functools

import numpy as np
import jax
import jax.numpy as jnp
from jax import lax
from jax.experimental import pallas as pl
from jax.experimental.pallas import tpu as pltpu

F32 = jnp.float32
BF16 = jnp.bfloat16
I32 = jnp.int32

D_MODEL = 1024
GRID_W = 64
EPS = 1e-6
LB_FLOOR = 1e-30
N_MOD = 6
ATT_W = 512
HG_W = 256
CV_W = 256
HEAD_DIM = 64
N_Q = 8
N_KV = 2
GQA_GROUP = 4
KV_W = 128
ROPE_BASE = 10000.0
ROPE_FREQS = 16
HG_HEADS = 4
N_EXPERTS = 16
EC_CAPACITY = 2
D_EXPERT = 2048
IN_COLS = 2816

ROW_TILE = 256
ATT_BLOCK = 128
HG_CHUNK = 128
HG_LEVELS = 7
HG_BATCH = 4
HG_LOCAL = 64
HG_LOCAL_FIRST = 32
HG_LOCAL_MAX_LOG = 115.0
GATHER_GROUP = 16
GATHER_WINDOW = 64
PACK_GRANULE = 16
FF_TILE = 512
EXPERT_ROWS = 1024
MOD_TILE = 1536
VMEM_LIMIT = 56 * 1024 * 1024

NEG_BIG = -1e30
LOG2E = 1.4426950408889634


def _cparams(sem):
    return pltpu.CompilerParams(dimension_semantics=sem, vmem_limit_bytes=VMEM_LIMIT)


def _dot(a, b):
    return jnp.dot(a, b, preferred_element_type=F32)


def _dot_nt(a, b):
    return lax.dot_general(a, b, (((1,), (1,)), ((), ())), preferred_element_type=F32)


def _mod_kernel(a_ref, w_ref, b_ref, o_ref):
    a = a_ref[...]
    a = a * jax.nn.sigmoid(a)
    w = w_ref[0]
    a_hi = a.astype(BF16)
    a_lo = (a - a_hi.astype(F32)).astype(BF16)
    w_hi = w.astype(BF16)
    w_lo = (w - w_hi.astype(F32)).astype(BF16)
    r = _dot(jnp.concatenate([a_hi, a_lo], axis=0), w_hi)
    o_ref[0] = r[0:8] + r[8:16] + _dot(a_hi, w_lo) + b_ref[0]


def _modulation(cvec, ada_w, ada_b):
    depth = ada_w.shape[0]
    ncol = ada_w.shape[2]
    return pl.pallas_call(
        _mod_kernel,
        grid=(depth, ncol // MOD_TILE),
        in_specs=[
            pl.BlockSpec((8, D_MODEL), lambda l, j: (0, 0)),
            pl.BlockSpec((1, D_MODEL, MOD_TILE), lambda l, j: (l, 0, j)),
            pl.BlockSpec((1, 1, MOD_TILE), lambda l, j: (l, 0, j)),
        ],
        out_specs=pl.BlockSpec((1, 8, MOD_TILE), lambda l, j: (l, 0, j)),
        out_shape=jax.ShapeDtypeStruct((depth, 8, ncol), F32),
        name="modulation",
        compiler_params=_cparams(("arbitrary", "arbitrary")),
    )(cvec, ada_w, ada_b.reshape(depth, 1, ncol))


def _swap_halves(x):
    n = x.shape[-1]
    lane = lax.broadcasted_iota(I32, x.shape, x.ndim - 1)
    up = pltpu.roll(x, n - ROPE_FREQS, x.ndim - 1)
    dn = pltpu.roll(x, ROPE_FREQS, x.ndim - 1)
    return jnp.where((lane % (2 * ROPE_FREQS)) < ROPE_FREQS, up, dn)


def _rmsnorm_mod(x, g, shift, scale):
    var = jnp.mean(x * x, axis=-1, keepdims=True)
    y = x * lax.rsqrt(var + EPS) * g
    return y * (1.0 + scale) + shift


def _row_sources(src):
    if isinstance(src, tuple):
        ctx, lat = src
        return ctx, lat, 1, ctx.shape[1] + lat.shape[1]
    return src, src, 0, src.shape[1]


def _inproj_kernel(xc_ref, x_ref, mod_ref, g_ref, w_ref, cos_ref, sin_ref,
                   q_ref, kv_ref, hg_ref, hqg_ref, cv_ref):
    b = pl.program_id(0)
    i = pl.program_id(1)
    row = jnp.where(i == 0, 4, b)
    shift = mod_ref[pl.ds(row, 1), 0:D_MODEL]
    scale = mod_ref[pl.ds(row, 1), D_MODEL:2 * D_MODEL]
    xin = jnp.where(i == 0, xc_ref[0], x_ref[0])
    h = _rmsnorm_mod(xin, g_ref[...], shift, scale)
    p = _dot(h.astype(BF16), w_ref[...])
    cos2 = cos_ref[...]
    sin2 = sin_ref[...]
    k = p[:, 0:KV_W]
    k = k * cos2 + _swap_halves(k) * sin2
    kv_ref[0, :, 0:KV_W] = k.astype(BF16)
    kv_ref[0, :, KV_W:2 * KV_W] = p[:, KV_W:2 * KV_W].astype(BF16)
    hg_ref[0] = p[:, 256:1024]
    q = p[:, 1024:1536]
    cos8 = jnp.concatenate([cos2] * 4, axis=1)
    sin8 = jnp.concatenate([sin2] * 4, axis=1)
    q = (q * cos8 + _swap_halves(q) * sin8) * (HEAD_DIM ** -0.5 * LOG2E)
    q_ref[0] = q.astype(BF16)
    hqg_ref[0] = p[:, 1536:2048]
    cv_ref[0] = p[:, 2048:2816]


def _in_projection(src, mod_l, g, w_bf16, cos_t, sin_t):
    xc, xl, off, S = _row_sources(src)
    B = xc.shape[0]
    nt = S // ROW_TILE
    row_map = lambda b, i: (b, i, 0)
    const2 = lambda b, i: (0, 0)
    return pl.pallas_call(
        _inproj_kernel,
        grid=(B, nt),
        in_specs=[
            pl.BlockSpec((1, ROW_TILE, D_MODEL), lambda b, i: (b, 0, 0)),
            pl.BlockSpec((1, ROW_TILE, D_MODEL), lambda b, i: (b, jnp.maximum(i - off, 0), 0)),
            pl.BlockSpec((8, N_MOD * D_MODEL), const2),
            pl.BlockSpec((1, D_MODEL), const2),
            pl.BlockSpec((D_MODEL, IN_COLS), const2),
            pl.BlockSpec((ROW_TILE, 2 * HEAD_DIM), lambda b, i: (i, 0)),
            pl.BlockSpec((ROW_TILE, 2 * HEAD_DIM), lambda b, i: (i, 0)),
        ],
        out_specs=[
            pl.BlockSpec((1, ROW_TILE, ATT_W), row_map),
            pl.BlockSpec((1, ROW_TILE, 2 * KV_W), row_map),
            pl.BlockSpec((1, ROW_TILE, 3 * HG_W), row_map),
            pl.BlockSpec((1, ROW_TILE, 2 * HG_W), row_map),
            pl.BlockSpec((1, ROW_TILE, 3 * CV_W), row_map),
        ],
        out_shape=[
            jax.ShapeDtypeStruct((B, S, ATT_W), BF16),
            jax.ShapeDtypeStruct((B, S, 2 * KV_W), BF16),
            jax.ShapeDtypeStruct((B, S, 3 * HG_W), F32),
            jax.ShapeDtypeStruct((B, S, 2 * HG_W), F32),
            jax.ShapeDtypeStruct((B, S, 3 * CV_W), F32),
        ],
        name="in_projection",
        compiler_params=_cparams(("arbitrary", "arbitrary")),
    )(xc, xl, mod_l, g, w_bf16, cos_t, sin_t)


def _attn_kernel(sink_ref, q_ref, kp_ref, kc_ref, kn_ref, kx_ref, o_ref, *, blk0, nblk, nctx):
    n = pl.program_id(1) + blk0
    is_lat = n >= nctx
    has_prev = n > nctx
    has_next = n < nblk - 1
    W = ATT_BLOCK
    q = q_ref[0]
    qrows = jnp.concatenate([q[:, g * W:(g + 1) * W] for g in range(GQA_GROUP)], axis=0)
    kv_all = jnp.concatenate([kp_ref[0], kc_ref[0], kn_ref[0], kx_ref[0]], axis=0)
    nkeys = kv_all.shape[0]
    k_all = kv_all[:, 0:KV_W]
    v_ext = jnp.concatenate([kv_all[:, KV_W:2 * KV_W], jnp.ones((nkeys, KV_W), BF16)], axis=1)
    rows = GQA_GROUP * W
    ri = lax.broadcasted_iota(I32, (rows, W), 0) % W
    cj = lax.broadcasted_iota(I32, (rows, W), 1)
    m_prev = (cj >= ri) & has_prev
    m_cur = jnp.broadcast_to(is_lat, (rows, W))
    m_next = (cj <= ri) & (has_next & is_lat)
    grp = lax.broadcasted_iota(I32, (rows, 1), 0) // W
    low = cj < HEAD_DIM
    outs = []
    for h in range(N_KV):
        qh = jnp.where(low if h == 0 else jnp.logical_not(low), qrows, jnp.zeros_like(qrows))
        sink = jnp.zeros((rows, 1), F32)
        for g in range(GQA_GROUP):
            sink = jnp.where(grp == g, sink_ref[h * GQA_GROUP + g] * LOG2E, sink)
        s = _dot_nt(qh, k_all)
        segs = [jnp.where(m_prev, s[:, 0:W], NEG_BIG),
                jnp.where(m_cur, s[:, W:2 * W], NEG_BIG),
                jnp.where(m_next, s[:, 2 * W:3 * W], NEG_BIG)]
        segs += [s[:, c:c + W] for c in range(3 * W, nkeys, W)]
        mx = segs[0]
        for sg in segs[1:]:
            mx = jnp.maximum(mx, sg)
        m = jnp.maximum(jnp.max(mx, axis=1, keepdims=True), sink)
        p = jnp.concatenate([jnp.exp2(sg - m).astype(BF16) for sg in segs], axis=1)
        oe = _dot(p, v_ext)
        den = oe[:, KV_W:2 * KV_W] + jnp.exp2(sink - m)
        outs.append(oe[:, 0:KV_W] / den)
    o = jnp.where(low, outs[0], outs[1])
    for g in range(GQA_GROUP):
        o_ref[0, :, g * W:(g + 1) * W] = o[g * W:(g + 1) * W].astype(BF16)


def _attention(q, kv, sink, ctx_len, skip_ctx):
    B, S, _ = q.shape
    nblk = S // ATT_BLOCK
    nctx = ctx_len // ATT_BLOCK
    blk0 = nctx if skip_ctx else 0
    blk = lambda f: (lambda b, n: (b, f(n + blk0), 0))
    kern = functools.partial(_attn_kernel, blk0=blk0, nblk=nblk, nctx=nctx)
    return pl.pallas_call(
        kern,
        grid=(B, nblk - blk0),
        in_specs=[
            pl.BlockSpec(memory_space=pltpu.SMEM),
            pl.BlockSpec((1, ATT_BLOCK, ATT_W), blk(lambda n: n)),
            pl.BlockSpec((1, ATT_BLOCK, 2 * KV_W), blk(lambda n: jnp.maximum(n - 1, 0))),
            pl.BlockSpec((1, ATT_BLOCK, 2 * KV_W), blk(lambda n: n)),
            pl.BlockSpec((1, ATT_BLOCK, 2 * KV_W), blk(lambda n: jnp.minimum(n + 1, nblk - 1))),
            pl.BlockSpec((1, ctx_len, 2 * KV_W), lambda b, n: (b, 0, 0)),
        ],
        out_specs=pl.BlockSpec((1, ATT_BLOCK, ATT_W), lambda b, n: (b, n, 0)),
        out_shape=jax.ShapeDtypeStruct((B, S - blk0 * ATT_BLOCK, ATT_W), BF16),
        name="attention",
        compiler_params=_cparams(("arbitrary", "arbitrary")),
    )(sink, q, kv, kv, kv, kv)


def _hgrn_constants():
    C = HG_CHUNK
    t = np.arange(C)[:, None]
    r = np.arange(C)[None, :]
    tri = np.stack([r <= t, r >= t]).astype(np.float32)
    x = t ^ r
    lvl = np.where(x > 0, np.floor(np.log2(np.maximum(x, 1))).astype(np.int32), HG_LEVELS)
    lvl_f = np.where(t >= r, lvl, -1).astype(np.int32)
    lvl_b = np.where(t <= r, lvl, -1).astype(np.int32)
    lvl2 = np.stack([np.tile(lvl_f, (1, HG_HEADS)), np.tile(lvl_b, (1, HG_HEADS))])
    return tri, lvl2


def _span_row(x, span, row):
    C = x.shape[0]
    if span >= 8:
        x3 = x.reshape(C // span, span, x.shape[1])
        return jnp.broadcast_to(x3[:, row:row + 1, :], x3.shape).reshape(x.shape)
    pos = lax.broadcasted_iota(I32, x.shape, 0) % span
    out = x
    for p in range(span):
        if p != row:
            out = jnp.where(pos == p, pltpu.roll(x, (p - row) % C, 0), out)
    return out


def _hgrn_prepare(v, z, qr, lb, tri, backward, n_local):
    C = HG_CHUNK
    logf = jnp.log(jnp.maximum(lb, LB_FLOOR) + (1.0 - lb) * jax.nn.sigmoid(z)) * LOG2E
    k = (1.0 - lb) * jax.nn.sigmoid(-z)
    q = qr * jax.nn.sigmoid(qr)
    hi = logf.astype(BF16)
    lo = (logf - hi.astype(F32)).astype(BF16)
    cs = _dot(tri, jnp.concatenate([hi, lo], axis=1))
    lam = cs[:, 0:HG_W] + cs[:, HG_W:2 * HG_W]
    tot = lam[0:1] if backward else lam[C - 1:C]
    local = lam - _span_row(lam, n_local, n_local // 2)
    return dict(v=v, q=q, k=k, lam=lam, tot=tot, local=local)


def _hgrn_finish(g, head_masks, same_head, lvl, st_ref, sidx, backward, fast, n_local):
    C = HG_CHUNK
    q, k, v, lam, tot = g["q"], g["k"], g["v"], g["lam"], g["tot"]
    zero = jnp.zeros((C, HG_W), BF16)

    def per_head_rows(x):
        return jnp.concatenate([jnp.where(hm, x, zero) for hm in head_masks], axis=0)

    qb = q.astype(BF16)
    kb = k.astype(BF16)
    if fast:
        local_levels = n_local.bit_length() - 1
        a = jnp.where((lvl == HG_LEVELS) | ((lvl >= 0) & (lvl < local_levels)),
                      _dot_nt((q * jnp.exp2(g["local"])).astype(BF16),
                              per_head_rows((k * jnp.exp2(-g["local"])).astype(BF16))), 0.0)
        levels = range(local_levels, HG_LEVELS)
    else:
        a = jnp.where(lvl == HG_LEVELS, _dot_nt(qb, per_head_rows(kb)), 0.0)
        levels = range(HG_LEVELS)
    for l in levels:
        m = 1 << l
        ref = _span_row(lam, 2 * m, m if backward else m - 1)
        fac = jnp.exp2(-jnp.abs(lam - ref)).astype(BF16)
        a = jnp.where(lvl == l, _dot_nt(qb * fac, per_head_rows(kb * fac)), a)
    st = st_ref[sidx]
    q_in = (q * jnp.exp2(lam)).astype(BF16)
    o = _dot(a.astype(BF16), per_head_rows(v.astype(BF16))) + _dot_nt(q_in, st.astype(BF16))
    k_out = (k * jnp.exp2(tot - lam)).astype(BF16)
    upd = _dot(v.T.astype(BF16), k_out)
    st_ref[sidx] = st * jnp.exp2(tot) + jnp.where(same_head, upd, 0.0)
    return o


def _hgrn_kernel(vf_ref, zf_ref, qf_ref, vb_ref, zb_ref, qb_ref, lb_ref, tri_ref, lvl_ref,
                 of_ref, ob_ref, st_ref, *, n_local):
    j = pl.program_id(1)

    @pl.when(j == 0)
    def _():
        st_ref[...] = jnp.zeros_like(st_ref)

    head_id = (lax.broadcasted_iota(I32, (HG_CHUNK, HG_W), 1) // HEAD_DIM).astype(F32).astype(BF16)
    head_masks = [head_id == float(h) for h in range(HG_HEADS)]
    same_head = (lax.broadcasted_iota(I32, (HG_W, HG_W), 0) // HEAD_DIM
                 == lax.broadcasted_iota(I32, (HG_W, HG_W), 1) // HEAD_DIM)
    streams = []
    for d, (v_ref, z_ref, q_ref, o_ref) in enumerate(
            ((vf_ref, zf_ref, qf_ref, of_ref), (vb_ref, zb_ref, qb_ref, ob_ref))):
        for s in range(v_ref.shape[0]):
            g = _hgrn_prepare(v_ref[s], z_ref[s], q_ref[s], lb_ref[d], tri_ref[d], d == 1, n_local)
            streams.append((d, s, o_ref, g))
    worst = jnp.abs(streams[0][3]["local"])
    for _, _, _, g in streams[1:]:
        worst = jnp.maximum(worst, jnp.abs(g["local"]))
    can_use_local = jnp.max(worst) <= HG_LOCAL_MAX_LOG

    def run(fast):
        for d, s, o_ref, g in streams:
            o_ref[s] = _hgrn_finish(g, head_masks, same_head, lvl_ref[d], st_ref, 2 * s + d,
                                    backward=(d == 1), fast=fast, n_local=n_local)

    pl.when(can_use_local)(functools.partial(run, True))
    pl.when(jnp.logical_not(can_use_local))(functools.partial(run, False))


def _hgrn(hg, hqg, lb_l, ctx_len, n_local):
    B, S, _ = hg.shape
    nc = S // HG_CHUNK
    nctx = ctx_len // HG_CHUNK
    tri_np, lvl_np = _hgrn_constants()
    tri = jnp.asarray(tri_np, BF16)
    lvl = jnp.asarray(lvl_np, I32)

    def back(j):
        return jnp.where(j < nctx, nctx - 1 - j, nc - 1 - (j - nctx))

    bs = HG_BATCH if B % HG_BATCH == 0 else 1
    blk = (bs, HG_CHUNK, HG_W)
    fwd = lambda col: (lambda b, j: (b, j, col))
    bwd = lambda col: (lambda b, j: (b, back(j), col))
    const3 = lambda b, j: (0, 0, 0)
    return pl.pallas_call(
        functools.partial(_hgrn_kernel, n_local=n_local),
        grid=(B // bs, nc),
        in_specs=[
            pl.BlockSpec(blk, fwd(0)), pl.BlockSpec(blk, fwd(1)), pl.BlockSpec(blk, fwd(0)),
            pl.BlockSpec(blk, bwd(0)), pl.BlockSpec(blk, bwd(2)), pl.BlockSpec(blk, bwd(0)),
            pl.BlockSpec((2, 1, HG_W), const3),
            pl.BlockSpec((2, HG_CHUNK, HG_CHUNK), const3),
            pl.BlockSpec((2, HG_CHUNK, HG_HEADS * HG_CHUNK), const3),
        ],
        out_specs=[pl.BlockSpec(blk, fwd(0)), pl.BlockSpec(blk, bwd(0))],
        out_shape=[jax.ShapeDtypeStruct((B, S, HG_W), F32)] * 2,
        scratch_shapes=[pltpu.VMEM((2 * bs, HG_W, HG_W), F32)],
        name="hgrn_scan",
        compiler_params=_cparams(("arbitrary", "arbitrary")),
    )(hg, hg, hqg, hg, hg, hqg, lb_l.reshape(2, 1, HG_W), tri, lvl)


def _mixout_kernel(xc_ref, x_ref, att_ref, of_ref, ob_ref, g_ref, cv_ref, cprev_ref, cnext_ref,
                   mod_ref, gain_ref, cw_ref, wo_ref, n2_ref, wr_ref, ones_ref,
                   x1_ref, h2_ref, aff_ref, *, tile0, ntile):
    b = pl.program_id(0)
    i = pl.program_id(1) + tile0
    row = jnp.where(i == 0, 4, b)
    R = ROW_TILE

    def modv(c):
        return mod_ref[pl.ds(row, 1), c * D_MODEL:(c + 1) * D_MODEL]

    o = of_ref[0] + ob_ref[0]
    sq = o * o
    sq_hi = sq.astype(BF16)
    sq_lo = (sq - sq_hi.astype(F32)).astype(BF16)
    ms = _dot(sq_hi, ones_ref[...]) + _dot(sq_lo, ones_ref[...])
    g = g_ref[0]
    hg = o * lax.rsqrt(ms * (1.0 / HEAD_DIM) + EPS) * gain_ref[...] * (g * jax.nn.sigmoid(g))
    cv = cv_ref[0]
    u = cv[:, CV_W:2 * CV_W] * cv[:, 2 * CV_W:3 * CV_W]
    up = cprev_ref[0]
    un = cnext_ref[0]
    u_prev_row = up[7:8, CV_W:2 * CV_W] * up[7:8, 2 * CV_W:3 * CV_W]
    u_next_row = un[0:1, CV_W:2 * CV_W] * un[0:1, 2 * CV_W:3 * CV_W]
    u_prev_row = jnp.where(i <= 1, 0.0, u_prev_row)
    u_next_row = jnp.where((i == 0) | (i == ntile - 1), 0.0, u_next_row)
    ridx = lax.broadcasted_iota(I32, (R, CV_W), 0)
    u_m1 = jnp.where(ridx == 0, u_prev_row, pltpu.roll(u, 1, 0))
    u_p1 = jnp.where(ridx == R - 1, u_next_row, pltpu.roll(u, R - 1, 0))
    cw = cw_ref[...]
    conv = cv[:, 0:CV_W] * (u_m1 * cw[0:1] + u * cw[1:2] + u_p1 * cw[2:3])
    mix = (_dot(att_ref[0], wo_ref[0:ATT_W])
           + _dot(hg.astype(BF16), wo_ref[ATT_W:ATT_W + HG_W])
           + _dot(conv.astype(BF16), wo_ref[ATT_W + HG_W:D_MODEL]))
    x1 = jnp.where(i == 0, xc_ref[0], x_ref[0]) + modv(2) * mix
    x1_ref[0] = x1
    h2 = _rmsnorm_mod(x1, n2_ref[...], modv(3), modv(4))
    h_hi = h2.astype(BF16)
    h2_ref[0] = h_hi
    h_lo = (h2 - h_hi.astype(F32)).astype(BF16)
    wr = wr_ref[...]
    w_hi = wr.astype(BF16)
    w_lo = (wr - w_hi.astype(F32)).astype(BF16)
    r1 = _dot_nt(jnp.concatenate([w_hi, w_lo], axis=0), h_hi)
    logits = r1[0:N_EXPERTS] + r1[N_EXPERTS:2 * N_EXPERTS] + _dot_nt(w_hi, h_lo)
    e = jnp.exp(logits - jnp.max(logits, axis=0, keepdims=True))
    aff_ref[0] = e / jnp.sum(e, axis=0, keepdims=True)


def _mix_out(src, att, o2, hqg, cv, mod_l, gain, conv_w, wo_bf16, n2g, wr_t, skip_ctx):
    xc, xl, off, S = _row_sources(src)
    B = xc.shape[0]
    ntile = S // ROW_TILE
    tile0 = 1 if skip_ctx else 0
    sub = ROW_TILE // 8
    nsub = S // 8
    rmap = lambda b, i: (b, i + tile0, 0)
    omap = lambda b, i: (b, i, 0)
    s_out = S - tile0 * ROW_TILE
    const2 = lambda b, i: (0, 0)
    ones = jnp.asarray(np.kron(np.eye(HG_HEADS), np.ones((HEAD_DIM, HEAD_DIM))), BF16)
    kern = functools.partial(_mixout_kernel, tile0=tile0, ntile=ntile)
    return pl.pallas_call(
        kern,
        grid=(B, ntile - tile0),
        in_specs=[
            pl.BlockSpec((1, ROW_TILE, D_MODEL), lambda b, i: (b, 0, 0)),
            pl.BlockSpec((1, ROW_TILE, D_MODEL),
                         lambda b, i: (b, jnp.maximum(i + tile0 - off, 0), 0)),
            pl.BlockSpec((1, ROW_TILE, ATT_W), omap),
            pl.BlockSpec((1, ROW_TILE, HG_W), rmap),
            pl.BlockSpec((1, ROW_TILE, HG_W), rmap),
            pl.BlockSpec((1, ROW_TILE, HG_W), lambda b, i: (b, i + tile0, 1)),
            pl.BlockSpec((1, ROW_TILE, 3 * CV_W), rmap),
            pl.BlockSpec((1, 8, 3 * CV_W),
                         lambda b, i: (b, jnp.maximum((i + tile0) * sub - 1, 0), 0)),
            pl.BlockSpec((1, 8, 3 * CV_W),
                         lambda b, i: (b, jnp.minimum((i + tile0 + 1) * sub, nsub - 1), 0)),
            pl.BlockSpec((8, N_MOD * D_MODEL), const2),
            pl.BlockSpec((1, HG_W), const2),
            pl.BlockSpec((3, CV_W), const2),
            pl.BlockSpec((D_MODEL, D_MODEL), const2),
            pl.BlockSpec((1, D_MODEL), const2),
            pl.BlockSpec((N_EXPERTS, D_MODEL), const2),
            pl.BlockSpec((HG_W, HG_W), const2),
        ],
        out_specs=[
            pl.BlockSpec((1, ROW_TILE, D_MODEL), omap),
            pl.BlockSpec((1, ROW_TILE, D_MODEL), omap),
            pl.BlockSpec((1, N_EXPERTS, ROW_TILE), lambda b, i: (b, 0, i)),
        ],
        out_shape=[
            jax.ShapeDtypeStruct((B, s_out, D_MODEL), F32),
            jax.ShapeDtypeStruct((B, s_out, D_MODEL), BF16),
            jax.ShapeDtypeStruct((B, N_EXPERTS, s_out), F32),
        ],
        name="mix_out",
        compiler_params=_cparams(("arbitrary", "arbitrary")),
    )(xc, xl, att, o2[0], o2[1], hqg, cv, cv, cv, mod_l, gain, conv_w, wo_bf16, n2g, wr_t, ones)


def _topk_kernel(aff_ref, tri_ref, pos_ref, cnt_ref, *, segments):
    tri = tri_ref[...]

    def excl_cumsum(mask, n):
        carry = jnp.zeros((N_EXPERTS, 1), F32)
        parts = []
        carries = []
        for c in range(n // 128):
            carries.append(carry)
            blk = jnp.where(mask[:, c * 128:(c + 1) * 128], 1.0, 0.0)
            parts.append(_dot(blk.astype(BF16), tri) + carry)
            carry = carry + jnp.sum(blk, axis=1, keepdims=True)
        return jnp.concatenate(parts, axis=1), carries + [carry]

    for (lo, n, k) in segments:
        a = aff_ref[0, :, lo:lo + n]

        def body(it, thr):
            cand = thr | jnp.left_shift(jnp.int32(1), 30 - it)
            cnt = jnp.sum(jnp.where(a >= pltpu.bitcast(cand, F32), 1.0, 0.0), axis=1, keepdims=True)
            return jnp.where(cnt >= k, cand, thr)

        thr = lax.fori_loop(0, 31, body, jnp.zeros((N_EXPERTS, 1), I32))
        above = a >= pltpu.bitcast(thr + 1, F32)
        tied = jnp.logical_and(a >= pltpu.bitcast(thr, F32), jnp.logical_not(above))
        n_above = jnp.sum(jnp.where(above, 1.0, 0.0), axis=1, keepdims=True)
        rank_tied, _ = excl_cumsum(tied, n)
        sel = above | (tied & (rank_tied < (k - n_above)))
        pos, counts = excl_cumsum(sel, n)
        pos_ref[0, :, lo:lo + n] = jnp.where(sel, pos.astype(I32), -1)
    step = ROW_TILE // 128
    lane = lax.broadcasted_iota(I32, (N_EXPERTS, 128), 1)
    cnt = jnp.zeros((N_EXPERTS, 128), F32)
    for c, col in enumerate(counts[0::step]):
        cnt = jnp.where(lane == c, col, cnt)
    cnt_ref[0] = cnt.astype(I32)


def _topk_positions(aff, segments):
    B, E, S = aff.shape
    assert segments[-1][1] // ROW_TILE + 1 <= 128
    tri = jnp.asarray(np.triu(np.ones((128, 128)), 1), BF16)
    kern = functools.partial(_topk_kernel, segments=segments)
    return pl.pallas_call(
        kern,
        grid=(B,),
        in_specs=[pl.BlockSpec((1, E, S), lambda b: (b, 0, 0)),
                  pl.BlockSpec((128, 128), lambda b: (0, 0))],
        out_specs=[pl.BlockSpec((1, E, S), lambda b: (b, 0, 0)),
                   pl.BlockSpec((1, E, 128), lambda b: (b, 0, 0))],
        out_shape=[jax.ShapeDtypeStruct((B, E, S), I32),
                   jax.ShapeDtypeStruct((B, E, 128), I32)],
        name="topk_positions",
        compiler_params=_cparams(("arbitrary",)),
    )(aff, tri)


def _div_pow2(x, d):
    assert d & (d - 1) == 0
    return lax.shift_right_logical(x, jnp.int32(d.bit_length() - 1))


def _gather_kernel(cnt_ref, pos_ref, aff_ref, h_ref, *refs, cap_l, cap_c):
    if cap_c:
        xl_ref, gi_ref, xc_ref, gc_ref = refs
    else:
        xl_ref, gi_ref = refs
    b = pl.program_id(0)
    e0 = pl.program_id(1) * GATHER_GROUP
    j = pl.program_id(2)
    G = GATHER_GROUP
    GR = PACK_GRANULE
    W = min(GATHER_WINDOW, cap_l)
    pos = pos_ref[0]
    aff = aff_ref[0]
    h = h_ref[0]

    @pl.when(j == 0)
    def _():
        xl_ref[...] = jnp.zeros_like(xl_ref)
        gi_ref[...] = jnp.zeros_like(gi_ref)

    first_lane = lax.broadcasted_iota(I32, (W, 2), 1) == 0

    def add_rows(g, start, onehot_f32, rows):
        a = pl.multiple_of(start, GR)
        xl_ref[g, 0, pl.ds(a, W), :] += rows.astype(BF16)
        gate = jnp.sum(onehot_f32 * aff[g:g + 1, :], axis=1, keepdims=True)
        tok_id = jnp.sum(onehot_f32 * tok, axis=1, keepdims=True)
        gi_ref[g, 0, pl.ds(a, W), :] += jnp.where(first_lane, gate, tok_id)

    def latent():
        c = j - 1 if cap_c else j
        slot_i32 = lax.broadcasted_iota(I32, (W, ROW_TILE), 0)
        starts = [jnp.minimum(_div_pow2(cnt_ref[b, e0 + g, c], GR) * GR, cap_l - W)
                  for g in range(G)]
        hits = [(pos[g:g + 1, :] - starts[g]) == slot_i32 for g in range(G)]
        onehots = [jnp.where(hit, 1.0, 0.0) for hit in hits]
        prod = _dot(jnp.concatenate(onehots, axis=0).astype(BF16), h)
        for g in range(G):
            add_rows(g, starts[g], onehots[g], prod[g * W:(g + 1) * W])
        over = [cnt_ref[b, e0 + g, c + 1] - (starts[g] + W) for g in range(G)]
        worst = over[0]
        for o in over[1:]:
            worst = jnp.maximum(worst, o)

        @pl.when(worst > 0)
        def _():
            for g in range(G):
                def extra_window(k, carry, g=g):
                    first_slot = starts[g] + (k + 1) * W
                    start = jnp.minimum(first_slot, cap_l - W)
                    pg = pos[g:g + 1, :]
                    hit = ((pg - start) == slot_i32) & (pg >= first_slot)
                    onehot = jnp.where(hit, 1.0, 0.0)
                    add_rows(g, start, onehot, _dot(onehot.astype(BF16), h))
                    return carry

                lax.fori_loop(0, _div_pow2(jnp.maximum(over[g], 0) + (W - 1), W), extra_window, 0)

    tok = None
    if cap_c:
        tok = (lax.broadcasted_iota(I32, (1, ROW_TILE), 1) + (j - 1) * ROW_TILE).astype(F32)

        @pl.when(j == 0)
        def _():
            slot_c = lax.broadcasted_iota(I32, (cap_c, ROW_TILE), 0)
            onehots = [jnp.where(pos[g:g + 1, :] == slot_c, 1.0, 0.0) for g in range(G)]
            prod = _dot(jnp.concatenate(onehots, axis=0).astype(BF16), h)
            for g in range(G):
                xc_ref[g, 0] = prod[g * cap_c:(g + 1) * cap_c].astype(BF16)
                gc_ref[g, 0] = jnp.sum(onehots[g] * aff[g:g + 1, :], axis=1, keepdims=True)

        pl.when(j > 0)(latent)
    else:
        tok = (lax.broadcasted_iota(I32, (1, ROW_TILE), 1) + j * ROW_TILE).astype(F32)
        latent()


def _gather_rows(cnt, posm, aff, h2, cap_l, cap_c):
    B, E, S = posm.shape
    G = GATHER_GROUP
    assert min(GATHER_WINDOW, cap_l) % PACK_GRANULE == 0 and cap_l % PACK_GRANULE == 0
    kern = functools.partial(_gather_kernel, cap_l=cap_l, cap_c=cap_c)
    omap = lambda b, g, j, cnt_ref: (g, b, 0, 0)
    out_specs = [pl.BlockSpec((G, 1, cap_l, D_MODEL), omap), pl.BlockSpec((G, 1, cap_l, 2), omap)]
    out_shape = [jax.ShapeDtypeStruct((E, B, cap_l, D_MODEL), BF16),
                 jax.ShapeDtypeStruct((E, B, cap_l, 2), F32)]
    if cap_c:
        out_specs += [pl.BlockSpec((G, 1, cap_c, D_MODEL), omap),
                      pl.BlockSpec((G, 1, cap_c, 1), omap)]
        out_shape += [jax.ShapeDtypeStruct((E, B, cap_c, D_MODEL), BF16),
                      jax.ShapeDtypeStruct((E, B, cap_c, 1), F32)]
    tile_map = lambda b, g, j, cnt_ref: (b, g, j)
    grid_spec = pltpu.PrefetchScalarGridSpec(
        num_scalar_prefetch=1,
        grid=(B, E // G, S // ROW_TILE),
        in_specs=[pl.BlockSpec((1, G, ROW_TILE), tile_map),
                  pl.BlockSpec((1, G, ROW_TILE), tile_map),
                  pl.BlockSpec((1, ROW_TILE, D_MODEL), lambda b, g, j, cnt_ref: (b, j, 0))],
        out_specs=out_specs,
    )
    return pl.pallas_call(
        kern,
        grid_spec=grid_spec,
        out_shape=out_shape,
        name="gather_rows",
        compiler_params=_cparams(("arbitrary", "arbitrary", "arbitrary")),
    )(cnt, posm, aff, h2)


def _expert_kernel(*refs, row_chunk, n_sets):
    x_refs = refs[:n_sets]
    g_refs = refs[n_sets:2 * n_sets]
    wg_ref, wu_ref, wd_ref = refs[2 * n_sets:2 * n_sets + 3]
    y_refs = refs[2 * n_sets + 3:3 * n_sets + 3]
    acc_refs = refs[3 * n_sets + 3:]
    f = pl.program_id(1)
    nf = pl.num_programs(1)
    wg = wg_ref[0, 0].astype(BF16)
    wu = wu_ref[0, 0].astype(BF16)
    wd = wd_ref[0, 0].astype(BF16)

    def hidden_tile(first):
        for x_ref, acc_ref in zip(x_refs, acc_refs):
            rows = x_ref.shape[1]
            step = min(row_chunk, rows)
            for r in range(rows // step):
                rs = slice(r * step, (r + 1) * step)
                x = x_ref[0, rs, :]
                g = _dot(x, wg)
                u = _dot(x, wu)
                y = _dot((g * jax.nn.sigmoid(g) * u).astype(BF16), wd)
                if first:
                    acc_ref[rs, :] = y
                else:
                    acc_ref[rs, :] += y

    pl.when(f == 0)(functools.partial(hidden_tile, True))
    pl.when(f > 0)(functools.partial(hidden_tile, False))

    @pl.when(f == nf - 1)
    def _():
        for g_ref, y_ref, acc_ref in zip(g_refs, y_refs, acc_refs):
            y_ref[0] = (acc_ref[...] * g_ref[0][:, 0:1]).astype(BF16)


def _expert_mlp(x_sets, g_sets, w_gate, w_up, w_down, layer, row_chunk):
    E = N_EXPERTS
    nf = D_EXPERT // FF_TILE
    n_sets = len(x_sets)
    kern = functools.partial(_expert_kernel, row_chunk=row_chunk, n_sets=n_sets)
    xspec = lambda r: pl.BlockSpec((1, r, D_MODEL), lambda e, f: (e, 0, 0))
    gspec = lambda gs: pl.BlockSpec((1,) + gs.shape[1:], lambda e, f: (e, 0, 0))
    return pl.pallas_call(
        kern,
        grid=(E, nf),
        in_specs=[xspec(xs.shape[1]) for xs in x_sets] + [gspec(gs) for gs in g_sets] + [
            pl.BlockSpec((1, 1, D_MODEL, FF_TILE), lambda e, f: (layer, e, 0, f)),
            pl.BlockSpec((1, 1, D_MODEL, FF_TILE), lambda e, f: (layer, e, 0, f)),
            pl.BlockSpec((1, 1, FF_TILE, D_MODEL), lambda e, f: (layer, e, f, 0))],
        out_specs=[xspec(xs.shape[1]) for xs in x_sets],
        out_shape=[jax.ShapeDtypeStruct(xs.shape, BF16) for xs in x_sets],
        scratch_shapes=[pltpu.VMEM(xs.shape[1:], F32) for xs in x_sets],
        name="expert_mlp",
        compiler_params=_cparams(("arbitrary", "arbitrary")),
    )(*x_sets, *g_sets, w_gate, w_up, w_down)


def _combine_kernel(cnt_ref, x_ref, pos_ref, *refs, cap_l, cap_c, final):
    if cap_c:
        yl_ref, il_ref, yc_ref, mod_ref, fg_ref, o_ref, acc_ref, pk_ref, tk_ref = refs
    else:
        yl_ref, il_ref, mod_ref, fg_ref, o_ref, acc_ref, pk_ref, tk_ref = refs
    b = pl.program_id(0)
    i = pl.program_id(1)
    row = jnp.where(i == 0, 4, b) if cap_c else b
    gate = mod_ref[pl.ds(row, 1), 5 * D_MODEL:6 * D_MODEL]
    pos = pos_ref[0]
    acc_ref[...] = jnp.zeros_like(acc_ref)
    GR = PACK_GRANULE
    W = min(GATHER_WINDOW, cap_l)

    def add_latent():
        t = i - 1 if cap_c else i
        starts = [jnp.minimum(_div_pow2(cnt_ref[b, e, t], GR) * GR, cap_l - W)
                  for e in range(N_EXPERTS)]
        for e in range(N_EXPERTS):
            a = pl.multiple_of(starts[e], GR)
            pk_ref[e * W:(e + 1) * W, :] = yl_ref[e, 0, pl.ds(a, W), :]
            tk_ref[e * W:(e + 1) * W, :] = il_ref[e, 0, pl.ds(a, W), 1:2].astype(I32)

        def product(tok_ids, rows):
            lane = lax.broadcasted_iota(I32, (tok_ids.shape[0], ROW_TILE), 1) + t * ROW_TILE
            onehot = jnp.where(tok_ids == lane, 1.0, 0.0)
            return _dot(onehot.T.astype(BF16), rows)

        acc_ref[...] += product(tk_ref[...], pk_ref[...])
        over = [cnt_ref[b, e, t + 1] - (starts[e] + W) for e in range(N_EXPERTS)]
        worst = over[0]
        for o in over[1:]:
            worst = jnp.maximum(worst, o)

        @pl.when(worst > 0)
        def _():
            wx = min(2 * W, cap_l)
            slot = lax.broadcasted_iota(I32, (wx, 1), 0)
            for e in range(N_EXPERTS):
                def extra_window(k, carry, e=e):
                    first_slot = starts[e] + W + k * wx
                    a = pl.multiple_of(jnp.minimum(first_slot, cap_l - wx), GR)
                    ids = jnp.where(slot + a >= first_slot,
                                    il_ref[e, 0, pl.ds(a, wx), 1:2].astype(I32), -1)
                    acc_ref[...] += product(ids, yl_ref[e, 0, pl.ds(a, wx), :])
                    return carry

                lax.fori_loop(0, _div_pow2(jnp.maximum(over[e], 0) + (wx - 1), wx),
                              extra_window, 0)

    def add_context():
        ncol = yc_ref.shape[1]
        lane = lax.broadcasted_iota(I32, (ROW_TILE, ncol), 1)
        for e in range(N_EXPERTS):
            pe = pos[:, e:e + 1]
            tgt = jnp.where(pe >= 0, pe + b * cap_c, -1)
            acc_ref[...] += _dot(jnp.where(tgt == lane, 1.0, 0.0).astype(BF16), yc_ref[e])

    if cap_c:
        pl.when(i == 0)(add_context)
        pl.when(i > 0)(add_latent)
    else:
        add_latent()
    x2 = x_ref[0] + gate * acc_ref[...]
    if final:
        var = jnp.mean(x2 * x2, axis=-1, keepdims=True)
        x2 = x2 * lax.rsqrt(var + EPS) * fg_ref[...]
    o_ref[0] = x2


def _combine(cnt, x1, pos_t, yl, il, yc, mod_l, final_g, cap_l, cap_c, final):
    B, S, _ = x1.shape
    E = N_EXPERTS
    rmap = lambda b, i, cnt_ref: (b, i, 0)
    const2 = lambda b, i, cnt_ref: (0, 0)
    kern = functools.partial(_combine_kernel, cap_l=cap_l, cap_c=cap_c, final=final)
    per_sample = lambda b, i, cnt_ref: (0, b, 0, 0)
    y_specs = [pl.BlockSpec((E, 1, cap_l, D_MODEL), per_sample, pipeline_mode=pl.Buffered(1)),
               pl.BlockSpec((E, 1, cap_l, 2), per_sample, pipeline_mode=pl.Buffered(1))]
    y_args = [yl, il]
    pack_rows = E * min(GATHER_WINDOW, cap_l)
    if cap_c:
        y_specs.append(pl.BlockSpec(yc.shape, lambda b, i, cnt_ref: (0, 0, 0),
                                    pipeline_mode=pl.Buffered(1)))
        y_args.append(yc)
    grid_spec = pltpu.PrefetchScalarGridSpec(
        num_scalar_prefetch=1,
        grid=(B, S // ROW_TILE),
        in_specs=[pl.BlockSpec((1, ROW_TILE, D_MODEL), rmap),
                  pl.BlockSpec((1, ROW_TILE, E), rmap)] + y_specs + [
                  pl.BlockSpec((8, N_MOD * D_MODEL), const2),
                  pl.BlockSpec((1, D_MODEL), const2)],
        out_specs=pl.BlockSpec((1, ROW_TILE, D_MODEL), rmap),
        scratch_shapes=[pltpu.VMEM((ROW_TILE, D_MODEL), F32),
                        pltpu.VMEM((pack_rows, D_MODEL), BF16),
                        pltpu.VMEM((pack_rows, 1), I32)],
    )
    return pl.pallas_call(
        kern,
        grid_spec=grid_spec,
        out_shape=jax.ShapeDtypeStruct((B, S, D_MODEL), F32),
        name="combine",
        compiler_params=_cparams(("arbitrary", "arbitrary")),
    )(cnt, x1, pos_t, *y_args, mod_l, final_g)


def _rope_tables(n_lat, ctx_len):
    t = np.arange(n_lat)
    pos = np.stack([t // GRID_W, t % GRID_W], axis=-1).astype(np.float32)
    inv = (ROPE_BASE ** (-np.arange(ROPE_FREQS, dtype=np.float32) / ROPE_FREQS)).astype(np.float32)
    ang = pos[:, :, None] * inv
    cos = np.cos(ang).astype(np.float32)
    sin = np.sin(ang).astype(np.float32)
    cos64 = np.concatenate([cos[:, 0], cos[:, 0], cos[:, 1], cos[:, 1]], axis=-1)
    sin64 = np.concatenate([-sin[:, 0], sin[:, 0], -sin[:, 1], sin[:, 1]], axis=-1)
    cos_t = np.concatenate([np.ones((ctx_len, 64), np.float32), cos64], axis=0)
    sin_t = np.concatenate([np.zeros((ctx_len, 64), np.float32), sin64], axis=0)
    return (jnp.asarray(np.tile(cos_t, (1, 2))), jnp.asarray(np.tile(sin_t, (1, 2))))


def kernel(x, c, ctx, c_ctx, ada_w, ada_b, norm1_g, norm2_g, w_in, attn_sink, hgrn_lb,
           hgrn_norm_g, conv_w, w_o, router_w, exp_w_gate, exp_w_up, exp_w_down, final_norm_g):
    B, T, D = x.shape
    L = ctx.shape[1]
    depth = ada_w.shape[0]
    assert D == D_MODEL and L == ROW_TILE and T % ROW_TILE == 0 and B <= 4
    S = L + T
    cap_l = EC_CAPACITY * T // N_EXPERTS
    cap_c = EC_CAPACITY * L // N_EXPERTS
    assert cap_l % 16 == 0 and cap_c % 16 == 0 and (B * cap_c) % 16 == 0

    cos_t, sin_t = _rope_tables(T, L)
    gamma = jax.nn.softmax(hgrn_lb.astype(F32), axis=0)
    lb_all = jnp.cumsum(gamma, axis=0) - gamma[0]
    cvec = jnp.concatenate([c, jnp.zeros((4 - B, D), F32), c_ctx[None],
                            jnp.zeros((3, D), F32)], axis=0)
    mod = _modulation(cvec, ada_w, ada_b)
    xs = (ctx, x)

    def interleave_heads(w, axis):
        shp = w.shape
        w = w.reshape(shp[:axis] + (N_KV, GQA_GROUP, HEAD_DIM) + shp[axis + 1:])
        return jnp.swapaxes(w, axis, axis + 1).reshape(shp)

    for l in range(depth):
        last = l == depth - 1
        w_in_l = w_in[l].astype(BF16)
        w_in_l = jnp.concatenate([w_in_l[:, :1024], interleave_heads(w_in_l[:, 1024:1536], 1),
                                  w_in_l[:, 1536:]], axis=1)
        w_o_l = w_o[l].astype(BF16)
        w_o_l = jnp.concatenate([interleave_heads(w_o_l[:ATT_W], 0), w_o_l[ATT_W:]], axis=0)
        q, kv, hg, hqg, cv = _in_projection(xs, mod[l], norm1_g[l][None], w_in_l, cos_t, sin_t)
        att = _attention(q, kv, attn_sink[l], L, skip_ctx=last)
        o2 = _hgrn(hg, hqg, lb_all[l], L, HG_LOCAL_FIRST if l == 0 else HG_LOCAL)
        x1, h2, aff = _mix_out(xs, att, o2, hqg, cv, mod[l], hgrn_norm_g[l][None], conv_w[l],
                               w_o_l, norm2_g[l][None], router_w[l].T, skip_ctx=last)
        lm, cc = (0, 0) if last else (L, cap_c)
        segments = ((lm, T, cap_l),) if last else ((0, L, cap_c), (L, T, cap_l))
        posm, cnt = _topk_positions(aff, segments)
        gathered = _gather_rows(cnt, posm, aff, h2, cap_l, cc)
        il = gathered[1]
        x_sets = [a.reshape(N_EXPERTS, -1, D) for a in gathered[0::2]]
        g_sets = [a.reshape(N_EXPERTS, -1, a.shape[-1]) for a in gathered[1::2]]
        ys = _expert_mlp(x_sets, g_sets, exp_w_gate, exp_w_up, exp_w_down, l, EXPERT_ROWS)
        yl = ys[0].reshape(N_EXPERTS, B, cap_l, D)
        xs = _combine(cnt, x1, jnp.swapaxes(posm, 1, 2), yl, il, None if last else ys[1], mod[l],
                      final_norm_g[None], cap_l, cc, final=last)
    return xs
```
